```python
import jax, jax.numpy as jnp
from jax import lax
import numpy as np

D_MODEL = 1024
BATCH = 8
SEQ = 4096
DEPTH = 1

PLE_DIM = 256
MIX_WIDTH = D_MODEL
HG_WIDTH = D_MODEL // 2
HG_HEADS = 4
HG_DK = HG_WIDTH // HG_HEADS
HG_DV = HG_WIDTH // HG_HEADS
HG_CHUNK = 64
SB_WIDTH = MIX_WIDTH - HG_WIDTH
SB_HEADS = 8
SB_DH = SB_WIDTH // SB_HEADS
SB_BLOCK = 128
D_FF = -(-8 * D_MODEL // (3 * 256)) * 256
IN_COLS = 4 * HG_WIDTH + 3 * SB_WIDTH
EPS = 1e-6

kernel_name = "hybrid_hgrn2_stickbreaking_block"


def rmsnorm(x, w):
    xf = x.astype(jnp.float32)
    xf = xf * lax.rsqrt(jnp.mean(xf * xf, axis=-1, keepdims=True) + EPS)
    return xf.astype(x.dtype) * w


def _to_chunks(a, heads, d):
    B, T, _ = a.shape
    return a.reshape(B, T // HG_CHUNK, HG_CHUNK, heads, d).transpose(1, 0, 3, 2, 4)


def hgrn2_mix(q, f_logit, i_in, lb):
    B, T, _ = q.shape
    q = jax.nn.silu(q.astype(jnp.float32))
    z = f_logit.astype(jnp.float32)
    log_f = jnp.logaddexp(jnp.log(lb), jnp.log1p(-lb) + jax.nn.log_sigmoid(z))
    k = -jnp.expm1(log_f)
    v = i_in.astype(jnp.float32)
    qc = _to_chunks(q, HG_HEADS, HG_DK)
    kc = _to_chunks(k, HG_HEADS, HG_DK)
    gc = _to_chunks(log_f, HG_HEADS, HG_DK)
    vc = _to_chunks(v, HG_HEADS, HG_DV)
    incl = jnp.tril(jnp.ones((HG_CHUNK, HG_CHUNK), dtype=bool))

    def step(S, inp):
        qb, kb, gb, vb = inp
        b = jnp.cumsum(gb, axis=2)
        o_inter = jnp.einsum('bhtk,bhkv->bhtv', qb * jnp.exp(b), S)
        diff = b[:, :, :, None, :] - b[:, :, None, :, :]
        decay = jnp.where(incl[:, :, None], jnp.exp(jnp.minimum(diff, 0.0)), 0.0)
        scores = jnp.einsum('bhtsk,bhsk->bhts', qb[:, :, :, None, :] * decay, kb)
        o_intra = jnp.einsum('bhts,bhsv->bhtv', scores, vb)
        b_last = b[:, :, -1:, :]
        S_new = jnp.exp(b_last[:, :, 0, :])[..., None] * S + jnp.einsum(
            'bhsk,bhsv->bhkv', kb * jnp.exp(b_last - b), vb)
        return S_new, o_inter + o_intra

    S0 = jnp.zeros((B, HG_HEADS, HG_DK, HG_DV), jnp.float32)
    _, oc = lax.scan(step, S0, (qc, kc, gc, vc))
    return oc.transpose(1, 0, 3, 2, 4).reshape(B, T, HG_HEADS * HG_DV)


def stick_breaking_mix(q, k, v):
    B, T, H, d = q.shape
    scale = d ** -0.5
    outs = []
    for blk in range(T // SB_BLOCK):
        t0 = blk * SB_BLOCK
        t1 = t0 + SB_BLOCK
        qb = q[:, t0:t1]
        kp = k[:, :t1]
        vp = v[:, :t1]
        z = jnp.einsum('bthd,bshd->bhts', qb, kp).astype(jnp.float32) * scale
        causal = jnp.arange(t1)[None, :] < jnp.arange(t0, t1)[:, None]
        log_1mb = jnp.where(causal, -jax.nn.softplus(z), 0.0)
        rem = lax.cumsum(log_1mb, axis=3, reverse=True) - log_1mb
        a = jnp.where(causal, jnp.exp(jax.nn.log_sigmoid(z) + rem), 0.0)
        outs.append(jnp.einsum('bhts,bshd->bthd', a.astype(v.dtype), vp))
    return jnp.concatenate(outs, axis=1)


def _fwd_setup_inputs(seed: int = 0) -> dict:
    key = jax.random.key(seed)
    ks = jax.random.split(key, 16)
    f32 = jnp.float32
    nrm = lambda k, shape, s: jax.random.normal(k, shape, f32) * s
    gain = lambda k, shape: 1.0 + 0.05 * jax.random.normal(k, shape, f32)
    return {
        "x": jax.random.normal(ks[0], (BATCH, SEQ, D_MODEL), f32),
        "p": jax.random.normal(ks[1], (DEPTH, BATCH, SEQ, PLE_DIM), f32),
        "attn_pre_norm": gain(ks[2], (DEPTH, D_MODEL)),
        "w_in": nrm(ks[3], (DEPTH, D_MODEL, IN_COLS), D_MODEL ** -0.5),
        "hg_lower_gamma": nrm(ks[4], (DEPTH + 1, HG_WIDTH), 0.5),
        "hg_out_norm": gain(ks[5], (DEPTH, HG_WIDTH)),
        "sb_out_norm": gain(ks[6], (DEPTH, SB_WIDTH)),
        "w_out": nrm(ks[7], (DEPTH, MIX_WIDTH, D_MODEL), MIX_WIDTH ** -0.5),
        "attn_post_norm": gain(ks[8], (DEPTH, D_MODEL)),
        "ffn_pre_norm": gain(ks[9], (DEPTH, D_MODEL)),
        "w_gate_up": nrm(ks[10], (DEPTH, D_MODEL, 2 * D_FF), D_MODEL ** -0.5),
        "w_down": nrm(ks[11], (DEPTH, D_FF, D_MODEL), D_FF ** -0.5),
        "ffn_post_norm": gain(ks[12], (DEPTH, D_MODEL)),
        "ple_proj": nrm(ks[13], (DEPTH, PLE_DIM, D_MODEL), PLE_DIM ** -0.5),
        "ple_gate": nrm(ks[14], (DEPTH, D_MODEL, D_MODEL), D_MODEL ** -0.5),
    }


def _fwd_reference(x, p, attn_pre_norm, w_in, hg_lower_gamma, hg_out_norm, sb_out_norm, w_out,
              attn_post_norm, ffn_pre_norm, w_gate_up, w_down, ffn_post_norm, ple_proj, ple_gate):
    B, T, _ = x.shape
    lb_all = jnp.cumsum(jax.nn.softmax(hg_lower_gamma.astype(jnp.float32), axis=0), axis=0)
    splits = [HG_WIDTH, 2 * HG_WIDTH, 3 * HG_WIDTH, 4 * HG_WIDTH,
              4 * HG_WIDTH + SB_WIDTH, 4 * HG_WIDTH + 2 * SB_WIDTH]
    h = x
    for i in range(DEPTH):
        u = rmsnorm(h, attn_pre_norm[i])
        proj = u @ w_in[i]
        hq, hf, hi, hg, sq, sk, sv = jnp.split(proj, splits, axis=-1)
        o_hg = hgrn2_mix(hq, hf, hi, lb_all[i])
        o_hg = rmsnorm(o_hg.astype(x.dtype), hg_out_norm[i]) * jax.nn.silu(hg)
        o_sb = stick_breaking_mix(sq.reshape(B, T, SB_HEADS, SB_DH),
                                  sk.reshape(B, T, SB_HEADS, SB_DH),
                                  sv.reshape(B, T, SB_HEADS, SB_DH)).reshape(B, T, SB_WIDTH)
        o_sb = rmsnorm(o_sb, sb_out_norm[i])
        mix = jnp.concatenate([o_hg, o_sb], axis=-1) @ w_out[i]
        h = h + rmsnorm(mix, attn_post_norm[i])
        u = rmsnorm(h, ffn_pre_norm[i])
        gate, up = jnp.split(u @ w_gate_up[i], [D_FF], axis=-1)
        y = (jax.nn.silu(gate) * up) @ w_down[i]
        h = h + rmsnorm(y, ffn_post_norm[i])
        h = h + (p[i] @ ple_proj[i]) * jax.nn.sigmoid(h @ ple_gate[i])
    return h


import jax as _jax
import jax.numpy as _jnp

TWIN_FORMAT = 'train_step'
FWD_PARAMS = ['x', 'p', 'attn_pre_norm', 'w_in', 'hg_lower_gamma', 'hg_out_norm', 'sb_out_norm', 'w_out', 'attn_post_norm', 'ffn_pre_norm', 'w_gate_up', 'w_down', 'ffn_post_norm', 'ple_proj', 'ple_gate']
TWIN_WEIGHTS = ['attn_pre_norm', 'w_in', 'hg_lower_gamma', 'hg_out_norm', 'sb_out_norm', 'w_out', 'attn_post_norm', 'ffn_pre_norm', 'w_gate_up', 'w_down', 'ffn_post_norm', 'ple_proj', 'ple_gate']
TWIN_DIFF_INPUT = 'x'
TWIN_INPUTS = ['x', 'p', 'attn_pre_norm', 'w_in', 'hg_lower_gamma', 'hg_out_norm', 'sb_out_norm', 'w_out', 'attn_post_norm', 'ffn_pre_norm', 'w_gate_up', 'w_down', 'ffn_post_norm', 'ple_proj', 'ple_gate', 'loss_target', 'm_attn_pre_norm', 'm_w_in', 'm_hg_lower_gamma', 'm_hg_out_norm', 'm_sb_out_norm', 'm_w_out', 'm_attn_post_norm', 'm_ffn_pre_norm', 'm_w_gate_up', 'm_w_down', 'm_ffn_post_norm', 'm_ple_proj', 'm_ple_gate', 'v_attn_pre_norm', 'v_w_in', 'v_hg_lower_gamma', 'v_hg_out_norm', 'v_sb_out_norm', 'v_w_out', 'v_attn_post_norm', 'v_ffn_pre_norm', 'v_w_gate_up', 'v_w_down', 'v_ffn_post_norm', 'v_ple_proj', 'v_ple_gate']
TWIN_OUTPUTS = ['loss', 'grad_x', 'grad_attn_pre_norm', 'grad_w_in', 'grad_hg_lower_gamma', 'grad_hg_out_norm', 'grad_sb_out_norm', 'grad_w_out', 'grad_attn_post_norm', 'grad_ffn_pre_norm', 'grad_w_gate_up', 'grad_w_down', 'grad_ffn_post_norm', 'grad_ple_proj', 'grad_ple_gate', 'delta_attn_pre_norm', 'delta_w_in', 'delta_hg_lower_gamma', 'delta_hg_out_norm', 'delta_sb_out_norm', 'delta_w_out', 'delta_attn_post_norm', 'delta_ffn_pre_norm', 'delta_w_gate_up', 'delta_w_down', 'delta_ffn_post_norm', 'delta_ple_proj', 'delta_ple_gate', 'new_m_attn_pre_norm', 'new_m_w_in', 'new_m_hg_lower_gamma', 'new_m_hg_out_norm', 'new_m_sb_out_norm', 'new_m_w_out', 'new_m_attn_post_norm', 'new_m_ffn_pre_norm', 'new_m_w_gate_up', 'new_m_w_down', 'new_m_ffn_post_norm', 'new_m_ple_proj', 'new_m_ple_gate', 'new_v_attn_pre_norm', 'new_v_w_in', 'new_v_hg_lower_gamma', 'new_v_hg_out_norm', 'new_v_sb_out_norm', 'new_v_w_out', 'new_v_attn_post_norm', 'new_v_ffn_pre_norm', 'new_v_w_gate_up', 'new_v_w_down', 'new_v_ffn_post_norm', 'new_v_ple_proj', 'new_v_ple_gate']
TWIN_LEAF_KINDS = {'loss': 'loss', 'grad_x': 'grad_x', 'grad_attn_pre_norm': 'grad_w', 'grad_w_in': 'grad_w', 'grad_hg_lower_gamma': 'grad_w', 'grad_hg_out_norm': 'grad_w', 'grad_sb_out_norm': 'grad_w', 'grad_w_out': 'grad_w', 'grad_attn_post_norm': 'grad_w', 'grad_ffn_pre_norm': 'grad_w', 'grad_w_gate_up': 'grad_w', 'grad_w_down': 'grad_w', 'grad_ffn_post_norm': 'grad_w', 'grad_ple_proj': 'grad_w', 'grad_ple_gate': 'grad_w', 'delta_attn_pre_norm': 'delta_w', 'delta_w_in': 'delta_w', 'delta_hg_lower_gamma': 'delta_w', 'delta_hg_out_norm': 'delta_w', 'delta_sb_out_norm': 'delta_w', 'delta_w_out': 'delta_w', 'delta_attn_post_norm': 'delta_w', 'delta_ffn_pre_norm': 'delta_w', 'delta_w_gate_up': 'delta_w', 'delta_w_down': 'delta_w', 'delta_ffn_post_norm': 'delta_w', 'delta_ple_proj': 'delta_w', 'delta_ple_gate': 'delta_w', 'new_m_attn_pre_norm': 'new_m', 'new_m_w_in': 'new_m', 'new_m_hg_lower_gamma': 'new_m', 'new_m_hg_out_norm': 'new_m', 'new_m_sb_out_norm': 'new_m', 'new_m_w_out': 'new_m', 'new_m_attn_post_norm': 'new_m', 'new_m_ffn_pre_norm': 'new_m', 'new_m_w_gate_up': 'new_m', 'new_m_w_down': 'new_m', 'new_m_ffn_post_norm': 'new_m', 'new_m_ple_proj': 'new_m', 'new_m_ple_gate': 'new_m', 'new_v_attn_pre_norm': 'new_v', 'new_v_w_in': 'new_v', 'new_v_hg_lower_gamma': 'new_v', 'new_v_hg_out_norm': 'new_v', 'new_v_sb_out_norm': 'new_v', 'new_v_w_out': 'new_v', 'new_v_attn_post_norm': 'new_v', 'new_v_ffn_pre_norm': 'new_v', 'new_v_w_gate_up': 'new_v', 'new_v_w_down': 'new_v', 'new_v_ffn_post_norm': 'new_v', 'new_v_ple_proj': 'new_v', 'new_v_ple_gate': 'new_v'}


def _forward(args):
    return _fwd_reference(*[args[k] for k in FWD_PARAMS])


def _output_shape():
    out = _jax.eval_shape(lambda: _forward(_fwd_setup_inputs(0)))
    return out.shape, out.dtype

N_MICROBATCH = 1
ADAM_LR = 0.001
ADAM_B1 = 0.9
ADAM_B2 = 0.999
ADAM_EPS = 1e-08
ADAM_WD = 0.01
ADAM_STEP = 10
PER_EXAMPLE_BATCH_AXIS = {'x': 0, 'p': 1, 'loss_target': 0}
SHARED_INPUTS = []
_WEIGHT_DTYPES = {'attn_pre_norm': _jnp.float32, 'w_in': _jnp.float32, 'hg_lower_gamma': _jnp.float32, 'hg_out_norm': _jnp.float32, 'sb_out_norm': _jnp.float32, 'w_out': _jnp.float32, 'attn_post_norm': _jnp.float32, 'ffn_pre_norm': _jnp.float32, 'w_gate_up': _jnp.float32, 'w_down': _jnp.float32, 'ffn_post_norm': _jnp.float32, 'ple_proj': _jnp.float32, 'ple_gate': _jnp.float32}
MOMENT_SCALE = {'attn_pre_norm': 6.586272e-01, 'w_in': 3.356836e-01, 'hg_lower_gamma': 3.738136e-02, 'hg_out_norm': 4.071240e-01, 'sb_out_norm': 6.716993e-01, 'w_out': 5.446829e-01, 'attn_post_norm': 3.239547e+01, 'ffn_pre_norm': 5.873510e-01, 'w_gate_up': 2.504813e-01, 'w_down': 4.794501e-01, 'ffn_post_norm': 3.251106e+01, 'ple_proj': 4.771396e-01, 'ple_gate': 1.353029e-01}


def _to_microbatches(a, axis):
    t = _jnp.moveaxis(a, axis, 0)
    t = t.reshape((N_MICROBATCH, t.shape[0] // N_MICROBATCH) + t.shape[1:])
    return _jnp.moveaxis(t, 1, axis + 1)


def setup_inputs(seed: int = 0) -> dict:
    inp = _fwd_setup_inputs(seed)
    key = _jax.random.fold_in(_jax.random.key(seed), 7919)
    shape, _ = _output_shape()
    out = dict(inp)
    out["loss_target"] = _jax.random.normal(_jax.random.fold_in(key, 0), shape, _jnp.float32)
    for i, name in enumerate(TWIN_WEIGHTS):
        w = inp[name].astype(_jnp.float32)
        if MOMENT_SCALE is None:
            s = _jnp.sqrt(_jnp.mean(_jnp.square(w)) + 1e-30)
        else:
            s = MOMENT_SCALE[name]
        km, kv = _jax.random.split(_jax.random.fold_in(key, i + 1))
        out[name] = w
        out["m_" + name] = s * _jax.random.normal(km, w.shape, _jnp.float32)
        out["v_" + name] = (s * s) * _jax.random.uniform(kv, w.shape, _jnp.float32, 0.5, 1.5)
    if N_MICROBATCH > 1:
        for name, axis in PER_EXAMPLE_BATCH_AXIS.items():
            out[name] = _to_microbatches(out[name], axis)
    return {'x': out['x'], 'p': out['p'], 'attn_pre_norm': out['attn_pre_norm'], 'w_in': out['w_in'], 'hg_lower_gamma': out['hg_lower_gamma'], 'hg_out_norm': out['hg_out_norm'], 'sb_out_norm': out['sb_out_norm'], 'w_out': out['w_out'], 'attn_post_norm': out['attn_post_norm'], 'ffn_pre_norm': out['ffn_pre_norm'], 'w_gate_up': out['w_gate_up'], 'w_down': out['w_down'], 'ffn_post_norm': out['ffn_post_norm'], 'ple_proj': out['ple_proj'], 'ple_gate': out['ple_gate'], 'loss_target': out['loss_target'], 'm_attn_pre_norm': out['m_attn_pre_norm'], 'm_w_in': out['m_w_in'], 'm_hg_lower_gamma': out['m_hg_lower_gamma'], 'm_hg_out_norm': out['m_hg_out_norm'], 'm_sb_out_norm': out['m_sb_out_norm'], 'm_w_out': out['m_w_out'], 'm_attn_post_norm': out['m_attn_post_norm'], 'm_ffn_pre_norm': out['m_ffn_pre_norm'], 'm_w_gate_up': out['m_w_gate_up'], 'm_w_down': out['m_w_down'], 'm_ffn_post_norm': out['m_ffn_post_norm'], 'm_ple_proj': out['m_ple_proj'], 'm_ple_gate': out['m_ple_gate'], 'v_attn_pre_norm': out['v_attn_pre_norm'], 'v_w_in': out['v_w_in'], 'v_hg_lower_gamma': out['v_hg_lower_gamma'], 'v_hg_out_norm': out['v_hg_out_norm'], 'v_sb_out_norm': out['v_sb_out_norm'], 'v_w_out': out['v_w_out'], 'v_attn_post_norm': out['v_attn_post_norm'], 'v_ffn_pre_norm': out['v_ffn_pre_norm'], 'v_w_gate_up': out['v_w_gate_up'], 'v_w_down': out['v_w_down'], 'v_ffn_post_norm': out['v_ffn_post_norm'], 'v_ple_proj': out['v_ple_proj'], 'v_ple_gate': out['v_ple_gate']}


def _loss(weights, diff, rest, loss_target):
    with _jax.named_scope("forward"):
        args = {**rest, TWIN_DIFF_INPUT: diff, **{k: w.astype(_WEIGHT_DTYPES[k]) for k, w in weights.items()}}
        y = _forward(args)
    with _jax.named_scope("loss_head"):
        err = _jnp.square(y.astype(_jnp.float32) - loss_target)
        return 0.5 * _jnp.sum(_jnp.mean(err, axis=-1)) if err.ndim else 0.5 * err


def _adamw(w, g, m, v):
    m = ADAM_B1 * m + (1.0 - ADAM_B1) * g
    v = ADAM_B2 * v + (1.0 - ADAM_B2) * _jnp.square(g)
    m_hat = m / (1.0 - ADAM_B1 ** ADAM_STEP)
    v_hat = v / (1.0 - ADAM_B2 ** ADAM_STEP)
    delta = -ADAM_LR * (m_hat / (_jnp.sqrt(v_hat) + ADAM_EPS) + ADAM_WD * w)
    return delta, m, v


def reference(x, p, attn_pre_norm, w_in, hg_lower_gamma, hg_out_norm, sb_out_norm, w_out, attn_post_norm, ffn_pre_norm, w_gate_up, w_down, ffn_post_norm, ple_proj, ple_gate, loss_target, m_attn_pre_norm, m_w_in, m_hg_lower_gamma, m_hg_out_norm, m_sb_out_norm, m_w_out, m_attn_post_norm, m_ffn_pre_norm, m_w_gate_up, m_w_down, m_ffn_post_norm, m_ple_proj, m_ple_gate, v_attn_pre_norm, v_w_in, v_hg_lower_gamma, v_hg_out_norm, v_sb_out_norm, v_w_out, v_attn_post_norm, v_ffn_pre_norm, v_w_gate_up, v_w_down, v_ffn_post_norm, v_ple_proj, v_ple_gate):
    given = dict(x=x, p=p, attn_pre_norm=attn_pre_norm, w_in=w_in, hg_lower_gamma=hg_lower_gamma, hg_out_norm=hg_out_norm, sb_out_norm=sb_out_norm, w_out=w_out, attn_post_norm=attn_post_norm, ffn_pre_norm=ffn_pre_norm, w_gate_up=w_gate_up, w_down=w_down, ffn_post_norm=ffn_post_norm, ple_proj=ple_proj, ple_gate=ple_gate, loss_target=loss_target, m_attn_pre_norm=m_attn_pre_norm, m_w_in=m_w_in, m_hg_lower_gamma=m_hg_lower_gamma, m_hg_out_norm=m_hg_out_norm, m_sb_out_norm=m_sb_out_norm, m_w_out=m_w_out, m_attn_post_norm=m_attn_post_norm, m_ffn_pre_norm=m_ffn_pre_norm, m_w_gate_up=m_w_gate_up, m_w_down=m_w_down, m_ffn_post_norm=m_ffn_post_norm, m_ple_proj=m_ple_proj, m_ple_gate=m_ple_gate, v_attn_pre_norm=v_attn_pre_norm, v_w_in=v_w_in, v_hg_lower_gamma=v_hg_lower_gamma, v_hg_out_norm=v_hg_out_norm, v_sb_out_norm=v_sb_out_norm, v_w_out=v_w_out, v_attn_post_norm=v_attn_post_norm, v_ffn_pre_norm=v_ffn_pre_norm, v_w_gate_up=v_w_gate_up, v_w_down=v_w_down, v_ffn_post_norm=v_ffn_post_norm, v_ple_proj=v_ple_proj, v_ple_gate=v_ple_gate)
    weights = {n: given[n] for n in TWIN_WEIGHTS}
    shared = {n: given[n] for n in SHARED_INPUTS}
    per_example = {n: given[n] for n in ['x', 'p']}
    grad_fn = _jax.value_and_grad(_loss, argnums=(0, 1))

    def one_microbatch(ex, loss_target):
        ex = dict(ex)
        diff = ex.pop(TWIN_DIFF_INPUT)
        return grad_fn(weights, diff, {**shared, **ex}, loss_target)

    if N_MICROBATCH == 1:
        loss, (grad_w, grad_x) = one_microbatch(per_example, given["loss_target"])
    else:
        def body(carry, xs):
            loss_sum, grad_sum = carry
            l_k, (gw_k, gx_k) = one_microbatch(xs[0], xs[1])
            with _jax.named_scope("update"):
                return (loss_sum + l_k, _jax.tree.map(_jnp.add, grad_sum, gw_k)), gx_k

        init = (_jnp.zeros((), _jnp.float32), _jax.tree.map(_jnp.zeros_like, weights))
        (loss, grad_w), grad_x = _jax.lax.scan(body, init, (per_example, given["loss_target"]))
    with _jax.named_scope("update"):
        delta_w, new_m, new_v = {}, {}, {}
        for n in TWIN_WEIGHTS:
            delta_w[n], new_m[n], new_v[n] = _adamw(weights[n], grad_w[n], given["m_" + n], given["v_" + n])
    return (loss, grad_x, *[grad_w[n] for n in TWIN_WEIGHTS], *[delta_w[n] for n in TWIN_WEIGHTS],
            *[new_m[n] for n in TWIN_WEIGHTS], *[new_v[n] for n in TWIN_WEIGHTS])
```

```python
import functools

import numpy as np
import jax
import jax.numpy as jnp
from jax import lax
from jax.experimental import pallas as pl
from jax.experimental.pallas import tpu as pltpu

F32 = jnp.float32
BF16 = jnp.bfloat16
MESH = pl.DeviceIdType.MESH

RMS_EPS = 1e-6
D_MODEL = 1024
HG_WIDTH = 512
HG_HEADS = 4
HG_DK = 128
HG_CHUNK = 64
HG_LEVELS = (32, 16, 8, 4, 2, 1)
SB_WIDTH = 512
SB_BLOCK = 128
SB_DH = 64
SB_SCALE = SB_DH ** -0.5
D_FF = 2816
N_CHIPS = 4
ROW_TILE = 256
V7X_VMEM_LIMIT = 56 * 1024 * 1024

ADAM_LR = 0.001
ADAM_B1 = 0.9
ADAM_B2 = 0.999
ADAM_EPS = 1e-08
ADAM_WD = 0.01
ADAM_STEP = 10

ROW_ATTN_PRE, ROW_GAMMA, ROW_MIX_NORMS, ROW_ATTN_POST, ROW_FFN_PRE, ROW_FFN_POST, ROW_LOSS = range(7)


def _params(sem=None, vmem=V7X_VMEM_LIMIT):
    return pltpu.CompilerParams(dimension_semantics=sem, vmem_limit_bytes=vmem)


def _dot(a, b):
    return jnp.dot(a.astype(BF16), b.astype(BF16), preferred_element_type=F32)


def _dot_nt(a, b):
    return lax.dot_general(a.astype(BF16), b.astype(BF16), (((1,), (1,)), ((), ())), preferred_element_type=F32)


def _dot_tn(a, b):
    return lax.dot_general(a.astype(BF16), b.astype(BF16), (((0,), (0,)), ((), ())), preferred_element_type=F32)


def _split(x):
    hi = x.astype(BF16)
    lo = (x - hi.astype(F32)).astype(BF16)
    return hi, lo


def _sum01_left(m01, x):
    hi, lo = _split(x)
    return jnp.dot(m01, hi, preferred_element_type=F32) + jnp.dot(m01, lo, preferred_element_type=F32)


def _sum01_right(x, m01):
    hi, lo = _split(x)
    return jnp.dot(hi, m01, preferred_element_type=F32) + jnp.dot(lo, m01, preferred_element_type=F32)


def _rms(x):
    r = lax.rsqrt(jnp.mean(x * x, axis=-1, keepdims=True) + RMS_EPS)
    return x * r, r


def _rms_bwd(dy, xhat, r, w):
    dxh = dy * w
    dx = r * (dxh - xhat * jnp.mean(dxh * xhat, axis=-1, keepdims=True))
    return dx, dy * xhat


def _sigmoid(x):
    return 1.0 / (1.0 + jnp.exp(-x))


def _neg_softplus(z):
    return -(jnp.maximum(z, 0.0) + jnp.log(1.0 + jnp.exp(-jnp.abs(z))))


def _colsum(x):
    return jnp.sum(x, axis=0, keepdims=True)


def _load_once(src_hbm, dst_vmem):
    @pl.when(pl.program_id(0) == 0)
    def _():
        pltpu.sync_copy(src_hbm, dst_vmem)


def _zero_first(ref):
    @pl.when(pl.program_id(0) == 0)
    def _():
        ref[...] = jnp.zeros(ref.shape, ref.dtype)


def _row_spec(width, col=0):
    return pl.BlockSpec((ROW_TILE, width), lambda i, col=col: (i, col))


def _full_spec(shape):
    return pl.BlockSpec(shape, lambda *_: (0,) * len(shape))


ANY_SPEC = pl.BlockSpec(memory_space=pl.ANY)
PACK_SPEC = _full_spec((8, D_MODEL))


def in_proj_fwd(x, g_pre, w_in):
    T = x.shape[0]
    pw = w_in.shape[2]

    def body(x_ref, g_ref, w_hbm, ph_ref, sqkv_ref, u_ref, w_vmem, proj_s):
        _load_once(w_hbm, w_vmem)
        xh, _ = _rms(x_ref[...])
        u = (xh * g_ref[...]).astype(BF16)
        u_ref[...] = u
        for q in range(N_CHIPS):
            proj_s[:, pw * q:pw * (q + 1)] = jnp.dot(u, w_vmem[q], preferred_element_type=F32)
        ph_ref[...] = proj_s[:, :4 * HG_WIDTH]
        sqkv_ref[:, :SB_WIDTH] = (proj_s[:, 4 * HG_WIDTH:4 * HG_WIDTH + SB_WIDTH] * SB_SCALE).astype(BF16)
        sqkv_ref[:, SB_WIDTH:] = proj_s[:, 4 * HG_WIDTH + SB_WIDTH:].astype(BF16)

    return pl.pallas_call(
        body, name="in_proj_fwd", grid=(T // ROW_TILE,),
        in_specs=[_row_spec(D_MODEL), _full_spec((1, D_MODEL)), ANY_SPEC],
        out_specs=[_row_spec(4 * HG_WIDTH), _row_spec(3 * SB_WIDTH), _row_spec(D_MODEL)],
        out_shape=[jax.ShapeDtypeStruct((T, 4 * HG_WIDTH), F32), jax.ShapeDtypeStruct((T, 3 * SB_WIDTH), BF16),
                   jax.ShapeDtypeStruct((T, D_MODEL), BF16)],
        scratch_shapes=[pltpu.VMEM(w_in.shape, BF16), pltpu.VMEM((ROW_TILE, N_CHIPS * pw), F32)],
        compiler_params=_params(("arbitrary",)),
    )(x, g_pre, w_in)


def _hg_sum_matrix():
    C = HG_CHUNK
    t = np.arange(C)[:, None]
    j = np.arange(C)[None, :]
    mats = [j <= t, j > t]
    for h in HG_LEVELS:
        start = (t // (2 * h)) * (2 * h)
        upper = (t & h) != 0
        mats.append(np.where(upper, (j >= start + h) & (j <= t), (j > t) & (j <= start + h - 1)))
    return np.concatenate(mats, 0).astype(np.float32)


def _hg_level_masks():
    C = HG_CHUNK
    t = lax.broadcasted_iota(jnp.int32, (C, C), 0)
    s = lax.broadcasted_iota(jnp.int32, (C, C), 1)
    x = t ^ s
    masks = [t == s]
    for h in HG_LEVELS:
        masks.append((x >= h) & (x < 2 * h) & (t > s))
    return masks


def _hg_gates(hq, hf, gamma):
    lb = 1.0 / (1.0 + jnp.exp(gamma[1:2, :] - gamma[0:1, :]))
    sq = _sigmoid(hq)
    q = hq * sq
    sig = _sigmoid(hf)
    nsig = _sigmoid(-hf)
    f = lb + (1.0 - lb) * sig
    k = (1.0 - lb) * nsig
    g = jnp.log(f)
    return q, k, g, dict(lb=lb, sq=sq, sig=sig, nsig=nsig, f=f)


def _hg_head_decays(A, h):
    C, K = HG_CHUNK, HG_DK
    sl = slice(K * h, K * (h + 1))
    blocks = [A[C * r:C * (r + 1), sl] for r in range(2 + len(HG_LEVELS))]
    return blocks[0], blocks[1], [None] + blocks[2:]


def _hg_scores(q, k, levels, masks):
    sc = jnp.where(masks[0], _dot_nt(q, k), 0.0)
    for a, m in zip(levels[1:], masks[1:]):
        sc = jnp.where(m, _dot_nt(q * a, k * a), sc)
    return sc


def hgrn2_fwd(proj_h, gamma):
    T = proj_h.shape[0]
    C, K, H = HG_CHUNK, HG_DK, HG_HEADS
    n_chunks = T // C
    msum = jnp.asarray(_hg_sum_matrix(), BF16)

    def body(hq_ref, hf_ref, hi_ref, gam_ref, msum_ref, o_ref, st_ref, st_s):
        _zero_first(st_s)
        q, k, g, _ = _hg_gates(hq_ref[...], hf_ref[...], gam_ref[...])
        v = hi_ref[...]
        A = jnp.exp(_sum01_left(msum_ref[...], g))
        masks = _hg_level_masks()
        for h in range(H):
            sl = slice(K * h, K * (h + 1))
            ab, ar, levels = _hg_head_decays(A, h)
            qh, kh, vh = q[:, sl], k[:, sl], v[:, sl]
            st = st_s[h]
            sc = _hg_scores(qh, kh, levels, masks)
            o_ref[:, sl] = _dot_nt(qh * ab, st) + _dot(sc, vh)
            st_new = st * ab[C - 1:C, :] + _dot_tn(vh, kh * ar)
            st_s[h] = st_new
            st_ref[0, h] = st_new

    blk = lambda col: pl.BlockSpec((C, HG_WIDTH), lambda c, col=col: (c, col))
    return pl.pallas_call(
        body, name="hgrn2_fwd", grid=(n_chunks,),
        in_specs=[blk(0), blk(1), blk(2), _full_spec((2, HG_WIDTH)), _full_spec(msum.shape)],
        out_specs=[blk(0), pl.BlockSpec((1, H, K, K), lambda c: (c, 0, 0, 0))],
        out_shape=[jax.ShapeDtypeStruct((T, HG_WIDTH), F32), jax.ShapeDtypeStruct((n_chunks, H, K, K), F32)],
        scratch_shapes=[pltpu.VMEM((H, K, K), F32)],
        compiler_params=_params(("arbitrary",)),
    )(proj_h, proj_h, proj_h, gamma, msum)


def hgrn2_bwd(proj_h, gamma, states, do):
    T = proj_h.shape[0]
    C, K, H = HG_CHUNK, HG_DK, HG_HEADS
    n_chunks = T // C
    n_sums = 2 + len(HG_LEVELS)
    msum = jnp.asarray(_hg_sum_matrix(), BF16)
    msum_t = jnp.asarray(_hg_sum_matrix().T, BF16)

    def body(hq_ref, hf_ref, hi_ref, do_ref, gam_ref, msum_ref, msum_t_ref, st_in_ref,
             dhq_ref, dhf_ref, dhi_ref, pack_ref, dst_s, dlb_s, dq_s, dk_s, de_s):
        step = pl.program_id(0)
        _zero_first(dst_s)
        _zero_first(dlb_s)
        _zero_first(pack_ref)
        hq = hq_ref[...]
        q, k, g, aux = _hg_gates(hq, hf_ref[...], gam_ref[...])
        v = hi_ref[...]
        do_all = do_ref[...]
        A = jnp.exp(_sum01_left(msum_ref[...], g))
        masks = _hg_level_masks()
        is_last_row = lax.broadcasted_iota(jnp.int32, (C, K), 0) == C - 1
        has_prev = (step < n_chunks - 1).astype(F32)
        for h in range(H):
            sl = slice(K * h, K * (h + 1))
            ab, ar, levels = _hg_head_decays(A, h)
            qh, kh, vh, doh = q[:, sl], k[:, sl], v[:, sl], do_all[:, sl]
            st_in = st_in_ref[0, h] * has_prev
            dst_out = dst_s[h]
            sc = _hg_scores(qh, kh, levels, masks)
            da = _dot_nt(doh, vh)
            t1 = ab * _dot(doh, st_in)
            t2 = ar * _dot(vh, dst_out)
            decayed = _colsum(st_in * dst_out) * ab[C - 1:C, :]
            de_s[0:C, sl] = qh * t1 + jnp.where(is_last_row, decayed, 0.0)
            de_s[C:2 * C, sl] = kh * t2
            dam = jnp.where(masks[0], da, 0.0)
            dq = t1 + _dot(dam, kh)
            dk = t2 + _dot_tn(dam, qh)
            for r, (a, m) in enumerate(zip(levels[1:], masks[1:])):
                dam = jnp.where(m, da, 0.0)
                t1 = a * _dot(dam, kh * a)
                t2 = a * _dot_tn(dam, qh * a)
                dq = dq + t1
                dk = dk + t2
                de_s[C * (r + 2):C * (r + 3), sl] = qh * t1 + kh * t2
            dhi_ref[:, sl] = _dot_tn(sc, doh) + _dot_nt(kh * ar, dst_out)
            dst_s[h] = dst_out * ab[C - 1:C, :] + _dot_tn(doh, qh * ab)
            dq_s[:, sl] = dq
            dk_s[:, sl] = dk
        dg = _sum01_left(msum_t_ref[...], de_s[...])
        dk = dk_s[...]
        sq, lb = aux["sq"], aux["lb"]
        dhq_ref[...] = dq_s[...] * (sq * (1.0 + hq * (1.0 - sq)))
        common = dg / aux["f"] - dk
        dhf_ref[...] = (1.0 - lb) * aux["sig"] * aux["nsig"] * common
        dlb_s[...] += _colsum(aux["nsig"] * common)

        @pl.when(step == n_chunks - 1)
        def _():
            dgam = lb * (1.0 - lb) * dlb_s[...]
            pack_ref[ROW_GAMMA:ROW_GAMMA + 1, :HG_WIDTH] = dgam
            pack_ref[ROW_GAMMA:ROW_GAMMA + 1, HG_WIDTH:] = -dgam

    last = n_chunks - 1
    blk = lambda col: pl.BlockSpec((C, HG_WIDTH), lambda c, col=col: (last - c, col))
    return pl.pallas_call(
        body, name="hgrn2_bwd", grid=(n_chunks,),
        in_specs=[blk(0), blk(1), blk(2), blk(0), _full_spec((2, HG_WIDTH)), _full_spec(msum.shape),
                  _full_spec(msum_t.shape),
                  pl.BlockSpec((1, H, K, K), lambda c: (jnp.maximum(last - c - 1, 0), 0, 0, 0))],
        out_specs=[blk(0), blk(0), blk(0), PACK_SPEC],
        out_shape=[jax.ShapeDtypeStruct((T, HG_WIDTH), F32)] * 3 + [jax.ShapeDtypeStruct((8, D_MODEL), F32)],
        scratch_shapes=[pltpu.VMEM((H, K, K), F32), pltpu.VMEM((1, HG_WIDTH), F32), pltpu.VMEM((C, HG_WIDTH), F32),
                        pltpu.VMEM((C, HG_WIDTH), F32), pltpu.VMEM((n_sums * C, HG_WIDTH), F32)],
        compiler_params=_params(("arbitrary",)),
    )(proj_h, proj_h, proj_h, do, gamma, msum, msum_t, states)


def _sb_sum_matrix(inclusive):
    B = SB_BLOCK
    j = np.arange(B)[:, None]
    s = np.arange(B)[None, :]
    tri = (j >= s) if inclusive else (j > s)
    return np.concatenate([tri, np.ones((B, B), bool)], 1).astype(np.float32)


def _sb_prefix_matrix(inclusive):
    B = SB_BLOCK
    j = np.arange(B)[:, None]
    s = np.arange(B)[None, :]
    tri = (j <= s) if inclusive else (j < s)
    return np.concatenate([tri, np.ones((B, B), bool)], 1).astype(np.float32)


def _sb_iotas():
    shape = (SB_BLOCK, SB_BLOCK)
    return lax.broadcasted_iota(jnp.int32, shape, 0), lax.broadcasted_iota(jnp.int32, shape, 1)


def _sb_weights(qh, kj, valid, carry, usum):
    z = _dot_nt(qh, kj)
    lnb = jnp.where(valid, _neg_softplus(z), 0.0)
    lb = z + lnb
    sums = _sum01_right(lnb, usum)
    a = jnp.where(valid, jnp.exp(lb + carry + sums[:, :SB_BLOCK]), 0.0)
    return a, lb, carry + sums[:, SB_BLOCK:]


def sb_fwd(sqkv):
    T = sqkv.shape[0]
    B = SB_BLOCK
    pairs = SB_WIDTH // B
    usum = jnp.asarray(_sb_sum_matrix(False), BF16)

    def body(q_ref, k_ref, v_ref, u_ref, o_ref, tl_ref):
        i = pl.program_id(1)
        row, lane = _sb_iotas()
        first = lane < SB_DH
        q = q_ref[...]
        heads = (jnp.where(first, q, jnp.zeros_like(q)), jnp.where(first, jnp.zeros_like(q), q))
        u = u_ref[...]

        def step(n, state):
            j = i - n
            off = pl.multiple_of(j * B, B)
            kj = k_ref[pl.ds(off, B), :]
            vj = v_ref[pl.ds(off, B), :]
            valid = (lane + j * B) < (row + i * B)
            out = []
            for qh, (carry, acc) in zip(heads, state):
                a, _, carry = _sb_weights(qh, kj, valid, carry, u)
                out.append((carry, acc + _dot(a, vj)))
            return tuple(out)

        zero = jnp.zeros((B, B), F32)
        (tot0, acc0), (tot1, acc1) = lax.fori_loop(0, i + 1, step, ((zero, zero), (zero, zero)))
        o_ref[...] = jnp.where(first, acc0, acc1)
        tl_ref[...] = jnp.where(first, tot0, tot1)

    return pl.pallas_call(
        body, name="sb_fwd", grid=(pairs, T // B),
        in_specs=[pl.BlockSpec((B, B), lambda p, i: (i, p)),
                  pl.BlockSpec((T, B), lambda p, i: (0, pairs + p)),
                  pl.BlockSpec((T, B), lambda p, i: (0, 2 * pairs + p)),
                  pl.BlockSpec(usum.shape, lambda p, i: (0, 0))],
        out_specs=[pl.BlockSpec((B, B), lambda p, i: (i, p))] * 2,
        out_shape=[jax.ShapeDtypeStruct((T, SB_WIDTH), F32)] * 2,
        compiler_params=_params(("arbitrary", "arbitrary")),
    )(sqkv, sqkv, sqkv, usum)


def sb_bwd(sqkv, do, tl):
    T = sqkv.shape[0]
    B = SB_BLOCK
    pairs = SB_WIDTH // B
    upre = jnp.asarray(_sb_prefix_matrix(True), BF16)
    uexc = jnp.asarray(_sb_prefix_matrix(False), BF16)

    def body(q_ref, k_ref, v_ref, do_ref, tl_ref, up_ref, ue_ref, dq_ref, dk_ref, dv_ref):
        i = pl.program_id(1)

        @pl.when(i == 0)
        def _():
            dk_ref[...] = jnp.zeros(dk_ref.shape, F32)
            dv_ref[...] = jnp.zeros(dv_ref.shape, F32)

        row, lane = _sb_iotas()
        first = lane < SB_DH
        q = q_ref[...]
        do = do_ref[...]
        tl_all = tl_ref[...]
        zb = jnp.zeros_like(q)
        zf = jnp.zeros_like(do)
        heads = (
            (jnp.where(first, q, zb), jnp.where(first, do, zf).astype(BF16), tl_all[:, 0:1]),
            (jnp.where(first, zb, q), jnp.where(first, zf, do).astype(BF16), tl_all[:, B - 1:B]),
        )
        up = up_ref[...]
        ue = ue_ref[...]

        def step(j, state):
            off = pl.multiple_of(j * B, B)
            kj = k_ref[pl.ds(off, B), :]
            vj = v_ref[pl.ds(off, B), :]
            valid = (lane + j * B) < (row + i * B)
            out = []
            dk_add = jnp.zeros((B, B), F32)
            dv_add = jnp.zeros((B, B), F32)
            for (qh, doh, total), (seen, seen_w, dq) in zip(heads, state):
                z = _dot_nt(qh, kj)
                lnb = jnp.where(valid, _neg_softplus(z), 0.0)
                lb = z + lnb
                sums = _sum01_right(lnb, up)
                a = jnp.where(valid, jnp.exp(lb + (total - seen - sums[:, :B])), 0.0)
                beta = jnp.exp(lb)
                w = a * _dot_nt(doh, vj)
                wsums = _sum01_right(w, ue)
                before = seen_w + wsums[:, :B]
                dz = jnp.where(valid, w * (1.0 - beta) - before * beta, 0.0)
                dk_add = dk_add + _dot_tn(dz, qh)
                dv_add = dv_add + _dot_tn(a, doh)
                out.append((seen + sums[:, B:], seen_w + wsums[:, B:], dq + _dot(dz, kj)))
            dk_ref[pl.ds(off, B), :] += dk_add
            dv_ref[pl.ds(off, B), :] += dv_add
            return tuple(out)

        zero = jnp.zeros((B, B), F32)
        (_, _, dq0), (_, _, dq1) = lax.fori_loop(0, i + 1, step, ((zero, zero, zero), (zero, zero, zero)))
        dq_ref[...] = jnp.where(first, dq0, dq1) * SB_SCALE

    qblk = pl.BlockSpec((B, B), lambda p, i: (i, p))
    full = pl.BlockSpec((T, B), lambda p, i: (0, p))
    return pl.pallas_call(
        body, name="sb_bwd", grid=(pairs, T // B),
        in_specs=[qblk, pl.BlockSpec((T, B), lambda p, i: (0, pairs + p)),
                  pl.BlockSpec((T, B), lambda p, i: (0, 2 * pairs + p)), qblk, qblk,
                  pl.BlockSpec(upre.shape, lambda p, i: (0, 0)), pl.BlockSpec(uexc.shape, lambda p, i: (0, 0))],
        out_specs=[qblk, full, full],
        out_shape=[jax.ShapeDtypeStruct((T, SB_WIDTH), F32)] * 3,
        compiler_params=_params(("arbitrary", "arbitrary")),
    )(sqkv, sqkv, sqkv, do, tl, upre, uexc)


def _mixer_out(o_hg, hg, o_sb, g_hg, g_sb):
    n_hg, r_hg = _rms(o_hg)
    s_hg = _sigmoid(hg)
    n_sb, r_sb = _rms(o_sb)
    return dict(n_hg=n_hg, r_hg=r_hg, s_hg=s_hg, n_sb=n_sb, r_sb=r_sb,
                y_hg=n_hg * g_hg * (hg * s_hg), y_sb=n_sb * g_sb)


def mix_out_fwd(o_hg, proj_h, o_sb, x, norms, g_post, w_out):
    T = x.shape[0]

    def body(ohg_ref, hg_ref, osb_ref, x_ref, nrm_ref, gp_ref, w_hbm, cat_ref, mix_ref, h1_ref, w_vmem):
        _load_once(w_hbm, w_vmem)
        nrm = nrm_ref[...]
        m = _mixer_out(ohg_ref[...], hg_ref[...], osb_ref[...], nrm[:, :HG_WIDTH], nrm[:, HG_WIDTH:])
        cat_ref[:, :HG_WIDTH] = m["y_hg"].astype(BF16)
        cat_ref[:, HG_WIDTH:] = m["y_sb"].astype(BF16)
        mix = jnp.dot(cat_ref[...], w_vmem[...], preferred_element_type=F32)
        mix_ref[...] = mix
        mh, _ = _rms(mix)
        h1_ref[...] = x_ref[...] + mh * gp_ref[...]

    return pl.pallas_call(
        body, name="mix_out_fwd", grid=(T // ROW_TILE,),
        in_specs=[_row_spec(HG_WIDTH), _row_spec(HG_WIDTH, 3), _row_spec(SB_WIDTH), _row_spec(D_MODEL),
                  _full_spec((1, D_MODEL)), _full_spec((1, D_MODEL)), ANY_SPEC],
        out_specs=[_row_spec(D_MODEL)] * 3,
        out_shape=[jax.ShapeDtypeStruct((T, D_MODEL), BF16), jax.ShapeDtypeStruct((T, D_MODEL), F32),
                   jax.ShapeDtypeStruct((T, D_MODEL), F32)],
        scratch_shapes=[pltpu.VMEM(w_out.shape, BF16)],
        compiler_params=_params(("arbitrary",)),
    )(o_hg, proj_h, o_sb, x, norms, g_post, w_out)


def ffn_fwd(h1, g_pre, g_post, w_gu, w_down):
    T = h1.shape[0]
    pw = w_gu.shape[2]

    def body(h1_ref, gpre_ref, gpost_ref, wgu_hbm, wd_hbm, u2_ref, gu_ref, act_ref, y_ref, h2_ref,
             wgu_vmem, wd_vmem, gu_s):
        _load_once(wgu_hbm, wgu_vmem)
        _load_once(wd_hbm, wd_vmem)
        h1v = h1_ref[...]
        hh, _ = _rms(h1v)
        u2 = (hh * gpre_ref[...]).astype(BF16)
        u2_ref[...] = u2
        for q in range(N_CHIPS):
            gu_s[:, pw * q:pw * (q + 1)] = jnp.dot(u2, wgu_vmem[q], preferred_element_type=F32)
        gu_ref[...] = gu_s[...].astype(BF16)
        gate = gu_s[:, :D_FF]
        act = (gate * _sigmoid(gate) * gu_s[:, D_FF:]).astype(BF16)
        act_ref[...] = act
        y = jnp.dot(act, wd_vmem[...], preferred_element_type=F32)
        y_ref[...] = y
        yh, _ = _rms(y)
        h2_ref[...] = h1v + yh * gpost_ref[...]

    return pl.pallas_call(
        body, name="ffn_fwd", grid=(T // ROW_TILE,),
        in_specs=[_row_spec(D_MODEL), _full_spec((1, D_MODEL)), _full_spec((1, D_MODEL)), ANY_SPEC, ANY_SPEC],
        out_specs=[_row_spec(D_MODEL), _row_spec(2 * D_FF), _row_spec(D_FF), _row_spec(D_MODEL), _row_spec(D_MODEL)],
        out_shape=[jax.ShapeDtypeStruct((T, D_MODEL), BF16), jax.ShapeDtypeStruct((T, 2 * D_FF), BF16),
                   jax.ShapeDtypeStruct((T, D_FF), BF16), jax.ShapeDtypeStruct((T, D_MODEL), F32),
                   jax.ShapeDtypeStruct((T, D_MODEL), F32)],
        scratch_shapes=[pltpu.VMEM(w_gu.shape, BF16), pltpu.VMEM(w_down.shape, BF16),
                        pltpu.VMEM((ROW_TILE, 2 * D_FF), F32)],
        compiler_params=_params(("arbitrary",)),
    )(h1, g_pre, g_post, w_gu, w_down)


def ple_loss(h2, p, target, w_ple, w_pg):
    T = h2.shape[0]
    pw = w_ple.shape[2]

    def body(h2_ref, p_ref, t_ref, wple_hbm, wpg_hbm, de_ref, ds_ref, dh2_ref, h2b_ref, pb_ref, pack_ref,
             wple_vmem, wpg_vmem, e_s):
        _load_once(wple_hbm, wple_vmem)
        _load_once(wpg_hbm, wpg_vmem)
        _zero_first(pack_ref)
        h2v = h2_ref[...]
        h2b = h2v.astype(BF16)
        h2b_ref[...] = h2b
        pb = p_ref[...].astype(BF16)
        pb_ref[...] = pb
        for q in range(N_CHIPS):
            e_s[:, pw * q:pw * (q + 1)] = jnp.dot(pb, wple_vmem[q], preferred_element_type=F32)
        e = e_s[...]
        sig = _sigmoid(jnp.dot(h2b, wpg_vmem[...], preferred_element_type=F32))
        err = h2v + e * sig - t_ref[...]
        part = 0.5 * jnp.sum(jnp.mean(err * err, axis=-1, keepdims=True), axis=0, keepdims=True)
        lane = lax.broadcasted_iota(jnp.int32, (1, D_MODEL), 1)
        pack_ref[ROW_LOSS:ROW_LOSS + 1, :] += jnp.where(lane == 0, part, 0.0)
        dh3 = err * (1.0 / D_MODEL)
        de_ref[...] = (dh3 * sig).astype(BF16)
        ds = (dh3 * e * sig * (1.0 - sig)).astype(BF16)
        ds_ref[...] = ds
        dh2_ref[...] = dh3 + _dot_nt(ds, wpg_vmem[...])

    return pl.pallas_call(
        body, name="ple_loss", grid=(T // ROW_TILE,),
        in_specs=[_row_spec(D_MODEL), _row_spec(p.shape[1]), _row_spec(D_MODEL), ANY_SPEC, ANY_SPEC],
        out_specs=[_row_spec(D_MODEL), _row_spec(D_MODEL), _row_spec(D_MODEL), _row_spec(D_MODEL),
                   _row_spec(p.shape[1]), PACK_SPEC],
        out_shape=[jax.ShapeDtypeStruct((T, D_MODEL), BF16), jax.ShapeDtypeStruct((T, D_MODEL), BF16),
                   jax.ShapeDtypeStruct((T, D_MODEL), F32), jax.ShapeDtypeStruct((T, D_MODEL), BF16),
                   jax.ShapeDtypeStruct(p.shape, BF16), jax.ShapeDtypeStruct((8, D_MODEL), F32)],
        scratch_shapes=[pltpu.VMEM(w_ple.shape, BF16), pltpu.VMEM(w_pg.shape, BF16), pltpu.VMEM((ROW_TILE, D_MODEL), F32)],
        compiler_params=_params(("arbitrary",)),
    )(h2, p, target, w_ple, w_pg)


def ffn_bwd(dh2, y, h1, gu, g_pre, g_post, w_gu, w_down):
    T = h1.shape[0]
    pw = w_gu.shape[2]

    def body(dh2_ref, y_ref, h1_ref, gu_ref, gpre_ref, gpost_ref, wgu_hbm, wd_hbm, dy_ref, dgu_ref, dh1_ref, pack_ref,
             wgu_vmem, wd_vmem):
        _load_once(wgu_hbm, wgu_vmem)
        _load_once(wd_hbm, wd_vmem)
        _zero_first(pack_ref)
        dh2v = dh2_ref[...]
        yh, ry = _rms(y_ref[...])
        dy, dw = _rms_bwd(dh2v, yh, ry, gpost_ref[...])
        pack_ref[ROW_FFN_POST:ROW_FFN_POST + 1, :] += _colsum(dw)
        dyb = dy.astype(BF16)
        dy_ref[...] = dyb
        dact = _dot_nt(dyb, wd_vmem[...])
        gate = gu_ref[:, :D_FF].astype(F32)
        up = gu_ref[:, D_FF:].astype(F32)
        sg = _sigmoid(gate)
        dgu_ref[:, :D_FF] = (dact * up * (sg * (1.0 + gate * (1.0 - sg)))).astype(BF16)
        dgu_ref[:, D_FF:] = (dact * gate * sg).astype(BF16)
        du2 = _dot_nt(dgu_ref[:, :pw], wgu_vmem[0])
        for q in range(1, N_CHIPS):
            du2 = du2 + _dot_nt(dgu_ref[:, pw * q:pw * (q + 1)], wgu_vmem[q])
        hh, rh = _rms(h1_ref[...])
        dh, dw = _rms_bwd(du2, hh, rh, gpre_ref[...])
        pack_ref[ROW_FFN_PRE:ROW_FFN_PRE + 1, :] += _colsum(dw)
        dh1_ref[...] = dh2v + dh

    return pl.pallas_call(
        body, name="ffn_bwd", grid=(T // ROW_TILE,),
        in_specs=[_row_spec(D_MODEL), _row_spec(D_MODEL), _row_spec(D_MODEL), _row_spec(2 * D_FF),
                  _full_spec((1, D_MODEL)), _full_spec((1, D_MODEL)), ANY_SPEC, ANY_SPEC],
        out_specs=[_row_spec(D_MODEL), _row_spec(2 * D_FF), _row_spec(D_MODEL), PACK_SPEC],
        out_shape=[jax.ShapeDtypeStruct((T, D_MODEL), BF16), jax.ShapeDtypeStruct((T, 2 * D_FF), BF16),
                   jax.ShapeDtypeStruct((T, D_MODEL), F32), jax.ShapeDtypeStruct((8, D_MODEL), F32)],
        scratch_shapes=[pltpu.VMEM(w_gu.shape, BF16), pltpu.VMEM(w_down.shape, BF16)],
        compiler_params=_params(("arbitrary",)),
    )(dh2, y, h1, gu, g_pre, g_post, w_gu, w_down)


def mix_out_bwd(dh1, mix, o_hg, proj_h, o_sb, norms, g_post, w_out):
    T = dh1.shape[0]

    def body(dh1_ref, mix_ref, ohg_ref, hg_ref, osb_ref, nrm_ref, gp_ref, w_hbm, dmix_ref, dohg_ref, dhg_ref, dosb_ref,
             pack_ref, w_vmem):
        _load_once(w_hbm, w_vmem)
        _zero_first(pack_ref)
        mh, rm = _rms(mix_ref[...])
        dmix, dw = _rms_bwd(dh1_ref[...], mh, rm, gp_ref[...])
        pack_ref[ROW_ATTN_POST:ROW_ATTN_POST + 1, :] += _colsum(dw)
        dmb = dmix.astype(BF16)
        dmix_ref[...] = dmb
        dcat = _dot_nt(dmb, w_vmem[...])
        nrm = nrm_ref[...]
        g_hg, g_sb = nrm[:, :HG_WIDTH], nrm[:, HG_WIDTH:]
        hg = hg_ref[...]
        m = _mixer_out(ohg_ref[...], hg, osb_ref[...], g_hg, g_sb)
        d_hg = dcat[:, :HG_WIDTH]
        silu = hg * m["s_hg"]
        dhg_ref[...] = d_hg * (m["n_hg"] * g_hg) * (m["s_hg"] * (1.0 + hg * (1.0 - m["s_hg"])))
        dx, dw = _rms_bwd(d_hg * silu, m["n_hg"], m["r_hg"], g_hg)
        dohg_ref[...] = dx
        pack_ref[ROW_MIX_NORMS:ROW_MIX_NORMS + 1, :HG_WIDTH] += _colsum(dw)
        dx, dw = _rms_bwd(dcat[:, HG_WIDTH:], m["n_sb"], m["r_sb"], g_sb)
        dosb_ref[...] = dx
        pack_ref[ROW_MIX_NORMS:ROW_MIX_NORMS + 1, HG_WIDTH:] += _colsum(dw)

    return pl.pallas_call(
        body, name="mix_out_bwd", grid=(T // ROW_TILE,),
        in_specs=[_row_spec(D_MODEL), _row_spec(D_MODEL), _row_spec(HG_WIDTH), _row_spec(HG_WIDTH, 3), _row_spec(SB_WIDTH),
                  _full_spec((1, D_MODEL)), _full_spec((1, D_MODEL)), ANY_SPEC],
        out_specs=[_row_spec(D_MODEL), _row_spec(HG_WIDTH), _row_spec(HG_WIDTH), _row_spec(SB_WIDTH), PACK_SPEC],
        out_shape=[jax.ShapeDtypeStruct((T, D_MODEL), BF16), jax.ShapeDtypeStruct((T, HG_WIDTH), F32),
                   jax.ShapeDtypeStruct((T, HG_WIDTH), F32), jax.ShapeDtypeStruct((T, SB_WIDTH), F32),
                   jax.ShapeDtypeStruct((8, D_MODEL), F32)],
        scratch_shapes=[pltpu.VMEM(w_out.shape, BF16)],
        compiler_params=_params(("arbitrary",)),
    )(dh1, mix, o_hg, proj_h, o_sb, norms, g_post, w_out)


def in_proj_bwd(parts, x, dh1, g_pre, w_in):
    T = x.shape[0]
    pw = w_in.shape[2]
    n_parts = len(parts)

    def body(*refs):
        part_refs = refs[:n_parts]
        x_ref, dh1_ref, g_ref, w_hbm, dproj_ref, dx_ref, pack_ref, w_vmem = refs[n_parts:]
        _load_once(w_hbm, w_vmem)
        _zero_first(pack_ref)
        for n, ref in enumerate(part_refs):
            dproj_ref[:, HG_WIDTH * n:HG_WIDTH * (n + 1)] = ref[...].astype(BF16)
        du = _dot_nt(dproj_ref[:, :pw], w_vmem[0])
        for q in range(1, N_CHIPS):
            du = du + _dot_nt(dproj_ref[:, pw * q:pw * (q + 1)], w_vmem[q])
        xh, r = _rms(x_ref[...])
        dx, dw = _rms_bwd(du, xh, r, g_ref[...])
        pack_ref[ROW_ATTN_PRE:ROW_ATTN_PRE + 1, :] += _colsum(dw)
        dx_ref[...] = dh1_ref[...] + dx

    return pl.pallas_call(
        body, name="in_proj_bwd", grid=(T // ROW_TILE,),
        in_specs=[_row_spec(HG_WIDTH)] * n_parts + [_row_spec(D_MODEL), _row_spec(D_MODEL), _full_spec((1, D_MODEL)), ANY_SPEC],
        out_specs=[_row_spec(n_parts * HG_WIDTH), _row_spec(D_MODEL), PACK_SPEC],
        out_shape=[jax.ShapeDtypeStruct((T, n_parts * HG_WIDTH), BF16), jax.ShapeDtypeStruct((T, D_MODEL), F32),
                   jax.ShapeDtypeStruct((8, D_MODEL), F32)],
        scratch_shapes=[pltpu.VMEM(w_in.shape, BF16)],
        compiler_params=_params(("arbitrary",)),
    )(*parts, x, dh1, g_pre, w_in)


def weight_grad(a, g, name, *, tm, tn, tk=512, col_pieces=False):
    T, M = a.shape
    N = g.shape[1]
    steps = T // tk

    def body(a_ref, g_ref, o_ref):
        @pl.when(pl.program_id(2) == 0)
        def _():
            o_ref[...] = jnp.zeros(o_ref.shape, F32)

        o_ref[...] += _dot_tn(a_ref[...], g_ref[...]).reshape(o_ref.shape)

    if col_pieces:
        out_shape = jax.ShapeDtypeStruct((N // tn, M, tn), F32)
        out_spec = pl.BlockSpec((1, tm, tn), lambda i, j, k: (j, i, 0))
    else:
        out_shape = jax.ShapeDtypeStruct((M, N), F32)
        out_spec = pl.BlockSpec((tm, tn), lambda i, j, k: (i, j))
    return pl.pallas_call(
        body, name=name, grid=(M // tm, N // tn, steps),
        in_specs=[pl.BlockSpec((tk, tm), lambda i, j, k: (k, i)), pl.BlockSpec((tk, tn), lambda i, j, k: (k, j))],
        out_specs=out_spec, out_shape=out_shape,
        compiler_params=_params(("arbitrary", "arbitrary", "arbitrary")),
    )(a, g)


def _place():
    x, y, c = lax.axis_index("x"), lax.axis_index("y"), lax.axis_index("c")
    chips = [(1 - x, y), (x, 1 - y), (1 - x, 1 - y)]
    return x, y, c, chips


def _chip_index(cx, cy):
    return 2 * cx + cy


def _rcopy(src, dst, send_sem, recv_sem, device):
    return pltpu.make_async_remote_copy(src_ref=src, dst_ref=dst, send_sem=send_sem, recv_sem=recv_sem,
                                        device_id=device, device_id_type=MESH)


def gather_weights(shards):
    n = len(shards)

    def body(*refs):
        ins, outs = refs[:n], refs[n:2 * n]
        send_sems, recv_sems, local_sems = refs[2 * n:]
        x, y, c, chips = _place()
        me = _chip_index(x, y)
        sibling = (x, y, 1 - c)

        def rows(w, core):
            half = ins[w].shape[0] // 2
            return pl.ds(core * half, half)

        local = [pltpu.make_async_copy(ins[w], outs[w].at[me], local_sems.at[w]) for w in range(n)]
        for cp in local:
            cp.start()
        sends = []
        for w in range(n):
            for j, chip in enumerate(chips):
                sends.append(_rcopy(ins[w].at[rows(w, c)], outs[w].at[me, rows(w, c)],
                                    send_sems.at[6 * w + j], recv_sems.at[6 * w + j], (*chip, c)))
        for cp in sends:
            cp.start()
        passed = []
        for w in range(n):
            for j, chip in enumerate(chips):
                block = outs[w].at[_chip_index(*chip), rows(w, c)]
                _rcopy(block, block, send_sems.at[6 * w + j], recv_sems.at[6 * w + j], (*chip, c)).wait_recv()
                cp = _rcopy(block, block, send_sems.at[6 * w + 3 + j], recv_sems.at[6 * w + 3 + j], sibling)
                cp.start()
                passed.append(cp)
        for w in range(n):
            for j, chip in enumerate(chips):
                block = outs[w].at[_chip_index(*chip), rows(w, 1 - c)]
                _rcopy(block, block, send_sems.at[6 * w + 3 + j], recv_sems.at[6 * w + 3 + j], sibling).wait_recv()
        for cp in sends + passed:
            cp.wait_send()
        for cp in local:
            cp.wait()

    return pl.pallas_call(
        body, name="gather_weights",
        in_specs=[ANY_SPEC] * n, out_specs=[ANY_SPEC] * n,
        out_shape=[jax.ShapeDtypeStruct((N_CHIPS,) + s.shape, s.dtype) for s in shards],
        scratch_shapes=[pltpu.SemaphoreType.DMA((6 * n,)), pltpu.SemaphoreType.DMA((6 * n,)), pltpu.SemaphoreType.DMA((n,))],
    )(*shards)


def exchange_core_halves(grads):
    n = len(grads)

    def body(*refs):
        ins, own, got = refs[:n], refs[n:2 * n], refs[2 * n:3 * n]
        send_sems, recv_sems, local_sems = refs[3 * n:]
        x, y, c, _ = _place()
        sibling = (x, y, 1 - c)
        copies = []
        for w in range(n):
            half = ins[w].shape[1] // 2
            copies.append(pltpu.make_async_copy(ins[w].at[:, pl.ds(c * half, half), :], own[w], local_sems.at[w]))
            copies.append(_rcopy(ins[w].at[:, pl.ds((1 - c) * half, half), :], got[w],
                                 send_sems.at[w], recv_sems.at[w], sibling))
        for cp in copies:
            cp.start()
        for cp in copies:
            cp.wait()

    half_shapes = [jax.ShapeDtypeStruct((g.shape[0], g.shape[1] // 2, g.shape[2]), g.dtype) for g in grads]
    res = pl.pallas_call(
        body, name="exchange_core_halves",
        in_specs=[ANY_SPEC] * n, out_specs=[ANY_SPEC] * (2 * n), out_shape=half_shapes + half_shapes,
        scratch_shapes=[pltpu.SemaphoreType.DMA((n,)), pltpu.SemaphoreType.DMA((n,)), pltpu.SemaphoreType.DMA((n,))],
    )(*grads)
    return res[:n], res[n:]


def exchange_chip_partials(partials):
    n = len(partials)

    def body(*refs):
        ins, outs = refs[:n], refs[n:2 * n]
        send_sems, recv_sems, local_sems = refs[2 * n:]
        x, y, c, chips = _place()
        me = _chip_index(x, y)
        copies = []
        for w in range(n):
            copies.append(pltpu.make_async_copy(ins[w].at[me], outs[w].at[me], local_sems.at[w]))
        sends = []
        for w in range(n):
            for j, chip in enumerate(chips):
                sends.append(_rcopy(ins[w].at[_chip_index(*chip)], outs[w].at[me],
                                    send_sems.at[3 * w + j], recv_sems.at[3 * w + j], (*chip, c)))
        for cp in copies + sends:
            cp.start()
        for w in range(n):
            for j, chip in enumerate(chips):
                block = outs[w].at[_chip_index(*chip)]
                _rcopy(block, block, send_sems.at[3 * w + j], recv_sems.at[3 * w + j], (*chip, c)).wait_recv()
        for cp in sends:
            cp.wait_send()
        for cp in copies:
            cp.wait()

    return pl.pallas_call(
        body, name="exchange_chip_partials",
        in_specs=[ANY_SPEC] * n, out_specs=[ANY_SPEC] * n,
        out_shape=[jax.ShapeDtypeStruct(p.shape, p.dtype) for p in partials],
        scratch_shapes=[pltpu.SemaphoreType.DMA((3 * n,)), pltpu.SemaphoreType.DMA((3 * n,)), pltpu.SemaphoreType.DMA((n,))],
    )(*partials)


def join_core_halves(halves):
    n = len(halves)

    def body(*refs):
        ins, outs = refs[:n], refs[n:2 * n]
        send_sems, recv_sems, local_sems = refs[2 * n:]
        x, y, c, _ = _place()
        sibling = (x, y, 1 - c)
        copies = []
        for w in range(n):
            half = ins[w].shape[0]
            mine = outs[w].at[pl.ds(c * half, half), :]
            copies.append(pltpu.make_async_copy(ins[w], mine, local_sems.at[w]))
            copies.append(_rcopy(ins[w], mine, send_sems.at[w], recv_sems.at[w], sibling))
        for cp in copies:
            cp.start()
        for w in range(n):
            half = ins[w].shape[0]
            theirs = outs[w].at[pl.ds((1 - c) * half, half), :]
            _rcopy(ins[w], theirs, send_sems.at[w], recv_sems.at[w], sibling).wait_recv()
        for w in range(n):
            copies[2 * w].wait()
            copies[2 * w + 1].wait_send()

    return pl.pallas_call(
        body, name="join_core_halves",
        in_specs=[ANY_SPEC] * n, out_specs=[ANY_SPEC] * n,
        out_shape=[jax.ShapeDtypeStruct((2 * h.shape[0], h.shape[1]), h.dtype) for h in halves],
        scratch_shapes=[pltpu.SemaphoreType.DMA((n,)), pltpu.SemaphoreType.DMA((n,)), pltpu.SemaphoreType.DMA((n,))],
    )(*halves)


def _elementwise_rows(rows, cap=512):
    for t in range(min(rows, cap), 0, -8):
        if rows % t == 0 and t % 16 == 0:
            return t
    return rows


def add_core_halves(own, got, name):
    _, rows, cols = own.shape
    tr = _elementwise_rows(rows)

    def body(a_ref, b_ref, o_ref):
        o_ref[...] = (a_ref[...] + b_ref[...]).astype(BF16)

    spec = pl.BlockSpec((1, tr, cols), lambda q, i: (q, i, 0))
    return pl.pallas_call(
        body, name=name, grid=(N_CHIPS, rows // tr), in_specs=[spec, spec], out_specs=spec,
        out_shape=jax.ShapeDtypeStruct(own.shape, BF16),
        compiler_params=_params(("arbitrary", "arbitrary")),
    )(own, got)


def add_chip_partials(parts, name):
    _, rows, cols = parts.shape
    tr = _elementwise_rows(rows)

    def body(p_ref, o_ref):
        acc = p_ref[0].astype(F32)
        for q in range(1, N_CHIPS):
            acc = acc + p_ref[q].astype(F32)
        o_ref[...] = acc

    return pl.pallas_call(
        body, name=name, grid=(rows // tr,),
        in_specs=[pl.BlockSpec((N_CHIPS, tr, cols), lambda i: (0, i, 0))],
        out_specs=pl.BlockSpec((tr, cols), lambda i: (i, 0)),
        out_shape=jax.ShapeDtypeStruct((rows, cols), F32),
        compiler_params=_params(("arbitrary",)),
    )(parts)


def _adamw_math(w, g, m, v):
    m = ADAM_B1 * m + (1.0 - ADAM_B1) * g
    v = ADAM_B2 * v + (1.0 - ADAM_B2) * (g * g)
    m_hat = m / (1.0 - ADAM_B1 ** ADAM_STEP)
    v_hat = v / (1.0 - ADAM_B2 ** ADAM_STEP)
    delta = -ADAM_LR * (m_hat / (jnp.sqrt(v_hat) + ADAM_EPS) + ADAM_WD * w)
    return delta, m, v


def adamw(w, g, m, v, name):
    rows, cols = w.shape
    tr = _elementwise_rows(rows, 256)

    def body(w_ref, g_ref, m_ref, v_ref, d_ref, nm_ref, nv_ref):
        d, nm, nv = _adamw_math(w_ref[...], g_ref[...], m_ref[...], v_ref[...])
        d_ref[...] = d
        nm_ref[...] = nm
        nv_ref[...] = nv

    spec = pl.BlockSpec((tr, cols), lambda i: (i, 0))
    return pl.pallas_call(
        body, name=name, grid=(rows // tr,), in_specs=[spec] * 4, out_specs=[spec] * 3,
        out_shape=[jax.ShapeDtypeStruct((rows, cols), F32)] * 3,
        compiler_params=_params(("arbitrary",)),
    )(w, g, m, v)


def reduce_small(packs, w, m, v):
    n = len(packs)
    n_dev = 8
    flips = [(fx, fy, fc) for fx in (0, 1) for fy in (0, 1) for fc in (0, 1)][1:]

    def body(*refs):
        pack_refs = refs[:n]
        w_ref, m_ref, v_ref, g_out, d_out, m_out, v_out, mine, slots, send_sems, recv_sems = refs[n:]
        x, y, c, _ = _place()
        me = 4 * x + 2 * y + c
        acc = pack_refs[0][...]
        for ref in pack_refs[1:]:
            acc = acc + ref[...]
        mine[...] = acc
        sends = []
        for k, (fx, fy, fc) in enumerate(flips):
            peer = (x ^ fx, y ^ fy, c ^ fc)
            sends.append(_rcopy(mine, slots.at[me], send_sems.at[k], recv_sems.at[me], peer))
        for cp in sends:
            cp.start()
        slots[me] = acc
        for fx, fy, fc in flips:
            src = 4 * (x ^ fx) + 2 * (y ^ fy) + (c ^ fc)
            _rcopy(mine, slots.at[src], send_sems.at[0], recv_sems.at[src], (x, y, c)).wait_recv()
        for cp in sends:
            cp.wait_send()
        total = slots[0]
        for d in range(1, n_dev):
            total = total + slots[d]
        g_out[...] = total
        d, nm, nv = _adamw_math(w_ref[...], total, m_ref[...], v_ref[...])
        d_out[...] = d
        m_out[...] = nm
        v_out[...] = nv

    vm = pl.BlockSpec(memory_space=pltpu.VMEM)
    return pl.pallas_call(
        body, name="reduce_small",
        in_specs=[vm] * (n + 3), out_specs=[vm] * 4,
        out_shape=[jax.ShapeDtypeStruct((8, D_MODEL), F32)] * 4,
        scratch_shapes=[pltpu.VMEM((8, D_MODEL), F32), pltpu.VMEM((n_dev, 8, D_MODEL), F32),
                        pltpu.SemaphoreType.DMA((len(flips),)), pltpu.SemaphoreType.DMA((n_dev,))],
    )(*packs, w, m, v)


def _pack_small(attn_pre, gamma, hg_norm, sb_norm, attn_post, ffn_pre, ffn_post):
    rows = [attn_pre, gamma.reshape(1, D_MODEL), jnp.concatenate([hg_norm, sb_norm], axis=1), attn_post, ffn_pre, ffn_post,
            jnp.zeros((2, D_MODEL), F32)]
    return jnp.concatenate(rows, axis=0)


def _unpack_small(pack):
    return (pack[ROW_ATTN_PRE:ROW_ATTN_PRE + 1], pack[ROW_GAMMA].reshape(2, HG_WIDTH),
            pack[ROW_MIX_NORMS:ROW_MIX_NORMS + 1, :HG_WIDTH], pack[ROW_MIX_NORMS:ROW_MIX_NORMS + 1, HG_WIDTH:],
            pack[ROW_ATTN_POST:ROW_ATTN_POST + 1], pack[ROW_FFN_PRE:ROW_FFN_PRE + 1], pack[ROW_FFN_POST:ROW_FFN_POST + 1])


def kernel(x, p, attn_pre_norm, w_in, hg_lower_gamma, hg_out_norm, sb_out_norm, w_out, attn_post_norm, ffn_pre_norm, w_gate_up, w_down, ffn_post_norm, ple_proj, ple_gate, loss_target, m_attn_pre_norm, m_w_in, m_hg_lower_gamma, m_hg_out_norm, m_sb_out_norm, m_w_out, m_attn_post_norm, m_ffn_pre_norm, m_w_gate_up, m_w_down, m_ffn_post_norm, m_ple_proj, m_ple_gate, v_attn_pre_norm, v_w_in, v_hg_lower_gamma, v_hg_out_norm, v_sb_out_norm, v_w_out, v_attn_post_norm, v_ffn_pre_norm, v_w_gate_up, v_w_down, v_ffn_post_norm, v_ple_proj, v_ple_gate):
    x2 = x[0]
    p2 = p[0, 0]
    target = loss_target[0]
    big = dict(w_in=(w_in, m_w_in, v_w_in), w_out=(w_out, m_w_out, v_w_out), w_gate_up=(w_gate_up, m_w_gate_up, v_w_gate_up),
               w_down=(w_down, m_w_down, v_w_down), ple_proj=(ple_proj, m_ple_proj, v_ple_proj),
               ple_gate=(ple_gate, m_ple_gate, v_ple_gate))
    names = list(big)
    big = {k: tuple(a[0] for a in t) for k, t in big.items()}

    full = dict(zip(names, gather_weights([big[k][0].astype(BF16) for k in names])))
    w_out_full = full["w_out"].reshape(D_MODEL, D_MODEL)
    w_down_full = full["w_down"].reshape(D_FF, D_MODEL)
    w_pg_full = full["ple_gate"].reshape(D_MODEL, D_MODEL)
    mix_norms = jnp.concatenate([hg_out_norm, sb_out_norm], axis=1)

    proj_h, sqkv, u1 = in_proj_fwd(x2, attn_pre_norm, full["w_in"])
    o_hg, states = hgrn2_fwd(proj_h, hg_lower_gamma)
    o_sb, sb_totals = sb_fwd(sqkv)
    cat, mix, h1 = mix_out_fwd(o_hg, proj_h, o_sb, x2, mix_norms, attn_post_norm, w_out_full)
    u2, gu, act, y, h2 = ffn_fwd(h1, ffn_pre_norm, ffn_post_norm, full["w_gate_up"], w_down_full)

    de, ds, dh2, h2b, pb, pack_loss = ple_loss(h2, p2, target, full["ple_proj"], w_pg_full)
    dy, dgu, dh1, pack_ffn = ffn_bwd(dh2, y, h1, gu, ffn_pre_norm, ffn_post_norm, full["w_gate_up"], w_down_full)
    dmix, do_hg, dhg, do_sb, pack_mix = mix_out_bwd(dh1, mix, o_hg, proj_h, o_sb, mix_norms, attn_post_norm, w_out_full)
    dsq, dsk, dsv = sb_bwd(sqkv, do_sb, sb_totals)
    dhq, dhf, dhi, pack_hg = hgrn2_bwd(proj_h, hg_lower_gamma, states, do_hg)
    dproj, grad_x, pack_in = in_proj_bwd([dhq, dhf, dhi, dhg, dsq, dsk, dsv], x2, dh1, attn_pre_norm, full["w_in"])

    local = dict(
        w_in=weight_grad(u1, dproj, "grad_w_in", tm=D_MODEL, tn=full["w_in"].shape[2], col_pieces=True),
        w_out=weight_grad(cat, dmix, "grad_w_out", tm=256, tn=D_MODEL).reshape(full["w_out"].shape),
        w_gate_up=weight_grad(u2, dgu, "grad_w_gate_up", tm=D_MODEL, tn=full["w_gate_up"].shape[2], col_pieces=True),
        w_down=weight_grad(act, dy, "grad_w_down", tm=D_FF // 2, tn=D_MODEL).reshape(full["w_down"].shape),
        ple_proj=weight_grad(pb, de, "grad_ple_proj", tm=pb.shape[1], tn=full["ple_proj"].shape[2], col_pieces=True),
        ple_gate=weight_grad(h2b, ds, "grad_ple_gate", tm=256, tn=D_MODEL).reshape(full["ple_gate"].shape),
    )

    own, got = exchange_core_halves([local[k] for k in names])
    partial = [add_core_halves(a, b, "add_core_halves_" + k) for k, a, b in zip(names, own, got)]
    by_source = exchange_chip_partials(partial)
    halves = [add_chip_partials(s, "add_chip_partials_" + k) for k, s in zip(names, by_source)]
    grads = dict(zip(names, join_core_halves(halves)))

    upd = {k: adamw(big[k][0], grads[k], big[k][1], big[k][2], "adamw_" + k) for k in names}

    small = reduce_small(
        [pack_loss, pack_ffn, pack_mix, pack_hg, pack_in],
        _pack_small(attn_pre_norm, hg_lower_gamma, hg_out_norm, sb_out_norm, attn_post_norm, ffn_pre_norm, ffn_post_norm),
        _pack_small(m_attn_pre_norm, m_hg_lower_gamma, m_hg_out_norm, m_sb_out_norm, m_attn_post_norm, m_ffn_pre_norm, m_ffn_post_norm),
        _pack_small(v_attn_pre_norm, v_hg_lower_gamma, v_hg_out_norm, v_sb_out_norm, v_attn_post_norm, v_ffn_pre_norm, v_ffn_post_norm),
    )
    loss = small[0][ROW_LOSS, 0]
    s_grad, s_delta, s_m, s_v = (_unpack_small(t) for t in small)

    def ordered(small_vals, big_vals):
        a_pre, gam, hg_n, sb_n, a_post, f_pre, f_post = small_vals
        b = {k: big_vals[k][None] for k in names}
        return (a_pre, b["w_in"], gam, hg_n, sb_n, b["w_out"], a_post, f_pre, b["w_gate_up"], b["w_down"], f_post,
                b["ple_proj"], b["ple_gate"])

    return (loss, grad_x[None],
            *ordered(s_grad, grads),
            *ordered(s_delta, {k: upd[k][0] for k in names}),
            *ordered(s_m, {k: upd[k][1] for k in names}),
            *ordered(s_v, {k: upd[k][2] for k in names}))
```

```python
import functools

import numpy as np
import jax
import jax.numpy as jnp
from jax import lax
from jax.experimental import pallas as pl
from jax.experimental.pallas import tpu as pltpu

F32 = jnp.float32
BF16 = jnp.bfloat16
MESH = pl.DeviceIdType.MESH

RMS_EPS = 1e-6
D_MODEL = 1024
HG_WIDTH = 512
HG_HEADS = 4
HG_DK = 128
HG_CHUNK = 64
HG_LEVELS = (32, 16, 8, 4, 2, 1)
SB_WIDTH = 512
SB_BLOCK = 128
SB_DH = 64
SB_SCALE = SB_DH ** -0.5
SB_UNDERFLOW_LOG = -90.0
D_FF = 2816
N_CHIPS = 4
ROW_TILE = 256
V7X_VMEM_LIMIT = 56 * 1024 * 1024

ADAM_LR = 0.001
ADAM_B1 = 0.9
ADAM_B2 = 0.999
ADAM_EPS = 1e-08
ADAM_WD = 0.01
ADAM_STEP = 10

ROW_ATTN_PRE, ROW_GAMMA, ROW_MIX_NORMS, ROW_ATTN_POST, ROW_FFN_PRE, ROW_FFN_POST, ROW_LOSS = range(7)


def _params(sem=None, vmem=V7X_VMEM_LIMIT):
    return pltpu.CompilerParams(dimension_semantics=sem, vmem_limit_bytes=vmem)


def _dot(a, b):
    return jnp.dot(a.astype(BF16), b.astype(BF16), preferred_element_type=F32)


def _dot_nt(a, b):
    return lax.dot_general(a.astype(BF16), b.astype(BF16), (((1,), (1,)), ((), ())), preferred_element_type=F32)


def _dot_tn(a, b):
    return lax.dot_general(a.astype(BF16), b.astype(BF16), (((0,), (0,)), ((), ())), preferred_element_type=F32)


def _split(x):
    hi = x.astype(BF16)
    lo = (x - hi.astype(F32)).astype(BF16)
    return hi, lo


def _sum01_left(m01, x):
    hi, lo = _split(x)
    return jnp.dot(m01, hi, preferred_element_type=F32) + jnp.dot(m01, lo, preferred_element_type=F32)


def _sum01_right(x, m01):
    hi, lo = _split(x)
    return jnp.dot(hi, m01, preferred_element_type=F32) + jnp.dot(lo, m01, preferred_element_type=F32)


def _rms(x):
    r = lax.rsqrt(jnp.mean(x * x, axis=-1, keepdims=True) + RMS_EPS)
    return x * r, r


def _rms_bwd(dy, xhat, r, w):
    dxh = dy * w
    dx = r * (dxh - xhat * jnp.mean(dxh * xhat, axis=-1, keepdims=True))
    return dx, dy * xhat


def _sigmoid(x):
    return 1.0 / (1.0 + jnp.exp(-x))


def _neg_softplus(z):
    return -(jnp.maximum(z, 0.0) + jnp.log(1.0 + jnp.exp(-jnp.abs(z))))


def _colsum(x):
    return jnp.sum(x, axis=0, keepdims=True)


def _load_once(src_hbm, dst_vmem):
    @pl.when(pl.program_id(0) == 0)
    def _():
        pltpu.sync_copy(src_hbm, dst_vmem)


def _zero_first(ref):
    @pl.when(pl.program_id(0) == 0)
    def _():
        ref[...] = jnp.zeros(ref.shape, ref.dtype)


def _row_spec(width, col=0):
    return pl.BlockSpec((ROW_TILE, width), lambda i, col=col: (i, col))


def _full_spec(shape):
    return pl.BlockSpec(shape, lambda *_: (0,) * len(shape))


ANY_SPEC = pl.BlockSpec(memory_space=pl.ANY)
PACK_SPEC = _full_spec((8, D_MODEL))


def in_proj_fwd(x, g_pre, w_in):
    T = x.shape[0]
    pw = w_in.shape[2]

    def body(x_ref, g_ref, w_hbm, ph_ref, sqkv_ref, u_ref, w_vmem, proj_s):
        _load_once(w_hbm, w_vmem)
        xh, _ = _rms(x_ref[...])
        u = (xh * g_ref[...]).astype(BF16)
        u_ref[...] = u
        for q in range(N_CHIPS):
            proj_s[:, pw * q:pw * (q + 1)] = jnp.dot(u, w_vmem[q], preferred_element_type=F32)
        ph_ref[...] = proj_s[:, :4 * HG_WIDTH]
        sqkv_ref[:, :SB_WIDTH] = (proj_s[:, 4 * HG_WIDTH:4 * HG_WIDTH + SB_WIDTH] * SB_SCALE).astype(BF16)
        sqkv_ref[:, SB_WIDTH:] = proj_s[:, 4 * HG_WIDTH + SB_WIDTH:].astype(BF16)

    return pl.pallas_call(
        body, name="in_proj_fwd", grid=(T // ROW_TILE,),
        in_specs=[_row_spec(D_MODEL), _full_spec((1, D_MODEL)), ANY_SPEC],
        out_specs=[_row_spec(4 * HG_WIDTH), _row_spec(3 * SB_WIDTH), _row_spec(D_MODEL)],
        out_shape=[jax.ShapeDtypeStruct((T, 4 * HG_WIDTH), F32), jax.ShapeDtypeStruct((T, 3 * SB_WIDTH), BF16),
                   jax.ShapeDtypeStruct((T, D_MODEL), BF16)],
        scratch_shapes=[pltpu.VMEM(w_in.shape, BF16), pltpu.VMEM((ROW_TILE, N_CHIPS * pw), F32)],
        compiler_params=_params(("arbitrary",)),
    )(x, g_pre, w_in)


def _hg_sum_matrix():
    C = HG_CHUNK
    t = np.arange(C)[:, None]
    j = np.arange(C)[None, :]
    mats = [j <= t, j > t]
    for h in HG_LEVELS:
        start = (t // (2 * h)) * (2 * h)
        upper = (t & h) != 0
        mats.append(np.where(upper, (j >= start + h) & (j <= t), (j > t) & (j <= start + h - 1)))
    return np.concatenate(mats, 0).astype(np.float32)


def _hg_level_masks():
    C = HG_CHUNK
    t = lax.broadcasted_iota(jnp.int32, (C, C), 0)
    s = lax.broadcasted_iota(jnp.int32, (C, C), 1)
    x = t ^ s
    masks = [t == s]
    for h in HG_LEVELS:
        masks.append((x >= h) & (x < 2 * h) & (t > s))
    return masks


def _hg_gates(hq, hf, gamma):
    lb = 1.0 / (1.0 + jnp.exp(gamma[1:2, :] - gamma[0:1, :]))
    sq = _sigmoid(hq)
    q = hq * sq
    sig = _sigmoid(hf)
    nsig = _sigmoid(-hf)
    f = lb + (1.0 - lb) * sig
    k = (1.0 - lb) * nsig
    g = jnp.log(f)
    return q, k, g, dict(lb=lb, sq=sq, sig=sig, nsig=nsig, f=f)


def _hg_head_decays(A, h):
    C, K = HG_CHUNK, HG_DK
    sl = slice(K * h, K * (h + 1))
    blocks = [A[C * r:C * (r + 1), sl] for r in range(2 + len(HG_LEVELS))]
    return blocks[0], blocks[1], [None] + blocks[2:]


def _hg_scores(q, k, levels, masks):
    sc = jnp.where(masks[0], _dot_nt(q, k), 0.0)
    for a, m in zip(levels[1:], masks[1:]):
        sc = jnp.where(m, _dot_nt(q * a, k * a), sc)
    return sc


def hgrn2_fwd(proj_h, gamma):
    T = proj_h.shape[0]
    C, K, H = HG_CHUNK, HG_DK, HG_HEADS
    n_chunks = T // C
    msum = jnp.asarray(_hg_sum_matrix(), BF16)

    def body(hq_ref, hf_ref, hi_ref, gam_ref, msum_ref, o_ref, st_ref, st_s):
        _zero_first(st_s)
        q, k, g, _ = _hg_gates(hq_ref[...], hf_ref[...], gam_ref[...])
        v = hi_ref[...]
        A = jnp.exp(_sum01_left(msum_ref[...], g))
        masks = _hg_level_masks()
        for h in range(H):
            sl = slice(K * h, K * (h + 1))
            ab, ar, levels = _hg_head_decays(A, h)
            qh, kh, vh = q[:, sl], k[:, sl], v[:, sl]
            st = st_s[h]
            sc = _hg_scores(qh, kh, levels, masks)
            o_ref[:, sl] = _dot_nt(qh * ab, st) + _dot(sc, vh)
            st_new = st * ab[C - 1:C, :] + _dot_tn(vh, kh * ar)
            st_s[h] = st_new
            st_ref[0, h] = st_new

    blk = lambda col: pl.BlockSpec((C, HG_WIDTH), lambda c, col=col: (c, col))
    return pl.pallas_call(
        body, name="hgrn2_fwd", grid=(n_chunks,),
        in_specs=[blk(0), blk(1), blk(2), _full_spec((2, HG_WIDTH)), _full_spec(msum.shape)],
        out_specs=[blk(0), pl.BlockSpec((1, H, K, K), lambda c: (c, 0, 0, 0))],
        out_shape=[jax.ShapeDtypeStruct((T, HG_WIDTH), F32), jax.ShapeDtypeStruct((n_chunks, H, K, K), F32)],
        scratch_shapes=[pltpu.VMEM((H, K, K), F32)],
        compiler_params=_params(("arbitrary",)),
    )(proj_h, proj_h, proj_h, gamma, msum)


def hgrn2_bwd(proj_h, gamma, states, do):
    T = proj_h.shape[0]
    C, K, H = HG_CHUNK, HG_DK, HG_HEADS
    n_chunks = T // C
    n_sums = 2 + len(HG_LEVELS)
    msum = jnp.asarray(_hg_sum_matrix(), BF16)
    msum_t = jnp.asarray(_hg_sum_matrix().T, BF16)

    def body(hq_ref, hf_ref, hi_ref, do_ref, gam_ref, msum_ref, msum_t_ref, st_in_ref,
             dhq_ref, dhf_ref, dhi_ref, pack_ref, dst_s, dlb_s, dq_s, dk_s, de_s):
        step = pl.program_id(0)
        _zero_first(dst_s)
        _zero_first(dlb_s)
        _zero_first(pack_ref)
        hq = hq_ref[...]
        q, k, g, aux = _hg_gates(hq, hf_ref[...], gam_ref[...])
        v = hi_ref[...]
        do_all = do_ref[...]
        A = jnp.exp(_sum01_left(msum_ref[...], g))
        masks = _hg_level_masks()
        is_last_row = lax.broadcasted_iota(jnp.int32, (C, K), 0) == C - 1
        has_prev = (step < n_chunks - 1).astype(F32)
        for h in range(H):
            sl = slice(K * h, K * (h + 1))
            ab, ar, levels = _hg_head_decays(A, h)
            qh, kh, vh, doh = q[:, sl], k[:, sl], v[:, sl], do_all[:, sl]
            st_in = st_in_ref[0, h] * has_prev
            dst_out = dst_s[h]
            sc = _hg_scores(qh, kh, levels, masks)
            da = _dot_nt(doh, vh)
            t1 = ab * _dot(doh, st_in)
            t2 = ar * _dot(vh, dst_out)
            decayed = _colsum(st_in * dst_out) * ab[C - 1:C, :]
            de_s[0:C, sl] = qh * t1 + jnp.where(is_last_row, decayed, 0.0)
            de_s[C:2 * C, sl] = kh * t2
            dam = jnp.where(masks[0], da, 0.0)
            dq = t1 + _dot(dam, kh)
            dk = t2 + _dot_tn(dam, qh)
            for r, (a, m) in enumerate(zip(levels[1:], masks[1:])):
                dam = jnp.where(m, da, 0.0)
                t1 = a * _dot(dam, kh * a)
                t2 = a * _dot_tn(dam, qh * a)
                dq = dq + t1
                dk = dk + t2
                de_s[C * (r + 2):C * (r + 3), sl] = qh * t1 + kh * t2
            dhi_ref[:, sl] = _dot_tn(sc, doh) + _dot_nt(kh * ar, dst_out)
            dst_s[h] = dst_out * ab[C - 1:C, :] + _dot_tn(doh, qh * ab)
            dq_s[:, sl] = dq
            dk_s[:, sl] = dk
        dg = _sum01_left(msum_t_ref[...], de_s[...])
        dk = dk_s[...]
        sq, lb = aux["sq"], aux["lb"]
        dhq_ref[...] = dq_s[...] * (sq * (1.0 + hq * (1.0 - sq)))
        common = dg / aux["f"] - dk
        dhf_ref[...] = (1.0 - lb) * aux["sig"] * aux["nsig"] * common
        dlb_s[...] += _colsum(aux["nsig"] * common)

        @pl.when(step == n_chunks - 1)
        def _():
            dgam = lb * (1.0 - lb) * dlb_s[...]
            pack_ref[ROW_GAMMA:ROW_GAMMA + 1, :HG_WIDTH] = dgam
            pack_ref[ROW_GAMMA:ROW_GAMMA + 1, HG_WIDTH:] = -dgam

    last = n_chunks - 1
    blk = lambda col: pl.BlockSpec((C, HG_WIDTH), lambda c, col=col: (last - c, col))
    return pl.pallas_call(
        body, name="hgrn2_bwd", grid=(n_chunks,),
        in_specs=[blk(0), blk(1), blk(2), blk(0), _full_spec((2, HG_WIDTH)), _full_spec(msum.shape),
                  _full_spec(msum_t.shape),
                  pl.BlockSpec((1, H, K, K), lambda c: (jnp.maximum(last - c - 1, 0), 0, 0, 0))],
        out_specs=[blk(0), blk(0), blk(0), PACK_SPEC],
        out_shape=[jax.ShapeDtypeStruct((T, HG_WIDTH), F32)] * 3 + [jax.ShapeDtypeStruct((8, D_MODEL), F32)],
        scratch_shapes=[pltpu.VMEM((H, K, K), F32), pltpu.VMEM((1, HG_WIDTH), F32), pltpu.VMEM((C, HG_WIDTH), F32),
                        pltpu.VMEM((C, HG_WIDTH), F32), pltpu.VMEM((n_sums * C, HG_WIDTH), F32)],
        compiler_params=_params(("arbitrary",)),
    )(proj_h, proj_h, proj_h, do, gamma, msum, msum_t, states)


def _sb_sum_matrix(inclusive):
    B = SB_BLOCK
    j = np.arange(B)[:, None]
    s = np.arange(B)[None, :]
    tri = (j >= s) if inclusive else (j > s)
    return np.concatenate([tri, np.ones((B, B), bool)], 1).astype(np.float32)


def _sb_prefix_matrix(inclusive):
    B = SB_BLOCK
    j = np.arange(B)[:, None]
    s = np.arange(B)[None, :]
    tri = (j <= s) if inclusive else (j < s)
    return np.concatenate([tri, np.ones((B, B), bool)], 1).astype(np.float32)


def _sb_iotas():
    shape = (SB_BLOCK, SB_BLOCK)
    return lax.broadcasted_iota(jnp.int32, shape, 0), lax.broadcasted_iota(jnp.int32, shape, 1)


def _sb_weights(qh, kj, valid, carry, usum):
    z = _dot_nt(qh, kj)
    lnb = jnp.where(valid, _neg_softplus(z), 0.0)
    lb = z + lnb
    sums = _sum01_right(lnb, usum)
    a = jnp.where(valid, jnp.exp(lb + carry + sums[:, :SB_BLOCK]), 0.0)
    return a, lb, carry + sums[:, SB_BLOCK:]


def sb_fwd(sqkv):
    T = sqkv.shape[0]
    B = SB_BLOCK
    pairs = SB_WIDTH // B
    usum = jnp.asarray(_sb_sum_matrix(False), BF16)

    def body(q_ref, k_ref, v_ref, u_ref, o_ref, tl_ref, first_ref):
        p, i = pl.program_id(0), pl.program_id(1)
        row, lane = _sb_iotas()
        first = lane < SB_DH
        q = q_ref[...]
        heads = (jnp.where(first, q, jnp.zeros_like(q)), jnp.where(first, jnp.zeros_like(q), q))
        u = u_ref[...]

        def more(loop):
            n, reachable, _ = loop
            return (n <= i) & (reachable > 0)

        def step(loop):
            n, _, state = loop
            j = i - n
            off = pl.multiple_of(j * B, B)
            kj = k_ref[pl.ds(off, B), :]
            vj = v_ref[pl.ds(off, B), :]
            valid = (lane + j * B) < (row + i * B)
            out = []
            for qh, (carry, acc) in zip(heads, state):
                a, _, carry = _sb_weights(qh, kj, valid, carry, u)
                out.append((carry, acc + _dot(a, vj)))
            reachable = (jnp.max(jnp.maximum(out[0][0], out[1][0])) > SB_UNDERFLOW_LOG).astype(jnp.int32)
            return n + 1, reachable, tuple(out)

        zero = jnp.zeros((B, B), F32)
        done, _, ((tot0, acc0), (tot1, acc1)) = lax.while_loop(
            more, step, (jnp.int32(0), jnp.int32(1), ((zero, zero), (zero, zero))))
        o_ref[...] = jnp.where(first, acc0, acc1)
        tl_ref[...] = jnp.where(first, tot0, tot1)
        first_ref[p, i] = i + 1 - done

    return pl.pallas_call(
        body, name="sb_fwd", grid=(pairs, T // B),
        in_specs=[pl.BlockSpec((B, B), lambda p, i: (i, p)),
                  pl.BlockSpec((T, B), lambda p, i: (0, pairs + p)),
                  pl.BlockSpec((T, B), lambda p, i: (0, 2 * pairs + p)),
                  pl.BlockSpec(usum.shape, lambda p, i: (0, 0))],
        out_specs=[pl.BlockSpec((B, B), lambda p, i: (i, p))] * 2 + [pl.BlockSpec(memory_space=pltpu.SMEM)],
        out_shape=[jax.ShapeDtypeStruct((T, SB_WIDTH), F32)] * 2 + [jax.ShapeDtypeStruct((pairs, T // B), jnp.int32)],
        compiler_params=_params(("arbitrary", "arbitrary")),
    )(sqkv, sqkv, sqkv, usum)


def sb_bwd(sqkv, do, tl, first_block):
    T = sqkv.shape[0]
    B = SB_BLOCK
    pairs = SB_WIDTH // B
    upre = jnp.asarray(_sb_prefix_matrix(True), BF16)
    uexc = jnp.asarray(_sb_prefix_matrix(False), BF16)

    def body(q_ref, k_ref, v_ref, do_ref, tl_ref, up_ref, ue_ref, first_ref, dq_ref, dk_ref, dv_ref):
        p, i = pl.program_id(0), pl.program_id(1)

        @pl.when(i == 0)
        def _():
            dk_ref[...] = jnp.zeros(dk_ref.shape, F32)
            dv_ref[...] = jnp.zeros(dv_ref.shape, F32)

        row, lane = _sb_iotas()
        first = lane < SB_DH
        q = q_ref[...]
        do = do_ref[...]
        tl_all = tl_ref[...]
        zb = jnp.zeros_like(q)
        zf = jnp.zeros_like(do)
        heads = (
            (jnp.where(first, q, zb), jnp.where(first, do, zf).astype(BF16), tl_all[:, 0:1]),
            (jnp.where(first, zb, q), jnp.where(first, zf, do).astype(BF16), tl_all[:, B - 1:B]),
        )
        up = up_ref[...]
        ue = ue_ref[...]

        def step(j, state):
            off = pl.multiple_of(j * B, B)
            kj = k_ref[pl.ds(off, B), :]
            vj = v_ref[pl.ds(off, B), :]
            valid = (lane + j * B) < (row + i * B)
            out = []
            dk_add = jnp.zeros((B, B), F32)
            dv_add = jnp.zeros((B, B), F32)
            for (qh, doh, total), (seen, seen_w, dq) in zip(heads, state):
                z = _dot_nt(qh, kj)
                lnb = jnp.where(valid, _neg_softplus(z), 0.0)
                lb = z + lnb
                sums = _sum01_right(lnb, up)
                a = jnp.where(valid, jnp.exp(lb + (total - seen - sums[:, :B])), 0.0)
                beta = jnp.exp(lb)
                w = a * _dot_nt(doh, vj)
                wsums = _sum01_right(w, ue)
                before = seen_w + wsums[:, :B]
                dz = jnp.where(valid, w * (1.0 - beta) - before * beta, 0.0)
                dk_add = dk_add + _dot_tn(dz, qh)
                dv_add = dv_add + _dot_tn(a, doh)
                out.append((seen + sums[:, B:], seen_w + wsums[:, B:], dq + _dot(dz, kj)))
            dk_ref[pl.ds(off, B), :] += dk_add
            dv_ref[pl.ds(off, B), :] += dv_add
            return tuple(out)

        zero = jnp.zeros((B, B), F32)
        (_, _, dq0), (_, _, dq1) = lax.fori_loop(first_ref[p, i], i + 1, step, ((zero, zero, zero), (zero, zero, zero)))
        dq_ref[...] = jnp.where(first, dq0, dq1) * SB_SCALE

    qblk = pl.BlockSpec((B, B), lambda p, i: (i, p))
    full = pl.BlockSpec((T, B), lambda p, i: (0, p))
    return pl.pallas_call(
        body, name="sb_bwd", grid=(pairs, T // B),
        in_specs=[qblk, pl.BlockSpec((T, B), lambda p, i: (0, pairs + p)),
                  pl.BlockSpec((T, B), lambda p, i: (0, 2 * pairs + p)), qblk, qblk,
                  pl.BlockSpec(upre.shape, lambda p, i: (0, 0)), pl.BlockSpec(uexc.shape, lambda p, i: (0, 0)),
                  pl.BlockSpec(memory_space=pltpu.SMEM)],
        out_specs=[qblk, full, full],
        out_shape=[jax.ShapeDtypeStruct((T, SB_WIDTH), F32)] * 3,
        compiler_params=_params(("arbitrary", "arbitrary")),
    )(sqkv, sqkv, sqkv, do, tl, upre, uexc, first_block)


def _mixer_out(o_hg, hg, o_sb, g_hg, g_sb):
    n_hg, r_hg = _rms(o_hg)
    s_hg = _sigmoid(hg)
    n_sb, r_sb = _rms(o_sb)
    return dict(n_hg=n_hg, r_hg=r_hg, s_hg=s_hg, n_sb=n_sb, r_sb=r_sb,
                y_hg=n_hg * g_hg * (hg * s_hg), y_sb=n_sb * g_sb)


def mix_out_fwd(o_hg, proj_h, o_sb, x, norms, g_post, w_out):
    T = x.shape[0]

    def body(ohg_ref, hg_ref, osb_ref, x_ref, nrm_ref, gp_ref, w_hbm, cat_ref, mix_ref, h1_ref, w_vmem):
        _load_once(w_hbm, w_vmem)
        nrm = nrm_ref[...]
        m = _mixer_out(ohg_ref[...], hg_ref[...], osb_ref[...], nrm[:, :HG_WIDTH], nrm[:, HG_WIDTH:])
        cat_ref[:, :HG_WIDTH] = m["y_hg"].astype(BF16)
        cat_ref[:, HG_WIDTH:] = m["y_sb"].astype(BF16)
        mix = jnp.dot(cat_ref[...], w_vmem[...], preferred_element_type=F32)
        mix_ref[...] = mix
        mh, _ = _rms(mix)
        h1_ref[...] = x_ref[...] + mh * gp_ref[...]

    return pl.pallas_call(
        body, name="mix_out_fwd", grid=(T // ROW_TILE,),
        in_specs=[_row_spec(HG_WIDTH), _row_spec(HG_WIDTH, 3), _row_spec(SB_WIDTH), _row_spec(D_MODEL),
                  _full_spec((1, D_MODEL)), _full_spec((1, D_MODEL)), ANY_SPEC],
        out_specs=[_row_spec(D_MODEL)] * 3,
        out_shape=[jax.ShapeDtypeStruct((T, D_MODEL), BF16), jax.ShapeDtypeStruct((T, D_MODEL), F32),
                   jax.ShapeDtypeStruct((T, D_MODEL), F32)],
        scratch_shapes=[pltpu.VMEM(w_out.shape, BF16)],
        compiler_params=_params(("arbitrary",)),
    )(o_hg, proj_h, o_sb, x, norms, g_post, w_out)


def ffn_fwd(h1, g_pre, g_post, w_gu, w_down):
    T = h1.shape[0]
    pw = w_gu.shape[2]

    def body(h1_ref, gpre_ref, gpost_ref, wgu_hbm, wd_hbm, u2_ref, gu_ref, act_ref, y_ref, h2_ref,
             wgu_vmem, wd_vmem, gu_s):
        _load_once(wgu_hbm, wgu_vmem)
        _load_once(wd_hbm, wd_vmem)
        h1v = h1_ref[...]
        hh, _ = _rms(h1v)
        u2 = (hh * gpre_ref[...]).astype(BF16)
        u2_ref[...] = u2
        for q in range(N_CHIPS):
            gu_s[:, pw * q:pw * (q + 1)] = jnp.dot(u2, wgu_vmem[q], preferred_element_type=F32)
        gu_ref[...] = gu_s[...].astype(BF16)
        gate = gu_s[:, :D_FF]
        act = (gate * _sigmoid(gate) * gu_s[:, D_FF:]).astype(BF16)
        act_ref[...] = act
        y = jnp.dot(act, wd_vmem[...], preferred_element_type=F32)
        y_ref[...] = y
        yh, _ = _rms(y)
        h2_ref[...] = h1v + yh * gpost_ref[...]

    return pl.pallas_call(
        body, name="ffn_fwd", grid=(T // ROW_TILE,),
        in_specs=[_row_spec(D_MODEL), _full_spec((1, D_MODEL)), _full_spec((1, D_MODEL)), ANY_SPEC, ANY_SPEC],
        out_specs=[_row_spec(D_MODEL), _row_spec(2 * D_FF), _row_spec(D_FF), _row_spec(D_MODEL), _row_spec(D_MODEL)],
        out_shape=[jax.ShapeDtypeStruct((T, D_MODEL), BF16), jax.ShapeDtypeStruct((T, 2 * D_FF), BF16),
                   jax.ShapeDtypeStruct((T, D_FF), BF16), jax.ShapeDtypeStruct((T, D_MODEL), F32),
                   jax.ShapeDtypeStruct((T, D_MODEL), F32)],
        scratch_shapes=[pltpu.VMEM(w_gu.shape, BF16), pltpu.VMEM(w_down.shape, BF16),
                        pltpu.VMEM((ROW_TILE, 2 * D_FF), F32)],
        compiler_params=_params(("arbitrary",)),
    )(h1, g_pre, g_post, w_gu, w_down)


def ple_loss(h2, p, target, w_ple, w_pg):
    T = h2.shape[0]
    pw = w_ple.shape[2]

    def body(h2_ref, p_ref, t_ref, wple_hbm, wpg_hbm, de_ref, ds_ref, dh2_ref, h2b_ref, pb_ref, pack_ref,
             wple_vmem, wpg_vmem, e_s):
        _load_once(wple_hbm, wple_vmem)
        _load_once(wpg_hbm, wpg_vmem)
        _zero_first(pack_ref)
        h2v = h2_ref[...]
        h2b = h2v.astype(BF16)
        h2b_ref[...] = h2b
        pb = p_ref[...].astype(BF16)
        pb_ref[...] = pb
        for q in range(N_CHIPS):
            e_s[:, pw * q:pw * (q + 1)] = jnp.dot(pb, wple_vmem[q], preferred_element_type=F32)
        e = e_s[...]
        sig = _sigmoid(jnp.dot(h2b, wpg_vmem[...], preferred_element_type=F32))
        err = h2v + e * sig - t_ref[...]
        part = 0.5 * jnp.sum(jnp.mean(err * err, axis=-1, keepdims=True), axis=0, keepdims=True)
        lane = lax.broadcasted_iota(jnp.int32, (1, D_MODEL), 1)
        pack_ref[ROW_LOSS:ROW_LOSS + 1, :] += jnp.where(lane == 0, part, 0.0)
        dh3 = err * (1.0 / D_MODEL)
        de_ref[...] = (dh3 * sig).astype(BF16)
        ds = (dh3 * e * sig * (1.0 - sig)).astype(BF16)
        ds_ref[...] = ds
        dh2_ref[...] = dh3 + _dot_nt(ds, wpg_vmem[...])

    return pl.pallas_call(
        body, name="ple_loss", grid=(T // ROW_TILE,),
        in_specs=[_row_spec(D_MODEL), _row_spec(p.shape[1]), _row_spec(D_MODEL), ANY_SPEC, ANY_SPEC],
        out_specs=[_row_spec(D_MODEL), _row_spec(D_MODEL), _row_spec(D_MODEL), _row_spec(D_MODEL),
                   _row_spec(p.shape[1]), PACK_SPEC],
        out_shape=[jax.ShapeDtypeStruct((T, D_MODEL), BF16), jax.ShapeDtypeStruct((T, D_MODEL), BF16),
                   jax.ShapeDtypeStruct((T, D_MODEL), F32), jax.ShapeDtypeStruct((T, D_MODEL), BF16),
                   jax.ShapeDtypeStruct(p.shape, BF16), jax.ShapeDtypeStruct((8, D_MODEL), F32)],
        scratch_shapes=[pltpu.VMEM(w_ple.shape, BF16), pltpu.VMEM(w_pg.shape, BF16), pltpu.VMEM((ROW_TILE, D_MODEL), F32)],
        compiler_params=_params(("arbitrary",)),
    )(h2, p, target, w_ple, w_pg)


def ffn_bwd(dh2, y, h1, gu, g_pre, g_post, w_gu, w_down):
    T = h1.shape[0]
    pw = w_gu.shape[2]

    def body(dh2_ref, y_ref, h1_ref, gu_ref, gpre_ref, gpost_ref, wgu_hbm, wd_hbm, dy_ref, dgu_ref, dh1_ref, pack_ref,
             wgu_vmem, wd_vmem):
        _load_once(wgu_hbm, wgu_vmem)
        _load_once(wd_hbm, wd_vmem)
        _zero_first(pack_ref)
        dh2v = dh2_ref[...]
        yh, ry = _rms(y_ref[...])
        dy, dw = _rms_bwd(dh2v, yh, ry, gpost_ref[...])
        pack_ref[ROW_FFN_POST:ROW_FFN_POST + 1, :] += _colsum(dw)
        dyb = dy.astype(BF16)
        dy_ref[...] = dyb
        dact = _dot_nt(dyb, wd_vmem[...])
        gate = gu_ref[:, :D_FF].astype(F32)
        up = gu_ref[:, D_FF:].astype(F32)
        sg = _sigmoid(gate)
        dgu_ref[:, :D_FF] = (dact * up * (sg * (1.0 + gate * (1.0 - sg)))).astype(BF16)
        dgu_ref[:, D_FF:] = (dact * gate * sg).astype(BF16)
        du2 = _dot_nt(dgu_ref[:, :pw], wgu_vmem[0])
        for q in range(1, N_CHIPS):
            du2 = du2 + _dot_nt(dgu_ref[:, pw * q:pw * (q + 1)], wgu_vmem[q])
        hh, rh = _rms(h1_ref[...])
        dh, dw = _rms_bwd(du2, hh, rh, gpre_ref[...])
        pack_ref[ROW_FFN_PRE:ROW_FFN_PRE + 1, :] += _colsum(dw)
        dh1_ref[...] = dh2v + dh

    return pl.pallas_call(
        body, name="ffn_bwd", grid=(T // ROW_TILE,),
        in_specs=[_row_spec(D_MODEL), _row_spec(D_MODEL), _row_spec(D_MODEL), _row_spec(2 * D_FF),
                  _full_spec((1, D_MODEL)), _full_spec((1, D_MODEL)), ANY_SPEC, ANY_SPEC],
        out_specs=[_row_spec(D_MODEL), _row_spec(2 * D_FF), _row_spec(D_MODEL), PACK_SPEC],
        out_shape=[jax.ShapeDtypeStruct((T, D_MODEL), BF16), jax.ShapeDtypeStruct((T, 2 * D_FF), BF16),
                   jax.ShapeDtypeStruct((T, D_MODEL), F32), jax.ShapeDtypeStruct((8, D_MODEL), F32)],
        scratch_shapes=[pltpu.VMEM(w_gu.shape, BF16), pltpu.VMEM(w_down.shape, BF16)],
        compiler_params=_params(("arbitrary",)),
    )(dh2, y, h1, gu, g_pre, g_post, w_gu, w_down)


def mix_out_bwd(dh1, mix, o_hg, proj_h, o_sb, norms, g_post, w_out):
    T = dh1.shape[0]

    def body(dh1_ref, mix_ref, ohg_ref, hg_ref, osb_ref, nrm_ref, gp_ref, w_hbm, dmix_ref, dohg_ref, dhg_ref, dosb_ref,
             pack_ref, w_vmem):
        _load_once(w_hbm, w_vmem)
        _zero_first(pack_ref)
        mh, rm = _rms(mix_ref[...])
        dmix, dw = _rms_bwd(dh1_ref[...], mh, rm, gp_ref[...])
        pack_ref[ROW_ATTN_POST:ROW_ATTN_POST + 1, :] += _colsum(dw)
        dmb = dmix.astype(BF16)
        dmix_ref[...] = dmb
        dcat = _dot_nt(dmb, w_vmem[...])
        nrm = nrm_ref[...]
        g_hg, g_sb = nrm[:, :HG_WIDTH], nrm[:, HG_WIDTH:]
        hg = hg_ref[...]
        m = _mixer_out(ohg_ref[...], hg, osb_ref[...], g_hg, g_sb)
        d_hg = dcat[:, :HG_WIDTH]
        silu = hg * m["s_hg"]
        dhg_ref[...] = d_hg * (m["n_hg"] * g_hg) * (m["s_hg"] * (1.0 + hg * (1.0 - m["s_hg"])))
        dx, dw = _rms_bwd(d_hg * silu, m["n_hg"], m["r_hg"], g_hg)
        dohg_ref[...] = dx
        pack_ref[ROW_MIX_NORMS:ROW_MIX_NORMS + 1, :HG_WIDTH] += _colsum(dw)
        dx, dw = _rms_bwd(dcat[:, HG_WIDTH:], m["n_sb"], m["r_sb"], g_sb)
        dosb_ref[...] = dx
        pack_ref[ROW_MIX_NORMS:ROW_MIX_NORMS + 1, HG_WIDTH:] += _colsum(dw)

    return pl.pallas_call(
        body, name="mix_out_bwd", grid=(T // ROW_TILE,),
        in_specs=[_row_spec(D_MODEL), _row_spec(D_MODEL), _row_spec(HG_WIDTH), _row_spec(HG_WIDTH, 3), _row_spec(SB_WIDTH),
                  _full_spec((1, D_MODEL)), _full_spec((1, D_MODEL)), ANY_SPEC],
        out_specs=[_row_spec(D_MODEL), _row_spec(HG_WIDTH), _row_spec(HG_WIDTH), _row_spec(SB_WIDTH), PACK_SPEC],
        out_shape=[jax.ShapeDtypeStruct((T, D_MODEL), BF16), jax.ShapeDtypeStruct((T, HG_WIDTH), F32),
                   jax.ShapeDtypeStruct((T, HG_WIDTH), F32), jax.ShapeDtypeStruct((T, SB_WIDTH), F32),
                   jax.ShapeDtypeStruct((8, D_MODEL), F32)],
        scratch_shapes=[pltpu.VMEM(w_out.shape, BF16)],
        compiler_params=_params(("arbitrary",)),
    )(dh1, mix, o_hg, proj_h, o_sb, norms, g_post, w_out)


def in_proj_bwd(parts, x, dh1, g_pre, w_in):
    T = x.shape[0]
    pw = w_in.shape[2]
    n_parts = len(parts)

    def body(*refs):
        part_refs = refs[:n_parts]
        x_ref, dh1_ref, g_ref, w_hbm, dproj_ref, dx_ref, pack_ref, w_vmem = refs[n_parts:]
        _load_once(w_hbm, w_vmem)
        _zero_first(pack_ref)
        for n, ref in enumerate(part_refs):
            dproj_ref[:, HG_WIDTH * n:HG_WIDTH * (n + 1)] = ref[...].astype(BF16)
        du = _dot_nt(dproj_ref[:, :pw], w_vmem[0])
        for q in range(1, N_CHIPS):
            du = du + _dot_nt(dproj_ref[:, pw * q:pw * (q + 1)], w_vmem[q])
        xh, r = _rms(x_ref[...])
        dx, dw = _rms_bwd(du, xh, r, g_ref[...])
        pack_ref[ROW_ATTN_PRE:ROW_ATTN_PRE + 1, :] += _colsum(dw)
        dx_ref[...] = dh1_ref[...] + dx

    return pl.pallas_call(
        body, name="in_proj_bwd", grid=(T // ROW_TILE,),
        in_specs=[_row_spec(HG_WIDTH)] * n_parts + [_row_spec(D_MODEL), _row_spec(D_MODEL), _full_spec((1, D_MODEL)), ANY_SPEC],
        out_specs=[_row_spec(n_parts * HG_WIDTH), _row_spec(D_MODEL), PACK_SPEC],
        out_shape=[jax.ShapeDtypeStruct((T, n_parts * HG_WIDTH), BF16), jax.ShapeDtypeStruct((T, D_MODEL), F32),
                   jax.ShapeDtypeStruct((8, D_MODEL), F32)],
        scratch_shapes=[pltpu.VMEM(w_in.shape, BF16)],
        compiler_params=_params(("arbitrary",)),
    )(*parts, x, dh1, g_pre, w_in)


def weight_grad(a, g, name, *, tm, tn, tk=512, col_pieces=False):
    T, M = a.shape
    N = g.shape[1]
    steps = T // tk

    def body(a_ref, g_ref, o_ref):
        @pl.when(pl.program_id(2) == 0)
        def _():
            o_ref[...] = jnp.zeros(o_ref.shape, F32)

        o_ref[...] += _dot_tn(a_ref[...], g_ref[...]).reshape(o_ref.shape)

    if col_pieces:
        out_shape = jax.ShapeDtypeStruct((N // tn, M, tn), F32)
        out_spec = pl.BlockSpec((1, tm, tn), lambda i, j, k: (j, i, 0))
    else:
        out_shape = jax.ShapeDtypeStruct((M, N), F32)
        out_spec = pl.BlockSpec((tm, tn), lambda i, j, k: (i, j))
    return pl.pallas_call(
        body, name=name, grid=(M // tm, N // tn, steps),
        in_specs=[pl.BlockSpec((tk, tm), lambda i, j, k: (k, i)), pl.BlockSpec((tk, tn), lambda i, j, k: (k, j))],
        out_specs=out_spec, out_shape=out_shape,
        compiler_params=_params(("arbitrary", "arbitrary", "arbitrary")),
    )(a, g)


def _place():
    x, y, c = lax.axis_index("x"), lax.axis_index("y"), lax.axis_index("c")
    chips = [(1 - x, y), (x, 1 - y), (1 - x, 1 - y)]
    return x, y, c, chips


def _chip_index(cx, cy):
    return 2 * cx + cy


def _rcopy(src, dst, send_sem, recv_sem, device):
    return pltpu.make_async_remote_copy(src_ref=src, dst_ref=dst, send_sem=send_sem, recv_sem=recv_sem,
                                        device_id=device, device_id_type=MESH)


def gather_weights(shards):
    n = len(shards)

    def body(*refs):
        ins, outs = refs[:n], refs[2 * n:3 * n]
        send_sems, recv_sems = refs[3 * n:]
        x, y, c, chips = _place()
        me = _chip_index(x, y)
        sibling = (x, y, 1 - c)

        def rows(w, core):
            half = ins[w].shape[0] // 2
            return pl.ds(core * half, half)

        sends = []
        for w in range(n):
            for j, chip in enumerate(chips):
                sends.append(_rcopy(ins[w].at[rows(w, c)], outs[w].at[me, rows(w, c)],
                                    send_sems.at[6 * w + j], recv_sems.at[6 * w + j], (*chip, c)))
        for cp in sends:
            cp.start()
        passed = []
        for w in range(n):
            for j, chip in enumerate(chips):
                block = outs[w].at[_chip_index(*chip), rows(w, c)]
                _rcopy(block, block, send_sems.at[6 * w + j], recv_sems.at[6 * w + j], (*chip, c)).wait_recv()
                cp = _rcopy(block, block, send_sems.at[6 * w + 3 + j], recv_sems.at[6 * w + 3 + j], sibling)
                cp.start()
                passed.append(cp)
        for w in range(n):
            for j, chip in enumerate(chips):
                block = outs[w].at[_chip_index(*chip), rows(w, 1 - c)]
                _rcopy(block, block, send_sems.at[6 * w + 3 + j], recv_sems.at[6 * w + 3 + j], sibling).wait_recv()
        for cp in sends + passed:
            cp.wait_send()

    filled = [jnp.broadcast_to(s[None], (N_CHIPS,) + s.shape) for s in shards]
    return pl.pallas_call(
        body, name="gather_weights",
        in_specs=[ANY_SPEC] * (2 * n), out_specs=[ANY_SPEC] * n,
        out_shape=[jax.ShapeDtypeStruct(f.shape, f.dtype) for f in filled],
        input_output_aliases={n + w: w for w in range(n)},
        scratch_shapes=[pltpu.SemaphoreType.DMA((6 * n,)), pltpu.SemaphoreType.DMA((6 * n,))],
    )(*shards, *filled)


def exchange_core_halves(grads):
    n = len(grads)

    def body(*refs):
        ins, got = refs[:n], refs[n:2 * n]
        send_sems, recv_sems = refs[2 * n:]
        x, y, c, _ = _place()
        copies = []
        for w in range(n):
            half = ins[w].shape[1] // 2
            copies.append(_rcopy(ins[w].at[:, pl.ds((1 - c) * half, half), :], got[w],
                                 send_sems.at[w], recv_sems.at[w], (x, y, 1 - c)))
        for cp in copies:
            cp.start()
        for cp in copies:
            cp.wait()

    return pl.pallas_call(
        body, name="exchange_core_halves",
        in_specs=[ANY_SPEC] * n, out_specs=[ANY_SPEC] * n,
        out_shape=[jax.ShapeDtypeStruct((g.shape[0], g.shape[1] // 2, g.shape[2]), g.dtype) for g in grads],
        scratch_shapes=[pltpu.SemaphoreType.DMA((n,)), pltpu.SemaphoreType.DMA((n,))],
    )(*grads)


def exchange_chip_partials(partials):
    n = len(partials)

    def body(*refs):
        ins, outs = refs[:n], refs[2 * n:3 * n]
        send_sems, recv_sems = refs[3 * n:]
        x, y, c, chips = _place()
        me = _chip_index(x, y)
        sends = []
        for w in range(n):
            for j, chip in enumerate(chips):
                sends.append(_rcopy(ins[w].at[_chip_index(*chip)], outs[w].at[me],
                                    send_sems.at[3 * w + j], recv_sems.at[3 * w + j], (*chip, c)))
        for cp in sends:
            cp.start()
        for w in range(n):
            for j, chip in enumerate(chips):
                block = outs[w].at[_chip_index(*chip)]
                _rcopy(block, block, send_sems.at[3 * w + j], recv_sems.at[3 * w + j], (*chip, c)).wait_recv()
        for cp in sends:
            cp.wait_send()

    me = _chip_index(lax.axis_index("x"), lax.axis_index("y"))
    filled = [jnp.broadcast_to(lax.dynamic_index_in_dim(p, me, 0, keepdims=True), p.shape) for p in partials]
    return pl.pallas_call(
        body, name="exchange_chip_partials",
        in_specs=[ANY_SPEC] * (2 * n), out_specs=[ANY_SPEC] * n,
        out_shape=[jax.ShapeDtypeStruct(p.shape, p.dtype) for p in partials],
        input_output_aliases={n + w: w for w in range(n)},
        scratch_shapes=[pltpu.SemaphoreType.DMA((3 * n,)), pltpu.SemaphoreType.DMA((3 * n,))],
    )(*partials, *filled)


def join_core_halves(grads):
    n = len(grads)

    def body(*refs):
        outs = refs[n:2 * n]
        send_sems, recv_sems = refs[2 * n:]
        x, y, c, _ = _place()
        sibling = (x, y, 1 - c)
        copies = []
        for w in range(n):
            half = outs[w].shape[0] // 2
            mine = outs[w].at[pl.ds(c * half, half), :]
            copies.append(_rcopy(mine, mine, send_sems.at[w], recv_sems.at[w], sibling))
        for cp in copies:
            cp.start()
        for w in range(n):
            half = outs[w].shape[0] // 2
            theirs = outs[w].at[pl.ds((1 - c) * half, half), :]
            _rcopy(theirs, theirs, send_sems.at[w], recv_sems.at[w], sibling).wait_recv()
        for cp in copies:
            cp.wait_send()

    return pl.pallas_call(
        body, name="join_core_halves",
        in_specs=[ANY_SPEC] * n, out_specs=[ANY_SPEC] * n,
        out_shape=[jax.ShapeDtypeStruct(g.shape, g.dtype) for g in grads],
        input_output_aliases={w: w for w in range(n)},
        scratch_shapes=[pltpu.SemaphoreType.DMA((n,)), pltpu.SemaphoreType.DMA((n,))],
    )(*grads)


def _elementwise_rows(rows, cap=512):
    for t in range(min(rows, cap), 0, -8):
        if rows % t == 0 and t % 16 == 0:
            return t
    return rows


def add_core_halves(grad, got, core, name):
    _, rows, cols = got.shape
    tr = _elementwise_rows(rows)
    nt = rows // tr

    def body(core_ref, a_ref, b_ref, o_ref):
        o_ref[...] = (a_ref[...] + b_ref[...]).astype(BF16)

    spec = pl.BlockSpec((1, tr, cols), lambda q, i, core_ref: (q, i, 0))
    own = pl.BlockSpec((1, tr, cols), lambda q, i, core_ref: (q, core_ref[0] * nt + i, 0))
    return pl.pallas_call(
        body, name=name,
        grid_spec=pltpu.PrefetchScalarGridSpec(num_scalar_prefetch=1, grid=(N_CHIPS, nt), in_specs=[own, spec],
                                               out_specs=spec),
        out_shape=jax.ShapeDtypeStruct(got.shape, BF16),
        compiler_params=_params(("arbitrary", "arbitrary")),
    )(core, grad, got)


def add_chip_partials(parts, core, name):
    _, rows, cols = parts.shape
    tr = _elementwise_rows(rows)
    nt = rows // tr

    def body(core_ref, p_ref, o_ref):
        acc = p_ref[0].astype(F32)
        for q in range(1, N_CHIPS):
            acc = acc + p_ref[q].astype(F32)
        o_ref[...] = acc

    return pl.pallas_call(
        body, name=name,
        grid_spec=pltpu.PrefetchScalarGridSpec(
            num_scalar_prefetch=1, grid=(nt,),
            in_specs=[pl.BlockSpec((N_CHIPS, tr, cols), lambda i, core_ref: (0, i, 0))],
            out_specs=pl.BlockSpec((tr, cols), lambda i, core_ref: (core_ref[0] * nt + i, 0))),
        out_shape=jax.ShapeDtypeStruct((2 * rows, cols), F32),
        compiler_params=_params(("arbitrary",)),
    )(core, parts)


def _adamw_math(w, g, m, v):
    m = ADAM_B1 * m + (1.0 - ADAM_B1) * g
    v = ADAM_B2 * v + (1.0 - ADAM_B2) * (g * g)
    m_hat = m / (1.0 - ADAM_B1 ** ADAM_STEP)
    v_hat = v / (1.0 - ADAM_B2 ** ADAM_STEP)
    delta = -ADAM_LR * (m_hat / (jnp.sqrt(v_hat) + ADAM_EPS) + ADAM_WD * w)
    return delta, m, v


def adamw(w, g, m, v, name):
    rows, cols = w.shape
    tr = _elementwise_rows(rows, 256)

    def body(w_ref, g_ref, m_ref, v_ref, d_ref, nm_ref, nv_ref):
        d, nm, nv = _adamw_math(w_ref[...], g_ref[...], m_ref[...], v_ref[...])
        d_ref[...] = d
        nm_ref[...] = nm
        nv_ref[...] = nv

    spec = pl.BlockSpec((tr, cols), lambda i: (i, 0))
    return pl.pallas_call(
        body, name=name, grid=(rows // tr,), in_specs=[spec] * 4, out_specs=[spec] * 3,
        out_shape=[jax.ShapeDtypeStruct((rows, cols), F32)] * 3,
        compiler_params=_params(("arbitrary",)),
    )(w, g, m, v)


def reduce_small(packs, w, m, v):
    n = len(packs)
    n_dev = 8
    flips = [(fx, fy, fc) for fx in (0, 1) for fy in (0, 1) for fc in (0, 1)][1:]

    def body(*refs):
        pack_refs = refs[:n]
        w_ref, m_ref, v_ref, g_out, d_out, m_out, v_out, mine, slots, send_sems, recv_sems = refs[n:]
        x, y, c, _ = _place()
        me = 4 * x + 2 * y + c
        acc = pack_refs[0][...]
        for ref in pack_refs[1:]:
            acc = acc + ref[...]
        mine[...] = acc
        sends = []
        for k, (fx, fy, fc) in enumerate(flips):
            peer = (x ^ fx, y ^ fy, c ^ fc)
            sends.append(_rcopy(mine, slots.at[me], send_sems.at[k], recv_sems.at[me], peer))
        for cp in sends:
            cp.start()
        slots[me] = acc
        for fx, fy, fc in flips:
            src = 4 * (x ^ fx) + 2 * (y ^ fy) + (c ^ fc)
            _rcopy(mine, slots.at[src], send_sems.at[0], recv_sems.at[src], (x, y, c)).wait_recv()
        for cp in sends:
            cp.wait_send()
        total = slots[0]
        for d in range(1, n_dev):
            total = total + slots[d]
        g_out[...] = total
        d, nm, nv = _adamw_math(w_ref[...], total, m_ref[...], v_ref[...])
        d_out[...] = d
        m_out[...] = nm
        v_out[...] = nv

    vm = pl.BlockSpec(memory_space=pltpu.VMEM)
    return pl.pallas_call(
        body, name="reduce_small",
        in_specs=[vm] * (n + 3), out_specs=[vm] * 4,
        out_shape=[jax.ShapeDtypeStruct((8, D_MODEL), F32)] * 4,
        scratch_shapes=[pltpu.VMEM((8, D_MODEL), F32), pltpu.VMEM((n_dev, 8, D_MODEL), F32),
                        pltpu.SemaphoreType.DMA((len(flips),)), pltpu.SemaphoreType.DMA((n_dev,))],
    )(*packs, w, m, v)


def _pack_small(attn_pre, gamma, hg_norm, sb_norm, attn_post, ffn_pre, ffn_post):
    rows = [attn_pre, gamma.reshape(1, D_MODEL), jnp.concatenate([hg_norm, sb_norm], axis=1), attn_post, ffn_pre, ffn_post,
            jnp.zeros((2, D_MODEL), F32)]
    return jnp.concatenate(rows, axis=0)


def _unpack_small(pack):
    return (pack[ROW_ATTN_PRE:ROW_ATTN_PRE + 1], pack[ROW_GAMMA].reshape(2, HG_WIDTH),
            pack[ROW_MIX_NORMS:ROW_MIX_NORMS + 1, :HG_WIDTH], pack[ROW_MIX_NORMS:ROW_MIX_NORMS + 1, HG_WIDTH:],
            pack[ROW_ATTN_POST:ROW_ATTN_POST + 1], pack[ROW_FFN_PRE:ROW_FFN_PRE + 1], pack[ROW_FFN_POST:ROW_FFN_POST + 1])


def kernel(x, p, attn_pre_norm, w_in, hg_lower_gamma, hg_out_norm, sb_out_norm, w_out, attn_post_norm, ffn_pre_norm, w_gate_up, w_down, ffn_post_norm, ple_proj, ple_gate, loss_target, m_attn_pre_norm, m_w_in, m_hg_lower_gamma, m_hg_out_norm, m_sb_out_norm, m_w_out, m_attn_post_norm, m_ffn_pre_norm, m_w_gate_up, m_w_down, m_ffn_post_norm, m_ple_proj, m_ple_gate, v_attn_pre_norm, v_w_in, v_hg_lower_gamma, v_hg_out_norm, v_sb_out_norm, v_w_out, v_attn_post_norm, v_ffn_pre_norm, v_w_gate_up, v_w_down, v_ffn_post_norm, v_ple_proj, v_ple_gate):
    x2 = x[0]
    p2 = p[0, 0]
    target = loss_target[0]
    big = dict(w_in=(w_in, m_w_in, v_w_in), w_out=(w_out, m_w_out, v_w_out), w_gate_up=(w_gate_up, m_w_gate_up, v_w_gate_up),
               w_down=(w_down, m_w_down, v_w_down), ple_proj=(ple_proj, m_ple_proj, v_ple_proj),
               ple_gate=(ple_gate, m_ple_gate, v_ple_gate))
    names = list(big)
    big = {k: tuple(a[0] for a in t) for k, t in big.items()}

    full = dict(zip(names, gather_weights([big[k][0].astype(BF16) for k in names])))
    w_out_full = full["w_out"].reshape(D_MODEL, D_MODEL)
    w_down_full = full["w_down"].reshape(D_FF, D_MODEL)
    w_pg_full = full["ple_gate"].reshape(D_MODEL, D_MODEL)
    mix_norms = jnp.concatenate([hg_out_norm, sb_out_norm], axis=1)

    proj_h, sqkv, u1 = in_proj_fwd(x2, attn_pre_norm, full["w_in"])
    o_hg, states = hgrn2_fwd(proj_h, hg_lower_gamma)
    o_sb, sb_totals, sb_first = sb_fwd(sqkv)
    cat, mix, h1 = mix_out_fwd(o_hg, proj_h, o_sb, x2, mix_norms, attn_post_norm, w_out_full)
    u2, gu, act, y, h2 = ffn_fwd(h1, ffn_pre_norm, ffn_post_norm, full["w_gate_up"], w_down_full)

    de, ds, dh2, h2b, pb, pack_loss = ple_loss(h2, p2, target, full["ple_proj"], w_pg_full)
    dy, dgu, dh1, pack_ffn = ffn_bwd(dh2, y, h1, gu, ffn_pre_norm, ffn_post_norm, full["w_gate_up"], w_down_full)
    dmix, do_hg, dhg, do_sb, pack_mix = mix_out_bwd(dh1, mix, o_hg, proj_h, o_sb, mix_norms, attn_post_norm, w_out_full)
    dsq, dsk, dsv = sb_bwd(sqkv, do_sb, sb_totals, sb_first)
    dhq, dhf, dhi, pack_hg = hgrn2_bwd(proj_h, hg_lower_gamma, states, do_hg)
    dproj, grad_x, pack_in = in_proj_bwd([dhq, dhf, dhi, dhg, dsq, dsk, dsv], x2, dh1, attn_pre_norm, full["w_in"])

    local = dict(
        w_in=weight_grad(u1, dproj, "grad_w_in", tm=D_MODEL, tn=full["w_in"].shape[2], col_pieces=True),
        w_out=weight_grad(cat, dmix, "grad_w_out", tm=256, tn=D_MODEL).reshape(full["w_out"].shape),
        w_gate_up=weight_grad(u2, dgu, "grad_w_gate_up", tm=D_MODEL, tn=full["w_gate_up"].shape[2], col_pieces=True),
        w_down=weight_grad(act, dy, "grad_w_down", tm=D_FF // 2, tn=D_MODEL).reshape(full["w_down"].shape),
        ple_proj=weight_grad(pb, de, "grad_ple_proj", tm=pb.shape[1], tn=full["ple_proj"].shape[2], col_pieces=True),
        ple_gate=weight_grad(h2b, ds, "grad_ple_gate", tm=256, tn=D_MODEL).reshape(full["ple_gate"].shape),
    )

    core = lax.axis_index("c").astype(jnp.int32).reshape(1)
    got = exchange_core_halves([local[k] for k in names])
    partial = [add_core_halves(local[k], g, core, "add_core_halves_" + k) for k, g in zip(names, got)]
    by_source = exchange_chip_partials(partial)
    halves = [add_chip_partials(s, core, "add_chip_partials_" + k) for k, s in zip(names, by_source)]
    grads = dict(zip(names, join_core_halves(halves)))

    upd = {k: adamw(big[k][0], grads[k], big[k][1], big[k][2], "adamw_" + k) for k in names}

    small = reduce_small(
        [pack_loss, pack_ffn, pack_mix, pack_hg, pack_in],
        _pack_small(attn_pre_norm, hg_lower_gamma, hg_out_norm, sb_out_norm, attn_post_norm, ffn_pre_norm, ffn_post_norm),
        _pack_small(m_attn_pre_norm, m_hg_lower_gamma, m_hg_out_norm, m_sb_out_norm, m_attn_post_norm, m_ffn_pre_norm, m_ffn_post_norm),
        _pack_small(v_attn_pre_norm, v_hg_lower_gamma, v_hg_out_norm, v_sb_out_norm, v_attn_post_norm, v_ffn_pre_norm, v_ffn_post_norm),
    )
    loss = small[0][ROW_LOSS, 0]
    s_grad, s_delta, s_m, s_v = (_unpack_small(t) for t in small)

    def ordered(small_vals, big_vals):
        a_pre, gam, hg_n, sb_n, a_post, f_pre, f_post = small_vals
        b = {k: big_vals[k][None] for k in names}
        return (a_pre, b["w_in"], gam, hg_n, sb_n, b["w_out"], a_post, f_pre, b["w_gate_up"], b["w_down"], f_post,
                b["ple_proj"], b["ple_gate"])

    return (loss, grad_x[None],
            *ordered(s_grad, grads),
            *ordered(s_delta, {k: upd[k][0] for k in names}),
            *ordered(s_m, {k: upd[k][1] for k in names}),
            *ordered(s_v, {k: upd[k][2] for k in names}))
```

```python
import functools

import numpy as np
import jax
import jax.numpy as jnp
from jax import lax
from jax.experimental import pallas as pl
from jax.experimental.pallas import tpu as pltpu

F32 = jnp.float32
BF16 = jnp.bfloat16
MESH = pl.DeviceIdType.MESH

RMS_EPS = 1e-6
D_MODEL = 1024
HG_WIDTH = 512
HG_HEADS = 4
HG_DK = 128
HG_CHUNK = 64
HG_LEVELS = (32, 16, 8, 4, 2, 1)
SB_WIDTH = 512
SB_BLOCK = 128
SB_DH = 64
SB_SCALE = SB_DH ** -0.5
SB_UNDERFLOW_LOG = -90.0
SB_UNROLL = 2
D_FF = 2816
N_CHIPS = 4
ROW_TILE = 256
V7X_VMEM_LIMIT = 56 * 1024 * 1024

ADAM_LR = 0.001
ADAM_B1 = 0.9
ADAM_B2 = 0.999
ADAM_EPS = 1e-08
ADAM_WD = 0.01
ADAM_STEP = 10

ROW_ATTN_PRE, ROW_GAMMA, ROW_MIX_NORMS, ROW_ATTN_POST, ROW_FFN_PRE, ROW_FFN_POST, ROW_LOSS = range(7)


def _params(sem=None, vmem=V7X_VMEM_LIMIT):
    return pltpu.CompilerParams(dimension_semantics=sem, vmem_limit_bytes=vmem)


def _dot(a, b):
    return jnp.dot(a.astype(BF16), b.astype(BF16), preferred_element_type=F32)


def _dot_nt(a, b):
    return lax.dot_general(a.astype(BF16), b.astype(BF16), (((1,), (1,)), ((), ())), preferred_element_type=F32)


def _dot_tn(a, b):
    return lax.dot_general(a.astype(BF16), b.astype(BF16), (((0,), (0,)), ((), ())), preferred_element_type=F32)


def _split(x):
    hi = x.astype(BF16)
    lo = (x - hi.astype(F32)).astype(BF16)
    return hi, lo


def _sum01_left(m01, x):
    hi, lo = _split(x)
    return jnp.dot(m01, hi, preferred_element_type=F32) + jnp.dot(m01, lo, preferred_element_type=F32)


def _sum01_right(x, m01):
    hi, lo = _split(x)
    return jnp.dot(hi, m01, preferred_element_type=F32) + jnp.dot(lo, m01, preferred_element_type=F32)


def _rms(x):
    r = lax.rsqrt(jnp.mean(x * x, axis=-1, keepdims=True) + RMS_EPS)
    return x * r, r


def _rms_bwd(dy, xhat, r, w):
    dxh = dy * w
    dx = r * (dxh - xhat * jnp.mean(dxh * xhat, axis=-1, keepdims=True))
    return dx, dy * xhat


def _sigmoid(x):
    return 1.0 / (1.0 + jnp.exp(-x))


def _neg_softplus(z):
    return -(jnp.maximum(z, 0.0) + jnp.log(1.0 + jnp.exp(-jnp.abs(z))))


def _colsum(x):
    return jnp.sum(x, axis=0, keepdims=True)


def _load_once(src_hbm, dst_vmem):
    @pl.when(pl.program_id(0) == 0)
    def _():
        pltpu.sync_copy(src_hbm, dst_vmem)


def _zero_first(ref):
    @pl.when(pl.program_id(0) == 0)
    def _():
        ref[...] = jnp.zeros(ref.shape, ref.dtype)


def _row_spec(width, col=0):
    return pl.BlockSpec((ROW_TILE, width), lambda i, col=col: (i, col))


def _full_spec(shape):
    return pl.BlockSpec(shape, lambda *_: (0,) * len(shape))


ANY_SPEC = pl.BlockSpec(memory_space=pl.ANY)
PACK_SPEC = _full_spec((8, D_MODEL))


def in_proj_fwd(x, g_pre, w_in):
    T = x.shape[0]
    pw = w_in.shape[2]

    def body(x_ref, g_ref, w_hbm, ph_ref, sqkv_ref, u_ref, w_vmem, proj_s):
        _load_once(w_hbm, w_vmem)
        xh, _ = _rms(x_ref[...])
        u = (xh * g_ref[...]).astype(BF16)
        u_ref[...] = u
        for q in range(N_CHIPS):
            proj_s[:, pw * q:pw * (q + 1)] = jnp.dot(u, w_vmem[q], preferred_element_type=F32)
        ph_ref[...] = proj_s[:, :4 * HG_WIDTH]
        sqkv_ref[:, :SB_WIDTH] = (proj_s[:, 4 * HG_WIDTH:4 * HG_WIDTH + SB_WIDTH] * SB_SCALE).astype(BF16)
        sqkv_ref[:, SB_WIDTH:] = proj_s[:, 4 * HG_WIDTH + SB_WIDTH:].astype(BF16)

    return pl.pallas_call(
        body, name="in_proj_fwd", grid=(T // ROW_TILE,),
        in_specs=[_row_spec(D_MODEL), _full_spec((1, D_MODEL)), ANY_SPEC],
        out_specs=[_row_spec(4 * HG_WIDTH), _row_spec(3 * SB_WIDTH), _row_spec(D_MODEL)],
        out_shape=[jax.ShapeDtypeStruct((T, 4 * HG_WIDTH), F32), jax.ShapeDtypeStruct((T, 3 * SB_WIDTH), BF16),
                   jax.ShapeDtypeStruct((T, D_MODEL), BF16)],
        scratch_shapes=[pltpu.VMEM(w_in.shape, BF16), pltpu.VMEM((ROW_TILE, N_CHIPS * pw), F32)],
        compiler_params=_params(("arbitrary",)),
    )(x, g_pre, w_in)


def _hg_sum_matrix():
    C = HG_CHUNK
    t = np.arange(C)[:, None]
    j = np.arange(C)[None, :]
    mats = [j <= t, j > t]
    for h in HG_LEVELS:
        start = (t // (2 * h)) * (2 * h)
        upper = (t & h) != 0
        mats.append(np.where(upper, (j >= start + h) & (j <= t), (j > t) & (j <= start + h - 1)))
    return np.concatenate(mats, 0).astype(np.float32)


def _hg_level_masks():
    C = HG_CHUNK
    t = lax.broadcasted_iota(jnp.int32, (C, C), 0)
    s = lax.broadcasted_iota(jnp.int32, (C, C), 1)
    x = t ^ s
    masks = [t == s]
    for h in HG_LEVELS:
        masks.append((x >= h) & (x < 2 * h) & (t > s))
    return masks


def _hg_gates(hq, hf, gamma):
    lb = 1.0 / (1.0 + jnp.exp(gamma[1:2, :] - gamma[0:1, :]))
    sq = _sigmoid(hq)
    q = hq * sq
    sig = _sigmoid(hf)
    nsig = _sigmoid(-hf)
    f = lb + (1.0 - lb) * sig
    k = (1.0 - lb) * nsig
    g = jnp.log(f)
    return q, k, g, dict(lb=lb, sq=sq, sig=sig, nsig=nsig, f=f)


def _hg_head_decays(A, h):
    C, K = HG_CHUNK, HG_DK
    sl = slice(K * h, K * (h + 1))
    blocks = [A[C * r:C * (r + 1), sl] for r in range(2 + len(HG_LEVELS))]
    return blocks[0], blocks[1], [None] + blocks[2:]


def _hg_products(q, k, levels):
    return [_dot_nt(q, k)] + [_dot_nt(q * a, k * a) for a in levels[1:]]


def _hg_select(prods, masks):
    sc = jnp.where(masks[0], prods[0], 0.0)
    for p, m in zip(prods[1:], masks[1:]):
        sc = jnp.where(m, p, sc)
    return sc


def hgrn2_fwd(proj_h, gamma):
    T = proj_h.shape[0]
    C, K, H = HG_CHUNK, HG_DK, HG_HEADS
    n_chunks = T // C
    msum = jnp.asarray(_hg_sum_matrix(), BF16)

    def body(hq_ref, hf_ref, hi_ref, gam_ref, msum_ref, o_ref, st_ref, st_s):
        _zero_first(st_s)
        q, k, g, _ = _hg_gates(hq_ref[...], hf_ref[...], gam_ref[...])
        v = hi_ref[...]
        A = jnp.exp(_sum01_left(msum_ref[...], g))
        masks = _hg_level_masks()
        heads = []
        for h in range(H):
            sl = slice(K * h, K * (h + 1))
            ab, ar, levels = _hg_head_decays(A, h)
            heads.append((sl, ab, ar, levels, q[:, sl], k[:, sl], v[:, sl], st_s[h]))
        prods = [_hg_products(qh, kh, levels) for _, _, _, levels, qh, kh, _, _ in heads]
        inter = [_dot_nt(qh * ab, st) for _, ab, _, _, qh, _, _, st in heads]
        grown = [_dot_tn(vh, kh * ar) for _, _, ar, _, _, kh, vh, _ in heads]
        scores = [_hg_select(p, masks) for p in prods]
        for (sl, ab, _, _, _, _, vh, st), sc, o_inter, st_add in zip(heads, scores, inter, grown):
            o_ref[:, sl] = o_inter + _dot(sc, vh)
            st_new = st * ab[C - 1:C, :] + st_add
            st_s[sl.start // K] = st_new
            st_ref[0, sl.start // K] = st_new

    blk = lambda col: pl.BlockSpec((C, HG_WIDTH), lambda c, col=col: (c, col))
    return pl.pallas_call(
        body, name="hgrn2_fwd", grid=(n_chunks,),
        in_specs=[blk(0), blk(1), blk(2), _full_spec((2, HG_WIDTH)), _full_spec(msum.shape)],
        out_specs=[blk(0), pl.BlockSpec((1, H, K, K), lambda c: (c, 0, 0, 0))],
        out_shape=[jax.ShapeDtypeStruct((T, HG_WIDTH), F32), jax.ShapeDtypeStruct((n_chunks, H, K, K), F32)],
        scratch_shapes=[pltpu.VMEM((H, K, K), F32)],
        compiler_params=_params(("arbitrary",)),
    )(proj_h, proj_h, proj_h, gamma, msum)


def hgrn2_bwd(proj_h, gamma, states, do):
    T = proj_h.shape[0]
    C, K, H = HG_CHUNK, HG_DK, HG_HEADS
    n_chunks = T // C
    n_sums = 2 + len(HG_LEVELS)
    msum = jnp.asarray(_hg_sum_matrix(), BF16)
    msum_t = jnp.asarray(_hg_sum_matrix().T, BF16)

    def body(hq_ref, hf_ref, hi_ref, do_ref, gam_ref, msum_ref, msum_t_ref, st_in_ref,
             dhq_ref, dhf_ref, dhi_ref, pack_ref, dst_s, dlb_s, dq_s, dk_s, de_s):
        step = pl.program_id(0)
        _zero_first(dst_s)
        _zero_first(dlb_s)
        _zero_first(pack_ref)
        hq = hq_ref[...]
        q, k, g, aux = _hg_gates(hq, hf_ref[...], gam_ref[...])
        v = hi_ref[...]
        do_all = do_ref[...]
        A = jnp.exp(_sum01_left(msum_ref[...], g))
        masks = _hg_level_masks()
        is_last_row = lax.broadcasted_iota(jnp.int32, (C, K), 0) == C - 1
        has_prev = (step < n_chunks - 1).astype(F32)
        heads = []
        for h in range(H):
            sl = slice(K * h, K * (h + 1))
            ab, ar, levels = _hg_head_decays(A, h)
            heads.append(dict(h=h, sl=sl, ab=ab, ar=ar, levels=levels, q=q[:, sl], k=k[:, sl], v=v[:, sl],
                              do=do_all[:, sl], st_in=st_in_ref[0, h] * has_prev, dst_out=dst_s[h]))
        for hd in heads:
            hd["prods"] = _hg_products(hd["q"], hd["k"], hd["levels"])
            hd["da"] = _dot_nt(hd["do"], hd["v"])
            hd["t1"] = hd["ab"] * _dot(hd["do"], hd["st_in"])
            hd["t2"] = hd["ar"] * _dot(hd["v"], hd["dst_out"])
            hd["dv_state"] = _dot_nt(hd["k"] * hd["ar"], hd["dst_out"])
            hd["dst_add"] = _dot_tn(hd["do"], hd["q"] * hd["ab"])
        for hd in heads:
            hd["sc"] = _hg_select(hd["prods"], masks)
            hd["dam"] = [jnp.where(m, hd["da"], 0.0) for m in masks]
        for hd in heads:
            qh, kh = hd["q"], hd["k"]
            hd["dq_parts"] = [_dot(hd["dam"][0], kh)] + [
                a * _dot(dam, kh * a) for a, dam in zip(hd["levels"][1:], hd["dam"][1:])]
            hd["dk_parts"] = [_dot_tn(hd["dam"][0], qh)] + [
                a * _dot_tn(dam, qh * a) for a, dam in zip(hd["levels"][1:], hd["dam"][1:])]
            hd["dv_intra"] = _dot_tn(hd["sc"], hd["do"])
        for hd in heads:
            h, sl, qh, kh, ab = hd["h"], hd["sl"], hd["q"], hd["k"], hd["ab"]
            decayed = _colsum(hd["st_in"] * hd["dst_out"]) * ab[C - 1:C, :]
            de_s[0:C, sl] = qh * hd["t1"] + jnp.where(is_last_row, decayed, 0.0)
            de_s[C:2 * C, sl] = kh * hd["t2"]
            dq = hd["t1"] + hd["dq_parts"][0]
            dk = hd["t2"] + hd["dk_parts"][0]
            for r, (t1, t2) in enumerate(zip(hd["dq_parts"][1:], hd["dk_parts"][1:])):
                dq = dq + t1
                dk = dk + t2
                de_s[C * (r + 2):C * (r + 3), sl] = qh * t1 + kh * t2
            dhi_ref[:, sl] = hd["dv_intra"] + hd["dv_state"]
            dst_s[h] = hd["dst_out"] * ab[C - 1:C, :] + hd["dst_add"]
            dq_s[:, sl] = dq
            dk_s[:, sl] = dk
        dg = _sum01_left(msum_t_ref[...], de_s[...])
        dk = dk_s[...]
        sq, lb = aux["sq"], aux["lb"]
        dhq_ref[...] = dq_s[...] * (sq * (1.0 + hq * (1.0 - sq)))
        common = dg / aux["f"] - dk
        dhf_ref[...] = (1.0 - lb) * aux["sig"] * aux["nsig"] * common
        dlb_s[...] += _colsum(aux["nsig"] * common)

        @pl.when(step == n_chunks - 1)
        def _():
            dgam = lb * (1.0 - lb) * dlb_s[...]
            pack_ref[ROW_GAMMA:ROW_GAMMA + 1, :HG_WIDTH] = dgam
            pack_ref[ROW_GAMMA:ROW_GAMMA + 1, HG_WIDTH:] = -dgam

    last = n_chunks - 1
    blk = lambda col: pl.BlockSpec((C, HG_WIDTH), lambda c, col=col: (last - c, col))
    return pl.pallas_call(
        body, name="hgrn2_bwd", grid=(n_chunks,),
        in_specs=[blk(0), blk(1), blk(2), blk(0), _full_spec((2, HG_WIDTH)), _full_spec(msum.shape),
                  _full_spec(msum_t.shape),
                  pl.BlockSpec((1, H, K, K), lambda c: (jnp.maximum(last - c - 1, 0), 0, 0, 0))],
        out_specs=[blk(0), blk(0), blk(0), PACK_SPEC],
        out_shape=[jax.ShapeDtypeStruct((T, HG_WIDTH), F32)] * 3 + [jax.ShapeDtypeStruct((8, D_MODEL), F32)],
        scratch_shapes=[pltpu.VMEM((H, K, K), F32), pltpu.VMEM((1, HG_WIDTH), F32), pltpu.VMEM((C, HG_WIDTH), F32),
                        pltpu.VMEM((C, HG_WIDTH), F32), pltpu.VMEM((n_sums * C, HG_WIDTH), F32)],
        compiler_params=_params(("arbitrary",)),
    )(proj_h, proj_h, proj_h, do, gamma, msum, msum_t, states)


def _sb_sum_matrix(inclusive):
    B = SB_BLOCK
    j = np.arange(B)[:, None]
    s = np.arange(B)[None, :]
    tri = (j >= s) if inclusive else (j > s)
    return np.concatenate([tri, np.ones((B, B), bool)], 1).astype(np.float32)


def _sb_prefix_matrix(inclusive):
    B = SB_BLOCK
    j = np.arange(B)[:, None]
    s = np.arange(B)[None, :]
    tri = (j <= s) if inclusive else (j < s)
    return np.concatenate([tri, np.ones((B, B), bool)], 1).astype(np.float32)


def _sb_iotas():
    shape = (SB_BLOCK, SB_BLOCK)
    return lax.broadcasted_iota(jnp.int32, shape, 0), lax.broadcasted_iota(jnp.int32, shape, 1)


def _sb_weights(qh, kj, valid, carry, usum):
    z = _dot_nt(qh, kj)
    lnb = jnp.where(valid, _neg_softplus(z), 0.0)
    lb = z + lnb
    sums = _sum01_right(lnb, usum)
    a = jnp.where(valid, jnp.exp(lb + carry + sums[:, :SB_BLOCK]), 0.0)
    return a, lb, carry + sums[:, SB_BLOCK:]


def sb_fwd(sqkv):
    T = sqkv.shape[0]
    B = SB_BLOCK
    pairs = SB_WIDTH // B
    usum = jnp.asarray(_sb_sum_matrix(False), BF16)

    def body(q_ref, k_ref, v_ref, u_ref, o_ref, tl_ref, first_ref):
        p, i = pl.program_id(0), pl.program_id(1)
        row, lane = _sb_iotas()
        first = lane < SB_DH
        q = q_ref[...]
        heads = (jnp.where(first, q, jnp.zeros_like(q)), jnp.where(first, jnp.zeros_like(q), q))
        u = u_ref[...]

        def more(loop):
            n, reachable, _ = loop
            return (SB_UNROLL * n <= i) & (reachable > 0)

        def step(loop):
            n, _, state = loop
            blocks = []
            for sub in range(SB_UNROLL):
                j = i - SB_UNROLL * n - sub
                off = pl.multiple_of(jnp.maximum(j, 0) * B, B)
                valid = ((lane + j * B) < (row + i * B)) & (j >= 0)
                blocks.append((k_ref[pl.ds(off, B), :], v_ref[pl.ds(off, B), :], valid))
            z = [[_dot_nt(qh, kj) for qh in heads] for kj, _, _ in blocks]
            lnb = [[jnp.where(valid, _neg_softplus(zz), 0.0) for zz in zs] for zs, (_, _, valid) in zip(z, blocks)]
            sums = [[_sum01_right(x, u) for x in xs] for xs in lnb]
            out = []
            for h, (carry, acc) in enumerate(state):
                for sub, (_, vj, valid) in enumerate(blocks):
                    expo = z[sub][h] + lnb[sub][h] + carry + sums[sub][h][:, :B]
                    acc = acc + _dot(jnp.where(valid, jnp.exp(expo), 0.0), vj)
                    carry = carry + sums[sub][h][:, B:]
                out.append((carry, acc))
            state = tuple(out)
            reachable = (jnp.max(jnp.maximum(state[0][0], state[1][0])) > SB_UNDERFLOW_LOG).astype(jnp.int32)
            return n + 1, reachable, state

        zero = jnp.zeros((B, B), F32)
        done, _, ((tot0, acc0), (tot1, acc1)) = lax.while_loop(
            more, step, (jnp.int32(0), jnp.int32(1), ((zero, zero), (zero, zero))))
        o_ref[...] = jnp.where(first, acc0, acc1)
        tl_ref[...] = jnp.where(first, tot0, tot1)
        first_ref[p, i] = jnp.maximum(i + 1 - SB_UNROLL * done, 0)

    return pl.pallas_call(
        body, name="sb_fwd", grid=(pairs, T // B),
        in_specs=[pl.BlockSpec((B, B), lambda p, i: (i, p)),
                  pl.BlockSpec((T, B), lambda p, i: (0, pairs + p)),
                  pl.BlockSpec((T, B), lambda p, i: (0, 2 * pairs + p)),
                  pl.BlockSpec(usum.shape, lambda p, i: (0, 0))],
        out_specs=[pl.BlockSpec((B, B), lambda p, i: (i, p))] * 2 + [pl.BlockSpec(memory_space=pltpu.SMEM)],
        out_shape=[jax.ShapeDtypeStruct((T, SB_WIDTH), F32)] * 2 + [jax.ShapeDtypeStruct((pairs, T // B), jnp.int32)],
        compiler_params=_params(("arbitrary", "arbitrary")),
    )(sqkv, sqkv, sqkv, usum)


def sb_bwd(sqkv, do, tl, first_block):
    T = sqkv.shape[0]
    B = SB_BLOCK
    pairs = SB_WIDTH // B
    upre = jnp.asarray(_sb_prefix_matrix(True), BF16)
    uexc = jnp.asarray(_sb_prefix_matrix(False), BF16)

    def body(q_ref, k_ref, v_ref, do_ref, tl_ref, up_ref, ue_ref, first_ref, dq_ref, dk_ref, dv_ref):
        p, i = pl.program_id(0), pl.program_id(1)

        @pl.when(i == 0)
        def _():
            dk_ref[...] = jnp.zeros(dk_ref.shape, F32)
            dv_ref[...] = jnp.zeros(dv_ref.shape, F32)

        row, lane = _sb_iotas()
        first = lane < SB_DH
        q = q_ref[...]
        do = do_ref[...]
        tl_all = tl_ref[...]
        zb = jnp.zeros_like(q)
        zf = jnp.zeros_like(do)
        heads = (
            (jnp.where(first, q, zb), jnp.where(first, do, zf).astype(BF16), tl_all[:, 0:1]),
            (jnp.where(first, zb, q), jnp.where(first, zf, do).astype(BF16), tl_all[:, B - 1:B]),
        )
        up = up_ref[...]
        ue = ue_ref[...]

        start = first_ref[p, i]

        def step(n, state):
            blocks = []
            for sub in range(SB_UNROLL):
                j = start + SB_UNROLL * n + sub
                off = pl.multiple_of(jnp.minimum(j, i) * B, B)
                valid = (lane + j * B) < (row + i * B)
                blocks.append((off, k_ref[pl.ds(off, B), :], v_ref[pl.ds(off, B), :], valid))
            combos = [(s, h) for s in range(SB_UNROLL) for h in range(len(heads))]
            z = {(s, h): _dot_nt(heads[h][0], blocks[s][1]) for s, h in combos}
            da = {(s, h): _dot_nt(heads[h][1], blocks[s][2]) for s, h in combos}
            lnb = {c: jnp.where(blocks[c[0]][3], _neg_softplus(z[c]), 0.0) for c in combos}
            lb = {c: z[c] + lnb[c] for c in combos}
            sums = {c: _sum01_right(lnb[c], up) for c in combos}
            a, w = {}, {}
            seen = [st[0] for st in state]
            for s, h in combos:
                expo = lb[s, h] + (heads[h][2] - seen[h] - sums[s, h][:, :B])
                a[s, h] = jnp.where(blocks[s][3], jnp.exp(expo), 0.0)
                w[s, h] = a[s, h] * da[s, h]
                seen[h] = seen[h] + sums[s, h][:, B:]
            wsums = {c: _sum01_right(w[c], ue) for c in combos}
            dz = {}
            seen_w = [st[1] for st in state]
            for s, h in combos:
                beta = jnp.exp(lb[s, h])
                before = seen_w[h] + wsums[s, h][:, :B]
                dz[s, h] = jnp.where(blocks[s][3], w[s, h] * (1.0 - beta) - before * beta, 0.0)
                seen_w[h] = seen_w[h] + wsums[s, h][:, B:]
            dq = [st[2] for st in state]
            for s, h in combos:
                dq[h] = dq[h] + _dot(dz[s, h], blocks[s][1])
            for s in range(SB_UNROLL):
                off = blocks[s][0]
                dk_add = _dot_tn(dz[s, 0], heads[0][0])
                dv_add = _dot_tn(a[s, 0], heads[0][1])
                for h in range(1, len(heads)):
                    dk_add = dk_add + _dot_tn(dz[s, h], heads[h][0])
                    dv_add = dv_add + _dot_tn(a[s, h], heads[h][1])
                dk_ref[pl.ds(off, B), :] += dk_add
                dv_ref[pl.ds(off, B), :] += dv_add
            return tuple(zip(seen, seen_w, dq))

        zero = jnp.zeros((B, B), F32)
        trips = (i - start + SB_UNROLL) // SB_UNROLL
        (_, _, dq0), (_, _, dq1) = lax.fori_loop(0, trips, step, ((zero, zero, zero), (zero, zero, zero)))
        dq_ref[...] = jnp.where(first, dq0, dq1) * SB_SCALE

    qblk = pl.BlockSpec((B, B), lambda p, i: (i, p))
    full = pl.BlockSpec((T, B), lambda p, i: (0, p))
    return pl.pallas_call(
        body, name="sb_bwd", grid=(pairs, T // B),
        in_specs=[qblk, pl.BlockSpec((T, B), lambda p, i: (0, pairs + p)),
                  pl.BlockSpec((T, B), lambda p, i: (0, 2 * pairs + p)), qblk, qblk,
                  pl.BlockSpec(upre.shape, lambda p, i: (0, 0)), pl.BlockSpec(uexc.shape, lambda p, i: (0, 0)),
                  pl.BlockSpec(memory_space=pltpu.SMEM)],
        out_specs=[qblk, full, full],
        out_shape=[jax.ShapeDtypeStruct((T, SB_WIDTH), F32)] * 3,
        compiler_params=_params(("arbitrary", "arbitrary")),
    )(sqkv, sqkv, sqkv, do, tl, upre, uexc, first_block)


def _mixer_out(o_hg, hg, o_sb, g_hg, g_sb):
    n_hg, r_hg = _rms(o_hg)
    s_hg = _sigmoid(hg)
    n_sb, r_sb = _rms(o_sb)
    return dict(n_hg=n_hg, r_hg=r_hg, s_hg=s_hg, n_sb=n_sb, r_sb=r_sb,
                y_hg=n_hg * g_hg * (hg * s_hg), y_sb=n_sb * g_sb)


def mix_out_fwd(o_hg, proj_h, o_sb, x, norms, g_post, w_out):
    T = x.shape[0]

    def body(ohg_ref, hg_ref, osb_ref, x_ref, nrm_ref, gp_ref, w_hbm, cat_ref, mix_ref, h1_ref, w_vmem):
        _load_once(w_hbm, w_vmem)
        nrm = nrm_ref[...]
        m = _mixer_out(ohg_ref[...], hg_ref[...], osb_ref[...], nrm[:, :HG_WIDTH], nrm[:, HG_WIDTH:])
        cat_ref[:, :HG_WIDTH] = m["y_hg"].astype(BF16)
        cat_ref[:, HG_WIDTH:] = m["y_sb"].astype(BF16)
        mix = jnp.dot(cat_ref[...], w_vmem[...], preferred_element_type=F32)
        mix_ref[...] = mix
        mh, _ = _rms(mix)
        h1_ref[...] = x_ref[...] + mh * gp_ref[...]

    return pl.pallas_call(
        body, name="mix_out_fwd", grid=(T // ROW_TILE,),
        in_specs=[_row_spec(HG_WIDTH), _row_spec(HG_WIDTH, 3), _row_spec(SB_WIDTH), _row_spec(D_MODEL),
                  _full_spec((1, D_MODEL)), _full_spec((1, D_MODEL)), ANY_SPEC],
        out_specs=[_row_spec(D_MODEL)] * 3,
        out_shape=[jax.ShapeDtypeStruct((T, D_MODEL), BF16), jax.ShapeDtypeStruct((T, D_MODEL), F32),
                   jax.ShapeDtypeStruct((T, D_MODEL), F32)],
        scratch_shapes=[pltpu.VMEM(w_out.shape, BF16)],
        compiler_params=_params(("arbitrary",)),
    )(o_hg, proj_h, o_sb, x, norms, g_post, w_out)


def ffn_fwd(h1, g_pre, g_post, w_gu, w_down):
    T = h1.shape[0]
    pw = w_gu.shape[2]

    def body(h1_ref, gpre_ref, gpost_ref, wgu_hbm, wd_hbm, u2_ref, gu_ref, act_ref, y_ref, h2_ref,
             wgu_vmem, wd_vmem, gu_s):
        _load_once(wgu_hbm, wgu_vmem)
        _load_once(wd_hbm, wd_vmem)
        h1v = h1_ref[...]
        hh, _ = _rms(h1v)
        u2 = (hh * gpre_ref[...]).astype(BF16)
        u2_ref[...] = u2
        for q in range(N_CHIPS):
            gu_s[:, pw * q:pw * (q + 1)] = jnp.dot(u2, wgu_vmem[q], preferred_element_type=F32)
        gu_ref[...] = gu_s[...].astype(BF16)
        gate = gu_s[:, :D_FF]
        act = (gate * _sigmoid(gate) * gu_s[:, D_FF:]).astype(BF16)
        act_ref[...] = act
        y = jnp.dot(act, wd_vmem[...], preferred_element_type=F32)
        y_ref[...] = y
        yh, _ = _rms(y)
        h2_ref[...] = h1v + yh * gpost_ref[...]

    return pl.pallas_call(
        body, name="ffn_fwd", grid=(T // ROW_TILE,),
        in_specs=[_row_spec(D_MODEL), _full_spec((1, D_MODEL)), _full_spec((1, D_MODEL)), ANY_SPEC, ANY_SPEC],
        out_specs=[_row_spec(D_MODEL), _row_spec(2 * D_FF), _row_spec(D_FF), _row_spec(D_MODEL), _row_spec(D_MODEL)],
        out_shape=[jax.ShapeDtypeStruct((T, D_MODEL), BF16), jax.ShapeDtypeStruct((T, 2 * D_FF), BF16),
                   jax.ShapeDtypeStruct((T, D_FF), BF16), jax.ShapeDtypeStruct((T, D_MODEL), F32),
                   jax.ShapeDtypeStruct((T, D_MODEL), F32)],
        scratch_shapes=[pltpu.VMEM(w_gu.shape, BF16), pltpu.VMEM(w_down.shape, BF16),
                        pltpu.VMEM((ROW_TILE, 2 * D_FF), F32)],
        compiler_params=_params(("arbitrary",)),
    )(h1, g_pre, g_post, w_gu, w_down)


def ple_loss(h2, p, target, w_ple, w_pg):
    T = h2.shape[0]
    pw = w_ple.shape[2]

    def body(h2_ref, p_ref, t_ref, wple_hbm, wpg_hbm, de_ref, ds_ref, dh2_ref, h2b_ref, pb_ref, pack_ref,
             wple_vmem, wpg_vmem, e_s):
        _load_once(wple_hbm, wple_vmem)
        _load_once(wpg_hbm, wpg_vmem)
        _zero_first(pack_ref)
        h2v = h2_ref[...]
        h2b = h2v.astype(BF16)
        h2b_ref[...] = h2b
        pb = p_ref[...].astype(BF16)
        pb_ref[...] = pb
        for q in range(N_CHIPS):
            e_s[:, pw * q:pw * (q + 1)] = jnp.dot(pb, wple_vmem[q], preferred_element_type=F32)
        e = e_s[...]
        sig = _sigmoid(jnp.dot(h2b, wpg_vmem[...], preferred_element_type=F32))
        err = h2v + e * sig - t_ref[...]
        part = 0.5 * jnp.sum(jnp.mean(err * err, axis=-1, keepdims=True), axis=0, keepdims=True)
        lane = lax.broadcasted_iota(jnp.int32, (1, D_MODEL), 1)
        pack_ref[ROW_LOSS:ROW_LOSS + 1, :] += jnp.where(lane == 0, part, 0.0)
        dh3 = err * (1.0 / D_MODEL)
        de_ref[...] = (dh3 * sig).astype(BF16)
        ds = (dh3 * e * sig * (1.0 - sig)).astype(BF16)
        ds_ref[...] = ds
        dh2_ref[...] = dh3 + _dot_nt(ds, wpg_vmem[...])

    return pl.pallas_call(
        body, name="ple_loss", grid=(T // ROW_TILE,),
        in_specs=[_row_spec(D_MODEL), _row_spec(p.shape[1]), _row_spec(D_MODEL), ANY_SPEC, ANY_SPEC],
        out_specs=[_row_spec(D_MODEL), _row_spec(D_MODEL), _row_spec(D_MODEL), _row_spec(D_MODEL),
                   _row_spec(p.shape[1]), PACK_SPEC],
        out_shape=[jax.ShapeDtypeStruct((T, D_MODEL), BF16), jax.ShapeDtypeStruct((T, D_MODEL), BF16),
                   jax.ShapeDtypeStruct((T, D_MODEL), F32), jax.ShapeDtypeStruct((T, D_MODEL), BF16),
                   jax.ShapeDtypeStruct(p.shape, BF16), jax.ShapeDtypeStruct((8, D_MODEL), F32)],
        scratch_shapes=[pltpu.VMEM(w_ple.shape, BF16), pltpu.VMEM(w_pg.shape, BF16), pltpu.VMEM((ROW_TILE, D_MODEL), F32)],
        compiler_params=_params(("arbitrary",)),
    )(h2, p, target, w_ple, w_pg)


def ffn_bwd(dh2, y, h1, gu, g_pre, g_post, w_gu, w_down):
    T = h1.shape[0]
    pw = w_gu.shape[2]

    def body(dh2_ref, y_ref, h1_ref, gu_ref, gpre_ref, gpost_ref, wgu_hbm, wd_hbm, dy_ref, dgu_ref, dh1_ref, pack_ref,
             wgu_vmem, wd_vmem):
        _load_once(wgu_hbm, wgu_vmem)
        _load_once(wd_hbm, wd_vmem)
        _zero_first(pack_ref)
        dh2v = dh2_ref[...]
        yh, ry = _rms(y_ref[...])
        dy, dw = _rms_bwd(dh2v, yh, ry, gpost_ref[...])
        pack_ref[ROW_FFN_POST:ROW_FFN_POST + 1, :] += _colsum(dw)
        dyb = dy.astype(BF16)
        dy_ref[...] = dyb
        dact = _dot_nt(dyb, wd_vmem[...])
        gate = gu_ref[:, :D_FF].astype(F32)
        up = gu_ref[:, D_FF:].astype(F32)
        sg = _sigmoid(gate)
        dgu_ref[:, :D_FF] = (dact * up * (sg * (1.0 + gate * (1.0 - sg)))).astype(BF16)
        dgu_ref[:, D_FF:] = (dact * gate * sg).astype(BF16)
        du2 = _dot_nt(dgu_ref[:, :pw], wgu_vmem[0])
        for q in range(1, N_CHIPS):
            du2 = du2 + _dot_nt(dgu_ref[:, pw * q:pw * (q + 1)], wgu_vmem[q])
        hh, rh = _rms(h1_ref[...])
        dh, dw = _rms_bwd(du2, hh, rh, gpre_ref[...])
        pack_ref[ROW_FFN_PRE:ROW_FFN_PRE + 1, :] += _colsum(dw)
        dh1_ref[...] = dh2v + dh

    return pl.pallas_call(
        body, name="ffn_bwd", grid=(T // ROW_TILE,),
        in_specs=[_row_spec(D_MODEL), _row_spec(D_MODEL), _row_spec(D_MODEL), _row_spec(2 * D_FF),
                  _full_spec((1, D_MODEL)), _full_spec((1, D_MODEL)), ANY_SPEC, ANY_SPEC],
        out_specs=[_row_spec(D_MODEL), _row_spec(2 * D_FF), _row_spec(D_MODEL), PACK_SPEC],
        out_shape=[jax.ShapeDtypeStruct((T, D_MODEL), BF16), jax.ShapeDtypeStruct((T, 2 * D_FF), BF16),
                   jax.ShapeDtypeStruct((T, D_MODEL), F32), jax.ShapeDtypeStruct((8, D_MODEL), F32)],
        scratch_shapes=[pltpu.VMEM(w_gu.shape, BF16), pltpu.VMEM(w_down.shape, BF16)],
        compiler_params=_params(("arbitrary",)),
    )(dh2, y, h1, gu, g_pre, g_post, w_gu, w_down)


def mix_out_bwd(dh1, mix, o_hg, proj_h, o_sb, norms, g_post, w_out):
    T = dh1.shape[0]

    def body(dh1_ref, mix_ref, ohg_ref, hg_ref, osb_ref, nrm_ref, gp_ref, w_hbm, dmix_ref, dohg_ref, dhg_ref, dosb_ref,
             pack_ref, w_vmem):
        _load_once(w_hbm, w_vmem)
        _zero_first(pack_ref)
        mh, rm = _rms(mix_ref[...])
        dmix, dw = _rms_bwd(dh1_ref[...], mh, rm, gp_ref[...])
        pack_ref[ROW_ATTN_POST:ROW_ATTN_POST + 1, :] += _colsum(dw)
        dmb = dmix.astype(BF16)
        dmix_ref[...] = dmb
        dcat = _dot_nt(dmb, w_vmem[...])
        nrm = nrm_ref[...]
        g_hg, g_sb = nrm[:, :HG_WIDTH], nrm[:, HG_WIDTH:]
        hg = hg_ref[...]
        m = _mixer_out(ohg_ref[...], hg, osb_ref[...], g_hg, g_sb)
        d_hg = dcat[:, :HG_WIDTH]
        silu = hg * m["s_hg"]
        dhg_ref[...] = d_hg * (m["n_hg"] * g_hg) * (m["s_hg"] * (1.0 + hg * (1.0 - m["s_hg"])))
        dx, dw = _rms_bwd(d_hg * silu, m["n_hg"], m["r_hg"], g_hg)
        dohg_ref[...] = dx
        pack_ref[ROW_MIX_NORMS:ROW_MIX_NORMS + 1, :HG_WIDTH] += _colsum(dw)
        dx, dw = _rms_bwd(dcat[:, HG_WIDTH:], m["n_sb"], m["r_sb"], g_sb)
        dosb_ref[...] = dx
        pack_ref[ROW_MIX_NORMS:ROW_MIX_NORMS + 1, HG_WIDTH:] += _colsum(dw)

    return pl.pallas_call(
        body, name="mix_out_bwd", grid=(T // ROW_TILE,),
        in_specs=[_row_spec(D_MODEL), _row_spec(D_MODEL), _row_spec(HG_WIDTH), _row_spec(HG_WIDTH, 3), _row_spec(SB_WIDTH),
                  _full_spec((1, D_MODEL)), _full_spec((1, D_MODEL)), ANY_SPEC],
        out_specs=[_row_spec(D_MODEL), _row_spec(HG_WIDTH), _row_spec(HG_WIDTH), _row_spec(SB_WIDTH), PACK_SPEC],
        out_shape=[jax.ShapeDtypeStruct((T, D_MODEL), BF16), jax.ShapeDtypeStruct((T, HG_WIDTH), F32),
                   jax.ShapeDtypeStruct((T, HG_WIDTH), F32), jax.ShapeDtypeStruct((T, SB_WIDTH), F32),
                   jax.ShapeDtypeStruct((8, D_MODEL), F32)],
        scratch_shapes=[pltpu.VMEM(w_out.shape, BF16)],
        compiler_params=_params(("arbitrary",)),
    )(dh1, mix, o_hg, proj_h, o_sb, norms, g_post, w_out)


def in_proj_bwd(parts, x, dh1, g_pre, w_in):
    T = x.shape[0]
    pw = w_in.shape[2]
    n_parts = len(parts)

    def body(*refs):
        part_refs = refs[:n_parts]
        x_ref, dh1_ref, g_ref, w_hbm, dproj_ref, dx_ref, pack_ref, w_vmem = refs[n_parts:]
        _load_once(w_hbm, w_vmem)
        _zero_first(pack_ref)
        for n, ref in enumerate(part_refs):
            dproj_ref[:, HG_WIDTH * n:HG_WIDTH * (n + 1)] = ref[...].astype(BF16)
        du = _dot_nt(dproj_ref[:, :pw], w_vmem[0])
        for q in range(1, N_CHIPS):
            du = du + _dot_nt(dproj_ref[:, pw * q:pw * (q + 1)], w_vmem[q])
        xh, r = _rms(x_ref[...])
        dx, dw = _rms_bwd(du, xh, r, g_ref[...])
        pack_ref[ROW_ATTN_PRE:ROW_ATTN_PRE + 1, :] += _colsum(dw)
        dx_ref[...] = dh1_ref[...] + dx

    return pl.pallas_call(
        body, name="in_proj_bwd", grid=(T // ROW_TILE,),
        in_specs=[_row_spec(HG_WIDTH)] * n_parts + [_row_spec(D_MODEL), _row_spec(D_MODEL), _full_spec((1, D_MODEL)), ANY_SPEC],
        out_specs=[_row_spec(n_parts * HG_WIDTH), _row_spec(D_MODEL), PACK_SPEC],
        out_shape=[jax.ShapeDtypeStruct((T, n_parts * HG_WIDTH), BF16), jax.ShapeDtypeStruct((T, D_MODEL), F32),
                   jax.ShapeDtypeStruct((8, D_MODEL), F32)],
        scratch_shapes=[pltpu.VMEM(w_in.shape, BF16)],
        compiler_params=_params(("arbitrary",)),
    )(*parts, x, dh1, g_pre, w_in)


def weight_grad(a, g, name, *, tm, tn, tk=512, col_pieces=False):
    T, M = a.shape
    N = g.shape[1]
    steps = T // tk

    def body(a_ref, g_ref, o_ref):
        @pl.when(pl.program_id(2) == 0)
        def _():
            o_ref[...] = jnp.zeros(o_ref.shape, F32)

        o_ref[...] += _dot_tn(a_ref[...], g_ref[...]).reshape(o_ref.shape)

    if col_pieces:
        out_shape = jax.ShapeDtypeStruct((N // tn, M, tn), F32)
        out_spec = pl.BlockSpec((1, tm, tn), lambda i, j, k: (j, i, 0))
    else:
        out_shape = jax.ShapeDtypeStruct((M, N), F32)
        out_spec = pl.BlockSpec((tm, tn), lambda i, j, k: (i, j))
    return pl.pallas_call(
        body, name=name, grid=(M // tm, N // tn, steps),
        in_specs=[pl.BlockSpec((tk, tm), lambda i, j, k: (k, i)), pl.BlockSpec((tk, tn), lambda i, j, k: (k, j))],
        out_specs=out_spec, out_shape=out_shape,
        compiler_params=_params(("arbitrary", "arbitrary", "arbitrary")),
    )(a, g)


def _place():
    x, y, c = lax.axis_index("x"), lax.axis_index("y"), lax.axis_index("c")
    chips = [(1 - x, y), (x, 1 - y), (1 - x, 1 - y)]
    return x, y, c, chips


def _chip_index(cx, cy):
    return 2 * cx + cy


def _rcopy(src, dst, send_sem, recv_sem, device):
    return pltpu.make_async_remote_copy(src_ref=src, dst_ref=dst, send_sem=send_sem, recv_sem=recv_sem,
                                        device_id=device, device_id_type=MESH)


def gather_weights(shards):
    n = len(shards)

    def body(*refs):
        ins, outs = refs[:n], refs[2 * n:3 * n]
        send_sems, recv_sems = refs[3 * n:]
        x, y, c, chips = _place()
        me = _chip_index(x, y)
        sibling = (x, y, 1 - c)

        def rows(w, core):
            half = ins[w].shape[0] // 2
            return pl.ds(core * half, half)

        sends = []
        for w in range(n):
            for j, chip in enumerate(chips):
                sends.append(_rcopy(ins[w].at[rows(w, c)], outs[w].at[me, rows(w, c)],
                                    send_sems.at[6 * w + j], recv_sems.at[6 * w + j], (*chip, c)))
        for cp in sends:
            cp.start()
        passed = []
        for w in range(n):
            for j, chip in enumerate(chips):
                block = outs[w].at[_chip_index(*chip), rows(w, c)]
                _rcopy(block, block, send_sems.at[6 * w + j], recv_sems.at[6 * w + j], (*chip, c)).wait_recv()
                cp = _rcopy(block, block, send_sems.at[6 * w + 3 + j], recv_sems.at[6 * w + 3 + j], sibling)
                cp.start()
                passed.append(cp)
        for w in range(n):
            for j, chip in enumerate(chips):
                block = outs[w].at[_chip_index(*chip), rows(w, 1 - c)]
                _rcopy(block, block, send_sems.at[6 * w + 3 + j], recv_sems.at[6 * w + 3 + j], sibling).wait_recv()
        for cp in sends + passed:
            cp.wait_send()

    filled = [jnp.broadcast_to(s[None], (N_CHIPS,) + s.shape) for s in shards]
    return pl.pallas_call(
        body, name="gather_weights",
        in_specs=[ANY_SPEC] * (2 * n), out_specs=[ANY_SPEC] * n,
        out_shape=[jax.ShapeDtypeStruct(f.shape, f.dtype) for f in filled],
        input_output_aliases={n + w: w for w in range(n)},
        scratch_shapes=[pltpu.SemaphoreType.DMA((6 * n,)), pltpu.SemaphoreType.DMA((6 * n,))],
    )(*shards, *filled)


def exchange_core_halves(grads):
    n = len(grads)

    def body(*refs):
        ins, got = refs[:n], refs[n:2 * n]
        send_sems, recv_sems = refs[2 * n:]
        x, y, c, _ = _place()
        copies = []
        for w in range(n):
            half = ins[w].shape[1] // 2
            copies.append(_rcopy(ins[w].at[:, pl.ds((1 - c) * half, half), :], got[w],
                                 send_sems.at[w], recv_sems.at[w], (x, y, 1 - c)))
        for cp in copies:
            cp.start()
        for cp in copies:
            cp.wait()

    return pl.pallas_call(
        body, name="exchange_core_halves",
        in_specs=[ANY_SPEC] * n, out_specs=[ANY_SPEC] * n,
        out_shape=[jax.ShapeDtypeStruct((g.shape[0], g.shape[1] // 2, g.shape[2]), g.dtype) for g in grads],
        scratch_shapes=[pltpu.SemaphoreType.DMA((n,)), pltpu.SemaphoreType.DMA((n,))],
    )(*grads)


def exchange_chip_partials(partials):
    n = len(partials)

    def body(*refs):
        ins, outs = refs[:n], refs[2 * n:3 * n]
        send_sems, recv_sems = refs[3 * n:]
        x, y, c, chips = _place()
        me = _chip_index(x, y)
        sends = []
        for w in range(n):
            for j, chip in enumerate(chips):
                sends.append(_rcopy(ins[w].at[_chip_index(*chip)], outs[w].at[me],
                                    send_sems.at[3 * w + j], recv_sems.at[3 * w + j], (*chip, c)))
        for cp in sends:
            cp.start()
        for w in range(n):
            for j, chip in enumerate(chips):
                block = outs[w].at[_chip_index(*chip)]
                _rcopy(block, block, send_sems.at[3 * w + j], recv_sems.at[3 * w + j], (*chip, c)).wait_recv()
        for cp in sends:
            cp.wait_send()

    me = _chip_index(lax.axis_index("x"), lax.axis_index("y"))
    filled = [jnp.broadcast_to(lax.dynamic_index_in_dim(p, me, 0, keepdims=True), p.shape) for p in partials]
    return pl.pallas_call(
        body, name="exchange_chip_partials",
        in_specs=[ANY_SPEC] * (2 * n), out_specs=[ANY_SPEC] * n,
        out_shape=[jax.ShapeDtypeStruct(p.shape, p.dtype) for p in partials],
        input_output_aliases={n + w: w for w in range(n)},
        scratch_shapes=[pltpu.SemaphoreType.DMA((3 * n,)), pltpu.SemaphoreType.DMA((3 * n,))],
    )(*partials, *filled)


def join_core_halves(grads):
    n = len(grads)

    def body(*refs):
        outs = refs[n:2 * n]
        send_sems, recv_sems = refs[2 * n:]
        x, y, c, _ = _place()
        sibling = (x, y, 1 - c)
        copies = []
        for w in range(n):
            half = outs[w].shape[0] // 2
            mine = outs[w].at[pl.ds(c * half, half), :]
            copies.append(_rcopy(mine, mine, send_sems.at[w], recv_sems.at[w], sibling))
        for cp in copies:
            cp.start()
        for w in range(n):
            half = outs[w].shape[0] // 2
            theirs = outs[w].at[pl.ds((1 - c) * half, half), :]
            _rcopy(theirs, theirs, send_sems.at[w], recv_sems.at[w], sibling).wait_recv()
        for cp in copies:
            cp.wait_send()

    return pl.pallas_call(
        body, name="join_core_halves",
        in_specs=[ANY_SPEC] * n, out_specs=[ANY_SPEC] * n,
        out_shape=[jax.ShapeDtypeStruct(g.shape, g.dtype) for g in grads],
        input_output_aliases={w: w for w in range(n)},
        scratch_shapes=[pltpu.SemaphoreType.DMA((n,)), pltpu.SemaphoreType.DMA((n,))],
    )(*grads)


def _elementwise_rows(rows, cap=512):
    for t in range(min(rows, cap), 0, -8):
        if rows % t == 0 and t % 16 == 0:
            return t
    return rows


def add_core_halves(grad, got, core, name):
    _, rows, cols = got.shape
    tr = _elementwise_rows(rows)
    nt = rows // tr

    def body(core_ref, a_ref, b_ref, o_ref):
        o_ref[...] = (a_ref[...] + b_ref[...]).astype(BF16)

    spec = pl.BlockSpec((1, tr, cols), lambda q, i, core_ref: (q, i, 0))
    own = pl.BlockSpec((1, tr, cols), lambda q, i, core_ref: (q, core_ref[0] * nt + i, 0))
    return pl.pallas_call(
        body, name=name,
        grid_spec=pltpu.PrefetchScalarGridSpec(num_scalar_prefetch=1, grid=(N_CHIPS, nt), in_specs=[own, spec],
                                               out_specs=spec),
        out_shape=jax.ShapeDtypeStruct(got.shape, BF16),
        compiler_params=_params(("arbitrary", "arbitrary")),
    )(core, grad, got)


def add_chip_partials(parts, core, name):
    _, rows, cols = parts.shape
    tr = _elementwise_rows(rows)
    nt = rows // tr

    def body(core_ref, p_ref, o_ref):
        acc = p_ref[0].astype(F32)
        for q in range(1, N_CHIPS):
            acc = acc + p_ref[q].astype(F32)
        o_ref[...] = acc

    return pl.pallas_call(
        body, name=name,
        grid_spec=pltpu.PrefetchScalarGridSpec(
            num_scalar_prefetch=1, grid=(nt,),
            in_specs=[pl.BlockSpec((N_CHIPS, tr, cols), lambda i, core_ref: (0, i, 0))],
            out_specs=pl.BlockSpec((tr, cols), lambda i, core_ref: (core_ref[0] * nt + i, 0))),
        out_shape=jax.ShapeDtypeStruct((2 * rows, cols), F32),
        compiler_params=_params(("arbitrary",)),
    )(core, parts)


def _adamw_math(w, g, m, v):
    m = ADAM_B1 * m + (1.0 - ADAM_B1) * g
    v = ADAM_B2 * v + (1.0 - ADAM_B2) * (g * g)
    m_hat = m / (1.0 - ADAM_B1 ** ADAM_STEP)
    v_hat = v / (1.0 - ADAM_B2 ** ADAM_STEP)
    delta = -ADAM_LR * (m_hat / (jnp.sqrt(v_hat) + ADAM_EPS) + ADAM_WD * w)
    return delta, m, v


def adamw(w, g, m, v, name):
    rows, cols = w.shape
    tr = _elementwise_rows(rows, 256)

    def body(w_ref, g_ref, m_ref, v_ref, d_ref, nm_ref, nv_ref):
        d, nm, nv = _adamw_math(w_ref[...], g_ref[...], m_ref[...], v_ref[...])
        d_ref[...] = d
        nm_ref[...] = nm
        nv_ref[...] = nv

    spec = pl.BlockSpec((tr, cols), lambda i: (i, 0))
    return pl.pallas_call(
        body, name=name, grid=(rows // tr,), in_specs=[spec] * 4, out_specs=[spec] * 3,
        out_shape=[jax.ShapeDtypeStruct((rows, cols), F32)] * 3,
        compiler_params=_params(("arbitrary",)),
    )(w, g, m, v)


def reduce_small(packs, w, m, v):
    n = len(packs)
    n_dev = 8
    flips = [(fx, fy, fc) for fx in (0, 1) for fy in (0, 1) for fc in (0, 1)][1:]

    def body(*refs):
        pack_refs = refs[:n]
        w_ref, m_ref, v_ref, g_out, d_out, m_out, v_out, mine, slots, send_sems, recv_sems = refs[n:]
        x, y, c, _ = _place()
        me = 4 * x + 2 * y + c
        acc = pack_refs[0][...]
        for ref in pack_refs[1:]:
            acc = acc + ref[...]
        mine[...] = acc
        sends = []
        for k, (fx, fy, fc) in enumerate(flips):
            peer = (x ^ fx, y ^ fy, c ^ fc)
            sends.append(_rcopy(mine, slots.at[me], send_sems.at[k], recv_sems.at[me], peer))
        for cp in sends:
            cp.start()
        slots[me] = acc
        for fx, fy, fc in flips:
            src = 4 * (x ^ fx) + 2 * (y ^ fy) + (c ^ fc)
            _rcopy(mine, slots.at[src], send_sems.at[0], recv_sems.at[src], (x, y, c)).wait_recv()
        for cp in sends:
            cp.wait_send()
        total = slots[0]
        for d in range(1, n_dev):
            total = total + slots[d]
        g_out[...] = total
        d, nm, nv = _adamw_math(w_ref[...], total, m_ref[...], v_ref[...])
        d_out[...] = d
        m_out[...] = nm
        v_out[...] = nv

    vm = pl.BlockSpec(memory_space=pltpu.VMEM)
    return pl.pallas_call(
        body, name="reduce_small",
        in_specs=[vm] * (n + 3), out_specs=[vm] * 4,
        out_shape=[jax.ShapeDtypeStruct((8, D_MODEL), F32)] * 4,
        scratch_shapes=[pltpu.VMEM((8, D_MODEL), F32), pltpu.VMEM((n_dev, 8, D_MODEL), F32),
                        pltpu.SemaphoreType.DMA((len(flips),)), pltpu.SemaphoreType.DMA((n_dev,))],
    )(*packs, w, m, v)


def _pack_small(attn_pre, gamma, hg_norm, sb_norm, attn_post, ffn_pre, ffn_post):
    rows = [attn_pre, gamma.reshape(1, D_MODEL), jnp.concatenate([hg_norm, sb_norm], axis=1), attn_post, ffn_pre, ffn_post,
            jnp.zeros((2, D_MODEL), F32)]
    return jnp.concatenate(rows, axis=0)


def _unpack_small(pack):
    return (pack[ROW_ATTN_PRE:ROW_ATTN_PRE + 1], pack[ROW_GAMMA].reshape(2, HG_WIDTH),
            pack[ROW_MIX_NORMS:ROW_MIX_NORMS + 1, :HG_WIDTH], pack[ROW_MIX_NORMS:ROW_MIX_NORMS + 1, HG_WIDTH:],
            pack[ROW_ATTN_POST:ROW_ATTN_POST + 1], pack[ROW_FFN_PRE:ROW_FFN_PRE + 1], pack[ROW_FFN_POST:ROW_FFN_POST + 1])


def kernel(x, p, attn_pre_norm, w_in, hg_lower_gamma, hg_out_norm, sb_out_norm, w_out, attn_post_norm, ffn_pre_norm, w_gate_up, w_down, ffn_post_norm, ple_proj, ple_gate, loss_target, m_attn_pre_norm, m_w_in, m_hg_lower_gamma, m_hg_out_norm, m_sb_out_norm, m_w_out, m_attn_post_norm, m_ffn_pre_norm, m_w_gate_up, m_w_down, m_ffn_post_norm, m_ple_proj, m_ple_gate, v_attn_pre_norm, v_w_in, v_hg_lower_gamma, v_hg_out_norm, v_sb_out_norm, v_w_out, v_attn_post_norm, v_ffn_pre_norm, v_w_gate_up, v_w_down, v_ffn_post_norm, v_ple_proj, v_ple_gate):
    x2 = x[0]
    p2 = p[0, 0]
    target = loss_target[0]
    big = dict(w_in=(w_in, m_w_in, v_w_in), w_out=(w_out, m_w_out, v_w_out), w_gate_up=(w_gate_up, m_w_gate_up, v_w_gate_up),
               w_down=(w_down, m_w_down, v_w_down), ple_proj=(ple_proj, m_ple_proj, v_ple_proj),
               ple_gate=(ple_gate, m_ple_gate, v_ple_gate))
    names = list(big)
    big = {k: tuple(a[0] for a in t) for k, t in big.items()}

    full = dict(zip(names, gather_weights([big[k][0].astype(BF16) for k in names])))
    w_out_full = full["w_out"].reshape(D_MODEL, D_MODEL)
    w_down_full = full["w_down"].reshape(D_FF, D_MODEL)
    w_pg_full = full["ple_gate"].reshape(D_MODEL, D_MODEL)
    mix_norms = jnp.concatenate([hg_out_norm, sb_out_norm], axis=1)

    proj_h, sqkv, u1 = in_proj_fwd(x2, attn_pre_norm, full["w_in"])
    o_hg, states = hgrn2_fwd(proj_h, hg_lower_gamma)
    o_sb, sb_totals, sb_first = sb_fwd(sqkv)
    cat, mix, h1 = mix_out_fwd(o_hg, proj_h, o_sb, x2, mix_norms, attn_post_norm, w_out_full)
    u2, gu, act, y, h2 = ffn_fwd(h1, ffn_pre_norm, ffn_post_norm, full["w_gate_up"], w_down_full)

    de, ds, dh2, h2b, pb, pack_loss = ple_loss(h2, p2, target, full["ple_proj"], w_pg_full)
    dy, dgu, dh1, pack_ffn = ffn_bwd(dh2, y, h1, gu, ffn_pre_norm, ffn_post_norm, full["w_gate_up"], w_down_full)
    dmix, do_hg, dhg, do_sb, pack_mix = mix_out_bwd(dh1, mix, o_hg, proj_h, o_sb, mix_norms, attn_post_norm, w_out_full)
    dsq, dsk, dsv = sb_bwd(sqkv, do_sb, sb_totals, sb_first)
    dhq, dhf, dhi, pack_hg = hgrn2_bwd(proj_h, hg_lower_gamma, states, do_hg)
    dproj, grad_x, pack_in = in_proj_bwd([dhq, dhf, dhi, dhg, dsq, dsk, dsv], x2, dh1, attn_pre_norm, full["w_in"])

    local = dict(
        w_in=weight_grad(u1, dproj, "grad_w_in", tm=D_MODEL, tn=full["w_in"].shape[2], col_pieces=True),
        w_out=weight_grad(cat, dmix, "grad_w_out", tm=D_MODEL, tn=D_MODEL).reshape(full["w_out"].shape),
        w_gate_up=weight_grad(u2, dgu, "grad_w_gate_up", tm=D_MODEL, tn=full["w_gate_up"].shape[2], col_pieces=True),
        w_down=weight_grad(act, dy, "grad_w_down", tm=D_FF // 2, tn=D_MODEL).reshape(full["w_down"].shape),
        ple_proj=weight_grad(pb, de, "grad_ple_proj", tm=pb.shape[1], tn=full["ple_proj"].shape[2], col_pieces=True),
        ple_gate=weight_grad(h2b, ds, "grad_ple_gate", tm=D_MODEL, tn=D_MODEL).reshape(full["ple_gate"].shape),
    )

    core = lax.axis_index("c").astype(jnp.int32).reshape(1)
    got = exchange_core_halves([local[k] for k in names])
    partial = [add_core_halves(local[k], g, core, "add_core_halves_" + k) for k, g in zip(names, got)]
    by_source = exchange_chip_partials(partial)
    halves = [add_chip_partials(s, core, "add_chip_partials_" + k) for k, s in zip(names, by_source)]
    grads = dict(zip(names, join_core_halves(halves)))

    upd = {k: adamw(big[k][0], grads[k], big[k][1], big[k][2], "adamw_" + k) for k in names}

    small = reduce_small(
        [pack_loss, pack_ffn, pack_mix, pack_hg, pack_in],
        _pack_small(attn_pre_norm, hg_lower_gamma, hg_out_norm, sb_out_norm, attn_post_norm, ffn_pre_norm, ffn_post_norm),
        _pack_small(m_attn_pre_norm, m_hg_lower_gamma, m_hg_out_norm, m_sb_out_norm, m_attn_post_norm, m_ffn_pre_norm, m_ffn_post_norm),
        _pack_small(v_attn_pre_norm, v_hg_lower_gamma, v_hg_out_norm, v_sb_out_norm, v_attn_post_norm, v_ffn_pre_norm, v_ffn_post_norm),
    )
    loss = small[0][ROW_LOSS, 0]
    s_grad, s_delta, s_m, s_v = (_unpack_small(t) for t in small)

    def ordered(small_vals, big_vals):
        a_pre, gam, hg_n, sb_n, a_post, f_pre, f_post = small_vals
        b = {k: big_vals[k][None] for k in names}
        return (a_pre, b["w_in"], gam, hg_n, sb_n, b["w_out"], a_post, f_pre, b["w_gate_up"], b["w_down"], f_post,
                b["ple_proj"], b["ple_gate"])

    return (loss, grad_x[None],
            *ordered(s_grad, grads),
            *ordered(s_delta, {k: upd[k][0] for k in names}),
            *ordered(s_m, {k: upd[k][1] for k in names}),
            *ordered(s_v, {k: upd[k][2] for k in names}))
```

```python
from typing import Callable, NamedTuple

import numpy as np
import jax
import jax.numpy as jnp
from jax import lax
from jax.experimental import pallas as pl
from jax.experimental.pallas import tpu as pltpu

F32 = jnp.float32
BF16 = jnp.bfloat16
MESH = pl.DeviceIdType.MESH

RMS_EPS = 1e-6
D_MODEL = 1024
HG_WIDTH = 512
HG_HEADS = 4
HG_DK = 128
HG_CHUNK = 64
HG_LEVELS = (32, 16, 8, 4, 2, 1)
SB_WIDTH = 512
SB_BLOCK = 128
SB_DH = 64
SB_SCALE = SB_DH ** -0.5
SB_UNDERFLOW_LOG = -90.0
SB_UNROLL = 2
D_FF = 2816
N_CHIPS = 4
ROW_TILE = 256
V7X_VMEM_LIMIT = 56 * 1024 * 1024

ADAM_LR = 0.001
ADAM_B1 = 0.9
ADAM_B2 = 0.999
ADAM_EPS = 1e-08
ADAM_WD = 0.01
ADAM_STEP = 10

ROW_ATTN_PRE, ROW_GAMMA, ROW_MIX_NORMS, ROW_ATTN_POST, ROW_FFN_PRE, ROW_FFN_POST, ROW_LOSS = range(7)


def _params(sem=None, vmem=V7X_VMEM_LIMIT):
    return pltpu.CompilerParams(dimension_semantics=sem, vmem_limit_bytes=vmem)


def _dot(a, b):
    return jnp.dot(a.astype(BF16), b.astype(BF16), preferred_element_type=F32)


def _dot_nt(a, b):
    return lax.dot_general(a.astype(BF16), b.astype(BF16), (((1,), (1,)), ((), ())), preferred_element_type=F32)


def _dot_tn(a, b):
    return lax.dot_general(a.astype(BF16), b.astype(BF16), (((0,), (0,)), ((), ())), preferred_element_type=F32)


def _split(x):
    hi = x.astype(BF16)
    lo = (x - hi.astype(F32)).astype(BF16)
    return hi, lo


def _sum01_left(m01, x):
    hi, lo = _split(x)
    return jnp.dot(m01, hi, preferred_element_type=F32) + jnp.dot(m01, lo, preferred_element_type=F32)


def _sum01_right(x, m01):
    hi, lo = _split(x)
    return jnp.dot(hi, m01, preferred_element_type=F32) + jnp.dot(lo, m01, preferred_element_type=F32)


def _rms(x):
    r = lax.rsqrt(jnp.mean(x * x, axis=-1, keepdims=True) + RMS_EPS)
    return x * r, r


def _rms_bwd(dy, xhat, r, w):
    dxh = dy * w
    dx = r * (dxh - xhat * jnp.mean(dxh * xhat, axis=-1, keepdims=True))
    return dx, dy * xhat


def _sigmoid(x):
    return 1.0 / (1.0 + jnp.exp(-x))


def _neg_softplus(z):
    return -(jnp.maximum(z, 0.0) + jnp.log(1.0 + jnp.exp(-jnp.abs(z))))


def _colsum(x):
    return jnp.sum(x, axis=0, keepdims=True)


def _load_once(src_hbm, dst_vmem):
    @pl.when(pl.program_id(0) == 0)
    def _():
        pltpu.sync_copy(src_hbm, dst_vmem)


def _zero_first(ref):
    @pl.when(pl.program_id(0) == 0)
    def _():
        ref[...] = jnp.zeros(ref.shape, ref.dtype)


def _row_spec(width, col=0):
    return pl.BlockSpec((ROW_TILE, width), lambda i, col=col: (i, col))


def _full_spec(shape):
    return pl.BlockSpec(shape, lambda *_: (0,) * len(shape))


ANY_SPEC = pl.BlockSpec(memory_space=pl.ANY)
PACK_SPEC = _full_spec((8, D_MODEL))


class Comm(NamedTuple):
    inputs: list
    out_shape: list
    aliases: dict
    scratch: list
    start: Callable
    finish: Callable


def _pallas(body, *, comm=None, edge=None, in_specs, out_specs, out_shape, scratch_shapes=(), **kw):
    if comm is None:
        return pl.pallas_call(body, in_specs=in_specs, out_specs=out_specs, out_shape=out_shape,
                              scratch_shapes=scratch_shapes, **kw)
    n_in, n_out, n_scr = len(in_specs), len(out_specs), len(scratch_shapes)
    c_in, c_out = len(comm.inputs), len(comm.out_shape)

    def both(*refs):
        ins, c_ins = refs[:n_in], refs[n_in:n_in + c_in]
        outs = refs[n_in + c_in:n_in + c_in + n_out]
        c_outs = refs[n_in + c_in + n_out:n_in + c_in + n_out + c_out]
        rest = refs[n_in + c_in + n_out + c_out:]
        scr, c_scr = rest[:n_scr], rest[n_scr:]
        first, last = edge()

        @pl.when(first)
        def _():
            comm.start(c_ins, c_outs, c_scr)

        body(*ins, *outs, *scr)

        @pl.when(last)
        def _():
            comm.finish(c_ins, c_outs, c_scr)

    call = pl.pallas_call(
        both, in_specs=list(in_specs) + [ANY_SPEC] * c_in, out_specs=list(out_specs) + [ANY_SPEC] * c_out,
        out_shape=list(out_shape) + list(comm.out_shape), scratch_shapes=list(scratch_shapes) + list(comm.scratch),
        input_output_aliases={n_in + a: n_out + b for a, b in comm.aliases.items()}, **kw)
    return lambda *args: call(*args, *comm.inputs)


def _grid_edge(steps):
    return lambda: (pl.program_id(0) == 0, pl.program_id(0) == steps - 1)


def in_proj_fwd(x, g_pre, w_in):
    T = x.shape[0]
    pw = w_in.shape[2]

    def body(x_ref, g_ref, w_hbm, ph_ref, sqkv_ref, u_ref, w_vmem, proj_s):
        _load_once(w_hbm, w_vmem)
        xh, _ = _rms(x_ref[...])
        u = (xh * g_ref[...]).astype(BF16)
        u_ref[...] = u
        for q in range(N_CHIPS):
            proj_s[:, pw * q:pw * (q + 1)] = jnp.dot(u, w_vmem[q], preferred_element_type=F32)
        ph_ref[...] = proj_s[:, :4 * HG_WIDTH]
        sqkv_ref[:, :SB_WIDTH] = (proj_s[:, 4 * HG_WIDTH:4 * HG_WIDTH + SB_WIDTH] * SB_SCALE).astype(BF16)
        sqkv_ref[:, SB_WIDTH:] = proj_s[:, 4 * HG_WIDTH + SB_WIDTH:].astype(BF16)

    return pl.pallas_call(
        body, name="in_proj_fwd", grid=(T // ROW_TILE,),
        in_specs=[_row_spec(D_MODEL), _full_spec((1, D_MODEL)), ANY_SPEC],
        out_specs=[_row_spec(4 * HG_WIDTH), _row_spec(3 * SB_WIDTH), _row_spec(D_MODEL)],
        out_shape=[jax.ShapeDtypeStruct((T, 4 * HG_WIDTH), F32), jax.ShapeDtypeStruct((T, 3 * SB_WIDTH), BF16),
                   jax.ShapeDtypeStruct((T, D_MODEL), BF16)],
        scratch_shapes=[pltpu.VMEM(w_in.shape, BF16), pltpu.VMEM((ROW_TILE, N_CHIPS * pw), F32)],
        compiler_params=_params(("arbitrary",)),
    )(x, g_pre, w_in)


def _hg_sum_matrix():
    C = HG_CHUNK
    t = np.arange(C)[:, None]
    j = np.arange(C)[None, :]
    mats = [j <= t, j > t]
    for h in HG_LEVELS:
        start = (t // (2 * h)) * (2 * h)
        upper = (t & h) != 0
        mats.append(np.where(upper, (j >= start + h) & (j <= t), (j > t) & (j <= start + h - 1)))
    return np.concatenate(mats, 0).astype(np.float32)


def _hg_level_masks():
    C = HG_CHUNK
    t = lax.broadcasted_iota(jnp.int32, (C, C), 0)
    s = lax.broadcasted_iota(jnp.int32, (C, C), 1)
    x = t ^ s
    masks = [t == s]
    for h in HG_LEVELS:
        masks.append((x >= h) & (x < 2 * h) & (t > s))
    return masks


def _hg_gates(hq, hf, gamma):
    lb = 1.0 / (1.0 + jnp.exp(gamma[1:2, :] - gamma[0:1, :]))
    sq = _sigmoid(hq)
    q = hq * sq
    sig = _sigmoid(hf)
    nsig = _sigmoid(-hf)
    f = lb + (1.0 - lb) * sig
    k = (1.0 - lb) * nsig
    g = jnp.log(f)
    return q, k, g, dict(lb=lb, sq=sq, sig=sig, nsig=nsig, f=f)


def _hg_head_decays(A, h):
    C, K = HG_CHUNK, HG_DK
    sl = slice(K * h, K * (h + 1))
    blocks = [A[C * r:C * (r + 1), sl] for r in range(2 + len(HG_LEVELS))]
    return blocks[0], blocks[1], [None] + blocks[2:]


def _hg_products(q, k, levels):
    return [_dot_nt(q, k)] + [_dot_nt(q * a, k * a) for a in levels[1:]]


def _hg_select(prods, masks):
    sc = jnp.where(masks[0], prods[0], 0.0)
    for p, m in zip(prods[1:], masks[1:]):
        sc = jnp.where(m, p, sc)
    return sc


def hgrn2_fwd(proj_h, gamma, comm=None):
    T = proj_h.shape[0]
    C, K, H = HG_CHUNK, HG_DK, HG_HEADS
    n_chunks = T // C
    msum = jnp.asarray(_hg_sum_matrix(), BF16)

    def body(hq_ref, hf_ref, hi_ref, gam_ref, msum_ref, o_ref, st_ref, st_s):
        _zero_first(st_s)
        q, k, g, _ = _hg_gates(hq_ref[...], hf_ref[...], gam_ref[...])
        v = hi_ref[...]
        A = jnp.exp(_sum01_left(msum_ref[...], g))
        masks = _hg_level_masks()
        heads = []
        for h in range(H):
            sl = slice(K * h, K * (h + 1))
            ab, ar, levels = _hg_head_decays(A, h)
            heads.append((sl, ab, ar, levels, q[:, sl], k[:, sl], v[:, sl], st_s[h]))
        prods = [_hg_products(qh, kh, levels) for _, _, _, levels, qh, kh, _, _ in heads]
        inter = [_dot_nt(qh * ab, st) for _, ab, _, _, qh, _, _, st in heads]
        grown = [_dot_tn(vh, kh * ar) for _, _, ar, _, _, kh, vh, _ in heads]
        scores = [_hg_select(p, masks) for p in prods]
        for (sl, ab, _, _, _, _, vh, st), sc, o_inter, st_add in zip(heads, scores, inter, grown):
            o_ref[:, sl] = o_inter + _dot(sc, vh)
            st_new = st * ab[C - 1:C, :] + st_add
            st_s[sl.start // K] = st_new
            st_ref[0, sl.start // K] = st_new

    blk = lambda col: pl.BlockSpec((C, HG_WIDTH), lambda c, col=col: (c, col))
    return _pallas(
        body, comm=comm, edge=_grid_edge(n_chunks), name="hgrn2_fwd", grid=(n_chunks,),
        in_specs=[blk(0), blk(1), blk(2), _full_spec((2, HG_WIDTH)), _full_spec(msum.shape)],
        out_specs=[blk(0), pl.BlockSpec((1, H, K, K), lambda c: (c, 0, 0, 0))],
        out_shape=[jax.ShapeDtypeStruct((T, HG_WIDTH), F32), jax.ShapeDtypeStruct((n_chunks, H, K, K), F32)],
        scratch_shapes=[pltpu.VMEM((H, K, K), F32)],
        compiler_params=_params(("arbitrary",)),
    )(proj_h, proj_h, proj_h, gamma, msum)


def hgrn2_bwd(proj_h, gamma, states, do, comm=None):
    T = proj_h.shape[0]
    C, K, H = HG_CHUNK, HG_DK, HG_HEADS
    n_chunks = T // C
    n_sums = 2 + len(HG_LEVELS)
    msum = jnp.asarray(_hg_sum_matrix(), BF16)
    msum_t = jnp.asarray(_hg_sum_matrix().T, BF16)

    def body(hq_ref, hf_ref, hi_ref, do_ref, gam_ref, msum_ref, msum_t_ref, st_in_ref,
             dhq_ref, dhf_ref, dhi_ref, pack_ref, dst_s, dlb_s, dq_s, dk_s, de_s):
        step = pl.program_id(0)
        _zero_first(dst_s)
        _zero_first(dlb_s)
        _zero_first(pack_ref)
        hq = hq_ref[...]
        q, k, g, aux = _hg_gates(hq, hf_ref[...], gam_ref[...])
        v = hi_ref[...]
        do_all = do_ref[...]
        A = jnp.exp(_sum01_left(msum_ref[...], g))
        masks = _hg_level_masks()
        is_last_row = lax.broadcasted_iota(jnp.int32, (C, K), 0) == C - 1
        has_prev = (step < n_chunks - 1).astype(F32)
        heads = []
        for h in range(H):
            sl = slice(K * h, K * (h + 1))
            ab, ar, levels = _hg_head_decays(A, h)
            heads.append(dict(h=h, sl=sl, ab=ab, ar=ar, levels=levels, q=q[:, sl], k=k[:, sl], v=v[:, sl],
                              do=do_all[:, sl], st_in=st_in_ref[0, h] * has_prev, dst_out=dst_s[h]))
        for hd in heads:
            hd["prods"] = _hg_products(hd["q"], hd["k"], hd["levels"])
            hd["da"] = _dot_nt(hd["do"], hd["v"])
            hd["t1"] = hd["ab"] * _dot(hd["do"], hd["st_in"])
            hd["t2"] = hd["ar"] * _dot(hd["v"], hd["dst_out"])
            hd["dv_state"] = _dot_nt(hd["k"] * hd["ar"], hd["dst_out"])
            hd["dst_add"] = _dot_tn(hd["do"], hd["q"] * hd["ab"])
        for hd in heads:
            hd["sc"] = _hg_select(hd["prods"], masks)
            hd["dam"] = [jnp.where(m, hd["da"], 0.0) for m in masks]
        for hd in heads:
            qh, kh = hd["q"], hd["k"]
            hd["dq_parts"] = [_dot(hd["dam"][0], kh)] + [
                a * _dot(dam, kh * a) for a, dam in zip(hd["levels"][1:], hd["dam"][1:])]
            hd["dk_parts"] = [_dot_tn(hd["dam"][0], qh)] + [
                a * _dot_tn(dam, qh * a) for a, dam in zip(hd["levels"][1:], hd["dam"][1:])]
            hd["dv_intra"] = _dot_tn(hd["sc"], hd["do"])
        for hd in heads:
            h, sl, qh, kh, ab = hd["h"], hd["sl"], hd["q"], hd["k"], hd["ab"]
            decayed = _colsum(hd["st_in"] * hd["dst_out"]) * ab[C - 1:C, :]
            de_s[0:C, sl] = qh * hd["t1"] + jnp.where(is_last_row, decayed, 0.0)
            de_s[C:2 * C, sl] = kh * hd["t2"]
            dq = hd["t1"] + hd["dq_parts"][0]
            dk = hd["t2"] + hd["dk_parts"][0]
            for r, (t1, t2) in enumerate(zip(hd["dq_parts"][1:], hd["dk_parts"][1:])):
                dq = dq + t1
                dk = dk + t2
                de_s[C * (r + 2):C * (r + 3), sl] = qh * t1 + kh * t2
            dhi_ref[:, sl] = hd["dv_intra"] + hd["dv_state"]
            dst_s[h] = hd["dst_out"] * ab[C - 1:C, :] + hd["dst_add"]
            dq_s[:, sl] = dq
            dk_s[:, sl] = dk
        dg = _sum01_left(msum_t_ref[...], de_s[...])
        dk = dk_s[...]
        sq, lb = aux["sq"], aux["lb"]
        dhq_ref[...] = dq_s[...] * (sq * (1.0 + hq * (1.0 - sq)))
        common = dg / aux["f"] - dk
        dhf_ref[...] = (1.0 - lb) * aux["sig"] * aux["nsig"] * common
        dlb_s[...] += _colsum(aux["nsig"] * common)

        @pl.when(step == n_chunks - 1)
        def _():
            dgam = lb * (1.0 - lb) * dlb_s[...]
            pack_ref[ROW_GAMMA:ROW_GAMMA + 1, :HG_WIDTH] = dgam
            pack_ref[ROW_GAMMA:ROW_GAMMA + 1, HG_WIDTH:] = -dgam

    last = n_chunks - 1
    blk = lambda col: pl.BlockSpec((C, HG_WIDTH), lambda c, col=col: (last - c, col))
    return _pallas(
        body, comm=comm, edge=_grid_edge(n_chunks), name="hgrn2_bwd", grid=(n_chunks,),
        in_specs=[blk(0), blk(1), blk(2), blk(0), _full_spec((2, HG_WIDTH)), _full_spec(msum.shape),
                  _full_spec(msum_t.shape),
                  pl.BlockSpec((1, H, K, K), lambda c: (jnp.maximum(last - c - 1, 0), 0, 0, 0))],
        out_specs=[blk(0), blk(0), blk(0), PACK_SPEC],
        out_shape=[jax.ShapeDtypeStruct((T, HG_WIDTH), F32)] * 3 + [jax.ShapeDtypeStruct((8, D_MODEL), F32)],
        scratch_shapes=[pltpu.VMEM((H, K, K), F32), pltpu.VMEM((1, HG_WIDTH), F32), pltpu.VMEM((C, HG_WIDTH), F32),
                        pltpu.VMEM((C, HG_WIDTH), F32), pltpu.VMEM((n_sums * C, HG_WIDTH), F32)],
        compiler_params=_params(("arbitrary",)),
    )(proj_h, proj_h, proj_h, do, gamma, msum, msum_t, states)


def _sb_sum_matrix(inclusive):
    B = SB_BLOCK
    j = np.arange(B)[:, None]
    s = np.arange(B)[None, :]
    tri = (j >= s) if inclusive else (j > s)
    return np.concatenate([tri, np.ones((B, B), bool)], 1).astype(np.float32)


def _sb_prefix_matrix(inclusive):
    B = SB_BLOCK
    j = np.arange(B)[:, None]
    s = np.arange(B)[None, :]
    tri = (j <= s) if inclusive else (j < s)
    return np.concatenate([tri, np.ones((B, B), bool)], 1).astype(np.float32)


def _sb_iotas():
    shape = (SB_BLOCK, SB_BLOCK)
    return lax.broadcasted_iota(jnp.int32, shape, 0), lax.broadcasted_iota(jnp.int32, shape, 1)


def _sb_weights(qh, kj, valid, carry, usum):
    z = _dot_nt(qh, kj)
    lnb = jnp.where(valid, _neg_softplus(z), 0.0)
    lb = z + lnb
    sums = _sum01_right(lnb, usum)
    a = jnp.where(valid, jnp.exp(lb + carry + sums[:, :SB_BLOCK]), 0.0)
    return a, lb, carry + sums[:, SB_BLOCK:]


def sb_fwd(sqkv, comm=None):
    T = sqkv.shape[0]
    B = SB_BLOCK
    pairs = SB_WIDTH // B
    usum = jnp.asarray(_sb_sum_matrix(False), BF16)

    def body(q_ref, k_ref, v_ref, u_ref, o_ref, tl_ref, first_ref):
        p, i = pl.program_id(0), pl.program_id(1)
        row, lane = _sb_iotas()
        first = lane < SB_DH
        q = q_ref[...]
        heads = (jnp.where(first, q, jnp.zeros_like(q)), jnp.where(first, jnp.zeros_like(q), q))
        u = u_ref[...]

        def more(loop):
            n, reachable, _ = loop
            return (SB_UNROLL * n <= i) & (reachable > 0)

        def step(loop):
            n, _, state = loop
            blocks = []
            for sub in range(SB_UNROLL):
                j = i - SB_UNROLL * n - sub
                off = pl.multiple_of(jnp.maximum(j, 0) * B, B)
                valid = ((lane + j * B) < (row + i * B)) & (j >= 0)
                blocks.append((k_ref[pl.ds(off, B), :], v_ref[pl.ds(off, B), :], valid))
            z = [[_dot_nt(qh, kj) for qh in heads] for kj, _, _ in blocks]
            lnb = [[jnp.where(valid, _neg_softplus(zz), 0.0) for zz in zs] for zs, (_, _, valid) in zip(z, blocks)]
            sums = [[_sum01_right(x, u) for x in xs] for xs in lnb]
            out = []
            for h, (carry, acc) in enumerate(state):
                for sub, (_, vj, valid) in enumerate(blocks):
                    expo = z[sub][h] + lnb[sub][h] + carry + sums[sub][h][:, :B]
                    acc = acc + _dot(jnp.where(valid, jnp.exp(expo), 0.0), vj)
                    carry = carry + sums[sub][h][:, B:]
                out.append((carry, acc))
            state = tuple(out)
            reachable = (jnp.max(jnp.maximum(state[0][0], state[1][0])) > SB_UNDERFLOW_LOG).astype(jnp.int32)
            return n + 1, reachable, state

        zero = jnp.zeros((B, B), F32)
        done, _, ((tot0, acc0), (tot1, acc1)) = lax.while_loop(
            more, step, (jnp.int32(0), jnp.int32(1), ((zero, zero), (zero, zero))))
        o_ref[...] = jnp.where(first, acc0, acc1)
        tl_ref[...] = jnp.where(first, tot0, tot1)
        first_ref[p, i] = jnp.maximum(i + 1 - SB_UNROLL * done, 0)

    def edge():
        p, i = pl.program_id(0), pl.program_id(1)
        return (p == 0) & (i == 0), (p == pairs - 1) & (i == T // B - 1)

    return _pallas(
        body, comm=comm, edge=edge, name="sb_fwd", grid=(pairs, T // B),
        in_specs=[pl.BlockSpec((B, B), lambda p, i: (i, p)),
                  pl.BlockSpec((T, B), lambda p, i: (0, pairs + p)),
                  pl.BlockSpec((T, B), lambda p, i: (0, 2 * pairs + p)),
                  pl.BlockSpec(usum.shape, lambda p, i: (0, 0))],
        out_specs=[pl.BlockSpec((B, B), lambda p, i: (i, p))] * 2 + [pl.BlockSpec(memory_space=pltpu.SMEM)],
        out_shape=[jax.ShapeDtypeStruct((T, SB_WIDTH), F32)] * 2 + [jax.ShapeDtypeStruct((pairs, T // B), jnp.int32)],
        compiler_params=_params(("arbitrary", "arbitrary")),
    )(sqkv, sqkv, sqkv, usum)


def sb_bwd(sqkv, do, tl, first_block):
    T = sqkv.shape[0]
    B = SB_BLOCK
    pairs = SB_WIDTH // B
    upre = jnp.asarray(_sb_prefix_matrix(True), BF16)
    uexc = jnp.asarray(_sb_prefix_matrix(False), BF16)

    def body(q_ref, k_ref, v_ref, do_ref, tl_ref, up_ref, ue_ref, first_ref, dq_ref, dk_ref, dv_ref):
        p, i = pl.program_id(0), pl.program_id(1)

        @pl.when(i == 0)
        def _():
            dk_ref[...] = jnp.zeros(dk_ref.shape, F32)
            dv_ref[...] = jnp.zeros(dv_ref.shape, F32)

        row, lane = _sb_iotas()
        first = lane < SB_DH
        q = q_ref[...]
        do = do_ref[...]
        tl_all = tl_ref[...]
        zb = jnp.zeros_like(q)
        zf = jnp.zeros_like(do)
        heads = (
            (jnp.where(first, q, zb), jnp.where(first, do, zf).astype(BF16), tl_all[:, 0:1]),
            (jnp.where(first, zb, q), jnp.where(first, zf, do).astype(BF16), tl_all[:, B - 1:B]),
        )
        up = up_ref[...]
        ue = ue_ref[...]

        start = first_ref[p, i]

        def step(n, state):
            blocks = []
            for sub in range(SB_UNROLL):
                j = start + SB_UNROLL * n + sub
                off = pl.multiple_of(jnp.minimum(j, i) * B, B)
                valid = (lane + j * B) < (row + i * B)
                blocks.append((off, k_ref[pl.ds(off, B), :], v_ref[pl.ds(off, B), :], valid))
            combos = [(s, h) for s in range(SB_UNROLL) for h in range(len(heads))]
            z = {(s, h): _dot_nt(heads[h][0], blocks[s][1]) for s, h in combos}
            da = {(s, h): _dot_nt(heads[h][1], blocks[s][2]) for s, h in combos}
            lnb = {c: jnp.where(blocks[c[0]][3], _neg_softplus(z[c]), 0.0) for c in combos}
            lb = {c: z[c] + lnb[c] for c in combos}
            sums = {c: _sum01_right(lnb[c], up) for c in combos}
            a, w = {}, {}
            seen = [st[0] for st in state]
            for s, h in combos:
                expo = lb[s, h] + (heads[h][2] - seen[h] - sums[s, h][:, :B])
                a[s, h] = jnp.where(blocks[s][3], jnp.exp(expo), 0.0)
                w[s, h] = a[s, h] * da[s, h]
                seen[h] = seen[h] + sums[s, h][:, B:]
            wsums = {c: _sum01_right(w[c], ue) for c in combos}
            dz = {}
            seen_w = [st[1] for st in state]
            for s, h in combos:
                beta = jnp.exp(lb[s, h])
                before = seen_w[h] + wsums[s, h][:, :B]
                dz[s, h] = jnp.where(blocks[s][3], w[s, h] * (1.0 - beta) - before * beta, 0.0)
                seen_w[h] = seen_w[h] + wsums[s, h][:, B:]
            dq = [st[2] for st in state]
            for s, h in combos:
                dq[h] = dq[h] + _dot(dz[s, h], blocks[s][1])
            for s in range(SB_UNROLL):
                off = blocks[s][0]
                dk_add = _dot_tn(dz[s, 0], heads[0][0])
                dv_add = _dot_tn(a[s, 0], heads[0][1])
                for h in range(1, len(heads)):
                    dk_add = dk_add + _dot_tn(dz[s, h], heads[h][0])
                    dv_add = dv_add + _dot_tn(a[s, h], heads[h][1])
                dk_ref[pl.ds(off, B), :] += dk_add
                dv_ref[pl.ds(off, B), :] += dv_add
            return tuple(zip(seen, seen_w, dq))

        zero = jnp.zeros((B, B), F32)
        trips = (i - start + SB_UNROLL) // SB_UNROLL
        (_, _, dq0), (_, _, dq1) = lax.fori_loop(0, trips, step, ((zero, zero, zero), (zero, zero, zero)))
        dq_ref[...] = jnp.where(first, dq0, dq1) * SB_SCALE

    qblk = pl.BlockSpec((B, B), lambda p, i: (i, p))
    full = pl.BlockSpec((T, B), lambda p, i: (0, p))
    return pl.pallas_call(
        body, name="sb_bwd", grid=(pairs, T // B),
        in_specs=[qblk, pl.BlockSpec((T, B), lambda p, i: (0, pairs + p)),
                  pl.BlockSpec((T, B), lambda p, i: (0, 2 * pairs + p)), qblk, qblk,
                  pl.BlockSpec(upre.shape, lambda p, i: (0, 0)), pl.BlockSpec(uexc.shape, lambda p, i: (0, 0)),
                  pl.BlockSpec(memory_space=pltpu.SMEM)],
        out_specs=[qblk, full, full],
        out_shape=[jax.ShapeDtypeStruct((T, SB_WIDTH), F32)] * 3,
        compiler_params=_params(("arbitrary", "arbitrary")),
    )(sqkv, sqkv, sqkv, do, tl, upre, uexc, first_block)


def _mixer_out(o_hg, hg, o_sb, g_hg, g_sb):
    n_hg, r_hg = _rms(o_hg)
    s_hg = _sigmoid(hg)
    n_sb, r_sb = _rms(o_sb)
    return dict(n_hg=n_hg, r_hg=r_hg, s_hg=s_hg, n_sb=n_sb, r_sb=r_sb,
                y_hg=n_hg * g_hg * (hg * s_hg), y_sb=n_sb * g_sb)


def mix_out_fwd(o_hg, proj_h, o_sb, x, norms, g_post, w_out):
    T = x.shape[0]

    def body(ohg_ref, hg_ref, osb_ref, x_ref, nrm_ref, gp_ref, w_hbm, cat_ref, mix_ref, h1_ref, w_vmem):
        _load_once(w_hbm, w_vmem)
        nrm = nrm_ref[...]
        m = _mixer_out(ohg_ref[...], hg_ref[...], osb_ref[...], nrm[:, :HG_WIDTH], nrm[:, HG_WIDTH:])
        cat_ref[:, :HG_WIDTH] = m["y_hg"].astype(BF16)
        cat_ref[:, HG_WIDTH:] = m["y_sb"].astype(BF16)
        mix = jnp.dot(cat_ref[...], w_vmem[...], preferred_element_type=F32)
        mix_ref[...] = mix
        mh, _ = _rms(mix)
        h1_ref[...] = x_ref[...] + mh * gp_ref[...]

    return pl.pallas_call(
        body, name="mix_out_fwd", grid=(T // ROW_TILE,),
        in_specs=[_row_spec(HG_WIDTH), _row_spec(HG_WIDTH, 3), _row_spec(SB_WIDTH), _row_spec(D_MODEL),
                  _full_spec((1, D_MODEL)), _full_spec((1, D_MODEL)), ANY_SPEC],
        out_specs=[_row_spec(D_MODEL)] * 3,
        out_shape=[jax.ShapeDtypeStruct((T, D_MODEL), BF16), jax.ShapeDtypeStruct((T, D_MODEL), F32),
                   jax.ShapeDtypeStruct((T, D_MODEL), F32)],
        scratch_shapes=[pltpu.VMEM(w_out.shape, BF16)],
        compiler_params=_params(("arbitrary",)),
    )(o_hg, proj_h, o_sb, x, norms, g_post, w_out)


def ffn_fwd(h1, g_pre, g_post, w_gu, w_down):
    T = h1.shape[0]
    pw = w_gu.shape[2]

    def body(h1_ref, gpre_ref, gpost_ref, wgu_hbm, wd_hbm, u2_ref, gu_ref, act_ref, y_ref, h2_ref,
             wgu_vmem, wd_vmem, gu_s):
        _load_once(wgu_hbm, wgu_vmem)
        _load_once(wd_hbm, wd_vmem)
        h1v = h1_ref[...]
        hh, _ = _rms(h1v)
        u2 = (hh * gpre_ref[...]).astype(BF16)
        u2_ref[...] = u2
        for q in range(N_CHIPS):
            gu_s[:, pw * q:pw * (q + 1)] = jnp.dot(u2, wgu_vmem[q], preferred_element_type=F32)
        gu_ref[...] = gu_s[...].astype(BF16)
        gate = gu_s[:, :D_FF]
        act = (gate * _sigmoid(gate) * gu_s[:, D_FF:]).astype(BF16)
        act_ref[...] = act
        y = jnp.dot(act, wd_vmem[...], preferred_element_type=F32)
        y_ref[...] = y
        yh, _ = _rms(y)
        h2_ref[...] = h1v + yh * gpost_ref[...]

    return pl.pallas_call(
        body, name="ffn_fwd", grid=(T // ROW_TILE,),
        in_specs=[_row_spec(D_MODEL), _full_spec((1, D_MODEL)), _full_spec((1, D_MODEL)), ANY_SPEC, ANY_SPEC],
        out_specs=[_row_spec(D_MODEL), _row_spec(2 * D_FF), _row_spec(D_FF), _row_spec(D_MODEL), _row_spec(D_MODEL)],
        out_shape=[jax.ShapeDtypeStruct((T, D_MODEL), BF16), jax.ShapeDtypeStruct((T, 2 * D_FF), BF16),
                   jax.ShapeDtypeStruct((T, D_FF), BF16), jax.ShapeDtypeStruct((T, D_MODEL), F32),
                   jax.ShapeDtypeStruct((T, D_MODEL), F32)],
        scratch_shapes=[pltpu.VMEM(w_gu.shape, BF16), pltpu.VMEM(w_down.shape, BF16),
                        pltpu.VMEM((ROW_TILE, 2 * D_FF), F32)],
        compiler_params=_params(("arbitrary",)),
    )(h1, g_pre, g_post, w_gu, w_down)


def ple_loss(h2, p, target, w_ple, w_pg):
    T = h2.shape[0]
    pw = w_ple.shape[2]

    def body(h2_ref, p_ref, t_ref, wple_hbm, wpg_hbm, de_ref, ds_ref, dh2_ref, h2b_ref, pb_ref, pack_ref,
             wple_vmem, wpg_vmem, e_s):
        _load_once(wple_hbm, wple_vmem)
        _load_once(wpg_hbm, wpg_vmem)
        _zero_first(pack_ref)
        h2v = h2_ref[...]
        h2b = h2v.astype(BF16)
        h2b_ref[...] = h2b
        pb = p_ref[...].astype(BF16)
        pb_ref[...] = pb
        for q in range(N_CHIPS):
            e_s[:, pw * q:pw * (q + 1)] = jnp.dot(pb, wple_vmem[q], preferred_element_type=F32)
        e = e_s[...]
        sig = _sigmoid(jnp.dot(h2b, wpg_vmem[...], preferred_element_type=F32))
        err = h2v + e * sig - t_ref[...]
        part = 0.5 * jnp.sum(jnp.mean(err * err, axis=-1, keepdims=True), axis=0, keepdims=True)
        lane = lax.broadcasted_iota(jnp.int32, (1, D_MODEL), 1)
        pack_ref[ROW_LOSS:ROW_LOSS + 1, :] += jnp.where(lane == 0, part, 0.0)
        dh3 = err * (1.0 / D_MODEL)
        de_ref[...] = (dh3 * sig).astype(BF16)
        ds = (dh3 * e * sig * (1.0 - sig)).astype(BF16)
        ds_ref[...] = ds
        dh2_ref[...] = dh3 + _dot_nt(ds, wpg_vmem[...])

    return pl.pallas_call(
        body, name="ple_loss", grid=(T // ROW_TILE,),
        in_specs=[_row_spec(D_MODEL), _row_spec(p.shape[1]), _row_spec(D_MODEL), ANY_SPEC, ANY_SPEC],
        out_specs=[_row_spec(D_MODEL), _row_spec(D_MODEL), _row_spec(D_MODEL), _row_spec(D_MODEL),
                   _row_spec(p.shape[1]), PACK_SPEC],
        out_shape=[jax.ShapeDtypeStruct((T, D_MODEL), BF16), jax.ShapeDtypeStruct((T, D_MODEL), BF16),
                   jax.ShapeDtypeStruct((T, D_MODEL), F32), jax.ShapeDtypeStruct((T, D_MODEL), BF16),
                   jax.ShapeDtypeStruct(p.shape, BF16), jax.ShapeDtypeStruct((8, D_MODEL), F32)],
        scratch_shapes=[pltpu.VMEM(w_ple.shape, BF16), pltpu.VMEM(w_pg.shape, BF16), pltpu.VMEM((ROW_TILE, D_MODEL), F32)],
        compiler_params=_params(("arbitrary",)),
    )(h2, p, target, w_ple, w_pg)


def ffn_bwd(dh2, y, h1, gu, g_pre, g_post, w_gu, w_down):
    T = h1.shape[0]
    pw = w_gu.shape[2]

    def body(dh2_ref, y_ref, h1_ref, gu_ref, gpre_ref, gpost_ref, wgu_hbm, wd_hbm, dy_ref, dgu_ref, dh1_ref, pack_ref,
             wgu_vmem, wd_vmem):
        _load_once(wgu_hbm, wgu_vmem)
        _load_once(wd_hbm, wd_vmem)
        _zero_first(pack_ref)
        dh2v = dh2_ref[...]
        yh, ry = _rms(y_ref[...])
        dy, dw = _rms_bwd(dh2v, yh, ry, gpost_ref[...])
        pack_ref[ROW_FFN_POST:ROW_FFN_POST + 1, :] += _colsum(dw)
        dyb = dy.astype(BF16)
        dy_ref[...] = dyb
        dact = _dot_nt(dyb, wd_vmem[...])
        gate = gu_ref[:, :D_FF].astype(F32)
        up = gu_ref[:, D_FF:].astype(F32)
        sg = _sigmoid(gate)
        dgu_ref[:, :D_FF] = (dact * up * (sg * (1.0 + gate * (1.0 - sg)))).astype(BF16)
        dgu_ref[:, D_FF:] = (dact * gate * sg).astype(BF16)
        du2 = _dot_nt(dgu_ref[:, :pw], wgu_vmem[0])
        for q in range(1, N_CHIPS):
            du2 = du2 + _dot_nt(dgu_ref[:, pw * q:pw * (q + 1)], wgu_vmem[q])
        hh, rh = _rms(h1_ref[...])
        dh, dw = _rms_bwd(du2, hh, rh, gpre_ref[...])
        pack_ref[ROW_FFN_PRE:ROW_FFN_PRE + 1, :] += _colsum(dw)
        dh1_ref[...] = dh2v + dh

    return pl.pallas_call(
        body, name="ffn_bwd", grid=(T // ROW_TILE,),
        in_specs=[_row_spec(D_MODEL), _row_spec(D_MODEL), _row_spec(D_MODEL), _row_spec(2 * D_FF),
                  _full_spec((1, D_MODEL)), _full_spec((1, D_MODEL)), ANY_SPEC, ANY_SPEC],
        out_specs=[_row_spec(D_MODEL), _row_spec(2 * D_FF), _row_spec(D_MODEL), PACK_SPEC],
        out_shape=[jax.ShapeDtypeStruct((T, D_MODEL), BF16), jax.ShapeDtypeStruct((T, 2 * D_FF), BF16),
                   jax.ShapeDtypeStruct((T, D_MODEL), F32), jax.ShapeDtypeStruct((8, D_MODEL), F32)],
        scratch_shapes=[pltpu.VMEM(w_gu.shape, BF16), pltpu.VMEM(w_down.shape, BF16)],
        compiler_params=_params(("arbitrary",)),
    )(dh2, y, h1, gu, g_pre, g_post, w_gu, w_down)


def mix_out_bwd(dh1, mix, o_hg, proj_h, o_sb, norms, g_post, w_out, comm=None):
    T = dh1.shape[0]

    def body(dh1_ref, mix_ref, ohg_ref, hg_ref, osb_ref, nrm_ref, gp_ref, w_hbm, dmix_ref, dohg_ref, dhg_ref, dosb_ref,
             pack_ref, w_vmem):
        _load_once(w_hbm, w_vmem)
        _zero_first(pack_ref)
        mh, rm = _rms(mix_ref[...])
        dmix, dw = _rms_bwd(dh1_ref[...], mh, rm, gp_ref[...])
        pack_ref[ROW_ATTN_POST:ROW_ATTN_POST + 1, :] += _colsum(dw)
        dmb = dmix.astype(BF16)
        dmix_ref[...] = dmb
        dcat = _dot_nt(dmb, w_vmem[...])
        nrm = nrm_ref[...]
        g_hg, g_sb = nrm[:, :HG_WIDTH], nrm[:, HG_WIDTH:]
        hg = hg_ref[...]
        m = _mixer_out(ohg_ref[...], hg, osb_ref[...], g_hg, g_sb)
        d_hg = dcat[:, :HG_WIDTH]
        silu = hg * m["s_hg"]
        dhg_ref[...] = d_hg * (m["n_hg"] * g_hg) * (m["s_hg"] * (1.0 + hg * (1.0 - m["s_hg"])))
        dx, dw = _rms_bwd(d_hg * silu, m["n_hg"], m["r_hg"], g_hg)
        dohg_ref[...] = dx
        pack_ref[ROW_MIX_NORMS:ROW_MIX_NORMS + 1, :HG_WIDTH] += _colsum(dw)
        dx, dw = _rms_bwd(dcat[:, HG_WIDTH:], m["n_sb"], m["r_sb"], g_sb)
        dosb_ref[...] = dx
        pack_ref[ROW_MIX_NORMS:ROW_MIX_NORMS + 1, HG_WIDTH:] += _colsum(dw)

    return _pallas(
        body, comm=comm, edge=_grid_edge(T // ROW_TILE), name="mix_out_bwd", grid=(T // ROW_TILE,),
        in_specs=[_row_spec(D_MODEL), _row_spec(D_MODEL), _row_spec(HG_WIDTH), _row_spec(HG_WIDTH, 3), _row_spec(SB_WIDTH),
                  _full_spec((1, D_MODEL)), _full_spec((1, D_MODEL)), ANY_SPEC],
        out_specs=[_row_spec(D_MODEL), _row_spec(HG_WIDTH), _row_spec(HG_WIDTH), _row_spec(SB_WIDTH), PACK_SPEC],
        out_shape=[jax.ShapeDtypeStruct((T, D_MODEL), BF16), jax.ShapeDtypeStruct((T, HG_WIDTH), F32),
                   jax.ShapeDtypeStruct((T, HG_WIDTH), F32), jax.ShapeDtypeStruct((T, SB_WIDTH), F32),
                   jax.ShapeDtypeStruct((8, D_MODEL), F32)],
        scratch_shapes=[pltpu.VMEM(w_out.shape, BF16)],
        compiler_params=_params(("arbitrary",)),
    )(dh1, mix, o_hg, proj_h, o_sb, norms, g_post, w_out)


def in_proj_bwd(parts, x, dh1, g_pre, w_in):
    T = x.shape[0]
    pw = w_in.shape[2]
    n_parts = len(parts)

    def body(*refs):
        part_refs = refs[:n_parts]
        x_ref, dh1_ref, g_ref, w_hbm, dproj_ref, dx_ref, pack_ref, w_vmem = refs[n_parts:]
        _load_once(w_hbm, w_vmem)
        _zero_first(pack_ref)
        for n, ref in enumerate(part_refs):
            dproj_ref[:, HG_WIDTH * n:HG_WIDTH * (n + 1)] = ref[...].astype(BF16)
        du = _dot_nt(dproj_ref[:, :pw], w_vmem[0])
        for q in range(1, N_CHIPS):
            du = du + _dot_nt(dproj_ref[:, pw * q:pw * (q + 1)], w_vmem[q])
        xh, r = _rms(x_ref[...])
        dx, dw = _rms_bwd(du, xh, r, g_ref[...])
        pack_ref[ROW_ATTN_PRE:ROW_ATTN_PRE + 1, :] += _colsum(dw)
        dx_ref[...] = dh1_ref[...] + dx

    return pl.pallas_call(
        body, name="in_proj_bwd", grid=(T // ROW_TILE,),
        in_specs=[_row_spec(HG_WIDTH)] * n_parts + [_row_spec(D_MODEL), _row_spec(D_MODEL), _full_spec((1, D_MODEL)), ANY_SPEC],
        out_specs=[_row_spec(n_parts * HG_WIDTH), _row_spec(D_MODEL), PACK_SPEC],
        out_shape=[jax.ShapeDtypeStruct((T, n_parts * HG_WIDTH), BF16), jax.ShapeDtypeStruct((T, D_MODEL), F32),
                   jax.ShapeDtypeStruct((8, D_MODEL), F32)],
        scratch_shapes=[pltpu.VMEM(w_in.shape, BF16)],
        compiler_params=_params(("arbitrary",)),
    )(*parts, x, dh1, g_pre, w_in)


def weight_grad(a, g, name, *, tm, tn, tk=512, col_pieces=False):
    T, M = a.shape
    N = g.shape[1]
    steps = T // tk

    def body(a_ref, g_ref, o_ref):
        @pl.when(pl.program_id(2) == 0)
        def _():
            o_ref[...] = jnp.zeros(o_ref.shape, F32)

        o_ref[...] += _dot_tn(a_ref[...], g_ref[...]).reshape(o_ref.shape)

    if col_pieces:
        out_shape = jax.ShapeDtypeStruct((N // tn, M, tn), F32)
        out_spec = pl.BlockSpec((1, tm, tn), lambda i, j, k: (j, i, 0))
    else:
        out_shape = jax.ShapeDtypeStruct((M, N), F32)
        out_spec = pl.BlockSpec((tm, tn), lambda i, j, k: (i, j))
    return pl.pallas_call(
        body, name=name, grid=(M // tm, N // tn, steps),
        in_specs=[pl.BlockSpec((tk, tm), lambda i, j, k: (k, i)), pl.BlockSpec((tk, tn), lambda i, j, k: (k, j))],
        out_specs=out_spec, out_shape=out_shape,
        compiler_params=_params(("arbitrary", "arbitrary", "arbitrary")),
    )(a, g)


def _place():
    x, y, c = lax.axis_index("x"), lax.axis_index("y"), lax.axis_index("c")
    chips = [(1 - x, y), (x, 1 - y), (1 - x, 1 - y)]
    return x, y, c, chips


def _chip_index(cx, cy):
    return 2 * cx + cy


def _rcopy(src, dst, send_sem, recv_sem, device):
    return pltpu.make_async_remote_copy(src_ref=src, dst_ref=dst, send_sem=send_sem, recv_sem=recv_sem,
                                        device_id=device, device_id_type=MESH)


def gather_weights(shards):
    n = len(shards)

    def body(*refs):
        ins, outs = refs[:n], refs[2 * n:3 * n]
        send_sems, recv_sems = refs[3 * n:]
        x, y, c, chips = _place()
        me = _chip_index(x, y)
        sibling = (x, y, 1 - c)

        def rows(w, core):
            half = ins[w].shape[0] // 2
            return pl.ds(core * half, half)

        sends = []
        for w in range(n):
            for j, chip in enumerate(chips):
                sends.append(_rcopy(ins[w].at[rows(w, c)], outs[w].at[me, rows(w, c)],
                                    send_sems.at[6 * w + j], recv_sems.at[6 * w + j], (*chip, c)))
        for cp in sends:
            cp.start()
        passed = []
        for w in range(n):
            for j, chip in enumerate(chips):
                block = outs[w].at[_chip_index(*chip), rows(w, c)]
                _rcopy(block, block, send_sems.at[6 * w + j], recv_sems.at[6 * w + j], (*chip, c)).wait_recv()
                cp = _rcopy(block, block, send_sems.at[6 * w + 3 + j], recv_sems.at[6 * w + 3 + j], sibling)
                cp.start()
                passed.append(cp)
        for w in range(n):
            for j, chip in enumerate(chips):
                block = outs[w].at[_chip_index(*chip), rows(w, 1 - c)]
                _rcopy(block, block, send_sems.at[6 * w + 3 + j], recv_sems.at[6 * w + 3 + j], sibling).wait_recv()
        for cp in sends + passed:
            cp.wait_send()

    filled = [jnp.broadcast_to(s[None], (N_CHIPS,) + s.shape) for s in shards]
    return pl.pallas_call(
        body, name="gather_weights",
        in_specs=[ANY_SPEC] * (2 * n), out_specs=[ANY_SPEC] * n,
        out_shape=[jax.ShapeDtypeStruct(f.shape, f.dtype) for f in filled],
        input_output_aliases={n + w: w for w in range(n)},
        scratch_shapes=[pltpu.SemaphoreType.DMA((6 * n,)), pltpu.SemaphoreType.DMA((6 * n,))],
    )(*shards, *filled)


def _run_comm(comm, name):
    c_in, c_out = len(comm.inputs), len(comm.out_shape)

    def body(*refs):
        parts = refs[:c_in], refs[c_in:c_in + c_out], refs[c_in + c_out:]
        comm.start(*parts)
        comm.finish(*parts)

    return pl.pallas_call(
        body, name=name, in_specs=[ANY_SPEC] * c_in, out_specs=[ANY_SPEC] * c_out, out_shape=comm.out_shape,
        scratch_shapes=comm.scratch, input_output_aliases=comm.aliases)(*comm.inputs)


def _dma_sems(count):
    return [pltpu.SemaphoreType.DMA((count,)), pltpu.SemaphoreType.DMA((count,))]


def gather_over_ici(shards):
    n = len(shards)

    def copies(ins, outs, sems):
        send_sems, recv_sems = sems
        x, y, c, chips = _place()
        me = _chip_index(x, y)
        pairs = []
        for w in range(n):
            half = shards[w].shape[0] // 2
            rows = pl.ds(c * half, half)
            for j, chip in enumerate(chips):
                k = 3 * w + j
                landed = outs[w].at[_chip_index(*chip), rows]
                pairs.append((_rcopy(ins[w].at[rows], outs[w].at[me, rows], send_sems.at[k], recv_sems.at[k], (*chip, c)),
                              _rcopy(landed, landed, send_sems.at[k], recv_sems.at[k], (*chip, c))))
        return pairs

    def start(*refs):
        for send, _ in copies(*refs):
            send.start()

    def finish(*refs):
        pairs = copies(*refs)
        for _, landed in pairs:
            landed.wait_recv()
        for send, _ in pairs:
            send.wait_send()

    filled = [jnp.broadcast_to(s[None], (N_CHIPS,) + s.shape) for s in shards]
    return Comm(list(shards) + filled, [jax.ShapeDtypeStruct(f.shape, f.dtype) for f in filled],
                {n + w: w for w in range(n)}, _dma_sems(3 * n), start, finish)


def gather_over_d2d(landed):
    n = len(landed)

    def copies(ins, outs, sems):
        send_sems, recv_sems = sems
        x, y, c, chips = _place()
        sibling = (x, y, 1 - c)
        pairs = []
        for w in range(n):
            half = landed[w].shape[1] // 2
            for j, chip in enumerate(chips):
                k = 3 * w + j
                mine = outs[w].at[_chip_index(*chip), pl.ds(c * half, half)]
                theirs = outs[w].at[_chip_index(*chip), pl.ds((1 - c) * half, half)]
                pairs.append((_rcopy(mine, mine, send_sems.at[k], recv_sems.at[k], sibling),
                              _rcopy(theirs, theirs, send_sems.at[k], recv_sems.at[k], sibling)))
        return pairs

    def start(*refs):
        for send, _ in copies(*refs):
            send.start()

    def finish(*refs):
        pairs = copies(*refs)
        for _, arrived in pairs:
            arrived.wait_recv()
        for send, _ in pairs:
            send.wait_send()

    return Comm(list(landed), [jax.ShapeDtypeStruct(a.shape, a.dtype) for a in landed], {w: w for w in range(n)},
                _dma_sems(3 * n), start, finish)


def core_halves(grads):
    n = len(grads)

    def copies(ins, outs, sems):
        send_sems, recv_sems = sems
        x, y, c, _ = _place()
        out = []
        for w in range(n):
            half = grads[w].shape[1] // 2
            out.append(_rcopy(ins[w].at[:, pl.ds((1 - c) * half, half), :], outs[w],
                              send_sems.at[w], recv_sems.at[w], (x, y, 1 - c)))
        return out

    def start(*refs):
        for cp in copies(*refs):
            cp.start()

    def finish(*refs):
        for cp in copies(*refs):
            cp.wait()

    return Comm(list(grads), [jax.ShapeDtypeStruct((g.shape[0], g.shape[1] // 2, g.shape[2]), g.dtype) for g in grads],
                {}, _dma_sems(n), start, finish)


def chip_partials(partials):
    n = len(partials)

    def copies(ins, outs, sems):
        send_sems, recv_sems = sems
        x, y, c, chips = _place()
        me = _chip_index(x, y)
        pairs = []
        for w in range(n):
            for j, chip in enumerate(chips):
                k = 3 * w + j
                landed = outs[w].at[_chip_index(*chip)]
                pairs.append((_rcopy(ins[w].at[_chip_index(*chip)], outs[w].at[me], send_sems.at[k], recv_sems.at[k],
                                     (*chip, c)),
                              _rcopy(landed, landed, send_sems.at[k], recv_sems.at[k], (*chip, c))))
        return pairs

    def start(*refs):
        for send, _ in copies(*refs):
            send.start()

    def finish(*refs):
        pairs = copies(*refs)
        for _, landed in pairs:
            landed.wait_recv()
        for send, _ in pairs:
            send.wait_send()

    me = _chip_index(lax.axis_index("x"), lax.axis_index("y"))
    filled = [jnp.broadcast_to(lax.dynamic_index_in_dim(p, me, 0, keepdims=True), p.shape) for p in partials]
    return Comm(list(partials) + filled, [jax.ShapeDtypeStruct(p.shape, p.dtype) for p in partials],
                {n + w: w for w in range(n)}, _dma_sems(3 * n), start, finish)


def join_core_halves(grads):
    n = len(grads)

    def body(*refs):
        outs = refs[n:2 * n]
        send_sems, recv_sems = refs[2 * n:]
        x, y, c, _ = _place()
        sibling = (x, y, 1 - c)
        copies = []
        for w in range(n):
            half = outs[w].shape[0] // 2
            mine = outs[w].at[pl.ds(c * half, half), :]
            copies.append(_rcopy(mine, mine, send_sems.at[w], recv_sems.at[w], sibling))
        for cp in copies:
            cp.start()
        for w in range(n):
            half = outs[w].shape[0] // 2
            theirs = outs[w].at[pl.ds((1 - c) * half, half), :]
            _rcopy(theirs, theirs, send_sems.at[w], recv_sems.at[w], sibling).wait_recv()
        for cp in copies:
            cp.wait_send()

    return pl.pallas_call(
        body, name="join_core_halves",
        in_specs=[ANY_SPEC] * n, out_specs=[ANY_SPEC] * n,
        out_shape=[jax.ShapeDtypeStruct(g.shape, g.dtype) for g in grads],
        input_output_aliases={w: w for w in range(n)},
        scratch_shapes=[pltpu.SemaphoreType.DMA((n,)), pltpu.SemaphoreType.DMA((n,))],
    )(*grads)


def _elementwise_rows(rows, cap=512):
    for t in range(min(rows, cap), 0, -8):
        if rows % t == 0 and t % 16 == 0:
            return t
    return rows


def add_core_halves(grad, got, core, name):
    _, rows, cols = got.shape
    tr = _elementwise_rows(rows)
    nt = rows // tr

    def body(core_ref, a_ref, b_ref, o_ref):
        o_ref[...] = (a_ref[...] + b_ref[...]).astype(BF16)

    spec = pl.BlockSpec((1, tr, cols), lambda q, i, core_ref: (q, i, 0))
    own = pl.BlockSpec((1, tr, cols), lambda q, i, core_ref: (q, core_ref[0] * nt + i, 0))
    return pl.pallas_call(
        body, name=name,
        grid_spec=pltpu.PrefetchScalarGridSpec(num_scalar_prefetch=1, grid=(N_CHIPS, nt), in_specs=[own, spec],
                                               out_specs=spec),
        out_shape=jax.ShapeDtypeStruct(got.shape, BF16),
        compiler_params=_params(("arbitrary", "arbitrary")),
    )(core, grad, got)


def add_chip_partials(parts, core, name):
    _, rows, cols = parts.shape
    tr = _elementwise_rows(rows)
    nt = rows // tr

    def body(core_ref, p_ref, o_ref):
        acc = p_ref[0].astype(F32)
        for q in range(1, N_CHIPS):
            acc = acc + p_ref[q].astype(F32)
        o_ref[...] = acc

    return pl.pallas_call(
        body, name=name,
        grid_spec=pltpu.PrefetchScalarGridSpec(
            num_scalar_prefetch=1, grid=(nt,),
            in_specs=[pl.BlockSpec((N_CHIPS, tr, cols), lambda i, core_ref: (0, i, 0))],
            out_specs=pl.BlockSpec((tr, cols), lambda i, core_ref: (core_ref[0] * nt + i, 0))),
        out_shape=jax.ShapeDtypeStruct((2 * rows, cols), F32),
        compiler_params=_params(("arbitrary",)),
    )(core, parts)


def _adamw_math(w, g, m, v):
    m = ADAM_B1 * m + (1.0 - ADAM_B1) * g
    v = ADAM_B2 * v + (1.0 - ADAM_B2) * (g * g)
    m_hat = m / (1.0 - ADAM_B1 ** ADAM_STEP)
    v_hat = v / (1.0 - ADAM_B2 ** ADAM_STEP)
    delta = -ADAM_LR * (m_hat / (jnp.sqrt(v_hat) + ADAM_EPS) + ADAM_WD * w)
    return delta, m, v


def adamw(w, g, m, v, name):
    rows, cols = w.shape
    tr = _elementwise_rows(rows, 256)

    def body(w_ref, g_ref, m_ref, v_ref, d_ref, nm_ref, nv_ref):
        d, nm, nv = _adamw_math(w_ref[...], g_ref[...], m_ref[...], v_ref[...])
        d_ref[...] = d
        nm_ref[...] = nm
        nv_ref[...] = nv

    spec = pl.BlockSpec((tr, cols), lambda i: (i, 0))
    return pl.pallas_call(
        body, name=name, grid=(rows // tr,), in_specs=[spec] * 4, out_specs=[spec] * 3,
        out_shape=[jax.ShapeDtypeStruct((rows, cols), F32)] * 3,
        compiler_params=_params(("arbitrary",)),
    )(w, g, m, v)


def reduce_small(packs, w, m, v):
    n = len(packs)
    n_dev = 8
    flips = [(fx, fy, fc) for fx in (0, 1) for fy in (0, 1) for fc in (0, 1)][1:]

    def body(*refs):
        pack_refs = refs[:n]
        w_ref, m_ref, v_ref, g_out, d_out, m_out, v_out, mine, slots, send_sems, recv_sems = refs[n:]
        x, y, c, _ = _place()
        me = 4 * x + 2 * y + c
        acc = pack_refs[0][...]
        for ref in pack_refs[1:]:
            acc = acc + ref[...]
        mine[...] = acc
        sends = []
        for k, (fx, fy, fc) in enumerate(flips):
            peer = (x ^ fx, y ^ fy, c ^ fc)
            sends.append(_rcopy(mine, slots.at[me], send_sems.at[k], recv_sems.at[me], peer))
        for cp in sends:
            cp.start()
        slots[me] = acc
        for fx, fy, fc in flips:
            src = 4 * (x ^ fx) + 2 * (y ^ fy) + (c ^ fc)
            _rcopy(mine, slots.at[src], send_sems.at[0], recv_sems.at[src], (x, y, c)).wait_recv()
        for cp in sends:
            cp.wait_send()
        total = slots[0]
        for d in range(1, n_dev):
            total = total + slots[d]
        g_out[...] = total
        d, nm, nv = _adamw_math(w_ref[...], total, m_ref[...], v_ref[...])
        d_out[...] = d
        m_out[...] = nm
        v_out[...] = nv

    vm = pl.BlockSpec(memory_space=pltpu.VMEM)
    return pl.pallas_call(
        body, name="reduce_small",
        in_specs=[vm] * (n + 3), out_specs=[vm] * 4,
        out_shape=[jax.ShapeDtypeStruct((8, D_MODEL), F32)] * 4,
        scratch_shapes=[pltpu.VMEM((8, D_MODEL), F32), pltpu.VMEM((n_dev, 8, D_MODEL), F32),
                        pltpu.SemaphoreType.DMA((len(flips),)), pltpu.SemaphoreType.DMA((n_dev,))],
    )(*packs, w, m, v)


def _pack_small(attn_pre, gamma, hg_norm, sb_norm, attn_post, ffn_pre, ffn_post):
    rows = [attn_pre, gamma.reshape(1, D_MODEL), jnp.concatenate([hg_norm, sb_norm], axis=1), attn_post, ffn_pre, ffn_post,
            jnp.zeros((2, D_MODEL), F32)]
    return jnp.concatenate(rows, axis=0)


def _unpack_small(pack):
    return (pack[ROW_ATTN_PRE:ROW_ATTN_PRE + 1], pack[ROW_GAMMA].reshape(2, HG_WIDTH),
            pack[ROW_MIX_NORMS:ROW_MIX_NORMS + 1, :HG_WIDTH], pack[ROW_MIX_NORMS:ROW_MIX_NORMS + 1, HG_WIDTH:],
            pack[ROW_ATTN_POST:ROW_ATTN_POST + 1], pack[ROW_FFN_PRE:ROW_FFN_PRE + 1], pack[ROW_FFN_POST:ROW_FFN_POST + 1])


def kernel(x, p, attn_pre_norm, w_in, hg_lower_gamma, hg_out_norm, sb_out_norm, w_out, attn_post_norm, ffn_pre_norm, w_gate_up, w_down, ffn_post_norm, ple_proj, ple_gate, loss_target, m_attn_pre_norm, m_w_in, m_hg_lower_gamma, m_hg_out_norm, m_sb_out_norm, m_w_out, m_attn_post_norm, m_ffn_pre_norm, m_w_gate_up, m_w_down, m_ffn_post_norm, m_ple_proj, m_ple_gate, v_attn_pre_norm, v_w_in, v_hg_lower_gamma, v_hg_out_norm, v_sb_out_norm, v_w_out, v_attn_post_norm, v_ffn_pre_norm, v_w_gate_up, v_w_down, v_ffn_post_norm, v_ple_proj, v_ple_gate):
    x2 = x[0]
    p2 = p[0, 0]
    target = loss_target[0]
    big = dict(w_in=(w_in, m_w_in, v_w_in), w_out=(w_out, m_w_out, v_w_out), w_gate_up=(w_gate_up, m_w_gate_up, v_w_gate_up),
               w_down=(w_down, m_w_down, v_w_down), ple_proj=(ple_proj, m_ple_proj, v_ple_proj),
               ple_gate=(ple_gate, m_ple_gate, v_ple_gate))
    names = list(big)
    big = {k: tuple(a[0] for a in t) for k, t in big.items()}

    shard16 = {k: big[k][0].astype(BF16) for k in names}
    later = [k for k in names if k != "w_in"]
    w_in_full, = gather_weights([shard16["w_in"]])
    mix_norms = jnp.concatenate([hg_out_norm, sb_out_norm], axis=1)

    proj_h, sqkv, u1 = in_proj_fwd(x2, attn_pre_norm, w_in_full)
    o_sb, sb_totals, sb_first, *landed = sb_fwd(sqkv, comm=gather_over_ici([shard16[k] for k in later]))
    o_hg, states, *gathered = hgrn2_fwd(proj_h, hg_lower_gamma, comm=gather_over_d2d(landed))
    full = dict(zip(later, gathered), w_in=w_in_full)
    w_out_full = full["w_out"].reshape(D_MODEL, D_MODEL)
    w_down_full = full["w_down"].reshape(D_FF, D_MODEL)
    w_pg_full = full["ple_gate"].reshape(D_MODEL, D_MODEL)
    cat, mix, h1 = mix_out_fwd(o_hg, proj_h, o_sb, x2, mix_norms, attn_post_norm, w_out_full)
    u2, gu, act, y, h2 = ffn_fwd(h1, ffn_pre_norm, ffn_post_norm, full["w_gate_up"], w_down_full)

    core = lax.axis_index("c").astype(jnp.int32).reshape(1)
    de, ds, dh2, h2b, pb, pack_loss = ple_loss(h2, p2, target, full["ple_proj"], w_pg_full)
    dy, dgu, dh1, pack_ffn = ffn_bwd(dh2, y, h1, gu, ffn_pre_norm, ffn_post_norm, full["w_gate_up"], w_down_full)
    local = dict(
        w_gate_up=weight_grad(u2, dgu, "grad_w_gate_up", tm=D_MODEL, tn=full["w_gate_up"].shape[2], col_pieces=True),
        w_down=weight_grad(act, dy, "grad_w_down", tm=D_FF // 2, tn=D_MODEL).reshape(full["w_down"].shape),
        ple_proj=weight_grad(pb, de, "grad_ple_proj", tm=pb.shape[1], tn=full["ple_proj"].shape[2], col_pieces=True),
        ple_gate=weight_grad(h2b, ds, "grad_ple_gate", tm=D_MODEL, tn=D_MODEL).reshape(full["ple_gate"].shape),
    )
    early = list(local)
    dmix, do_hg, dhg, do_sb, pack_mix, *got = mix_out_bwd(
        dh1, mix, o_hg, proj_h, o_sb, mix_norms, attn_post_norm, w_out_full, comm=core_halves([local[k] for k in early]))
    partial = [add_core_halves(local[k], g, core, "add_core_halves_" + k) for k, g in zip(early, got)]
    dsq, dsk, dsv = sb_bwd(sqkv, do_sb, sb_totals, sb_first)
    dhq, dhf, dhi, pack_hg, *by_source = hgrn2_bwd(proj_h, hg_lower_gamma, states, do_hg, comm=chip_partials(partial))
    halves = {k: add_chip_partials(s, core, "add_chip_partials_" + k) for k, s in zip(early, by_source)}
    dproj, grad_x, pack_in = in_proj_bwd([dhq, dhf, dhi, dhg, dsq, dsk, dsv], x2, dh1, attn_pre_norm, full["w_in"])

    late = ["w_in", "w_out"]
    local["w_in"] = weight_grad(u1, dproj, "grad_w_in", tm=D_MODEL, tn=full["w_in"].shape[2], col_pieces=True)
    local["w_out"] = weight_grad(cat, dmix, "grad_w_out", tm=D_MODEL, tn=D_MODEL).reshape(full["w_out"].shape)
    got = _run_comm(core_halves([local[k] for k in late]), "exchange_core_halves")
    partial = [add_core_halves(local[k], g, core, "add_core_halves_" + k) for k, g in zip(late, got)]
    by_source = _run_comm(chip_partials(partial), "exchange_chip_partials")
    halves.update({k: add_chip_partials(s, core, "add_chip_partials_" + k) for k, s in zip(late, by_source)})
    grads = dict(zip(names, join_core_halves([halves[k] for k in names])))

    upd = {k: adamw(big[k][0], grads[k], big[k][1], big[k][2], "adamw_" + k) for k in names}

    small = reduce_small(
        [pack_loss, pack_ffn, pack_mix, pack_hg, pack_in],
        _pack_small(attn_pre_norm, hg_lower_gamma, hg_out_norm, sb_out_norm, attn_post_norm, ffn_pre_norm, ffn_post_norm),
        _pack_small(m_attn_pre_norm, m_hg_lower_gamma, m_hg_out_norm, m_sb_out_norm, m_attn_post_norm, m_ffn_pre_norm, m_ffn_post_norm),
        _pack_small(v_attn_pre_norm, v_hg_lower_gamma, v_hg_out_norm, v_sb_out_norm, v_attn_post_norm, v_ffn_pre_norm, v_ffn_post_norm),
    )
    loss = small[0][ROW_LOSS, 0]
    s_grad, s_delta, s_m, s_v = (_unpack_small(t) for t in small)

    def ordered(small_vals, big_vals):
        a_pre, gam, hg_n, sb_n, a_post, f_pre, f_post = small_vals
        b = {k: big_vals[k][None] for k in names}
        return (a_pre, b["w_in"], gam, hg_n, sb_n, b["w_out"], a_post, f_pre, b["w_gate_up"], b["w_down"], f_post,
                b["ple_proj"], b["ple_gate"])

    return (loss, grad_x[None],
            *ordered(s_grad, grads),
            *ordered(s_delta, {k: upd[k][0] for k in names}),
            *ordered(s_m, {k: upd[k][1] for k in names}),
            *ordered(s_v, {k: upd[k][2] for k in names}))
```

```python
from typing import Callable, NamedTuple

import numpy as np
import jax
import jax.numpy as jnp
from jax import lax
from jax.experimental import pallas as pl
from jax.experimental.pallas import tpu as pltpu

F32 = jnp.float32
BF16 = jnp.bfloat16
MESH = pl.DeviceIdType.MESH

RMS_EPS = 1e-6
D_MODEL = 1024
HG_WIDTH = 512
HG_HEADS = 4
HG_DK = 128
HG_CHUNK = 64
HG_LEVELS = (32, 16, 8, 4, 2, 1)
SB_WIDTH = 512
SB_BLOCK = 128
SB_DH = 64
SB_SCALE = SB_DH ** -0.5
SB_UNDERFLOW_LOG = -87.5
SB_UNROLL = 2
SB_GROUP = 2
D_FF = 2816
N_CHIPS = 4
ROW_TILE = 256
V7X_VMEM_LIMIT = 56 * 1024 * 1024

ADAM_LR = 0.001
ADAM_B1 = 0.9
ADAM_B2 = 0.999
ADAM_EPS = 1e-08
ADAM_WD = 0.01
ADAM_STEP = 10

ROW_ATTN_PRE, ROW_GAMMA, ROW_MIX_NORMS, ROW_ATTN_POST, ROW_FFN_PRE, ROW_FFN_POST, ROW_LOSS = range(7)


def _params(sem=None, vmem=V7X_VMEM_LIMIT):
    return pltpu.CompilerParams(dimension_semantics=sem, vmem_limit_bytes=vmem)


def _dot(a, b):
    return jnp.dot(a.astype(BF16), b.astype(BF16), preferred_element_type=F32)


def _dot_nt(a, b):
    return lax.dot_general(a.astype(BF16), b.astype(BF16), (((1,), (1,)), ((), ())), preferred_element_type=F32)


def _dot_tn(a, b):
    return lax.dot_general(a.astype(BF16), b.astype(BF16), (((0,), (0,)), ((), ())), preferred_element_type=F32)


def _split(x):
    hi = x.astype(BF16)
    lo = (x - hi.astype(F32)).astype(BF16)
    return hi, lo


def _sum01_left(m01, x):
    hi, lo = _split(x)
    return jnp.dot(m01, hi, preferred_element_type=F32) + jnp.dot(m01, lo, preferred_element_type=F32)


def _sum01_right(x, m01):
    hi, lo = _split(x)
    return jnp.dot(hi, m01, preferred_element_type=F32) + jnp.dot(lo, m01, preferred_element_type=F32)


def _rms(x):
    r = lax.rsqrt(jnp.mean(x * x, axis=-1, keepdims=True) + RMS_EPS)
    return x * r, r


def _rms_bwd(dy, xhat, r, w):
    dxh = dy * w
    dx = r * (dxh - xhat * jnp.mean(dxh * xhat, axis=-1, keepdims=True))
    return dx, dy * xhat


def _sigmoid(x):
    return 1.0 / (1.0 + jnp.exp(-x))


def _neg_softplus(z):
    return -(jnp.maximum(z, 0.0) + jnp.log(1.0 + jnp.exp(-jnp.abs(z))))


def _colsum(x):
    return jnp.sum(x, axis=0, keepdims=True)


def _load_once(src_hbm, dst_vmem):
    @pl.when(pl.program_id(0) == 0)
    def _():
        pltpu.sync_copy(src_hbm, dst_vmem)


def _zero_first(ref):
    @pl.when(pl.program_id(0) == 0)
    def _():
        ref[...] = jnp.zeros(ref.shape, ref.dtype)


def _row_spec(width, col=0):
    return pl.BlockSpec((ROW_TILE, width), lambda i, col=col: (i, col))


def _full_spec(shape):
    return pl.BlockSpec(shape, lambda *_: (0,) * len(shape))


ANY_SPEC = pl.BlockSpec(memory_space=pl.ANY)
PACK_SPEC = _full_spec((8, D_MODEL))


class Comm(NamedTuple):
    inputs: list
    out_shape: list
    aliases: dict
    scratch: list
    start: Callable
    finish: Callable


def _pallas(body, *, comm=None, edge=None, in_specs, out_specs, out_shape, scratch_shapes=(), **kw):
    if comm is None:
        return pl.pallas_call(body, in_specs=in_specs, out_specs=out_specs, out_shape=out_shape,
                              scratch_shapes=scratch_shapes, **kw)
    n_in, n_out, n_scr = len(in_specs), len(out_specs), len(scratch_shapes)
    c_in, c_out = len(comm.inputs), len(comm.out_shape)

    def both(*refs):
        ins, c_ins = refs[:n_in], refs[n_in:n_in + c_in]
        outs = refs[n_in + c_in:n_in + c_in + n_out]
        c_outs = refs[n_in + c_in + n_out:n_in + c_in + n_out + c_out]
        rest = refs[n_in + c_in + n_out + c_out:]
        scr, c_scr = rest[:n_scr], rest[n_scr:]
        first, last = edge()

        @pl.when(first)
        def _():
            comm.start(c_ins, c_outs, c_scr)

        body(*ins, *outs, *scr)

        @pl.when(last)
        def _():
            comm.finish(c_ins, c_outs, c_scr)

    call = pl.pallas_call(
        both, in_specs=list(in_specs) + [ANY_SPEC] * c_in, out_specs=list(out_specs) + [ANY_SPEC] * c_out,
        out_shape=list(out_shape) + list(comm.out_shape), scratch_shapes=list(scratch_shapes) + list(comm.scratch),
        input_output_aliases={n_in + a: n_out + b for a, b in comm.aliases.items()}, **kw)
    return lambda *args: call(*args, *comm.inputs)


def _grid_edge(steps):
    return lambda: (pl.program_id(0) == 0, pl.program_id(0) == steps - 1)


def in_proj_fwd(x, g_pre, w_in):
    T = x.shape[0]
    pw = w_in.shape[2]

    def body(x_ref, g_ref, w_hbm, ph_ref, sqkv_ref, u_ref, w_vmem, proj_s):
        _load_once(w_hbm, w_vmem)
        xh, _ = _rms(x_ref[...])
        u = (xh * g_ref[...]).astype(BF16)
        u_ref[...] = u
        for q in range(N_CHIPS):
            proj_s[:, pw * q:pw * (q + 1)] = jnp.dot(u, w_vmem[q], preferred_element_type=F32)
        ph_ref[...] = proj_s[:, :4 * HG_WIDTH]
        sqkv_ref[:, :SB_WIDTH] = (proj_s[:, 4 * HG_WIDTH:4 * HG_WIDTH + SB_WIDTH] * SB_SCALE).astype(BF16)
        sqkv_ref[:, SB_WIDTH:] = proj_s[:, 4 * HG_WIDTH + SB_WIDTH:].astype(BF16)

    return pl.pallas_call(
        body, name="in_proj_fwd", grid=(T // ROW_TILE,),
        in_specs=[_row_spec(D_MODEL), _full_spec((1, D_MODEL)), ANY_SPEC],
        out_specs=[_row_spec(4 * HG_WIDTH), _row_spec(3 * SB_WIDTH), _row_spec(D_MODEL)],
        out_shape=[jax.ShapeDtypeStruct((T, 4 * HG_WIDTH), F32), jax.ShapeDtypeStruct((T, 3 * SB_WIDTH), BF16),
                   jax.ShapeDtypeStruct((T, D_MODEL), BF16)],
        scratch_shapes=[pltpu.VMEM(w_in.shape, BF16), pltpu.VMEM((ROW_TILE, N_CHIPS * pw), F32)],
        compiler_params=_params(("arbitrary",)),
    )(x, g_pre, w_in)


def _hg_sum_matrix():
    C = HG_CHUNK
    t = np.arange(C)[:, None]
    j = np.arange(C)[None, :]
    mats = [j <= t, j > t]
    for h in HG_LEVELS:
        start = (t // (2 * h)) * (2 * h)
        upper = (t & h) != 0
        mats.append(np.where(upper, (j >= start + h) & (j <= t), (j > t) & (j <= start + h - 1)))
    return np.concatenate(mats, 0).astype(np.float32)


def _hg_level_masks():
    C = HG_CHUNK
    t = lax.broadcasted_iota(jnp.int32, (C, C), 0)
    s = lax.broadcasted_iota(jnp.int32, (C, C), 1)
    x = t ^ s
    masks = [t == s]
    for h in HG_LEVELS:
        masks.append((x >= h) & (x < 2 * h) & (t > s))
    return masks


def _hg_gates(hq, hf, gamma):
    lb = 1.0 / (1.0 + jnp.exp(gamma[1:2, :] - gamma[0:1, :]))
    sq = _sigmoid(hq)
    q = hq * sq
    sig = _sigmoid(hf)
    nsig = _sigmoid(-hf)
    f = lb + (1.0 - lb) * sig
    k = (1.0 - lb) * nsig
    g = jnp.log(f)
    return q, k, g, dict(lb=lb, sq=sq, sig=sig, nsig=nsig, f=f)


def _hg_head_decays(A, h):
    C, K = HG_CHUNK, HG_DK
    sl = slice(K * h, K * (h + 1))
    blocks = [A[C * r:C * (r + 1), sl] for r in range(2 + len(HG_LEVELS))]
    return blocks[0], blocks[1], [None] + blocks[2:]


def _hg_products(q, k, levels):
    return [_dot_nt(q, k)] + [_dot_nt(q * a, k * a) for a in levels[1:]]


def _hg_select(prods, masks):
    sc = jnp.where(masks[0], prods[0], 0.0)
    for p, m in zip(prods[1:], masks[1:]):
        sc = jnp.where(m, p, sc)
    return sc


def hgrn2_fwd(proj_h, gamma, comm=None):
    T = proj_h.shape[0]
    C, K, H = HG_CHUNK, HG_DK, HG_HEADS
    n_chunks = T // C
    msum = jnp.asarray(_hg_sum_matrix(), BF16)

    def body(hq_ref, hf_ref, hi_ref, gam_ref, msum_ref, o_ref, st_ref, st_s):
        _zero_first(st_s)
        q, k, g, _ = _hg_gates(hq_ref[...], hf_ref[...], gam_ref[...])
        v = hi_ref[...]
        A = jnp.exp(_sum01_left(msum_ref[...], g))
        masks = _hg_level_masks()
        heads = []
        for h in range(H):
            sl = slice(K * h, K * (h + 1))
            ab, ar, levels = _hg_head_decays(A, h)
            heads.append((sl, ab, ar, levels, q[:, sl], k[:, sl], v[:, sl], st_s[h]))
        prods = [_hg_products(qh, kh, levels) for _, _, _, levels, qh, kh, _, _ in heads]
        inter = [_dot_nt(qh * ab, st) for _, ab, _, _, qh, _, _, st in heads]
        grown = [_dot_tn(vh, kh * ar) for _, _, ar, _, _, kh, vh, _ in heads]
        scores = [_hg_select(p, masks) for p in prods]
        for (sl, ab, _, _, _, _, vh, st), sc, o_inter, st_add in zip(heads, scores, inter, grown):
            o_ref[:, sl] = o_inter + _dot(sc, vh)
            st_new = st * ab[C - 1:C, :] + st_add
            st_s[sl.start // K] = st_new
            st_ref[0, sl.start // K] = st_new

    blk = lambda col: pl.BlockSpec((C, HG_WIDTH), lambda c, col=col: (c, col))
    return _pallas(
        body, comm=comm, edge=_grid_edge(n_chunks), name="hgrn2_fwd", grid=(n_chunks,),
        in_specs=[blk(0), blk(1), blk(2), _full_spec((2, HG_WIDTH)), _full_spec(msum.shape)],
        out_specs=[blk(0), pl.BlockSpec((1, H, K, K), lambda c: (c, 0, 0, 0))],
        out_shape=[jax.ShapeDtypeStruct((T, HG_WIDTH), F32), jax.ShapeDtypeStruct((n_chunks, H, K, K), F32)],
        scratch_shapes=[pltpu.VMEM((H, K, K), F32)],
        compiler_params=_params(("arbitrary",)),
    )(proj_h, proj_h, proj_h, gamma, msum)


def hgrn2_bwd(proj_h, gamma, states, do, comm=None):
    T = proj_h.shape[0]
    C, K, H = HG_CHUNK, HG_DK, HG_HEADS
    n_chunks = T // C
    n_sums = 2 + len(HG_LEVELS)
    msum = jnp.asarray(_hg_sum_matrix(), BF16)
    msum_t = jnp.asarray(_hg_sum_matrix().T, BF16)

    def body(hq_ref, hf_ref, hi_ref, do_ref, gam_ref, msum_ref, msum_t_ref, st_in_ref,
             dhq_ref, dhf_ref, dhi_ref, pack_ref, dst_s, dlb_s, dq_s, dk_s, de_s):
        step = pl.program_id(0)
        _zero_first(dst_s)
        _zero_first(dlb_s)
        _zero_first(pack_ref)
        hq = hq_ref[...]
        q, k, g, aux = _hg_gates(hq, hf_ref[...], gam_ref[...])
        v = hi_ref[...]
        do_all = do_ref[...]
        A = jnp.exp(_sum01_left(msum_ref[...], g))
        masks = _hg_level_masks()
        is_last_row = lax.broadcasted_iota(jnp.int32, (C, K), 0) == C - 1
        has_prev = (step < n_chunks - 1).astype(F32)
        heads = []
        for h in range(H):
            sl = slice(K * h, K * (h + 1))
            ab, ar, levels = _hg_head_decays(A, h)
            heads.append(dict(h=h, sl=sl, ab=ab, ar=ar, levels=levels, q=q[:, sl], k=k[:, sl], v=v[:, sl],
                              do=do_all[:, sl], st_in=st_in_ref[0, h] * has_prev, dst_out=dst_s[h]))
        for hd in heads:
            hd["prods"] = _hg_products(hd["q"], hd["k"], hd["levels"])
            hd["da"] = _dot_nt(hd["do"], hd["v"])
            hd["t1"] = hd["ab"] * _dot(hd["do"], hd["st_in"])
            hd["t2"] = hd["ar"] * _dot(hd["v"], hd["dst_out"])
            hd["dv_state"] = _dot_nt(hd["k"] * hd["ar"], hd["dst_out"])
            hd["dst_add"] = _dot_tn(hd["do"], hd["q"] * hd["ab"])
        for hd in heads:
            hd["sc"] = _hg_select(hd["prods"], masks)
            hd["dam"] = [jnp.where(m, hd["da"], 0.0) for m in masks]
        for hd in heads:
            qh, kh = hd["q"], hd["k"]
            hd["dq_parts"] = [_dot(hd["dam"][0], kh)] + [
                a * _dot(dam, kh * a) for a, dam in zip(hd["levels"][1:], hd["dam"][1:])]
            hd["dk_parts"] = [_dot_tn(hd["dam"][0], qh)] + [
                a * _dot_tn(dam, qh * a) for a, dam in zip(hd["levels"][1:], hd["dam"][1:])]
            hd["dv_intra"] = _dot_tn(hd["sc"], hd["do"])
        for hd in heads:
            h, sl, qh, kh, ab = hd["h"], hd["sl"], hd["q"], hd["k"], hd["ab"]
            decayed = _colsum(hd["st_in"] * hd["dst_out"]) * ab[C - 1:C, :]
            de_s[0:C, sl] = qh * hd["t1"] + jnp.where(is_last_row, decayed, 0.0)
            de_s[C:2 * C, sl] = kh * hd["t2"]
            dq = hd["t1"] + hd["dq_parts"][0]
            dk = hd["t2"] + hd["dk_parts"][0]
            for r, (t1, t2) in enumerate(zip(hd["dq_parts"][1:], hd["dk_parts"][1:])):
                dq = dq + t1
                dk = dk + t2
                de_s[C * (r + 2):C * (r + 3), sl] = qh * t1 + kh * t2
            dhi_ref[:, sl] = hd["dv_intra"] + hd["dv_state"]
            dst_s[h] = hd["dst_out"] * ab[C - 1:C, :] + hd["dst_add"]
            dq_s[:, sl] = dq
            dk_s[:, sl] = dk
        dg = _sum01_left(msum_t_ref[...], de_s[...])
        dk = dk_s[...]
        sq, lb = aux["sq"], aux["lb"]
        dhq_ref[...] = dq_s[...] * (sq * (1.0 + hq * (1.0 - sq)))
        common = dg / aux["f"] - dk
        dhf_ref[...] = (1.0 - lb) * aux["sig"] * aux["nsig"] * common
        dlb_s[...] += _colsum(aux["nsig"] * common)

        @pl.when(step == n_chunks - 1)
        def _():
            dgam = lb * (1.0 - lb) * dlb_s[...]
            pack_ref[ROW_GAMMA:ROW_GAMMA + 1, :HG_WIDTH] = dgam
            pack_ref[ROW_GAMMA:ROW_GAMMA + 1, HG_WIDTH:] = -dgam

    last = n_chunks - 1
    blk = lambda col: pl.BlockSpec((C, HG_WIDTH), lambda c, col=col: (last - c, col))
    return _pallas(
        body, comm=comm, edge=_grid_edge(n_chunks), name="hgrn2_bwd", grid=(n_chunks,),
        in_specs=[blk(0), blk(1), blk(2), blk(0), _full_spec((2, HG_WIDTH)), _full_spec(msum.shape),
                  _full_spec(msum_t.shape),
                  pl.BlockSpec((1, H, K, K), lambda c: (jnp.maximum(last - c - 1, 0), 0, 0, 0))],
        out_specs=[blk(0), blk(0), blk(0), PACK_SPEC],
        out_shape=[jax.ShapeDtypeStruct((T, HG_WIDTH), F32)] * 3 + [jax.ShapeDtypeStruct((8, D_MODEL), F32)],
        scratch_shapes=[pltpu.VMEM((H, K, K), F32), pltpu.VMEM((1, HG_WIDTH), F32), pltpu.VMEM((C, HG_WIDTH), F32),
                        pltpu.VMEM((C, HG_WIDTH), F32), pltpu.VMEM((n_sums * C, HG_WIDTH), F32)],
        compiler_params=_params(("arbitrary",)),
    )(proj_h, proj_h, proj_h, do, gamma, msum, msum_t, states)


def _sb_sum_matrix(inclusive):
    B = SB_BLOCK
    j = np.arange(B)[:, None]
    s = np.arange(B)[None, :]
    tri = (j >= s) if inclusive else (j > s)
    return np.concatenate([tri, np.ones((B, B), bool)], 1).astype(np.float32)


def _sb_prefix_matrix(inclusive):
    B = SB_BLOCK
    j = np.arange(B)[:, None]
    s = np.arange(B)[None, :]
    tri = (j <= s) if inclusive else (j < s)
    return np.concatenate([tri, np.ones((B, B), bool)], 1).astype(np.float32)


def _sb_iotas():
    shape = (SB_BLOCK, SB_BLOCK)
    return lax.broadcasted_iota(jnp.int32, shape, 0), lax.broadcasted_iota(jnp.int32, shape, 1)


def _sb_heads(q, first):
    heads = []
    for g in range(SB_GROUP):
        qg = q[:, SB_BLOCK * g:SB_BLOCK * (g + 1)]
        zero = jnp.zeros_like(qg)
        heads += [(g, jnp.where(first, qg, zero)), (g, jnp.where(first, zero, qg))]
    return heads


def _lanes(x, g):
    return x[:, SB_BLOCK * g:SB_BLOCK * (g + 1)]


def sb_fwd(sqkv, comm=None):
    T = sqkv.shape[0]
    B = SB_BLOCK
    W = SB_GROUP * B
    groups = SB_WIDTH // W
    usum = jnp.asarray(_sb_sum_matrix(False), BF16)

    def body(q_ref, k_ref, v_ref, u_ref, o_ref, tl_ref, first_ref):
        p, i = pl.program_id(0), pl.program_id(1)
        row, lane = _sb_iotas()
        first = lane < SB_DH
        heads = _sb_heads(q_ref[...], first)
        u = u_ref[...]

        def more(loop):
            n, reachable, _ = loop
            return (SB_UNROLL * n <= i) & (reachable > 0)

        def step(loop):
            n, _, state = loop
            blocks = []
            for sub in range(SB_UNROLL):
                j = i - SB_UNROLL * n - sub
                off = pl.multiple_of(jnp.maximum(j, 0) * B, B)
                valid = ((lane + j * B) < (row + i * B)) & (j >= 0)
                blocks.append((k_ref[pl.ds(off, B), :], v_ref[pl.ds(off, B), :], valid))
            z = [[_dot_nt(qh, _lanes(kj, g)) for g, qh in heads] for kj, _, _ in blocks]
            lnb = [[jnp.where(valid, _neg_softplus(zz), 0.0) for zz in zs] for zs, (_, _, valid) in zip(z, blocks)]
            sums = [[_sum01_right(x, u) for x in xs] for xs in lnb]
            out = []
            for h, (carry, acc) in enumerate(state):
                for sub, (_, vj, valid) in enumerate(blocks):
                    expo = z[sub][h] + lnb[sub][h] + carry + sums[sub][h][:, :B]
                    acc = acc + _dot(jnp.where(valid, jnp.exp(expo), 0.0), _lanes(vj, heads[h][0]))
                    carry = carry + sums[sub][h][:, B:]
                out.append((carry, acc))
            state = tuple(out)
            worst = state[0][0]
            for carry, _ in state[1:]:
                worst = jnp.maximum(worst, carry)
            reachable = (jnp.max(worst) > SB_UNDERFLOW_LOG).astype(jnp.int32)
            return n + 1, reachable, state

        zero = jnp.zeros((B, B), F32)
        done, _, state = lax.while_loop(
            more, step, (jnp.int32(0), jnp.int32(1), tuple((zero, zero) for _ in heads)))
        for g in range(SB_GROUP):
            (tot0, acc0), (tot1, acc1) = state[2 * g], state[2 * g + 1]
            o_ref[:, B * g:B * (g + 1)] = jnp.where(first, acc0, acc1)
            tl_ref[:, B * g:B * (g + 1)] = jnp.where(first, tot0, tot1)
        first_ref[p, i] = jnp.maximum(i + 1 - SB_UNROLL * done, 0)

    def edge():
        p, i = pl.program_id(0), pl.program_id(1)
        return (p == 0) & (i == 0), (p == groups - 1) & (i == T // B - 1)

    return _pallas(
        body, comm=comm, edge=edge, name="sb_fwd", grid=(groups, T // B),
        in_specs=[pl.BlockSpec((B, W), lambda p, i: (i, p)),
                  pl.BlockSpec((T, W), lambda p, i: (0, groups + p)),
                  pl.BlockSpec((T, W), lambda p, i: (0, 2 * groups + p)),
                  pl.BlockSpec(usum.shape, lambda p, i: (0, 0))],
        out_specs=[pl.BlockSpec((B, W), lambda p, i: (i, p))] * 2 + [pl.BlockSpec(memory_space=pltpu.SMEM)],
        out_shape=[jax.ShapeDtypeStruct((T, SB_WIDTH), F32)] * 2 + [jax.ShapeDtypeStruct((groups, T // B), jnp.int32)],
        compiler_params=_params(("arbitrary", "arbitrary")),
    )(sqkv, sqkv, sqkv, usum)


def sb_bwd(sqkv, do, tl, first_block):
    T = sqkv.shape[0]
    B = SB_BLOCK
    W = SB_GROUP * B
    groups = SB_WIDTH // W
    upre = jnp.asarray(_sb_prefix_matrix(True), BF16)
    uexc = jnp.asarray(_sb_prefix_matrix(False), BF16)

    def body(q_ref, k_ref, v_ref, do_ref, tl_ref, up_ref, ue_ref, first_ref, dq_ref, dk_ref, dv_ref):
        p, i = pl.program_id(0), pl.program_id(1)

        @pl.when(i == 0)
        def _():
            dk_ref[...] = jnp.zeros(dk_ref.shape, F32)
            dv_ref[...] = jnp.zeros(dv_ref.shape, F32)

        row, lane = _sb_iotas()
        first = lane < SB_DH
        do = do_ref[...]
        tl_all = tl_ref[...]
        heads = []
        for (g, qh), at in zip(_sb_heads(q_ref[...], first), (0, B - 1) * SB_GROUP):
            dog = _lanes(do, g)
            keep = first if at == 0 else jnp.logical_not(first)
            heads.append((g, qh, jnp.where(keep, dog, jnp.zeros_like(dog)).astype(BF16),
                          _lanes(tl_all, g)[:, at:at + 1]))
        up = up_ref[...]
        ue = ue_ref[...]
        start = first_ref[p, i]

        def step(n, state):
            blocks = []
            for sub in range(SB_UNROLL):
                j = start + SB_UNROLL * n + sub
                off = pl.multiple_of(jnp.minimum(j, i) * B, B)
                valid = (lane + j * B) < (row + i * B)
                blocks.append((off, k_ref[pl.ds(off, B), :], v_ref[pl.ds(off, B), :], valid))
            combos = [(s, h) for s in range(SB_UNROLL) for h in range(len(heads))]
            z = {(s, h): _dot_nt(heads[h][1], _lanes(blocks[s][1], heads[h][0])) for s, h in combos}
            da = {(s, h): _dot_nt(heads[h][2], _lanes(blocks[s][2], heads[h][0])) for s, h in combos}
            lnb = {c: jnp.where(blocks[c[0]][3], _neg_softplus(z[c]), 0.0) for c in combos}
            lb = {c: z[c] + lnb[c] for c in combos}
            sums = {c: _sum01_right(lnb[c], up) for c in combos}
            a, w = {}, {}
            seen = [st[0] for st in state]
            for s, h in combos:
                expo = lb[s, h] + (heads[h][3] - seen[h] - sums[s, h][:, :B])
                a[s, h] = jnp.where(blocks[s][3], jnp.exp(expo), 0.0)
                w[s, h] = a[s, h] * da[s, h]
                seen[h] = seen[h] + sums[s, h][:, B:]
            wsums = {c: _sum01_right(w[c], ue) for c in combos}
            dz = {}
            seen_w = [st[1] for st in state]
            for s, h in combos:
                beta = jnp.exp(lb[s, h])
                before = seen_w[h] + wsums[s, h][:, :B]
                dz[s, h] = jnp.where(blocks[s][3], w[s, h] * (1.0 - beta) - before * beta, 0.0)
                seen_w[h] = seen_w[h] + wsums[s, h][:, B:]
            dq = [st[2] for st in state]
            for s, h in combos:
                dq[h] = dq[h] + _dot(dz[s, h], _lanes(blocks[s][1], heads[h][0]))
            for s in range(SB_UNROLL):
                off = blocks[s][0]
                for g in range(SB_GROUP):
                    h0, h1 = 2 * g, 2 * g + 1
                    dk_ref[pl.ds(off, B), B * g:B * (g + 1)] += (_dot_tn(dz[s, h0], heads[h0][1])
                                                                 + _dot_tn(dz[s, h1], heads[h1][1]))
                    dv_ref[pl.ds(off, B), B * g:B * (g + 1)] += (_dot_tn(a[s, h0], heads[h0][2])
                                                                 + _dot_tn(a[s, h1], heads[h1][2]))
            return tuple(zip(seen, seen_w, dq))

        zero = jnp.zeros((B, B), F32)
        trips = (i - start + SB_UNROLL) // SB_UNROLL
        state = lax.fori_loop(0, trips, step, tuple((zero, zero, zero) for _ in heads))
        for g in range(SB_GROUP):
            dq_ref[:, B * g:B * (g + 1)] = jnp.where(first, state[2 * g][2], state[2 * g + 1][2]) * SB_SCALE

    qblk = pl.BlockSpec((B, W), lambda p, i: (i, p))
    full = pl.BlockSpec((T, W), lambda p, i: (0, p))
    return pl.pallas_call(
        body, name="sb_bwd", grid=(groups, T // B),
        in_specs=[qblk, pl.BlockSpec((T, W), lambda p, i: (0, groups + p)),
                  pl.BlockSpec((T, W), lambda p, i: (0, 2 * groups + p)), qblk, qblk,
                  pl.BlockSpec(upre.shape, lambda p, i: (0, 0)), pl.BlockSpec(uexc.shape, lambda p, i: (0, 0)),
                  pl.BlockSpec(memory_space=pltpu.SMEM)],
        out_specs=[qblk, full, full],
        out_shape=[jax.ShapeDtypeStruct((T, SB_WIDTH), F32)] * 3,
        compiler_params=_params(("arbitrary", "arbitrary")),
    )(sqkv, sqkv, sqkv, do, tl, upre, uexc, first_block)


def _mixer_out(o_hg, hg, o_sb, g_hg, g_sb):
    n_hg, r_hg = _rms(o_hg)
    s_hg = _sigmoid(hg)
    n_sb, r_sb = _rms(o_sb)
    return dict(n_hg=n_hg, r_hg=r_hg, s_hg=s_hg, n_sb=n_sb, r_sb=r_sb,
                y_hg=n_hg * g_hg * (hg * s_hg), y_sb=n_sb * g_sb)


def mix_out_fwd(o_hg, proj_h, o_sb, x, norms, g_post, w_out):
    T = x.shape[0]

    def body(ohg_ref, hg_ref, osb_ref, x_ref, nrm_ref, gp_ref, w_hbm, cat_ref, mix_ref, h1_ref, w_vmem):
        _load_once(w_hbm, w_vmem)
        nrm = nrm_ref[...]
        m = _mixer_out(ohg_ref[...], hg_ref[...], osb_ref[...], nrm[:, :HG_WIDTH], nrm[:, HG_WIDTH:])
        cat_ref[:, :HG_WIDTH] = m["y_hg"].astype(BF16)
        cat_ref[:, HG_WIDTH:] = m["y_sb"].astype(BF16)
        mix = jnp.dot(cat_ref[...], w_vmem[...], preferred_element_type=F32)
        mix_ref[...] = mix
        mh, _ = _rms(mix)
        h1_ref[...] = x_ref[...] + mh * gp_ref[...]

    return pl.pallas_call(
        body, name="mix_out_fwd", grid=(T // ROW_TILE,),
        in_specs=[_row_spec(HG_WIDTH), _row_spec(HG_WIDTH, 3), _row_spec(SB_WIDTH), _row_spec(D_MODEL),
                  _full_spec((1, D_MODEL)), _full_spec((1, D_MODEL)), ANY_SPEC],
        out_specs=[_row_spec(D_MODEL)] * 3,
        out_shape=[jax.ShapeDtypeStruct((T, D_MODEL), BF16), jax.ShapeDtypeStruct((T, D_MODEL), F32),
                   jax.ShapeDtypeStruct((T, D_MODEL), F32)],
        scratch_shapes=[pltpu.VMEM(w_out.shape, BF16)],
        compiler_params=_params(("arbitrary",)),
    )(o_hg, proj_h, o_sb, x, norms, g_post, w_out)


def ffn_fwd(h1, g_pre, g_post, w_gu, w_down):
    T = h1.shape[0]
    pw = w_gu.shape[2]

    def body(h1_ref, gpre_ref, gpost_ref, wgu_hbm, wd_hbm, u2_ref, gu_ref, act_ref, y_ref, h2_ref,
             wgu_vmem, wd_vmem, gu_s):
        _load_once(wgu_hbm, wgu_vmem)
        _load_once(wd_hbm, wd_vmem)
        h1v = h1_ref[...]
        hh, _ = _rms(h1v)
        u2 = (hh * gpre_ref[...]).astype(BF16)
        u2_ref[...] = u2
        for q in range(N_CHIPS):
            gu_s[:, pw * q:pw * (q + 1)] = jnp.dot(u2, wgu_vmem[q], preferred_element_type=F32)
        gu_ref[...] = gu_s[...].astype(BF16)
        gate = gu_s[:, :D_FF]
        act = (gate * _sigmoid(gate) * gu_s[:, D_FF:]).astype(BF16)
        act_ref[...] = act
        y = jnp.dot(act, wd_vmem[...], preferred_element_type=F32)
        y_ref[...] = y
        yh, _ = _rms(y)
        h2_ref[...] = h1v + yh * gpost_ref[...]

    return pl.pallas_call(
        body, name="ffn_fwd", grid=(T // ROW_TILE,),
        in_specs=[_row_spec(D_MODEL), _full_spec((1, D_MODEL)), _full_spec((1, D_MODEL)), ANY_SPEC, ANY_SPEC],
        out_specs=[_row_spec(D_MODEL), _row_spec(2 * D_FF), _row_spec(D_FF), _row_spec(D_MODEL), _row_spec(D_MODEL)],
        out_shape=[jax.ShapeDtypeStruct((T, D_MODEL), BF16), jax.ShapeDtypeStruct((T, 2 * D_FF), BF16),
                   jax.ShapeDtypeStruct((T, D_FF), BF16), jax.ShapeDtypeStruct((T, D_MODEL), F32),
                   jax.ShapeDtypeStruct((T, D_MODEL), F32)],
        scratch_shapes=[pltpu.VMEM(w_gu.shape, BF16), pltpu.VMEM(w_down.shape, BF16),
                        pltpu.VMEM((ROW_TILE, 2 * D_FF), F32)],
        compiler_params=_params(("arbitrary",)),
    )(h1, g_pre, g_post, w_gu, w_down)


def ple_loss(h2, p, target, w_ple, w_pg):
    T = h2.shape[0]
    pw = w_ple.shape[2]

    def body(h2_ref, p_ref, t_ref, wple_hbm, wpg_hbm, de_ref, ds_ref, dh2_ref, h2b_ref, pb_ref, pack_ref,
             wple_vmem, wpg_vmem, e_s):
        _load_once(wple_hbm, wple_vmem)
        _load_once(wpg_hbm, wpg_vmem)
        _zero_first(pack_ref)
        h2v = h2_ref[...]
        h2b = h2v.astype(BF16)
        h2b_ref[...] = h2b
        pb = p_ref[...].astype(BF16)
        pb_ref[...] = pb
        for q in range(N_CHIPS):
            e_s[:, pw * q:pw * (q + 1)] = jnp.dot(pb, wple_vmem[q], preferred_element_type=F32)
        e = e_s[...]
        sig = _sigmoid(jnp.dot(h2b, wpg_vmem[...], preferred_element_type=F32))
        err = h2v + e * sig - t_ref[...]
        part = 0.5 * jnp.sum(jnp.mean(err * err, axis=-1, keepdims=True), axis=0, keepdims=True)
        lane = lax.broadcasted_iota(jnp.int32, (1, D_MODEL), 1)
        pack_ref[ROW_LOSS:ROW_LOSS + 1, :] += jnp.where(lane == 0, part, 0.0)
        dh3 = err * (1.0 / D_MODEL)
        de_ref[...] = (dh3 * sig).astype(BF16)
        ds = (dh3 * e * sig * (1.0 - sig)).astype(BF16)
        ds_ref[...] = ds
        dh2_ref[...] = dh3 + _dot_nt(ds, wpg_vmem[...])

    return pl.pallas_call(
        body, name="ple_loss", grid=(T // ROW_TILE,),
        in_specs=[_row_spec(D_MODEL), _row_spec(p.shape[1]), _row_spec(D_MODEL), ANY_SPEC, ANY_SPEC],
        out_specs=[_row_spec(D_MODEL), _row_spec(D_MODEL), _row_spec(D_MODEL), _row_spec(D_MODEL),
                   _row_spec(p.shape[1]), PACK_SPEC],
        out_shape=[jax.ShapeDtypeStruct((T, D_MODEL), BF16), jax.ShapeDtypeStruct((T, D_MODEL), BF16),
                   jax.ShapeDtypeStruct((T, D_MODEL), F32), jax.ShapeDtypeStruct((T, D_MODEL), BF16),
                   jax.ShapeDtypeStruct(p.shape, BF16), jax.ShapeDtypeStruct((8, D_MODEL), F32)],
        scratch_shapes=[pltpu.VMEM(w_ple.shape, BF16), pltpu.VMEM(w_pg.shape, BF16), pltpu.VMEM((ROW_TILE, D_MODEL), F32)],
        compiler_params=_params(("arbitrary",)),
    )(h2, p, target, w_ple, w_pg)


def ffn_bwd(dh2, y, h1, gu, g_pre, g_post, w_gu, w_down):
    T = h1.shape[0]
    pw = w_gu.shape[2]

    def body(dh2_ref, y_ref, h1_ref, gu_ref, gpre_ref, gpost_ref, wgu_hbm, wd_hbm, dy_ref, dgu_ref, dh1_ref, pack_ref,
             wgu_vmem, wd_vmem):
        _load_once(wgu_hbm, wgu_vmem)
        _load_once(wd_hbm, wd_vmem)
        _zero_first(pack_ref)
        dh2v = dh2_ref[...]
        yh, ry = _rms(y_ref[...])
        dy, dw = _rms_bwd(dh2v, yh, ry, gpost_ref[...])
        pack_ref[ROW_FFN_POST:ROW_FFN_POST + 1, :] += _colsum(dw)
        dyb = dy.astype(BF16)
        dy_ref[...] = dyb
        dact = _dot_nt(dyb, wd_vmem[...])
        gate = gu_ref[:, :D_FF].astype(F32)
        up = gu_ref[:, D_FF:].astype(F32)
        sg = _sigmoid(gate)
        dgu_ref[:, :D_FF] = (dact * up * (sg * (1.0 + gate * (1.0 - sg)))).astype(BF16)
        dgu_ref[:, D_FF:] = (dact * gate * sg).astype(BF16)
        du2 = _dot_nt(dgu_ref[:, :pw], wgu_vmem[0])
        for q in range(1, N_CHIPS):
            du2 = du2 + _dot_nt(dgu_ref[:, pw * q:pw * (q + 1)], wgu_vmem[q])
        hh, rh = _rms(h1_ref[...])
        dh, dw = _rms_bwd(du2, hh, rh, gpre_ref[...])
        pack_ref[ROW_FFN_PRE:ROW_FFN_PRE + 1, :] += _colsum(dw)
        dh1_ref[...] = dh2v + dh

    return pl.pallas_call(
        body, name="ffn_bwd", grid=(T // ROW_TILE,),
        in_specs=[_row_spec(D_MODEL), _row_spec(D_MODEL), _row_spec(D_MODEL), _row_spec(2 * D_FF),
                  _full_spec((1, D_MODEL)), _full_spec((1, D_MODEL)), ANY_SPEC, ANY_SPEC],
        out_specs=[_row_spec(D_MODEL), _row_spec(2 * D_FF), _row_spec(D_MODEL), PACK_SPEC],
        out_shape=[jax.ShapeDtypeStruct((T, D_MODEL), BF16), jax.ShapeDtypeStruct((T, 2 * D_FF), BF16),
                   jax.ShapeDtypeStruct((T, D_MODEL), F32), jax.ShapeDtypeStruct((8, D_MODEL), F32)],
        scratch_shapes=[pltpu.VMEM(w_gu.shape, BF16), pltpu.VMEM(w_down.shape, BF16)],
        compiler_params=_params(("arbitrary",)),
    )(dh2, y, h1, gu, g_pre, g_post, w_gu, w_down)


def mix_out_bwd(dh1, mix, o_hg, proj_h, o_sb, norms, g_post, w_out, comm=None):
    T = dh1.shape[0]

    def body(dh1_ref, mix_ref, ohg_ref, hg_ref, osb_ref, nrm_ref, gp_ref, w_hbm, dmix_ref, dohg_ref, dhg_ref, dosb_ref,
             pack_ref, w_vmem):
        _load_once(w_hbm, w_vmem)
        _zero_first(pack_ref)
        mh, rm = _rms(mix_ref[...])
        dmix, dw = _rms_bwd(dh1_ref[...], mh, rm, gp_ref[...])
        pack_ref[ROW_ATTN_POST:ROW_ATTN_POST + 1, :] += _colsum(dw)
        dmb = dmix.astype(BF16)
        dmix_ref[...] = dmb
        dcat = _dot_nt(dmb, w_vmem[...])
        nrm = nrm_ref[...]
        g_hg, g_sb = nrm[:, :HG_WIDTH], nrm[:, HG_WIDTH:]
        hg = hg_ref[...]
        m = _mixer_out(ohg_ref[...], hg, osb_ref[...], g_hg, g_sb)
        d_hg = dcat[:, :HG_WIDTH]
        silu = hg * m["s_hg"]
        dhg_ref[...] = d_hg * (m["n_hg"] * g_hg) * (m["s_hg"] * (1.0 + hg * (1.0 - m["s_hg"])))
        dx, dw = _rms_bwd(d_hg * silu, m["n_hg"], m["r_hg"], g_hg)
        dohg_ref[...] = dx
        pack_ref[ROW_MIX_NORMS:ROW_MIX_NORMS + 1, :HG_WIDTH] += _colsum(dw)
        dx, dw = _rms_bwd(dcat[:, HG_WIDTH:], m["n_sb"], m["r_sb"], g_sb)
        dosb_ref[...] = dx
        pack_ref[ROW_MIX_NORMS:ROW_MIX_NORMS + 1, HG_WIDTH:] += _colsum(dw)

    return _pallas(
        body, comm=comm, edge=_grid_edge(T // ROW_TILE), name="mix_out_bwd", grid=(T // ROW_TILE,),
        in_specs=[_row_spec(D_MODEL), _row_spec(D_MODEL), _row_spec(HG_WIDTH), _row_spec(HG_WIDTH, 3), _row_spec(SB_WIDTH),
                  _full_spec((1, D_MODEL)), _full_spec((1, D_MODEL)), ANY_SPEC],
        out_specs=[_row_spec(D_MODEL), _row_spec(HG_WIDTH), _row_spec(HG_WIDTH), _row_spec(SB_WIDTH), PACK_SPEC],
        out_shape=[jax.ShapeDtypeStruct((T, D_MODEL), BF16), jax.ShapeDtypeStruct((T, HG_WIDTH), F32),
                   jax.ShapeDtypeStruct((T, HG_WIDTH), F32), jax.ShapeDtypeStruct((T, SB_WIDTH), F32),
                   jax.ShapeDtypeStruct((8, D_MODEL), F32)],
        scratch_shapes=[pltpu.VMEM(w_out.shape, BF16)],
        compiler_params=_params(("arbitrary",)),
    )(dh1, mix, o_hg, proj_h, o_sb, norms, g_post, w_out)


def in_proj_bwd(parts, x, dh1, g_pre, w_in):
    T = x.shape[0]
    pw = w_in.shape[2]
    n_parts = len(parts)

    def body(*refs):
        part_refs = refs[:n_parts]
        x_ref, dh1_ref, g_ref, w_hbm, dproj_ref, dx_ref, pack_ref, w_vmem = refs[n_parts:]
        _load_once(w_hbm, w_vmem)
        _zero_first(pack_ref)
        for n, ref in enumerate(part_refs):
            dproj_ref[:, HG_WIDTH * n:HG_WIDTH * (n + 1)] = ref[...].astype(BF16)
        du = _dot_nt(dproj_ref[:, :pw], w_vmem[0])
        for q in range(1, N_CHIPS):
            du = du + _dot_nt(dproj_ref[:, pw * q:pw * (q + 1)], w_vmem[q])
        xh, r = _rms(x_ref[...])
        dx, dw = _rms_bwd(du, xh, r, g_ref[...])
        pack_ref[ROW_ATTN_PRE:ROW_ATTN_PRE + 1, :] += _colsum(dw)
        dx_ref[...] = dh1_ref[...] + dx

    return pl.pallas_call(
        body, name="in_proj_bwd", grid=(T // ROW_TILE,),
        in_specs=[_row_spec(HG_WIDTH)] * n_parts + [_row_spec(D_MODEL), _row_spec(D_MODEL), _full_spec((1, D_MODEL)), ANY_SPEC],
        out_specs=[_row_spec(n_parts * HG_WIDTH), _row_spec(D_MODEL), PACK_SPEC],
        out_shape=[jax.ShapeDtypeStruct((T, n_parts * HG_WIDTH), BF16), jax.ShapeDtypeStruct((T, D_MODEL), F32),
                   jax.ShapeDtypeStruct((8, D_MODEL), F32)],
        scratch_shapes=[pltpu.VMEM(w_in.shape, BF16)],
        compiler_params=_params(("arbitrary",)),
    )(*parts, x, dh1, g_pre, w_in)


def weight_grad(a, g, name, *, tm, tn, tk=512, col_pieces=False):
    T, M = a.shape
    N = g.shape[1]
    steps = T // tk

    def body(a_ref, g_ref, o_ref):
        @pl.when(pl.program_id(2) == 0)
        def _():
            o_ref[...] = jnp.zeros(o_ref.shape, F32)

        o_ref[...] += _dot_tn(a_ref[...], g_ref[...]).reshape(o_ref.shape)

    if col_pieces:
        out_shape = jax.ShapeDtypeStruct((N // tn, M, tn), F32)
        out_spec = pl.BlockSpec((1, tm, tn), lambda i, j, k: (j, i, 0))
    else:
        out_shape = jax.ShapeDtypeStruct((M, N), F32)
        out_spec = pl.BlockSpec((tm, tn), lambda i, j, k: (i, j))
    return pl.pallas_call(
        body, name=name, grid=(M // tm, N // tn, steps),
        in_specs=[pl.BlockSpec((tk, tm), lambda i, j, k: (k, i)), pl.BlockSpec((tk, tn), lambda i, j, k: (k, j))],
        out_specs=out_spec, out_shape=out_shape,
        compiler_params=_params(("arbitrary", "arbitrary", "arbitrary")),
    )(a, g)


def _place():
    x, y, c = lax.axis_index("x"), lax.axis_index("y"), lax.axis_index("c")
    chips = [(1 - x, y), (x, 1 - y), (1 - x, 1 - y)]
    return x, y, c, chips


def _chip_index(cx, cy):
    return 2 * cx + cy


def _rcopy(src, dst, send_sem, recv_sem, device):
    return pltpu.make_async_remote_copy(src_ref=src, dst_ref=dst, send_sem=send_sem, recv_sem=recv_sem,
                                        device_id=device, device_id_type=MESH)


def gather_weights(shards):
    n = len(shards)

    def body(*refs):
        ins, outs = refs[:n], refs[2 * n:3 * n]
        send_sems, recv_sems = refs[3 * n:]
        x, y, c, chips = _place()
        me = _chip_index(x, y)
        sibling = (x, y, 1 - c)

        def rows(w, core):
            half = ins[w].shape[0] // 2
            return pl.ds(core * half, half)

        sends = []
        for w in range(n):
            for j, chip in enumerate(chips):
                sends.append(_rcopy(ins[w].at[rows(w, c)], outs[w].at[me, rows(w, c)],
                                    send_sems.at[6 * w + j], recv_sems.at[6 * w + j], (*chip, c)))
        for cp in sends:
            cp.start()
        passed = []
        for w in range(n):
            for j, chip in enumerate(chips):
                block = outs[w].at[_chip_index(*chip), rows(w, c)]
                _rcopy(block, block, send_sems.at[6 * w + j], recv_sems.at[6 * w + j], (*chip, c)).wait_recv()
                cp = _rcopy(block, block, send_sems.at[6 * w + 3 + j], recv_sems.at[6 * w + 3 + j], sibling)
                cp.start()
                passed.append(cp)
        for w in range(n):
            for j, chip in enumerate(chips):
                block = outs[w].at[_chip_index(*chip), rows(w, 1 - c)]
                _rcopy(block, block, send_sems.at[6 * w + 3 + j], recv_sems.at[6 * w + 3 + j], sibling).wait_recv()
        for cp in sends + passed:
            cp.wait_send()

    filled = [jnp.broadcast_to(s[None], (N_CHIPS,) + s.shape) for s in shards]
    return pl.pallas_call(
        body, name="gather_weights",
        in_specs=[ANY_SPEC] * (2 * n), out_specs=[ANY_SPEC] * n,
        out_shape=[jax.ShapeDtypeStruct(f.shape, f.dtype) for f in filled],
        input_output_aliases={n + w: w for w in range(n)},
        scratch_shapes=[pltpu.SemaphoreType.DMA((6 * n,)), pltpu.SemaphoreType.DMA((6 * n,))],
    )(*shards, *filled)


def _run_comm(comm, name):
    c_in, c_out = len(comm.inputs), len(comm.out_shape)

    def body(*refs):
        parts = refs[:c_in], refs[c_in:c_in + c_out], refs[c_in + c_out:]
        comm.start(*parts)
        comm.finish(*parts)

    return pl.pallas_call(
        body, name=name, in_specs=[ANY_SPEC] * c_in, out_specs=[ANY_SPEC] * c_out, out_shape=comm.out_shape,
        scratch_shapes=comm.scratch, input_output_aliases=comm.aliases)(*comm.inputs)


def _dma_sems(count):
    return [pltpu.SemaphoreType.DMA((count,)), pltpu.SemaphoreType.DMA((count,))]


def gather_over_ici(shards):
    n = len(shards)

    def copies(ins, outs, sems):
        send_sems, recv_sems = sems
        x, y, c, chips = _place()
        me = _chip_index(x, y)
        pairs = []
        for w in range(n):
            half = shards[w].shape[0] // 2
            rows = pl.ds(c * half, half)
            for j, chip in enumerate(chips):
                k = 3 * w + j
                landed = outs[w].at[_chip_index(*chip), rows]
                pairs.append((_rcopy(ins[w].at[rows], outs[w].at[me, rows], send_sems.at[k], recv_sems.at[k], (*chip, c)),
                              _rcopy(landed, landed, send_sems.at[k], recv_sems.at[k], (*chip, c))))
        return pairs

    def start(*refs):
        for send, _ in copies(*refs):
            send.start()

    def finish(*refs):
        pairs = copies(*refs)
        for _, landed in pairs:
            landed.wait_recv()
        for send, _ in pairs:
            send.wait_send()

    filled = [jnp.broadcast_to(s[None], (N_CHIPS,) + s.shape) for s in shards]
    return Comm(list(shards) + filled, [jax.ShapeDtypeStruct(f.shape, f.dtype) for f in filled],
                {n + w: w for w in range(n)}, _dma_sems(3 * n), start, finish)


def gather_over_d2d(landed):
    n = len(landed)

    def copies(ins, outs, sems):
        send_sems, recv_sems = sems
        x, y, c, chips = _place()
        sibling = (x, y, 1 - c)
        pairs = []
        for w in range(n):
            half = landed[w].shape[1] // 2
            for j, chip in enumerate(chips):
                k = 3 * w + j
                mine = outs[w].at[_chip_index(*chip), pl.ds(c * half, half)]
                theirs = outs[w].at[_chip_index(*chip), pl.ds((1 - c) * half, half)]
                pairs.append((_rcopy(mine, mine, send_sems.at[k], recv_sems.at[k], sibling),
                              _rcopy(theirs, theirs, send_sems.at[k], recv_sems.at[k], sibling)))
        return pairs

    def start(*refs):
        for send, _ in copies(*refs):
            send.start()

    def finish(*refs):
        pairs = copies(*refs)
        for _, arrived in pairs:
            arrived.wait_recv()
        for send, _ in pairs:
            send.wait_send()

    return Comm(list(landed), [jax.ShapeDtypeStruct(a.shape, a.dtype) for a in landed], {w: w for w in range(n)},
                _dma_sems(3 * n), start, finish)


def core_halves(grads):
    n = len(grads)

    def copies(ins, outs, sems):
        send_sems, recv_sems = sems
        x, y, c, _ = _place()
        out = []
        for w in range(n):
            half = grads[w].shape[1] // 2
            out.append(_rcopy(ins[w].at[:, pl.ds((1 - c) * half, half), :], outs[w],
                              send_sems.at[w], recv_sems.at[w], (x, y, 1 - c)))
        return out

    def start(*refs):
        for cp in copies(*refs):
            cp.start()

    def finish(*refs):
        for cp in copies(*refs):
            cp.wait()

    return Comm(list(grads), [jax.ShapeDtypeStruct((g.shape[0], g.shape[1] // 2, g.shape[2]), g.dtype) for g in grads],
                {}, _dma_sems(n), start, finish)


def chip_partials(partials):
    n = len(partials)

    def copies(ins, outs, sems):
        send_sems, recv_sems = sems
        x, y, c, chips = _place()
        me = _chip_index(x, y)
        pairs = []
        for w in range(n):
            for j, chip in enumerate(chips):
                k = 3 * w + j
                landed = outs[w].at[_chip_index(*chip)]
                pairs.append((_rcopy(ins[w].at[_chip_index(*chip)], outs[w].at[me], send_sems.at[k], recv_sems.at[k],
                                     (*chip, c)),
                              _rcopy(landed, landed, send_sems.at[k], recv_sems.at[k], (*chip, c))))
        return pairs

    def start(*refs):
        for send, _ in copies(*refs):
            send.start()

    def finish(*refs):
        pairs = copies(*refs)
        for _, landed in pairs:
            landed.wait_recv()
        for send, _ in pairs:
            send.wait_send()

    me = _chip_index(lax.axis_index("x"), lax.axis_index("y"))
    filled = [jnp.broadcast_to(lax.dynamic_index_in_dim(p, me, 0, keepdims=True), p.shape) for p in partials]
    return Comm(list(partials) + filled, [jax.ShapeDtypeStruct(p.shape, p.dtype) for p in partials],
                {n + w: w for w in range(n)}, _dma_sems(3 * n), start, finish)


def join_core_halves(grads):
    n = len(grads)

    def body(*refs):
        outs = refs[n:2 * n]
        send_sems, recv_sems = refs[2 * n:]
        x, y, c, _ = _place()
        sibling = (x, y, 1 - c)
        copies = []
        for w in range(n):
            half = outs[w].shape[0] // 2
            mine = outs[w].at[pl.ds(c * half, half), :]
            copies.append(_rcopy(mine, mine, send_sems.at[w], recv_sems.at[w], sibling))
        for cp in copies:
            cp.start()
        for w in range(n):
            half = outs[w].shape[0] // 2
            theirs = outs[w].at[pl.ds((1 - c) * half, half), :]
            _rcopy(theirs, theirs, send_sems.at[w], recv_sems.at[w], sibling).wait_recv()
        for cp in copies:
            cp.wait_send()

    return pl.pallas_call(
        body, name="join_core_halves",
        in_specs=[ANY_SPEC] * n, out_specs=[ANY_SPEC] * n,
        out_shape=[jax.ShapeDtypeStruct(g.shape, g.dtype) for g in grads],
        input_output_aliases={w: w for w in range(n)},
        scratch_shapes=[pltpu.SemaphoreType.DMA((n,)), pltpu.SemaphoreType.DMA((n,))],
    )(*grads)


def _elementwise_rows(rows, cap=512):
    for t in range(min(rows, cap), 0, -8):
        if rows % t == 0 and t % 16 == 0:
            return t
    return rows


def add_core_halves(grad, got, core, name):
    _, rows, cols = got.shape
    tr = _elementwise_rows(rows)
    nt = rows // tr

    def body(core_ref, a_ref, b_ref, o_ref):
        o_ref[...] = (a_ref[...] + b_ref[...]).astype(BF16)

    spec = pl.BlockSpec((1, tr, cols), lambda q, i, core_ref: (q, i, 0))
    own = pl.BlockSpec((1, tr, cols), lambda q, i, core_ref: (q, core_ref[0] * nt + i, 0))
    return pl.pallas_call(
        body, name=name,
        grid_spec=pltpu.PrefetchScalarGridSpec(num_scalar_prefetch=1, grid=(N_CHIPS, nt), in_specs=[own, spec],
                                               out_specs=spec),
        out_shape=jax.ShapeDtypeStruct(got.shape, BF16),
        compiler_params=_params(("arbitrary", "arbitrary")),
    )(core, grad, got)


def add_chip_partials(parts, core, name):
    _, rows, cols = parts.shape
    tr = _elementwise_rows(rows)
    nt = rows // tr

    def body(core_ref, p_ref, o_ref):
        acc = p_ref[0].astype(F32)
        for q in range(1, N_CHIPS):
            acc = acc + p_ref[q].astype(F32)
        o_ref[...] = acc

    return pl.pallas_call(
        body, name=name,
        grid_spec=pltpu.PrefetchScalarGridSpec(
            num_scalar_prefetch=1, grid=(nt,),
            in_specs=[pl.BlockSpec((N_CHIPS, tr, cols), lambda i, core_ref: (0, i, 0))],
            out_specs=pl.BlockSpec((tr, cols), lambda i, core_ref: (core_ref[0] * nt + i, 0))),
        out_shape=jax.ShapeDtypeStruct((2 * rows, cols), F32),
        compiler_params=_params(("arbitrary",)),
    )(core, parts)


def _adamw_math(w, g, m, v):
    m = ADAM_B1 * m + (1.0 - ADAM_B1) * g
    v = ADAM_B2 * v + (1.0 - ADAM_B2) * (g * g)
    m_hat = m / (1.0 - ADAM_B1 ** ADAM_STEP)
    v_hat = v / (1.0 - ADAM_B2 ** ADAM_STEP)
    delta = -ADAM_LR * (m_hat / (jnp.sqrt(v_hat) + ADAM_EPS) + ADAM_WD * w)
    return delta, m, v


def adamw(w, g, m, v, name):
    rows, cols = w.shape
    tr = _elementwise_rows(rows, 256)

    def body(w_ref, g_ref, m_ref, v_ref, d_ref, nm_ref, nv_ref):
        d, nm, nv = _adamw_math(w_ref[...], g_ref[...], m_ref[...], v_ref[...])
        d_ref[...] = d
        nm_ref[...] = nm
        nv_ref[...] = nv

    spec = pl.BlockSpec((tr, cols), lambda i: (i, 0))
    return pl.pallas_call(
        body, name=name, grid=(rows // tr,), in_specs=[spec] * 4, out_specs=[spec] * 3,
        out_shape=[jax.ShapeDtypeStruct((rows, cols), F32)] * 3,
        compiler_params=_params(("arbitrary",)),
    )(w, g, m, v)


def reduce_small(packs, w, m, v):
    n = len(packs)
    n_dev = 8
    flips = [(fx, fy, fc) for fx in (0, 1) for fy in (0, 1) for fc in (0, 1)][1:]

    def body(*refs):
        pack_refs = refs[:n]
        w_ref, m_ref, v_ref, g_out, d_out, m_out, v_out, mine, slots, send_sems, recv_sems = refs[n:]
        x, y, c, _ = _place()
        me = 4 * x + 2 * y + c
        acc = pack_refs[0][...]
        for ref in pack_refs[1:]:
            acc = acc + ref[...]
        mine[...] = acc
        sends = []
        for k, (fx, fy, fc) in enumerate(flips):
            peer = (x ^ fx, y ^ fy, c ^ fc)
            sends.append(_rcopy(mine, slots.at[me], send_sems.at[k], recv_sems.at[me], peer))
        for cp in sends:
            cp.start()
        slots[me] = acc
        for fx, fy, fc in flips:
            src = 4 * (x ^ fx) + 2 * (y ^ fy) + (c ^ fc)
            _rcopy(mine, slots.at[src], send_sems.at[0], recv_sems.at[src], (x, y, c)).wait_recv()
        for cp in sends:
            cp.wait_send()
        total = slots[0]
        for d in range(1, n_dev):
            total = total + slots[d]
        g_out[...] = total
        d, nm, nv = _adamw_math(w_ref[...], total, m_ref[...], v_ref[...])
        d_out[...] = d
        m_out[...] = nm
        v_out[...] = nv

    vm = pl.BlockSpec(memory_space=pltpu.VMEM)
    return pl.pallas_call(
        body, name="reduce_small",
        in_specs=[vm] * (n + 3), out_specs=[vm] * 4,
        out_shape=[jax.ShapeDtypeStruct((8, D_MODEL), F32)] * 4,
        scratch_shapes=[pltpu.VMEM((8, D_MODEL), F32), pltpu.VMEM((n_dev, 8, D_MODEL), F32),
                        pltpu.SemaphoreType.DMA((len(flips),)), pltpu.SemaphoreType.DMA((n_dev,))],
    )(*packs, w, m, v)


def _pack_small(attn_pre, gamma, hg_norm, sb_norm, attn_post, ffn_pre, ffn_post):
    rows = [attn_pre, gamma.reshape(1, D_MODEL), jnp.concatenate([hg_norm, sb_norm], axis=1), attn_post, ffn_pre, ffn_post,
            jnp.zeros((2, D_MODEL), F32)]
    return jnp.concatenate(rows, axis=0)


def _unpack_small(pack):
    return (pack[ROW_ATTN_PRE:ROW_ATTN_PRE + 1], pack[ROW_GAMMA].reshape(2, HG_WIDTH),
            pack[ROW_MIX_NORMS:ROW_MIX_NORMS + 1, :HG_WIDTH], pack[ROW_MIX_NORMS:ROW_MIX_NORMS + 1, HG_WIDTH:],
            pack[ROW_ATTN_POST:ROW_ATTN_POST + 1], pack[ROW_FFN_PRE:ROW_FFN_PRE + 1], pack[ROW_FFN_POST:ROW_FFN_POST + 1])


def kernel(x, p, attn_pre_norm, w_in, hg_lower_gamma, hg_out_norm, sb_out_norm, w_out, attn_post_norm, ffn_pre_norm, w_gate_up, w_down, ffn_post_norm, ple_proj, ple_gate, loss_target, m_attn_pre_norm, m_w_in, m_hg_lower_gamma, m_hg_out_norm, m_sb_out_norm, m_w_out, m_attn_post_norm, m_ffn_pre_norm, m_w_gate_up, m_w_down, m_ffn_post_norm, m_ple_proj, m_ple_gate, v_attn_pre_norm, v_w_in, v_hg_lower_gamma, v_hg_out_norm, v_sb_out_norm, v_w_out, v_attn_post_norm, v_ffn_pre_norm, v_w_gate_up, v_w_down, v_ffn_post_norm, v_ple_proj, v_ple_gate):
    x2 = x[0]
    p2 = p[0, 0]
    target = loss_target[0]
    big = dict(w_in=(w_in, m_w_in, v_w_in), w_out=(w_out, m_w_out, v_w_out), w_gate_up=(w_gate_up, m_w_gate_up, v_w_gate_up),
               w_down=(w_down, m_w_down, v_w_down), ple_proj=(ple_proj, m_ple_proj, v_ple_proj),
               ple_gate=(ple_gate, m_ple_gate, v_ple_gate))
    names = list(big)
    big = {k: tuple(a[0] for a in t) for k, t in big.items()}

    shard16 = {k: big[k][0].astype(BF16) for k in names}
    later = [k for k in names if k != "w_in"]
    w_in_full, = gather_weights([shard16["w_in"]])
    mix_norms = jnp.concatenate([hg_out_norm, sb_out_norm], axis=1)

    proj_h, sqkv, u1 = in_proj_fwd(x2, attn_pre_norm, w_in_full)
    o_sb, sb_totals, sb_first, *landed = sb_fwd(sqkv, comm=gather_over_ici([shard16[k] for k in later]))
    o_hg, states, *gathered = hgrn2_fwd(proj_h, hg_lower_gamma, comm=gather_over_d2d(landed))
    full = dict(zip(later, gathered), w_in=w_in_full)
    w_out_full = full["w_out"].reshape(D_MODEL, D_MODEL)
    w_down_full = full["w_down"].reshape(D_FF, D_MODEL)
    w_pg_full = full["ple_gate"].reshape(D_MODEL, D_MODEL)
    cat, mix, h1 = mix_out_fwd(o_hg, proj_h, o_sb, x2, mix_norms, attn_post_norm, w_out_full)
    u2, gu, act, y, h2 = ffn_fwd(h1, ffn_pre_norm, ffn_post_norm, full["w_gate_up"], w_down_full)

    core = lax.axis_index("c").astype(jnp.int32).reshape(1)
    de, ds, dh2, h2b, pb, pack_loss = ple_loss(h2, p2, target, full["ple_proj"], w_pg_full)
    dy, dgu, dh1, pack_ffn = ffn_bwd(dh2, y, h1, gu, ffn_pre_norm, ffn_post_norm, full["w_gate_up"], w_down_full)
    local = dict(
        w_gate_up=weight_grad(u2, dgu, "grad_w_gate_up", tm=D_MODEL, tn=full["w_gate_up"].shape[2], tk=1024, col_pieces=True),
        w_down=weight_grad(act, dy, "grad_w_down", tm=D_FF // 2, tn=D_MODEL, tk=1024).reshape(full["w_down"].shape),
        ple_proj=weight_grad(pb, de, "grad_ple_proj", tm=pb.shape[1], tn=full["ple_proj"].shape[2], col_pieces=True),
        ple_gate=weight_grad(h2b, ds, "grad_ple_gate", tm=D_MODEL, tn=D_MODEL).reshape(full["ple_gate"].shape),
    )
    early = list(local)
    dmix, do_hg, dhg, do_sb, pack_mix, *got = mix_out_bwd(
        dh1, mix, o_hg, proj_h, o_sb, mix_norms, attn_post_norm, w_out_full, comm=core_halves([local[k] for k in early]))
    partial = [add_core_halves(local[k], g, core, "add_core_halves_" + k) for k, g in zip(early, got)]
    dsq, dsk, dsv = sb_bwd(sqkv, do_sb, sb_totals, sb_first)
    dhq, dhf, dhi, pack_hg, *by_source = hgrn2_bwd(proj_h, hg_lower_gamma, states, do_hg, comm=chip_partials(partial))
    halves = {k: add_chip_partials(s, core, "add_chip_partials_" + k) for k, s in zip(early, by_source)}
    dproj, grad_x, pack_in = in_proj_bwd([dhq, dhf, dhi, dhg, dsq, dsk, dsv], x2, dh1, attn_pre_norm, full["w_in"])

    late = ["w_in", "w_out"]
    local["w_in"] = weight_grad(u1, dproj, "grad_w_in", tm=D_MODEL, tn=full["w_in"].shape[2], tk=1024, col_pieces=True)
    local["w_out"] = weight_grad(cat, dmix, "grad_w_out", tm=D_MODEL, tn=D_MODEL).reshape(full["w_out"].shape)
    got = _run_comm(core_halves([local[k] for k in late]), "exchange_core_halves")
    partial = [add_core_halves(local[k], g, core, "add_core_halves_" + k) for k, g in zip(late, got)]
    by_source = _run_comm(chip_partials(partial), "exchange_chip_partials")
    halves.update({k: add_chip_partials(s, core, "add_chip_partials_" + k) for k, s in zip(late, by_source)})
    grads = dict(zip(names, join_core_halves([halves[k] for k in names])))

    upd = {k: adamw(big[k][0], grads[k], big[k][1], big[k][2], "adamw_" + k) for k in names}

    small = reduce_small(
        [pack_loss, pack_ffn, pack_mix, pack_hg, pack_in],
        _pack_small(attn_pre_norm, hg_lower_gamma, hg_out_norm, sb_out_norm, attn_post_norm, ffn_pre_norm, ffn_post_norm),
        _pack_small(m_attn_pre_norm, m_hg_lower_gamma, m_hg_out_norm, m_sb_out_norm, m_attn_post_norm, m_ffn_pre_norm, m_ffn_post_norm),
        _pack_small(v_attn_pre_norm, v_hg_lower_gamma, v_hg_out_norm, v_sb_out_norm, v_attn_post_norm, v_ffn_pre_norm, v_ffn_post_norm),
    )
    loss = small[0][ROW_LOSS, 0]
    s_grad, s_delta, s_m, s_v = (_unpack_small(t) for t in small)

    def ordered(small_vals, big_vals):
        a_pre, gam, hg_n, sb_n, a_post, f_pre, f_post = small_vals
        b = {k: big_vals[k][None] for k in names}
        return (a_pre, b["w_in"], gam, hg_n, sb_n, b["w_out"], a_post, f_pre, b["w_gate_up"], b["w_down"], f_post,
                b["ple_proj"], b["ple_gate"])

    return (loss, grad_x[None],
            *ordered(s_grad, grads),
            *ordered(s_delta, {k: upd[k][0] for k in names}),
            *ordered(s_m, {k: upd[k][1] for k in names}),
            *ordered(s_v, {k: upd[k][2] for k in names}))
```

```python
from typing import Callable, NamedTuple

import numpy as np
import jax
import jax.numpy as jnp
from jax import lax
from jax.experimental import pallas as pl
from jax.experimental.pallas import tpu as pltpu

F32 = jnp.float32
BF16 = jnp.bfloat16
MESH = pl.DeviceIdType.MESH

RMS_EPS = 1e-6
D_MODEL = 1024
HG_WIDTH = 512
HG_HEADS = 4
HG_DK = 128
HG_CHUNK = 64
HG_LEVELS = (32, 16, 8, 4, 2, 1)
SB_WIDTH = 512
SB_BLOCK = 128
SB_DH = 64
SB_SCALE = SB_DH ** -0.5
SB_UNDERFLOW_LOG = -87.5
SB_UNROLL = 2
SB_GROUP = 2
D_FF = 2816
N_CHIPS = 4
ROW_TILE = 256
V7X_VMEM_LIMIT = 56 * 1024 * 1024

ADAM_LR = 0.001
ADAM_B1 = 0.9
ADAM_B2 = 0.999
ADAM_EPS = 1e-08
ADAM_WD = 0.01
ADAM_STEP = 10

ROW_ATTN_PRE, ROW_GAMMA, ROW_MIX_NORMS, ROW_ATTN_POST, ROW_FFN_PRE, ROW_FFN_POST, ROW_LOSS = range(7)


def _params(sem=None, vmem=V7X_VMEM_LIMIT):
    return pltpu.CompilerParams(dimension_semantics=sem, vmem_limit_bytes=vmem)


def _dot(a, b):
    return jnp.dot(a.astype(BF16), b.astype(BF16), preferred_element_type=F32)


def _dot_nt(a, b):
    return lax.dot_general(a.astype(BF16), b.astype(BF16), (((1,), (1,)), ((), ())), preferred_element_type=F32)


def _dot_tn(a, b):
    return lax.dot_general(a.astype(BF16), b.astype(BF16), (((0,), (0,)), ((), ())), preferred_element_type=F32)


def _split(x):
    hi = x.astype(BF16)
    lo = (x - hi.astype(F32)).astype(BF16)
    return hi, lo


def _sum01_left(m01, x):
    hi, lo = _split(x)
    return jnp.dot(m01, hi, preferred_element_type=F32) + jnp.dot(m01, lo, preferred_element_type=F32)


def _sum01_right(x, m01):
    hi, lo = _split(x)
    return jnp.dot(hi, m01, preferred_element_type=F32) + jnp.dot(lo, m01, preferred_element_type=F32)


def _rms(x):
    r = lax.rsqrt(jnp.mean(x * x, axis=-1, keepdims=True) + RMS_EPS)
    return x * r, r


def _rms_bwd(dy, xhat, r, w):
    dxh = dy * w
    dx = r * (dxh - xhat * jnp.mean(dxh * xhat, axis=-1, keepdims=True))
    return dx, dy * xhat


def _sigmoid(x):
    return 1.0 / (1.0 + jnp.exp(-x))


def _neg_softplus(z):
    return -(jnp.maximum(z, 0.0) + jnp.log(1.0 + jnp.exp(-jnp.abs(z))))


def _colsum(x):
    return jnp.sum(x, axis=0, keepdims=True)


def _load_once(src_hbm, dst_vmem):
    @pl.when(pl.program_id(0) == 0)
    def _():
        pltpu.sync_copy(src_hbm, dst_vmem)


def _zero_first(ref):
    @pl.when(pl.program_id(0) == 0)
    def _():
        ref[...] = jnp.zeros(ref.shape, ref.dtype)


def _row_spec(width, col=0):
    return pl.BlockSpec((ROW_TILE, width), lambda i, col=col: (i, col))


def _full_spec(shape):
    return pl.BlockSpec(shape, lambda *_: (0,) * len(shape))


ANY_SPEC = pl.BlockSpec(memory_space=pl.ANY)
PACK_SPEC = _full_spec((8, D_MODEL))


class Comm(NamedTuple):
    inputs: list
    out_shape: list
    aliases: dict
    scratch: list
    start: Callable
    finish: Callable


def _pallas(body, *, comm=None, edge=None, in_specs, out_specs, out_shape, scratch_shapes=(), **kw):
    if comm is None:
        return pl.pallas_call(body, in_specs=in_specs, out_specs=out_specs, out_shape=out_shape,
                              scratch_shapes=scratch_shapes, **kw)
    n_in, n_out, n_scr = len(in_specs), len(out_specs), len(scratch_shapes)
    c_in, c_out = len(comm.inputs), len(comm.out_shape)

    def both(*refs):
        ins, c_ins = refs[:n_in], refs[n_in:n_in + c_in]
        outs = refs[n_in + c_in:n_in + c_in + n_out]
        c_outs = refs[n_in + c_in + n_out:n_in + c_in + n_out + c_out]
        rest = refs[n_in + c_in + n_out + c_out:]
        scr, c_scr = rest[:n_scr], rest[n_scr:]
        first, last = edge()

        @pl.when(first)
        def _():
            comm.start(c_ins, c_outs, c_scr)

        body(*ins, *outs, *scr)

        @pl.when(last)
        def _():
            comm.finish(c_ins, c_outs, c_scr)

    call = pl.pallas_call(
        both, in_specs=list(in_specs) + [ANY_SPEC] * c_in, out_specs=list(out_specs) + [ANY_SPEC] * c_out,
        out_shape=list(out_shape) + list(comm.out_shape), scratch_shapes=list(scratch_shapes) + list(comm.scratch),
        input_output_aliases={n_in + a: n_out + b for a, b in comm.aliases.items()}, **kw)
    return lambda *args: call(*args, *comm.inputs)


def _grid_edge(steps):
    return lambda: (pl.program_id(0) == 0, pl.program_id(0) == steps - 1)


def in_proj_fwd(x, g_pre, w_in, comm=None):
    T = x.shape[0]
    pw = w_in.shape[2]

    def body(x_ref, g_ref, w_hbm, ph_ref, sqkv_ref, u_ref, w_vmem, proj_s):
        _load_once(w_hbm, w_vmem)
        xh, _ = _rms(x_ref[...])
        u = (xh * g_ref[...]).astype(BF16)
        u_ref[...] = u
        for q in range(N_CHIPS):
            proj_s[:, pw * q:pw * (q + 1)] = jnp.dot(u, w_vmem[q], preferred_element_type=F32)
        ph_ref[...] = proj_s[:, :4 * HG_WIDTH]
        sqkv_ref[:, :SB_WIDTH] = (proj_s[:, 4 * HG_WIDTH:4 * HG_WIDTH + SB_WIDTH] * SB_SCALE).astype(BF16)
        sqkv_ref[:, SB_WIDTH:] = proj_s[:, 4 * HG_WIDTH + SB_WIDTH:].astype(BF16)

    return _pallas(
        body, comm=comm, edge=_grid_edge(T // ROW_TILE), name="in_proj_fwd", grid=(T // ROW_TILE,),
        in_specs=[_row_spec(D_MODEL), _full_spec((1, D_MODEL)), ANY_SPEC],
        out_specs=[_row_spec(4 * HG_WIDTH), _row_spec(3 * SB_WIDTH), _row_spec(D_MODEL)],
        out_shape=[jax.ShapeDtypeStruct((T, 4 * HG_WIDTH), F32), jax.ShapeDtypeStruct((T, 3 * SB_WIDTH), BF16),
                   jax.ShapeDtypeStruct((T, D_MODEL), BF16)],
        scratch_shapes=[pltpu.VMEM(w_in.shape, BF16), pltpu.VMEM((ROW_TILE, N_CHIPS * pw), F32)],
        compiler_params=_params(("arbitrary",)),
    )(x, g_pre, w_in)


def _hg_sum_matrix():
    C = HG_CHUNK
    t = np.arange(C)[:, None]
    j = np.arange(C)[None, :]
    mats = [j <= t, j > t]
    for h in HG_LEVELS:
        start = (t // (2 * h)) * (2 * h)
        upper = (t & h) != 0
        mats.append(np.where(upper, (j >= start + h) & (j <= t), (j > t) & (j <= start + h - 1)))
    return np.concatenate(mats, 0).astype(np.float32)


def _hg_level_masks():
    C = HG_CHUNK
    t = lax.broadcasted_iota(jnp.int32, (C, C), 0)
    s = lax.broadcasted_iota(jnp.int32, (C, C), 1)
    x = t ^ s
    masks = [t == s]
    for h in HG_LEVELS:
        masks.append((x >= h) & (x < 2 * h) & (t > s))
    return masks


def _hg_gates(hq, hf, gamma):
    lb = 1.0 / (1.0 + jnp.exp(gamma[1:2, :] - gamma[0:1, :]))
    sq = _sigmoid(hq)
    q = hq * sq
    sig = _sigmoid(hf)
    nsig = _sigmoid(-hf)
    f = lb + (1.0 - lb) * sig
    k = (1.0 - lb) * nsig
    g = jnp.log(f)
    return q, k, g, dict(lb=lb, sq=sq, sig=sig, nsig=nsig, f=f)


def _hg_head_decays(A, h):
    C, K = HG_CHUNK, HG_DK
    sl = slice(K * h, K * (h + 1))
    blocks = [A[C * r:C * (r + 1), sl] for r in range(2 + len(HG_LEVELS))]
    return blocks[0], blocks[1], [None] + blocks[2:]


def _hg_products(q, k, levels):
    return [_dot_nt(q, k)] + [_dot_nt(q * a, k * a) for a in levels[1:]]


def _hg_select(prods, masks):
    sc = jnp.where(masks[0], prods[0], 0.0)
    for p, m in zip(prods[1:], masks[1:]):
        sc = jnp.where(m, p, sc)
    return sc


def hgrn2_fwd(proj_h, gamma, comm=None):
    T = proj_h.shape[0]
    C, K, H = HG_CHUNK, HG_DK, HG_HEADS
    n_chunks = T // C
    msum = jnp.asarray(_hg_sum_matrix(), BF16)

    def body(hq_ref, hf_ref, hi_ref, gam_ref, msum_ref, o_ref, st_ref, st_s):
        _zero_first(st_s)
        q, k, g, _ = _hg_gates(hq_ref[...], hf_ref[...], gam_ref[...])
        v = hi_ref[...]
        A = jnp.exp(_sum01_left(msum_ref[...], g))
        masks = _hg_level_masks()
        heads = []
        for h in range(H):
            sl = slice(K * h, K * (h + 1))
            ab, ar, levels = _hg_head_decays(A, h)
            heads.append((sl, ab, ar, levels, q[:, sl], k[:, sl], v[:, sl], st_s[h]))
        prods = [_hg_products(qh, kh, levels) for _, _, _, levels, qh, kh, _, _ in heads]
        inter = [_dot_nt(qh * ab, st) for _, ab, _, _, qh, _, _, st in heads]
        grown = [_dot_tn(vh, kh * ar) for _, _, ar, _, _, kh, vh, _ in heads]
        scores = [_hg_select(p, masks) for p in prods]
        for (sl, ab, _, _, _, _, vh, st), sc, o_inter, st_add in zip(heads, scores, inter, grown):
            o_ref[:, sl] = o_inter + _dot(sc, vh)
            st_new = st * ab[C - 1:C, :] + st_add
            st_s[sl.start // K] = st_new
            st_ref[0, sl.start // K] = st_new

    blk = lambda col: pl.BlockSpec((C, HG_WIDTH), lambda c, col=col: (c, col))
    return _pallas(
        body, comm=comm, edge=_grid_edge(n_chunks), name="hgrn2_fwd", grid=(n_chunks,),
        in_specs=[blk(0), blk(1), blk(2), _full_spec((2, HG_WIDTH)), _full_spec(msum.shape)],
        out_specs=[blk(0), pl.BlockSpec((1, H, K, K), lambda c: (c, 0, 0, 0))],
        out_shape=[jax.ShapeDtypeStruct((T, HG_WIDTH), F32), jax.ShapeDtypeStruct((n_chunks, H, K, K), F32)],
        scratch_shapes=[pltpu.VMEM((H, K, K), F32)],
        compiler_params=_params(("arbitrary",)),
    )(proj_h, proj_h, proj_h, gamma, msum)


def hgrn2_bwd(proj_h, gamma, states, do, comm=None):
    T = proj_h.shape[0]
    C, K, H = HG_CHUNK, HG_DK, HG_HEADS
    n_chunks = T // C
    n_sums = 2 + len(HG_LEVELS)
    msum = jnp.asarray(_hg_sum_matrix(), BF16)
    msum_t = jnp.asarray(_hg_sum_matrix().T, BF16)

    def body(hq_ref, hf_ref, hi_ref, do_ref, gam_ref, msum_ref, msum_t_ref, st_in_ref,
             dhq_ref, dhf_ref, dhi_ref, pack_ref, dst_s, dlb_s, dq_s, dk_s, de_s):
        step = pl.program_id(0)
        _zero_first(dst_s)
        _zero_first(dlb_s)
        _zero_first(pack_ref)
        hq = hq_ref[...]
        q, k, g, aux = _hg_gates(hq, hf_ref[...], gam_ref[...])
        v = hi_ref[...]
        do_all = do_ref[...]
        A = jnp.exp(_sum01_left(msum_ref[...], g))
        masks = _hg_level_masks()
        is_last_row = lax.broadcasted_iota(jnp.int32, (C, K), 0) == C - 1
        has_prev = (step < n_chunks - 1).astype(F32)
        heads = []
        for h in range(H):
            sl = slice(K * h, K * (h + 1))
            ab, ar, levels = _hg_head_decays(A, h)
            heads.append(dict(h=h, sl=sl, ab=ab, ar=ar, levels=levels, q=q[:, sl], k=k[:, sl], v=v[:, sl],
                              do=do_all[:, sl], st_in=st_in_ref[0, h] * has_prev, dst_out=dst_s[h]))
        for hd in heads:
            hd["prods"] = _hg_products(hd["q"], hd["k"], hd["levels"])
            hd["da"] = _dot_nt(hd["do"], hd["v"])
            hd["t1"] = hd["ab"] * _dot(hd["do"], hd["st_in"])
            hd["t2"] = hd["ar"] * _dot(hd["v"], hd["dst_out"])
            hd["dv_state"] = _dot_nt(hd["k"] * hd["ar"], hd["dst_out"])
            hd["dst_add"] = _dot_tn(hd["do"], hd["q"] * hd["ab"])
        for hd in heads:
            hd["sc"] = _hg_select(hd["prods"], masks)
            hd["dam"] = [jnp.where(m, hd["da"], 0.0) for m in masks]
        for hd in heads:
            qh, kh = hd["q"], hd["k"]
            hd["dq_parts"] = [_dot(hd["dam"][0], kh)] + [
                a * _dot(dam, kh * a) for a, dam in zip(hd["levels"][1:], hd["dam"][1:])]
            hd["dk_parts"] = [_dot_tn(hd["dam"][0], qh)] + [
                a * _dot_tn(dam, qh * a) for a, dam in zip(hd["levels"][1:], hd["dam"][1:])]
            hd["dv_intra"] = _dot_tn(hd["sc"], hd["do"])
        for hd in heads:
            h, sl, qh, kh, ab = hd["h"], hd["sl"], hd["q"], hd["k"], hd["ab"]
            decayed = _colsum(hd["st_in"] * hd["dst_out"]) * ab[C - 1:C, :]
            de_s[0:C, sl] = qh * hd["t1"] + jnp.where(is_last_row, decayed, 0.0)
            de_s[C:2 * C, sl] = kh * hd["t2"]
            dq = hd["t1"] + hd["dq_parts"][0]
            dk = hd["t2"] + hd["dk_parts"][0]
            for r, (t1, t2) in enumerate(zip(hd["dq_parts"][1:], hd["dk_parts"][1:])):
                dq = dq + t1
                dk = dk + t2
                de_s[C * (r + 2):C * (r + 3), sl] = qh * t1 + kh * t2
            dhi_ref[:, sl] = hd["dv_intra"] + hd["dv_state"]
            dst_s[h] = hd["dst_out"] * ab[C - 1:C, :] + hd["dst_add"]
            dq_s[:, sl] = dq
            dk_s[:, sl] = dk
        dg = _sum01_left(msum_t_ref[...], de_s[...])
        dk = dk_s[...]
        sq, lb = aux["sq"], aux["lb"]
        dhq_ref[...] = dq_s[...] * (sq * (1.0 + hq * (1.0 - sq)))
        common = dg / aux["f"] - dk
        dhf_ref[...] = (1.0 - lb) * aux["sig"] * aux["nsig"] * common
        dlb_s[...] += _colsum(aux["nsig"] * common)

        @pl.when(step == n_chunks - 1)
        def _():
            dgam = lb * (1.0 - lb) * dlb_s[...]
            pack_ref[ROW_GAMMA:ROW_GAMMA + 1, :HG_WIDTH] = dgam
            pack_ref[ROW_GAMMA:ROW_GAMMA + 1, HG_WIDTH:] = -dgam

    last = n_chunks - 1
    blk = lambda col: pl.BlockSpec((C, HG_WIDTH), lambda c, col=col: (last - c, col))
    return _pallas(
        body, comm=comm, edge=_grid_edge(n_chunks), name="hgrn2_bwd", grid=(n_chunks,),
        in_specs=[blk(0), blk(1), blk(2), blk(0), _full_spec((2, HG_WIDTH)), _full_spec(msum.shape),
                  _full_spec(msum_t.shape),
                  pl.BlockSpec((1, H, K, K), lambda c: (jnp.maximum(last - c - 1, 0), 0, 0, 0))],
        out_specs=[blk(0), blk(0), blk(0), PACK_SPEC],
        out_shape=[jax.ShapeDtypeStruct((T, HG_WIDTH), F32)] * 3 + [jax.ShapeDtypeStruct((8, D_MODEL), F32)],
        scratch_shapes=[pltpu.VMEM((H, K, K), F32), pltpu.VMEM((1, HG_WIDTH), F32), pltpu.VMEM((C, HG_WIDTH), F32),
                        pltpu.VMEM((C, HG_WIDTH), F32), pltpu.VMEM((n_sums * C, HG_WIDTH), F32)],
        compiler_params=_params(("arbitrary",)),
    )(proj_h, proj_h, proj_h, do, gamma, msum, msum_t, states)


def _sb_sum_matrix(inclusive):
    B = SB_BLOCK
    j = np.arange(B)[:, None]
    s = np.arange(B)[None, :]
    tri = (j >= s) if inclusive else (j > s)
    return np.concatenate([tri, np.ones((B, B), bool)], 1).astype(np.float32)


def _sb_prefix_matrix(inclusive):
    B = SB_BLOCK
    j = np.arange(B)[:, None]
    s = np.arange(B)[None, :]
    tri = (j <= s) if inclusive else (j < s)
    return np.concatenate([tri, np.ones((B, B), bool)], 1).astype(np.float32)


def _sb_iotas():
    shape = (SB_BLOCK, SB_BLOCK)
    return lax.broadcasted_iota(jnp.int32, shape, 0), lax.broadcasted_iota(jnp.int32, shape, 1)


def _sb_heads(q, first):
    heads = []
    for g in range(SB_GROUP):
        qg = q[:, SB_BLOCK * g:SB_BLOCK * (g + 1)]
        zero = jnp.zeros_like(qg)
        heads += [(g, jnp.where(first, qg, zero)), (g, jnp.where(first, zero, qg))]
    return heads


def _lanes(x, g):
    return x[:, SB_BLOCK * g:SB_BLOCK * (g + 1)]


def sb_fwd(sqkv, comm=None):
    T = sqkv.shape[0]
    B = SB_BLOCK
    W = SB_GROUP * B
    groups = SB_WIDTH // W
    usum = jnp.asarray(_sb_sum_matrix(False), BF16)

    def body(q_ref, k_ref, v_ref, u_ref, o_ref, tl_ref, first_ref):
        p, i = pl.program_id(0), pl.program_id(1)
        row, lane = _sb_iotas()
        first = lane < SB_DH
        heads = _sb_heads(q_ref[...], first)
        u = u_ref[...]

        def more(loop):
            n, reachable, _ = loop
            return (SB_UNROLL * n <= i) & (reachable > 0)

        def step(loop):
            n, _, state = loop
            blocks = []
            for sub in range(SB_UNROLL):
                j = i - SB_UNROLL * n - sub
                off = pl.multiple_of(jnp.maximum(j, 0) * B, B)
                valid = ((lane + j * B) < (row + i * B)) & (j >= 0)
                blocks.append((k_ref[pl.ds(off, B), :], v_ref[pl.ds(off, B), :], valid))
            z = [[_dot_nt(qh, _lanes(kj, g)) for g, qh in heads] for kj, _, _ in blocks]
            lnb = [[jnp.where(valid, _neg_softplus(zz), 0.0) for zz in zs] for zs, (_, _, valid) in zip(z, blocks)]
            sums = [[_sum01_right(x, u) for x in xs] for xs in lnb]
            out = []
            for h, (carry, acc) in enumerate(state):
                for sub, (_, vj, valid) in enumerate(blocks):
                    expo = z[sub][h] + lnb[sub][h] + carry + sums[sub][h][:, :B]
                    acc = acc + _dot(jnp.where(valid, jnp.exp(expo), 0.0), _lanes(vj, heads[h][0]))
                    carry = carry + sums[sub][h][:, B:]
                out.append((carry, acc))
            state = tuple(out)
            worst = state[0][0]
            for carry, _ in state[1:]:
                worst = jnp.maximum(worst, carry)
            reachable = (jnp.max(worst) > SB_UNDERFLOW_LOG).astype(jnp.int32)
            return n + 1, reachable, state

        zero = jnp.zeros((B, B), F32)
        done, _, state = lax.while_loop(
            more, step, (jnp.int32(0), jnp.int32(1), tuple((zero, zero) for _ in heads)))
        for g in range(SB_GROUP):
            (tot0, acc0), (tot1, acc1) = state[2 * g], state[2 * g + 1]
            o_ref[:, B * g:B * (g + 1)] = jnp.where(first, acc0, acc1)
            tl_ref[:, B * g:B * (g + 1)] = jnp.where(first, tot0, tot1)
        first_ref[p, i] = jnp.maximum(i + 1 - SB_UNROLL * done, 0)

    def edge():
        p, i = pl.program_id(0), pl.program_id(1)
        return (p == 0) & (i == 0), (p == groups - 1) & (i == T // B - 1)

    return _pallas(
        body, comm=comm, edge=edge, name="sb_fwd", grid=(groups, T // B),
        in_specs=[pl.BlockSpec((B, W), lambda p, i: (i, p)),
                  pl.BlockSpec((T, W), lambda p, i: (0, groups + p)),
                  pl.BlockSpec((T, W), lambda p, i: (0, 2 * groups + p)),
                  pl.BlockSpec(usum.shape, lambda p, i: (0, 0))],
        out_specs=[pl.BlockSpec((B, W), lambda p, i: (i, p))] * 2 + [pl.BlockSpec(memory_space=pltpu.SMEM)],
        out_shape=[jax.ShapeDtypeStruct((T, SB_WIDTH), F32)] * 2 + [jax.ShapeDtypeStruct((groups, T // B), jnp.int32)],
        compiler_params=_params(("arbitrary", "arbitrary")),
    )(sqkv, sqkv, sqkv, usum)


def sb_bwd(sqkv, do, tl, first_block):
    T = sqkv.shape[0]
    B = SB_BLOCK
    W = SB_GROUP * B
    groups = SB_WIDTH // W
    upre = jnp.asarray(_sb_prefix_matrix(True), BF16)
    uexc = jnp.asarray(_sb_prefix_matrix(False), BF16)

    def body(q_ref, k_ref, v_ref, do_ref, tl_ref, up_ref, ue_ref, first_ref, dq_ref, dk_ref, dv_ref):
        p, i = pl.program_id(0), pl.program_id(1)

        @pl.when(i == 0)
        def _():
            dk_ref[...] = jnp.zeros(dk_ref.shape, F32)
            dv_ref[...] = jnp.zeros(dv_ref.shape, F32)

        row, lane = _sb_iotas()
        first = lane < SB_DH
        do = do_ref[...]
        tl_all = tl_ref[...]
        heads = []
        for (g, qh), at in zip(_sb_heads(q_ref[...], first), (0, B - 1) * SB_GROUP):
            dog = _lanes(do, g)
            keep = first if at == 0 else jnp.logical_not(first)
            heads.append((g, qh, jnp.where(keep, dog, jnp.zeros_like(dog)).astype(BF16),
                          _lanes(tl_all, g)[:, at:at + 1]))
        up = up_ref[...]
        ue = ue_ref[...]
        start = first_ref[p, i]

        def step(n, state):
            blocks = []
            for sub in range(SB_UNROLL):
                j = start + SB_UNROLL * n + sub
                off = pl.multiple_of(jnp.minimum(j, i) * B, B)
                valid = (lane + j * B) < (row + i * B)
                blocks.append((off, k_ref[pl.ds(off, B), :], v_ref[pl.ds(off, B), :], valid))
            combos = [(s, h) for s in range(SB_UNROLL) for h in range(len(heads))]
            z = {(s, h): _dot_nt(heads[h][1], _lanes(blocks[s][1], heads[h][0])) for s, h in combos}
            da = {(s, h): _dot_nt(heads[h][2], _lanes(blocks[s][2], heads[h][0])) for s, h in combos}
            lnb = {c: jnp.where(blocks[c[0]][3], _neg_softplus(z[c]), 0.0) for c in combos}
            lb = {c: z[c] + lnb[c] for c in combos}
            sums = {c: _sum01_right(lnb[c], up) for c in combos}
            a, w = {}, {}
            seen = [st[0] for st in state]
            for s, h in combos:
                expo = lb[s, h] + (heads[h][3] - seen[h] - sums[s, h][:, :B])
                a[s, h] = jnp.where(blocks[s][3], jnp.exp(expo), 0.0)
                w[s, h] = a[s, h] * da[s, h]
                seen[h] = seen[h] + sums[s, h][:, B:]
            wsums = {c: _sum01_right(w[c], ue) for c in combos}
            dz = {}
            seen_w = [st[1] for st in state]
            for s, h in combos:
                beta = jnp.exp(lb[s, h])
                before = seen_w[h] + wsums[s, h][:, :B]
                dz[s, h] = jnp.where(blocks[s][3], w[s, h] * (1.0 - beta) - before * beta, 0.0)
                seen_w[h] = seen_w[h] + wsums[s, h][:, B:]
            dq = [st[2] for st in state]
            for s, h in combos:
                dq[h] = dq[h] + _dot(dz[s, h], _lanes(blocks[s][1], heads[h][0]))
            for s in range(SB_UNROLL):
                off = blocks[s][0]
                for g in range(SB_GROUP):
                    h0, h1 = 2 * g, 2 * g + 1
                    dk_ref[pl.ds(off, B), B * g:B * (g + 1)] += (_dot_tn(dz[s, h0], heads[h0][1])
                                                                 + _dot_tn(dz[s, h1], heads[h1][1]))
                    dv_ref[pl.ds(off, B), B * g:B * (g + 1)] += (_dot_tn(a[s, h0], heads[h0][2])
                                                                 + _dot_tn(a[s, h1], heads[h1][2]))
            return tuple(zip(seen, seen_w, dq))

        zero = jnp.zeros((B, B), F32)
        trips = (i - start + SB_UNROLL) // SB_UNROLL
        state = lax.fori_loop(0, trips, step, tuple((zero, zero, zero) for _ in heads))
        for g in range(SB_GROUP):
            dq_ref[:, B * g:B * (g + 1)] = jnp.where(first, state[2 * g][2], state[2 * g + 1][2]) * SB_SCALE

    qblk = pl.BlockSpec((B, W), lambda p, i: (i, p))
    full = pl.BlockSpec((T, W), lambda p, i: (0, p))
    return pl.pallas_call(
        body, name="sb_bwd", grid=(groups, T // B),
        in_specs=[qblk, pl.BlockSpec((T, W), lambda p, i: (0, groups + p)),
                  pl.BlockSpec((T, W), lambda p, i: (0, 2 * groups + p)), qblk, qblk,
                  pl.BlockSpec(upre.shape, lambda p, i: (0, 0)), pl.BlockSpec(uexc.shape, lambda p, i: (0, 0)),
                  pl.BlockSpec(memory_space=pltpu.SMEM)],
        out_specs=[qblk, full, full],
        out_shape=[jax.ShapeDtypeStruct((T, SB_WIDTH), F32)] * 3,
        compiler_params=_params(("arbitrary", "arbitrary")),
    )(sqkv, sqkv, sqkv, do, tl, upre, uexc, first_block)


def _mixer_out(o_hg, hg, o_sb, g_hg, g_sb):
    n_hg, r_hg = _rms(o_hg)
    s_hg = _sigmoid(hg)
    n_sb, r_sb = _rms(o_sb)
    return dict(n_hg=n_hg, r_hg=r_hg, s_hg=s_hg, n_sb=n_sb, r_sb=r_sb,
                y_hg=n_hg * g_hg * (hg * s_hg), y_sb=n_sb * g_sb)


def mix_out_fwd(o_hg, proj_h, o_sb, x, norms, g_post, w_out, comm=None):
    T = x.shape[0]

    def body(ohg_ref, hg_ref, osb_ref, x_ref, nrm_ref, gp_ref, w_hbm, cat_ref, mix_ref, h1_ref, w_vmem):
        _load_once(w_hbm, w_vmem)
        nrm = nrm_ref[...]
        m = _mixer_out(ohg_ref[...], hg_ref[...], osb_ref[...], nrm[:, :HG_WIDTH], nrm[:, HG_WIDTH:])
        cat_ref[:, :HG_WIDTH] = m["y_hg"].astype(BF16)
        cat_ref[:, HG_WIDTH:] = m["y_sb"].astype(BF16)
        mix = jnp.dot(cat_ref[...], w_vmem[...], preferred_element_type=F32)
        mix_ref[...] = mix
        mh, _ = _rms(mix)
        h1_ref[...] = x_ref[...] + mh * gp_ref[...]

    return _pallas(
        body, comm=comm, edge=_grid_edge(T // ROW_TILE), name="mix_out_fwd", grid=(T // ROW_TILE,),
        in_specs=[_row_spec(HG_WIDTH), _row_spec(HG_WIDTH, 3), _row_spec(SB_WIDTH), _row_spec(D_MODEL),
                  _full_spec((1, D_MODEL)), _full_spec((1, D_MODEL)), ANY_SPEC],
        out_specs=[_row_spec(D_MODEL)] * 3,
        out_shape=[jax.ShapeDtypeStruct((T, D_MODEL), BF16), jax.ShapeDtypeStruct((T, D_MODEL), F32),
                   jax.ShapeDtypeStruct((T, D_MODEL), F32)],
        scratch_shapes=[pltpu.VMEM(w_out.shape, BF16)],
        compiler_params=_params(("arbitrary",)),
    )(o_hg, proj_h, o_sb, x, norms, g_post, w_out)


def ffn_fwd(h1, g_pre, g_post, w_gu, w_down):
    T = h1.shape[0]
    pw = w_gu.shape[2]

    def body(h1_ref, gpre_ref, gpost_ref, wgu_hbm, wd_hbm, u2_ref, gu_ref, act_ref, y_ref, h2_ref,
             wgu_vmem, wd_vmem, gu_s):
        _load_once(wgu_hbm, wgu_vmem)
        _load_once(wd_hbm, wd_vmem)
        h1v = h1_ref[...]
        hh, _ = _rms(h1v)
        u2 = (hh * gpre_ref[...]).astype(BF16)
        u2_ref[...] = u2
        for q in range(N_CHIPS):
            gu_s[:, pw * q:pw * (q + 1)] = jnp.dot(u2, wgu_vmem[q], preferred_element_type=F32)
        gu_ref[...] = gu_s[...].astype(BF16)
        gate = gu_s[:, :D_FF]
        act = (gate * _sigmoid(gate) * gu_s[:, D_FF:]).astype(BF16)
        act_ref[...] = act
        y = jnp.dot(act, wd_vmem[...], preferred_element_type=F32)
        y_ref[...] = y
        yh, _ = _rms(y)
        h2_ref[...] = h1v + yh * gpost_ref[...]

    return pl.pallas_call(
        body, name="ffn_fwd", grid=(T // ROW_TILE,),
        in_specs=[_row_spec(D_MODEL), _full_spec((1, D_MODEL)), _full_spec((1, D_MODEL)), ANY_SPEC, ANY_SPEC],
        out_specs=[_row_spec(D_MODEL), _row_spec(2 * D_FF), _row_spec(D_FF), _row_spec(D_MODEL), _row_spec(D_MODEL)],
        out_shape=[jax.ShapeDtypeStruct((T, D_MODEL), BF16), jax.ShapeDtypeStruct((T, 2 * D_FF), BF16),
                   jax.ShapeDtypeStruct((T, D_FF), BF16), jax.ShapeDtypeStruct((T, D_MODEL), F32),
                   jax.ShapeDtypeStruct((T, D_MODEL), F32)],
        scratch_shapes=[pltpu.VMEM(w_gu.shape, BF16), pltpu.VMEM(w_down.shape, BF16),
                        pltpu.VMEM((ROW_TILE, 2 * D_FF), F32)],
        compiler_params=_params(("arbitrary",)),
    )(h1, g_pre, g_post, w_gu, w_down)


def ple_loss(h2, p, target, w_ple, w_pg):
    T = h2.shape[0]
    pw = w_ple.shape[2]

    def body(h2_ref, p_ref, t_ref, wple_hbm, wpg_hbm, de_ref, ds_ref, dh2_ref, h2b_ref, pb_ref, pack_ref,
             wple_vmem, wpg_vmem, e_s):
        _load_once(wple_hbm, wple_vmem)
        _load_once(wpg_hbm, wpg_vmem)
        _zero_first(pack_ref)
        h2v = h2_ref[...]
        h2b = h2v.astype(BF16)
        h2b_ref[...] = h2b
        pb = p_ref[...].astype(BF16)
        pb_ref[...] = pb
        for q in range(N_CHIPS):
            e_s[:, pw * q:pw * (q + 1)] = jnp.dot(pb, wple_vmem[q], preferred_element_type=F32)
        e = e_s[...]
        sig = _sigmoid(jnp.dot(h2b, wpg_vmem[...], preferred_element_type=F32))
        err = h2v + e * sig - t_ref[...]
        part = 0.5 * jnp.sum(jnp.mean(err * err, axis=-1, keepdims=True), axis=0, keepdims=True)
        lane = lax.broadcasted_iota(jnp.int32, (1, D_MODEL), 1)
        pack_ref[ROW_LOSS:ROW_LOSS + 1, :] += jnp.where(lane == 0, part, 0.0)
        dh3 = err * (1.0 / D_MODEL)
        de_ref[...] = (dh3 * sig).astype(BF16)
        ds = (dh3 * e * sig * (1.0 - sig)).astype(BF16)
        ds_ref[...] = ds
        dh2_ref[...] = dh3 + _dot_nt(ds, wpg_vmem[...])

    return pl.pallas_call(
        body, name="ple_loss", grid=(T // ROW_TILE,),
        in_specs=[_row_spec(D_MODEL), _row_spec(p.shape[1]), _row_spec(D_MODEL), ANY_SPEC, ANY_SPEC],
        out_specs=[_row_spec(D_MODEL), _row_spec(D_MODEL), _row_spec(D_MODEL), _row_spec(D_MODEL),
                   _row_spec(p.shape[1]), PACK_SPEC],
        out_shape=[jax.ShapeDtypeStruct((T, D_MODEL), BF16), jax.ShapeDtypeStruct((T, D_MODEL), BF16),
                   jax.ShapeDtypeStruct((T, D_MODEL), F32), jax.ShapeDtypeStruct((T, D_MODEL), BF16),
                   jax.ShapeDtypeStruct(p.shape, BF16), jax.ShapeDtypeStruct((8, D_MODEL), F32)],
        scratch_shapes=[pltpu.VMEM(w_ple.shape, BF16), pltpu.VMEM(w_pg.shape, BF16), pltpu.VMEM((ROW_TILE, D_MODEL), F32)],
        compiler_params=_params(("arbitrary",)),
    )(h2, p, target, w_ple, w_pg)


def ffn_bwd(dh2, y, h1, gu, g_pre, g_post, w_gu, w_down):
    T = h1.shape[0]
    pw = w_gu.shape[2]

    def body(dh2_ref, y_ref, h1_ref, gu_ref, gpre_ref, gpost_ref, wgu_hbm, wd_hbm, dy_ref, dgu_ref, dh1_ref, pack_ref,
             wgu_vmem, wd_vmem):
        _load_once(wgu_hbm, wgu_vmem)
        _load_once(wd_hbm, wd_vmem)
        _zero_first(pack_ref)
        dh2v = dh2_ref[...]
        yh, ry = _rms(y_ref[...])
        dy, dw = _rms_bwd(dh2v, yh, ry, gpost_ref[...])
        pack_ref[ROW_FFN_POST:ROW_FFN_POST + 1, :] += _colsum(dw)
        dyb = dy.astype(BF16)
        dy_ref[...] = dyb
        dact = _dot_nt(dyb, wd_vmem[...])
        gate = gu_ref[:, :D_FF].astype(F32)
        up = gu_ref[:, D_FF:].astype(F32)
        sg = _sigmoid(gate)
        dgu_ref[:, :D_FF] = (dact * up * (sg * (1.0 + gate * (1.0 - sg)))).astype(BF16)
        dgu_ref[:, D_FF:] = (dact * gate * sg).astype(BF16)
        du2 = _dot_nt(dgu_ref[:, :pw], wgu_vmem[0])
        for q in range(1, N_CHIPS):
            du2 = du2 + _dot_nt(dgu_ref[:, pw * q:pw * (q + 1)], wgu_vmem[q])
        hh, rh = _rms(h1_ref[...])
        dh, dw = _rms_bwd(du2, hh, rh, gpre_ref[...])
        pack_ref[ROW_FFN_PRE:ROW_FFN_PRE + 1, :] += _colsum(dw)
        dh1_ref[...] = dh2v + dh

    return pl.pallas_call(
        body, name="ffn_bwd", grid=(T // ROW_TILE,),
        in_specs=[_row_spec(D_MODEL), _row_spec(D_MODEL), _row_spec(D_MODEL), _row_spec(2 * D_FF),
                  _full_spec((1, D_MODEL)), _full_spec((1, D_MODEL)), ANY_SPEC, ANY_SPEC],
        out_specs=[_row_spec(D_MODEL), _row_spec(2 * D_FF), _row_spec(D_MODEL), PACK_SPEC],
        out_shape=[jax.ShapeDtypeStruct((T, D_MODEL), BF16), jax.ShapeDtypeStruct((T, 2 * D_FF), BF16),
                   jax.ShapeDtypeStruct((T, D_MODEL), F32), jax.ShapeDtypeStruct((8, D_MODEL), F32)],
        scratch_shapes=[pltpu.VMEM(w_gu.shape, BF16), pltpu.VMEM(w_down.shape, BF16)],
        compiler_params=_params(("arbitrary",)),
    )(dh2, y, h1, gu, g_pre, g_post, w_gu, w_down)


def mix_out_bwd(dh1, mix, o_hg, proj_h, o_sb, norms, g_post, w_out, comm=None):
    T = dh1.shape[0]

    def body(dh1_ref, mix_ref, ohg_ref, hg_ref, osb_ref, nrm_ref, gp_ref, w_hbm, dmix_ref, dohg_ref, dhg_ref, dosb_ref,
             pack_ref, w_vmem):
        _load_once(w_hbm, w_vmem)
        _zero_first(pack_ref)
        mh, rm = _rms(mix_ref[...])
        dmix, dw = _rms_bwd(dh1_ref[...], mh, rm, gp_ref[...])
        pack_ref[ROW_ATTN_POST:ROW_ATTN_POST + 1, :] += _colsum(dw)
        dmb = dmix.astype(BF16)
        dmix_ref[...] = dmb
        dcat = _dot_nt(dmb, w_vmem[...])
        nrm = nrm_ref[...]
        g_hg, g_sb = nrm[:, :HG_WIDTH], nrm[:, HG_WIDTH:]
        hg = hg_ref[...]
        m = _mixer_out(ohg_ref[...], hg, osb_ref[...], g_hg, g_sb)
        d_hg = dcat[:, :HG_WIDTH]
        silu = hg * m["s_hg"]
        dhg_ref[...] = d_hg * (m["n_hg"] * g_hg) * (m["s_hg"] * (1.0 + hg * (1.0 - m["s_hg"])))
        dx, dw = _rms_bwd(d_hg * silu, m["n_hg"], m["r_hg"], g_hg)
        dohg_ref[...] = dx
        pack_ref[ROW_MIX_NORMS:ROW_MIX_NORMS + 1, :HG_WIDTH] += _colsum(dw)
        dx, dw = _rms_bwd(dcat[:, HG_WIDTH:], m["n_sb"], m["r_sb"], g_sb)
        dosb_ref[...] = dx
        pack_ref[ROW_MIX_NORMS:ROW_MIX_NORMS + 1, HG_WIDTH:] += _colsum(dw)

    return _pallas(
        body, comm=comm, edge=_grid_edge(T // ROW_TILE), name="mix_out_bwd", grid=(T // ROW_TILE,),
        in_specs=[_row_spec(D_MODEL), _row_spec(D_MODEL), _row_spec(HG_WIDTH), _row_spec(HG_WIDTH, 3), _row_spec(SB_WIDTH),
                  _full_spec((1, D_MODEL)), _full_spec((1, D_MODEL)), ANY_SPEC],
        out_specs=[_row_spec(D_MODEL), _row_spec(HG_WIDTH), _row_spec(HG_WIDTH), _row_spec(SB_WIDTH), PACK_SPEC],
        out_shape=[jax.ShapeDtypeStruct((T, D_MODEL), BF16), jax.ShapeDtypeStruct((T, HG_WIDTH), F32),
                   jax.ShapeDtypeStruct((T, HG_WIDTH), F32), jax.ShapeDtypeStruct((T, SB_WIDTH), F32),
                   jax.ShapeDtypeStruct((8, D_MODEL), F32)],
        scratch_shapes=[pltpu.VMEM(w_out.shape, BF16)],
        compiler_params=_params(("arbitrary",)),
    )(dh1, mix, o_hg, proj_h, o_sb, norms, g_post, w_out)


def in_proj_bwd(parts, x, dh1, g_pre, w_in):
    T = x.shape[0]
    pw = w_in.shape[2]
    n_parts = len(parts)

    def body(*refs):
        part_refs = refs[:n_parts]
        x_ref, dh1_ref, g_ref, w_hbm, dproj_ref, dx_ref, pack_ref, w_vmem = refs[n_parts:]
        _load_once(w_hbm, w_vmem)
        _zero_first(pack_ref)
        for n, ref in enumerate(part_refs):
            dproj_ref[:, HG_WIDTH * n:HG_WIDTH * (n + 1)] = ref[...].astype(BF16)
        du = _dot_nt(dproj_ref[:, :pw], w_vmem[0])
        for q in range(1, N_CHIPS):
            du = du + _dot_nt(dproj_ref[:, pw * q:pw * (q + 1)], w_vmem[q])
        xh, r = _rms(x_ref[...])
        dx, dw = _rms_bwd(du, xh, r, g_ref[...])
        pack_ref[ROW_ATTN_PRE:ROW_ATTN_PRE + 1, :] += _colsum(dw)
        dx_ref[...] = dh1_ref[...] + dx

    return pl.pallas_call(
        body, name="in_proj_bwd", grid=(T // ROW_TILE,),
        in_specs=[_row_spec(HG_WIDTH)] * n_parts + [_row_spec(D_MODEL), _row_spec(D_MODEL), _full_spec((1, D_MODEL)), ANY_SPEC],
        out_specs=[_row_spec(n_parts * HG_WIDTH), _row_spec(D_MODEL), PACK_SPEC],
        out_shape=[jax.ShapeDtypeStruct((T, n_parts * HG_WIDTH), BF16), jax.ShapeDtypeStruct((T, D_MODEL), F32),
                   jax.ShapeDtypeStruct((8, D_MODEL), F32)],
        scratch_shapes=[pltpu.VMEM(w_in.shape, BF16)],
        compiler_params=_params(("arbitrary",)),
    )(*parts, x, dh1, g_pre, w_in)


def weight_grad(a, g, name, *, tm, tn, tk=512, col_pieces=False):
    T, M = a.shape
    N = g.shape[1]
    tk = min(tk, T)
    steps = T // tk

    def body(a_ref, g_ref, o_ref):
        @pl.when(pl.program_id(2) == 0)
        def _():
            o_ref[...] = jnp.zeros(o_ref.shape, F32)

        o_ref[...] += _dot_tn(a_ref[...], g_ref[...]).reshape(o_ref.shape)

    if col_pieces:
        out_shape = jax.ShapeDtypeStruct((N // tn, M, tn), F32)
        out_spec = pl.BlockSpec((1, tm, tn), lambda i, j, k: (j, i, 0))
    else:
        out_shape = jax.ShapeDtypeStruct((M, N), F32)
        out_spec = pl.BlockSpec((tm, tn), lambda i, j, k: (i, j))
    return pl.pallas_call(
        body, name=name, grid=(M // tm, N // tn, steps),
        in_specs=[pl.BlockSpec((tk, tm), lambda i, j, k: (k, i)), pl.BlockSpec((tk, tn), lambda i, j, k: (k, j))],
        out_specs=out_spec, out_shape=out_shape,
        compiler_params=_params(("arbitrary", "arbitrary", "arbitrary")),
    )(a, g)


def _place():
    x, y, c = lax.axis_index("x"), lax.axis_index("y"), lax.axis_index("c")
    chips = [(1 - x, y), (x, 1 - y), (1 - x, 1 - y)]
    return x, y, c, chips


def _chip_index(cx, cy):
    return 2 * cx + cy


def _rcopy(src, dst, send_sem, recv_sem, device):
    return pltpu.make_async_remote_copy(src_ref=src, dst_ref=dst, send_sem=send_sem, recv_sem=recv_sem,
                                        device_id=device, device_id_type=MESH)


def gather_weights(shards):
    n = len(shards)

    def body(*refs):
        ins, outs = refs[:n], refs[2 * n:3 * n]
        send_sems, recv_sems = refs[3 * n:]
        x, y, c, chips = _place()
        me = _chip_index(x, y)
        sibling = (x, y, 1 - c)

        def rows(w, core):
            half = ins[w].shape[0] // 2
            return pl.ds(core * half, half)

        sends = []
        for w in range(n):
            for j, chip in enumerate(chips):
                sends.append(_rcopy(ins[w].at[rows(w, c)], outs[w].at[me, rows(w, c)],
                                    send_sems.at[6 * w + j], recv_sems.at[6 * w + j], (*chip, c)))
        for cp in sends:
            cp.start()
        passed = []
        for w in range(n):
            for j, chip in enumerate(chips):
                block = outs[w].at[_chip_index(*chip), rows(w, c)]
                _rcopy(block, block, send_sems.at[6 * w + j], recv_sems.at[6 * w + j], (*chip, c)).wait_recv()
                cp = _rcopy(block, block, send_sems.at[6 * w + 3 + j], recv_sems.at[6 * w + 3 + j], sibling)
                cp.start()
                passed.append(cp)
        for w in range(n):
            for j, chip in enumerate(chips):
                block = outs[w].at[_chip_index(*chip), rows(w, 1 - c)]
                _rcopy(block, block, send_sems.at[6 * w + 3 + j], recv_sems.at[6 * w + 3 + j], sibling).wait_recv()
        for cp in sends + passed:
            cp.wait_send()

    filled = [jnp.broadcast_to(s[None], (N_CHIPS,) + s.shape) for s in shards]
    return pl.pallas_call(
        body, name="gather_weights",
        in_specs=[ANY_SPEC] * (2 * n), out_specs=[ANY_SPEC] * n,
        out_shape=[jax.ShapeDtypeStruct(f.shape, f.dtype) for f in filled],
        input_output_aliases={n + w: w for w in range(n)},
        scratch_shapes=[pltpu.SemaphoreType.DMA((6 * n,)), pltpu.SemaphoreType.DMA((6 * n,))],
    )(*shards, *filled)


def _run_comm(comm, name):
    c_in, c_out = len(comm.inputs), len(comm.out_shape)

    def body(*refs):
        parts = refs[:c_in], refs[c_in:c_in + c_out], refs[c_in + c_out:]
        comm.start(*parts)
        comm.finish(*parts)

    return pl.pallas_call(
        body, name=name, in_specs=[ANY_SPEC] * c_in, out_specs=[ANY_SPEC] * c_out, out_shape=comm.out_shape,
        scratch_shapes=comm.scratch, input_output_aliases=comm.aliases)(*comm.inputs)


def _both(first, second):
    n_in, n_out, n_scr = len(first.inputs), len(first.out_shape), len(first.scratch)

    def split(ins, outs, scr):
        return (ins[:n_in], outs[:n_out], scr[:n_scr]), (ins[n_in:], outs[n_out:], scr[n_scr:])

    def start(*refs):
        a, b = split(*refs)
        first.start(*a)
        second.start(*b)

    def finish(*refs):
        a, b = split(*refs)
        first.finish(*a)
        second.finish(*b)

    aliases = dict(first.aliases)
    aliases.update({n_in + i: n_out + o for i, o in second.aliases.items()})
    return Comm(first.inputs + second.inputs, first.out_shape + second.out_shape, aliases,
                first.scratch + second.scratch, start, finish)


def _dma_sems(count):
    return [pltpu.SemaphoreType.DMA((count,)), pltpu.SemaphoreType.DMA((count,))]


def gather_over_ici(shards):
    n = len(shards)

    def copies(ins, outs, sems):
        send_sems, recv_sems = sems
        x, y, c, chips = _place()
        me = _chip_index(x, y)
        pairs = []
        for w in range(n):
            half = shards[w].shape[0] // 2
            rows = pl.ds(c * half, half)
            for j, chip in enumerate(chips):
                k = 3 * w + j
                landed = outs[w].at[_chip_index(*chip), rows]
                pairs.append((_rcopy(ins[w].at[rows], outs[w].at[me, rows], send_sems.at[k], recv_sems.at[k], (*chip, c)),
                              _rcopy(landed, landed, send_sems.at[k], recv_sems.at[k], (*chip, c))))
        return pairs

    def start(*refs):
        for send, _ in copies(*refs):
            send.start()

    def finish(*refs):
        pairs = copies(*refs)
        for _, landed in pairs:
            landed.wait_recv()
        for send, _ in pairs:
            send.wait_send()

    filled = [jnp.broadcast_to(s[None], (N_CHIPS,) + s.shape) for s in shards]
    return Comm(list(shards) + filled, [jax.ShapeDtypeStruct(f.shape, f.dtype) for f in filled],
                {n + w: w for w in range(n)}, _dma_sems(3 * n), start, finish)


def gather_over_d2d(landed):
    n = len(landed)

    def copies(ins, outs, sems):
        send_sems, recv_sems = sems
        x, y, c, chips = _place()
        sibling = (x, y, 1 - c)
        pairs = []
        for w in range(n):
            half = landed[w].shape[1] // 2
            for j, chip in enumerate(chips):
                k = 3 * w + j
                mine = outs[w].at[_chip_index(*chip), pl.ds(c * half, half)]
                theirs = outs[w].at[_chip_index(*chip), pl.ds((1 - c) * half, half)]
                pairs.append((_rcopy(mine, mine, send_sems.at[k], recv_sems.at[k], sibling),
                              _rcopy(theirs, theirs, send_sems.at[k], recv_sems.at[k], sibling)))
        return pairs

    def start(*refs):
        for send, _ in copies(*refs):
            send.start()

    def finish(*refs):
        pairs = copies(*refs)
        for _, arrived in pairs:
            arrived.wait_recv()
        for send, _ in pairs:
            send.wait_send()

    return Comm(list(landed), [jax.ShapeDtypeStruct(a.shape, a.dtype) for a in landed], {w: w for w in range(n)},
                _dma_sems(3 * n), start, finish)


def core_halves(grads):
    n = len(grads)

    def copies(ins, outs, sems):
        send_sems, recv_sems = sems
        x, y, c, _ = _place()
        out = []
        for w in range(n):
            half = grads[w].shape[1] // 2
            out.append(_rcopy(ins[w].at[:, pl.ds((1 - c) * half, half), :], outs[w],
                              send_sems.at[w], recv_sems.at[w], (x, y, 1 - c)))
        return out

    def start(*refs):
        for cp in copies(*refs):
            cp.start()

    def finish(*refs):
        for cp in copies(*refs):
            cp.wait()

    return Comm(list(grads), [jax.ShapeDtypeStruct((g.shape[0], g.shape[1] // 2, g.shape[2]), g.dtype) for g in grads],
                {}, _dma_sems(n), start, finish)


def chip_partials(partials):
    n = len(partials)

    def copies(ins, outs, sems):
        send_sems, recv_sems = sems
        x, y, c, chips = _place()
        me = _chip_index(x, y)
        pairs = []
        for w in range(n):
            for j, chip in enumerate(chips):
                k = 3 * w + j
                landed = outs[w].at[_chip_index(*chip)]
                pairs.append((_rcopy(ins[w].at[_chip_index(*chip)], outs[w].at[me], send_sems.at[k], recv_sems.at[k],
                                     (*chip, c)),
                              _rcopy(landed, landed, send_sems.at[k], recv_sems.at[k], (*chip, c))))
        return pairs

    def start(*refs):
        for send, _ in copies(*refs):
            send.start()

    def finish(*refs):
        pairs = copies(*refs)
        for _, landed in pairs:
            landed.wait_recv()
        for send, _ in pairs:
            send.wait_send()

    me = _chip_index(lax.axis_index("x"), lax.axis_index("y"))
    filled = [jnp.broadcast_to(lax.dynamic_index_in_dim(p, me, 0, keepdims=True), p.shape) for p in partials]
    return Comm(list(partials) + filled, [jax.ShapeDtypeStruct(p.shape, p.dtype) for p in partials],
                {n + w: w for w in range(n)}, _dma_sems(3 * n), start, finish)


def join_core_halves(grads):
    n = len(grads)

    def body(*refs):
        outs = refs[n:2 * n]
        send_sems, recv_sems = refs[2 * n:]
        x, y, c, _ = _place()
        sibling = (x, y, 1 - c)
        copies = []
        for w in range(n):
            half = outs[w].shape[0] // 2
            mine = outs[w].at[pl.ds(c * half, half), :]
            copies.append(_rcopy(mine, mine, send_sems.at[w], recv_sems.at[w], sibling))
        for cp in copies:
            cp.start()
        for w in range(n):
            half = outs[w].shape[0] // 2
            theirs = outs[w].at[pl.ds((1 - c) * half, half), :]
            _rcopy(theirs, theirs, send_sems.at[w], recv_sems.at[w], sibling).wait_recv()
        for cp in copies:
            cp.wait_send()

    return pl.pallas_call(
        body, name="join_core_halves",
        in_specs=[ANY_SPEC] * n, out_specs=[ANY_SPEC] * n,
        out_shape=[jax.ShapeDtypeStruct(g.shape, g.dtype) for g in grads],
        input_output_aliases={w: w for w in range(n)},
        scratch_shapes=[pltpu.SemaphoreType.DMA((n,)), pltpu.SemaphoreType.DMA((n,))],
    )(*grads)


def _elementwise_rows(rows, cap=512):
    for t in range(min(rows, cap), 0, -8):
        if rows % t == 0 and t % 16 == 0:
            return t
    return rows


def add_core_halves(grad, got, core, name):
    _, rows, cols = got.shape
    tr = _elementwise_rows(rows)
    nt = rows // tr

    def body(core_ref, a_ref, b_ref, o_ref):
        o_ref[...] = (a_ref[...] + b_ref[...]).astype(BF16)

    spec = pl.BlockSpec((1, tr, cols), lambda q, i, core_ref: (q, i, 0))
    own = pl.BlockSpec((1, tr, cols), lambda q, i, core_ref: (q, core_ref[0] * nt + i, 0))
    return pl.pallas_call(
        body, name=name,
        grid_spec=pltpu.PrefetchScalarGridSpec(num_scalar_prefetch=1, grid=(N_CHIPS, nt), in_specs=[own, spec],
                                               out_specs=spec),
        out_shape=jax.ShapeDtypeStruct(got.shape, BF16),
        compiler_params=_params(("arbitrary", "arbitrary")),
    )(core, grad, got)


def add_chip_partials(parts, core, name):
    _, rows, cols = parts.shape
    tr = _elementwise_rows(rows)
    nt = rows // tr

    def body(core_ref, p_ref, o_ref):
        acc = p_ref[0].astype(F32)
        for q in range(1, N_CHIPS):
            acc = acc + p_ref[q].astype(F32)
        o_ref[...] = acc

    return pl.pallas_call(
        body, name=name,
        grid_spec=pltpu.PrefetchScalarGridSpec(
            num_scalar_prefetch=1, grid=(nt,),
            in_specs=[pl.BlockSpec((N_CHIPS, tr, cols), lambda i, core_ref: (0, i, 0))],
            out_specs=pl.BlockSpec((tr, cols), lambda i, core_ref: (core_ref[0] * nt + i, 0))),
        out_shape=jax.ShapeDtypeStruct((2 * rows, cols), F32),
        compiler_params=_params(("arbitrary",)),
    )(core, parts)


def _adamw_math(w, g, m, v):
    m = ADAM_B1 * m + (1.0 - ADAM_B1) * g
    v = ADAM_B2 * v + (1.0 - ADAM_B2) * (g * g)
    m_hat = m / (1.0 - ADAM_B1 ** ADAM_STEP)
    v_hat = v / (1.0 - ADAM_B2 ** ADAM_STEP)
    delta = -ADAM_LR * (m_hat / (jnp.sqrt(v_hat) + ADAM_EPS) + ADAM_WD * w)
    return delta, m, v


def adamw(w, g, m, v, name):
    rows, cols = w.shape
    tr = _elementwise_rows(rows, 256)

    def body(w_ref, g_ref, m_ref, v_ref, d_ref, nm_ref, nv_ref):
        d, nm, nv = _adamw_math(w_ref[...], g_ref[...], m_ref[...], v_ref[...])
        d_ref[...] = d
        nm_ref[...] = nm
        nv_ref[...] = nv

    spec = pl.BlockSpec((tr, cols), lambda i: (i, 0))
    return pl.pallas_call(
        body, name=name, grid=(rows // tr,), in_specs=[spec] * 4, out_specs=[spec] * 3,
        out_shape=[jax.ShapeDtypeStruct((rows, cols), F32)] * 3,
        compiler_params=_params(("arbitrary",)),
    )(w, g, m, v)


def reduce_small(packs, w, m, v):
    n = len(packs)
    n_dev = 8
    flips = [(fx, fy, fc) for fx in (0, 1) for fy in (0, 1) for fc in (0, 1)][1:]

    def body(*refs):
        pack_refs = refs[:n]
        w_ref, m_ref, v_ref, g_out, d_out, m_out, v_out, mine, slots, send_sems, recv_sems = refs[n:]
        x, y, c, _ = _place()
        me = 4 * x + 2 * y + c
        acc = pack_refs[0][...]
        for ref in pack_refs[1:]:
            acc = acc + ref[...]
        mine[...] = acc
        sends = []
        for k, (fx, fy, fc) in enumerate(flips):
            peer = (x ^ fx, y ^ fy, c ^ fc)
            sends.append(_rcopy(mine, slots.at[me], send_sems.at[k], recv_sems.at[me], peer))
        for cp in sends:
            cp.start()
        slots[me] = acc
        for fx, fy, fc in flips:
            src = 4 * (x ^ fx) + 2 * (y ^ fy) + (c ^ fc)
            _rcopy(mine, slots.at[src], send_sems.at[0], recv_sems.at[src], (x, y, c)).wait_recv()
        for cp in sends:
            cp.wait_send()
        total = slots[0]
        for d in range(1, n_dev):
            total = total + slots[d]
        g_out[...] = total
        d, nm, nv = _adamw_math(w_ref[...], total, m_ref[...], v_ref[...])
        d_out[...] = d
        m_out[...] = nm
        v_out[...] = nv

    vm = pl.BlockSpec(memory_space=pltpu.VMEM)
    return pl.pallas_call(
        body, name="reduce_small",
        in_specs=[vm] * (n + 3), out_specs=[vm] * 4,
        out_shape=[jax.ShapeDtypeStruct((8, D_MODEL), F32)] * 4,
        scratch_shapes=[pltpu.VMEM((8, D_MODEL), F32), pltpu.VMEM((n_dev, 8, D_MODEL), F32),
                        pltpu.SemaphoreType.DMA((len(flips),)), pltpu.SemaphoreType.DMA((n_dev,))],
    )(*packs, w, m, v)


def _pack_small(attn_pre, gamma, hg_norm, sb_norm, attn_post, ffn_pre, ffn_post):
    rows = [attn_pre, gamma.reshape(1, D_MODEL), jnp.concatenate([hg_norm, sb_norm], axis=1), attn_post, ffn_pre, ffn_post,
            jnp.zeros((2, D_MODEL), F32)]
    return jnp.concatenate(rows, axis=0)


def _unpack_small(pack):
    return (pack[ROW_ATTN_PRE:ROW_ATTN_PRE + 1], pack[ROW_GAMMA].reshape(2, HG_WIDTH),
            pack[ROW_MIX_NORMS:ROW_MIX_NORMS + 1, :HG_WIDTH], pack[ROW_MIX_NORMS:ROW_MIX_NORMS + 1, HG_WIDTH:],
            pack[ROW_ATTN_POST:ROW_ATTN_POST + 1], pack[ROW_FFN_PRE:ROW_FFN_PRE + 1], pack[ROW_FFN_POST:ROW_FFN_POST + 1])


def kernel(x, p, attn_pre_norm, w_in, hg_lower_gamma, hg_out_norm, sb_out_norm, w_out, attn_post_norm, ffn_pre_norm, w_gate_up, w_down, ffn_post_norm, ple_proj, ple_gate, loss_target, m_attn_pre_norm, m_w_in, m_hg_lower_gamma, m_hg_out_norm, m_sb_out_norm, m_w_out, m_attn_post_norm, m_ffn_pre_norm, m_w_gate_up, m_w_down, m_ffn_post_norm, m_ple_proj, m_ple_gate, v_attn_pre_norm, v_w_in, v_hg_lower_gamma, v_hg_out_norm, v_sb_out_norm, v_w_out, v_attn_post_norm, v_ffn_pre_norm, v_w_gate_up, v_w_down, v_ffn_post_norm, v_ple_proj, v_ple_gate):
    x2 = x[0]
    p2 = p[0, 0]
    target = loss_target[0]
    big = dict(w_in=(w_in, m_w_in, v_w_in), w_out=(w_out, m_w_out, v_w_out), w_gate_up=(w_gate_up, m_w_gate_up, v_w_gate_up),
               w_down=(w_down, m_w_down, v_w_down), ple_proj=(ple_proj, m_ple_proj, v_ple_proj),
               ple_gate=(ple_gate, m_ple_gate, v_ple_gate))
    names = list(big)
    big = {k: tuple(a[0] for a in t) for k, t in big.items()}

    shard16 = {k: big[k][0].astype(BF16) for k in names}
    w_in_full, = gather_weights([shard16["w_in"]])
    mix_norms = jnp.concatenate([hg_out_norm, sb_out_norm], axis=1)
    small_ones = ["w_out", "ple_proj", "ple_gate"]

    proj_h, sqkv, u1, *landed_small = in_proj_fwd(
        x2, attn_pre_norm, w_in_full, comm=gather_over_ici([shard16[k] for k in small_ones]))
    o_sb, sb_totals, sb_first, landed_gu = sb_fwd(sqkv, comm=gather_over_ici([shard16["w_gate_up"]]))
    o_hg, states, landed_down, *full_small = hgrn2_fwd(
        proj_h, hg_lower_gamma, comm=_both(gather_over_ici([shard16["w_down"]]), gather_over_d2d(landed_small)))
    full = dict(zip(small_ones, full_small), w_in=w_in_full)
    w_out_full = full["w_out"].reshape(D_MODEL, D_MODEL)
    w_pg_full = full["ple_gate"].reshape(D_MODEL, D_MODEL)
    cat, mix, h1, full["w_gate_up"], full["w_down"] = mix_out_fwd(
        o_hg, proj_h, o_sb, x2, mix_norms, attn_post_norm, w_out_full, comm=gather_over_d2d([landed_gu, landed_down]))
    w_down_full = full["w_down"].reshape(D_FF, D_MODEL)
    u2, gu, act, y, h2 = ffn_fwd(h1, ffn_pre_norm, ffn_post_norm, full["w_gate_up"], w_down_full)

    core = lax.axis_index("c").astype(jnp.int32).reshape(1)
    de, ds, dh2, h2b, pb, pack_loss = ple_loss(h2, p2, target, full["ple_proj"], w_pg_full)
    dy, dgu, dh1, pack_ffn = ffn_bwd(dh2, y, h1, gu, ffn_pre_norm, ffn_post_norm, full["w_gate_up"], w_down_full)
    local = dict(
        w_gate_up=weight_grad(u2, dgu, "grad_w_gate_up", tm=D_MODEL, tn=full["w_gate_up"].shape[2], tk=1024, col_pieces=True),
        w_down=weight_grad(act, dy, "grad_w_down", tm=D_FF // 2, tn=D_MODEL, tk=1024).reshape(full["w_down"].shape),
        ple_proj=weight_grad(pb, de, "grad_ple_proj", tm=pb.shape[1], tn=full["ple_proj"].shape[2], col_pieces=True),
        ple_gate=weight_grad(h2b, ds, "grad_ple_gate", tm=D_MODEL, tn=D_MODEL).reshape(full["ple_gate"].shape),
    )
    early = list(local)
    dmix, do_hg, dhg, do_sb, pack_mix, *got = mix_out_bwd(
        dh1, mix, o_hg, proj_h, o_sb, mix_norms, attn_post_norm, w_out_full, comm=core_halves([local[k] for k in early]))
    partial = [add_core_halves(local[k], g, core, "add_core_halves_" + k) for k, g in zip(early, got)]
    dsq, dsk, dsv = sb_bwd(sqkv, do_sb, sb_totals, sb_first)
    dhq, dhf, dhi, pack_hg, *by_source = hgrn2_bwd(proj_h, hg_lower_gamma, states, do_hg, comm=chip_partials(partial))
    halves = {k: add_chip_partials(s, core, "add_chip_partials_" + k) for k, s in zip(early, by_source)}
    dproj, grad_x, pack_in = in_proj_bwd([dhq, dhf, dhi, dhg, dsq, dsk, dsv], x2, dh1, attn_pre_norm, full["w_in"])

    late = ["w_in", "w_out"]
    local["w_in"] = weight_grad(u1, dproj, "grad_w_in", tm=D_MODEL, tn=full["w_in"].shape[2], tk=1024, col_pieces=True)
    local["w_out"] = weight_grad(cat, dmix, "grad_w_out", tm=D_MODEL, tn=D_MODEL).reshape(full["w_out"].shape)
    got = _run_comm(core_halves([local[k] for k in late]), "exchange_core_halves")
    partial = [add_core_halves(local[k], g, core, "add_core_halves_" + k) for k, g in zip(late, got)]
    by_source = _run_comm(chip_partials(partial), "exchange_chip_partials")
    halves.update({k: add_chip_partials(s, core, "add_chip_partials_" + k) for k, s in zip(late, by_source)})
    grads = dict(zip(names, join_core_halves([halves[k] for k in names])))

    upd = {k: adamw(big[k][0], grads[k], big[k][1], big[k][2], "adamw_" + k) for k in names}

    small = reduce_small(
        [pack_loss, pack_ffn, pack_mix, pack_hg, pack_in],
        _pack_small(attn_pre_norm, hg_lower_gamma, hg_out_norm, sb_out_norm, attn_post_norm, ffn_pre_norm, ffn_post_norm),
        _pack_small(m_attn_pre_norm, m_hg_lower_gamma, m_hg_out_norm, m_sb_out_norm, m_attn_post_norm, m_ffn_pre_norm, m_ffn_post_norm),
        _pack_small(v_attn_pre_norm, v_hg_lower_gamma, v_hg_out_norm, v_sb_out_norm, v_attn_post_norm, v_ffn_pre_norm, v_ffn_post_norm),
    )
    loss = small[0][ROW_LOSS, 0]
    s_grad, s_delta, s_m, s_v = (_unpack_small(t) for t in small)

    def ordered(small_vals, big_vals):
        a_pre, gam, hg_n, sb_n, a_post, f_pre, f_post = small_vals
        b = {k: big_vals[k][None] for k in names}
        return (a_pre, b["w_in"], gam, hg_n, sb_n, b["w_out"], a_post, f_pre, b["w_gate_up"], b["w_down"], f_post,
                b["ple_proj"], b["ple_gate"])

    return (loss, grad_x[None],
            *ordered(s_grad, grads),
            *ordered(s_delta, {k: upd[k][0] for k in names}),
            *ordered(s_m, {k: upd[k][1] for k in names}),
            *ordered(s_v, {k: upd[k][2] for k in names}))
```

```python
from typing import Callable, NamedTuple

import numpy as np
import jax
import jax.numpy as jnp
from jax import lax
from jax.experimental import pallas as pl
from jax.experimental.pallas import tpu as pltpu

F32 = jnp.float32
BF16 = jnp.bfloat16
MESH = pl.DeviceIdType.MESH

RMS_EPS = 1e-6
D_MODEL = 1024
HG_WIDTH = 512
HG_HEADS = 4
HG_DK = 128
HG_CHUNK = 64
HG_LEVELS = (32, 16, 8, 4, 2, 1)
HG_CHUNKS_PER_STEP = 2
SB_WIDTH = 512
SB_BLOCK = 128
SB_DH = 64
SB_SCALE = SB_DH ** -0.5
SB_UNDERFLOW_LOG = -87.5
SB_UNROLL = 2
SB_GROUP = 2
D_FF = 2816
N_CHIPS = 4
ROW_TILE = 256
V7X_VMEM_LIMIT = 56 * 1024 * 1024

ADAM_LR = 0.001
ADAM_B1 = 0.9
ADAM_B2 = 0.999
ADAM_EPS = 1e-08
ADAM_WD = 0.01
ADAM_STEP = 10

ROW_ATTN_PRE, ROW_GAMMA, ROW_MIX_NORMS, ROW_ATTN_POST, ROW_FFN_PRE, ROW_FFN_POST, ROW_LOSS = range(7)


def _params(sem=None, vmem=V7X_VMEM_LIMIT):
    return pltpu.CompilerParams(dimension_semantics=sem, vmem_limit_bytes=vmem)


def _dot(a, b):
    return jnp.dot(a.astype(BF16), b.astype(BF16), preferred_element_type=F32)


def _dot_nt(a, b):
    return lax.dot_general(a.astype(BF16), b.astype(BF16), (((1,), (1,)), ((), ())), preferred_element_type=F32)


def _dot_tn(a, b):
    return lax.dot_general(a.astype(BF16), b.astype(BF16), (((0,), (0,)), ((), ())), preferred_element_type=F32)


def _split(x):
    hi = x.astype(BF16)
    lo = (x - hi.astype(F32)).astype(BF16)
    return hi, lo


def _sum01_left(m01, x):
    hi, lo = _split(x)
    return jnp.dot(m01, hi, preferred_element_type=F32) + jnp.dot(m01, lo, preferred_element_type=F32)


def _sum01_right(x, m01):
    hi, lo = _split(x)
    return jnp.dot(hi, m01, preferred_element_type=F32) + jnp.dot(lo, m01, preferred_element_type=F32)


def _rms(x):
    r = lax.rsqrt(jnp.mean(x * x, axis=-1, keepdims=True) + RMS_EPS)
    return x * r, r


def _rms_bwd(dy, xhat, r, w):
    dxh = dy * w
    dx = r * (dxh - xhat * jnp.mean(dxh * xhat, axis=-1, keepdims=True))
    return dx, dy * xhat


def _sigmoid(x):
    return 1.0 / (1.0 + jnp.exp(-x))


def _neg_softplus(z):
    return -(jnp.maximum(z, 0.0) + jnp.log(1.0 + jnp.exp(-jnp.abs(z))))


def _colsum(x):
    return jnp.sum(x, axis=0, keepdims=True)


def _load_once(src_hbm, dst_vmem):
    @pl.when(pl.program_id(0) == 0)
    def _():
        pltpu.sync_copy(src_hbm, dst_vmem)


def _zero_first(ref):
    @pl.when(pl.program_id(0) == 0)
    def _():
        ref[...] = jnp.zeros(ref.shape, ref.dtype)


def _row_spec(width, col=0):
    return pl.BlockSpec((ROW_TILE, width), lambda i, col=col: (i, col))


def _full_spec(shape):
    return pl.BlockSpec(shape, lambda *_: (0,) * len(shape))


ANY_SPEC = pl.BlockSpec(memory_space=pl.ANY)
PACK_SPEC = _full_spec((8, D_MODEL))


class Comm(NamedTuple):
    inputs: list
    out_shape: list
    aliases: dict
    scratch: list
    start: Callable
    finish: Callable


def _pallas(body, *, comm=None, edge=None, in_specs, out_specs, out_shape, scratch_shapes=(), **kw):
    if comm is None:
        return pl.pallas_call(body, in_specs=in_specs, out_specs=out_specs, out_shape=out_shape,
                              scratch_shapes=scratch_shapes, **kw)
    n_in, n_out, n_scr = len(in_specs), len(out_specs), len(scratch_shapes)
    c_in, c_out = len(comm.inputs), len(comm.out_shape)

    def both(*refs):
        ins, c_ins = refs[:n_in], refs[n_in:n_in + c_in]
        outs = refs[n_in + c_in:n_in + c_in + n_out]
        c_outs = refs[n_in + c_in + n_out:n_in + c_in + n_out + c_out]
        rest = refs[n_in + c_in + n_out + c_out:]
        scr, c_scr = rest[:n_scr], rest[n_scr:]
        first, last = edge()

        @pl.when(first)
        def _():
            comm.start(c_ins, c_outs, c_scr)

        body(*ins, *outs, *scr)

        @pl.when(last)
        def _():
            comm.finish(c_ins, c_outs, c_scr)

    call = pl.pallas_call(
        both, in_specs=list(in_specs) + [ANY_SPEC] * c_in, out_specs=list(out_specs) + [ANY_SPEC] * c_out,
        out_shape=list(out_shape) + list(comm.out_shape), scratch_shapes=list(scratch_shapes) + list(comm.scratch),
        input_output_aliases={n_in + a: n_out + b for a, b in comm.aliases.items()}, **kw)
    return lambda *args: call(*args, *comm.inputs)


def _grid_edge(steps):
    return lambda: (pl.program_id(0) == 0, pl.program_id(0) == steps - 1)


def in_proj_fwd(x, g_pre, w_in, comm=None):
    T = x.shape[0]
    pw = w_in.shape[2]

    def body(x_ref, g_ref, w_hbm, ph_ref, sqkv_ref, u_ref, w_vmem, proj_s):
        _load_once(w_hbm, w_vmem)
        xh, _ = _rms(x_ref[...])
        u = (xh * g_ref[...]).astype(BF16)
        u_ref[...] = u
        for q in range(N_CHIPS):
            proj_s[:, pw * q:pw * (q + 1)] = jnp.dot(u, w_vmem[q], preferred_element_type=F32)
        ph_ref[...] = proj_s[:, :4 * HG_WIDTH]
        sqkv_ref[:, :SB_WIDTH] = (proj_s[:, 4 * HG_WIDTH:4 * HG_WIDTH + SB_WIDTH] * SB_SCALE).astype(BF16)
        sqkv_ref[:, SB_WIDTH:] = proj_s[:, 4 * HG_WIDTH + SB_WIDTH:].astype(BF16)

    return _pallas(
        body, comm=comm, edge=_grid_edge(T // ROW_TILE), name="in_proj_fwd", grid=(T // ROW_TILE,),
        in_specs=[_row_spec(D_MODEL), _full_spec((1, D_MODEL)), ANY_SPEC],
        out_specs=[_row_spec(4 * HG_WIDTH), _row_spec(3 * SB_WIDTH), _row_spec(D_MODEL)],
        out_shape=[jax.ShapeDtypeStruct((T, 4 * HG_WIDTH), F32), jax.ShapeDtypeStruct((T, 3 * SB_WIDTH), BF16),
                   jax.ShapeDtypeStruct((T, D_MODEL), BF16)],
        scratch_shapes=[pltpu.VMEM(w_in.shape, BF16), pltpu.VMEM((ROW_TILE, N_CHIPS * pw), F32)],
        compiler_params=_params(("arbitrary",)),
    )(x, g_pre, w_in)


def _hg_sum_matrix():
    C = HG_CHUNK
    t = np.arange(C)[:, None]
    j = np.arange(C)[None, :]
    mats = [j <= t, j > t]
    for h in HG_LEVELS:
        start = (t // (2 * h)) * (2 * h)
        upper = (t & h) != 0
        mats.append(np.where(upper, (j >= start + h) & (j <= t), (j > t) & (j <= start + h - 1)))
    return np.concatenate(mats, 0).astype(np.float32)


def _hg_level_masks():
    C = HG_CHUNK
    t = lax.broadcasted_iota(jnp.int32, (C, C), 0)
    s = lax.broadcasted_iota(jnp.int32, (C, C), 1)
    x = t ^ s
    masks = [t == s]
    for h in HG_LEVELS:
        masks.append((x >= h) & (x < 2 * h) & (t > s))
    return masks


def _hg_gates(hq, hf, gamma):
    lb = 1.0 / (1.0 + jnp.exp(gamma[1:2, :] - gamma[0:1, :]))
    sq = _sigmoid(hq)
    q = hq * sq
    sig = _sigmoid(hf)
    nsig = _sigmoid(-hf)
    f = lb + (1.0 - lb) * sig
    k = (1.0 - lb) * nsig
    g = jnp.log(f)
    return q, k, g, dict(lb=lb, sq=sq, sig=sig, nsig=nsig, f=f)


def _hg_head_decays(A, h):
    C, K = HG_CHUNK, HG_DK
    sl = slice(K * h, K * (h + 1))
    blocks = [A[C * r:C * (r + 1), sl] for r in range(2 + len(HG_LEVELS))]
    return blocks[0], blocks[1], [None] + blocks[2:]


def _hg_products(q, k, levels):
    return [_dot_nt(q, k)] + [_dot_nt(q * a, k * a) for a in levels[1:]]


def _hg_select(prods, masks):
    sc = jnp.where(masks[0], prods[0], 0.0)
    for p, m in zip(prods[1:], masks[1:]):
        sc = jnp.where(m, p, sc)
    return sc


def hgrn2_fwd(proj_h, gamma, comm=None):
    T = proj_h.shape[0]
    C, K, H, S = HG_CHUNK, HG_DK, HG_HEADS, HG_CHUNKS_PER_STEP
    n_steps = T // (S * C)
    msum = jnp.asarray(_hg_sum_matrix(), BF16)

    def body(hq_ref, hf_ref, hi_ref, gam_ref, msum_ref, o_ref, st_ref, st_s):
        _zero_first(st_s)
        q, k, g, _ = _hg_gates(hq_ref[...], hf_ref[...], gam_ref[...])
        v = hi_ref[...]
        masks = _hg_level_masks()
        parts = []
        for s in range(S):
            rows = slice(C * s, C * (s + 1))
            A = jnp.exp(_sum01_left(msum_ref[...], g[rows]))
            for h in range(H):
                sl = slice(K * h, K * (h + 1))
                ab, ar, levels = _hg_head_decays(A, h)
                parts.append(dict(s=s, h=h, rows=rows, sl=sl, ab=ab, ar=ar, levels=levels,
                                  q=q[rows, sl], k=k[rows, sl], v=v[rows, sl]))
        for pt in parts:
            pt["prods"] = _hg_products(pt["q"], pt["k"], pt["levels"])
            pt["grown"] = _dot_tn(pt["v"], pt["k"] * pt["ar"])
        for pt in parts:
            pt["sc"] = _hg_select(pt["prods"], masks)
        state = [st_s[h] for h in range(H)]
        for pt in parts:
            h, ab = pt["h"], pt["ab"]
            o_ref[pt["rows"], pt["sl"]] = _dot_nt(pt["q"] * ab, state[h]) + _dot(pt["sc"], pt["v"])
            state[h] = state[h] * ab[C - 1:C, :] + pt["grown"]
            st_ref[pt["s"], h] = state[h]
        for h in range(H):
            st_s[h] = state[h]

    blk = lambda col: pl.BlockSpec((S * C, HG_WIDTH), lambda c, col=col: (c, col))
    return _pallas(
        body, comm=comm, edge=_grid_edge(n_steps), name="hgrn2_fwd", grid=(n_steps,),
        in_specs=[blk(0), blk(1), blk(2), _full_spec((2, HG_WIDTH)), _full_spec(msum.shape)],
        out_specs=[blk(0), pl.BlockSpec((S, H, K, K), lambda c: (c, 0, 0, 0))],
        out_shape=[jax.ShapeDtypeStruct((T, HG_WIDTH), F32), jax.ShapeDtypeStruct((S * n_steps, H, K, K), F32)],
        scratch_shapes=[pltpu.VMEM((H, K, K), F32)],
        compiler_params=_params(("arbitrary",)),
    )(proj_h, proj_h, proj_h, gamma, msum)


def hgrn2_bwd(proj_h, gamma, states, do, comm=None):
    T = proj_h.shape[0]
    C, K, H, S = HG_CHUNK, HG_DK, HG_HEADS, HG_CHUNKS_PER_STEP
    n_steps = T // (S * C)
    n_sums = 2 + len(HG_LEVELS)
    msum = jnp.asarray(_hg_sum_matrix(), BF16)
    msum_t = jnp.asarray(_hg_sum_matrix().T, BF16)

    def body(hq_ref, hf_ref, hi_ref, do_ref, gam_ref, msum_ref, msum_t_ref, st_prev_ref, st_ref,
             dhq_ref, dhf_ref, dhi_ref, pack_ref, dst_s, dlb_s, dq_s, dk_s, de_s):
        step = pl.program_id(0)
        _zero_first(dst_s)
        _zero_first(dlb_s)
        _zero_first(pack_ref)
        hq = hq_ref[...]
        q, k, g, aux = _hg_gates(hq, hf_ref[...], gam_ref[...])
        v = hi_ref[...]
        do_all = do_ref[...]
        masks = _hg_level_masks()
        is_last_row = lax.broadcasted_iota(jnp.int32, (C, K), 0) == C - 1
        has_prev = (step < n_steps - 1).astype(F32)
        parts = []
        for s in reversed(range(S)):
            rows = slice(C * s, C * (s + 1))
            A = jnp.exp(_sum01_left(msum_ref[...], g[rows]))
            for h in range(H):
                sl = slice(K * h, K * (h + 1))
                ab, ar, levels = _hg_head_decays(A, h)
                st_in = st_prev_ref[0, h] * has_prev if s == 0 else st_ref[s - 1, h]
                parts.append(dict(s=s, h=h, rows=rows, sl=sl, ab=ab, ar=ar, levels=levels, st_in=st_in,
                                  q=q[rows, sl], k=k[rows, sl], v=v[rows, sl], do=do_all[rows, sl]))
        for pt in parts:
            pt["prods"] = _hg_products(pt["q"], pt["k"], pt["levels"])
            pt["da"] = _dot_nt(pt["do"], pt["v"])
            pt["t1"] = pt["ab"] * _dot(pt["do"], pt["st_in"])
            pt["dst_add"] = _dot_tn(pt["do"], pt["q"] * pt["ab"])
        dstate = [dst_s[h] for h in range(H)]
        for pt in parts:
            h = pt["h"]
            pt["dst_out"] = dstate[h]
            pt["t2"] = pt["ar"] * _dot(pt["v"], dstate[h])
            pt["dv_state"] = _dot_nt(pt["k"] * pt["ar"], dstate[h])
            dstate[h] = dstate[h] * pt["ab"][C - 1:C, :] + pt["dst_add"]
        for h in range(H):
            dst_s[h] = dstate[h]
        for pt in parts:
            pt["sc"] = _hg_select(pt["prods"], masks)
            pt["dam"] = [jnp.where(m, pt["da"], 0.0) for m in masks]
        for pt in parts:
            qh, kh = pt["q"], pt["k"]
            pt["dq_parts"] = [_dot(pt["dam"][0], kh)] + [
                a * _dot(dam, kh * a) for a, dam in zip(pt["levels"][1:], pt["dam"][1:])]
            pt["dk_parts"] = [_dot_tn(pt["dam"][0], qh)] + [
                a * _dot_tn(dam, qh * a) for a, dam in zip(pt["levels"][1:], pt["dam"][1:])]
            pt["dv_intra"] = _dot_tn(pt["sc"], pt["do"])
        for pt in parts:
            s, rows, sl, qh, kh, ab = pt["s"], pt["rows"], pt["sl"], pt["q"], pt["k"], pt["ab"]
            decayed = _colsum(pt["st_in"] * pt["dst_out"]) * ab[C - 1:C, :]
            de_s[s, 0:C, sl] = qh * pt["t1"] + jnp.where(is_last_row, decayed, 0.0)
            de_s[s, C:2 * C, sl] = kh * pt["t2"]
            dq = pt["t1"] + pt["dq_parts"][0]
            dk = pt["t2"] + pt["dk_parts"][0]
            for r, (t1, t2) in enumerate(zip(pt["dq_parts"][1:], pt["dk_parts"][1:])):
                dq = dq + t1
                dk = dk + t2
                de_s[s, C * (r + 2):C * (r + 3), sl] = qh * t1 + kh * t2
            dhi_ref[rows, sl] = pt["dv_intra"] + pt["dv_state"]
            dq_s[rows, sl] = dq
            dk_s[rows, sl] = dk
        dg = jnp.concatenate([_sum01_left(msum_t_ref[...], de_s[s]) for s in range(S)], axis=0)
        dk = dk_s[...]
        sq, lb = aux["sq"], aux["lb"]
        dhq_ref[...] = dq_s[...] * (sq * (1.0 + hq * (1.0 - sq)))
        common = dg / aux["f"] - dk
        dhf_ref[...] = (1.0 - lb) * aux["sig"] * aux["nsig"] * common
        dlb_s[...] += _colsum(aux["nsig"] * common)

        @pl.when(step == n_steps - 1)
        def _():
            dgam = lb * (1.0 - lb) * dlb_s[...]
            pack_ref[ROW_GAMMA:ROW_GAMMA + 1, :HG_WIDTH] = dgam
            pack_ref[ROW_GAMMA:ROW_GAMMA + 1, HG_WIDTH:] = -dgam

    last = n_steps - 1
    blk = lambda col: pl.BlockSpec((S * C, HG_WIDTH), lambda c, col=col: (last - c, col))
    return _pallas(
        body, comm=comm, edge=_grid_edge(n_steps), name="hgrn2_bwd", grid=(n_steps,),
        in_specs=[blk(0), blk(1), blk(2), blk(0), _full_spec((2, HG_WIDTH)), _full_spec(msum.shape),
                  _full_spec(msum_t.shape),
                  pl.BlockSpec((1, H, K, K), lambda c: (jnp.maximum(S * (last - c) - 1, 0), 0, 0, 0)),
                  pl.BlockSpec((S, H, K, K), lambda c: (last - c, 0, 0, 0))],
        out_specs=[blk(0), blk(0), blk(0), PACK_SPEC],
        out_shape=[jax.ShapeDtypeStruct((T, HG_WIDTH), F32)] * 3 + [jax.ShapeDtypeStruct((8, D_MODEL), F32)],
        scratch_shapes=[pltpu.VMEM((H, K, K), F32), pltpu.VMEM((1, HG_WIDTH), F32), pltpu.VMEM((S * C, HG_WIDTH), F32),
                        pltpu.VMEM((S * C, HG_WIDTH), F32), pltpu.VMEM((S, n_sums * C, HG_WIDTH), F32)],
        compiler_params=_params(("arbitrary",)),
    )(proj_h, proj_h, proj_h, do, gamma, msum, msum_t, states, states)


def _sb_sum_matrix(inclusive):
    B = SB_BLOCK
    j = np.arange(B)[:, None]
    s = np.arange(B)[None, :]
    tri = (j >= s) if inclusive else (j > s)
    return np.concatenate([tri, np.ones((B, B), bool)], 1).astype(np.float32)


def _sb_prefix_matrix(inclusive):
    B = SB_BLOCK
    j = np.arange(B)[:, None]
    s = np.arange(B)[None, :]
    tri = (j <= s) if inclusive else (j < s)
    return np.concatenate([tri, np.ones((B, B), bool)], 1).astype(np.float32)


def _sb_iotas():
    shape = (SB_BLOCK, SB_BLOCK)
    return lax.broadcasted_iota(jnp.int32, shape, 0), lax.broadcasted_iota(jnp.int32, shape, 1)


def _sb_heads(q, first):
    heads = []
    for g in range(SB_GROUP):
        qg = q[:, SB_BLOCK * g:SB_BLOCK * (g + 1)]
        zero = jnp.zeros_like(qg)
        heads += [(g, jnp.where(first, qg, zero)), (g, jnp.where(first, zero, qg))]
    return heads


def _lanes(x, g):
    return x[:, SB_BLOCK * g:SB_BLOCK * (g + 1)]


def sb_fwd(sqkv, comm=None):
    T = sqkv.shape[0]
    B = SB_BLOCK
    W = SB_GROUP * B
    groups = SB_WIDTH // W
    usum = jnp.asarray(_sb_sum_matrix(False), BF16)

    def body(q_ref, k_ref, v_ref, u_ref, o_ref, tl_ref, first_ref):
        p, i = pl.program_id(0), pl.program_id(1)
        row, lane = _sb_iotas()
        first = lane < SB_DH
        heads = _sb_heads(q_ref[...], first)
        u = u_ref[...]

        def more(loop):
            n, reachable, _ = loop
            return (SB_UNROLL * n <= i) & (reachable > 0)

        def step(loop):
            n, _, state = loop
            blocks = []
            for sub in range(SB_UNROLL):
                j = i - SB_UNROLL * n - sub
                off = pl.multiple_of(jnp.maximum(j, 0) * B, B)
                valid = ((lane + j * B) < (row + i * B)) & (j >= 0)
                blocks.append((k_ref[pl.ds(off, B), :], v_ref[pl.ds(off, B), :], valid))
            z = [[_dot_nt(qh, _lanes(kj, g)) for g, qh in heads] for kj, _, _ in blocks]
            lnb = [[jnp.where(valid, _neg_softplus(zz), 0.0) for zz in zs] for zs, (_, _, valid) in zip(z, blocks)]
            sums = [[_sum01_right(x, u) for x in xs] for xs in lnb]
            out = []
            for h, (carry, acc) in enumerate(state):
                for sub, (_, vj, valid) in enumerate(blocks):
                    expo = z[sub][h] + lnb[sub][h] + carry + sums[sub][h][:, :B]
                    acc = acc + _dot(jnp.where(valid, jnp.exp(expo), 0.0), _lanes(vj, heads[h][0]))
                    carry = carry + sums[sub][h][:, B:]
                out.append((carry, acc))
            state = tuple(out)
            worst = state[0][0]
            for carry, _ in state[1:]:
                worst = jnp.maximum(worst, carry)
            reachable = (jnp.max(worst) > SB_UNDERFLOW_LOG).astype(jnp.int32)
            return n + 1, reachable, state

        zero = jnp.zeros((B, B), F32)
        done, _, state = lax.while_loop(
            more, step, (jnp.int32(0), jnp.int32(1), tuple((zero, zero) for _ in heads)))
        for g in range(SB_GROUP):
            (tot0, acc0), (tot1, acc1) = state[2 * g], state[2 * g + 1]
            o_ref[:, B * g:B * (g + 1)] = jnp.where(first, acc0, acc1)
            tl_ref[:, B * g:B * (g + 1)] = jnp.where(first, tot0, tot1)
        first_ref[p, i] = jnp.maximum(i + 1 - SB_UNROLL * done, 0)

    def edge():
        p, i = pl.program_id(0), pl.program_id(1)
        return (p == 0) & (i == 0), (p == groups - 1) & (i == T // B - 1)

    return _pallas(
        body, comm=comm, edge=edge, name="sb_fwd", grid=(groups, T // B),
        in_specs=[pl.BlockSpec((B, W), lambda p, i: (i, p)),
                  pl.BlockSpec((T, W), lambda p, i: (0, groups + p)),
                  pl.BlockSpec((T, W), lambda p, i: (0, 2 * groups + p)),
                  pl.BlockSpec(usum.shape, lambda p, i: (0, 0))],
        out_specs=[pl.BlockSpec((B, W), lambda p, i: (i, p))] * 2 + [pl.BlockSpec(memory_space=pltpu.SMEM)],
        out_shape=[jax.ShapeDtypeStruct((T, SB_WIDTH), F32)] * 2 + [jax.ShapeDtypeStruct((groups, T // B), jnp.int32)],
        compiler_params=_params(("arbitrary", "arbitrary")),
    )(sqkv, sqkv, sqkv, usum)


def sb_bwd(sqkv, do, tl, first_block):
    T = sqkv.shape[0]
    B = SB_BLOCK
    W = SB_GROUP * B
    groups = SB_WIDTH // W
    upre = jnp.asarray(_sb_prefix_matrix(True), BF16)
    uexc = jnp.asarray(_sb_prefix_matrix(False), BF16)

    def body(q_ref, k_ref, v_ref, do_ref, tl_ref, up_ref, ue_ref, first_ref, dq_ref, dk_ref, dv_ref):
        p, i = pl.program_id(0), pl.program_id(1)

        @pl.when(i == 0)
        def _():
            dk_ref[...] = jnp.zeros(dk_ref.shape, F32)
            dv_ref[...] = jnp.zeros(dv_ref.shape, F32)

        row, lane = _sb_iotas()
        first = lane < SB_DH
        do = do_ref[...]
        tl_all = tl_ref[...]
        heads = []
        for (g, qh), at in zip(_sb_heads(q_ref[...], first), (0, B - 1) * SB_GROUP):
            dog = _lanes(do, g)
            keep = first if at == 0 else jnp.logical_not(first)
            heads.append((g, qh, jnp.where(keep, dog, jnp.zeros_like(dog)).astype(BF16),
                          _lanes(tl_all, g)[:, at:at + 1]))
        up = up_ref[...]
        ue = ue_ref[...]
        start = first_ref[p, i]

        def step(n, state):
            blocks = []
            for sub in range(SB_UNROLL):
                j = start + SB_UNROLL * n + sub
                off = pl.multiple_of(jnp.minimum(j, i) * B, B)
                valid = (lane + j * B) < (row + i * B)
                blocks.append((off, k_ref[pl.ds(off, B), :], v_ref[pl.ds(off, B), :], valid))
            combos = [(s, h) for s in range(SB_UNROLL) for h in range(len(heads))]
            z = {(s, h): _dot_nt(heads[h][1], _lanes(blocks[s][1], heads[h][0])) for s, h in combos}
            da = {(s, h): _dot_nt(heads[h][2], _lanes(blocks[s][2], heads[h][0])) for s, h in combos}
            lnb = {c: jnp.where(blocks[c[0]][3], _neg_softplus(z[c]), 0.0) for c in combos}
            lb = {c: z[c] + lnb[c] for c in combos}
            sums = {c: _sum01_right(lnb[c], up) for c in combos}
            a, w = {}, {}
            seen = [st[0] for st in state]
            for s, h in combos:
                expo = lb[s, h] + (heads[h][3] - seen[h] - sums[s, h][:, :B])
                a[s, h] = jnp.where(blocks[s][3], jnp.exp(expo), 0.0)
                w[s, h] = a[s, h] * da[s, h]
                seen[h] = seen[h] + sums[s, h][:, B:]
            wsums = {c: _sum01_right(w[c], ue) for c in combos}
            dz = {}
            seen_w = [st[1] for st in state]
            for s, h in combos:
                beta = jnp.exp(lb[s, h])
                before = seen_w[h] + wsums[s, h][:, :B]
                dz[s, h] = jnp.where(blocks[s][3], w[s, h] * (1.0 - beta) - before * beta, 0.0)
                seen_w[h] = seen_w[h] + wsums[s, h][:, B:]
            dq = [st[2] for st in state]
            for s, h in combos:
                dq[h] = dq[h] + _dot(dz[s, h], _lanes(blocks[s][1], heads[h][0]))
            for s in range(SB_UNROLL):
                off = blocks[s][0]
                for g in range(SB_GROUP):
                    h0, h1 = 2 * g, 2 * g + 1
                    dk_ref[pl.ds(off, B), B * g:B * (g + 1)] += (_dot_tn(dz[s, h0], heads[h0][1])
                                                                 + _dot_tn(dz[s, h1], heads[h1][1]))
                    dv_ref[pl.ds(off, B), B * g:B * (g + 1)] += (_dot_tn(a[s, h0], heads[h0][2])
                                                                 + _dot_tn(a[s, h1], heads[h1][2]))
            return tuple(zip(seen, seen_w, dq))

        zero = jnp.zeros((B, B), F32)
        trips = (i - start + SB_UNROLL) // SB_UNROLL
        state = lax.fori_loop(0, trips, step, tuple((zero, zero, zero) for _ in heads))
        for g in range(SB_GROUP):
            dq_ref[:, B * g:B * (g + 1)] = jnp.where(first, state[2 * g][2], state[2 * g + 1][2]) * SB_SCALE

    qblk = pl.BlockSpec((B, W), lambda p, i: (i, p))
    full = pl.BlockSpec((T, W), lambda p, i: (0, p))
    return pl.pallas_call(
        body, name="sb_bwd", grid=(groups, T // B),
        in_specs=[qblk, pl.BlockSpec((T, W), lambda p, i: (0, groups + p)),
                  pl.BlockSpec((T, W), lambda p, i: (0, 2 * groups + p)), qblk, qblk,
                  pl.BlockSpec(upre.shape, lambda p, i: (0, 0)), pl.BlockSpec(uexc.shape, lambda p, i: (0, 0)),
                  pl.BlockSpec(memory_space=pltpu.SMEM)],
        out_specs=[qblk, full, full],
        out_shape=[jax.ShapeDtypeStruct((T, SB_WIDTH), F32)] * 3,
        compiler_params=_params(("arbitrary", "arbitrary")),
    )(sqkv, sqkv, sqkv, do, tl, upre, uexc, first_block)


def _mixer_out(o_hg, hg, o_sb, g_hg, g_sb):
    n_hg, r_hg = _rms(o_hg)
    s_hg = _sigmoid(hg)
    n_sb, r_sb = _rms(o_sb)
    return dict(n_hg=n_hg, r_hg=r_hg, s_hg=s_hg, n_sb=n_sb, r_sb=r_sb,
                y_hg=n_hg * g_hg * (hg * s_hg), y_sb=n_sb * g_sb)


def mix_out_fwd(o_hg, proj_h, o_sb, x, norms, g_post, w_out, comm=None):
    T = x.shape[0]

    def body(ohg_ref, hg_ref, osb_ref, x_ref, nrm_ref, gp_ref, w_hbm, cat_ref, mix_ref, h1_ref, w_vmem):
        _load_once(w_hbm, w_vmem)
        nrm = nrm_ref[...]
        m = _mixer_out(ohg_ref[...], hg_ref[...], osb_ref[...], nrm[:, :HG_WIDTH], nrm[:, HG_WIDTH:])
        cat_ref[:, :HG_WIDTH] = m["y_hg"].astype(BF16)
        cat_ref[:, HG_WIDTH:] = m["y_sb"].astype(BF16)
        mix = jnp.dot(cat_ref[...], w_vmem[...], preferred_element_type=F32)
        mix_ref[...] = mix
        mh, _ = _rms(mix)
        h1_ref[...] = x_ref[...] + mh * gp_ref[...]

    return _pallas(
        body, comm=comm, edge=_grid_edge(T // ROW_TILE), name="mix_out_fwd", grid=(T // ROW_TILE,),
        in_specs=[_row_spec(HG_WIDTH), _row_spec(HG_WIDTH, 3), _row_spec(SB_WIDTH), _row_spec(D_MODEL),
                  _full_spec((1, D_MODEL)), _full_spec((1, D_MODEL)), ANY_SPEC],
        out_specs=[_row_spec(D_MODEL)] * 3,
        out_shape=[jax.ShapeDtypeStruct((T, D_MODEL), BF16), jax.ShapeDtypeStruct((T, D_MODEL), F32),
                   jax.ShapeDtypeStruct((T, D_MODEL), F32)],
        scratch_shapes=[pltpu.VMEM(w_out.shape, BF16)],
        compiler_params=_params(("arbitrary",)),
    )(o_hg, proj_h, o_sb, x, norms, g_post, w_out)


def ffn_fwd(h1, g_pre, g_post, w_gu, w_down):
    T = h1.shape[0]
    pw = w_gu.shape[2]

    def body(h1_ref, gpre_ref, gpost_ref, wgu_hbm, wd_hbm, u2_ref, gu_ref, act_ref, y_ref, h2_ref,
             wgu_vmem, wd_vmem, gu_s):
        _load_once(wgu_hbm, wgu_vmem)
        _load_once(wd_hbm, wd_vmem)
        h1v = h1_ref[...]
        hh, _ = _rms(h1v)
        u2 = (hh * gpre_ref[...]).astype(BF16)
        u2_ref[...] = u2
        for q in range(N_CHIPS):
            gu_s[:, pw * q:pw * (q + 1)] = jnp.dot(u2, wgu_vmem[q], preferred_element_type=F32)
        gu_ref[...] = gu_s[...].astype(BF16)
        gate = gu_s[:, :D_FF]
        act = (gate * _sigmoid(gate) * gu_s[:, D_FF:]).astype(BF16)
        act_ref[...] = act
        y = jnp.dot(act, wd_vmem[...], preferred_element_type=F32)
        y_ref[...] = y
        yh, _ = _rms(y)
        h2_ref[...] = h1v + yh * gpost_ref[...]

    return pl.pallas_call(
        body, name="ffn_fwd", grid=(T // ROW_TILE,),
        in_specs=[_row_spec(D_MODEL), _full_spec((1, D_MODEL)), _full_spec((1, D_MODEL)), ANY_SPEC, ANY_SPEC],
        out_specs=[_row_spec(D_MODEL), _row_spec(2 * D_FF), _row_spec(D_FF), _row_spec(D_MODEL), _row_spec(D_MODEL)],
        out_shape=[jax.ShapeDtypeStruct((T, D_MODEL), BF16), jax.ShapeDtypeStruct((T, 2 * D_FF), BF16),
                   jax.ShapeDtypeStruct((T, D_FF), BF16), jax.ShapeDtypeStruct((T, D_MODEL), F32),
                   jax.ShapeDtypeStruct((T, D_MODEL), F32)],
        scratch_shapes=[pltpu.VMEM(w_gu.shape, BF16), pltpu.VMEM(w_down.shape, BF16),
                        pltpu.VMEM((ROW_TILE, 2 * D_FF), F32)],
        compiler_params=_params(("arbitrary",)),
    )(h1, g_pre, g_post, w_gu, w_down)


def ple_loss(h2, p, target, w_ple, w_pg):
    T = h2.shape[0]
    pw = w_ple.shape[2]

    def body(h2_ref, p_ref, t_ref, wple_hbm, wpg_hbm, de_ref, ds_ref, dh2_ref, h2b_ref, pb_ref, pack_ref,
             wple_vmem, wpg_vmem, e_s):
        _load_once(wple_hbm, wple_vmem)
        _load_once(wpg_hbm, wpg_vmem)
        _zero_first(pack_ref)
        h2v = h2_ref[...]
        h2b = h2v.astype(BF16)
        h2b_ref[...] = h2b
        pb = p_ref[...].astype(BF16)
        pb_ref[...] = pb
        for q in range(N_CHIPS):
            e_s[:, pw * q:pw * (q + 1)] = jnp.dot(pb, wple_vmem[q], preferred_element_type=F32)
        e = e_s[...]
        sig = _sigmoid(jnp.dot(h2b, wpg_vmem[...], preferred_element_type=F32))
        err = h2v + e * sig - t_ref[...]
        part = 0.5 * jnp.sum(jnp.mean(err * err, axis=-1, keepdims=True), axis=0, keepdims=True)
        lane = lax.broadcasted_iota(jnp.int32, (1, D_MODEL), 1)
        pack_ref[ROW_LOSS:ROW_LOSS + 1, :] += jnp.where(lane == 0, part, 0.0)
        dh3 = err * (1.0 / D_MODEL)
        de_ref[...] = (dh3 * sig).astype(BF16)
        ds = (dh3 * e * sig * (1.0 - sig)).astype(BF16)
        ds_ref[...] = ds
        dh2_ref[...] = dh3 + _dot_nt(ds, wpg_vmem[...])

    return pl.pallas_call(
        body, name="ple_loss", grid=(T // ROW_TILE,),
        in_specs=[_row_spec(D_MODEL), _row_spec(p.shape[1]), _row_spec(D_MODEL), ANY_SPEC, ANY_SPEC],
        out_specs=[_row_spec(D_MODEL), _row_spec(D_MODEL), _row_spec(D_MODEL), _row_spec(D_MODEL),
                   _row_spec(p.shape[1]), PACK_SPEC],
        out_shape=[jax.ShapeDtypeStruct((T, D_MODEL), BF16), jax.ShapeDtypeStruct((T, D_MODEL), BF16),
                   jax.ShapeDtypeStruct((T, D_MODEL), F32), jax.ShapeDtypeStruct((T, D_MODEL), BF16),
                   jax.ShapeDtypeStruct(p.shape, BF16), jax.ShapeDtypeStruct((8, D_MODEL), F32)],
        scratch_shapes=[pltpu.VMEM(w_ple.shape, BF16), pltpu.VMEM(w_pg.shape, BF16), pltpu.VMEM((ROW_TILE, D_MODEL), F32)],
        compiler_params=_params(("arbitrary",)),
    )(h2, p, target, w_ple, w_pg)


def ffn_bwd(dh2, y, h1, gu, g_pre, g_post, w_gu, w_down):
    T = h1.shape[0]
    pw = w_gu.shape[2]

    def body(dh2_ref, y_ref, h1_ref, gu_ref, gpre_ref, gpost_ref, wgu_hbm, wd_hbm, dy_ref, dgu_ref, dh1_ref, pack_ref,
             wgu_vmem, wd_vmem):
        _load_once(wgu_hbm, wgu_vmem)
        _load_once(wd_hbm, wd_vmem)
        _zero_first(pack_ref)
        dh2v = dh2_ref[...]
        yh, ry = _rms(y_ref[...])
        dy, dw = _rms_bwd(dh2v, yh, ry, gpost_ref[...])
        pack_ref[ROW_FFN_POST:ROW_FFN_POST + 1, :] += _colsum(dw)
        dyb = dy.astype(BF16)
        dy_ref[...] = dyb
        dact = _dot_nt(dyb, wd_vmem[...])
        gate = gu_ref[:, :D_FF].astype(F32)
        up = gu_ref[:, D_FF:].astype(F32)
        sg = _sigmoid(gate)
        dgu_ref[:, :D_FF] = (dact * up * (sg * (1.0 + gate * (1.0 - sg)))).astype(BF16)
        dgu_ref[:, D_FF:] = (dact * gate * sg).astype(BF16)
        du2 = _dot_nt(dgu_ref[:, :pw], wgu_vmem[0])
        for q in range(1, N_CHIPS):
            du2 = du2 + _dot_nt(dgu_ref[:, pw * q:pw * (q + 1)], wgu_vmem[q])
        hh, rh = _rms(h1_ref[...])
        dh, dw = _rms_bwd(du2, hh, rh, gpre_ref[...])
        pack_ref[ROW_FFN_PRE:ROW_FFN_PRE + 1, :] += _colsum(dw)
        dh1_ref[...] = dh2v + dh

    return pl.pallas_call(
        body, name="ffn_bwd", grid=(T // ROW_TILE,),
        in_specs=[_row_spec(D_MODEL), _row_spec(D_MODEL), _row_spec(D_MODEL), _row_spec(2 * D_FF),
                  _full_spec((1, D_MODEL)), _full_spec((1, D_MODEL)), ANY_SPEC, ANY_SPEC],
        out_specs=[_row_spec(D_MODEL), _row_spec(2 * D_FF), _row_spec(D_MODEL), PACK_SPEC],
        out_shape=[jax.ShapeDtypeStruct((T, D_MODEL), BF16), jax.ShapeDtypeStruct((T, 2 * D_FF), BF16),
                   jax.ShapeDtypeStruct((T, D_MODEL), F32), jax.ShapeDtypeStruct((8, D_MODEL), F32)],
        scratch_shapes=[pltpu.VMEM(w_gu.shape, BF16), pltpu.VMEM(w_down.shape, BF16)],
        compiler_params=_params(("arbitrary",)),
    )(dh2, y, h1, gu, g_pre, g_post, w_gu, w_down)


def mix_out_bwd(dh1, mix, o_hg, proj_h, o_sb, norms, g_post, w_out, comm=None):
    T = dh1.shape[0]

    def body(dh1_ref, mix_ref, ohg_ref, hg_ref, osb_ref, nrm_ref, gp_ref, w_hbm, dmix_ref, dohg_ref, dhg_ref, dosb_ref,
             pack_ref, w_vmem):
        _load_once(w_hbm, w_vmem)
        _zero_first(pack_ref)
        mh, rm = _rms(mix_ref[...])
        dmix, dw = _rms_bwd(dh1_ref[...], mh, rm, gp_ref[...])
        pack_ref[ROW_ATTN_POST:ROW_ATTN_POST + 1, :] += _colsum(dw)
        dmb = dmix.astype(BF16)
        dmix_ref[...] = dmb
        dcat = _dot_nt(dmb, w_vmem[...])
        nrm = nrm_ref[...]
        g_hg, g_sb = nrm[:, :HG_WIDTH], nrm[:, HG_WIDTH:]
        hg = hg_ref[...]
        m = _mixer_out(ohg_ref[...], hg, osb_ref[...], g_hg, g_sb)
        d_hg = dcat[:, :HG_WIDTH]
        silu = hg * m["s_hg"]
        dhg_ref[...] = d_hg * (m["n_hg"] * g_hg) * (m["s_hg"] * (1.0 + hg * (1.0 - m["s_hg"])))
        dx, dw = _rms_bwd(d_hg * silu, m["n_hg"], m["r_hg"], g_hg)
        dohg_ref[...] = dx
        pack_ref[ROW_MIX_NORMS:ROW_MIX_NORMS + 1, :HG_WIDTH] += _colsum(dw)
        dx, dw = _rms_bwd(dcat[:, HG_WIDTH:], m["n_sb"], m["r_sb"], g_sb)
        dosb_ref[...] = dx
        pack_ref[ROW_MIX_NORMS:ROW_MIX_NORMS + 1, HG_WIDTH:] += _colsum(dw)

    return _pallas(
        body, comm=comm, edge=_grid_edge(T // ROW_TILE), name="mix_out_bwd", grid=(T // ROW_TILE,),
        in_specs=[_row_spec(D_MODEL), _row_spec(D_MODEL), _row_spec(HG_WIDTH), _row_spec(HG_WIDTH, 3), _row_spec(SB_WIDTH),
                  _full_spec((1, D_MODEL)), _full_spec((1, D_MODEL)), ANY_SPEC],
        out_specs=[_row_spec(D_MODEL), _row_spec(HG_WIDTH), _row_spec(HG_WIDTH), _row_spec(SB_WIDTH), PACK_SPEC],
        out_shape=[jax.ShapeDtypeStruct((T, D_MODEL), BF16), jax.ShapeDtypeStruct((T, HG_WIDTH), F32),
                   jax.ShapeDtypeStruct((T, HG_WIDTH), F32), jax.ShapeDtypeStruct((T, SB_WIDTH), F32),
                   jax.ShapeDtypeStruct((8, D_MODEL), F32)],
        scratch_shapes=[pltpu.VMEM(w_out.shape, BF16)],
        compiler_params=_params(("arbitrary",)),
    )(dh1, mix, o_hg, proj_h, o_sb, norms, g_post, w_out)


def in_proj_bwd(parts, x, dh1, g_pre, w_in):
    T = x.shape[0]
    pw = w_in.shape[2]
    n_parts = len(parts)

    def body(*refs):
        part_refs = refs[:n_parts]
        x_ref, dh1_ref, g_ref, w_hbm, dproj_ref, dx_ref, pack_ref, w_vmem = refs[n_parts:]
        _load_once(w_hbm, w_vmem)
        _zero_first(pack_ref)
        for n, ref in enumerate(part_refs):
            dproj_ref[:, HG_WIDTH * n:HG_WIDTH * (n + 1)] = ref[...].astype(BF16)
        du = _dot_nt(dproj_ref[:, :pw], w_vmem[0])
        for q in range(1, N_CHIPS):
            du = du + _dot_nt(dproj_ref[:, pw * q:pw * (q + 1)], w_vmem[q])
        xh, r = _rms(x_ref[...])
        dx, dw = _rms_bwd(du, xh, r, g_ref[...])
        pack_ref[ROW_ATTN_PRE:ROW_ATTN_PRE + 1, :] += _colsum(dw)
        dx_ref[...] = dh1_ref[...] + dx

    return pl.pallas_call(
        body, name="in_proj_bwd", grid=(T // ROW_TILE,),
        in_specs=[_row_spec(HG_WIDTH)] * n_parts + [_row_spec(D_MODEL), _row_spec(D_MODEL), _full_spec((1, D_MODEL)), ANY_SPEC],
        out_specs=[_row_spec(n_parts * HG_WIDTH), _row_spec(D_MODEL), PACK_SPEC],
        out_shape=[jax.ShapeDtypeStruct((T, n_parts * HG_WIDTH), BF16), jax.ShapeDtypeStruct((T, D_MODEL), F32),
                   jax.ShapeDtypeStruct((8, D_MODEL), F32)],
        scratch_shapes=[pltpu.VMEM(w_in.shape, BF16)],
        compiler_params=_params(("arbitrary",)),
    )(*parts, x, dh1, g_pre, w_in)


def weight_grad(a, g, name, *, tm, tn, tk=512, col_pieces=False):
    T, M = a.shape
    N = g.shape[1]
    tk = min(tk, T)
    steps = T // tk

    def body(a_ref, g_ref, o_ref):
        @pl.when(pl.program_id(2) == 0)
        def _():
            o_ref[...] = jnp.zeros(o_ref.shape, F32)

        o_ref[...] += _dot_tn(a_ref[...], g_ref[...]).reshape(o_ref.shape)

    if col_pieces:
        out_shape = jax.ShapeDtypeStruct((N // tn, M, tn), F32)
        out_spec = pl.BlockSpec((1, tm, tn), lambda i, j, k: (j, i, 0))
    else:
        out_shape = jax.ShapeDtypeStruct((M, N), F32)
        out_spec = pl.BlockSpec((tm, tn), lambda i, j, k: (i, j))
    return pl.pallas_call(
        body, name=name, grid=(M // tm, N // tn, steps),
        in_specs=[pl.BlockSpec((tk, tm), lambda i, j, k: (k, i)), pl.BlockSpec((tk, tn), lambda i, j, k: (k, j))],
        out_specs=out_spec, out_shape=out_shape,
        compiler_params=_params(("arbitrary", "arbitrary", "arbitrary")),
    )(a, g)


def _place():
    x, y, c = lax.axis_index("x"), lax.axis_index("y"), lax.axis_index("c")
    chips = [(1 - x, y), (x, 1 - y), (1 - x, 1 - y)]
    return x, y, c, chips


def _chip_index(cx, cy):
    return 2 * cx + cy


def _rcopy(src, dst, send_sem, recv_sem, device):
    return pltpu.make_async_remote_copy(src_ref=src, dst_ref=dst, send_sem=send_sem, recv_sem=recv_sem,
                                        device_id=device, device_id_type=MESH)


def gather_weights(shards):
    n = len(shards)

    def body(*refs):
        ins, outs = refs[:n], refs[2 * n:3 * n]
        send_sems, recv_sems = refs[3 * n:]
        x, y, c, chips = _place()
        me = _chip_index(x, y)
        sibling = (x, y, 1 - c)

        def rows(w, core):
            half = ins[w].shape[0] // 2
            return pl.ds(core * half, half)

        sends = []
        for w in range(n):
            for j, chip in enumerate(chips):
                sends.append(_rcopy(ins[w].at[rows(w, c)], outs[w].at[me, rows(w, c)],
                                    send_sems.at[6 * w + j], recv_sems.at[6 * w + j], (*chip, c)))
        for cp in sends:
            cp.start()
        passed = []
        for w in range(n):
            for j, chip in enumerate(chips):
                block = outs[w].at[_chip_index(*chip), rows(w, c)]
                _rcopy(block, block, send_sems.at[6 * w + j], recv_sems.at[6 * w + j], (*chip, c)).wait_recv()
                cp = _rcopy(block, block, send_sems.at[6 * w + 3 + j], recv_sems.at[6 * w + 3 + j], sibling)
                cp.start()
                passed.append(cp)
        for w in range(n):
            for j, chip in enumerate(chips):
                block = outs[w].at[_chip_index(*chip), rows(w, 1 - c)]
                _rcopy(block, block, send_sems.at[6 * w + 3 + j], recv_sems.at[6 * w + 3 + j], sibling).wait_recv()
        for cp in sends + passed:
            cp.wait_send()

    filled = [jnp.broadcast_to(s[None], (N_CHIPS,) + s.shape) for s in shards]
    return pl.pallas_call(
        body, name="gather_weights",
        in_specs=[ANY_SPEC] * (2 * n), out_specs=[ANY_SPEC] * n,
        out_shape=[jax.ShapeDtypeStruct(f.shape, f.dtype) for f in filled],
        input_output_aliases={n + w: w for w in range(n)},
        scratch_shapes=[pltpu.SemaphoreType.DMA((6 * n,)), pltpu.SemaphoreType.DMA((6 * n,))],
    )(*shards, *filled)


def _run_comm(comm, name):
    c_in, c_out = len(comm.inputs), len(comm.out_shape)

    def body(*refs):
        parts = refs[:c_in], refs[c_in:c_in + c_out], refs[c_in + c_out:]
        comm.start(*parts)
        comm.finish(*parts)

    return pl.pallas_call(
        body, name=name, in_specs=[ANY_SPEC] * c_in, out_specs=[ANY_SPEC] * c_out, out_shape=comm.out_shape,
        scratch_shapes=comm.scratch, input_output_aliases=comm.aliases)(*comm.inputs)


def _both(first, second):
    n_in, n_out, n_scr = len(first.inputs), len(first.out_shape), len(first.scratch)

    def split(ins, outs, scr):
        return (ins[:n_in], outs[:n_out], scr[:n_scr]), (ins[n_in:], outs[n_out:], scr[n_scr:])

    def start(*refs):
        a, b = split(*refs)
        first.start(*a)
        second.start(*b)

    def finish(*refs):
        a, b = split(*refs)
        first.finish(*a)
        second.finish(*b)

    aliases = dict(first.aliases)
    aliases.update({n_in + i: n_out + o for i, o in second.aliases.items()})
    return Comm(first.inputs + second.inputs, first.out_shape + second.out_shape, aliases,
                first.scratch + second.scratch, start, finish)


def _dma_sems(count):
    return [pltpu.SemaphoreType.DMA((count,)), pltpu.SemaphoreType.DMA((count,))]


def gather_over_ici(shards):
    n = len(shards)

    def copies(ins, outs, sems):
        send_sems, recv_sems = sems
        x, y, c, chips = _place()
        me = _chip_index(x, y)
        pairs = []
        for w in range(n):
            half = shards[w].shape[0] // 2
            rows = pl.ds(c * half, half)
            for j, chip in enumerate(chips):
                k = 3 * w + j
                landed = outs[w].at[_chip_index(*chip), rows]
                pairs.append((_rcopy(ins[w].at[rows], outs[w].at[me, rows], send_sems.at[k], recv_sems.at[k], (*chip, c)),
                              _rcopy(landed, landed, send_sems.at[k], recv_sems.at[k], (*chip, c))))
        return pairs

    def start(*refs):
        for send, _ in copies(*refs):
            send.start()

    def finish(*refs):
        pairs = copies(*refs)
        for _, landed in pairs:
            landed.wait_recv()
        for send, _ in pairs:
            send.wait_send()

    filled = [jnp.broadcast_to(s[None], (N_CHIPS,) + s.shape) for s in shards]
    return Comm(list(shards) + filled, [jax.ShapeDtypeStruct(f.shape, f.dtype) for f in filled],
                {n + w: w for w in range(n)}, _dma_sems(3 * n), start, finish)


def gather_over_d2d(landed):
    n = len(landed)

    def copies(ins, outs, sems):
        send_sems, recv_sems = sems
        x, y, c, chips = _place()
        sibling = (x, y, 1 - c)
        pairs = []
        for w in range(n):
            half = landed[w].shape[1] // 2
            for j, chip in enumerate(chips):
                k = 3 * w + j
                mine = outs[w].at[_chip_index(*chip), pl.ds(c * half, half)]
                theirs = outs[w].at[_chip_index(*chip), pl.ds((1 - c) * half, half)]
                pairs.append((_rcopy(mine, mine, send_sems.at[k], recv_sems.at[k], sibling),
                              _rcopy(theirs, theirs, send_sems.at[k], recv_sems.at[k], sibling)))
        return pairs

    def start(*refs):
        for send, _ in copies(*refs):
            send.start()

    def finish(*refs):
        pairs = copies(*refs)
        for _, arrived in pairs:
            arrived.wait_recv()
        for send, _ in pairs:
            send.wait_send()

    return Comm(list(landed), [jax.ShapeDtypeStruct(a.shape, a.dtype) for a in landed], {w: w for w in range(n)},
                _dma_sems(3 * n), start, finish)


def core_halves(grads):
    n = len(grads)

    def copies(ins, outs, sems):
        send_sems, recv_sems = sems
        x, y, c, _ = _place()
        out = []
        for w in range(n):
            half = grads[w].shape[1] // 2
            out.append(_rcopy(ins[w].at[:, pl.ds((1 - c) * half, half), :], outs[w],
                              send_sems.at[w], recv_sems.at[w], (x, y, 1 - c)))
        return out

    def start(*refs):
        for cp in copies(*refs):
            cp.start()

    def finish(*refs):
        for cp in copies(*refs):
            cp.wait()

    return Comm(list(grads), [jax.ShapeDtypeStruct((g.shape[0], g.shape[1] // 2, g.shape[2]), g.dtype) for g in grads],
                {}, _dma_sems(n), start, finish)


def chip_partials(partials):
    n = len(partials)

    def copies(ins, outs, sems):
        send_sems, recv_sems = sems
        x, y, c, chips = _place()
        me = _chip_index(x, y)
        pairs = []
        for w in range(n):
            for j, chip in enumerate(chips):
                k = 3 * w + j
                landed = outs[w].at[_chip_index(*chip)]
                pairs.append((_rcopy(ins[w].at[_chip_index(*chip)], outs[w].at[me], send_sems.at[k], recv_sems.at[k],
                                     (*chip, c)),
                              _rcopy(landed, landed, send_sems.at[k], recv_sems.at[k], (*chip, c))))
        return pairs

    def start(*refs):
        for send, _ in copies(*refs):
            send.start()

    def finish(*refs):
        pairs = copies(*refs)
        for _, landed in pairs:
            landed.wait_recv()
        for send, _ in pairs:
            send.wait_send()

    me = _chip_index(lax.axis_index("x"), lax.axis_index("y"))
    filled = [jnp.broadcast_to(lax.dynamic_index_in_dim(p, me, 0, keepdims=True), p.shape) for p in partials]
    return Comm(list(partials) + filled, [jax.ShapeDtypeStruct(p.shape, p.dtype) for p in partials],
                {n + w: w for w in range(n)}, _dma_sems(3 * n), start, finish)


def join_core_halves(grads):
    n = len(grads)

    def body(*refs):
        outs = refs[n:2 * n]
        send_sems, recv_sems = refs[2 * n:]
        x, y, c, _ = _place()
        sibling = (x, y, 1 - c)
        copies = []
        for w in range(n):
            half = outs[w].shape[0] // 2
            mine = outs[w].at[pl.ds(c * half, half), :]
            copies.append(_rcopy(mine, mine, send_sems.at[w], recv_sems.at[w], sibling))
        for cp in copies:
            cp.start()
        for w in range(n):
            half = outs[w].shape[0] // 2
            theirs = outs[w].at[pl.ds((1 - c) * half, half), :]
            _rcopy(theirs, theirs, send_sems.at[w], recv_sems.at[w], sibling).wait_recv()
        for cp in copies:
            cp.wait_send()

    return pl.pallas_call(
        body, name="join_core_halves",
        in_specs=[ANY_SPEC] * n, out_specs=[ANY_SPEC] * n,
        out_shape=[jax.ShapeDtypeStruct(g.shape, g.dtype) for g in grads],
        input_output_aliases={w: w for w in range(n)},
        scratch_shapes=[pltpu.SemaphoreType.DMA((n,)), pltpu.SemaphoreType.DMA((n,))],
    )(*grads)


def _elementwise_rows(rows, cap=512):
    for t in range(min(rows, cap), 0, -8):
        if rows % t == 0 and t % 16 == 0:
            return t
    return rows


def add_core_halves(grad, got, core, name):
    _, rows, cols = got.shape
    tr = _elementwise_rows(rows)
    nt = rows // tr

    def body(core_ref, a_ref, b_ref, o_ref):
        o_ref[...] = (a_ref[...] + b_ref[...]).astype(BF16)

    spec = pl.BlockSpec((1, tr, cols), lambda q, i, core_ref: (q, i, 0))
    own = pl.BlockSpec((1, tr, cols), lambda q, i, core_ref: (q, core_ref[0] * nt + i, 0))
    return pl.pallas_call(
        body, name=name,
        grid_spec=pltpu.PrefetchScalarGridSpec(num_scalar_prefetch=1, grid=(N_CHIPS, nt), in_specs=[own, spec],
                                               out_specs=spec),
        out_shape=jax.ShapeDtypeStruct(got.shape, BF16),
        compiler_params=_params(("arbitrary", "arbitrary")),
    )(core, grad, got)


def add_chip_partials(parts, core, name):
    _, rows, cols = parts.shape
    tr = _elementwise_rows(rows)
    nt = rows // tr

    def body(core_ref, p_ref, o_ref):
        acc = p_ref[0].astype(F32)
        for q in range(1, N_CHIPS):
            acc = acc + p_ref[q].astype(F32)
        o_ref[...] = acc

    return pl.pallas_call(
        body, name=name,
        grid_spec=pltpu.PrefetchScalarGridSpec(
            num_scalar_prefetch=1, grid=(nt,),
            in_specs=[pl.BlockSpec((N_CHIPS, tr, cols), lambda i, core_ref: (0, i, 0))],
            out_specs=pl.BlockSpec((tr, cols), lambda i, core_ref: (core_ref[0] * nt + i, 0))),
        out_shape=jax.ShapeDtypeStruct((2 * rows, cols), F32),
        compiler_params=_params(("arbitrary",)),
    )(core, parts)


def _adamw_math(w, g, m, v):
    m = ADAM_B1 * m + (1.0 - ADAM_B1) * g
    v = ADAM_B2 * v + (1.0 - ADAM_B2) * (g * g)
    m_hat = m / (1.0 - ADAM_B1 ** ADAM_STEP)
    v_hat = v / (1.0 - ADAM_B2 ** ADAM_STEP)
    delta = -ADAM_LR * (m_hat / (jnp.sqrt(v_hat) + ADAM_EPS) + ADAM_WD * w)
    return delta, m, v


def adamw(w, g, m, v, name):
    rows, cols = w.shape
    tr = _elementwise_rows(rows, 256)

    def body(w_ref, g_ref, m_ref, v_ref, d_ref, nm_ref, nv_ref):
        d, nm, nv = _adamw_math(w_ref[...], g_ref[...], m_ref[...], v_ref[...])
        d_ref[...] = d
        nm_ref[...] = nm
        nv_ref[...] = nv

    spec = pl.BlockSpec((tr, cols), lambda i: (i, 0))
    return pl.pallas_call(
        body, name=name, grid=(rows // tr,), in_specs=[spec] * 4, out_specs=[spec] * 3,
        out_shape=[jax.ShapeDtypeStruct((rows, cols), F32)] * 3,
        compiler_params=_params(("arbitrary",)),
    )(w, g, m, v)


def reduce_small(packs, w, m, v):
    n = len(packs)
    n_dev = 8
    flips = [(fx, fy, fc) for fx in (0, 1) for fy in (0, 1) for fc in (0, 1)][1:]

    def body(*refs):
        pack_refs = refs[:n]
        w_ref, m_ref, v_ref, g_out, d_out, m_out, v_out, mine, slots, send_sems, recv_sems = refs[n:]
        x, y, c, _ = _place()
        me = 4 * x + 2 * y + c
        acc = pack_refs[0][...]
        for ref in pack_refs[1:]:
            acc = acc + ref[...]
        mine[...] = acc
        sends = []
        for k, (fx, fy, fc) in enumerate(flips):
            peer = (x ^ fx, y ^ fy, c ^ fc)
            sends.append(_rcopy(mine, slots.at[me], send_sems.at[k], recv_sems.at[me], peer))
        for cp in sends:
            cp.start()
        slots[me] = acc
        for fx, fy, fc in flips:
            src = 4 * (x ^ fx) + 2 * (y ^ fy) + (c ^ fc)
            _rcopy(mine, slots.at[src], send_sems.at[0], recv_sems.at[src], (x, y, c)).wait_recv()
        for cp in sends:
            cp.wait_send()
        total = slots[0]
        for d in range(1, n_dev):
            total = total + slots[d]
        g_out[...] = total
        d, nm, nv = _adamw_math(w_ref[...], total, m_ref[...], v_ref[...])
        d_out[...] = d
        m_out[...] = nm
        v_out[...] = nv

    vm = pl.BlockSpec(memory_space=pltpu.VMEM)
    return pl.pallas_call(
        body, name="reduce_small",
        in_specs=[vm] * (n + 3), out_specs=[vm] * 4,
        out_shape=[jax.ShapeDtypeStruct((8, D_MODEL), F32)] * 4,
        scratch_shapes=[pltpu.VMEM((8, D_MODEL), F32), pltpu.VMEM((n_dev, 8, D_MODEL), F32),
                        pltpu.SemaphoreType.DMA((len(flips),)), pltpu.SemaphoreType.DMA((n_dev,))],
    )(*packs, w, m, v)


def _pack_small(attn_pre, gamma, hg_norm, sb_norm, attn_post, ffn_pre, ffn_post):
    rows = [attn_pre, gamma.reshape(1, D_MODEL), jnp.concatenate([hg_norm, sb_norm], axis=1), attn_post, ffn_pre, ffn_post,
            jnp.zeros((2, D_MODEL), F32)]
    return jnp.concatenate(rows, axis=0)


def _unpack_small(pack):
    return (pack[ROW_ATTN_PRE:ROW_ATTN_PRE + 1], pack[ROW_GAMMA].reshape(2, HG_WIDTH),
            pack[ROW_MIX_NORMS:ROW_MIX_NORMS + 1, :HG_WIDTH], pack[ROW_MIX_NORMS:ROW_MIX_NORMS + 1, HG_WIDTH:],
            pack[ROW_ATTN_POST:ROW_ATTN_POST + 1], pack[ROW_FFN_PRE:ROW_FFN_PRE + 1], pack[ROW_FFN_POST:ROW_FFN_POST + 1])


def kernel(x, p, attn_pre_norm, w_in, hg_lower_gamma, hg_out_norm, sb_out_norm, w_out, attn_post_norm, ffn_pre_norm, w_gate_up, w_down, ffn_post_norm, ple_proj, ple_gate, loss_target, m_attn_pre_norm, m_w_in, m_hg_lower_gamma, m_hg_out_norm, m_sb_out_norm, m_w_out, m_attn_post_norm, m_ffn_pre_norm, m_w_gate_up, m_w_down, m_ffn_post_norm, m_ple_proj, m_ple_gate, v_attn_pre_norm, v_w_in, v_hg_lower_gamma, v_hg_out_norm, v_sb_out_norm, v_w_out, v_attn_post_norm, v_ffn_pre_norm, v_w_gate_up, v_w_down, v_ffn_post_norm, v_ple_proj, v_ple_gate):
    x2 = x[0]
    p2 = p[0, 0]
    target = loss_target[0]
    big = dict(w_in=(w_in, m_w_in, v_w_in), w_out=(w_out, m_w_out, v_w_out), w_gate_up=(w_gate_up, m_w_gate_up, v_w_gate_up),
               w_down=(w_down, m_w_down, v_w_down), ple_proj=(ple_proj, m_ple_proj, v_ple_proj),
               ple_gate=(ple_gate, m_ple_gate, v_ple_gate))
    names = list(big)
    big = {k: tuple(a[0] for a in t) for k, t in big.items()}

    shard16 = {k: big[k][0].astype(BF16) for k in names}
    w_in_full, = gather_weights([shard16["w_in"]])
    mix_norms = jnp.concatenate([hg_out_norm, sb_out_norm], axis=1)
    small_ones = ["w_out", "ple_proj", "ple_gate"]

    proj_h, sqkv, u1, *landed_small = in_proj_fwd(
        x2, attn_pre_norm, w_in_full, comm=gather_over_ici([shard16[k] for k in small_ones]))
    o_sb, sb_totals, sb_first, landed_gu = sb_fwd(sqkv, comm=gather_over_ici([shard16["w_gate_up"]]))
    o_hg, states, landed_down, *full_small = hgrn2_fwd(
        proj_h, hg_lower_gamma, comm=_both(gather_over_ici([shard16["w_down"]]), gather_over_d2d(landed_small)))
    full = dict(zip(small_ones, full_small), w_in=w_in_full)
    w_out_full = full["w_out"].reshape(D_MODEL, D_MODEL)
    w_pg_full = full["ple_gate"].reshape(D_MODEL, D_MODEL)
    cat, mix, h1, full["w_gate_up"], full["w_down"] = mix_out_fwd(
        o_hg, proj_h, o_sb, x2, mix_norms, attn_post_norm, w_out_full, comm=gather_over_d2d([landed_gu, landed_down]))
    w_down_full = full["w_down"].reshape(D_FF, D_MODEL)
    u2, gu, act, y, h2 = ffn_fwd(h1, ffn_pre_norm, ffn_post_norm, full["w_gate_up"], w_down_full)

    core = lax.axis_index("c").astype(jnp.int32).reshape(1)
    de, ds, dh2, h2b, pb, pack_loss = ple_loss(h2, p2, target, full["ple_proj"], w_pg_full)
    dy, dgu, dh1, pack_ffn = ffn_bwd(dh2, y, h1, gu, ffn_pre_norm, ffn_post_norm, full["w_gate_up"], w_down_full)
    local = dict(
        w_gate_up=weight_grad(u2, dgu, "grad_w_gate_up", tm=D_MODEL, tn=full["w_gate_up"].shape[2], tk=1024, col_pieces=True),
        w_down=weight_grad(act, dy, "grad_w_down", tm=D_FF // 2, tn=D_MODEL, tk=1024).reshape(full["w_down"].shape),
        ple_proj=weight_grad(pb, de, "grad_ple_proj", tm=pb.shape[1], tn=full["ple_proj"].shape[2], col_pieces=True),
        ple_gate=weight_grad(h2b, ds, "grad_ple_gate", tm=D_MODEL, tn=D_MODEL).reshape(full["ple_gate"].shape),
    )
    early = list(local)
    dmix, do_hg, dhg, do_sb, pack_mix, *got = mix_out_bwd(
        dh1, mix, o_hg, proj_h, o_sb, mix_norms, attn_post_norm, w_out_full, comm=core_halves([local[k] for k in early]))
    partial = [add_core_halves(local[k], g, core, "add_core_halves_" + k) for k, g in zip(early, got)]
    dsq, dsk, dsv = sb_bwd(sqkv, do_sb, sb_totals, sb_first)
    dhq, dhf, dhi, pack_hg, *by_source = hgrn2_bwd(proj_h, hg_lower_gamma, states, do_hg, comm=chip_partials(partial))
    halves = {k: add_chip_partials(s, core, "add_chip_partials_" + k) for k, s in zip(early, by_source)}
    dproj, grad_x, pack_in = in_proj_bwd([dhq, dhf, dhi, dhg, dsq, dsk, dsv], x2, dh1, attn_pre_norm, full["w_in"])

    late = ["w_in", "w_out"]
    local["w_in"] = weight_grad(u1, dproj, "grad_w_in", tm=D_MODEL, tn=full["w_in"].shape[2], tk=1024, col_pieces=True)
    local["w_out"] = weight_grad(cat, dmix, "grad_w_out", tm=D_MODEL, tn=D_MODEL).reshape(full["w_out"].shape)
    got = _run_comm(core_halves([local[k] for k in late]), "exchange_core_halves")
    partial = [add_core_halves(local[k], g, core, "add_core_halves_" + k) for k, g in zip(late, got)]
    by_source = _run_comm(chip_partials(partial), "exchange_chip_partials")
    halves.update({k: add_chip_partials(s, core, "add_chip_partials_" + k) for k, s in zip(late, by_source)})
    grads = dict(zip(names, join_core_halves([halves[k] for k in names])))

    upd = {k: adamw(big[k][0], grads[k], big[k][1], big[k][2], "adamw_" + k) for k in names}

    small = reduce_small(
        [pack_loss, pack_ffn, pack_mix, pack_hg, pack_in],
        _pack_small(attn_pre_norm, hg_lower_gamma, hg_out_norm, sb_out_norm, attn_post_norm, ffn_pre_norm, ffn_post_norm),
        _pack_small(m_attn_pre_norm, m_hg_lower_gamma, m_hg_out_norm, m_sb_out_norm, m_attn_post_norm, m_ffn_pre_norm, m_ffn_post_norm),
        _pack_small(v_attn_pre_norm, v_hg_lower_gamma, v_hg_out_norm, v_sb_out_norm, v_attn_post_norm, v_ffn_pre_norm, v_ffn_post_norm),
    )
    loss = small[0][ROW_LOSS, 0]
    s_grad, s_delta, s_m, s_v = (_unpack_small(t) for t in small)

    def ordered(small_vals, big_vals):
        a_pre, gam, hg_n, sb_n, a_post, f_pre, f_post = small_vals
        b = {k: big_vals[k][None] for k in names}
        return (a_pre, b["w_in"], gam, hg_n, sb_n, b["w_out"], a_post, f_pre, b["w_gate_up"], b["w_down"], f_post,
                b["ple_proj"], b["ple_gate"])

    return (loss, grad_x[None],
            *ordered(s_grad, grads),
            *ordered(s_delta, {k: upd[k][0] for k in names}),
            *ordered(s_m, {k: upd[k][1] for k in names}),
            *ordered(s_v, {k: upd[k][2] for k in names}))
```

```python
from typing import Callable, NamedTuple

import numpy as np
import jax
import jax.numpy as jnp
from jax import lax
from jax.experimental import pallas as pl
from jax.experimental.pallas import tpu as pltpu

F32 = jnp.float32
BF16 = jnp.bfloat16
MESH = pl.DeviceIdType.MESH

RMS_EPS = 1e-6
D_MODEL = 1024
HG_WIDTH = 512
HG_HEADS = 4
HG_DK = 128
HG_CHUNK = 64
HG_LEVELS = (32, 16, 8, 4, 2, 1)
HG_CHUNKS_PER_STEP = 2
SB_WIDTH = 512
SB_BLOCK = 128
SB_DH = 64
SB_SCALE = SB_DH ** -0.5
SB_UNDERFLOW_LOG = -87.5
SB_UNROLL = 2
SB_GROUP = 2
D_FF = 2816
N_CHIPS = 4
ROW_TILE = 256
V7X_VMEM_LIMIT = 56 * 1024 * 1024

ADAM_LR = 0.001
ADAM_B1 = 0.9
ADAM_B2 = 0.999
ADAM_EPS = 1e-08
ADAM_WD = 0.01
ADAM_STEP = 10

ROW_ATTN_PRE, ROW_GAMMA, ROW_MIX_NORMS, ROW_ATTN_POST, ROW_FFN_PRE, ROW_FFN_POST, ROW_LOSS = range(7)


def _params(sem=None, vmem=V7X_VMEM_LIMIT):
    return pltpu.CompilerParams(dimension_semantics=sem, vmem_limit_bytes=vmem)


def _dot(a, b):
    return jnp.dot(a.astype(BF16), b.astype(BF16), preferred_element_type=F32)


def _dot_nt(a, b):
    return lax.dot_general(a.astype(BF16), b.astype(BF16), (((1,), (1,)), ((), ())), preferred_element_type=F32)


def _dot_tn(a, b):
    return lax.dot_general(a.astype(BF16), b.astype(BF16), (((0,), (0,)), ((), ())), preferred_element_type=F32)


def _split(x):
    hi = x.astype(BF16)
    lo = (x - hi.astype(F32)).astype(BF16)
    return hi, lo


def _sum01_left(m01, x):
    hi, lo = _split(x)
    return jnp.dot(m01, hi, preferred_element_type=F32) + jnp.dot(m01, lo, preferred_element_type=F32)


def _sum01_right(x, m01_twice):
    hi, lo = _split(x)
    return jnp.dot(jnp.concatenate([hi, lo], axis=1), m01_twice, preferred_element_type=F32)


def _rms(x):
    r = lax.rsqrt(jnp.mean(x * x, axis=-1, keepdims=True) + RMS_EPS)
    return x * r, r


def _rms_bwd(dy, xhat, r, w):
    dxh = dy * w
    dx = r * (dxh - xhat * jnp.mean(dxh * xhat, axis=-1, keepdims=True))
    return dx, dy * xhat


def _sigmoid(x):
    return 1.0 / (1.0 + jnp.exp(-x))


def _neg_softplus(z):
    return -(jnp.maximum(z, 0.0) + jnp.log(1.0 + jnp.exp(-jnp.abs(z))))


def _colsum(x):
    return jnp.sum(x, axis=0, keepdims=True)


def _load_once(src_hbm, dst_vmem):
    @pl.when(pl.program_id(0) == 0)
    def _():
        pltpu.sync_copy(src_hbm, dst_vmem)


def _zero_first(ref):
    @pl.when(pl.program_id(0) == 0)
    def _():
        ref[...] = jnp.zeros(ref.shape, ref.dtype)


def _row_spec(width, col=0):
    return pl.BlockSpec((ROW_TILE, width), lambda i, col=col: (i, col))


def _full_spec(shape):
    return pl.BlockSpec(shape, lambda *_: (0,) * len(shape))


ANY_SPEC = pl.BlockSpec(memory_space=pl.ANY)
PACK_SPEC = _full_spec((8, D_MODEL))


class Comm(NamedTuple):
    inputs: list
    out_shape: list
    aliases: dict
    scratch: list
    start: Callable
    finish: Callable


def _pallas(body, *, comm=None, edge=None, in_specs, out_specs, out_shape, scratch_shapes=(), **kw):
    if comm is None:
        return pl.pallas_call(body, in_specs=in_specs, out_specs=out_specs, out_shape=out_shape,
                              scratch_shapes=scratch_shapes, **kw)
    n_in, n_out, n_scr = len(in_specs), len(out_specs), len(scratch_shapes)
    c_in, c_out = len(comm.inputs), len(comm.out_shape)

    def both(*refs):
        ins, c_ins = refs[:n_in], refs[n_in:n_in + c_in]
        outs = refs[n_in + c_in:n_in + c_in + n_out]
        c_outs = refs[n_in + c_in + n_out:n_in + c_in + n_out + c_out]
        rest = refs[n_in + c_in + n_out + c_out:]
        scr, c_scr = rest[:n_scr], rest[n_scr:]
        first, last = edge()

        @pl.when(first)
        def _():
            comm.start(c_ins, c_outs, c_scr)

        body(*ins, *outs, *scr)

        @pl.when(last)
        def _():
            comm.finish(c_ins, c_outs, c_scr)

    call = pl.pallas_call(
        both, in_specs=list(in_specs) + [ANY_SPEC] * c_in, out_specs=list(out_specs) + [ANY_SPEC] * c_out,
        out_shape=list(out_shape) + list(comm.out_shape), scratch_shapes=list(scratch_shapes) + list(comm.scratch),
        input_output_aliases={n_in + a: n_out + b for a, b in comm.aliases.items()}, **kw)
    return lambda *args: call(*args, *comm.inputs)


def _grid_edge(steps):
    return lambda: (pl.program_id(0) == 0, pl.program_id(0) == steps - 1)


def in_proj_fwd(x, g_pre, w_in, comm=None):
    T = x.shape[0]
    pw = w_in.shape[2]

    def body(x_ref, g_ref, w_hbm, ph_ref, sqkv_ref, u_ref, w_vmem, proj_s):
        _load_once(w_hbm, w_vmem)
        xh, _ = _rms(x_ref[...])
        u = (xh * g_ref[...]).astype(BF16)
        u_ref[...] = u
        for q in range(N_CHIPS):
            proj_s[:, pw * q:pw * (q + 1)] = jnp.dot(u, w_vmem[q], preferred_element_type=F32)
        ph_ref[...] = proj_s[:, :4 * HG_WIDTH]
        sqkv_ref[:, :SB_WIDTH] = (proj_s[:, 4 * HG_WIDTH:4 * HG_WIDTH + SB_WIDTH] * SB_SCALE).astype(BF16)
        sqkv_ref[:, SB_WIDTH:] = proj_s[:, 4 * HG_WIDTH + SB_WIDTH:].astype(BF16)

    return _pallas(
        body, comm=comm, edge=_grid_edge(T // ROW_TILE), name="in_proj_fwd", grid=(T // ROW_TILE,),
        in_specs=[_row_spec(D_MODEL), _full_spec((1, D_MODEL)), ANY_SPEC],
        out_specs=[_row_spec(4 * HG_WIDTH), _row_spec(3 * SB_WIDTH), _row_spec(D_MODEL)],
        out_shape=[jax.ShapeDtypeStruct((T, 4 * HG_WIDTH), F32), jax.ShapeDtypeStruct((T, 3 * SB_WIDTH), BF16),
                   jax.ShapeDtypeStruct((T, D_MODEL), BF16)],
        scratch_shapes=[pltpu.VMEM(w_in.shape, BF16), pltpu.VMEM((ROW_TILE, N_CHIPS * pw), F32)],
        compiler_params=_params(("arbitrary",)),
    )(x, g_pre, w_in)


def _hg_sum_matrix():
    C = HG_CHUNK
    t = np.arange(C)[:, None]
    j = np.arange(C)[None, :]
    mats = [j <= t, j > t]
    for h in HG_LEVELS:
        start = (t // (2 * h)) * (2 * h)
        upper = (t & h) != 0
        mats.append(np.where(upper, (j >= start + h) & (j <= t), (j > t) & (j <= start + h - 1)))
    return np.concatenate(mats, 0).astype(np.float32)


def _hg_level_masks():
    C = HG_CHUNK
    t = lax.broadcasted_iota(jnp.int32, (C, C), 0)
    s = lax.broadcasted_iota(jnp.int32, (C, C), 1)
    x = t ^ s
    masks = [t == s]
    for h in HG_LEVELS:
        masks.append((x >= h) & (x < 2 * h) & (t > s))
    return masks


def _hg_gates(hq, hf, gamma):
    lb = 1.0 / (1.0 + jnp.exp(gamma[1:2, :] - gamma[0:1, :]))
    sq = _sigmoid(hq)
    q = hq * sq
    sig = _sigmoid(hf)
    nsig = _sigmoid(-hf)
    f = lb + (1.0 - lb) * sig
    k = (1.0 - lb) * nsig
    g = jnp.log(f)
    return q, k, g, dict(lb=lb, sq=sq, sig=sig, nsig=nsig, f=f)


def _hg_head_decays(A, h):
    C, K = HG_CHUNK, HG_DK
    sl = slice(K * h, K * (h + 1))
    blocks = [A[C * r:C * (r + 1), sl] for r in range(2 + len(HG_LEVELS))]
    return blocks[0], blocks[1], [None] + blocks[2:]


def _hg_products(q, k, levels):
    return [_dot_nt(q, k)] + [_dot_nt(q * a, k * a) for a in levels[1:]]


def _hg_select(prods, masks):
    sc = jnp.where(masks[0], prods[0], 0.0)
    for p, m in zip(prods[1:], masks[1:]):
        sc = jnp.where(m, p, sc)
    return sc


def hgrn2_fwd(proj_h, gamma, comm=None):
    T = proj_h.shape[0]
    C, K, H, S = HG_CHUNK, HG_DK, HG_HEADS, HG_CHUNKS_PER_STEP
    n_steps = T // (S * C)
    msum = jnp.asarray(_hg_sum_matrix(), BF16)

    def body(hq_ref, hf_ref, hi_ref, gam_ref, msum_ref, o_ref, st_ref, st_s):
        _zero_first(st_s)
        q, k, g, _ = _hg_gates(hq_ref[...], hf_ref[...], gam_ref[...])
        v = hi_ref[...]
        masks = _hg_level_masks()
        parts = []
        for s in range(S):
            rows = slice(C * s, C * (s + 1))
            A = jnp.exp(_sum01_left(msum_ref[...], g[rows]))
            for h in range(H):
                sl = slice(K * h, K * (h + 1))
                ab, ar, levels = _hg_head_decays(A, h)
                parts.append(dict(s=s, h=h, rows=rows, sl=sl, ab=ab, ar=ar, levels=levels,
                                  q=q[rows, sl], k=k[rows, sl], v=v[rows, sl]))
        for pt in parts:
            pt["prods"] = _hg_products(pt["q"], pt["k"], pt["levels"])
            pt["grown"] = _dot_tn(pt["v"], pt["k"] * pt["ar"])
        for pt in parts:
            pt["sc"] = _hg_select(pt["prods"], masks)
        state = [st_s[h] for h in range(H)]
        for pt in parts:
            h, ab = pt["h"], pt["ab"]
            o_ref[pt["rows"], pt["sl"]] = _dot_nt(pt["q"] * ab, state[h]) + _dot(pt["sc"], pt["v"])
            state[h] = state[h] * ab[C - 1:C, :] + pt["grown"]
            st_ref[pt["s"], h] = state[h]
        for h in range(H):
            st_s[h] = state[h]

    blk = lambda col: pl.BlockSpec((S * C, HG_WIDTH), lambda c, col=col: (c, col))
    return _pallas(
        body, comm=comm, edge=_grid_edge(n_steps), name="hgrn2_fwd", grid=(n_steps,),
        in_specs=[blk(0), blk(1), blk(2), _full_spec((2, HG_WIDTH)), _full_spec(msum.shape)],
        out_specs=[blk(0), pl.BlockSpec((S, H, K, K), lambda c: (c, 0, 0, 0))],
        out_shape=[jax.ShapeDtypeStruct((T, HG_WIDTH), F32), jax.ShapeDtypeStruct((S * n_steps, H, K, K), F32)],
        scratch_shapes=[pltpu.VMEM((H, K, K), F32)],
        compiler_params=_params(("arbitrary",)),
    )(proj_h, proj_h, proj_h, gamma, msum)


def hgrn2_bwd(proj_h, gamma, states, do, comm=None):
    T = proj_h.shape[0]
    C, K, H, S = HG_CHUNK, HG_DK, HG_HEADS, HG_CHUNKS_PER_STEP
    n_steps = T // (S * C)
    n_sums = 2 + len(HG_LEVELS)
    msum = jnp.asarray(_hg_sum_matrix(), BF16)
    msum_t = jnp.asarray(_hg_sum_matrix().T, BF16)

    def body(hq_ref, hf_ref, hi_ref, do_ref, gam_ref, msum_ref, msum_t_ref, st_prev_ref, st_ref,
             dhq_ref, dhf_ref, dhi_ref, pack_ref, dst_s, dlb_s, dq_s, dk_s, de_s):
        step = pl.program_id(0)
        _zero_first(dst_s)
        _zero_first(dlb_s)
        _zero_first(pack_ref)
        hq = hq_ref[...]
        q, k, g, aux = _hg_gates(hq, hf_ref[...], gam_ref[...])
        v = hi_ref[...]
        do_all = do_ref[...]
        masks = _hg_level_masks()
        is_last_row = lax.broadcasted_iota(jnp.int32, (C, K), 0) == C - 1
        has_prev = (step < n_steps - 1).astype(F32)
        parts = []
        for s in reversed(range(S)):
            rows = slice(C * s, C * (s + 1))
            A = jnp.exp(_sum01_left(msum_ref[...], g[rows]))
            for h in range(H):
                sl = slice(K * h, K * (h + 1))
                ab, ar, levels = _hg_head_decays(A, h)
                st_in = st_prev_ref[0, h] * has_prev if s == 0 else st_ref[s - 1, h]
                parts.append(dict(s=s, h=h, rows=rows, sl=sl, ab=ab, ar=ar, levels=levels, st_in=st_in,
                                  q=q[rows, sl], k=k[rows, sl], v=v[rows, sl], do=do_all[rows, sl]))
        for pt in parts:
            pt["prods"] = _hg_products(pt["q"], pt["k"], pt["levels"])
            pt["da"] = _dot_nt(pt["do"], pt["v"])
            pt["t1"] = pt["ab"] * _dot(pt["do"], pt["st_in"])
            pt["dst_add"] = _dot_tn(pt["do"], pt["q"] * pt["ab"])
        dstate = [dst_s[h] for h in range(H)]
        for pt in parts:
            h = pt["h"]
            pt["dst_out"] = dstate[h]
            pt["t2"] = pt["ar"] * _dot(pt["v"], dstate[h])
            pt["dv_state"] = _dot_nt(pt["k"] * pt["ar"], dstate[h])
            dstate[h] = dstate[h] * pt["ab"][C - 1:C, :] + pt["dst_add"]
        for h in range(H):
            dst_s[h] = dstate[h]
        for pt in parts:
            pt["sc"] = _hg_select(pt["prods"], masks)
            pt["dam"] = [jnp.where(m, pt["da"], 0.0) for m in masks]
        for pt in parts:
            qh, kh = pt["q"], pt["k"]
            pt["dq_parts"] = [_dot(pt["dam"][0], kh)] + [
                a * _dot(dam, kh * a) for a, dam in zip(pt["levels"][1:], pt["dam"][1:])]
            pt["dk_parts"] = [_dot_tn(pt["dam"][0], qh)] + [
                a * _dot_tn(dam, qh * a) for a, dam in zip(pt["levels"][1:], pt["dam"][1:])]
            pt["dv_intra"] = _dot_tn(pt["sc"], pt["do"])
        for pt in parts:
            s, rows, sl, qh, kh, ab = pt["s"], pt["rows"], pt["sl"], pt["q"], pt["k"], pt["ab"]
            decayed = _colsum(pt["st_in"] * pt["dst_out"]) * ab[C - 1:C, :]
            de_s[s, 0:C, sl] = qh * pt["t1"] + jnp.where(is_last_row, decayed, 0.0)
            de_s[s, C:2 * C, sl] = kh * pt["t2"]
            dq = pt["t1"] + pt["dq_parts"][0]
            dk = pt["t2"] + pt["dk_parts"][0]
            for r, (t1, t2) in enumerate(zip(pt["dq_parts"][1:], pt["dk_parts"][1:])):
                dq = dq + t1
                dk = dk + t2
                de_s[s, C * (r + 2):C * (r + 3), sl] = qh * t1 + kh * t2
            dhi_ref[rows, sl] = pt["dv_intra"] + pt["dv_state"]
            dq_s[rows, sl] = dq
            dk_s[rows, sl] = dk
        dg = jnp.concatenate([_sum01_left(msum_t_ref[...], de_s[s]) for s in range(S)], axis=0)
        dk = dk_s[...]
        sq, lb = aux["sq"], aux["lb"]
        dhq_ref[...] = dq_s[...] * (sq * (1.0 + hq * (1.0 - sq)))
        common = dg / aux["f"] - dk
        dhf_ref[...] = (1.0 - lb) * aux["sig"] * aux["nsig"] * common
        dlb_s[...] += _colsum(aux["nsig"] * common)

        @pl.when(step == n_steps - 1)
        def _():
            dgam = lb * (1.0 - lb) * dlb_s[...]
            pack_ref[ROW_GAMMA:ROW_GAMMA + 1, :HG_WIDTH] = dgam
            pack_ref[ROW_GAMMA:ROW_GAMMA + 1, HG_WIDTH:] = -dgam

    last = n_steps - 1
    blk = lambda col: pl.BlockSpec((S * C, HG_WIDTH), lambda c, col=col: (last - c, col))
    return _pallas(
        body, comm=comm, edge=_grid_edge(n_steps), name="hgrn2_bwd", grid=(n_steps,),
        in_specs=[blk(0), blk(1), blk(2), blk(0), _full_spec((2, HG_WIDTH)), _full_spec(msum.shape),
                  _full_spec(msum_t.shape),
                  pl.BlockSpec((1, H, K, K), lambda c: (jnp.maximum(S * (last - c) - 1, 0), 0, 0, 0)),
                  pl.BlockSpec((S, H, K, K), lambda c: (last - c, 0, 0, 0))],
        out_specs=[blk(0), blk(0), blk(0), PACK_SPEC],
        out_shape=[jax.ShapeDtypeStruct((T, HG_WIDTH), F32)] * 3 + [jax.ShapeDtypeStruct((8, D_MODEL), F32)],
        scratch_shapes=[pltpu.VMEM((H, K, K), F32), pltpu.VMEM((1, HG_WIDTH), F32), pltpu.VMEM((S * C, HG_WIDTH), F32),
                        pltpu.VMEM((S * C, HG_WIDTH), F32), pltpu.VMEM((S, n_sums * C, HG_WIDTH), F32)],
        compiler_params=_params(("arbitrary",)),
    )(proj_h, proj_h, proj_h, do, gamma, msum, msum_t, states, states)


def _sb_sum_matrix(inclusive):
    B = SB_BLOCK
    j = np.arange(B)[:, None]
    s = np.arange(B)[None, :]
    tri = (j >= s) if inclusive else (j > s)
    once = np.concatenate([tri, np.ones((B, B), bool)], 1).astype(np.float32)
    return np.concatenate([once, once], 0)


def _sb_prefix_matrix(inclusive):
    B = SB_BLOCK
    j = np.arange(B)[:, None]
    s = np.arange(B)[None, :]
    tri = (j <= s) if inclusive else (j < s)
    once = np.concatenate([tri, np.ones((B, B), bool)], 1).astype(np.float32)
    return np.concatenate([once, once], 0)


def _sb_iotas():
    shape = (SB_BLOCK, SB_BLOCK)
    return lax.broadcasted_iota(jnp.int32, shape, 0), lax.broadcasted_iota(jnp.int32, shape, 1)


def _sb_heads(q, first):
    heads = []
    for g in range(SB_GROUP):
        qg = q[:, SB_BLOCK * g:SB_BLOCK * (g + 1)]
        zero = jnp.zeros_like(qg)
        heads += [(g, jnp.where(first, qg, zero)), (g, jnp.where(first, zero, qg))]
    return heads


def _lanes(x, g):
    return x[:, SB_BLOCK * g:SB_BLOCK * (g + 1)]


def sb_fwd(sqkv, comm=None):
    T = sqkv.shape[0]
    B = SB_BLOCK
    W = SB_GROUP * B
    groups = SB_WIDTH // W
    usum = jnp.asarray(_sb_sum_matrix(False), BF16)

    def body(q_ref, k_ref, v_ref, u_ref, o_ref, tl_ref, first_ref):
        p, i = pl.program_id(0), pl.program_id(1)
        row, lane = _sb_iotas()
        first = lane < SB_DH
        heads = _sb_heads(q_ref[...], first)
        u = u_ref[...]

        def more(loop):
            n, reachable, _ = loop
            return (SB_UNROLL * n <= i) & (reachable > 0)

        def step(loop):
            n, _, state = loop
            blocks = []
            for sub in range(SB_UNROLL):
                j = i - SB_UNROLL * n - sub
                off = pl.multiple_of(jnp.maximum(j, 0) * B, B)
                valid = ((lane + j * B) < (row + i * B)) & (j >= 0)
                blocks.append((k_ref[pl.ds(off, B), :], v_ref[pl.ds(off, B), :], valid))
            z = [[_dot_nt(qh, _lanes(kj, g)) for g, qh in heads] for kj, _, _ in blocks]
            lnb = [[jnp.where(valid, _neg_softplus(zz), 0.0) for zz in zs] for zs, (_, _, valid) in zip(z, blocks)]
            sums = [[_sum01_right(x, u) for x in xs] for xs in lnb]
            out = []
            for h, (carry, acc) in enumerate(state):
                for sub, (_, vj, valid) in enumerate(blocks):
                    expo = z[sub][h] + lnb[sub][h] + carry + sums[sub][h][:, :B]
                    acc = acc + _dot(jnp.where(valid, jnp.exp(expo), 0.0), _lanes(vj, heads[h][0]))
                    carry = carry + sums[sub][h][:, B:]
                out.append((carry, acc))
            state = tuple(out)
            worst = state[0][0]
            for carry, _ in state[1:]:
                worst = jnp.maximum(worst, carry)
            reachable = (jnp.max(worst) > SB_UNDERFLOW_LOG).astype(jnp.int32)
            return n + 1, reachable, state

        zero = jnp.zeros((B, B), F32)
        done, _, state = lax.while_loop(
            more, step, (jnp.int32(0), jnp.int32(1), tuple((zero, zero) for _ in heads)))
        for g in range(SB_GROUP):
            (tot0, acc0), (tot1, acc1) = state[2 * g], state[2 * g + 1]
            o_ref[:, B * g:B * (g + 1)] = jnp.where(first, acc0, acc1)
            tl_ref[:, B * g:B * (g + 1)] = jnp.where(first, tot0, tot1)
        first_ref[p, i] = jnp.maximum(i + 1 - SB_UNROLL * done, 0)

    def edge():
        p, i = pl.program_id(0), pl.program_id(1)
        return (p == 0) & (i == 0), (p == groups - 1) & (i == T // B - 1)

    return _pallas(
        body, comm=comm, edge=edge, name="sb_fwd", grid=(groups, T // B),
        in_specs=[pl.BlockSpec((B, W), lambda p, i: (i, p)),
                  pl.BlockSpec((T, W), lambda p, i: (0, groups + p)),
                  pl.BlockSpec((T, W), lambda p, i: (0, 2 * groups + p)),
                  pl.BlockSpec(usum.shape, lambda p, i: (0, 0))],
        out_specs=[pl.BlockSpec((B, W), lambda p, i: (i, p))] * 2 + [pl.BlockSpec(memory_space=pltpu.SMEM)],
        out_shape=[jax.ShapeDtypeStruct((T, SB_WIDTH), F32)] * 2 + [jax.ShapeDtypeStruct((groups, T // B), jnp.int32)],
        compiler_params=_params(("arbitrary", "arbitrary")),
    )(sqkv, sqkv, sqkv, usum)


def sb_bwd(sqkv, do, tl, first_block):
    T = sqkv.shape[0]
    B = SB_BLOCK
    W = SB_GROUP * B
    groups = SB_WIDTH // W
    upre = jnp.asarray(_sb_prefix_matrix(True), BF16)
    uexc = jnp.asarray(_sb_prefix_matrix(False), BF16)

    def body(q_ref, k_ref, v_ref, do_ref, tl_ref, up_ref, ue_ref, first_ref, dq_ref, dk_ref, dv_ref):
        p, i = pl.program_id(0), pl.program_id(1)

        @pl.when(i == 0)
        def _():
            dk_ref[...] = jnp.zeros(dk_ref.shape, F32)
            dv_ref[...] = jnp.zeros(dv_ref.shape, F32)

        row, lane = _sb_iotas()
        first = lane < SB_DH
        do = do_ref[...]
        tl_all = tl_ref[...]
        heads = []
        for (g, qh), at in zip(_sb_heads(q_ref[...], first), (0, B - 1) * SB_GROUP):
            dog = _lanes(do, g)
            keep = first if at == 0 else jnp.logical_not(first)
            heads.append((g, qh, jnp.where(keep, dog, jnp.zeros_like(dog)).astype(BF16),
                          _lanes(tl_all, g)[:, at:at + 1]))
        up = up_ref[...]
        ue = ue_ref[...]
        start = first_ref[p, i]

        def step(n, state):
            blocks = []
            for sub in range(SB_UNROLL):
                j = start + SB_UNROLL * n + sub
                off = pl.multiple_of(jnp.minimum(j, i) * B, B)
                valid = (lane + j * B) < (row + i * B)
                blocks.append((off, k_ref[pl.ds(off, B), :], v_ref[pl.ds(off, B), :], valid))
            combos = [(s, h) for s in range(SB_UNROLL) for h in range(len(heads))]
            z = {(s, h): _dot_nt(heads[h][1], _lanes(blocks[s][1], heads[h][0])) for s, h in combos}
            da = {(s, h): _dot_nt(heads[h][2], _lanes(blocks[s][2], heads[h][0])) for s, h in combos}
            lnb = {c: jnp.where(blocks[c[0]][3], _neg_softplus(z[c]), 0.0) for c in combos}
            lb = {c: z[c] + lnb[c] for c in combos}
            sums = {c: _sum01_right(lnb[c], up) for c in combos}
            a, w = {}, {}
            seen = [st[0] for st in state]
            for s, h in combos:
                expo = lb[s, h] + (heads[h][3] - seen[h] - sums[s, h][:, :B])
                a[s, h] = jnp.where(blocks[s][3], jnp.exp(expo), 0.0)
                w[s, h] = a[s, h] * da[s, h]
                seen[h] = seen[h] + sums[s, h][:, B:]
            wsums = {c: _sum01_right(w[c], ue) for c in combos}
            dz = {}
            seen_w = [st[1] for st in state]
            for s, h in combos:
                beta = jnp.exp(lb[s, h])
                before = seen_w[h] + wsums[s, h][:, :B]
                dz[s, h] = jnp.where(blocks[s][3], w[s, h] * (1.0 - beta) - before * beta, 0.0)
                seen_w[h] = seen_w[h] + wsums[s, h][:, B:]
            dq = [st[2] for st in state]
            for s, h in combos:
                dq[h] = dq[h] + _dot(dz[s, h], _lanes(blocks[s][1], heads[h][0]))
            for s in range(SB_UNROLL):
                off = blocks[s][0]
                for g in range(SB_GROUP):
                    h0, h1 = 2 * g, 2 * g + 1
                    dk_ref[pl.ds(off, B), B * g:B * (g + 1)] += (_dot_tn(dz[s, h0], heads[h0][1])
                                                                 + _dot_tn(dz[s, h1], heads[h1][1]))
                    dv_ref[pl.ds(off, B), B * g:B * (g + 1)] += (_dot_tn(a[s, h0], heads[h0][2])
                                                                 + _dot_tn(a[s, h1], heads[h1][2]))
            return tuple(zip(seen, seen_w, dq))

        zero = jnp.zeros((B, B), F32)
        trips = (i - start + SB_UNROLL) // SB_UNROLL
        state = lax.fori_loop(0, trips, step, tuple((zero, zero, zero) for _ in heads))
        for g in range(SB_GROUP):
            dq_ref[:, B * g:B * (g + 1)] = jnp.where(first, state[2 * g][2], state[2 * g + 1][2]) * SB_SCALE

    qblk = pl.BlockSpec((B, W), lambda p, i: (i, p))
    full = pl.BlockSpec((T, W), lambda p, i: (0, p))
    return pl.pallas_call(
        body, name="sb_bwd", grid=(groups, T // B),
        in_specs=[qblk, pl.BlockSpec((T, W), lambda p, i: (0, groups + p)),
                  pl.BlockSpec((T, W), lambda p, i: (0, 2 * groups + p)), qblk, qblk,
                  pl.BlockSpec(upre.shape, lambda p, i: (0, 0)), pl.BlockSpec(uexc.shape, lambda p, i: (0, 0)),
                  pl.BlockSpec(memory_space=pltpu.SMEM)],
        out_specs=[qblk, full, full],
        out_shape=[jax.ShapeDtypeStruct((T, SB_WIDTH), F32)] * 3,
        compiler_params=_params(("arbitrary", "arbitrary")),
    )(sqkv, sqkv, sqkv, do, tl, upre, uexc, first_block)


def _mixer_out(o_hg, hg, o_sb, g_hg, g_sb):
    n_hg, r_hg = _rms(o_hg)
    s_hg = _sigmoid(hg)
    n_sb, r_sb = _rms(o_sb)
    return dict(n_hg=n_hg, r_hg=r_hg, s_hg=s_hg, n_sb=n_sb, r_sb=r_sb,
                y_hg=n_hg * g_hg * (hg * s_hg), y_sb=n_sb * g_sb)


def mix_out_fwd(o_hg, proj_h, o_sb, x, norms, g_post, w_out, comm=None):
    T = x.shape[0]

    def body(ohg_ref, hg_ref, osb_ref, x_ref, nrm_ref, gp_ref, w_hbm, cat_ref, mix_ref, h1_ref, w_vmem):
        _load_once(w_hbm, w_vmem)
        nrm = nrm_ref[...]
        m = _mixer_out(ohg_ref[...], hg_ref[...], osb_ref[...], nrm[:, :HG_WIDTH], nrm[:, HG_WIDTH:])
        cat_ref[:, :HG_WIDTH] = m["y_hg"].astype(BF16)
        cat_ref[:, HG_WIDTH:] = m["y_sb"].astype(BF16)
        mix = jnp.dot(cat_ref[...], w_vmem[...], preferred_element_type=F32)
        mix_ref[...] = mix
        mh, _ = _rms(mix)
        h1_ref[...] = x_ref[...] + mh * gp_ref[...]

    return _pallas(
        body, comm=comm, edge=_grid_edge(T // ROW_TILE), name="mix_out_fwd", grid=(T // ROW_TILE,),
        in_specs=[_row_spec(HG_WIDTH), _row_spec(HG_WIDTH, 3), _row_spec(SB_WIDTH), _row_spec(D_MODEL),
                  _full_spec((1, D_MODEL)), _full_spec((1, D_MODEL)), ANY_SPEC],
        out_specs=[_row_spec(D_MODEL)] * 3,
        out_shape=[jax.ShapeDtypeStruct((T, D_MODEL), BF16), jax.ShapeDtypeStruct((T, D_MODEL), F32),
                   jax.ShapeDtypeStruct((T, D_MODEL), F32)],
        scratch_shapes=[pltpu.VMEM(w_out.shape, BF16)],
        compiler_params=_params(("arbitrary",)),
    )(o_hg, proj_h, o_sb, x, norms, g_post, w_out)


def ffn_fwd(h1, g_pre, g_post, w_gu, w_down):
    T = h1.shape[0]
    pw = w_gu.shape[2]

    def body(h1_ref, gpre_ref, gpost_ref, wgu_hbm, wd_hbm, u2_ref, gu_ref, act_ref, y_ref, h2_ref,
             wgu_vmem, wd_vmem, gu_s):
        _load_once(wgu_hbm, wgu_vmem)
        _load_once(wd_hbm, wd_vmem)
        h1v = h1_ref[...]
        hh, _ = _rms(h1v)
        u2 = (hh * gpre_ref[...]).astype(BF16)
        u2_ref[...] = u2
        for q in range(N_CHIPS):
            gu_s[:, pw * q:pw * (q + 1)] = jnp.dot(u2, wgu_vmem[q], preferred_element_type=F32)
        gu_ref[...] = gu_s[...].astype(BF16)
        gate = gu_s[:, :D_FF]
        act = (gate * _sigmoid(gate) * gu_s[:, D_FF:]).astype(BF16)
        act_ref[...] = act
        y = jnp.dot(act, wd_vmem[...], preferred_element_type=F32)
        y_ref[...] = y
        yh, _ = _rms(y)
        h2_ref[...] = h1v + yh * gpost_ref[...]

    return pl.pallas_call(
        body, name="ffn_fwd", grid=(T // ROW_TILE,),
        in_specs=[_row_spec(D_MODEL), _full_spec((1, D_MODEL)), _full_spec((1, D_MODEL)), ANY_SPEC, ANY_SPEC],
        out_specs=[_row_spec(D_MODEL), _row_spec(2 * D_FF), _row_spec(D_FF), _row_spec(D_MODEL), _row_spec(D_MODEL)],
        out_shape=[jax.ShapeDtypeStruct((T, D_MODEL), BF16), jax.ShapeDtypeStruct((T, 2 * D_FF), BF16),
                   jax.ShapeDtypeStruct((T, D_FF), BF16), jax.ShapeDtypeStruct((T, D_MODEL), F32),
                   jax.ShapeDtypeStruct((T, D_MODEL), F32)],
        scratch_shapes=[pltpu.VMEM(w_gu.shape, BF16), pltpu.VMEM(w_down.shape, BF16),
                        pltpu.VMEM((ROW_TILE, 2 * D_FF), F32)],
        compiler_params=_params(("arbitrary",)),
    )(h1, g_pre, g_post, w_gu, w_down)


def ple_loss(h2, p, target, w_ple, w_pg):
    T = h2.shape[0]
    pw = w_ple.shape[2]

    def body(h2_ref, p_ref, t_ref, wple_hbm, wpg_hbm, de_ref, ds_ref, dh2_ref, h2b_ref, pb_ref, pack_ref,
             wple_vmem, wpg_vmem, e_s):
        _load_once(wple_hbm, wple_vmem)
        _load_once(wpg_hbm, wpg_vmem)
        _zero_first(pack_ref)
        h2v = h2_ref[...]
        h2b = h2v.astype(BF16)
        h2b_ref[...] = h2b
        pb = p_ref[...].astype(BF16)
        pb_ref[...] = pb
        for q in range(N_CHIPS):
            e_s[:, pw * q:pw * (q + 1)] = jnp.dot(pb, wple_vmem[q], preferred_element_type=F32)
        e = e_s[...]
        sig = _sigmoid(jnp.dot(h2b, wpg_vmem[...], preferred_element_type=F32))
        err = h2v + e * sig - t_ref[...]
        part = 0.5 * jnp.sum(jnp.mean(err * err, axis=-1, keepdims=True), axis=0, keepdims=True)
        lane = lax.broadcasted_iota(jnp.int32, (1, D_MODEL), 1)
        pack_ref[ROW_LOSS:ROW_LOSS + 1, :] += jnp.where(lane == 0, part, 0.0)
        dh3 = err * (1.0 / D_MODEL)
        de_ref[...] = (dh3 * sig).astype(BF16)
        ds = (dh3 * e * sig * (1.0 - sig)).astype(BF16)
        ds_ref[...] = ds
        dh2_ref[...] = dh3 + _dot_nt(ds, wpg_vmem[...])

    return pl.pallas_call(
        body, name="ple_loss", grid=(T // ROW_TILE,),
        in_specs=[_row_spec(D_MODEL), _row_spec(p.shape[1]), _row_spec(D_MODEL), ANY_SPEC, ANY_SPEC],
        out_specs=[_row_spec(D_MODEL), _row_spec(D_MODEL), _row_spec(D_MODEL), _row_spec(D_MODEL),
                   _row_spec(p.shape[1]), PACK_SPEC],
        out_shape=[jax.ShapeDtypeStruct((T, D_MODEL), BF16), jax.ShapeDtypeStruct((T, D_MODEL), BF16),
                   jax.ShapeDtypeStruct((T, D_MODEL), F32), jax.ShapeDtypeStruct((T, D_MODEL), BF16),
                   jax.ShapeDtypeStruct(p.shape, BF16), jax.ShapeDtypeStruct((8, D_MODEL), F32)],
        scratch_shapes=[pltpu.VMEM(w_ple.shape, BF16), pltpu.VMEM(w_pg.shape, BF16), pltpu.VMEM((ROW_TILE, D_MODEL), F32)],
        compiler_params=_params(("arbitrary",)),
    )(h2, p, target, w_ple, w_pg)


def ffn_bwd(dh2, y, h1, gu, g_pre, g_post, w_gu, w_down):
    T = h1.shape[0]
    pw = w_gu.shape[2]

    def body(dh2_ref, y_ref, h1_ref, gu_ref, gpre_ref, gpost_ref, wgu_hbm, wd_hbm, dy_ref, dgu_ref, dh1_ref, pack_ref,
             wgu_vmem, wd_vmem):
        _load_once(wgu_hbm, wgu_vmem)
        _load_once(wd_hbm, wd_vmem)
        _zero_first(pack_ref)
        dh2v = dh2_ref[...]
        yh, ry = _rms(y_ref[...])
        dy, dw = _rms_bwd(dh2v, yh, ry, gpost_ref[...])
        pack_ref[ROW_FFN_POST:ROW_FFN_POST + 1, :] += _colsum(dw)
        dyb = dy.astype(BF16)
        dy_ref[...] = dyb
        dact = _dot_nt(dyb, wd_vmem[...])
        gate = gu_ref[:, :D_FF].astype(F32)
        up = gu_ref[:, D_FF:].astype(F32)
        sg = _sigmoid(gate)
        dgu_ref[:, :D_FF] = (dact * up * (sg * (1.0 + gate * (1.0 - sg)))).astype(BF16)
        dgu_ref[:, D_FF:] = (dact * gate * sg).astype(BF16)
        du2 = _dot_nt(dgu_ref[:, :pw], wgu_vmem[0])
        for q in range(1, N_CHIPS):
            du2 = du2 + _dot_nt(dgu_ref[:, pw * q:pw * (q + 1)], wgu_vmem[q])
        hh, rh = _rms(h1_ref[...])
        dh, dw = _rms_bwd(du2, hh, rh, gpre_ref[...])
        pack_ref[ROW_FFN_PRE:ROW_FFN_PRE + 1, :] += _colsum(dw)
        dh1_ref[...] = dh2v + dh

    return pl.pallas_call(
        body, name="ffn_bwd", grid=(T // ROW_TILE,),
        in_specs=[_row_spec(D_MODEL), _row_spec(D_MODEL), _row_spec(D_MODEL), _row_spec(2 * D_FF),
                  _full_spec((1, D_MODEL)), _full_spec((1, D_MODEL)), ANY_SPEC, ANY_SPEC],
        out_specs=[_row_spec(D_MODEL), _row_spec(2 * D_FF), _row_spec(D_MODEL), PACK_SPEC],
        out_shape=[jax.ShapeDtypeStruct((T, D_MODEL), BF16), jax.ShapeDtypeStruct((T, 2 * D_FF), BF16),
                   jax.ShapeDtypeStruct((T, D_MODEL), F32), jax.ShapeDtypeStruct((8, D_MODEL), F32)],
        scratch_shapes=[pltpu.VMEM(w_gu.shape, BF16), pltpu.VMEM(w_down.shape, BF16)],
        compiler_params=_params(("arbitrary",)),
    )(dh2, y, h1, gu, g_pre, g_post, w_gu, w_down)


def mix_out_bwd(dh1, mix, o_hg, proj_h, o_sb, norms, g_post, w_out, comm=None):
    T = dh1.shape[0]

    def body(dh1_ref, mix_ref, ohg_ref, hg_ref, osb_ref, nrm_ref, gp_ref, w_hbm, dmix_ref, dohg_ref, dhg_ref, dosb_ref,
             pack_ref, w_vmem):
        _load_once(w_hbm, w_vmem)
        _zero_first(pack_ref)
        mh, rm = _rms(mix_ref[...])
        dmix, dw = _rms_bwd(dh1_ref[...], mh, rm, gp_ref[...])
        pack_ref[ROW_ATTN_POST:ROW_ATTN_POST + 1, :] += _colsum(dw)
        dmb = dmix.astype(BF16)
        dmix_ref[...] = dmb
        dcat = _dot_nt(dmb, w_vmem[...])
        nrm = nrm_ref[...]
        g_hg, g_sb = nrm[:, :HG_WIDTH], nrm[:, HG_WIDTH:]
        hg = hg_ref[...]
        m = _mixer_out(ohg_ref[...], hg, osb_ref[...], g_hg, g_sb)
        d_hg = dcat[:, :HG_WIDTH]
        silu = hg * m["s_hg"]
        dhg_ref[...] = d_hg * (m["n_hg"] * g_hg) * (m["s_hg"] * (1.0 + hg * (1.0 - m["s_hg"])))
        dx, dw = _rms_bwd(d_hg * silu, m["n_hg"], m["r_hg"], g_hg)
        dohg_ref[...] = dx
        pack_ref[ROW_MIX_NORMS:ROW_MIX_NORMS + 1, :HG_WIDTH] += _colsum(dw)
        dx, dw = _rms_bwd(dcat[:, HG_WIDTH:], m["n_sb"], m["r_sb"], g_sb)
        dosb_ref[...] = dx
        pack_ref[ROW_MIX_NORMS:ROW_MIX_NORMS + 1, HG_WIDTH:] += _colsum(dw)

    return _pallas(
        body, comm=comm, edge=_grid_edge(T // ROW_TILE), name="mix_out_bwd", grid=(T // ROW_TILE,),
        in_specs=[_row_spec(D_MODEL), _row_spec(D_MODEL), _row_spec(HG_WIDTH), _row_spec(HG_WIDTH, 3), _row_spec(SB_WIDTH),
                  _full_spec((1, D_MODEL)), _full_spec((1, D_MODEL)), ANY_SPEC],
        out_specs=[_row_spec(D_MODEL), _row_spec(HG_WIDTH), _row_spec(HG_WIDTH), _row_spec(SB_WIDTH), PACK_SPEC],
        out_shape=[jax.ShapeDtypeStruct((T, D_MODEL), BF16), jax.ShapeDtypeStruct((T, HG_WIDTH), F32),
                   jax.ShapeDtypeStruct((T, HG_WIDTH), F32), jax.ShapeDtypeStruct((T, SB_WIDTH), F32),
                   jax.ShapeDtypeStruct((8, D_MODEL), F32)],
        scratch_shapes=[pltpu.VMEM(w_out.shape, BF16)],
        compiler_params=_params(("arbitrary",)),
    )(dh1, mix, o_hg, proj_h, o_sb, norms, g_post, w_out)


def in_proj_bwd(parts, x, dh1, g_pre, w_in, comm=None):
    T = x.shape[0]
    pw = w_in.shape[2]
    n_parts = len(parts)

    def body(*refs):
        part_refs = refs[:n_parts]
        x_ref, dh1_ref, g_ref, w_hbm, dproj_ref, dx_ref, pack_ref, w_vmem = refs[n_parts:]
        _load_once(w_hbm, w_vmem)
        _zero_first(pack_ref)
        for n, ref in enumerate(part_refs):
            dproj_ref[:, HG_WIDTH * n:HG_WIDTH * (n + 1)] = ref[...].astype(BF16)
        du = _dot_nt(dproj_ref[:, :pw], w_vmem[0])
        for q in range(1, N_CHIPS):
            du = du + _dot_nt(dproj_ref[:, pw * q:pw * (q + 1)], w_vmem[q])
        xh, r = _rms(x_ref[...])
        dx, dw = _rms_bwd(du, xh, r, g_ref[...])
        pack_ref[ROW_ATTN_PRE:ROW_ATTN_PRE + 1, :] += _colsum(dw)
        dx_ref[...] = dh1_ref[...] + dx

    return _pallas(
        body, comm=comm, edge=_grid_edge(T // ROW_TILE), name="in_proj_bwd", grid=(T // ROW_TILE,),
        in_specs=[_row_spec(HG_WIDTH)] * n_parts + [_row_spec(D_MODEL), _row_spec(D_MODEL), _full_spec((1, D_MODEL)), ANY_SPEC],
        out_specs=[_row_spec(n_parts * HG_WIDTH), _row_spec(D_MODEL), PACK_SPEC],
        out_shape=[jax.ShapeDtypeStruct((T, n_parts * HG_WIDTH), BF16), jax.ShapeDtypeStruct((T, D_MODEL), F32),
                   jax.ShapeDtypeStruct((8, D_MODEL), F32)],
        scratch_shapes=[pltpu.VMEM(w_in.shape, BF16)],
        compiler_params=_params(("arbitrary",)),
    )(*parts, x, dh1, g_pre, w_in)


def weight_grad(a, g, name, *, tm, tn, tk=512, col_pieces=False):
    T, M = a.shape
    N = g.shape[1]
    tk = min(tk, T)
    steps = T // tk

    def body(a_ref, g_ref, o_ref):
        @pl.when(pl.program_id(2) == 0)
        def _():
            o_ref[...] = jnp.zeros(o_ref.shape, F32)

        o_ref[...] += _dot_tn(a_ref[...], g_ref[...]).reshape(o_ref.shape)

    if col_pieces:
        out_shape = jax.ShapeDtypeStruct((N // tn, M, tn), F32)
        out_spec = pl.BlockSpec((1, tm, tn), lambda i, j, k: (j, i, 0))
    else:
        out_shape = jax.ShapeDtypeStruct((M, N), F32)
        out_spec = pl.BlockSpec((tm, tn), lambda i, j, k: (i, j))
    return pl.pallas_call(
        body, name=name, grid=(M // tm, N // tn, steps),
        in_specs=[pl.BlockSpec((tk, tm), lambda i, j, k: (k, i)), pl.BlockSpec((tk, tn), lambda i, j, k: (k, j))],
        out_specs=out_spec, out_shape=out_shape,
        compiler_params=_params(("arbitrary", "arbitrary", "arbitrary")),
    )(a, g)


def _place():
    x, y, c = lax.axis_index("x"), lax.axis_index("y"), lax.axis_index("c")
    chips = [(1 - x, y), (x, 1 - y), (1 - x, 1 - y)]
    return x, y, c, chips


def _chip_index(cx, cy):
    return 2 * cx + cy


def _rcopy(src, dst, send_sem, recv_sem, device):
    return pltpu.make_async_remote_copy(src_ref=src, dst_ref=dst, send_sem=send_sem, recv_sem=recv_sem,
                                        device_id=device, device_id_type=MESH)


def gather_weights(shards):
    n = len(shards)

    def body(*refs):
        ins, outs = refs[:n], refs[2 * n:3 * n]
        send_sems, recv_sems = refs[3 * n:]
        x, y, c, chips = _place()
        me = _chip_index(x, y)
        sibling = (x, y, 1 - c)

        def rows(w, core):
            half = ins[w].shape[0] // 2
            return pl.ds(core * half, half)

        sends = []
        for w in range(n):
            for j, chip in enumerate(chips):
                sends.append(_rcopy(ins[w].at[rows(w, c)], outs[w].at[me, rows(w, c)],
                                    send_sems.at[6 * w + j], recv_sems.at[6 * w + j], (*chip, c)))
        for cp in sends:
            cp.start()
        passed = []
        for w in range(n):
            for j, chip in enumerate(chips):
                block = outs[w].at[_chip_index(*chip), rows(w, c)]
                _rcopy(block, block, send_sems.at[6 * w + j], recv_sems.at[6 * w + j], (*chip, c)).wait_recv()
                cp = _rcopy(block, block, send_sems.at[6 * w + 3 + j], recv_sems.at[6 * w + 3 + j], sibling)
                cp.start()
                passed.append(cp)
        for w in range(n):
            for j, chip in enumerate(chips):
                block = outs[w].at[_chip_index(*chip), rows(w, 1 - c)]
                _rcopy(block, block, send_sems.at[6 * w + 3 + j], recv_sems.at[6 * w + 3 + j], sibling).wait_recv()
        for cp in sends + passed:
            cp.wait_send()

    filled = [jnp.broadcast_to(s[None], (N_CHIPS,) + s.shape) for s in shards]
    return pl.pallas_call(
        body, name="gather_weights",
        in_specs=[ANY_SPEC] * (2 * n), out_specs=[ANY_SPEC] * n,
        out_shape=[jax.ShapeDtypeStruct(f.shape, f.dtype) for f in filled],
        input_output_aliases={n + w: w for w in range(n)},
        scratch_shapes=[pltpu.SemaphoreType.DMA((6 * n,)), pltpu.SemaphoreType.DMA((6 * n,))],
    )(*shards, *filled)


def _run_comm(comm, name):
    c_in, c_out = len(comm.inputs), len(comm.out_shape)

    def body(*refs):
        parts = refs[:c_in], refs[c_in:c_in + c_out], refs[c_in + c_out:]
        comm.start(*parts)
        comm.finish(*parts)

    return pl.pallas_call(
        body, name=name, in_specs=[ANY_SPEC] * c_in, out_specs=[ANY_SPEC] * c_out, out_shape=comm.out_shape,
        scratch_shapes=comm.scratch, input_output_aliases=comm.aliases)(*comm.inputs)


def _both(first, second):
    n_in, n_out, n_scr = len(first.inputs), len(first.out_shape), len(first.scratch)

    def split(ins, outs, scr):
        return (ins[:n_in], outs[:n_out], scr[:n_scr]), (ins[n_in:], outs[n_out:], scr[n_scr:])

    def start(*refs):
        a, b = split(*refs)
        first.start(*a)
        second.start(*b)

    def finish(*refs):
        a, b = split(*refs)
        first.finish(*a)
        second.finish(*b)

    aliases = dict(first.aliases)
    aliases.update({n_in + i: n_out + o for i, o in second.aliases.items()})
    return Comm(first.inputs + second.inputs, first.out_shape + second.out_shape, aliases,
                first.scratch + second.scratch, start, finish)


def _dma_sems(count):
    return [pltpu.SemaphoreType.DMA((count,)), pltpu.SemaphoreType.DMA((count,))]


def gather_over_ici(shards):
    n = len(shards)

    def copies(ins, outs, sems):
        send_sems, recv_sems = sems
        x, y, c, chips = _place()
        me = _chip_index(x, y)
        pairs = []
        for w in range(n):
            half = shards[w].shape[0] // 2
            rows = pl.ds(c * half, half)
            for j, chip in enumerate(chips):
                k = 3 * w + j
                landed = outs[w].at[_chip_index(*chip), rows]
                pairs.append((_rcopy(ins[w].at[rows], outs[w].at[me, rows], send_sems.at[k], recv_sems.at[k], (*chip, c)),
                              _rcopy(landed, landed, send_sems.at[k], recv_sems.at[k], (*chip, c))))
        return pairs

    def start(*refs):
        for send, _ in copies(*refs):
            send.start()

    def finish(*refs):
        pairs = copies(*refs)
        for _, landed in pairs:
            landed.wait_recv()
        for send, _ in pairs:
            send.wait_send()

    filled = [jnp.broadcast_to(s[None], (N_CHIPS,) + s.shape) for s in shards]
    return Comm(list(shards) + filled, [jax.ShapeDtypeStruct(f.shape, f.dtype) for f in filled],
                {n + w: w for w in range(n)}, _dma_sems(3 * n), start, finish)


def gather_over_d2d(landed):
    n = len(landed)

    def copies(ins, outs, sems):
        send_sems, recv_sems = sems
        x, y, c, chips = _place()
        sibling = (x, y, 1 - c)
        pairs = []
        for w in range(n):
            half = landed[w].shape[1] // 2
            for j, chip in enumerate(chips):
                k = 3 * w + j
                mine = outs[w].at[_chip_index(*chip), pl.ds(c * half, half)]
                theirs = outs[w].at[_chip_index(*chip), pl.ds((1 - c) * half, half)]
                pairs.append((_rcopy(mine, mine, send_sems.at[k], recv_sems.at[k], sibling),
                              _rcopy(theirs, theirs, send_sems.at[k], recv_sems.at[k], sibling)))
        return pairs

    def start(*refs):
        for send, _ in copies(*refs):
            send.start()

    def finish(*refs):
        pairs = copies(*refs)
        for _, arrived in pairs:
            arrived.wait_recv()
        for send, _ in pairs:
            send.wait_send()

    return Comm(list(landed), [jax.ShapeDtypeStruct(a.shape, a.dtype) for a in landed], {w: w for w in range(n)},
                _dma_sems(3 * n), start, finish)


def core_halves(grads):
    n = len(grads)

    def copies(ins, outs, sems):
        send_sems, recv_sems = sems
        x, y, c, _ = _place()
        out = []
        for w in range(n):
            half = grads[w].shape[1] // 2
            out.append(_rcopy(ins[w].at[:, pl.ds((1 - c) * half, half), :], outs[w],
                              send_sems.at[w], recv_sems.at[w], (x, y, 1 - c)))
        return out

    def start(*refs):
        for cp in copies(*refs):
            cp.start()

    def finish(*refs):
        for cp in copies(*refs):
            cp.wait()

    return Comm(list(grads), [jax.ShapeDtypeStruct((g.shape[0], g.shape[1] // 2, g.shape[2]), g.dtype) for g in grads],
                {}, _dma_sems(n), start, finish)


def chip_partials(partials):
    n = len(partials)

    def copies(ins, outs, sems):
        send_sems, recv_sems = sems
        x, y, c, chips = _place()
        me = _chip_index(x, y)
        pairs = []
        for w in range(n):
            for j, chip in enumerate(chips):
                k = 3 * w + j
                landed = outs[w].at[_chip_index(*chip)]
                pairs.append((_rcopy(ins[w].at[_chip_index(*chip)], outs[w].at[me], send_sems.at[k], recv_sems.at[k],
                                     (*chip, c)),
                              _rcopy(landed, landed, send_sems.at[k], recv_sems.at[k], (*chip, c))))
        return pairs

    def start(*refs):
        for send, _ in copies(*refs):
            send.start()

    def finish(*refs):
        pairs = copies(*refs)
        for _, landed in pairs:
            landed.wait_recv()
        for send, _ in pairs:
            send.wait_send()

    me = _chip_index(lax.axis_index("x"), lax.axis_index("y"))
    filled = [jnp.broadcast_to(lax.dynamic_index_in_dim(p, me, 0, keepdims=True), p.shape) for p in partials]
    return Comm(list(partials) + filled, [jax.ShapeDtypeStruct(p.shape, p.dtype) for p in partials],
                {n + w: w for w in range(n)}, _dma_sems(3 * n), start, finish)


def join_core_halves(grads):
    n = len(grads)

    def body(*refs):
        outs = refs[n:2 * n]
        send_sems, recv_sems = refs[2 * n:]
        x, y, c, _ = _place()
        sibling = (x, y, 1 - c)
        copies = []
        for w in range(n):
            half = outs[w].shape[0] // 2
            mine = outs[w].at[pl.ds(c * half, half), :]
            copies.append(_rcopy(mine, mine, send_sems.at[w], recv_sems.at[w], sibling))
        for cp in copies:
            cp.start()
        for w in range(n):
            half = outs[w].shape[0] // 2
            theirs = outs[w].at[pl.ds((1 - c) * half, half), :]
            _rcopy(theirs, theirs, send_sems.at[w], recv_sems.at[w], sibling).wait_recv()
        for cp in copies:
            cp.wait_send()

    return pl.pallas_call(
        body, name="join_core_halves",
        in_specs=[ANY_SPEC] * n, out_specs=[ANY_SPEC] * n,
        out_shape=[jax.ShapeDtypeStruct(g.shape, g.dtype) for g in grads],
        input_output_aliases={w: w for w in range(n)},
        scratch_shapes=[pltpu.SemaphoreType.DMA((n,)), pltpu.SemaphoreType.DMA((n,))],
    )(*grads)


def _elementwise_rows(rows, cap=512):
    for t in range(min(rows, cap), 0, -8):
        if rows % t == 0 and t % 16 == 0:
            return t
    return rows


def add_core_halves(grad, got, core, name):
    _, rows, cols = got.shape
    tr = _elementwise_rows(rows)
    nt = rows // tr

    def body(core_ref, a_ref, b_ref, o_ref):
        o_ref[...] = (a_ref[...] + b_ref[...]).astype(BF16)

    spec = pl.BlockSpec((1, tr, cols), lambda q, i, core_ref: (q, i, 0))
    own = pl.BlockSpec((1, tr, cols), lambda q, i, core_ref: (q, core_ref[0] * nt + i, 0))
    return pl.pallas_call(
        body, name=name,
        grid_spec=pltpu.PrefetchScalarGridSpec(num_scalar_prefetch=1, grid=(N_CHIPS, nt), in_specs=[own, spec],
                                               out_specs=spec),
        out_shape=jax.ShapeDtypeStruct(got.shape, BF16),
        compiler_params=_params(("arbitrary", "arbitrary")),
    )(core, grad, got)


def add_chip_partials(parts, core, name):
    _, rows, cols = parts.shape
    tr = _elementwise_rows(rows)
    nt = rows // tr

    def body(core_ref, p_ref, o_ref):
        acc = p_ref[0].astype(F32)
        for q in range(1, N_CHIPS):
            acc = acc + p_ref[q].astype(F32)
        o_ref[...] = acc

    return pl.pallas_call(
        body, name=name,
        grid_spec=pltpu.PrefetchScalarGridSpec(
            num_scalar_prefetch=1, grid=(nt,),
            in_specs=[pl.BlockSpec((N_CHIPS, tr, cols), lambda i, core_ref: (0, i, 0))],
            out_specs=pl.BlockSpec((tr, cols), lambda i, core_ref: (core_ref[0] * nt + i, 0))),
        out_shape=jax.ShapeDtypeStruct((2 * rows, cols), F32),
        compiler_params=_params(("arbitrary",)),
    )(core, parts)


def _adamw_math(w, g, m, v):
    m = ADAM_B1 * m + (1.0 - ADAM_B1) * g
    v = ADAM_B2 * v + (1.0 - ADAM_B2) * (g * g)
    m_hat = m / (1.0 - ADAM_B1 ** ADAM_STEP)
    v_hat = v / (1.0 - ADAM_B2 ** ADAM_STEP)
    delta = -ADAM_LR * (m_hat / (jnp.sqrt(v_hat) + ADAM_EPS) + ADAM_WD * w)
    return delta, m, v


def adamw(w, g, m, v, name):
    rows, cols = w.shape
    tr = _elementwise_rows(rows, 256)

    def body(w_ref, g_ref, m_ref, v_ref, d_ref, nm_ref, nv_ref):
        d, nm, nv = _adamw_math(w_ref[...], g_ref[...], m_ref[...], v_ref[...])
        d_ref[...] = d
        nm_ref[...] = nm
        nv_ref[...] = nv

    spec = pl.BlockSpec((tr, cols), lambda i: (i, 0))
    return pl.pallas_call(
        body, name=name, grid=(rows // tr,), in_specs=[spec] * 4, out_specs=[spec] * 3,
        out_shape=[jax.ShapeDtypeStruct((rows, cols), F32)] * 3,
        compiler_params=_params(("arbitrary",)),
    )(w, g, m, v)


def reduce_small(packs, w, m, v):
    n = len(packs)
    n_dev = 8
    flips = [(fx, fy, fc) for fx in (0, 1) for fy in (0, 1) for fc in (0, 1)][1:]

    def body(*refs):
        pack_refs = refs[:n]
        w_ref, m_ref, v_ref, g_out, d_out, m_out, v_out, mine, slots, send_sems, recv_sems = refs[n:]
        x, y, c, _ = _place()
        me = 4 * x + 2 * y + c
        acc = pack_refs[0][...]
        for ref in pack_refs[1:]:
            acc = acc + ref[...]
        mine[...] = acc
        sends = []
        for k, (fx, fy, fc) in enumerate(flips):
            peer = (x ^ fx, y ^ fy, c ^ fc)
            sends.append(_rcopy(mine, slots.at[me], send_sems.at[k], recv_sems.at[me], peer))
        for cp in sends:
            cp.start()
        slots[me] = acc
        for fx, fy, fc in flips:
            src = 4 * (x ^ fx) + 2 * (y ^ fy) + (c ^ fc)
            _rcopy(mine, slots.at[src], send_sems.at[0], recv_sems.at[src], (x, y, c)).wait_recv()
        for cp in sends:
            cp.wait_send()
        total = slots[0]
        for d in range(1, n_dev):
            total = total + slots[d]
        g_out[...] = total
        d, nm, nv = _adamw_math(w_ref[...], total, m_ref[...], v_ref[...])
        d_out[...] = d
        m_out[...] = nm
        v_out[...] = nv

    vm = pl.BlockSpec(memory_space=pltpu.VMEM)
    return pl.pallas_call(
        body, name="reduce_small",
        in_specs=[vm] * (n + 3), out_specs=[vm] * 4,
        out_shape=[jax.ShapeDtypeStruct((8, D_MODEL), F32)] * 4,
        scratch_shapes=[pltpu.VMEM((8, D_MODEL), F32), pltpu.VMEM((n_dev, 8, D_MODEL), F32),
                        pltpu.SemaphoreType.DMA((len(flips),)), pltpu.SemaphoreType.DMA((n_dev,))],
    )(*packs, w, m, v)


def _pack_small(attn_pre, gamma, hg_norm, sb_norm, attn_post, ffn_pre, ffn_post):
    rows = [attn_pre, gamma.reshape(1, D_MODEL), jnp.concatenate([hg_norm, sb_norm], axis=1), attn_post, ffn_pre, ffn_post,
            jnp.zeros((2, D_MODEL), F32)]
    return jnp.concatenate(rows, axis=0)


def _unpack_small(pack):
    return (pack[ROW_ATTN_PRE:ROW_ATTN_PRE + 1], pack[ROW_GAMMA].reshape(2, HG_WIDTH),
            pack[ROW_MIX_NORMS:ROW_MIX_NORMS + 1, :HG_WIDTH], pack[ROW_MIX_NORMS:ROW_MIX_NORMS + 1, HG_WIDTH:],
            pack[ROW_ATTN_POST:ROW_ATTN_POST + 1], pack[ROW_FFN_PRE:ROW_FFN_PRE + 1], pack[ROW_FFN_POST:ROW_FFN_POST + 1])


def kernel(x, p, attn_pre_norm, w_in, hg_lower_gamma, hg_out_norm, sb_out_norm, w_out, attn_post_norm, ffn_pre_norm, w_gate_up, w_down, ffn_post_norm, ple_proj, ple_gate, loss_target, m_attn_pre_norm, m_w_in, m_hg_lower_gamma, m_hg_out_norm, m_sb_out_norm, m_w_out, m_attn_post_norm, m_ffn_pre_norm, m_w_gate_up, m_w_down, m_ffn_post_norm, m_ple_proj, m_ple_gate, v_attn_pre_norm, v_w_in, v_hg_lower_gamma, v_hg_out_norm, v_sb_out_norm, v_w_out, v_attn_post_norm, v_ffn_pre_norm, v_w_gate_up, v_w_down, v_ffn_post_norm, v_ple_proj, v_ple_gate):
    x2 = x[0]
    p2 = p[0, 0]
    target = loss_target[0]
    big = dict(w_in=(w_in, m_w_in, v_w_in), w_out=(w_out, m_w_out, v_w_out), w_gate_up=(w_gate_up, m_w_gate_up, v_w_gate_up),
               w_down=(w_down, m_w_down, v_w_down), ple_proj=(ple_proj, m_ple_proj, v_ple_proj),
               ple_gate=(ple_gate, m_ple_gate, v_ple_gate))
    names = list(big)
    big = {k: tuple(a[0] for a in t) for k, t in big.items()}

    shard16 = {k: big[k][0].astype(BF16) for k in names}
    w_in_full, = gather_weights([shard16["w_in"]])
    mix_norms = jnp.concatenate([hg_out_norm, sb_out_norm], axis=1)
    small_ones = ["w_out", "ple_proj", "ple_gate"]

    proj_h, sqkv, u1, *landed_small = in_proj_fwd(
        x2, attn_pre_norm, w_in_full, comm=gather_over_ici([shard16[k] for k in small_ones]))
    o_sb, sb_totals, sb_first, landed_gu = sb_fwd(sqkv, comm=gather_over_ici([shard16["w_gate_up"]]))
    o_hg, states, landed_down, *full_small = hgrn2_fwd(
        proj_h, hg_lower_gamma, comm=_both(gather_over_ici([shard16["w_down"]]), gather_over_d2d(landed_small)))
    full = dict(zip(small_ones, full_small), w_in=w_in_full)
    w_out_full = full["w_out"].reshape(D_MODEL, D_MODEL)
    w_pg_full = full["ple_gate"].reshape(D_MODEL, D_MODEL)
    cat, mix, h1, full["w_gate_up"], full["w_down"] = mix_out_fwd(
        o_hg, proj_h, o_sb, x2, mix_norms, attn_post_norm, w_out_full, comm=gather_over_d2d([landed_gu, landed_down]))
    w_down_full = full["w_down"].reshape(D_FF, D_MODEL)
    u2, gu, act, y, h2 = ffn_fwd(h1, ffn_pre_norm, ffn_post_norm, full["w_gate_up"], w_down_full)

    core = lax.axis_index("c").astype(jnp.int32).reshape(1)
    de, ds, dh2, h2b, pb, pack_loss = ple_loss(h2, p2, target, full["ple_proj"], w_pg_full)
    dy, dgu, dh1, pack_ffn = ffn_bwd(dh2, y, h1, gu, ffn_pre_norm, ffn_post_norm, full["w_gate_up"], w_down_full)
    local = dict(
        w_gate_up=weight_grad(u2, dgu, "grad_w_gate_up", tm=D_MODEL, tn=full["w_gate_up"].shape[2], tk=1024, col_pieces=True),
        w_down=weight_grad(act, dy, "grad_w_down", tm=D_FF // 2, tn=D_MODEL, tk=1024).reshape(full["w_down"].shape),
        ple_proj=weight_grad(pb, de, "grad_ple_proj", tm=pb.shape[1], tn=full["ple_proj"].shape[2], col_pieces=True),
        ple_gate=weight_grad(h2b, ds, "grad_ple_gate", tm=D_MODEL, tn=D_MODEL).reshape(full["ple_gate"].shape),
    )
    early = list(local)
    dmix, do_hg, dhg, do_sb, pack_mix, *got = mix_out_bwd(
        dh1, mix, o_hg, proj_h, o_sb, mix_norms, attn_post_norm, w_out_full, comm=core_halves([local[k] for k in early]))
    partial = [add_core_halves(local[k], g, core, "add_core_halves_" + k) for k, g in zip(early, got)]
    dsq, dsk, dsv = sb_bwd(sqkv, do_sb, sb_totals, sb_first)
    dhq, dhf, dhi, pack_hg, *by_source = hgrn2_bwd(proj_h, hg_lower_gamma, states, do_hg, comm=chip_partials(partial[:1]))
    dproj, grad_x, pack_in, *more = in_proj_bwd(
        [dhq, dhf, dhi, dhg, dsq, dsk, dsv], x2, dh1, attn_pre_norm, full["w_in"], comm=chip_partials(partial[1:]))
    halves = {k: add_chip_partials(s, core, "add_chip_partials_" + k) for k, s in zip(early, by_source + more)}

    late = ["w_in", "w_out"]
    local["w_in"] = weight_grad(u1, dproj, "grad_w_in", tm=D_MODEL, tn=full["w_in"].shape[2], tk=1024, col_pieces=True)
    local["w_out"] = weight_grad(cat, dmix, "grad_w_out", tm=D_MODEL, tn=D_MODEL).reshape(full["w_out"].shape)
    got = _run_comm(core_halves([local[k] for k in late]), "exchange_core_halves")
    partial = [add_core_halves(local[k], g, core, "add_core_halves_" + k) for k, g in zip(late, got)]
    by_source = _run_comm(chip_partials(partial), "exchange_chip_partials")
    halves.update({k: add_chip_partials(s, core, "add_chip_partials_" + k) for k, s in zip(late, by_source)})
    grads = dict(zip(names, join_core_halves([halves[k] for k in names])))

    upd = {k: adamw(big[k][0], grads[k], big[k][1], big[k][2], "adamw_" + k) for k in names}

    small = reduce_small(
        [pack_loss, pack_ffn, pack_mix, pack_hg, pack_in],
        _pack_small(attn_pre_norm, hg_lower_gamma, hg_out_norm, sb_out_norm, attn_post_norm, ffn_pre_norm, ffn_post_norm),
        _pack_small(m_attn_pre_norm, m_hg_lower_gamma, m_hg_out_norm, m_sb_out_norm, m_attn_post_norm, m_ffn_pre_norm, m_ffn_post_norm),
        _pack_small(v_attn_pre_norm, v_hg_lower_gamma, v_hg_out_norm, v_sb_out_norm, v_attn_post_norm, v_ffn_pre_norm, v_ffn_post_norm),
    )
    loss = small[0][ROW_LOSS, 0]
    s_grad, s_delta, s_m, s_v = (_unpack_small(t) for t in small)

    def ordered(small_vals, big_vals):
        a_pre, gam, hg_n, sb_n, a_post, f_pre, f_post = small_vals
        b = {k: big_vals[k][None] for k in names}
        return (a_pre, b["w_in"], gam, hg_n, sb_n, b["w_out"], a_post, f_pre, b["w_gate_up"], b["w_down"], f_post,
                b["ple_proj"], b["ple_gate"])

    return (loss, grad_x[None],
            *ordered(s_grad, grads),
            *ordered(s_delta, {k: upd[k][0] for k in names}),
            *ordered(s_m, {k: upd[k][1] for k in names}),
            *ordered(s_v, {k: upd[k][2] for k in names}))
```

```python
from typing import Callable, NamedTuple

import numpy as np
import jax
import jax.numpy as jnp
from jax import lax
from jax.experimental import pallas as pl
from jax.experimental.pallas import tpu as pltpu

F32 = jnp.float32
BF16 = jnp.bfloat16
MESH = pl.DeviceIdType.MESH

RMS_EPS = 1e-6
D_MODEL = 1024
HG_WIDTH = 512
HG_HEADS = 4
HG_DK = 128
HG_CHUNK = 64
HG_LEVELS = (32, 16, 8, 4, 2, 1)
HG_CHUNKS_PER_STEP = 2
SB_WIDTH = 512
SB_BLOCK = 128
SB_DH = 64
SB_SCALE = SB_DH ** -0.5
SB_UNDERFLOW_LOG = -87.5
SB_UNROLL = 2
SB_GROUP = 2
D_FF = 2816
N_CHIPS = 4
ROW_TILE = 256
V7X_VMEM_LIMIT = 56 * 1024 * 1024

ADAM_LR = 0.001
ADAM_B1 = 0.9
ADAM_B2 = 0.999
ADAM_EPS = 1e-08
ADAM_WD = 0.01
ADAM_STEP = 10

ROW_ATTN_PRE, ROW_GAMMA, ROW_MIX_NORMS, ROW_ATTN_POST, ROW_FFN_PRE, ROW_FFN_POST, ROW_LOSS = range(7)


def _params(sem=None, vmem=V7X_VMEM_LIMIT):
    return pltpu.CompilerParams(dimension_semantics=sem, vmem_limit_bytes=vmem)


def _dot(a, b):
    return jnp.dot(a.astype(BF16), b.astype(BF16), preferred_element_type=F32)


def _dot_nt(a, b):
    return lax.dot_general(a.astype(BF16), b.astype(BF16), (((1,), (1,)), ((), ())), preferred_element_type=F32)


def _dot_tn(a, b):
    return lax.dot_general(a.astype(BF16), b.astype(BF16), (((0,), (0,)), ((), ())), preferred_element_type=F32)


def _split(x):
    hi = x.astype(BF16)
    lo = (x - hi.astype(F32)).astype(BF16)
    return hi, lo


def _sum01_left(m01, x):
    hi, lo = _split(x)
    return jnp.dot(m01, hi, preferred_element_type=F32) + jnp.dot(m01, lo, preferred_element_type=F32)


def _sum01_right(x, m01_twice):
    hi, lo = _split(x)
    return jnp.dot(jnp.concatenate([hi, lo], axis=1), m01_twice, preferred_element_type=F32)


def _rms(x):
    r = lax.rsqrt(jnp.mean(x * x, axis=-1, keepdims=True) + RMS_EPS)
    return x * r, r


def _rms_bwd(dy, xhat, r, w):
    dxh = dy * w
    dx = r * (dxh - xhat * jnp.mean(dxh * xhat, axis=-1, keepdims=True))
    return dx, dy * xhat


def _sigmoid(x):
    return 1.0 / (1.0 + jnp.exp(-x))


def _neg_softplus(z):
    return -(jnp.maximum(z, 0.0) + jnp.log(1.0 + jnp.exp(-jnp.abs(z))))


def _colsum(x):
    return jnp.sum(x, axis=0, keepdims=True)


def _load_once(src_hbm, dst_vmem):
    @pl.when(pl.program_id(0) == 0)
    def _():
        pltpu.sync_copy(src_hbm, dst_vmem)


def _zero_first(ref):
    @pl.when(pl.program_id(0) == 0)
    def _():
        ref[...] = jnp.zeros(ref.shape, ref.dtype)


def _row_spec(width, col=0):
    return pl.BlockSpec((ROW_TILE, width), lambda i, col=col: (i, col))


def _full_spec(shape):
    return pl.BlockSpec(shape, lambda *_: (0,) * len(shape))


ANY_SPEC = pl.BlockSpec(memory_space=pl.ANY)
PACK_SPEC = _full_spec((8, D_MODEL))


class Comm(NamedTuple):
    inputs: list
    out_shape: list
    aliases: dict
    scratch: list
    start: Callable
    finish: Callable


def _pallas(body, *, comm=None, edge=None, in_specs, out_specs, out_shape, scratch_shapes=(), **kw):
    if comm is None:
        return pl.pallas_call(body, in_specs=in_specs, out_specs=out_specs, out_shape=out_shape,
                              scratch_shapes=scratch_shapes, **kw)
    n_in, n_out, n_scr = len(in_specs), len(out_specs), len(scratch_shapes)
    c_in, c_out = len(comm.inputs), len(comm.out_shape)

    def both(*refs):
        ins, c_ins = refs[:n_in], refs[n_in:n_in + c_in]
        outs = refs[n_in + c_in:n_in + c_in + n_out]
        c_outs = refs[n_in + c_in + n_out:n_in + c_in + n_out + c_out]
        rest = refs[n_in + c_in + n_out + c_out:]
        scr, c_scr = rest[:n_scr], rest[n_scr:]
        first, last = edge()

        @pl.when(first)
        def _():
            comm.start(c_ins, c_outs, c_scr)

        body(*ins, *outs, *scr)

        @pl.when(last)
        def _():
            comm.finish(c_ins, c_outs, c_scr)

    call = pl.pallas_call(
        both, in_specs=list(in_specs) + [ANY_SPEC] * c_in, out_specs=list(out_specs) + [ANY_SPEC] * c_out,
        out_shape=list(out_shape) + list(comm.out_shape), scratch_shapes=list(scratch_shapes) + list(comm.scratch),
        input_output_aliases={n_in + a: n_out + b for a, b in comm.aliases.items()}, **kw)
    return lambda *args: call(*args, *comm.inputs)


def _grid_edge(steps):
    return lambda: (pl.program_id(0) == 0, pl.program_id(0) == steps - 1)


def in_proj_fwd(x, g_pre, w_in, comm=None):
    T = x.shape[0]
    pw = w_in.shape[2]

    def body(x_ref, g_ref, w_hbm, ph_ref, sqkv_ref, u_ref, w_vmem, proj_s):
        _load_once(w_hbm, w_vmem)
        xh, _ = _rms(x_ref[...])
        u = (xh * g_ref[...]).astype(BF16)
        u_ref[...] = u
        for q in range(N_CHIPS):
            proj_s[:, pw * q:pw * (q + 1)] = jnp.dot(u, w_vmem[q], preferred_element_type=F32)
        ph_ref[...] = proj_s[:, :4 * HG_WIDTH]
        sqkv_ref[:, :SB_WIDTH] = (proj_s[:, 4 * HG_WIDTH:4 * HG_WIDTH + SB_WIDTH] * SB_SCALE).astype(BF16)
        sqkv_ref[:, SB_WIDTH:] = proj_s[:, 4 * HG_WIDTH + SB_WIDTH:].astype(BF16)

    return _pallas(
        body, comm=comm, edge=_grid_edge(T // ROW_TILE), name="in_proj_fwd", grid=(T // ROW_TILE,),
        in_specs=[_row_spec(D_MODEL), _full_spec((1, D_MODEL)), ANY_SPEC],
        out_specs=[_row_spec(4 * HG_WIDTH), _row_spec(3 * SB_WIDTH), _row_spec(D_MODEL)],
        out_shape=[jax.ShapeDtypeStruct((T, 4 * HG_WIDTH), F32), jax.ShapeDtypeStruct((T, 3 * SB_WIDTH), BF16),
                   jax.ShapeDtypeStruct((T, D_MODEL), BF16)],
        scratch_shapes=[pltpu.VMEM(w_in.shape, BF16), pltpu.VMEM((ROW_TILE, N_CHIPS * pw), F32)],
        compiler_params=_params(("arbitrary",)),
    )(x, g_pre, w_in)


def _hg_sum_matrix():
    C = HG_CHUNK
    t = np.arange(C)[:, None]
    j = np.arange(C)[None, :]
    mats = [j <= t, j > t]
    for h in HG_LEVELS:
        start = (t // (2 * h)) * (2 * h)
        upper = (t & h) != 0
        mats.append(np.where(upper, (j >= start + h) & (j <= t), (j > t) & (j <= start + h - 1)))
    return np.concatenate(mats, 0).astype(np.float32)


def _hg_level_masks():
    C = HG_CHUNK
    t = lax.broadcasted_iota(jnp.int32, (C, C), 0)
    s = lax.broadcasted_iota(jnp.int32, (C, C), 1)
    x = t ^ s
    masks = [t == s]
    for h in HG_LEVELS:
        masks.append((x >= h) & (x < 2 * h) & (t > s))
    return masks


def _hg_gates(hq, hf, gamma):
    lb = 1.0 / (1.0 + jnp.exp(gamma[1:2, :] - gamma[0:1, :]))
    sq = _sigmoid(hq)
    q = hq * sq
    sig = _sigmoid(hf)
    nsig = _sigmoid(-hf)
    f = lb + (1.0 - lb) * sig
    k = (1.0 - lb) * nsig
    g = jnp.log(f)
    return q, k, g, dict(lb=lb, sq=sq, sig=sig, nsig=nsig, f=f)


def _hg_head_decays(A, h):
    C, K = HG_CHUNK, HG_DK
    sl = slice(K * h, K * (h + 1))
    blocks = [A[C * r:C * (r + 1), sl] for r in range(2 + len(HG_LEVELS))]
    return blocks[0], blocks[1], [None] + blocks[2:]


def _hg_products(q, k, levels):
    return [_dot_nt(q, k)] + [_dot_nt(q * a, k * a) for a in levels[1:]]


def _hg_select(prods, masks):
    sc = jnp.where(masks[0], prods[0], 0.0)
    for p, m in zip(prods[1:], masks[1:]):
        sc = jnp.where(m, p, sc)
    return sc


def hgrn2_fwd(proj_h, gamma, comm=None):
    T = proj_h.shape[0]
    C, K, H, S = HG_CHUNK, HG_DK, HG_HEADS, HG_CHUNKS_PER_STEP
    n_steps = T // (S * C)
    msum = jnp.asarray(_hg_sum_matrix(), BF16)

    def body(hq_ref, hf_ref, hi_ref, gam_ref, msum_ref, o_ref, st_ref, st_s):
        _zero_first(st_s)
        q, k, g, _ = _hg_gates(hq_ref[...], hf_ref[...], gam_ref[...])
        v = hi_ref[...]
        masks = _hg_level_masks()
        parts = []
        for s in range(S):
            rows = slice(C * s, C * (s + 1))
            A = jnp.exp(_sum01_left(msum_ref[...], g[rows]))
            for h in range(H):
                sl = slice(K * h, K * (h + 1))
                ab, ar, levels = _hg_head_decays(A, h)
                parts.append(dict(s=s, h=h, rows=rows, sl=sl, ab=ab, ar=ar, levels=levels,
                                  q=q[rows, sl], k=k[rows, sl], v=v[rows, sl]))
        for pt in parts:
            pt["prods"] = _hg_products(pt["q"], pt["k"], pt["levels"])
            pt["grown"] = _dot_tn(pt["v"], pt["k"] * pt["ar"])
        for pt in parts:
            pt["sc"] = _hg_select(pt["prods"], masks)
        state = [st_s[h] for h in range(H)]
        for pt in parts:
            h, ab = pt["h"], pt["ab"]
            o_ref[pt["rows"], pt["sl"]] = _dot_nt(pt["q"] * ab, state[h]) + _dot(pt["sc"], pt["v"])
            state[h] = state[h] * ab[C - 1:C, :] + pt["grown"]
            st_ref[pt["s"], h] = state[h]
        for h in range(H):
            st_s[h] = state[h]

    blk = lambda col: pl.BlockSpec((S * C, HG_WIDTH), lambda c, col=col: (c, col))
    return _pallas(
        body, comm=comm, edge=_grid_edge(n_steps), name="hgrn2_fwd", grid=(n_steps,),
        in_specs=[blk(0), blk(1), blk(2), _full_spec((2, HG_WIDTH)), _full_spec(msum.shape)],
        out_specs=[blk(0), pl.BlockSpec((S, H, K, K), lambda c: (c, 0, 0, 0))],
        out_shape=[jax.ShapeDtypeStruct((T, HG_WIDTH), F32), jax.ShapeDtypeStruct((S * n_steps, H, K, K), F32)],
        scratch_shapes=[pltpu.VMEM((H, K, K), F32)],
        compiler_params=_params(("arbitrary",)),
    )(proj_h, proj_h, proj_h, gamma, msum)


def hgrn2_bwd(proj_h, gamma, states, do, comm=None):
    T = proj_h.shape[0]
    C, K, H, S = HG_CHUNK, HG_DK, HG_HEADS, HG_CHUNKS_PER_STEP
    n_steps = T // (S * C)
    n_sums = 2 + len(HG_LEVELS)
    msum = jnp.asarray(_hg_sum_matrix(), BF16)
    msum_t = jnp.asarray(_hg_sum_matrix().T, BF16)

    def body(hq_ref, hf_ref, hi_ref, do_ref, gam_ref, msum_ref, msum_t_ref, st_prev_ref, st_ref,
             dhq_ref, dhf_ref, dhi_ref, pack_ref, dst_s, dlb_s, dq_s, dk_s, de_s):
        step = pl.program_id(0)
        _zero_first(dst_s)
        _zero_first(dlb_s)
        _zero_first(pack_ref)
        hq = hq_ref[...]
        q, k, g, aux = _hg_gates(hq, hf_ref[...], gam_ref[...])
        v = hi_ref[...]
        do_all = do_ref[...]
        masks = _hg_level_masks()
        is_last_row = lax.broadcasted_iota(jnp.int32, (C, K), 0) == C - 1
        has_prev = (step < n_steps - 1).astype(F32)
        parts = []
        for s in reversed(range(S)):
            rows = slice(C * s, C * (s + 1))
            A = jnp.exp(_sum01_left(msum_ref[...], g[rows]))
            for h in range(H):
                sl = slice(K * h, K * (h + 1))
                ab, ar, levels = _hg_head_decays(A, h)
                st_in = st_prev_ref[0, h] * has_prev if s == 0 else st_ref[s - 1, h]
                parts.append(dict(s=s, h=h, rows=rows, sl=sl, ab=ab, ar=ar, levels=levels, st_in=st_in,
                                  q=q[rows, sl], k=k[rows, sl], v=v[rows, sl], do=do_all[rows, sl]))
        for pt in parts:
            pt["prods"] = _hg_products(pt["q"], pt["k"], pt["levels"])
            pt["da"] = _dot_nt(pt["do"], pt["v"])
            pt["t1"] = pt["ab"] * _dot(pt["do"], pt["st_in"])
            pt["dst_add"] = _dot_tn(pt["do"], pt["q"] * pt["ab"])
        dstate = [dst_s[h] for h in range(H)]
        for pt in parts:
            h = pt["h"]
            pt["dst_out"] = dstate[h]
            pt["t2"] = pt["ar"] * _dot(pt["v"], dstate[h])
            pt["dv_state"] = _dot_nt(pt["k"] * pt["ar"], dstate[h])
            dstate[h] = dstate[h] * pt["ab"][C - 1:C, :] + pt["dst_add"]
        for h in range(H):
            dst_s[h] = dstate[h]
        for pt in parts:
            pt["sc"] = _hg_select(pt["prods"], masks)
            pt["dam"] = [jnp.where(m, pt["da"], 0.0) for m in masks]
        for pt in parts:
            qh, kh = pt["q"], pt["k"]
            pt["dq_parts"] = [_dot(pt["dam"][0], kh)] + [
                a * _dot(dam, kh * a) for a, dam in zip(pt["levels"][1:], pt["dam"][1:])]
            pt["dk_parts"] = [_dot_tn(pt["dam"][0], qh)] + [
                a * _dot_tn(dam, qh * a) for a, dam in zip(pt["levels"][1:], pt["dam"][1:])]
            pt["dv_intra"] = _dot_tn(pt["sc"], pt["do"])
        for pt in parts:
            s, rows, sl, qh, kh, ab = pt["s"], pt["rows"], pt["sl"], pt["q"], pt["k"], pt["ab"]
            decayed = _colsum(pt["st_in"] * pt["dst_out"]) * ab[C - 1:C, :]
            de_s[s, 0:C, sl] = qh * pt["t1"] + jnp.where(is_last_row, decayed, 0.0)
            de_s[s, C:2 * C, sl] = kh * pt["t2"]
            dq = pt["t1"] + pt["dq_parts"][0]
            dk = pt["t2"] + pt["dk_parts"][0]
            for r, (t1, t2) in enumerate(zip(pt["dq_parts"][1:], pt["dk_parts"][1:])):
                dq = dq + t1
                dk = dk + t2
                de_s[s, C * (r + 2):C * (r + 3), sl] = qh * t1 + kh * t2
            dhi_ref[rows, sl] = pt["dv_intra"] + pt["dv_state"]
            dq_s[rows, sl] = dq
            dk_s[rows, sl] = dk
        dg = jnp.concatenate([_sum01_left(msum_t_ref[...], de_s[s]) for s in range(S)], axis=0)
        dk = dk_s[...]
        sq, lb = aux["sq"], aux["lb"]
        dhq_ref[...] = dq_s[...] * (sq * (1.0 + hq * (1.0 - sq)))
        common = dg / aux["f"] - dk
        dhf_ref[...] = (1.0 - lb) * aux["sig"] * aux["nsig"] * common
        dlb_s[...] += _colsum(aux["nsig"] * common)

        @pl.when(step == n_steps - 1)
        def _():
            dgam = lb * (1.0 - lb) * dlb_s[...]
            pack_ref[ROW_GAMMA:ROW_GAMMA + 1, :HG_WIDTH] = dgam
            pack_ref[ROW_GAMMA:ROW_GAMMA + 1, HG_WIDTH:] = -dgam

    last = n_steps - 1
    blk = lambda col: pl.BlockSpec((S * C, HG_WIDTH), lambda c, col=col: (last - c, col))
    return _pallas(
        body, comm=comm, edge=_grid_edge(n_steps), name="hgrn2_bwd", grid=(n_steps,),
        in_specs=[blk(0), blk(1), blk(2), blk(0), _full_spec((2, HG_WIDTH)), _full_spec(msum.shape),
                  _full_spec(msum_t.shape),
                  pl.BlockSpec((1, H, K, K), lambda c: (jnp.maximum(S * (last - c) - 1, 0), 0, 0, 0)),
                  pl.BlockSpec((S, H, K, K), lambda c: (last - c, 0, 0, 0))],
        out_specs=[blk(0), blk(0), blk(0), PACK_SPEC],
        out_shape=[jax.ShapeDtypeStruct((T, HG_WIDTH), F32)] * 3 + [jax.ShapeDtypeStruct((8, D_MODEL), F32)],
        scratch_shapes=[pltpu.VMEM((H, K, K), F32), pltpu.VMEM((1, HG_WIDTH), F32), pltpu.VMEM((S * C, HG_WIDTH), F32),
                        pltpu.VMEM((S * C, HG_WIDTH), F32), pltpu.VMEM((S, n_sums * C, HG_WIDTH), F32)],
        compiler_params=_params(("arbitrary",)),
    )(proj_h, proj_h, proj_h, do, gamma, msum, msum_t, states, states)


def _sb_sum_matrix(inclusive):
    B = SB_BLOCK
    j = np.arange(B)[:, None]
    s = np.arange(B)[None, :]
    tri = (j >= s) if inclusive else (j > s)
    once = np.concatenate([tri, np.ones((B, B), bool)], 1).astype(np.float32)
    return np.concatenate([once, once], 0)


def _sb_prefix_matrix(inclusive):
    B = SB_BLOCK
    j = np.arange(B)[:, None]
    s = np.arange(B)[None, :]
    tri = (j <= s) if inclusive else (j < s)
    once = np.concatenate([tri, np.ones((B, B), bool)], 1).astype(np.float32)
    return np.concatenate([once, once], 0)


def _sb_iotas():
    shape = (SB_BLOCK, SB_BLOCK)
    return lax.broadcasted_iota(jnp.int32, shape, 0), lax.broadcasted_iota(jnp.int32, shape, 1)


def _sb_heads(q, first):
    heads = []
    for g in range(SB_GROUP):
        qg = q[:, SB_BLOCK * g:SB_BLOCK * (g + 1)]
        zero = jnp.zeros_like(qg)
        heads += [(g, jnp.where(first, qg, zero)), (g, jnp.where(first, zero, qg))]
    return heads


def _lanes(x, g):
    return x[:, SB_BLOCK * g:SB_BLOCK * (g + 1)]


def sb_fwd(sqkv, comm=None):
    T = sqkv.shape[0]
    B = SB_BLOCK
    W = SB_GROUP * B
    groups = SB_WIDTH // W
    usum = jnp.asarray(_sb_sum_matrix(False), BF16)

    def body(q_ref, k_ref, v_ref, u_ref, o_ref, tl_ref, first_ref):
        p, i = pl.program_id(0), pl.program_id(1)
        row, lane = _sb_iotas()
        first = lane < SB_DH
        heads = _sb_heads(q_ref[...], first)
        u = u_ref[...]

        def more(loop):
            n, reachable, _ = loop
            return (SB_UNROLL * n <= i) & (reachable > 0)

        def step(loop):
            n, _, state = loop
            blocks = []
            for sub in range(SB_UNROLL):
                j = i - SB_UNROLL * n - sub
                off = pl.multiple_of(jnp.maximum(j, 0) * B, B)
                valid = ((lane + j * B) < (row + i * B)) & (j >= 0)
                blocks.append((k_ref[pl.ds(off, B), :], v_ref[pl.ds(off, B), :], valid))
            z = [[_dot_nt(qh, _lanes(kj, g)) for g, qh in heads] for kj, _, _ in blocks]
            lnb = [[jnp.where(valid, _neg_softplus(zz), 0.0) for zz in zs] for zs, (_, _, valid) in zip(z, blocks)]
            sums = [[_sum01_right(x, u) for x in xs] for xs in lnb]
            out = []
            for h, (carry, acc) in enumerate(state):
                for sub, (_, vj, valid) in enumerate(blocks):
                    expo = z[sub][h] + lnb[sub][h] + carry + sums[sub][h][:, :B]
                    acc = acc + _dot(jnp.where(valid, jnp.exp(expo), 0.0), _lanes(vj, heads[h][0]))
                    carry = carry + sums[sub][h][:, B:]
                out.append((carry, acc))
            state = tuple(out)
            worst = state[0][0]
            for carry, _ in state[1:]:
                worst = jnp.maximum(worst, carry)
            reachable = (jnp.max(worst) > SB_UNDERFLOW_LOG).astype(jnp.int32)
            return n + 1, reachable, state

        zero = jnp.zeros((B, B), F32)
        done, _, state = lax.while_loop(
            more, step, (jnp.int32(0), jnp.int32(1), tuple((zero, zero) for _ in heads)))
        for g in range(SB_GROUP):
            (tot0, acc0), (tot1, acc1) = state[2 * g], state[2 * g + 1]
            o_ref[:, B * g:B * (g + 1)] = jnp.where(first, acc0, acc1)
            tl_ref[:, B * g:B * (g + 1)] = jnp.where(first, tot0, tot1)
        first_ref[p, i] = jnp.maximum(i + 1 - SB_UNROLL * done, 0)

    def edge():
        p, i = pl.program_id(0), pl.program_id(1)
        return (p == 0) & (i == 0), (p == groups - 1) & (i == T // B - 1)

    return _pallas(
        body, comm=comm, edge=edge, name="sb_fwd", grid=(groups, T // B),
        in_specs=[pl.BlockSpec((B, W), lambda p, i: (i, p)),
                  pl.BlockSpec((T, W), lambda p, i: (0, groups + p)),
                  pl.BlockSpec((T, W), lambda p, i: (0, 2 * groups + p)),
                  pl.BlockSpec(usum.shape, lambda p, i: (0, 0))],
        out_specs=[pl.BlockSpec((B, W), lambda p, i: (i, p))] * 2 + [pl.BlockSpec(memory_space=pltpu.SMEM)],
        out_shape=[jax.ShapeDtypeStruct((T, SB_WIDTH), F32)] * 2 + [jax.ShapeDtypeStruct((groups, T // B), jnp.int32)],
        compiler_params=_params(("arbitrary", "arbitrary")),
    )(sqkv, sqkv, sqkv, usum)


def sb_bwd(sqkv, do, tl, first_block):
    T = sqkv.shape[0]
    B = SB_BLOCK
    W = SB_GROUP * B
    groups = SB_WIDTH // W
    upre = jnp.asarray(_sb_prefix_matrix(True), BF16)
    uexc = jnp.asarray(_sb_prefix_matrix(False), BF16)

    def body(q_ref, k_ref, v_ref, do_ref, tl_ref, up_ref, ue_ref, first_ref, dq_ref, dk_ref, dv_ref):
        p, i = pl.program_id(0), pl.program_id(1)

        @pl.when(i == 0)
        def _():
            dk_ref[...] = jnp.zeros(dk_ref.shape, F32)
            dv_ref[...] = jnp.zeros(dv_ref.shape, F32)

        row, lane = _sb_iotas()
        first = lane < SB_DH
        do = do_ref[...]
        tl_all = tl_ref[...]
        heads = []
        for (g, qh), at in zip(_sb_heads(q_ref[...], first), (0, B - 1) * SB_GROUP):
            dog = _lanes(do, g)
            keep = first if at == 0 else jnp.logical_not(first)
            heads.append((g, qh, jnp.where(keep, dog, jnp.zeros_like(dog)).astype(BF16),
                          _lanes(tl_all, g)[:, at:at + 1]))
        up = up_ref[...]
        ue = ue_ref[...]
        start = first_ref[p, i]

        def step(n, state):
            blocks = []
            for sub in range(SB_UNROLL):
                j = start + SB_UNROLL * n + sub
                off = pl.multiple_of(jnp.minimum(j, i) * B, B)
                valid = (lane + j * B) < (row + i * B)
                blocks.append((off, k_ref[pl.ds(off, B), :], v_ref[pl.ds(off, B), :], valid))
            combos = [(s, h) for s in range(SB_UNROLL) for h in range(len(heads))]
            z = {(s, h): _dot_nt(heads[h][1], _lanes(blocks[s][1], heads[h][0])) for s, h in combos}
            da = {(s, h): _dot_nt(heads[h][2], _lanes(blocks[s][2], heads[h][0])) for s, h in combos}
            lnb = {c: jnp.where(blocks[c[0]][3], _neg_softplus(z[c]), 0.0) for c in combos}
            lb = {c: z[c] + lnb[c] for c in combos}
            sums = {c: _sum01_right(lnb[c], up) for c in combos}
            a, w = {}, {}
            seen = [st[0] for st in state]
            for s, h in combos:
                expo = lb[s, h] + (heads[h][3] - seen[h] - sums[s, h][:, :B])
                a[s, h] = jnp.where(blocks[s][3], jnp.exp(expo), 0.0)
                w[s, h] = a[s, h] * da[s, h]
                seen[h] = seen[h] + sums[s, h][:, B:]
            wsums = {c: _sum01_right(w[c], ue) for c in combos}
            dz = {}
            seen_w = [st[1] for st in state]
            for s, h in combos:
                beta = jnp.exp(lb[s, h])
                before = seen_w[h] + wsums[s, h][:, :B]
                dz[s, h] = jnp.where(blocks[s][3], w[s, h] * (1.0 - beta) - before * beta, 0.0)
                seen_w[h] = seen_w[h] + wsums[s, h][:, B:]
            dq = [st[2] for st in state]
            for s, h in combos:
                dq[h] = dq[h] + _dot(dz[s, h], _lanes(blocks[s][1], heads[h][0]))
            for s in range(SB_UNROLL):
                off = blocks[s][0]
                for g in range(SB_GROUP):
                    h0, h1 = 2 * g, 2 * g + 1
                    dk_ref[pl.ds(off, B), B * g:B * (g + 1)] += (_dot_tn(dz[s, h0], heads[h0][1])
                                                                 + _dot_tn(dz[s, h1], heads[h1][1]))
                    dv_ref[pl.ds(off, B), B * g:B * (g + 1)] += (_dot_tn(a[s, h0], heads[h0][2])
                                                                 + _dot_tn(a[s, h1], heads[h1][2]))
            return tuple(zip(seen, seen_w, dq))

        zero = jnp.zeros((B, B), F32)
        trips = (i - start + SB_UNROLL) // SB_UNROLL
        state = lax.fori_loop(0, trips, step, tuple((zero, zero, zero) for _ in heads))
        for g in range(SB_GROUP):
            dq_ref[:, B * g:B * (g + 1)] = jnp.where(first, state[2 * g][2], state[2 * g + 1][2]) * SB_SCALE

    qblk = pl.BlockSpec((B, W), lambda p, i: (i, p))
    full = pl.BlockSpec((T, W), lambda p, i: (0, p))
    return pl.pallas_call(
        body, name="sb_bwd", grid=(groups, T // B),
        in_specs=[qblk, pl.BlockSpec((T, W), lambda p, i: (0, groups + p)),
                  pl.BlockSpec((T, W), lambda p, i: (0, 2 * groups + p)), qblk, qblk,
                  pl.BlockSpec(upre.shape, lambda p, i: (0, 0)), pl.BlockSpec(uexc.shape, lambda p, i: (0, 0)),
                  pl.BlockSpec(memory_space=pltpu.SMEM)],
        out_specs=[qblk, full, full],
        out_shape=[jax.ShapeDtypeStruct((T, SB_WIDTH), F32)] * 3,
        compiler_params=_params(("arbitrary", "arbitrary")),
    )(sqkv, sqkv, sqkv, do, tl, upre, uexc, first_block)


def _mixer_out(o_hg, hg, o_sb, g_hg, g_sb):
    n_hg, r_hg = _rms(o_hg)
    s_hg = _sigmoid(hg)
    n_sb, r_sb = _rms(o_sb)
    return dict(n_hg=n_hg, r_hg=r_hg, s_hg=s_hg, n_sb=n_sb, r_sb=r_sb,
                y_hg=n_hg * g_hg * (hg * s_hg), y_sb=n_sb * g_sb)


def mix_out_fwd(o_hg, proj_h, o_sb, x, norms, g_post, w_out, comm=None):
    T = x.shape[0]

    def body(ohg_ref, hg_ref, osb_ref, x_ref, nrm_ref, gp_ref, w_hbm, cat_ref, mix_ref, h1_ref, w_vmem):
        _load_once(w_hbm, w_vmem)
        nrm = nrm_ref[...]
        m = _mixer_out(ohg_ref[...], hg_ref[...], osb_ref[...], nrm[:, :HG_WIDTH], nrm[:, HG_WIDTH:])
        cat_ref[:, :HG_WIDTH] = m["y_hg"].astype(BF16)
        cat_ref[:, HG_WIDTH:] = m["y_sb"].astype(BF16)
        mix = jnp.dot(cat_ref[...], w_vmem[...], preferred_element_type=F32)
        mix_ref[...] = mix
        mh, _ = _rms(mix)
        h1_ref[...] = x_ref[...] + mh * gp_ref[...]

    return _pallas(
        body, comm=comm, edge=_grid_edge(T // ROW_TILE), name="mix_out_fwd", grid=(T // ROW_TILE,),
        in_specs=[_row_spec(HG_WIDTH), _row_spec(HG_WIDTH, 3), _row_spec(SB_WIDTH), _row_spec(D_MODEL),
                  _full_spec((1, D_MODEL)), _full_spec((1, D_MODEL)), ANY_SPEC],
        out_specs=[_row_spec(D_MODEL)] * 3,
        out_shape=[jax.ShapeDtypeStruct((T, D_MODEL), BF16), jax.ShapeDtypeStruct((T, D_MODEL), F32),
                   jax.ShapeDtypeStruct((T, D_MODEL), F32)],
        scratch_shapes=[pltpu.VMEM(w_out.shape, BF16)],
        compiler_params=_params(("arbitrary",)),
    )(o_hg, proj_h, o_sb, x, norms, g_post, w_out)


def ffn_fwd(h1, g_pre, g_post, w_gu, w_down):
    T = h1.shape[0]
    pw = w_gu.shape[2]

    def body(h1_ref, gpre_ref, gpost_ref, wgu_hbm, wd_hbm, u2_ref, gu_ref, act_ref, y_ref, h2_ref,
             wgu_vmem, wd_vmem, gu_s):
        _load_once(wgu_hbm, wgu_vmem)
        _load_once(wd_hbm, wd_vmem)
        h1v = h1_ref[...]
        hh, _ = _rms(h1v)
        u2 = (hh * gpre_ref[...]).astype(BF16)
        u2_ref[...] = u2
        for q in range(N_CHIPS):
            gu_s[:, pw * q:pw * (q + 1)] = jnp.dot(u2, wgu_vmem[q], preferred_element_type=F32)
        gu_ref[...] = gu_s[...].astype(BF16)
        gate = gu_s[:, :D_FF]
        act = (gate * _sigmoid(gate) * gu_s[:, D_FF:]).astype(BF16)
        act_ref[...] = act
        y = jnp.dot(act, wd_vmem[...], preferred_element_type=F32)
        y_ref[...] = y
        yh, _ = _rms(y)
        h2_ref[...] = h1v + yh * gpost_ref[...]

    return pl.pallas_call(
        body, name="ffn_fwd", grid=(T // ROW_TILE,),
        in_specs=[_row_spec(D_MODEL), _full_spec((1, D_MODEL)), _full_spec((1, D_MODEL)), ANY_SPEC, ANY_SPEC],
        out_specs=[_row_spec(D_MODEL), _row_spec(2 * D_FF), _row_spec(D_FF), _row_spec(D_MODEL), _row_spec(D_MODEL)],
        out_shape=[jax.ShapeDtypeStruct((T, D_MODEL), BF16), jax.ShapeDtypeStruct((T, 2 * D_FF), BF16),
                   jax.ShapeDtypeStruct((T, D_FF), BF16), jax.ShapeDtypeStruct((T, D_MODEL), F32),
                   jax.ShapeDtypeStruct((T, D_MODEL), F32)],
        scratch_shapes=[pltpu.VMEM(w_gu.shape, BF16), pltpu.VMEM(w_down.shape, BF16),
                        pltpu.VMEM((ROW_TILE, 2 * D_FF), F32)],
        compiler_params=_params(("arbitrary",)),
    )(h1, g_pre, g_post, w_gu, w_down)


def ple_loss(h2, p, target, w_ple, w_pg):
    T = h2.shape[0]
    pw = w_ple.shape[2]

    def body(h2_ref, p_ref, t_ref, wple_hbm, wpg_hbm, de_ref, ds_ref, dh2_ref, h2b_ref, pb_ref, pack_ref,
             wple_vmem, wpg_vmem, e_s):
        _load_once(wple_hbm, wple_vmem)
        _load_once(wpg_hbm, wpg_vmem)
        _zero_first(pack_ref)
        h2v = h2_ref[...]
        h2b = h2v.astype(BF16)
        h2b_ref[...] = h2b
        pb = p_ref[...].astype(BF16)
        pb_ref[...] = pb
        for q in range(N_CHIPS):
            e_s[:, pw * q:pw * (q + 1)] = jnp.dot(pb, wple_vmem[q], preferred_element_type=F32)
        e = e_s[...]
        sig = _sigmoid(jnp.dot(h2b, wpg_vmem[...], preferred_element_type=F32))
        err = h2v + e * sig - t_ref[...]
        part = 0.5 * jnp.sum(jnp.mean(err * err, axis=-1, keepdims=True), axis=0, keepdims=True)
        lane = lax.broadcasted_iota(jnp.int32, (1, D_MODEL), 1)
        pack_ref[ROW_LOSS:ROW_LOSS + 1, :] += jnp.where(lane == 0, part, 0.0)
        dh3 = err * (1.0 / D_MODEL)
        de_ref[...] = (dh3 * sig).astype(BF16)
        ds = (dh3 * e * sig * (1.0 - sig)).astype(BF16)
        ds_ref[...] = ds
        dh2_ref[...] = dh3 + _dot_nt(ds, wpg_vmem[...])

    return pl.pallas_call(
        body, name="ple_loss", grid=(T // ROW_TILE,),
        in_specs=[_row_spec(D_MODEL), _row_spec(p.shape[1]), _row_spec(D_MODEL), ANY_SPEC, ANY_SPEC],
        out_specs=[_row_spec(D_MODEL), _row_spec(D_MODEL), _row_spec(D_MODEL), _row_spec(D_MODEL),
                   _row_spec(p.shape[1]), PACK_SPEC],
        out_shape=[jax.ShapeDtypeStruct((T, D_MODEL), BF16), jax.ShapeDtypeStruct((T, D_MODEL), BF16),
                   jax.ShapeDtypeStruct((T, D_MODEL), F32), jax.ShapeDtypeStruct((T, D_MODEL), BF16),
                   jax.ShapeDtypeStruct(p.shape, BF16), jax.ShapeDtypeStruct((8, D_MODEL), F32)],
        scratch_shapes=[pltpu.VMEM(w_ple.shape, BF16), pltpu.VMEM(w_pg.shape, BF16), pltpu.VMEM((ROW_TILE, D_MODEL), F32)],
        compiler_params=_params(("arbitrary",)),
    )(h2, p, target, w_ple, w_pg)


def ffn_bwd(dh2, y, h1, gu, g_pre, g_post, w_gu, w_down):
    T = h1.shape[0]
    pw = w_gu.shape[2]

    def body(dh2_ref, y_ref, h1_ref, gu_ref, gpre_ref, gpost_ref, wgu_hbm, wd_hbm, dy_ref, dgu_ref, dh1_ref, pack_ref,
             wgu_vmem, wd_vmem):
        _load_once(wgu_hbm, wgu_vmem)
        _load_once(wd_hbm, wd_vmem)
        _zero_first(pack_ref)
        dh2v = dh2_ref[...]
        yh, ry = _rms(y_ref[...])
        dy, dw = _rms_bwd(dh2v, yh, ry, gpost_ref[...])
        pack_ref[ROW_FFN_POST:ROW_FFN_POST + 1, :] += _colsum(dw)
        dyb = dy.astype(BF16)
        dy_ref[...] = dyb
        dact = _dot_nt(dyb, wd_vmem[...])
        gate = gu_ref[:, :D_FF].astype(F32)
        up = gu_ref[:, D_FF:].astype(F32)
        sg = _sigmoid(gate)
        dgu_ref[:, :D_FF] = (dact * up * (sg * (1.0 + gate * (1.0 - sg)))).astype(BF16)
        dgu_ref[:, D_FF:] = (dact * gate * sg).astype(BF16)
        du2 = _dot_nt(dgu_ref[:, :pw], wgu_vmem[0])
        for q in range(1, N_CHIPS):
            du2 = du2 + _dot_nt(dgu_ref[:, pw * q:pw * (q + 1)], wgu_vmem[q])
        hh, rh = _rms(h1_ref[...])
        dh, dw = _rms_bwd(du2, hh, rh, gpre_ref[...])
        pack_ref[ROW_FFN_PRE:ROW_FFN_PRE + 1, :] += _colsum(dw)
        dh1_ref[...] = dh2v + dh

    return pl.pallas_call(
        body, name="ffn_bwd", grid=(T // ROW_TILE,),
        in_specs=[_row_spec(D_MODEL), _row_spec(D_MODEL), _row_spec(D_MODEL), _row_spec(2 * D_FF),
                  _full_spec((1, D_MODEL)), _full_spec((1, D_MODEL)), ANY_SPEC, ANY_SPEC],
        out_specs=[_row_spec(D_MODEL), _row_spec(2 * D_FF), _row_spec(D_MODEL), PACK_SPEC],
        out_shape=[jax.ShapeDtypeStruct((T, D_MODEL), BF16), jax.ShapeDtypeStruct((T, 2 * D_FF), BF16),
                   jax.ShapeDtypeStruct((T, D_MODEL), F32), jax.ShapeDtypeStruct((8, D_MODEL), F32)],
        scratch_shapes=[pltpu.VMEM(w_gu.shape, BF16), pltpu.VMEM(w_down.shape, BF16)],
        compiler_params=_params(("arbitrary",)),
    )(dh2, y, h1, gu, g_pre, g_post, w_gu, w_down)


def mix_out_bwd(dh1, mix, o_hg, proj_h, o_sb, norms, g_post, w_out, comm=None):
    T = dh1.shape[0]

    def body(dh1_ref, mix_ref, ohg_ref, hg_ref, osb_ref, nrm_ref, gp_ref, w_hbm, dmix_ref, dohg_ref, dhg_ref, dosb_ref,
             pack_ref, w_vmem):
        _load_once(w_hbm, w_vmem)
        _zero_first(pack_ref)
        mh, rm = _rms(mix_ref[...])
        dmix, dw = _rms_bwd(dh1_ref[...], mh, rm, gp_ref[...])
        pack_ref[ROW_ATTN_POST:ROW_ATTN_POST + 1, :] += _colsum(dw)
        dmb = dmix.astype(BF16)
        dmix_ref[...] = dmb
        dcat = _dot_nt(dmb, w_vmem[...])
        nrm = nrm_ref[...]
        g_hg, g_sb = nrm[:, :HG_WIDTH], nrm[:, HG_WIDTH:]
        hg = hg_ref[...]
        m = _mixer_out(ohg_ref[...], hg, osb_ref[...], g_hg, g_sb)
        d_hg = dcat[:, :HG_WIDTH]
        silu = hg * m["s_hg"]
        dhg_ref[...] = d_hg * (m["n_hg"] * g_hg) * (m["s_hg"] * (1.0 + hg * (1.0 - m["s_hg"])))
        dx, dw = _rms_bwd(d_hg * silu, m["n_hg"], m["r_hg"], g_hg)
        dohg_ref[...] = dx
        pack_ref[ROW_MIX_NORMS:ROW_MIX_NORMS + 1, :HG_WIDTH] += _colsum(dw)
        dx, dw = _rms_bwd(dcat[:, HG_WIDTH:], m["n_sb"], m["r_sb"], g_sb)
        dosb_ref[...] = dx
        pack_ref[ROW_MIX_NORMS:ROW_MIX_NORMS + 1, HG_WIDTH:] += _colsum(dw)

    return _pallas(
        body, comm=comm, edge=_grid_edge(T // ROW_TILE), name="mix_out_bwd", grid=(T // ROW_TILE,),
        in_specs=[_row_spec(D_MODEL), _row_spec(D_MODEL), _row_spec(HG_WIDTH), _row_spec(HG_WIDTH, 3), _row_spec(SB_WIDTH),
                  _full_spec((1, D_MODEL)), _full_spec((1, D_MODEL)), ANY_SPEC],
        out_specs=[_row_spec(D_MODEL), _row_spec(HG_WIDTH), _row_spec(HG_WIDTH), _row_spec(SB_WIDTH), PACK_SPEC],
        out_shape=[jax.ShapeDtypeStruct((T, D_MODEL), BF16), jax.ShapeDtypeStruct((T, HG_WIDTH), F32),
                   jax.ShapeDtypeStruct((T, HG_WIDTH), F32), jax.ShapeDtypeStruct((T, SB_WIDTH), F32),
                   jax.ShapeDtypeStruct((8, D_MODEL), F32)],
        scratch_shapes=[pltpu.VMEM(w_out.shape, BF16)],
        compiler_params=_params(("arbitrary",)),
    )(dh1, mix, o_hg, proj_h, o_sb, norms, g_post, w_out)


def in_proj_bwd(parts, x, dh1, g_pre, w_in, comm=None):
    T = x.shape[0]
    pw = w_in.shape[2]
    n_parts = len(parts)

    def body(*refs):
        part_refs = refs[:n_parts]
        x_ref, dh1_ref, g_ref, w_hbm, dproj_ref, dx_ref, pack_ref, w_vmem = refs[n_parts:]
        _load_once(w_hbm, w_vmem)
        _zero_first(pack_ref)
        for n, ref in enumerate(part_refs):
            dproj_ref[:, HG_WIDTH * n:HG_WIDTH * (n + 1)] = ref[...].astype(BF16)
        du = _dot_nt(dproj_ref[:, :pw], w_vmem[0])
        for q in range(1, N_CHIPS):
            du = du + _dot_nt(dproj_ref[:, pw * q:pw * (q + 1)], w_vmem[q])
        xh, r = _rms(x_ref[...])
        dx, dw = _rms_bwd(du, xh, r, g_ref[...])
        pack_ref[ROW_ATTN_PRE:ROW_ATTN_PRE + 1, :] += _colsum(dw)
        dx_ref[...] = dh1_ref[...] + dx

    return _pallas(
        body, comm=comm, edge=_grid_edge(T // ROW_TILE), name="in_proj_bwd", grid=(T // ROW_TILE,),
        in_specs=[_row_spec(HG_WIDTH)] * n_parts + [_row_spec(D_MODEL), _row_spec(D_MODEL), _full_spec((1, D_MODEL)), ANY_SPEC],
        out_specs=[_row_spec(n_parts * HG_WIDTH), _row_spec(D_MODEL), PACK_SPEC],
        out_shape=[jax.ShapeDtypeStruct((T, n_parts * HG_WIDTH), BF16), jax.ShapeDtypeStruct((T, D_MODEL), F32),
                   jax.ShapeDtypeStruct((8, D_MODEL), F32)],
        scratch_shapes=[pltpu.VMEM(w_in.shape, BF16)],
        compiler_params=_params(("arbitrary",)),
    )(*parts, x, dh1, g_pre, w_in)


def weight_grad(a, g, name, *, tm, tn, tk=512, col_pieces=False, comm=None):
    T, M = a.shape
    N = g.shape[1]
    tk = min(tk, T)
    steps = T // tk

    def body(a_ref, g_ref, o_ref):
        @pl.when(pl.program_id(2) == 0)
        def _():
            o_ref[...] = jnp.zeros(o_ref.shape, F32)

        o_ref[...] += _dot_tn(a_ref[...], g_ref[...]).reshape(o_ref.shape)

    if col_pieces:
        out_shape = jax.ShapeDtypeStruct((N // tn, M, tn), F32)
        out_spec = pl.BlockSpec((1, tm, tn), lambda i, j, k: (j, i, 0))
    else:
        out_shape = jax.ShapeDtypeStruct((M, N), F32)
        out_spec = pl.BlockSpec((tm, tn), lambda i, j, k: (i, j))
    grid = (M // tm, N // tn, steps)

    def edge():
        at = [pl.program_id(d) for d in range(3)]
        return ((at[0] == 0) & (at[1] == 0) & (at[2] == 0),
                (at[0] == grid[0] - 1) & (at[1] == grid[1] - 1) & (at[2] == grid[2] - 1))

    return _pallas(
        body, comm=comm, edge=edge, name=name, grid=grid,
        in_specs=[pl.BlockSpec((tk, tm), lambda i, j, k: (k, i)), pl.BlockSpec((tk, tn), lambda i, j, k: (k, j))],
        out_specs=[out_spec], out_shape=[out_shape],
        compiler_params=_params(("arbitrary", "arbitrary", "arbitrary")),
    )(a, g)


def _place():
    x, y, c = lax.axis_index("x"), lax.axis_index("y"), lax.axis_index("c")
    chips = [(1 - x, y), (x, 1 - y), (1 - x, 1 - y)]
    return x, y, c, chips


def _chip_index(cx, cy):
    return 2 * cx + cy


def _rcopy(src, dst, send_sem, recv_sem, device):
    return pltpu.make_async_remote_copy(src_ref=src, dst_ref=dst, send_sem=send_sem, recv_sem=recv_sem,
                                        device_id=device, device_id_type=MESH)


def gather_weights(shards):
    n = len(shards)

    def body(*refs):
        ins, outs = refs[:n], refs[2 * n:3 * n]
        send_sems, recv_sems = refs[3 * n:]
        x, y, c, chips = _place()
        me = _chip_index(x, y)
        sibling = (x, y, 1 - c)

        def rows(w, core):
            half = ins[w].shape[0] // 2
            return pl.ds(core * half, half)

        sends = []
        for w in range(n):
            for j, chip in enumerate(chips):
                sends.append(_rcopy(ins[w].at[rows(w, c)], outs[w].at[me, rows(w, c)],
                                    send_sems.at[6 * w + j], recv_sems.at[6 * w + j], (*chip, c)))
        for cp in sends:
            cp.start()
        passed = []
        for w in range(n):
            for j, chip in enumerate(chips):
                block = outs[w].at[_chip_index(*chip), rows(w, c)]
                _rcopy(block, block, send_sems.at[6 * w + j], recv_sems.at[6 * w + j], (*chip, c)).wait_recv()
                cp = _rcopy(block, block, send_sems.at[6 * w + 3 + j], recv_sems.at[6 * w + 3 + j], sibling)
                cp.start()
                passed.append(cp)
        for w in range(n):
            for j, chip in enumerate(chips):
                block = outs[w].at[_chip_index(*chip), rows(w, 1 - c)]
                _rcopy(block, block, send_sems.at[6 * w + 3 + j], recv_sems.at[6 * w + 3 + j], sibling).wait_recv()
        for cp in sends + passed:
            cp.wait_send()

    filled = [jnp.broadcast_to(s[None], (N_CHIPS,) + s.shape) for s in shards]
    return pl.pallas_call(
        body, name="gather_weights",
        in_specs=[ANY_SPEC] * (2 * n), out_specs=[ANY_SPEC] * n,
        out_shape=[jax.ShapeDtypeStruct(f.shape, f.dtype) for f in filled],
        input_output_aliases={n + w: w for w in range(n)},
        scratch_shapes=[pltpu.SemaphoreType.DMA((6 * n,)), pltpu.SemaphoreType.DMA((6 * n,))],
    )(*shards, *filled)


def _run_comm(comm, name):
    c_in, c_out = len(comm.inputs), len(comm.out_shape)

    def body(*refs):
        parts = refs[:c_in], refs[c_in:c_in + c_out], refs[c_in + c_out:]
        comm.start(*parts)
        comm.finish(*parts)

    return pl.pallas_call(
        body, name=name, in_specs=[ANY_SPEC] * c_in, out_specs=[ANY_SPEC] * c_out, out_shape=comm.out_shape,
        scratch_shapes=comm.scratch, input_output_aliases=comm.aliases)(*comm.inputs)


def _both(first, second):
    n_in, n_out, n_scr = len(first.inputs), len(first.out_shape), len(first.scratch)

    def split(ins, outs, scr):
        return (ins[:n_in], outs[:n_out], scr[:n_scr]), (ins[n_in:], outs[n_out:], scr[n_scr:])

    def start(*refs):
        a, b = split(*refs)
        first.start(*a)
        second.start(*b)

    def finish(*refs):
        a, b = split(*refs)
        first.finish(*a)
        second.finish(*b)

    aliases = dict(first.aliases)
    aliases.update({n_in + i: n_out + o for i, o in second.aliases.items()})
    return Comm(first.inputs + second.inputs, first.out_shape + second.out_shape, aliases,
                first.scratch + second.scratch, start, finish)


def _dma_sems(count):
    return [pltpu.SemaphoreType.DMA((count,)), pltpu.SemaphoreType.DMA((count,))]


def gather_over_ici(shards):
    n = len(shards)

    def copies(ins, outs, sems):
        send_sems, recv_sems = sems
        x, y, c, chips = _place()
        me = _chip_index(x, y)
        pairs = []
        for w in range(n):
            half = shards[w].shape[0] // 2
            rows = pl.ds(c * half, half)
            for j, chip in enumerate(chips):
                k = 3 * w + j
                landed = outs[w].at[_chip_index(*chip), rows]
                pairs.append((_rcopy(ins[w].at[rows], outs[w].at[me, rows], send_sems.at[k], recv_sems.at[k], (*chip, c)),
                              _rcopy(landed, landed, send_sems.at[k], recv_sems.at[k], (*chip, c))))
        return pairs

    def start(*refs):
        for send, _ in copies(*refs):
            send.start()

    def finish(*refs):
        pairs = copies(*refs)
        for _, landed in pairs:
            landed.wait_recv()
        for send, _ in pairs:
            send.wait_send()

    filled = [jnp.broadcast_to(s[None], (N_CHIPS,) + s.shape) for s in shards]
    return Comm(list(shards) + filled, [jax.ShapeDtypeStruct(f.shape, f.dtype) for f in filled],
                {n + w: w for w in range(n)}, _dma_sems(3 * n), start, finish)


def gather_over_d2d(landed):
    n = len(landed)

    def copies(ins, outs, sems):
        send_sems, recv_sems = sems
        x, y, c, chips = _place()
        sibling = (x, y, 1 - c)
        pairs = []
        for w in range(n):
            half = landed[w].shape[1] // 2
            for j, chip in enumerate(chips):
                k = 3 * w + j
                mine = outs[w].at[_chip_index(*chip), pl.ds(c * half, half)]
                theirs = outs[w].at[_chip_index(*chip), pl.ds((1 - c) * half, half)]
                pairs.append((_rcopy(mine, mine, send_sems.at[k], recv_sems.at[k], sibling),
                              _rcopy(theirs, theirs, send_sems.at[k], recv_sems.at[k], sibling)))
        return pairs

    def start(*refs):
        for send, _ in copies(*refs):
            send.start()

    def finish(*refs):
        pairs = copies(*refs)
        for _, arrived in pairs:
            arrived.wait_recv()
        for send, _ in pairs:
            send.wait_send()

    return Comm(list(landed), [jax.ShapeDtypeStruct(a.shape, a.dtype) for a in landed], {w: w for w in range(n)},
                _dma_sems(3 * n), start, finish)


def core_halves(grads):
    n = len(grads)

    def copies(ins, outs, sems):
        send_sems, recv_sems = sems
        x, y, c, _ = _place()
        out = []
        for w in range(n):
            half = grads[w].shape[1] // 2
            out.append(_rcopy(ins[w].at[:, pl.ds((1 - c) * half, half), :], outs[w],
                              send_sems.at[w], recv_sems.at[w], (x, y, 1 - c)))
        return out

    def start(*refs):
        for cp in copies(*refs):
            cp.start()

    def finish(*refs):
        for cp in copies(*refs):
            cp.wait()

    return Comm(list(grads), [jax.ShapeDtypeStruct((g.shape[0], g.shape[1] // 2, g.shape[2]), g.dtype) for g in grads],
                {}, _dma_sems(n), start, finish)


def chip_partials(partials):
    n = len(partials)

    def copies(ins, outs, sems):
        send_sems, recv_sems = sems
        x, y, c, chips = _place()
        me = _chip_index(x, y)
        pairs = []
        for w in range(n):
            for j, chip in enumerate(chips):
                k = 3 * w + j
                landed = outs[w].at[_chip_index(*chip)]
                pairs.append((_rcopy(ins[w].at[_chip_index(*chip)], outs[w].at[me], send_sems.at[k], recv_sems.at[k],
                                     (*chip, c)),
                              _rcopy(landed, landed, send_sems.at[k], recv_sems.at[k], (*chip, c))))
        return pairs

    def start(*refs):
        for send, _ in copies(*refs):
            send.start()

    def finish(*refs):
        pairs = copies(*refs)
        for _, landed in pairs:
            landed.wait_recv()
        for send, _ in pairs:
            send.wait_send()

    me = _chip_index(lax.axis_index("x"), lax.axis_index("y"))
    filled = [jnp.broadcast_to(lax.dynamic_index_in_dim(p, me, 0, keepdims=True), p.shape) for p in partials]
    return Comm(list(partials) + filled, [jax.ShapeDtypeStruct(p.shape, p.dtype) for p in partials],
                {n + w: w for w in range(n)}, _dma_sems(3 * n), start, finish)


def join_core_halves(grads):
    n = len(grads)

    def body(*refs):
        outs = refs[n:2 * n]
        send_sems, recv_sems = refs[2 * n:]
        x, y, c, _ = _place()
        sibling = (x, y, 1 - c)
        copies = []
        for w in range(n):
            half = outs[w].shape[0] // 2
            mine = outs[w].at[pl.ds(c * half, half), :]
            copies.append(_rcopy(mine, mine, send_sems.at[w], recv_sems.at[w], sibling))
        for cp in copies:
            cp.start()
        for w in range(n):
            half = outs[w].shape[0] // 2
            theirs = outs[w].at[pl.ds((1 - c) * half, half), :]
            _rcopy(theirs, theirs, send_sems.at[w], recv_sems.at[w], sibling).wait_recv()
        for cp in copies:
            cp.wait_send()

    return pl.pallas_call(
        body, name="join_core_halves",
        in_specs=[ANY_SPEC] * n, out_specs=[ANY_SPEC] * n,
        out_shape=[jax.ShapeDtypeStruct(g.shape, g.dtype) for g in grads],
        input_output_aliases={w: w for w in range(n)},
        scratch_shapes=[pltpu.SemaphoreType.DMA((n,)), pltpu.SemaphoreType.DMA((n,))],
    )(*grads)


def _elementwise_rows(rows, cap=512):
    for t in range(min(rows, cap), 0, -8):
        if rows % t == 0 and t % 16 == 0:
            return t
    return rows


def add_core_halves(grad, got, core, name):
    _, rows, cols = got.shape
    tr = _elementwise_rows(rows)
    nt = rows // tr

    def body(core_ref, a_ref, b_ref, o_ref):
        o_ref[...] = (a_ref[...] + b_ref[...]).astype(BF16)

    spec = pl.BlockSpec((1, tr, cols), lambda q, i, core_ref: (q, i, 0))
    own = pl.BlockSpec((1, tr, cols), lambda q, i, core_ref: (q, core_ref[0] * nt + i, 0))
    return pl.pallas_call(
        body, name=name,
        grid_spec=pltpu.PrefetchScalarGridSpec(num_scalar_prefetch=1, grid=(N_CHIPS, nt), in_specs=[own, spec],
                                               out_specs=spec),
        out_shape=jax.ShapeDtypeStruct(got.shape, BF16),
        compiler_params=_params(("arbitrary", "arbitrary")),
    )(core, grad, got)


def add_chip_partials(parts, core, name):
    _, rows, cols = parts.shape
    tr = _elementwise_rows(rows)
    nt = rows // tr

    def body(core_ref, p_ref, o_ref):
        acc = p_ref[0].astype(F32)
        for q in range(1, N_CHIPS):
            acc = acc + p_ref[q].astype(F32)
        o_ref[...] = acc

    return pl.pallas_call(
        body, name=name,
        grid_spec=pltpu.PrefetchScalarGridSpec(
            num_scalar_prefetch=1, grid=(nt,),
            in_specs=[pl.BlockSpec((N_CHIPS, tr, cols), lambda i, core_ref: (0, i, 0))],
            out_specs=pl.BlockSpec((tr, cols), lambda i, core_ref: (core_ref[0] * nt + i, 0))),
        out_shape=jax.ShapeDtypeStruct((2 * rows, cols), F32),
        compiler_params=_params(("arbitrary",)),
    )(core, parts)


def _adamw_math(w, g, m, v):
    m = ADAM_B1 * m + (1.0 - ADAM_B1) * g
    v = ADAM_B2 * v + (1.0 - ADAM_B2) * (g * g)
    m_hat = m / (1.0 - ADAM_B1 ** ADAM_STEP)
    v_hat = v / (1.0 - ADAM_B2 ** ADAM_STEP)
    delta = -ADAM_LR * (m_hat / (jnp.sqrt(v_hat) + ADAM_EPS) + ADAM_WD * w)
    return delta, m, v


def adamw(w, g, m, v, name):
    rows, cols = w.shape
    tr = _elementwise_rows(rows, 256)

    def body(w_ref, g_ref, m_ref, v_ref, d_ref, nm_ref, nv_ref):
        d, nm, nv = _adamw_math(w_ref[...], g_ref[...], m_ref[...], v_ref[...])
        d_ref[...] = d
        nm_ref[...] = nm
        nv_ref[...] = nv

    spec = pl.BlockSpec((tr, cols), lambda i: (i, 0))
    return pl.pallas_call(
        body, name=name, grid=(rows // tr,), in_specs=[spec] * 4, out_specs=[spec] * 3,
        out_shape=[jax.ShapeDtypeStruct((rows, cols), F32)] * 3,
        compiler_params=_params(("arbitrary",)),
    )(w, g, m, v)


def reduce_small(packs, w, m, v):
    n = len(packs)
    n_dev = 8
    flips = [(fx, fy, fc) for fx in (0, 1) for fy in (0, 1) for fc in (0, 1)][1:]

    def body(*refs):
        pack_refs = refs[:n]
        w_ref, m_ref, v_ref, g_out, d_out, m_out, v_out, mine, slots, send_sems, recv_sems = refs[n:]
        x, y, c, _ = _place()
        me = 4 * x + 2 * y + c
        acc = pack_refs[0][...]
        for ref in pack_refs[1:]:
            acc = acc + ref[...]
        mine[...] = acc
        sends = []
        for k, (fx, fy, fc) in enumerate(flips):
            peer = (x ^ fx, y ^ fy, c ^ fc)
            sends.append(_rcopy(mine, slots.at[me], send_sems.at[k], recv_sems.at[me], peer))
        for cp in sends:
            cp.start()
        slots[me] = acc
        for fx, fy, fc in flips:
            src = 4 * (x ^ fx) + 2 * (y ^ fy) + (c ^ fc)
            _rcopy(mine, slots.at[src], send_sems.at[0], recv_sems.at[src], (x, y, c)).wait_recv()
        for cp in sends:
            cp.wait_send()
        total = slots[0]
        for d in range(1, n_dev):
            total = total + slots[d]
        g_out[...] = total
        d, nm, nv = _adamw_math(w_ref[...], total, m_ref[...], v_ref[...])
        d_out[...] = d
        m_out[...] = nm
        v_out[...] = nv

    vm = pl.BlockSpec(memory_space=pltpu.VMEM)
    return pl.pallas_call(
        body, name="reduce_small",
        in_specs=[vm] * (n + 3), out_specs=[vm] * 4,
        out_shape=[jax.ShapeDtypeStruct((8, D_MODEL), F32)] * 4,
        scratch_shapes=[pltpu.VMEM((8, D_MODEL), F32), pltpu.VMEM((n_dev, 8, D_MODEL), F32),
                        pltpu.SemaphoreType.DMA((len(flips),)), pltpu.SemaphoreType.DMA((n_dev,))],
    )(*packs, w, m, v)


def _pack_small(attn_pre, gamma, hg_norm, sb_norm, attn_post, ffn_pre, ffn_post):
    rows = [attn_pre, gamma.reshape(1, D_MODEL), jnp.concatenate([hg_norm, sb_norm], axis=1), attn_post, ffn_pre, ffn_post,
            jnp.zeros((2, D_MODEL), F32)]
    return jnp.concatenate(rows, axis=0)


def _unpack_small(pack):
    return (pack[ROW_ATTN_PRE:ROW_ATTN_PRE + 1], pack[ROW_GAMMA].reshape(2, HG_WIDTH),
            pack[ROW_MIX_NORMS:ROW_MIX_NORMS + 1, :HG_WIDTH], pack[ROW_MIX_NORMS:ROW_MIX_NORMS + 1, HG_WIDTH:],
            pack[ROW_ATTN_POST:ROW_ATTN_POST + 1], pack[ROW_FFN_PRE:ROW_FFN_PRE + 1], pack[ROW_FFN_POST:ROW_FFN_POST + 1])


def kernel(x, p, attn_pre_norm, w_in, hg_lower_gamma, hg_out_norm, sb_out_norm, w_out, attn_post_norm, ffn_pre_norm, w_gate_up, w_down, ffn_post_norm, ple_proj, ple_gate, loss_target, m_attn_pre_norm, m_w_in, m_hg_lower_gamma, m_hg_out_norm, m_sb_out_norm, m_w_out, m_attn_post_norm, m_ffn_pre_norm, m_w_gate_up, m_w_down, m_ffn_post_norm, m_ple_proj, m_ple_gate, v_attn_pre_norm, v_w_in, v_hg_lower_gamma, v_hg_out_norm, v_sb_out_norm, v_w_out, v_attn_post_norm, v_ffn_pre_norm, v_w_gate_up, v_w_down, v_ffn_post_norm, v_ple_proj, v_ple_gate):
    x2 = x[0]
    p2 = p[0, 0]
    target = loss_target[0]
    big = dict(w_in=(w_in, m_w_in, v_w_in), w_out=(w_out, m_w_out, v_w_out), w_gate_up=(w_gate_up, m_w_gate_up, v_w_gate_up),
               w_down=(w_down, m_w_down, v_w_down), ple_proj=(ple_proj, m_ple_proj, v_ple_proj),
               ple_gate=(ple_gate, m_ple_gate, v_ple_gate))
    names = list(big)
    big = {k: tuple(a[0] for a in t) for k, t in big.items()}

    shard16 = {k: big[k][0].astype(BF16) for k in names}
    w_in_full, = gather_weights([shard16["w_in"]])
    mix_norms = jnp.concatenate([hg_out_norm, sb_out_norm], axis=1)
    small_ones = ["w_out", "ple_proj", "ple_gate"]

    proj_h, sqkv, u1, *landed_small = in_proj_fwd(
        x2, attn_pre_norm, w_in_full, comm=gather_over_ici([shard16[k] for k in small_ones]))
    o_sb, sb_totals, sb_first, landed_gu = sb_fwd(sqkv, comm=gather_over_ici([shard16["w_gate_up"]]))
    o_hg, states, landed_down, *full_small = hgrn2_fwd(
        proj_h, hg_lower_gamma, comm=_both(gather_over_ici([shard16["w_down"]]), gather_over_d2d(landed_small)))
    full = dict(zip(small_ones, full_small), w_in=w_in_full)
    w_out_full = full["w_out"].reshape(D_MODEL, D_MODEL)
    w_pg_full = full["ple_gate"].reshape(D_MODEL, D_MODEL)
    cat, mix, h1, full["w_gate_up"], full["w_down"] = mix_out_fwd(
        o_hg, proj_h, o_sb, x2, mix_norms, attn_post_norm, w_out_full, comm=gather_over_d2d([landed_gu, landed_down]))
    w_down_full = full["w_down"].reshape(D_FF, D_MODEL)
    u2, gu, act, y, h2 = ffn_fwd(h1, ffn_pre_norm, ffn_post_norm, full["w_gate_up"], w_down_full)

    core = lax.axis_index("c").astype(jnp.int32).reshape(1)
    de, ds, dh2, h2b, pb, pack_loss = ple_loss(h2, p2, target, full["ple_proj"], w_pg_full)
    dy, dgu, dh1, pack_ffn = ffn_bwd(dh2, y, h1, gu, ffn_pre_norm, ffn_post_norm, full["w_gate_up"], w_down_full)
    local = dict(
        w_gate_up=weight_grad(u2, dgu, "grad_w_gate_up", tm=D_MODEL, tn=full["w_gate_up"].shape[2], tk=1024, col_pieces=True)[0],
        w_down=weight_grad(act, dy, "grad_w_down", tm=D_FF // 2, tn=D_MODEL, tk=1024)[0].reshape(full["w_down"].shape),
        ple_proj=weight_grad(pb, de, "grad_ple_proj", tm=pb.shape[1], tn=full["ple_proj"].shape[2], col_pieces=True)[0],
        ple_gate=weight_grad(h2b, ds, "grad_ple_gate", tm=D_MODEL, tn=D_MODEL)[0].reshape(full["ple_gate"].shape),
    )
    early = list(local)
    dmix, do_hg, dhg, do_sb, pack_mix, *got = mix_out_bwd(
        dh1, mix, o_hg, proj_h, o_sb, mix_norms, attn_post_norm, w_out_full, comm=core_halves([local[k] for k in early]))
    partial = [add_core_halves(local[k], g, core, "add_core_halves_" + k) for k, g in zip(early, got)]
    dsq, dsk, dsv = sb_bwd(sqkv, do_sb, sb_totals, sb_first)
    dhq, dhf, dhi, pack_hg, *by_source = hgrn2_bwd(proj_h, hg_lower_gamma, states, do_hg, comm=chip_partials(partial[:1]))
    dproj, grad_x, pack_in = in_proj_bwd([dhq, dhf, dhi, dhg, dsq, dsk, dsv], x2, dh1, attn_pre_norm, full["w_in"])

    late = ["w_in", "w_out"]
    local["w_in"], *more = weight_grad(u1, dproj, "grad_w_in", tm=D_MODEL, tn=full["w_in"].shape[2], tk=1024,
                                       col_pieces=True, comm=chip_partials(partial[1:]))
    halves = {k: add_chip_partials(s, core, "add_chip_partials_" + k) for k, s in zip(early, by_source + more)}
    local["w_out"] = weight_grad(cat, dmix, "grad_w_out", tm=D_MODEL, tn=D_MODEL)[0].reshape(full["w_out"].shape)
    got = _run_comm(core_halves([local[k] for k in late]), "exchange_core_halves")
    partial = [add_core_halves(local[k], g, core, "add_core_halves_" + k) for k, g in zip(late, got)]
    by_source = _run_comm(chip_partials(partial), "exchange_chip_partials")
    halves.update({k: add_chip_partials(s, core, "add_chip_partials_" + k) for k, s in zip(late, by_source)})
    grads = dict(zip(names, join_core_halves([halves[k] for k in names])))

    upd = {k: adamw(big[k][0], grads[k], big[k][1], big[k][2], "adamw_" + k) for k in names}

    small = reduce_small(
        [pack_loss, pack_ffn, pack_mix, pack_hg, pack_in],
        _pack_small(attn_pre_norm, hg_lower_gamma, hg_out_norm, sb_out_norm, attn_post_norm, ffn_pre_norm, ffn_post_norm),
        _pack_small(m_attn_pre_norm, m_hg_lower_gamma, m_hg_out_norm, m_sb_out_norm, m_attn_post_norm, m_ffn_pre_norm, m_ffn_post_norm),
        _pack_small(v_attn_pre_norm, v_hg_lower_gamma, v_hg_out_norm, v_sb_out_norm, v_attn_post_norm, v_ffn_pre_norm, v_ffn_post_norm),
    )
    loss = small[0][ROW_LOSS, 0]
    s_grad, s_delta, s_m, s_v = (_unpack_small(t) for t in small)

    def ordered(small_vals, big_vals):
        a_pre, gam, hg_n, sb_n, a_post, f_pre, f_post = small_vals
        b = {k: big_vals[k][None] for k in names}
        return (a_pre, b["w_in"], gam, hg_n, sb_n, b["w_out"], a_post, f_pre, b["w_gate_up"], b["w_down"], f_post,
                b["ple_proj"], b["ple_gate"])

    return (loss, grad_x[None],
            *ordered(s_grad, grads),
            *ordered(s_delta, {k: upd[k][0] for k in names}),
            *ordered(s_m, {k: upd[k][1] for k in names}),
            *ordered(s_v, {k: upd[k][2] for k in names}))
```

```python
from typing import Callable, NamedTuple

import numpy as np
import jax
import jax.numpy as jnp
from jax import lax
from jax.experimental import pallas as pl
from jax.experimental.pallas import tpu as pltpu

F32 = jnp.float32
BF16 = jnp.bfloat16
MESH = pl.DeviceIdType.MESH

RMS_EPS = 1e-6
D_MODEL = 1024
HG_WIDTH = 512
HG_HEADS = 4
HG_DK = 128
HG_CHUNK = 64
HG_LEVELS = (32, 16, 8, 4, 2, 1)
HG_CHUNKS_PER_STEP = 2
SB_WIDTH = 512
SB_BLOCK = 128
SB_DH = 64
SB_SCALE = SB_DH ** -0.5
SB_UNDERFLOW_LOG = -87.5
SB_UNROLL = 2
SB_GROUP = 2
D_FF = 2816
N_CHIPS = 4
ROW_TILE = 256
V7X_VMEM_LIMIT = 56 * 1024 * 1024

ADAM_LR = 0.001
ADAM_B1 = 0.9
ADAM_B2 = 0.999
ADAM_EPS = 1e-08
ADAM_WD = 0.01
ADAM_STEP = 10

ROW_ATTN_PRE, ROW_GAMMA, ROW_MIX_NORMS, ROW_ATTN_POST, ROW_FFN_PRE, ROW_FFN_POST, ROW_LOSS = range(7)


def _params(sem=None, vmem=V7X_VMEM_LIMIT):
    return pltpu.CompilerParams(dimension_semantics=sem, vmem_limit_bytes=vmem)


def _dot(a, b):
    return jnp.dot(a.astype(BF16), b.astype(BF16), preferred_element_type=F32)


def _dot_nt(a, b):
    return lax.dot_general(a.astype(BF16), b.astype(BF16), (((1,), (1,)), ((), ())), preferred_element_type=F32)


def _dot_tn(a, b):
    return lax.dot_general(a.astype(BF16), b.astype(BF16), (((0,), (0,)), ((), ())), preferred_element_type=F32)


def _split(x):
    hi = x.astype(BF16)
    lo = (x - hi.astype(F32)).astype(BF16)
    return hi, lo


def _sum01_left(m01, x):
    hi, lo = _split(x)
    return jnp.dot(m01, hi, preferred_element_type=F32) + jnp.dot(m01, lo, preferred_element_type=F32)


def _sum01_right(x, m01_twice):
    hi, lo = _split(x)
    return jnp.dot(jnp.concatenate([hi, lo], axis=1), m01_twice, preferred_element_type=F32)


def _rms(x):
    r = lax.rsqrt(jnp.mean(x * x, axis=-1, keepdims=True) + RMS_EPS)
    return x * r, r


def _rms_bwd(dy, xhat, r, w):
    dxh = dy * w
    dx = r * (dxh - xhat * jnp.mean(dxh * xhat, axis=-1, keepdims=True))
    return dx, dy * xhat


def _sigmoid(x):
    return 1.0 / (1.0 + jnp.exp(-x))


def _neg_softplus(z):
    return -(jnp.maximum(z, 0.0) + jnp.log(1.0 + jnp.exp(-jnp.abs(z))))


def _colsum(x):
    return jnp.sum(x, axis=0, keepdims=True)


def _load_once(src_hbm, dst_vmem):
    @pl.when(pl.program_id(0) == 0)
    def _():
        pltpu.sync_copy(src_hbm, dst_vmem)


def _zero_first(ref):
    @pl.when(pl.program_id(0) == 0)
    def _():
        ref[...] = jnp.zeros(ref.shape, ref.dtype)


def _row_spec(width, col=0):
    return pl.BlockSpec((ROW_TILE, width), lambda i, col=col: (i, col))


def _full_spec(shape):
    return pl.BlockSpec(shape, lambda *_: (0,) * len(shape))


ANY_SPEC = pl.BlockSpec(memory_space=pl.ANY)
PACK_SPEC = _full_spec((8, D_MODEL))


class Comm(NamedTuple):
    inputs: list
    out_shape: list
    aliases: dict
    scratch: list
    start: Callable
    finish: Callable


def _pallas(body, *, comm=None, edge=None, in_specs, out_specs, out_shape, scratch_shapes=(), **kw):
    if comm is None:
        return pl.pallas_call(body, in_specs=in_specs, out_specs=out_specs, out_shape=out_shape,
                              scratch_shapes=scratch_shapes, **kw)
    n_in, n_out, n_scr = len(in_specs), len(out_specs), len(scratch_shapes)
    c_in, c_out = len(comm.inputs), len(comm.out_shape)

    def both(*refs):
        ins, c_ins = refs[:n_in], refs[n_in:n_in + c_in]
        outs = refs[n_in + c_in:n_in + c_in + n_out]
        c_outs = refs[n_in + c_in + n_out:n_in + c_in + n_out + c_out]
        rest = refs[n_in + c_in + n_out + c_out:]
        scr, c_scr = rest[:n_scr], rest[n_scr:]
        first, last = edge()

        @pl.when(first)
        def _():
            comm.start(c_ins, c_outs, c_scr)

        body(*ins, *outs, *scr)

        @pl.when(last)
        def _():
            comm.finish(c_ins, c_outs, c_scr)

    call = pl.pallas_call(
        both, in_specs=list(in_specs) + [ANY_SPEC] * c_in, out_specs=list(out_specs) + [ANY_SPEC] * c_out,
        out_shape=list(out_shape) + list(comm.out_shape), scratch_shapes=list(scratch_shapes) + list(comm.scratch),
        input_output_aliases={n_in + a: n_out + b for a, b in comm.aliases.items()}, **kw)
    return lambda *args: call(*args, *comm.inputs)


def _grid_edge(steps):
    return lambda: (pl.program_id(0) == 0, pl.program_id(0) == steps - 1)


def in_proj_fwd(x, g_pre, w_in, comm=None):
    T = x.shape[0]
    pw = w_in.shape[2]

    def body(x_ref, g_ref, w_hbm, ph_ref, sqkv_ref, u_ref, w_vmem, proj_s):
        _load_once(w_hbm, w_vmem)
        xh, _ = _rms(x_ref[...])
        u = (xh * g_ref[...]).astype(BF16)
        u_ref[...] = u
        for q in range(N_CHIPS):
            proj_s[:, pw * q:pw * (q + 1)] = jnp.dot(u, w_vmem[q], preferred_element_type=F32)
        ph_ref[...] = proj_s[:, :4 * HG_WIDTH]
        sqkv_ref[:, :SB_WIDTH] = (proj_s[:, 4 * HG_WIDTH:4 * HG_WIDTH + SB_WIDTH] * SB_SCALE).astype(BF16)
        sqkv_ref[:, SB_WIDTH:] = proj_s[:, 4 * HG_WIDTH + SB_WIDTH:].astype(BF16)

    return _pallas(
        body, comm=comm, edge=_grid_edge(T // ROW_TILE), name="in_proj_fwd", grid=(T // ROW_TILE,),
        in_specs=[_row_spec(D_MODEL), _full_spec((1, D_MODEL)), ANY_SPEC],
        out_specs=[_row_spec(4 * HG_WIDTH), _row_spec(3 * SB_WIDTH), _row_spec(D_MODEL)],
        out_shape=[jax.ShapeDtypeStruct((T, 4 * HG_WIDTH), F32), jax.ShapeDtypeStruct((T, 3 * SB_WIDTH), BF16),
                   jax.ShapeDtypeStruct((T, D_MODEL), BF16)],
        scratch_shapes=[pltpu.VMEM(w_in.shape, BF16), pltpu.VMEM((ROW_TILE, N_CHIPS * pw), F32)],
        compiler_params=_params(("arbitrary",)),
    )(x, g_pre, w_in)


def _hg_sum_matrix():
    C = HG_CHUNK
    t = np.arange(C)[:, None]
    j = np.arange(C)[None, :]
    mats = [j <= t, j > t]
    for h in HG_LEVELS:
        start = (t // (2 * h)) * (2 * h)
        upper = (t & h) != 0
        mats.append(np.where(upper, (j >= start + h) & (j <= t), (j > t) & (j <= start + h - 1)))
    return np.concatenate(mats, 0).astype(np.float32)


def _hg_level_masks():
    C = HG_CHUNK
    t = lax.broadcasted_iota(jnp.int32, (C, C), 0)
    s = lax.broadcasted_iota(jnp.int32, (C, C), 1)
    x = t ^ s
    masks = [t == s]
    for h in HG_LEVELS:
        masks.append((x >= h) & (x < 2 * h) & (t > s))
    return masks


def _hg_gates(hq, hf, gamma):
    lb = 1.0 / (1.0 + jnp.exp(gamma[1:2, :] - gamma[0:1, :]))
    sq = _sigmoid(hq)
    q = hq * sq
    sig = _sigmoid(hf)
    nsig = _sigmoid(-hf)
    f = lb + (1.0 - lb) * sig
    k = (1.0 - lb) * nsig
    g = jnp.log(f)
    return q, k, g, dict(lb=lb, sq=sq, sig=sig, nsig=nsig, f=f)


def _hg_head_decays(A, h):
    C, K = HG_CHUNK, HG_DK
    sl = slice(K * h, K * (h + 1))
    blocks = [A[C * r:C * (r + 1), sl] for r in range(2 + len(HG_LEVELS))]
    return blocks[0], blocks[1], [None] + blocks[2:]


def _hg_products(q, k, levels):
    return [_dot_nt(q, k)] + [_dot_nt(q * a, k * a) for a in levels[1:]]


def _hg_select(prods, masks):
    sc = jnp.where(masks[0], prods[0], 0.0)
    for p, m in zip(prods[1:], masks[1:]):
        sc = jnp.where(m, p, sc)
    return sc


def hgrn2_fwd(proj_h, gamma, comm=None):
    T = proj_h.shape[0]
    C, K, H, S = HG_CHUNK, HG_DK, HG_HEADS, HG_CHUNKS_PER_STEP
    n_steps = T // (S * C)
    msum = jnp.asarray(_hg_sum_matrix(), BF16)

    def body(hq_ref, hf_ref, hi_ref, gam_ref, msum_ref, o_ref, st_ref, st_s):
        _zero_first(st_s)
        q, k, g, _ = _hg_gates(hq_ref[...], hf_ref[...], gam_ref[...])
        v = hi_ref[...]
        masks = _hg_level_masks()
        parts = []
        for s in range(S):
            rows = slice(C * s, C * (s + 1))
            A = jnp.exp(_sum01_left(msum_ref[...], g[rows]))
            for h in range(H):
                sl = slice(K * h, K * (h + 1))
                ab, ar, levels = _hg_head_decays(A, h)
                parts.append(dict(s=s, h=h, rows=rows, sl=sl, ab=ab, ar=ar, levels=levels,
                                  q=q[rows, sl], k=k[rows, sl], v=v[rows, sl]))
        for pt in parts:
            pt["prods"] = _hg_products(pt["q"], pt["k"], pt["levels"])
            pt["grown"] = _dot_tn(pt["v"], pt["k"] * pt["ar"])
        for pt in parts:
            pt["sc"] = _hg_select(pt["prods"], masks)
        state = [st_s[h] for h in range(H)]
        for pt in parts:
            h, ab = pt["h"], pt["ab"]
            o_ref[pt["rows"], pt["sl"]] = _dot_nt(pt["q"] * ab, state[h]) + _dot(pt["sc"], pt["v"])
            state[h] = state[h] * ab[C - 1:C, :] + pt["grown"]
            st_ref[pt["s"], h] = state[h]
        for h in range(H):
            st_s[h] = state[h]

    blk = lambda col: pl.BlockSpec((S * C, HG_WIDTH), lambda c, col=col: (c, col))
    return _pallas(
        body, comm=comm, edge=_grid_edge(n_steps), name="hgrn2_fwd", grid=(n_steps,),
        in_specs=[blk(0), blk(1), blk(2), _full_spec((2, HG_WIDTH)), _full_spec(msum.shape)],
        out_specs=[blk(0), pl.BlockSpec((S, H, K, K), lambda c: (c, 0, 0, 0))],
        out_shape=[jax.ShapeDtypeStruct((T, HG_WIDTH), F32), jax.ShapeDtypeStruct((S * n_steps, H, K, K), F32)],
        scratch_shapes=[pltpu.VMEM((H, K, K), F32)],
        compiler_params=_params(("arbitrary",)),
    )(proj_h, proj_h, proj_h, gamma, msum)


def hgrn2_bwd(proj_h, gamma, states, do, comm=None):
    T = proj_h.shape[0]
    C, K, H, S = HG_CHUNK, HG_DK, HG_HEADS, HG_CHUNKS_PER_STEP
    n_steps = T // (S * C)
    n_sums = 2 + len(HG_LEVELS)
    msum = jnp.asarray(_hg_sum_matrix(), BF16)
    msum_t = jnp.asarray(_hg_sum_matrix().T, BF16)

    def body(hq_ref, hf_ref, hi_ref, do_ref, gam_ref, msum_ref, msum_t_ref, st_prev_ref, st_ref,
             dhq_ref, dhf_ref, dhi_ref, pack_ref, dst_s, dlb_s, dq_s, dk_s, de_s):
        step = pl.program_id(0)
        _zero_first(dst_s)
        _zero_first(dlb_s)
        _zero_first(pack_ref)
        hq = hq_ref[...]
        q, k, g, aux = _hg_gates(hq, hf_ref[...], gam_ref[...])
        v = hi_ref[...]
        do_all = do_ref[...]
        masks = _hg_level_masks()
        is_last_row = lax.broadcasted_iota(jnp.int32, (C, K), 0) == C - 1
        has_prev = (step < n_steps - 1).astype(F32)
        parts = []
        for s in reversed(range(S)):
            rows = slice(C * s, C * (s + 1))
            A = jnp.exp(_sum01_left(msum_ref[...], g[rows]))
            for h in range(H):
                sl = slice(K * h, K * (h + 1))
                ab, ar, levels = _hg_head_decays(A, h)
                st_in = st_prev_ref[0, h] * has_prev if s == 0 else st_ref[s - 1, h]
                parts.append(dict(s=s, h=h, rows=rows, sl=sl, ab=ab, ar=ar, levels=levels, st_in=st_in,
                                  q=q[rows, sl], k=k[rows, sl], v=v[rows, sl], do=do_all[rows, sl]))
        for pt in parts:
            pt["prods"] = _hg_products(pt["q"], pt["k"], pt["levels"])
            pt["da"] = _dot_nt(pt["do"], pt["v"])
            pt["t1"] = pt["ab"] * _dot(pt["do"], pt["st_in"])
            pt["dst_add"] = _dot_tn(pt["do"], pt["q"] * pt["ab"])
        dstate = [dst_s[h] for h in range(H)]
        for pt in parts:
            h = pt["h"]
            pt["dst_out"] = dstate[h]
            pt["t2"] = pt["ar"] * _dot(pt["v"], dstate[h])
            pt["dv_state"] = _dot_nt(pt["k"] * pt["ar"], dstate[h])
            dstate[h] = dstate[h] * pt["ab"][C - 1:C, :] + pt["dst_add"]
        for h in range(H):
            dst_s[h] = dstate[h]
        for pt in parts:
            pt["sc"] = _hg_select(pt["prods"], masks)
            pt["dam"] = [jnp.where(m, pt["da"], 0.0) for m in masks]
        for pt in parts:
            qh, kh = pt["q"], pt["k"]
            pt["dq_parts"] = [_dot(pt["dam"][0], kh)] + [
                a * _dot(dam, kh * a) for a, dam in zip(pt["levels"][1:], pt["dam"][1:])]
            pt["dk_parts"] = [_dot_tn(pt["dam"][0], qh)] + [
                a * _dot_tn(dam, qh * a) for a, dam in zip(pt["levels"][1:], pt["dam"][1:])]
            pt["dv_intra"] = _dot_tn(pt["sc"], pt["do"])
        for pt in parts:
            s, rows, sl, qh, kh, ab = pt["s"], pt["rows"], pt["sl"], pt["q"], pt["k"], pt["ab"]
            decayed = _colsum(pt["st_in"] * pt["dst_out"]) * ab[C - 1:C, :]
            de_s[s, 0:C, sl] = qh * pt["t1"] + jnp.where(is_last_row, decayed, 0.0)
            de_s[s, C:2 * C, sl] = kh * pt["t2"]
            dq = pt["t1"] + pt["dq_parts"][0]
            dk = pt["t2"] + pt["dk_parts"][0]
            for r, (t1, t2) in enumerate(zip(pt["dq_parts"][1:], pt["dk_parts"][1:])):
                dq = dq + t1
                dk = dk + t2
                de_s[s, C * (r + 2):C * (r + 3), sl] = qh * t1 + kh * t2
            dhi_ref[rows, sl] = pt["dv_intra"] + pt["dv_state"]
            dq_s[rows, sl] = dq
            dk_s[rows, sl] = dk
        dg = jnp.concatenate([_sum01_left(msum_t_ref[...], de_s[s]) for s in range(S)], axis=0)
        dk = dk_s[...]
        sq, lb = aux["sq"], aux["lb"]
        dhq_ref[...] = dq_s[...] * (sq * (1.0 + hq * (1.0 - sq)))
        common = dg / aux["f"] - dk
        dhf_ref[...] = (1.0 - lb) * aux["sig"] * aux["nsig"] * common
        dlb_s[...] += _colsum(aux["nsig"] * common)

        @pl.when(step == n_steps - 1)
        def _():
            dgam = lb * (1.0 - lb) * dlb_s[...]
            pack_ref[ROW_GAMMA:ROW_GAMMA + 1, :HG_WIDTH] = dgam
            pack_ref[ROW_GAMMA:ROW_GAMMA + 1, HG_WIDTH:] = -dgam

    last = n_steps - 1
    blk = lambda col: pl.BlockSpec((S * C, HG_WIDTH), lambda c, col=col: (last - c, col))
    return _pallas(
        body, comm=comm, edge=_grid_edge(n_steps), name="hgrn2_bwd", grid=(n_steps,),
        in_specs=[blk(0), blk(1), blk(2), blk(0), _full_spec((2, HG_WIDTH)), _full_spec(msum.shape),
                  _full_spec(msum_t.shape),
                  pl.BlockSpec((1, H, K, K), lambda c: (jnp.maximum(S * (last - c) - 1, 0), 0, 0, 0)),
                  pl.BlockSpec((S, H, K, K), lambda c: (last - c, 0, 0, 0))],
        out_specs=[blk(0), blk(0), blk(0), PACK_SPEC],
        out_shape=[jax.ShapeDtypeStruct((T, HG_WIDTH), F32)] * 3 + [jax.ShapeDtypeStruct((8, D_MODEL), F32)],
        scratch_shapes=[pltpu.VMEM((H, K, K), F32), pltpu.VMEM((1, HG_WIDTH), F32), pltpu.VMEM((S * C, HG_WIDTH), F32),
                        pltpu.VMEM((S * C, HG_WIDTH), F32), pltpu.VMEM((S, n_sums * C, HG_WIDTH), F32)],
        compiler_params=_params(("arbitrary",)),
    )(proj_h, proj_h, proj_h, do, gamma, msum, msum_t, states, states)


def _sb_sum_matrix(inclusive):
    B = SB_BLOCK
    j = np.arange(B)[:, None]
    s = np.arange(B)[None, :]
    tri = (j >= s) if inclusive else (j > s)
    once = np.concatenate([tri, np.ones((B, B), bool)], 1).astype(np.float32)
    return np.concatenate([once, once], 0)


def _sb_prefix_matrix(inclusive):
    B = SB_BLOCK
    j = np.arange(B)[:, None]
    s = np.arange(B)[None, :]
    tri = (j <= s) if inclusive else (j < s)
    once = np.concatenate([tri, np.ones((B, B), bool)], 1).astype(np.float32)
    return np.concatenate([once, once], 0)


def _sb_iotas():
    shape = (SB_BLOCK, SB_BLOCK)
    return lax.broadcasted_iota(jnp.int32, shape, 0), lax.broadcasted_iota(jnp.int32, shape, 1)


def _sb_heads(q, first):
    heads = []
    for g in range(SB_GROUP):
        qg = q[:, SB_BLOCK * g:SB_BLOCK * (g + 1)]
        zero = jnp.zeros_like(qg)
        heads += [(g, jnp.where(first, qg, zero)), (g, jnp.where(first, zero, qg))]
    return heads


def _lanes(x, g):
    return x[:, SB_BLOCK * g:SB_BLOCK * (g + 1)]


def sb_fwd(sqkv, comm=None):
    T = sqkv.shape[0]
    B = SB_BLOCK
    W = SB_GROUP * B
    groups = SB_WIDTH // W
    usum = jnp.asarray(_sb_sum_matrix(False), BF16)

    def body(q_ref, k_ref, v_ref, u_ref, o_ref, tl_ref, first_ref):
        p, i = pl.program_id(0), pl.program_id(1)
        row, lane = _sb_iotas()
        first = lane < SB_DH
        heads = _sb_heads(q_ref[...], first)
        u = u_ref[...]

        def more(loop):
            n, reachable, _ = loop
            return (SB_UNROLL * n <= i) & (reachable > 0)

        def step(loop):
            n, _, state = loop
            blocks = []
            for sub in range(SB_UNROLL):
                j = i - SB_UNROLL * n - sub
                off = pl.multiple_of(jnp.maximum(j, 0) * B, B)
                valid = ((lane + j * B) < (row + i * B)) & (j >= 0)
                blocks.append((k_ref[pl.ds(off, B), :], v_ref[pl.ds(off, B), :], valid))
            z = [[_dot_nt(qh, _lanes(kj, g)) for g, qh in heads] for kj, _, _ in blocks]
            lnb = [[jnp.where(valid, _neg_softplus(zz), 0.0) for zz in zs] for zs, (_, _, valid) in zip(z, blocks)]
            sums = [[_sum01_right(x, u) for x in xs] for xs in lnb]
            out = []
            for h, (carry, acc) in enumerate(state):
                for sub, (_, vj, valid) in enumerate(blocks):
                    expo = z[sub][h] + lnb[sub][h] + carry + sums[sub][h][:, :B]
                    acc = acc + _dot(jnp.where(valid, jnp.exp(expo), 0.0), _lanes(vj, heads[h][0]))
                    carry = carry + sums[sub][h][:, B:]
                out.append((carry, acc))
            state = tuple(out)
            worst = state[0][0]
            for carry, _ in state[1:]:
                worst = jnp.maximum(worst, carry)
            reachable = (jnp.max(worst) > SB_UNDERFLOW_LOG).astype(jnp.int32)
            return n + 1, reachable, state

        zero = jnp.zeros((B, B), F32)
        done, _, state = lax.while_loop(
            more, step, (jnp.int32(0), jnp.int32(1), tuple((zero, zero) for _ in heads)))
        for g in range(SB_GROUP):
            (tot0, acc0), (tot1, acc1) = state[2 * g], state[2 * g + 1]
            o_ref[:, B * g:B * (g + 1)] = jnp.where(first, acc0, acc1)
            tl_ref[:, B * g:B * (g + 1)] = jnp.where(first, tot0, tot1)
        first_ref[p, i] = jnp.maximum(i + 1 - SB_UNROLL * done, 0)

    def edge():
        p, i = pl.program_id(0), pl.program_id(1)
        return (p == 0) & (i == 0), (p == groups - 1) & (i == T // B - 1)

    return _pallas(
        body, comm=comm, edge=edge, name="sb_fwd", grid=(groups, T // B),
        in_specs=[pl.BlockSpec((B, W), lambda p, i: (i, p)),
                  pl.BlockSpec((T, W), lambda p, i: (0, groups + p)),
                  pl.BlockSpec((T, W), lambda p, i: (0, 2 * groups + p)),
                  pl.BlockSpec(usum.shape, lambda p, i: (0, 0))],
        out_specs=[pl.BlockSpec((B, W), lambda p, i: (i, p))] * 2 + [pl.BlockSpec(memory_space=pltpu.SMEM)],
        out_shape=[jax.ShapeDtypeStruct((T, SB_WIDTH), F32)] * 2 + [jax.ShapeDtypeStruct((groups, T // B), jnp.int32)],
        compiler_params=_params(("arbitrary", "arbitrary")),
    )(sqkv, sqkv, sqkv, usum)


def sb_bwd(sqkv, do, tl, first_block):
    T = sqkv.shape[0]
    B = SB_BLOCK
    W = SB_GROUP * B
    groups = SB_WIDTH // W
    upre = jnp.asarray(_sb_prefix_matrix(True), BF16)
    uexc = jnp.asarray(_sb_prefix_matrix(False), BF16)

    def body(q_ref, k_ref, v_ref, do_ref, tl_ref, up_ref, ue_ref, first_ref, dq_ref, dk_ref, dv_ref):
        p, i = pl.program_id(0), pl.program_id(1)

        @pl.when(i == 0)
        def _():
            dk_ref[...] = jnp.zeros(dk_ref.shape, F32)
            dv_ref[...] = jnp.zeros(dv_ref.shape, F32)

        row, lane = _sb_iotas()
        first = lane < SB_DH
        do = do_ref[...]
        tl_all = tl_ref[...]
        heads = []
        for (g, qh), at in zip(_sb_heads(q_ref[...], first), (0, B - 1) * SB_GROUP):
            dog = _lanes(do, g)
            keep = first if at == 0 else jnp.logical_not(first)
            heads.append((g, qh, jnp.where(keep, dog, jnp.zeros_like(dog)).astype(BF16),
                          _lanes(tl_all, g)[:, at:at + 1]))
        up = up_ref[...]
        ue = ue_ref[...]
        start = first_ref[p, i]

        def step(n, state):
            blocks = []
            for sub in range(SB_UNROLL):
                j = start + SB_UNROLL * n + sub
                off = pl.multiple_of(jnp.minimum(j, i) * B, B)
                valid = (lane + j * B) < (row + i * B)
                blocks.append((off, k_ref[pl.ds(off, B), :], v_ref[pl.ds(off, B), :], valid))
            combos = [(s, h) for s in range(SB_UNROLL) for h in range(len(heads))]
            z = {(s, h): _dot_nt(heads[h][1], _lanes(blocks[s][1], heads[h][0])) for s, h in combos}
            da = {(s, h): _dot_nt(heads[h][2], _lanes(blocks[s][2], heads[h][0])) for s, h in combos}
            lnb = {c: jnp.where(blocks[c[0]][3], _neg_softplus(z[c]), 0.0) for c in combos}
            lb = {c: z[c] + lnb[c] for c in combos}
            sums = {c: _sum01_right(lnb[c], up) for c in combos}
            a, w = {}, {}
            seen = [st[0] for st in state]
            for s, h in combos:
                expo = lb[s, h] + (heads[h][3] - seen[h] - sums[s, h][:, :B])
                a[s, h] = jnp.where(blocks[s][3], jnp.exp(expo), 0.0)
                w[s, h] = a[s, h] * da[s, h]
                seen[h] = seen[h] + sums[s, h][:, B:]
            wsums = {c: _sum01_right(w[c], ue) for c in combos}
            dz = {}
            seen_w = [st[1] for st in state]
            for s, h in combos:
                beta = jnp.exp(lb[s, h])
                before = seen_w[h] + wsums[s, h][:, :B]
                dz[s, h] = jnp.where(blocks[s][3], w[s, h] * (1.0 - beta) - before * beta, 0.0)
                seen_w[h] = seen_w[h] + wsums[s, h][:, B:]
            dq = [st[2] for st in state]
            for s, h in combos:
                dq[h] = dq[h] + _dot(dz[s, h], _lanes(blocks[s][1], heads[h][0]))
            for s in range(SB_UNROLL):
                off = blocks[s][0]
                for g in range(SB_GROUP):
                    h0, h1 = 2 * g, 2 * g + 1
                    dk_ref[pl.ds(off, B), B * g:B * (g + 1)] += (_dot_tn(dz[s, h0], heads[h0][1])
                                                                 + _dot_tn(dz[s, h1], heads[h1][1]))
                    dv_ref[pl.ds(off, B), B * g:B * (g + 1)] += (_dot_tn(a[s, h0], heads[h0][2])
                                                                 + _dot_tn(a[s, h1], heads[h1][2]))
            return tuple(zip(seen, seen_w, dq))

        zero = jnp.zeros((B, B), F32)
        trips = (i - start + SB_UNROLL) // SB_UNROLL
        state = lax.fori_loop(0, trips, step, tuple((zero, zero, zero) for _ in heads))
        for g in range(SB_GROUP):
            dq_ref[:, B * g:B * (g + 1)] = jnp.where(first, state[2 * g][2], state[2 * g + 1][2]) * SB_SCALE

    qblk = pl.BlockSpec((B, W), lambda p, i: (i, p))
    full = pl.BlockSpec((T, W), lambda p, i: (0, p))
    return pl.pallas_call(
        body, name="sb_bwd", grid=(groups, T // B),
        in_specs=[qblk, pl.BlockSpec((T, W), lambda p, i: (0, groups + p)),
                  pl.BlockSpec((T, W), lambda p, i: (0, 2 * groups + p)), qblk, qblk,
                  pl.BlockSpec(upre.shape, lambda p, i: (0, 0)), pl.BlockSpec(uexc.shape, lambda p, i: (0, 0)),
                  pl.BlockSpec(memory_space=pltpu.SMEM)],
        out_specs=[qblk, full, full],
        out_shape=[jax.ShapeDtypeStruct((T, SB_WIDTH), F32)] * 3,
        compiler_params=_params(("arbitrary", "arbitrary")),
    )(sqkv, sqkv, sqkv, do, tl, upre, uexc, first_block)


def _mixer_out(o_hg, hg, o_sb, g_hg, g_sb):
    n_hg, r_hg = _rms(o_hg)
    s_hg = _sigmoid(hg)
    n_sb, r_sb = _rms(o_sb)
    return dict(n_hg=n_hg, r_hg=r_hg, s_hg=s_hg, n_sb=n_sb, r_sb=r_sb,
                y_hg=n_hg * g_hg * (hg * s_hg), y_sb=n_sb * g_sb)


def mix_out_fwd(o_hg, proj_h, o_sb, x, norms, g_post, w_out, comm=None):
    T = x.shape[0]

    def body(ohg_ref, hg_ref, osb_ref, x_ref, nrm_ref, gp_ref, w_hbm, cat_ref, mix_ref, h1_ref, w_vmem):
        _load_once(w_hbm, w_vmem)
        nrm = nrm_ref[...]
        m = _mixer_out(ohg_ref[...], hg_ref[...], osb_ref[...], nrm[:, :HG_WIDTH], nrm[:, HG_WIDTH:])
        cat_ref[:, :HG_WIDTH] = m["y_hg"].astype(BF16)
        cat_ref[:, HG_WIDTH:] = m["y_sb"].astype(BF16)
        mix = jnp.dot(cat_ref[...], w_vmem[...], preferred_element_type=F32)
        mix_ref[...] = mix
        mh, _ = _rms(mix)
        h1_ref[...] = x_ref[...] + mh * gp_ref[...]

    return _pallas(
        body, comm=comm, edge=_grid_edge(T // ROW_TILE), name="mix_out_fwd", grid=(T // ROW_TILE,),
        in_specs=[_row_spec(HG_WIDTH), _row_spec(HG_WIDTH, 3), _row_spec(SB_WIDTH), _row_spec(D_MODEL),
                  _full_spec((1, D_MODEL)), _full_spec((1, D_MODEL)), ANY_SPEC],
        out_specs=[_row_spec(D_MODEL)] * 3,
        out_shape=[jax.ShapeDtypeStruct((T, D_MODEL), BF16), jax.ShapeDtypeStruct((T, D_MODEL), F32),
                   jax.ShapeDtypeStruct((T, D_MODEL), F32)],
        scratch_shapes=[pltpu.VMEM(w_out.shape, BF16)],
        compiler_params=_params(("arbitrary",)),
    )(o_hg, proj_h, o_sb, x, norms, g_post, w_out)


def ffn_fwd(h1, g_pre, g_post, w_gu, w_down):
    T = h1.shape[0]
    pw = w_gu.shape[2]

    def body(h1_ref, gpre_ref, gpost_ref, wgu_hbm, wd_hbm, u2_ref, gu_ref, act_ref, y_ref, h2_ref,
             wgu_vmem, wd_vmem, gu_s):
        _load_once(wgu_hbm, wgu_vmem)
        _load_once(wd_hbm, wd_vmem)
        h1v = h1_ref[...]
        hh, _ = _rms(h1v)
        u2 = (hh * gpre_ref[...]).astype(BF16)
        u2_ref[...] = u2
        for q in range(N_CHIPS):
            gu_s[:, pw * q:pw * (q + 1)] = jnp.dot(u2, wgu_vmem[q], preferred_element_type=F32)
        gu_ref[...] = gu_s[...].astype(BF16)
        gate = gu_s[:, :D_FF]
        act = (gate * _sigmoid(gate) * gu_s[:, D_FF:]).astype(BF16)
        act_ref[...] = act
        y = jnp.dot(act, wd_vmem[...], preferred_element_type=F32)
        y_ref[...] = y
        yh, _ = _rms(y)
        h2_ref[...] = h1v + yh * gpost_ref[...]

    return pl.pallas_call(
        body, name="ffn_fwd", grid=(T // ROW_TILE,),
        in_specs=[_row_spec(D_MODEL), _full_spec((1, D_MODEL)), _full_spec((1, D_MODEL)), ANY_SPEC, ANY_SPEC],
        out_specs=[_row_spec(D_MODEL), _row_spec(2 * D_FF), _row_spec(D_FF), _row_spec(D_MODEL), _row_spec(D_MODEL)],
        out_shape=[jax.ShapeDtypeStruct((T, D_MODEL), BF16), jax.ShapeDtypeStruct((T, 2 * D_FF), BF16),
                   jax.ShapeDtypeStruct((T, D_FF), BF16), jax.ShapeDtypeStruct((T, D_MODEL), F32),
                   jax.ShapeDtypeStruct((T, D_MODEL), F32)],
        scratch_shapes=[pltpu.VMEM(w_gu.shape, BF16), pltpu.VMEM(w_down.shape, BF16),
                        pltpu.VMEM((ROW_TILE, 2 * D_FF), F32)],
        compiler_params=_params(("arbitrary",)),
    )(h1, g_pre, g_post, w_gu, w_down)


def ple_loss(h2, p, target, w_ple, w_pg):
    T = h2.shape[0]
    pw = w_ple.shape[2]

    def body(h2_ref, p_ref, t_ref, wple_hbm, wpg_hbm, de_ref, ds_ref, dh2_ref, h2b_ref, pb_ref, pack_ref,
             wple_vmem, wpg_vmem, e_s):
        _load_once(wple_hbm, wple_vmem)
        _load_once(wpg_hbm, wpg_vmem)
        _zero_first(pack_ref)
        h2v = h2_ref[...]
        h2b = h2v.astype(BF16)
        h2b_ref[...] = h2b
        pb = p_ref[...].astype(BF16)
        pb_ref[...] = pb
        for q in range(N_CHIPS):
            e_s[:, pw * q:pw * (q + 1)] = jnp.dot(pb, wple_vmem[q], preferred_element_type=F32)
        e = e_s[...]
        sig = _sigmoid(jnp.dot(h2b, wpg_vmem[...], preferred_element_type=F32))
        err = h2v + e * sig - t_ref[...]
        part = 0.5 * jnp.sum(jnp.mean(err * err, axis=-1, keepdims=True), axis=0, keepdims=True)
        lane = lax.broadcasted_iota(jnp.int32, (1, D_MODEL), 1)
        pack_ref[ROW_LOSS:ROW_LOSS + 1, :] += jnp.where(lane == 0, part, 0.0)
        dh3 = err * (1.0 / D_MODEL)
        de_ref[...] = (dh3 * sig).astype(BF16)
        ds = (dh3 * e * sig * (1.0 - sig)).astype(BF16)
        ds_ref[...] = ds
        dh2_ref[...] = dh3 + _dot_nt(ds, wpg_vmem[...])

    return pl.pallas_call(
        body, name="ple_loss", grid=(T // ROW_TILE,),
        in_specs=[_row_spec(D_MODEL), _row_spec(p.shape[1]), _row_spec(D_MODEL), ANY_SPEC, ANY_SPEC],
        out_specs=[_row_spec(D_MODEL), _row_spec(D_MODEL), _row_spec(D_MODEL), _row_spec(D_MODEL),
                   _row_spec(p.shape[1]), PACK_SPEC],
        out_shape=[jax.ShapeDtypeStruct((T, D_MODEL), BF16), jax.ShapeDtypeStruct((T, D_MODEL), BF16),
                   jax.ShapeDtypeStruct((T, D_MODEL), F32), jax.ShapeDtypeStruct((T, D_MODEL), BF16),
                   jax.ShapeDtypeStruct(p.shape, BF16), jax.ShapeDtypeStruct((8, D_MODEL), F32)],
        scratch_shapes=[pltpu.VMEM(w_ple.shape, BF16), pltpu.VMEM(w_pg.shape, BF16), pltpu.VMEM((ROW_TILE, D_MODEL), F32)],
        compiler_params=_params(("arbitrary",)),
    )(h2, p, target, w_ple, w_pg)


def ffn_bwd(dh2, y, h1, gu, g_pre, g_post, w_gu, w_down):
    T = h1.shape[0]
    pw = w_gu.shape[2]

    def body(dh2_ref, y_ref, h1_ref, gu_ref, gpre_ref, gpost_ref, wgu_hbm, wd_hbm, dy_ref, dgu_ref, dh1_ref, pack_ref,
             wgu_vmem, wd_vmem):
        _load_once(wgu_hbm, wgu_vmem)
        _load_once(wd_hbm, wd_vmem)
        _zero_first(pack_ref)
        dh2v = dh2_ref[...]
        yh, ry = _rms(y_ref[...])
        dy, dw = _rms_bwd(dh2v, yh, ry, gpost_ref[...])
        pack_ref[ROW_FFN_POST:ROW_FFN_POST + 1, :] += _colsum(dw)
        dyb = dy.astype(BF16)
        dy_ref[...] = dyb
        dact = _dot_nt(dyb, wd_vmem[...])
        gate = gu_ref[:, :D_FF].astype(F32)
        up = gu_ref[:, D_FF:].astype(F32)
        sg = _sigmoid(gate)
        dgu_ref[:, :D_FF] = (dact * up * (sg * (1.0 + gate * (1.0 - sg)))).astype(BF16)
        dgu_ref[:, D_FF:] = (dact * gate * sg).astype(BF16)
        du2 = _dot_nt(dgu_ref[:, :pw], wgu_vmem[0])
        for q in range(1, N_CHIPS):
            du2 = du2 + _dot_nt(dgu_ref[:, pw * q:pw * (q + 1)], wgu_vmem[q])
        hh, rh = _rms(h1_ref[...])
        dh, dw = _rms_bwd(du2, hh, rh, gpre_ref[...])
        pack_ref[ROW_FFN_PRE:ROW_FFN_PRE + 1, :] += _colsum(dw)
        dh1_ref[...] = dh2v + dh

    return pl.pallas_call(
        body, name="ffn_bwd", grid=(T // ROW_TILE,),
        in_specs=[_row_spec(D_MODEL), _row_spec(D_MODEL), _row_spec(D_MODEL), _row_spec(2 * D_FF),
                  _full_spec((1, D_MODEL)), _full_spec((1, D_MODEL)), ANY_SPEC, ANY_SPEC],
        out_specs=[_row_spec(D_MODEL), _row_spec(2 * D_FF), _row_spec(D_MODEL), PACK_SPEC],
        out_shape=[jax.ShapeDtypeStruct((T, D_MODEL), BF16), jax.ShapeDtypeStruct((T, 2 * D_FF), BF16),
                   jax.ShapeDtypeStruct((T, D_MODEL), F32), jax.ShapeDtypeStruct((8, D_MODEL), F32)],
        scratch_shapes=[pltpu.VMEM(w_gu.shape, BF16), pltpu.VMEM(w_down.shape, BF16)],
        compiler_params=_params(("arbitrary",)),
    )(dh2, y, h1, gu, g_pre, g_post, w_gu, w_down)


def mix_out_bwd(dh1, mix, o_hg, proj_h, o_sb, norms, g_post, w_out, comm=None):
    T = dh1.shape[0]

    def body(dh1_ref, mix_ref, ohg_ref, hg_ref, osb_ref, nrm_ref, gp_ref, w_hbm, dmix_ref, dohg_ref, dhg_ref, dosb_ref,
             pack_ref, w_vmem):
        _load_once(w_hbm, w_vmem)
        _zero_first(pack_ref)
        mh, rm = _rms(mix_ref[...])
        dmix, dw = _rms_bwd(dh1_ref[...], mh, rm, gp_ref[...])
        pack_ref[ROW_ATTN_POST:ROW_ATTN_POST + 1, :] += _colsum(dw)
        dmb = dmix.astype(BF16)
        dmix_ref[...] = dmb
        dcat = _dot_nt(dmb, w_vmem[...])
        nrm = nrm_ref[...]
        g_hg, g_sb = nrm[:, :HG_WIDTH], nrm[:, HG_WIDTH:]
        hg = hg_ref[...]
        m = _mixer_out(ohg_ref[...], hg, osb_ref[...], g_hg, g_sb)
        d_hg = dcat[:, :HG_WIDTH]
        silu = hg * m["s_hg"]
        dhg_ref[...] = d_hg * (m["n_hg"] * g_hg) * (m["s_hg"] * (1.0 + hg * (1.0 - m["s_hg"])))
        dx, dw = _rms_bwd(d_hg * silu, m["n_hg"], m["r_hg"], g_hg)
        dohg_ref[...] = dx
        pack_ref[ROW_MIX_NORMS:ROW_MIX_NORMS + 1, :HG_WIDTH] += _colsum(dw)
        dx, dw = _rms_bwd(dcat[:, HG_WIDTH:], m["n_sb"], m["r_sb"], g_sb)
        dosb_ref[...] = dx
        pack_ref[ROW_MIX_NORMS:ROW_MIX_NORMS + 1, HG_WIDTH:] += _colsum(dw)

    return _pallas(
        body, comm=comm, edge=_grid_edge(T // ROW_TILE), name="mix_out_bwd", grid=(T // ROW_TILE,),
        in_specs=[_row_spec(D_MODEL), _row_spec(D_MODEL), _row_spec(HG_WIDTH), _row_spec(HG_WIDTH, 3), _row_spec(SB_WIDTH),
                  _full_spec((1, D_MODEL)), _full_spec((1, D_MODEL)), ANY_SPEC],
        out_specs=[_row_spec(D_MODEL), _row_spec(HG_WIDTH), _row_spec(HG_WIDTH), _row_spec(SB_WIDTH), PACK_SPEC],
        out_shape=[jax.ShapeDtypeStruct((T, D_MODEL), BF16), jax.ShapeDtypeStruct((T, HG_WIDTH), F32),
                   jax.ShapeDtypeStruct((T, HG_WIDTH), F32), jax.ShapeDtypeStruct((T, SB_WIDTH), F32),
                   jax.ShapeDtypeStruct((8, D_MODEL), F32)],
        scratch_shapes=[pltpu.VMEM(w_out.shape, BF16)],
        compiler_params=_params(("arbitrary",)),
    )(dh1, mix, o_hg, proj_h, o_sb, norms, g_post, w_out)


def in_proj_bwd(parts, x, dh1, g_pre, w_in, comm=None):
    T = x.shape[0]
    pw = w_in.shape[2]
    n_parts = len(parts)

    def body(*refs):
        part_refs = refs[:n_parts]
        x_ref, dh1_ref, g_ref, w_hbm, dproj_ref, dx_ref, pack_ref, w_vmem = refs[n_parts:]
        _load_once(w_hbm, w_vmem)
        _zero_first(pack_ref)
        for n, ref in enumerate(part_refs):
            dproj_ref[:, HG_WIDTH * n:HG_WIDTH * (n + 1)] = ref[...].astype(BF16)
        du = _dot_nt(dproj_ref[:, :pw], w_vmem[0])
        for q in range(1, N_CHIPS):
            du = du + _dot_nt(dproj_ref[:, pw * q:pw * (q + 1)], w_vmem[q])
        xh, r = _rms(x_ref[...])
        dx, dw = _rms_bwd(du, xh, r, g_ref[...])
        pack_ref[ROW_ATTN_PRE:ROW_ATTN_PRE + 1, :] += _colsum(dw)
        dx_ref[...] = dh1_ref[...] + dx

    return _pallas(
        body, comm=comm, edge=_grid_edge(T // ROW_TILE), name="in_proj_bwd", grid=(T // ROW_TILE,),
        in_specs=[_row_spec(HG_WIDTH)] * n_parts + [_row_spec(D_MODEL), _row_spec(D_MODEL), _full_spec((1, D_MODEL)), ANY_SPEC],
        out_specs=[_row_spec(n_parts * HG_WIDTH), _row_spec(D_MODEL), PACK_SPEC],
        out_shape=[jax.ShapeDtypeStruct((T, n_parts * HG_WIDTH), BF16), jax.ShapeDtypeStruct((T, D_MODEL), F32),
                   jax.ShapeDtypeStruct((8, D_MODEL), F32)],
        scratch_shapes=[pltpu.VMEM(w_in.shape, BF16)],
        compiler_params=_params(("arbitrary",)),
    )(*parts, x, dh1, g_pre, w_in)


def weight_grad(a, g, name, *, tm, tn, tk=512, col_pieces=False, comm=None):
    T, M = a.shape
    N = g.shape[1]
    tk = min(tk, T)
    steps = T // tk

    def body(a_ref, g_ref, o_ref):
        @pl.when(pl.program_id(2) == 0)
        def _():
            o_ref[...] = jnp.zeros(o_ref.shape, F32)

        o_ref[...] += _dot_tn(a_ref[...], g_ref[...]).reshape(o_ref.shape)

    if col_pieces:
        out_shape = jax.ShapeDtypeStruct((N // tn, M, tn), F32)
        out_spec = pl.BlockSpec((1, tm, tn), lambda i, j, k: (j, i, 0))
    else:
        out_shape = jax.ShapeDtypeStruct((M, N), F32)
        out_spec = pl.BlockSpec((tm, tn), lambda i, j, k: (i, j))
    grid = (M // tm, N // tn, steps)

    def edge():
        at = [pl.program_id(d) for d in range(3)]
        return ((at[0] == 0) & (at[1] == 0) & (at[2] == 0),
                (at[0] == grid[0] - 1) & (at[1] == grid[1] - 1) & (at[2] == grid[2] - 1))

    return _pallas(
        body, comm=comm, edge=edge, name=name, grid=grid,
        in_specs=[pl.BlockSpec((tk, tm), lambda i, j, k: (k, i)), pl.BlockSpec((tk, tn), lambda i, j, k: (k, j))],
        out_specs=[out_spec], out_shape=[out_shape],
        compiler_params=_params(("arbitrary", "arbitrary", "arbitrary")),
    )(a, g)


def _place():
    x, y, c = lax.axis_index("x"), lax.axis_index("y"), lax.axis_index("c")
    chips = [(1 - x, y), (x, 1 - y), (1 - x, 1 - y)]
    return x, y, c, chips


def _chip_index(cx, cy):
    return 2 * cx + cy


def _rcopy(src, dst, send_sem, recv_sem, device):
    return pltpu.make_async_remote_copy(src_ref=src, dst_ref=dst, send_sem=send_sem, recv_sem=recv_sem,
                                        device_id=device, device_id_type=MESH)


def gather_weights(shards):
    n = len(shards)

    def body(*refs):
        ins, outs = refs[:n], refs[2 * n:3 * n]
        send_sems, recv_sems = refs[3 * n:]
        x, y, c, chips = _place()
        me = _chip_index(x, y)
        sibling = (x, y, 1 - c)

        def rows(w, core):
            half = ins[w].shape[0] // 2
            return pl.ds(core * half, half)

        sends = []
        for w in range(n):
            for j, chip in enumerate(chips):
                sends.append(_rcopy(ins[w].at[rows(w, c)], outs[w].at[me, rows(w, c)],
                                    send_sems.at[6 * w + j], recv_sems.at[6 * w + j], (*chip, c)))
        for cp in sends:
            cp.start()
        passed = []
        for w in range(n):
            for j, chip in enumerate(chips):
                block = outs[w].at[_chip_index(*chip), rows(w, c)]
                _rcopy(block, block, send_sems.at[6 * w + j], recv_sems.at[6 * w + j], (*chip, c)).wait_recv()
                cp = _rcopy(block, block, send_sems.at[6 * w + 3 + j], recv_sems.at[6 * w + 3 + j], sibling)
                cp.start()
                passed.append(cp)
        for w in range(n):
            for j, chip in enumerate(chips):
                block = outs[w].at[_chip_index(*chip), rows(w, 1 - c)]
                _rcopy(block, block, send_sems.at[6 * w + 3 + j], recv_sems.at[6 * w + 3 + j], sibling).wait_recv()
        for cp in sends + passed:
            cp.wait_send()

    filled = [jnp.broadcast_to(s[None], (N_CHIPS,) + s.shape) for s in shards]
    return pl.pallas_call(
        body, name="gather_weights",
        in_specs=[ANY_SPEC] * (2 * n), out_specs=[ANY_SPEC] * n,
        out_shape=[jax.ShapeDtypeStruct(f.shape, f.dtype) for f in filled],
        input_output_aliases={n + w: w for w in range(n)},
        scratch_shapes=[pltpu.SemaphoreType.DMA((6 * n,)), pltpu.SemaphoreType.DMA((6 * n,))],
    )(*shards, *filled)


def _run_comm(comm, name):
    c_in, c_out = len(comm.inputs), len(comm.out_shape)

    def body(*refs):
        parts = refs[:c_in], refs[c_in:c_in + c_out], refs[c_in + c_out:]
        comm.start(*parts)
        comm.finish(*parts)

    return pl.pallas_call(
        body, name=name, in_specs=[ANY_SPEC] * c_in, out_specs=[ANY_SPEC] * c_out, out_shape=comm.out_shape,
        scratch_shapes=comm.scratch, input_output_aliases=comm.aliases)(*comm.inputs)


def _both(first, second):
    n_in, n_out, n_scr = len(first.inputs), len(first.out_shape), len(first.scratch)

    def split(ins, outs, scr):
        return (ins[:n_in], outs[:n_out], scr[:n_scr]), (ins[n_in:], outs[n_out:], scr[n_scr:])

    def start(*refs):
        a, b = split(*refs)
        first.start(*a)
        second.start(*b)

    def finish(*refs):
        a, b = split(*refs)
        first.finish(*a)
        second.finish(*b)

    aliases = dict(first.aliases)
    aliases.update({n_in + i: n_out + o for i, o in second.aliases.items()})
    return Comm(first.inputs + second.inputs, first.out_shape + second.out_shape, aliases,
                first.scratch + second.scratch, start, finish)


def _dma_sems(count):
    return [pltpu.SemaphoreType.DMA((count,)), pltpu.SemaphoreType.DMA((count,))]


def gather_over_ici(shards):
    n = len(shards)

    def copies(ins, outs, sems):
        send_sems, recv_sems = sems
        x, y, c, chips = _place()
        me = _chip_index(x, y)
        pairs = []
        for w in range(n):
            half = shards[w].shape[0] // 2
            rows = pl.ds(c * half, half)
            for j, chip in enumerate(chips):
                k = 3 * w + j
                landed = outs[w].at[_chip_index(*chip), rows]
                pairs.append((_rcopy(ins[w].at[rows], outs[w].at[me, rows], send_sems.at[k], recv_sems.at[k], (*chip, c)),
                              _rcopy(landed, landed, send_sems.at[k], recv_sems.at[k], (*chip, c))))
        return pairs

    def start(*refs):
        for send, _ in copies(*refs):
            send.start()

    def finish(*refs):
        pairs = copies(*refs)
        for _, landed in pairs:
            landed.wait_recv()
        for send, _ in pairs:
            send.wait_send()

    filled = [jnp.broadcast_to(s[None], (N_CHIPS,) + s.shape) for s in shards]
    return Comm(list(shards) + filled, [jax.ShapeDtypeStruct(f.shape, f.dtype) for f in filled],
                {n + w: w for w in range(n)}, _dma_sems(3 * n), start, finish)


def gather_over_d2d(landed):
    n = len(landed)

    def copies(ins, outs, sems):
        send_sems, recv_sems = sems
        x, y, c, chips = _place()
        sibling = (x, y, 1 - c)
        pairs = []
        for w in range(n):
            half = landed[w].shape[1] // 2
            for j, chip in enumerate(chips):
                k = 3 * w + j
                mine = outs[w].at[_chip_index(*chip), pl.ds(c * half, half)]
                theirs = outs[w].at[_chip_index(*chip), pl.ds((1 - c) * half, half)]
                pairs.append((_rcopy(mine, mine, send_sems.at[k], recv_sems.at[k], sibling),
                              _rcopy(theirs, theirs, send_sems.at[k], recv_sems.at[k], sibling)))
        return pairs

    def start(*refs):
        for send, _ in copies(*refs):
            send.start()

    def finish(*refs):
        pairs = copies(*refs)
        for _, arrived in pairs:
            arrived.wait_recv()
        for send, _ in pairs:
            send.wait_send()

    return Comm(list(landed), [jax.ShapeDtypeStruct(a.shape, a.dtype) for a in landed], {w: w for w in range(n)},
                _dma_sems(3 * n), start, finish)


def core_halves(grads):
    n = len(grads)

    def copies(ins, outs, sems):
        send_sems, recv_sems = sems
        x, y, c, _ = _place()
        out = []
        for w in range(n):
            half = grads[w].shape[1] // 2
            out.append(_rcopy(ins[w].at[:, pl.ds((1 - c) * half, half), :], outs[w],
                              send_sems.at[w], recv_sems.at[w], (x, y, 1 - c)))
        return out

    def start(*refs):
        for cp in copies(*refs):
            cp.start()

    def finish(*refs):
        for cp in copies(*refs):
            cp.wait()

    return Comm(list(grads), [jax.ShapeDtypeStruct((g.shape[0], g.shape[1] // 2, g.shape[2]), g.dtype) for g in grads],
                {}, _dma_sems(n), start, finish)


def chip_partials(partials):
    n = len(partials)

    def copies(ins, outs, sems):
        send_sems, recv_sems = sems
        x, y, c, chips = _place()
        me = _chip_index(x, y)
        pairs = []
        for w in range(n):
            for j, chip in enumerate(chips):
                k = 3 * w + j
                landed = outs[w].at[_chip_index(*chip)]
                pairs.append((_rcopy(ins[w].at[_chip_index(*chip)], outs[w].at[me], send_sems.at[k], recv_sems.at[k],
                                     (*chip, c)),
                              _rcopy(landed, landed, send_sems.at[k], recv_sems.at[k], (*chip, c))))
        return pairs

    def start(*refs):
        for send, _ in copies(*refs):
            send.start()

    def finish(*refs):
        pairs = copies(*refs)
        for _, landed in pairs:
            landed.wait_recv()
        for send, _ in pairs:
            send.wait_send()

    me = _chip_index(lax.axis_index("x"), lax.axis_index("y"))
    filled = [jnp.broadcast_to(lax.dynamic_index_in_dim(p, me, 0, keepdims=True), p.shape) for p in partials]
    return Comm(list(partials) + filled, [jax.ShapeDtypeStruct(p.shape, p.dtype) for p in partials],
                {n + w: w for w in range(n)}, _dma_sems(3 * n), start, finish)


def join_core_halves(grads):
    n = len(grads)

    def body(*refs):
        outs = refs[n:2 * n]
        send_sems, recv_sems = refs[2 * n:]
        x, y, c, _ = _place()
        sibling = (x, y, 1 - c)
        copies = []
        for w in range(n):
            half = outs[w].shape[0] // 2
            mine = outs[w].at[pl.ds(c * half, half), :]
            copies.append(_rcopy(mine, mine, send_sems.at[w], recv_sems.at[w], sibling))
        for cp in copies:
            cp.start()
        for w in range(n):
            half = outs[w].shape[0] // 2
            theirs = outs[w].at[pl.ds((1 - c) * half, half), :]
            _rcopy(theirs, theirs, send_sems.at[w], recv_sems.at[w], sibling).wait_recv()
        for cp in copies:
            cp.wait_send()

    return pl.pallas_call(
        body, name="join_core_halves",
        in_specs=[ANY_SPEC] * n, out_specs=[ANY_SPEC] * n,
        out_shape=[jax.ShapeDtypeStruct(g.shape, g.dtype) for g in grads],
        input_output_aliases={w: w for w in range(n)},
        scratch_shapes=[pltpu.SemaphoreType.DMA((n,)), pltpu.SemaphoreType.DMA((n,))],
    )(*grads)


def _elementwise_rows(rows, cap=512):
    for t in range(min(rows, cap), 0, -8):
        if rows % t == 0 and t % 16 == 0:
            return t
    return rows


def add_core_halves(grad, got, core, name):
    _, rows, cols = got.shape
    tr = _elementwise_rows(rows)
    nt = rows // tr

    def body(core_ref, a_ref, b_ref, o_ref):
        o_ref[...] = (a_ref[...] + b_ref[...]).astype(BF16)

    spec = pl.BlockSpec((1, tr, cols), lambda q, i, core_ref: (q, i, 0))
    own = pl.BlockSpec((1, tr, cols), lambda q, i, core_ref: (q, core_ref[0] * nt + i, 0))
    return pl.pallas_call(
        body, name=name,
        grid_spec=pltpu.PrefetchScalarGridSpec(num_scalar_prefetch=1, grid=(N_CHIPS, nt), in_specs=[own, spec],
                                               out_specs=spec),
        out_shape=jax.ShapeDtypeStruct(got.shape, BF16),
        compiler_params=_params(("arbitrary", "arbitrary")),
    )(core, grad, got)


def add_chip_partials(parts, core, name):
    _, rows, cols = parts.shape
    tr = _elementwise_rows(rows)
    nt = rows // tr

    def body(core_ref, p_ref, o_ref):
        acc = p_ref[0].astype(F32)
        for q in range(1, N_CHIPS):
            acc = acc + p_ref[q].astype(F32)
        o_ref[...] = acc

    return pl.pallas_call(
        body, name=name,
        grid_spec=pltpu.PrefetchScalarGridSpec(
            num_scalar_prefetch=1, grid=(nt,),
            in_specs=[pl.BlockSpec((N_CHIPS, tr, cols), lambda i, core_ref: (0, i, 0))],
            out_specs=pl.BlockSpec((tr, cols), lambda i, core_ref: (core_ref[0] * nt + i, 0))),
        out_shape=jax.ShapeDtypeStruct((2 * rows, cols), F32),
        compiler_params=_params(("arbitrary",)),
    )(core, parts)


def _adamw_math(w, g, m, v):
    m = ADAM_B1 * m + (1.0 - ADAM_B1) * g
    v = ADAM_B2 * v + (1.0 - ADAM_B2) * (g * g)
    m_hat = m / (1.0 - ADAM_B1 ** ADAM_STEP)
    v_hat = v / (1.0 - ADAM_B2 ** ADAM_STEP)
    delta = -ADAM_LR * (m_hat / (jnp.sqrt(v_hat) + ADAM_EPS) + ADAM_WD * w)
    return delta, m, v


def adamw(w, g, m, v, name):
    rows, cols = w.shape
    tr = _elementwise_rows(rows, 256)

    def body(w_ref, g_ref, m_ref, v_ref, d_ref, nm_ref, nv_ref):
        d, nm, nv = _adamw_math(w_ref[...], g_ref[...], m_ref[...], v_ref[...])
        d_ref[...] = d
        nm_ref[...] = nm
        nv_ref[...] = nv

    spec = pl.BlockSpec((tr, cols), lambda i: (i, 0))
    return pl.pallas_call(
        body, name=name, grid=(rows // tr,), in_specs=[spec] * 4, out_specs=[spec] * 3,
        out_shape=[jax.ShapeDtypeStruct((rows, cols), F32)] * 3,
        compiler_params=_params(("arbitrary",)),
    )(w, g, m, v)


def reduce_small(packs, w, m, v):
    n = len(packs)
    n_dev = 8
    flips = [(fx, fy, fc) for fx in (0, 1) for fy in (0, 1) for fc in (0, 1)][1:]

    def body(*refs):
        pack_refs = refs[:n]
        w_ref, m_ref, v_ref, g_out, d_out, m_out, v_out, mine, slots, send_sems, recv_sems = refs[n:]
        x, y, c, _ = _place()
        me = 4 * x + 2 * y + c
        acc = pack_refs[0][...]
        for ref in pack_refs[1:]:
            acc = acc + ref[...]
        mine[...] = acc
        sends = []
        for k, (fx, fy, fc) in enumerate(flips):
            peer = (x ^ fx, y ^ fy, c ^ fc)
            sends.append(_rcopy(mine, slots.at[me], send_sems.at[k], recv_sems.at[me], peer))
        for cp in sends:
            cp.start()
        slots[me] = acc
        for fx, fy, fc in flips:
            src = 4 * (x ^ fx) + 2 * (y ^ fy) + (c ^ fc)
            _rcopy(mine, slots.at[src], send_sems.at[0], recv_sems.at[src], (x, y, c)).wait_recv()
        for cp in sends:
            cp.wait_send()
        total = slots[0]
        for d in range(1, n_dev):
            total = total + slots[d]
        g_out[...] = total
        d, nm, nv = _adamw_math(w_ref[...], total, m_ref[...], v_ref[...])
        d_out[...] = d
        m_out[...] = nm
        v_out[...] = nv

    vm = pl.BlockSpec(memory_space=pltpu.VMEM)
    return pl.pallas_call(
        body, name="reduce_small",
        in_specs=[vm] * (n + 3), out_specs=[vm] * 4,
        out_shape=[jax.ShapeDtypeStruct((8, D_MODEL), F32)] * 4,
        scratch_shapes=[pltpu.VMEM((8, D_MODEL), F32), pltpu.VMEM((n_dev, 8, D_MODEL), F32),
                        pltpu.SemaphoreType.DMA((len(flips),)), pltpu.SemaphoreType.DMA((n_dev,))],
    )(*packs, w, m, v)


def _column_pieces(g):
    return g.reshape(g.shape[0], N_CHIPS, g.shape[1] // N_CHIPS).transpose(1, 0, 2)


def _pack_small(attn_pre, gamma, hg_norm, sb_norm, attn_post, ffn_pre, ffn_post):
    rows = [attn_pre, gamma.reshape(1, D_MODEL), jnp.concatenate([hg_norm, sb_norm], axis=1), attn_post, ffn_pre, ffn_post,
            jnp.zeros((2, D_MODEL), F32)]
    return jnp.concatenate(rows, axis=0)


def _unpack_small(pack):
    return (pack[ROW_ATTN_PRE:ROW_ATTN_PRE + 1], pack[ROW_GAMMA].reshape(2, HG_WIDTH),
            pack[ROW_MIX_NORMS:ROW_MIX_NORMS + 1, :HG_WIDTH], pack[ROW_MIX_NORMS:ROW_MIX_NORMS + 1, HG_WIDTH:],
            pack[ROW_ATTN_POST:ROW_ATTN_POST + 1], pack[ROW_FFN_PRE:ROW_FFN_PRE + 1], pack[ROW_FFN_POST:ROW_FFN_POST + 1])


def kernel(x, p, attn_pre_norm, w_in, hg_lower_gamma, hg_out_norm, sb_out_norm, w_out, attn_post_norm, ffn_pre_norm, w_gate_up, w_down, ffn_post_norm, ple_proj, ple_gate, loss_target, m_attn_pre_norm, m_w_in, m_hg_lower_gamma, m_hg_out_norm, m_sb_out_norm, m_w_out, m_attn_post_norm, m_ffn_pre_norm, m_w_gate_up, m_w_down, m_ffn_post_norm, m_ple_proj, m_ple_gate, v_attn_pre_norm, v_w_in, v_hg_lower_gamma, v_hg_out_norm, v_sb_out_norm, v_w_out, v_attn_post_norm, v_ffn_pre_norm, v_w_gate_up, v_w_down, v_ffn_post_norm, v_ple_proj, v_ple_gate):
    x2 = x[0]
    p2 = p[0, 0]
    target = loss_target[0]
    big = dict(w_in=(w_in, m_w_in, v_w_in), w_out=(w_out, m_w_out, v_w_out), w_gate_up=(w_gate_up, m_w_gate_up, v_w_gate_up),
               w_down=(w_down, m_w_down, v_w_down), ple_proj=(ple_proj, m_ple_proj, v_ple_proj),
               ple_gate=(ple_gate, m_ple_gate, v_ple_gate))
    names = list(big)
    big = {k: tuple(a[0] for a in t) for k, t in big.items()}

    shard16 = {k: big[k][0].astype(BF16) for k in names}
    w_in_full, = gather_weights([shard16["w_in"]])
    mix_norms = jnp.concatenate([hg_out_norm, sb_out_norm], axis=1)
    small_ones = ["w_out", "ple_proj", "ple_gate"]

    proj_h, sqkv, u1, *landed_small = in_proj_fwd(
        x2, attn_pre_norm, w_in_full, comm=gather_over_ici([shard16[k] for k in small_ones]))
    o_sb, sb_totals, sb_first, landed_gu = sb_fwd(sqkv, comm=gather_over_ici([shard16["w_gate_up"]]))
    o_hg, states, landed_down, *full_small = hgrn2_fwd(
        proj_h, hg_lower_gamma, comm=_both(gather_over_ici([shard16["w_down"]]), gather_over_d2d(landed_small)))
    full = dict(zip(small_ones, full_small), w_in=w_in_full)
    w_out_full = full["w_out"].reshape(D_MODEL, D_MODEL)
    w_pg_full = full["ple_gate"].reshape(D_MODEL, D_MODEL)
    cat, mix, h1, full["w_gate_up"], full["w_down"] = mix_out_fwd(
        o_hg, proj_h, o_sb, x2, mix_norms, attn_post_norm, w_out_full, comm=gather_over_d2d([landed_gu, landed_down]))
    w_down_full = full["w_down"].reshape(D_FF, D_MODEL)
    u2, gu, act, y, h2 = ffn_fwd(h1, ffn_pre_norm, ffn_post_norm, full["w_gate_up"], w_down_full)

    core = lax.axis_index("c").astype(jnp.int32).reshape(1)
    de, ds, dh2, h2b, pb, pack_loss = ple_loss(h2, p2, target, full["ple_proj"], w_pg_full)
    dy, dgu, dh1, pack_ffn = ffn_bwd(dh2, y, h1, gu, ffn_pre_norm, ffn_post_norm, full["w_gate_up"], w_down_full)
    local = dict(
        w_gate_up=weight_grad(u2, dgu, "grad_w_gate_up", tm=D_MODEL, tn=full["w_gate_up"].shape[2], tk=1024, col_pieces=True)[0],
        w_down=weight_grad(act, dy, "grad_w_down", tm=D_FF // 2, tn=D_MODEL, tk=1024)[0].reshape(full["w_down"].shape),
        ple_proj=_column_pieces(weight_grad(pb, de, "grad_ple_proj", tm=pb.shape[1], tn=D_MODEL, tk=1024)[0]),
        ple_gate=weight_grad(h2b, ds, "grad_ple_gate", tm=D_MODEL, tn=D_MODEL, tk=1024)[0].reshape(full["ple_gate"].shape),
    )
    early = list(local)
    dmix, do_hg, dhg, do_sb, pack_mix, *got = mix_out_bwd(
        dh1, mix, o_hg, proj_h, o_sb, mix_norms, attn_post_norm, w_out_full, comm=core_halves([local[k] for k in early]))
    partial = [add_core_halves(local[k], g, core, "add_core_halves_" + k) for k, g in zip(early, got)]
    local["w_out"] = weight_grad(cat, dmix, "grad_w_out", tm=D_MODEL, tn=D_MODEL, tk=1024)[0].reshape(full["w_out"].shape)
    dsq, dsk, dsv = sb_bwd(sqkv, do_sb, sb_totals, sb_first)
    dhq, dhf, dhi, pack_hg, by_source_gu, got_out = hgrn2_bwd(
        proj_h, hg_lower_gamma, states, do_hg, comm=_both(chip_partials(partial[:1]), core_halves([local["w_out"]])))
    early.append("w_out")
    partial.append(add_core_halves(local["w_out"], got_out, core, "add_core_halves_w_out"))
    dproj, grad_x, pack_in = in_proj_bwd([dhq, dhf, dhi, dhg, dsq, dsk, dsv], x2, dh1, attn_pre_norm, full["w_in"])

    late = ["w_in"]
    local["w_in"], *more = weight_grad(u1, dproj, "grad_w_in", tm=D_MODEL, tn=full["w_in"].shape[2], tk=1024,
                                       col_pieces=True, comm=chip_partials(partial[1:]))
    halves = {k: add_chip_partials(s, core, "add_chip_partials_" + k) for k, s in zip(early, [by_source_gu] + more)}
    got = _run_comm(core_halves([local[k] for k in late]), "exchange_core_halves")
    partial = [add_core_halves(local[k], g, core, "add_core_halves_" + k) for k, g in zip(late, got)]
    by_source = _run_comm(chip_partials(partial), "exchange_chip_partials")
    halves.update({k: add_chip_partials(s, core, "add_chip_partials_" + k) for k, s in zip(late, by_source)})
    grads = dict(zip(names, join_core_halves([halves[k] for k in names])))

    upd = {k: adamw(big[k][0], grads[k], big[k][1], big[k][2], "adamw_" + k) for k in names}

    small = reduce_small(
        [pack_loss, pack_ffn, pack_mix, pack_hg, pack_in],
        _pack_small(attn_pre_norm, hg_lower_gamma, hg_out_norm, sb_out_norm, attn_post_norm, ffn_pre_norm, ffn_post_norm),
        _pack_small(m_attn_pre_norm, m_hg_lower_gamma, m_hg_out_norm, m_sb_out_norm, m_attn_post_norm, m_ffn_pre_norm, m_ffn_post_norm),
        _pack_small(v_attn_pre_norm, v_hg_lower_gamma, v_hg_out_norm, v_sb_out_norm, v_attn_post_norm, v_ffn_pre_norm, v_ffn_post_norm),
    )
    loss = small[0][ROW_LOSS, 0]
    s_grad, s_delta, s_m, s_v = (_unpack_small(t) for t in small)

    def ordered(small_vals, big_vals):
        a_pre, gam, hg_n, sb_n, a_post, f_pre, f_post = small_vals
        b = {k: big_vals[k][None] for k in names}
        return (a_pre, b["w_in"], gam, hg_n, sb_n, b["w_out"], a_post, f_pre, b["w_gate_up"], b["w_down"], f_post,
                b["ple_proj"], b["ple_gate"])

    return (loss, grad_x[None],
            *ordered(s_grad, grads),
            *ordered(s_delta, {k: upd[k][0] for k in names}),
            *ordered(s_m, {k: upd[k][1] for k in names}),
            *ordered(s_v, {k: upd[k][2] for k in names}))
```

```python
from typing import Callable, NamedTuple

import numpy as np
import jax
import jax.numpy as jnp
from jax import lax
from jax.experimental import pallas as pl
from jax.experimental.pallas import tpu as pltpu

F32 = jnp.float32
BF16 = jnp.bfloat16
MESH = pl.DeviceIdType.MESH

RMS_EPS = 1e-6
D_MODEL = 1024
HG_WIDTH = 512
HG_HEADS = 4
HG_DK = 128
HG_CHUNK = 64
HG_LEVELS = (32, 16, 8, 4, 2, 1)
HG_CHUNKS_PER_STEP = 2
SB_WIDTH = 512
SB_BLOCK = 128
SB_DH = 64
SB_SCALE = SB_DH ** -0.5
SB_UNDERFLOW_LOG = -87.5
SB_UNROLL = 2
SB_GROUP = 2
D_FF = 2816
N_CHIPS = 4
ROW_TILE = 256
V7X_VMEM_LIMIT = 56 * 1024 * 1024

ADAM_LR = 0.001
ADAM_B1 = 0.9
ADAM_B2 = 0.999
ADAM_EPS = 1e-08
ADAM_WD = 0.01
ADAM_STEP = 10

ROW_ATTN_PRE, ROW_GAMMA, ROW_MIX_NORMS, ROW_ATTN_POST, ROW_FFN_PRE, ROW_FFN_POST, ROW_LOSS = range(7)


def _params(sem=None, vmem=V7X_VMEM_LIMIT):
    return pltpu.CompilerParams(dimension_semantics=sem, vmem_limit_bytes=vmem)


def _dot(a, b):
    return jnp.dot(a.astype(BF16), b.astype(BF16), preferred_element_type=F32)


def _dot_nt(a, b):
    return lax.dot_general(a.astype(BF16), b.astype(BF16), (((1,), (1,)), ((), ())), preferred_element_type=F32)


def _dot_tn(a, b):
    return lax.dot_general(a.astype(BF16), b.astype(BF16), (((0,), (0,)), ((), ())), preferred_element_type=F32)


def _split(x):
    hi = x.astype(BF16)
    lo = (x - hi.astype(F32)).astype(BF16)
    return hi, lo


def _sum01_left(m01, x):
    hi, lo = _split(x)
    return jnp.dot(m01, hi, preferred_element_type=F32) + jnp.dot(m01, lo, preferred_element_type=F32)


def _sum01_right(x, m01_twice):
    hi, lo = _split(x)
    return jnp.dot(jnp.concatenate([hi, lo], axis=1), m01_twice, preferred_element_type=F32)


def _rms(x):
    r = lax.rsqrt(jnp.mean(x * x, axis=-1, keepdims=True) + RMS_EPS)
    return x * r, r


def _rms_bwd(dy, xhat, r, w):
    dxh = dy * w
    dx = r * (dxh - xhat * jnp.mean(dxh * xhat, axis=-1, keepdims=True))
    return dx, dy * xhat


def _sigmoid(x):
    return 1.0 / (1.0 + jnp.exp(-x))


def _neg_softplus(z):
    return -(jnp.maximum(z, 0.0) + jnp.log(1.0 + jnp.exp(-jnp.abs(z))))


def _colsum(x):
    return jnp.sum(x, axis=0, keepdims=True)


def _load_once(src_hbm, dst_vmem):
    @pl.when(pl.program_id(0) == 0)
    def _():
        pltpu.sync_copy(src_hbm, dst_vmem)


def _zero_first(ref):
    @pl.when(pl.program_id(0) == 0)
    def _():
        ref[...] = jnp.zeros(ref.shape, ref.dtype)


def _row_spec(width, col=0):
    return pl.BlockSpec((ROW_TILE, width), lambda i, col=col: (i, col))


def _full_spec(shape):
    return pl.BlockSpec(shape, lambda *_: (0,) * len(shape))


ANY_SPEC = pl.BlockSpec(memory_space=pl.ANY)
PACK_SPEC = _full_spec((8, D_MODEL))


class Comm(NamedTuple):
    inputs: list
    out_shape: list
    aliases: dict
    scratch: list
    start: Callable
    finish: Callable


def _pallas(body, *, comm=None, edge=None, in_specs, out_specs, out_shape, scratch_shapes=(), **kw):
    if comm is None:
        return pl.pallas_call(body, in_specs=in_specs, out_specs=out_specs, out_shape=out_shape,
                              scratch_shapes=scratch_shapes, **kw)
    n_in, n_out, n_scr = len(in_specs), len(out_specs), len(scratch_shapes)
    c_in, c_out = len(comm.inputs), len(comm.out_shape)

    def both(*refs):
        ins, c_ins = refs[:n_in], refs[n_in:n_in + c_in]
        outs = refs[n_in + c_in:n_in + c_in + n_out]
        c_outs = refs[n_in + c_in + n_out:n_in + c_in + n_out + c_out]
        rest = refs[n_in + c_in + n_out + c_out:]
        scr, c_scr = rest[:n_scr], rest[n_scr:]
        first, last = edge()

        @pl.when(first)
        def _():
            comm.start(c_ins, c_outs, c_scr)

        body(*ins, *outs, *scr)

        @pl.when(last)
        def _():
            comm.finish(c_ins, c_outs, c_scr)

    call = pl.pallas_call(
        both, in_specs=list(in_specs) + [ANY_SPEC] * c_in, out_specs=list(out_specs) + [ANY_SPEC] * c_out,
        out_shape=list(out_shape) + list(comm.out_shape), scratch_shapes=list(scratch_shapes) + list(comm.scratch),
        input_output_aliases={n_in + a: n_out + b for a, b in comm.aliases.items()}, **kw)
    return lambda *args: call(*args, *comm.inputs)


def _grid_edge(steps):
    return lambda: (pl.program_id(0) == 0, pl.program_id(0) == steps - 1)


def in_proj_fwd(x, g_pre, w_in, comm=None):
    T = x.shape[0]
    pw = w_in.shape[2]

    def body(x_ref, g_ref, w_hbm, ph_ref, sqkv_ref, u_ref, w_vmem, proj_s):
        _load_once(w_hbm, w_vmem)
        xh, _ = _rms(x_ref[...])
        u = (xh * g_ref[...]).astype(BF16)
        u_ref[...] = u
        for q in range(N_CHIPS):
            proj_s[:, pw * q:pw * (q + 1)] = jnp.dot(u, w_vmem[q], preferred_element_type=F32)
        ph_ref[...] = proj_s[:, :4 * HG_WIDTH]
        sqkv_ref[:, :SB_WIDTH] = (proj_s[:, 4 * HG_WIDTH:4 * HG_WIDTH + SB_WIDTH] * SB_SCALE).astype(BF16)
        sqkv_ref[:, SB_WIDTH:] = proj_s[:, 4 * HG_WIDTH + SB_WIDTH:].astype(BF16)

    return _pallas(
        body, comm=comm, edge=_grid_edge(T // ROW_TILE), name="in_proj_fwd", grid=(T // ROW_TILE,),
        in_specs=[_row_spec(D_MODEL), _full_spec((1, D_MODEL)), ANY_SPEC],
        out_specs=[_row_spec(4 * HG_WIDTH), _row_spec(3 * SB_WIDTH), _row_spec(D_MODEL)],
        out_shape=[jax.ShapeDtypeStruct((T, 4 * HG_WIDTH), F32), jax.ShapeDtypeStruct((T, 3 * SB_WIDTH), BF16),
                   jax.ShapeDtypeStruct((T, D_MODEL), BF16)],
        scratch_shapes=[pltpu.VMEM(w_in.shape, BF16), pltpu.VMEM((ROW_TILE, N_CHIPS * pw), F32)],
        compiler_params=_params(("arbitrary",)),
    )(x, g_pre, w_in)


def _hg_sum_matrix():
    C = HG_CHUNK
    t = np.arange(C)[:, None]
    j = np.arange(C)[None, :]
    mats = [j <= t, j > t]
    for h in HG_LEVELS:
        start = (t // (2 * h)) * (2 * h)
        upper = (t & h) != 0
        mats.append(np.where(upper, (j >= start + h) & (j <= t), (j > t) & (j <= start + h - 1)))
    return np.concatenate(mats, 0).astype(np.float32)


def _hg_level_masks():
    C = HG_CHUNK
    t = lax.broadcasted_iota(jnp.int32, (C, C), 0)
    s = lax.broadcasted_iota(jnp.int32, (C, C), 1)
    x = t ^ s
    masks = [t == s]
    for h in HG_LEVELS:
        masks.append((x >= h) & (x < 2 * h) & (t > s))
    return masks


def _hg_gates(hq, hf, gamma):
    lb = 1.0 / (1.0 + jnp.exp(gamma[1:2, :] - gamma[0:1, :]))
    sq = _sigmoid(hq)
    q = hq * sq
    sig = _sigmoid(hf)
    nsig = _sigmoid(-hf)
    f = lb + (1.0 - lb) * sig
    k = (1.0 - lb) * nsig
    g = jnp.log(f)
    return q, k, g, dict(lb=lb, sq=sq, sig=sig, nsig=nsig, f=f)


def _hg_head_decays(A, h):
    C, K = HG_CHUNK, HG_DK
    sl = slice(K * h, K * (h + 1))
    blocks = [A[C * r:C * (r + 1), sl] for r in range(2 + len(HG_LEVELS))]
    return blocks[0], blocks[1], [None] + blocks[2:]


def _hg_products(q, k, levels):
    return [_dot_nt(q, k)] + [_dot_nt(q * a, k * a) for a in levels[1:]]


def _hg_select(prods, masks):
    sc = jnp.where(masks[0], prods[0], 0.0)
    for p, m in zip(prods[1:], masks[1:]):
        sc = jnp.where(m, p, sc)
    return sc


def hgrn2_fwd(proj_h, gamma, comm=None):
    T = proj_h.shape[0]
    C, K, H, S = HG_CHUNK, HG_DK, HG_HEADS, HG_CHUNKS_PER_STEP
    n_steps = T // (S * C)
    msum = jnp.asarray(_hg_sum_matrix(), BF16)

    def body(hq_ref, hf_ref, hi_ref, gam_ref, msum_ref, o_ref, st_ref, st_s):
        _zero_first(st_s)
        q, k, g, _ = _hg_gates(hq_ref[...], hf_ref[...], gam_ref[...])
        v = hi_ref[...]
        masks = _hg_level_masks()
        parts = []
        for s in range(S):
            rows = slice(C * s, C * (s + 1))
            A = jnp.exp(_sum01_left(msum_ref[...], g[rows]))
            for h in range(H):
                sl = slice(K * h, K * (h + 1))
                ab, ar, levels = _hg_head_decays(A, h)
                parts.append(dict(s=s, h=h, rows=rows, sl=sl, ab=ab, ar=ar, levels=levels,
                                  q=q[rows, sl], k=k[rows, sl], v=v[rows, sl]))
        for pt in parts:
            pt["prods"] = _hg_products(pt["q"], pt["k"], pt["levels"])
            pt["grown"] = _dot_tn(pt["v"], pt["k"] * pt["ar"])
        for pt in parts:
            pt["sc"] = _hg_select(pt["prods"], masks)
        state = [st_s[h] for h in range(H)]
        for pt in parts:
            h, ab = pt["h"], pt["ab"]
            o_ref[pt["rows"], pt["sl"]] = _dot_nt(pt["q"] * ab, state[h]) + _dot(pt["sc"], pt["v"])
            state[h] = state[h] * ab[C - 1:C, :] + pt["grown"]
            st_ref[pt["s"], h] = state[h]
        for h in range(H):
            st_s[h] = state[h]

    blk = lambda col: pl.BlockSpec((S * C, HG_WIDTH), lambda c, col=col: (c, col))
    return _pallas(
        body, comm=comm, edge=_grid_edge(n_steps), name="hgrn2_fwd", grid=(n_steps,),
        in_specs=[blk(0), blk(1), blk(2), _full_spec((2, HG_WIDTH)), _full_spec(msum.shape)],
        out_specs=[blk(0), pl.BlockSpec((S, H, K, K), lambda c: (c, 0, 0, 0))],
        out_shape=[jax.ShapeDtypeStruct((T, HG_WIDTH), F32), jax.ShapeDtypeStruct((S * n_steps, H, K, K), F32)],
        scratch_shapes=[pltpu.VMEM((H, K, K), F32)],
        compiler_params=_params(("arbitrary",)),
    )(proj_h, proj_h, proj_h, gamma, msum)


def hgrn2_bwd(proj_h, gamma, states, do, comm=None):
    T = proj_h.shape[0]
    C, K, H, S = HG_CHUNK, HG_DK, HG_HEADS, HG_CHUNKS_PER_STEP
    n_steps = T // (S * C)
    n_sums = 2 + len(HG_LEVELS)
    msum = jnp.asarray(_hg_sum_matrix(), BF16)
    msum_t = jnp.asarray(_hg_sum_matrix().T, BF16)

    def body(hq_ref, hf_ref, hi_ref, do_ref, gam_ref, msum_ref, msum_t_ref, st_prev_ref, st_ref,
             dhq_ref, dhf_ref, dhi_ref, pack_ref, dst_s, dlb_s, dq_s, dk_s, de_s):
        step = pl.program_id(0)
        _zero_first(dst_s)
        _zero_first(dlb_s)
        _zero_first(pack_ref)
        hq = hq_ref[...]
        q, k, g, aux = _hg_gates(hq, hf_ref[...], gam_ref[...])
        v = hi_ref[...]
        do_all = do_ref[...]
        masks = _hg_level_masks()
        is_last_row = lax.broadcasted_iota(jnp.int32, (C, K), 0) == C - 1
        has_prev = (step < n_steps - 1).astype(F32)
        parts = []
        for s in reversed(range(S)):
            rows = slice(C * s, C * (s + 1))
            A = jnp.exp(_sum01_left(msum_ref[...], g[rows]))
            for h in range(H):
                sl = slice(K * h, K * (h + 1))
                ab, ar, levels = _hg_head_decays(A, h)
                st_in = st_prev_ref[0, h] * has_prev if s == 0 else st_ref[s - 1, h]
                parts.append(dict(s=s, h=h, rows=rows, sl=sl, ab=ab, ar=ar, levels=levels, st_in=st_in,
                                  q=q[rows, sl], k=k[rows, sl], v=v[rows, sl], do=do_all[rows, sl]))
        for pt in parts:
            pt["prods"] = _hg_products(pt["q"], pt["k"], pt["levels"])
            pt["da"] = _dot_nt(pt["do"], pt["v"])
            pt["t1"] = pt["ab"] * _dot(pt["do"], pt["st_in"])
            pt["dst_add"] = _dot_tn(pt["do"], pt["q"] * pt["ab"])
        dstate = [dst_s[h] for h in range(H)]
        for pt in parts:
            h = pt["h"]
            pt["dst_out"] = dstate[h]
            pt["t2"] = pt["ar"] * _dot(pt["v"], dstate[h])
            pt["dv_state"] = _dot_nt(pt["k"] * pt["ar"], dstate[h])
            dstate[h] = dstate[h] * pt["ab"][C - 1:C, :] + pt["dst_add"]
        for h in range(H):
            dst_s[h] = dstate[h]
        for pt in parts:
            pt["sc"] = _hg_select(pt["prods"], masks)
            pt["dam"] = [jnp.where(m, pt["da"], 0.0) for m in masks]
        for pt in parts:
            qh, kh = pt["q"], pt["k"]
            pt["dq_parts"] = [_dot(pt["dam"][0], kh)] + [
                a * _dot(dam, kh * a) for a, dam in zip(pt["levels"][1:], pt["dam"][1:])]
            pt["dk_parts"] = [_dot_tn(pt["dam"][0], qh)] + [
                a * _dot_tn(dam, qh * a) for a, dam in zip(pt["levels"][1:], pt["dam"][1:])]
            pt["dv_intra"] = _dot_tn(pt["sc"], pt["do"])
        for pt in parts:
            s, rows, sl, qh, kh, ab = pt["s"], pt["rows"], pt["sl"], pt["q"], pt["k"], pt["ab"]
            decayed = _colsum(pt["st_in"] * pt["dst_out"]) * ab[C - 1:C, :]
            de_s[s, 0:C, sl] = qh * pt["t1"] + jnp.where(is_last_row, decayed, 0.0)
            de_s[s, C:2 * C, sl] = kh * pt["t2"]
            dq = pt["t1"] + pt["dq_parts"][0]
            dk = pt["t2"] + pt["dk_parts"][0]
            for r, (t1, t2) in enumerate(zip(pt["dq_parts"][1:], pt["dk_parts"][1:])):
                dq = dq + t1
                dk = dk + t2
                de_s[s, C * (r + 2):C * (r + 3), sl] = qh * t1 + kh * t2
            dhi_ref[rows, sl] = pt["dv_intra"] + pt["dv_state"]
            dq_s[rows, sl] = dq
            dk_s[rows, sl] = dk
        dg = jnp.concatenate([_sum01_left(msum_t_ref[...], de_s[s]) for s in range(S)], axis=0)
        dk = dk_s[...]
        sq, lb = aux["sq"], aux["lb"]
        dhq_ref[...] = dq_s[...] * (sq * (1.0 + hq * (1.0 - sq)))
        common = dg / aux["f"] - dk
        dhf_ref[...] = (1.0 - lb) * aux["sig"] * aux["nsig"] * common
        dlb_s[...] += _colsum(aux["nsig"] * common)

        @pl.when(step == n_steps - 1)
        def _():
            dgam = lb * (1.0 - lb) * dlb_s[...]
            pack_ref[ROW_GAMMA:ROW_GAMMA + 1, :HG_WIDTH] = dgam
            pack_ref[ROW_GAMMA:ROW_GAMMA + 1, HG_WIDTH:] = -dgam

    last = n_steps - 1
    blk = lambda col: pl.BlockSpec((S * C, HG_WIDTH), lambda c, col=col: (last - c, col))
    return _pallas(
        body, comm=comm, edge=_grid_edge(n_steps), name="hgrn2_bwd", grid=(n_steps,),
        in_specs=[blk(0), blk(1), blk(2), blk(0), _full_spec((2, HG_WIDTH)), _full_spec(msum.shape),
                  _full_spec(msum_t.shape),
                  pl.BlockSpec((1, H, K, K), lambda c: (jnp.maximum(S * (last - c) - 1, 0), 0, 0, 0)),
                  pl.BlockSpec((S, H, K, K), lambda c: (last - c, 0, 0, 0))],
        out_specs=[blk(0), blk(0), blk(0), PACK_SPEC],
        out_shape=[jax.ShapeDtypeStruct((T, HG_WIDTH), F32)] * 3 + [jax.ShapeDtypeStruct((8, D_MODEL), F32)],
        scratch_shapes=[pltpu.VMEM((H, K, K), F32), pltpu.VMEM((1, HG_WIDTH), F32), pltpu.VMEM((S * C, HG_WIDTH), F32),
                        pltpu.VMEM((S * C, HG_WIDTH), F32), pltpu.VMEM((S, n_sums * C, HG_WIDTH), F32)],
        compiler_params=_params(("arbitrary",)),
    )(proj_h, proj_h, proj_h, do, gamma, msum, msum_t, states, states)


def _sb_sum_matrix(inclusive):
    B = SB_BLOCK
    j = np.arange(B)[:, None]
    s = np.arange(B)[None, :]
    tri = (j >= s) if inclusive else (j > s)
    once = np.concatenate([tri, np.ones((B, B), bool)], 1).astype(np.float32)
    return np.concatenate([once, once], 0)


def _sb_prefix_matrix(inclusive):
    B = SB_BLOCK
    j = np.arange(B)[:, None]
    s = np.arange(B)[None, :]
    tri = (j <= s) if inclusive else (j < s)
    once = np.concatenate([tri, np.ones((B, B), bool)], 1).astype(np.float32)
    return np.concatenate([once, once], 0)


def _sb_iotas():
    shape = (SB_BLOCK, SB_BLOCK)
    return lax.broadcasted_iota(jnp.int32, shape, 0), lax.broadcasted_iota(jnp.int32, shape, 1)


def _sb_heads(q, first):
    heads = []
    for g in range(SB_GROUP):
        qg = q[:, SB_BLOCK * g:SB_BLOCK * (g + 1)]
        zero = jnp.zeros_like(qg)
        heads += [(g, jnp.where(first, qg, zero)), (g, jnp.where(first, zero, qg))]
    return heads


def _lanes(x, g):
    return x[:, SB_BLOCK * g:SB_BLOCK * (g + 1)]


def sb_fwd(sqkv, comm=None):
    T = sqkv.shape[0]
    B = SB_BLOCK
    W = SB_GROUP * B
    groups = SB_WIDTH // W
    usum = jnp.asarray(_sb_sum_matrix(False), BF16)

    def body(q_ref, k_ref, v_ref, u_ref, o_ref, tl_ref, first_ref):
        p, i = pl.program_id(0), pl.program_id(1)
        row, lane = _sb_iotas()
        first = lane < SB_DH
        heads = _sb_heads(q_ref[...], first)
        u = u_ref[...]

        def more(loop):
            n, reachable, _ = loop
            return (SB_UNROLL * n <= i) & (reachable > 0)

        def step(loop):
            n, _, state = loop
            blocks = []
            for sub in range(SB_UNROLL):
                j = i - SB_UNROLL * n - sub
                off = pl.multiple_of(jnp.maximum(j, 0) * B, B)
                valid = ((lane + j * B) < (row + i * B)) & (j >= 0)
                blocks.append((k_ref[pl.ds(off, B), :], v_ref[pl.ds(off, B), :], valid))
            z = [[_dot_nt(qh, _lanes(kj, g)) for g, qh in heads] for kj, _, _ in blocks]
            lnb = [[jnp.where(valid, _neg_softplus(zz), 0.0) for zz in zs] for zs, (_, _, valid) in zip(z, blocks)]
            sums = [[_sum01_right(x, u) for x in xs] for xs in lnb]
            out = []
            for h, (carry, acc) in enumerate(state):
                for sub, (_, vj, valid) in enumerate(blocks):
                    expo = z[sub][h] + lnb[sub][h] + carry + sums[sub][h][:, :B]
                    acc = acc + _dot(jnp.where(valid, jnp.exp(expo), 0.0), _lanes(vj, heads[h][0]))
                    carry = carry + sums[sub][h][:, B:]
                out.append((carry, acc))
            state = tuple(out)
            worst = state[0][0]
            for carry, _ in state[1:]:
                worst = jnp.maximum(worst, carry)
            reachable = (jnp.max(worst) > SB_UNDERFLOW_LOG).astype(jnp.int32)
            return n + 1, reachable, state

        zero = jnp.zeros((B, B), F32)
        done, _, state = lax.while_loop(
            more, step, (jnp.int32(0), jnp.int32(1), tuple((zero, zero) for _ in heads)))
        for g in range(SB_GROUP):
            (tot0, acc0), (tot1, acc1) = state[2 * g], state[2 * g + 1]
            o_ref[:, B * g:B * (g + 1)] = jnp.where(first, acc0, acc1)
            tl_ref[:, B * g:B * (g + 1)] = jnp.where(first, tot0, tot1)
        first_ref[p, i] = jnp.maximum(i + 1 - SB_UNROLL * done, 0)

    def edge():
        p, i = pl.program_id(0), pl.program_id(1)
        return (p == 0) & (i == 0), (p == groups - 1) & (i == T // B - 1)

    return _pallas(
        body, comm=comm, edge=edge, name="sb_fwd", grid=(groups, T // B),
        in_specs=[pl.BlockSpec((B, W), lambda p, i: (i, p)),
                  pl.BlockSpec((T, W), lambda p, i: (0, groups + p)),
                  pl.BlockSpec((T, W), lambda p, i: (0, 2 * groups + p)),
                  pl.BlockSpec(usum.shape, lambda p, i: (0, 0))],
        out_specs=[pl.BlockSpec((B, W), lambda p, i: (i, p))] * 2 + [pl.BlockSpec(memory_space=pltpu.SMEM)],
        out_shape=[jax.ShapeDtypeStruct((T, SB_WIDTH), F32)] * 2 + [jax.ShapeDtypeStruct((groups, T // B), jnp.int32)],
        compiler_params=_params(("arbitrary", "arbitrary")),
    )(sqkv, sqkv, sqkv, usum)


def sb_bwd(sqkv, do, tl, first_block):
    T = sqkv.shape[0]
    B = SB_BLOCK
    W = SB_GROUP * B
    groups = SB_WIDTH // W
    upre = jnp.asarray(_sb_prefix_matrix(True), BF16)
    uexc = jnp.asarray(_sb_prefix_matrix(False), BF16)

    def body(q_ref, k_ref, v_ref, do_ref, tl_ref, up_ref, ue_ref, first_ref, dq_ref, dk_ref, dv_ref):
        p, i = pl.program_id(0), pl.program_id(1)

        @pl.when(i == 0)
        def _():
            dk_ref[...] = jnp.zeros(dk_ref.shape, F32)
            dv_ref[...] = jnp.zeros(dv_ref.shape, F32)

        row, lane = _sb_iotas()
        first = lane < SB_DH
        do = do_ref[...]
        tl_all = tl_ref[...]
        heads = []
        for (g, qh), at in zip(_sb_heads(q_ref[...], first), (0, B - 1) * SB_GROUP):
            dog = _lanes(do, g)
            keep = first if at == 0 else jnp.logical_not(first)
            heads.append((g, qh, jnp.where(keep, dog, jnp.zeros_like(dog)).astype(BF16),
                          _lanes(tl_all, g)[:, at:at + 1]))
        up = up_ref[...]
        ue = ue_ref[...]
        start = first_ref[p, i]

        def step(n, state):
            blocks = []
            for sub in range(SB_UNROLL):
                j = start + SB_UNROLL * n + sub
                off = pl.multiple_of(jnp.minimum(j, i) * B, B)
                valid = (lane + j * B) < (row + i * B)
                blocks.append((off, k_ref[pl.ds(off, B), :], v_ref[pl.ds(off, B), :], valid))
            combos = [(s, h) for s in range(SB_UNROLL) for h in range(len(heads))]
            z = {(s, h): _dot_nt(heads[h][1], _lanes(blocks[s][1], heads[h][0])) for s, h in combos}
            da = {(s, h): _dot_nt(heads[h][2], _lanes(blocks[s][2], heads[h][0])) for s, h in combos}
            lnb = {c: jnp.where(blocks[c[0]][3], _neg_softplus(z[c]), 0.0) for c in combos}
            lb = {c: z[c] + lnb[c] for c in combos}
            sums = {c: _sum01_right(lnb[c], up) for c in combos}
            a, w = {}, {}
            seen = [st[0] for st in state]
            for s, h in combos:
                expo = lb[s, h] + (heads[h][3] - seen[h] - sums[s, h][:, :B])
                a[s, h] = jnp.where(blocks[s][3], jnp.exp(expo), 0.0)
                w[s, h] = a[s, h] * da[s, h]
                seen[h] = seen[h] + sums[s, h][:, B:]
            wsums = {c: _sum01_right(w[c], ue) for c in combos}
            dz = {}
            seen_w = [st[1] for st in state]
            for s, h in combos:
                beta = jnp.exp(lb[s, h])
                before = seen_w[h] + wsums[s, h][:, :B]
                dz[s, h] = jnp.where(blocks[s][3], w[s, h] * (1.0 - beta) - before * beta, 0.0)
                seen_w[h] = seen_w[h] + wsums[s, h][:, B:]
            dq = [st[2] for st in state]
            for s, h in combos:
                dq[h] = dq[h] + _dot(dz[s, h], _lanes(blocks[s][1], heads[h][0]))
            for s in range(SB_UNROLL):
                off = blocks[s][0]
                for g in range(SB_GROUP):
                    h0, h1 = 2 * g, 2 * g + 1
                    dk_ref[pl.ds(off, B), B * g:B * (g + 1)] += (_dot_tn(dz[s, h0], heads[h0][1])
                                                                 + _dot_tn(dz[s, h1], heads[h1][1]))
                    dv_ref[pl.ds(off, B), B * g:B * (g + 1)] += (_dot_tn(a[s, h0], heads[h0][2])
                                                                 + _dot_tn(a[s, h1], heads[h1][2]))
            return tuple(zip(seen, seen_w, dq))

        zero = jnp.zeros((B, B), F32)
        trips = (i - start + SB_UNROLL) // SB_UNROLL
        state = lax.fori_loop(0, trips, step, tuple((zero, zero, zero) for _ in heads))
        for g in range(SB_GROUP):
            dq_ref[:, B * g:B * (g + 1)] = jnp.where(first, state[2 * g][2], state[2 * g + 1][2]) * SB_SCALE

    qblk = pl.BlockSpec((B, W), lambda p, i: (i, p))
    full = pl.BlockSpec((T, W), lambda p, i: (0, p))
    return pl.pallas_call(
        body, name="sb_bwd", grid=(groups, T // B),
        in_specs=[qblk, pl.BlockSpec((T, W), lambda p, i: (0, groups + p)),
                  pl.BlockSpec((T, W), lambda p, i: (0, 2 * groups + p)), qblk, qblk,
                  pl.BlockSpec(upre.shape, lambda p, i: (0, 0)), pl.BlockSpec(uexc.shape, lambda p, i: (0, 0)),
                  pl.BlockSpec(memory_space=pltpu.SMEM)],
        out_specs=[qblk, full, full],
        out_shape=[jax.ShapeDtypeStruct((T, SB_WIDTH), F32)] * 3,
        compiler_params=_params(("arbitrary", "arbitrary")),
    )(sqkv, sqkv, sqkv, do, tl, upre, uexc, first_block)


def _mixer_out(o_hg, hg, o_sb, g_hg, g_sb):
    n_hg, r_hg = _rms(o_hg)
    s_hg = _sigmoid(hg)
    n_sb, r_sb = _rms(o_sb)
    return dict(n_hg=n_hg, r_hg=r_hg, s_hg=s_hg, n_sb=n_sb, r_sb=r_sb,
                y_hg=n_hg * g_hg * (hg * s_hg), y_sb=n_sb * g_sb)


def mix_out_fwd(o_hg, proj_h, o_sb, x, norms, g_post, w_out, comm=None):
    T = x.shape[0]

    def body(ohg_ref, hg_ref, osb_ref, x_ref, nrm_ref, gp_ref, w_hbm, cat_ref, mix_ref, h1_ref, w_vmem):
        _load_once(w_hbm, w_vmem)
        nrm = nrm_ref[...]
        m = _mixer_out(ohg_ref[...], hg_ref[...], osb_ref[...], nrm[:, :HG_WIDTH], nrm[:, HG_WIDTH:])
        cat_ref[:, :HG_WIDTH] = m["y_hg"].astype(BF16)
        cat_ref[:, HG_WIDTH:] = m["y_sb"].astype(BF16)
        mix = jnp.dot(cat_ref[...], w_vmem[...], preferred_element_type=F32)
        mix_ref[...] = mix
        mh, _ = _rms(mix)
        h1_ref[...] = x_ref[...] + mh * gp_ref[...]

    return _pallas(
        body, comm=comm, edge=_grid_edge(T // ROW_TILE), name="mix_out_fwd", grid=(T // ROW_TILE,),
        in_specs=[_row_spec(HG_WIDTH), _row_spec(HG_WIDTH, 3), _row_spec(SB_WIDTH), _row_spec(D_MODEL),
                  _full_spec((1, D_MODEL)), _full_spec((1, D_MODEL)), ANY_SPEC],
        out_specs=[_row_spec(D_MODEL)] * 3,
        out_shape=[jax.ShapeDtypeStruct((T, D_MODEL), BF16), jax.ShapeDtypeStruct((T, D_MODEL), F32),
                   jax.ShapeDtypeStruct((T, D_MODEL), F32)],
        scratch_shapes=[pltpu.VMEM(w_out.shape, BF16)],
        compiler_params=_params(("arbitrary",)),
    )(o_hg, proj_h, o_sb, x, norms, g_post, w_out)


def ffn_fwd(h1, g_pre, g_post, w_gu, w_down):
    T = h1.shape[0]
    pw = w_gu.shape[2]

    def body(h1_ref, gpre_ref, gpost_ref, wgu_hbm, wd_hbm, u2_ref, gu_ref, act_ref, y_ref, h2_ref,
             wgu_vmem, wd_vmem, gu_s):
        _load_once(wgu_hbm, wgu_vmem)
        _load_once(wd_hbm, wd_vmem)
        h1v = h1_ref[...]
        hh, _ = _rms(h1v)
        u2 = (hh * gpre_ref[...]).astype(BF16)
        u2_ref[...] = u2
        for q in range(N_CHIPS):
            gu_s[:, pw * q:pw * (q + 1)] = jnp.dot(u2, wgu_vmem[q], preferred_element_type=F32)
        gu_ref[...] = gu_s[...].astype(BF16)
        gate = gu_s[:, :D_FF]
        act = (gate * _sigmoid(gate) * gu_s[:, D_FF:]).astype(BF16)
        act_ref[...] = act
        y = jnp.dot(act, wd_vmem[...], preferred_element_type=F32)
        y_ref[...] = y
        yh, _ = _rms(y)
        h2_ref[...] = h1v + yh * gpost_ref[...]

    return pl.pallas_call(
        body, name="ffn_fwd", grid=(T // ROW_TILE,),
        in_specs=[_row_spec(D_MODEL), _full_spec((1, D_MODEL)), _full_spec((1, D_MODEL)), ANY_SPEC, ANY_SPEC],
        out_specs=[_row_spec(D_MODEL), _row_spec(2 * D_FF), _row_spec(D_FF), _row_spec(D_MODEL), _row_spec(D_MODEL)],
        out_shape=[jax.ShapeDtypeStruct((T, D_MODEL), BF16), jax.ShapeDtypeStruct((T, 2 * D_FF), BF16),
                   jax.ShapeDtypeStruct((T, D_FF), BF16), jax.ShapeDtypeStruct((T, D_MODEL), F32),
                   jax.ShapeDtypeStruct((T, D_MODEL), F32)],
        scratch_shapes=[pltpu.VMEM(w_gu.shape, BF16), pltpu.VMEM(w_down.shape, BF16),
                        pltpu.VMEM((ROW_TILE, 2 * D_FF), F32)],
        compiler_params=_params(("arbitrary",)),
    )(h1, g_pre, g_post, w_gu, w_down)


def ple_loss(h2, p, target, w_ple, w_pg):
    T = h2.shape[0]
    pw = w_ple.shape[2]

    def body(h2_ref, p_ref, t_ref, wple_hbm, wpg_hbm, de_ref, ds_ref, dh2_ref, h2b_ref, pb_ref, pack_ref,
             wple_vmem, wpg_vmem, e_s):
        _load_once(wple_hbm, wple_vmem)
        _load_once(wpg_hbm, wpg_vmem)
        _zero_first(pack_ref)
        h2v = h2_ref[...]
        h2b = h2v.astype(BF16)
        h2b_ref[...] = h2b
        pb = p_ref[...].astype(BF16)
        pb_ref[...] = pb
        for q in range(N_CHIPS):
            e_s[:, pw * q:pw * (q + 1)] = jnp.dot(pb, wple_vmem[q], preferred_element_type=F32)
        e = e_s[...]
        sig = _sigmoid(jnp.dot(h2b, wpg_vmem[...], preferred_element_type=F32))
        err = h2v + e * sig - t_ref[...]
        part = 0.5 * jnp.sum(jnp.mean(err * err, axis=-1, keepdims=True), axis=0, keepdims=True)
        lane = lax.broadcasted_iota(jnp.int32, (1, D_MODEL), 1)
        pack_ref[ROW_LOSS:ROW_LOSS + 1, :] += jnp.where(lane == 0, part, 0.0)
        dh3 = err * (1.0 / D_MODEL)
        de_ref[...] = (dh3 * sig).astype(BF16)
        ds = (dh3 * e * sig * (1.0 - sig)).astype(BF16)
        ds_ref[...] = ds
        dh2_ref[...] = dh3 + _dot_nt(ds, wpg_vmem[...])

    return pl.pallas_call(
        body, name="ple_loss", grid=(T // ROW_TILE,),
        in_specs=[_row_spec(D_MODEL), _row_spec(p.shape[1]), _row_spec(D_MODEL), ANY_SPEC, ANY_SPEC],
        out_specs=[_row_spec(D_MODEL), _row_spec(D_MODEL), _row_spec(D_MODEL), _row_spec(D_MODEL),
                   _row_spec(p.shape[1]), PACK_SPEC],
        out_shape=[jax.ShapeDtypeStruct((T, D_MODEL), BF16), jax.ShapeDtypeStruct((T, D_MODEL), BF16),
                   jax.ShapeDtypeStruct((T, D_MODEL), F32), jax.ShapeDtypeStruct((T, D_MODEL), BF16),
                   jax.ShapeDtypeStruct(p.shape, BF16), jax.ShapeDtypeStruct((8, D_MODEL), F32)],
        scratch_shapes=[pltpu.VMEM(w_ple.shape, BF16), pltpu.VMEM(w_pg.shape, BF16), pltpu.VMEM((ROW_TILE, D_MODEL), F32)],
        compiler_params=_params(("arbitrary",)),
    )(h2, p, target, w_ple, w_pg)


def ffn_bwd(dh2, y, h1, gu, g_pre, g_post, w_gu, w_down):
    T = h1.shape[0]
    pw = w_gu.shape[2]

    def body(dh2_ref, y_ref, h1_ref, gu_ref, gpre_ref, gpost_ref, wgu_hbm, wd_hbm, dy_ref, dgu_ref, dh1_ref, pack_ref,
             wgu_vmem, wd_vmem):
        _load_once(wgu_hbm, wgu_vmem)
        _load_once(wd_hbm, wd_vmem)
        _zero_first(pack_ref)
        dh2v = dh2_ref[...]
        yh, ry = _rms(y_ref[...])
        dy, dw = _rms_bwd(dh2v, yh, ry, gpost_ref[...])
        pack_ref[ROW_FFN_POST:ROW_FFN_POST + 1, :] += _colsum(dw)
        dyb = dy.astype(BF16)
        dy_ref[...] = dyb
        dact = _dot_nt(dyb, wd_vmem[...])
        gate = gu_ref[:, :D_FF].astype(F32)
        up = gu_ref[:, D_FF:].astype(F32)
        sg = _sigmoid(gate)
        dgu_ref[:, :D_FF] = (dact * up * (sg * (1.0 + gate * (1.0 - sg)))).astype(BF16)
        dgu_ref[:, D_FF:] = (dact * gate * sg).astype(BF16)
        du2 = _dot_nt(dgu_ref[:, :pw], wgu_vmem[0])
        for q in range(1, N_CHIPS):
            du2 = du2 + _dot_nt(dgu_ref[:, pw * q:pw * (q + 1)], wgu_vmem[q])
        hh, rh = _rms(h1_ref[...])
        dh, dw = _rms_bwd(du2, hh, rh, gpre_ref[...])
        pack_ref[ROW_FFN_PRE:ROW_FFN_PRE + 1, :] += _colsum(dw)
        dh1_ref[...] = dh2v + dh

    return pl.pallas_call(
        body, name="ffn_bwd", grid=(T // ROW_TILE,),
        in_specs=[_row_spec(D_MODEL), _row_spec(D_MODEL), _row_spec(D_MODEL), _row_spec(2 * D_FF),
                  _full_spec((1, D_MODEL)), _full_spec((1, D_MODEL)), ANY_SPEC, ANY_SPEC],
        out_specs=[_row_spec(D_MODEL), _row_spec(2 * D_FF), _row_spec(D_MODEL), PACK_SPEC],
        out_shape=[jax.ShapeDtypeStruct((T, D_MODEL), BF16), jax.ShapeDtypeStruct((T, 2 * D_FF), BF16),
                   jax.ShapeDtypeStruct((T, D_MODEL), F32), jax.ShapeDtypeStruct((8, D_MODEL), F32)],
        scratch_shapes=[pltpu.VMEM(w_gu.shape, BF16), pltpu.VMEM(w_down.shape, BF16)],
        compiler_params=_params(("arbitrary",)),
    )(dh2, y, h1, gu, g_pre, g_post, w_gu, w_down)


def mix_out_bwd(dh1, mix, o_hg, proj_h, o_sb, norms, g_post, w_out, comm=None):
    T = dh1.shape[0]

    def body(dh1_ref, mix_ref, ohg_ref, hg_ref, osb_ref, nrm_ref, gp_ref, w_hbm, dmix_ref, dohg_ref, dhg_ref, dosb_ref,
             pack_ref, w_vmem):
        _load_once(w_hbm, w_vmem)
        _zero_first(pack_ref)
        mh, rm = _rms(mix_ref[...])
        dmix, dw = _rms_bwd(dh1_ref[...], mh, rm, gp_ref[...])
        pack_ref[ROW_ATTN_POST:ROW_ATTN_POST + 1, :] += _colsum(dw)
        dmb = dmix.astype(BF16)
        dmix_ref[...] = dmb
        dcat = _dot_nt(dmb, w_vmem[...])
        nrm = nrm_ref[...]
        g_hg, g_sb = nrm[:, :HG_WIDTH], nrm[:, HG_WIDTH:]
        hg = hg_ref[...]
        m = _mixer_out(ohg_ref[...], hg, osb_ref[...], g_hg, g_sb)
        d_hg = dcat[:, :HG_WIDTH]
        silu = hg * m["s_hg"]
        dhg_ref[...] = d_hg * (m["n_hg"] * g_hg) * (m["s_hg"] * (1.0 + hg * (1.0 - m["s_hg"])))
        dx, dw = _rms_bwd(d_hg * silu, m["n_hg"], m["r_hg"], g_hg)
        dohg_ref[...] = dx
        pack_ref[ROW_MIX_NORMS:ROW_MIX_NORMS + 1, :HG_WIDTH] += _colsum(dw)
        dx, dw = _rms_bwd(dcat[:, HG_WIDTH:], m["n_sb"], m["r_sb"], g_sb)
        dosb_ref[...] = dx
        pack_ref[ROW_MIX_NORMS:ROW_MIX_NORMS + 1, HG_WIDTH:] += _colsum(dw)

    return _pallas(
        body, comm=comm, edge=_grid_edge(T // ROW_TILE), name="mix_out_bwd", grid=(T // ROW_TILE,),
        in_specs=[_row_spec(D_MODEL), _row_spec(D_MODEL), _row_spec(HG_WIDTH), _row_spec(HG_WIDTH, 3), _row_spec(SB_WIDTH),
                  _full_spec((1, D_MODEL)), _full_spec((1, D_MODEL)), ANY_SPEC],
        out_specs=[_row_spec(D_MODEL), _row_spec(HG_WIDTH), _row_spec(HG_WIDTH), _row_spec(SB_WIDTH), PACK_SPEC],
        out_shape=[jax.ShapeDtypeStruct((T, D_MODEL), BF16), jax.ShapeDtypeStruct((T, HG_WIDTH), F32),
                   jax.ShapeDtypeStruct((T, HG_WIDTH), F32), jax.ShapeDtypeStruct((T, SB_WIDTH), F32),
                   jax.ShapeDtypeStruct((8, D_MODEL), F32)],
        scratch_shapes=[pltpu.VMEM(w_out.shape, BF16)],
        compiler_params=_params(("arbitrary",)),
    )(dh1, mix, o_hg, proj_h, o_sb, norms, g_post, w_out)


def in_proj_bwd(parts, x, dh1, g_pre, w_in, comm=None):
    T = x.shape[0]
    pw = w_in.shape[2]
    n_parts = len(parts)

    def body(*refs):
        part_refs = refs[:n_parts]
        x_ref, dh1_ref, g_ref, w_hbm, dproj_ref, dx_ref, pack_ref, w_vmem = refs[n_parts:]
        _load_once(w_hbm, w_vmem)
        _zero_first(pack_ref)
        for n, ref in enumerate(part_refs):
            dproj_ref[:, HG_WIDTH * n:HG_WIDTH * (n + 1)] = ref[...].astype(BF16)
        du = _dot_nt(dproj_ref[:, :pw], w_vmem[0])
        for q in range(1, N_CHIPS):
            du = du + _dot_nt(dproj_ref[:, pw * q:pw * (q + 1)], w_vmem[q])
        xh, r = _rms(x_ref[...])
        dx, dw = _rms_bwd(du, xh, r, g_ref[...])
        pack_ref[ROW_ATTN_PRE:ROW_ATTN_PRE + 1, :] += _colsum(dw)
        dx_ref[...] = dh1_ref[...] + dx

    return _pallas(
        body, comm=comm, edge=_grid_edge(T // ROW_TILE), name="in_proj_bwd", grid=(T // ROW_TILE,),
        in_specs=[_row_spec(HG_WIDTH)] * n_parts + [_row_spec(D_MODEL), _row_spec(D_MODEL), _full_spec((1, D_MODEL)), ANY_SPEC],
        out_specs=[_row_spec(n_parts * HG_WIDTH), _row_spec(D_MODEL), PACK_SPEC],
        out_shape=[jax.ShapeDtypeStruct((T, n_parts * HG_WIDTH), BF16), jax.ShapeDtypeStruct((T, D_MODEL), F32),
                   jax.ShapeDtypeStruct((8, D_MODEL), F32)],
        scratch_shapes=[pltpu.VMEM(w_in.shape, BF16)],
        compiler_params=_params(("arbitrary",)),
    )(*parts, x, dh1, g_pre, w_in)


def weight_grad(a, g, name, *, tm, tn, tk=512, col_pieces=False, comm=None):
    T, M = a.shape
    N = g.shape[1]
    tk = min(tk, T)
    steps = T // tk

    def body(a_ref, g_ref, o_ref):
        @pl.when(pl.program_id(2) == 0)
        def _():
            o_ref[...] = jnp.zeros(o_ref.shape, F32)

        o_ref[...] += _dot_tn(a_ref[...], g_ref[...]).reshape(o_ref.shape)

    if col_pieces:
        out_shape = jax.ShapeDtypeStruct((N // tn, M, tn), F32)
        out_spec = pl.BlockSpec((1, tm, tn), lambda i, j, k: (j, i, 0))
    else:
        out_shape = jax.ShapeDtypeStruct((M, N), F32)
        out_spec = pl.BlockSpec((tm, tn), lambda i, j, k: (i, j))
    grid = (M // tm, N // tn, steps)

    def edge():
        at = [pl.program_id(d) for d in range(3)]
        return ((at[0] == 0) & (at[1] == 0) & (at[2] == 0),
                (at[0] == grid[0] - 1) & (at[1] == grid[1] - 1) & (at[2] == grid[2] - 1))

    return _pallas(
        body, comm=comm, edge=edge, name=name, grid=grid,
        in_specs=[pl.BlockSpec((tk, tm), lambda i, j, k: (k, i)), pl.BlockSpec((tk, tn), lambda i, j, k: (k, j))],
        out_specs=[out_spec], out_shape=[out_shape],
        compiler_params=_params(("arbitrary", "arbitrary", "arbitrary")),
    )(a, g)


def _place():
    x, y, c = lax.axis_index("x"), lax.axis_index("y"), lax.axis_index("c")
    chips = [(1 - x, y), (x, 1 - y), (1 - x, 1 - y)]
    return x, y, c, chips


def _chip_index(cx, cy):
    return 2 * cx + cy


def _own_slot(piece, slots):
    me = _chip_index(lax.axis_index("x"), lax.axis_index("y"))
    landing = lax.empty((slots,) + piece.shape[1:], piece.dtype)
    return lax.dynamic_update_slice(landing, piece, (me,) + (0,) * (piece.ndim - 1))


def _rcopy(src, dst, send_sem, recv_sem, device):
    return pltpu.make_async_remote_copy(src_ref=src, dst_ref=dst, send_sem=send_sem, recv_sem=recv_sem,
                                        device_id=device, device_id_type=MESH)


def gather_weights(shards):
    n = len(shards)

    def body(*refs):
        ins, outs = refs[:n], refs[2 * n:3 * n]
        send_sems, recv_sems = refs[3 * n:]
        x, y, c, chips = _place()
        me = _chip_index(x, y)
        sibling = (x, y, 1 - c)

        def rows(w, core):
            half = ins[w].shape[0] // 2
            return pl.ds(core * half, half)

        sends = []
        for w in range(n):
            for j, chip in enumerate(chips):
                sends.append(_rcopy(ins[w].at[rows(w, c)], outs[w].at[me, rows(w, c)],
                                    send_sems.at[6 * w + j], recv_sems.at[6 * w + j], (*chip, c)))
        for cp in sends:
            cp.start()
        passed = []
        for w in range(n):
            for j, chip in enumerate(chips):
                block = outs[w].at[_chip_index(*chip), rows(w, c)]
                _rcopy(block, block, send_sems.at[6 * w + j], recv_sems.at[6 * w + j], (*chip, c)).wait_recv()
                cp = _rcopy(block, block, send_sems.at[6 * w + 3 + j], recv_sems.at[6 * w + 3 + j], sibling)
                cp.start()
                passed.append(cp)
        for w in range(n):
            for j, chip in enumerate(chips):
                block = outs[w].at[_chip_index(*chip), rows(w, 1 - c)]
                _rcopy(block, block, send_sems.at[6 * w + 3 + j], recv_sems.at[6 * w + 3 + j], sibling).wait_recv()
        for cp in sends + passed:
            cp.wait_send()

    filled = [_own_slot(s[None], N_CHIPS) for s in shards]
    return pl.pallas_call(
        body, name="gather_weights",
        in_specs=[ANY_SPEC] * (2 * n), out_specs=[ANY_SPEC] * n,
        out_shape=[jax.ShapeDtypeStruct(f.shape, f.dtype) for f in filled],
        input_output_aliases={n + w: w for w in range(n)},
        scratch_shapes=[pltpu.SemaphoreType.DMA((6 * n,)), pltpu.SemaphoreType.DMA((6 * n,))],
    )(*shards, *filled)


def _run_comm(comm, name):
    c_in, c_out = len(comm.inputs), len(comm.out_shape)

    def body(*refs):
        parts = refs[:c_in], refs[c_in:c_in + c_out], refs[c_in + c_out:]
        comm.start(*parts)
        comm.finish(*parts)

    return pl.pallas_call(
        body, name=name, in_specs=[ANY_SPEC] * c_in, out_specs=[ANY_SPEC] * c_out, out_shape=comm.out_shape,
        scratch_shapes=comm.scratch, input_output_aliases=comm.aliases)(*comm.inputs)


def _both(first, second):
    n_in, n_out, n_scr = len(first.inputs), len(first.out_shape), len(first.scratch)

    def split(ins, outs, scr):
        return (ins[:n_in], outs[:n_out], scr[:n_scr]), (ins[n_in:], outs[n_out:], scr[n_scr:])

    def start(*refs):
        a, b = split(*refs)
        first.start(*a)
        second.start(*b)

    def finish(*refs):
        a, b = split(*refs)
        first.finish(*a)
        second.finish(*b)

    aliases = dict(first.aliases)
    aliases.update({n_in + i: n_out + o for i, o in second.aliases.items()})
    return Comm(first.inputs + second.inputs, first.out_shape + second.out_shape, aliases,
                first.scratch + second.scratch, start, finish)


def _dma_sems(count):
    return [pltpu.SemaphoreType.DMA((count,)), pltpu.SemaphoreType.DMA((count,))]


def gather_over_ici(shards):
    n = len(shards)

    def copies(ins, outs, sems):
        send_sems, recv_sems = sems
        x, y, c, chips = _place()
        me = _chip_index(x, y)
        pairs = []
        for w in range(n):
            half = shards[w].shape[0] // 2
            rows = pl.ds(c * half, half)
            for j, chip in enumerate(chips):
                k = 3 * w + j
                landed = outs[w].at[_chip_index(*chip), rows]
                pairs.append((_rcopy(ins[w].at[rows], outs[w].at[me, rows], send_sems.at[k], recv_sems.at[k], (*chip, c)),
                              _rcopy(landed, landed, send_sems.at[k], recv_sems.at[k], (*chip, c))))
        return pairs

    def start(*refs):
        for send, _ in copies(*refs):
            send.start()

    def finish(*refs):
        pairs = copies(*refs)
        for _, landed in pairs:
            landed.wait_recv()
        for send, _ in pairs:
            send.wait_send()

    filled = [_own_slot(s[None], N_CHIPS) for s in shards]
    return Comm(list(shards) + filled, [jax.ShapeDtypeStruct(f.shape, f.dtype) for f in filled],
                {n + w: w for w in range(n)}, _dma_sems(3 * n), start, finish)


def gather_over_d2d(landed):
    n = len(landed)

    def copies(ins, outs, sems):
        send_sems, recv_sems = sems
        x, y, c, chips = _place()
        sibling = (x, y, 1 - c)
        pairs = []
        for w in range(n):
            half = landed[w].shape[1] // 2
            for j, chip in enumerate(chips):
                k = 3 * w + j
                mine = outs[w].at[_chip_index(*chip), pl.ds(c * half, half)]
                theirs = outs[w].at[_chip_index(*chip), pl.ds((1 - c) * half, half)]
                pairs.append((_rcopy(mine, mine, send_sems.at[k], recv_sems.at[k], sibling),
                              _rcopy(theirs, theirs, send_sems.at[k], recv_sems.at[k], sibling)))
        return pairs

    def start(*refs):
        for send, _ in copies(*refs):
            send.start()

    def finish(*refs):
        pairs = copies(*refs)
        for _, arrived in pairs:
            arrived.wait_recv()
        for send, _ in pairs:
            send.wait_send()

    return Comm(list(landed), [jax.ShapeDtypeStruct(a.shape, a.dtype) for a in landed], {w: w for w in range(n)},
                _dma_sems(3 * n), start, finish)


def core_halves(grads):
    n = len(grads)

    def copies(ins, outs, sems):
        send_sems, recv_sems = sems
        x, y, c, _ = _place()
        out = []
        for w in range(n):
            half = grads[w].shape[1] // 2
            out.append(_rcopy(ins[w].at[:, pl.ds((1 - c) * half, half), :], outs[w],
                              send_sems.at[w], recv_sems.at[w], (x, y, 1 - c)))
        return out

    def start(*refs):
        for cp in copies(*refs):
            cp.start()

    def finish(*refs):
        for cp in copies(*refs):
            cp.wait()

    return Comm(list(grads), [jax.ShapeDtypeStruct((g.shape[0], g.shape[1] // 2, g.shape[2]), g.dtype) for g in grads],
                {}, _dma_sems(n), start, finish)


def chip_partials(partials):
    n = len(partials)

    def copies(ins, outs, sems):
        send_sems, recv_sems = sems
        x, y, c, chips = _place()
        me = _chip_index(x, y)
        pairs = []
        for w in range(n):
            for j, chip in enumerate(chips):
                k = 3 * w + j
                landed = outs[w].at[_chip_index(*chip)]
                pairs.append((_rcopy(ins[w].at[_chip_index(*chip)], outs[w].at[me], send_sems.at[k], recv_sems.at[k],
                                     (*chip, c)),
                              _rcopy(landed, landed, send_sems.at[k], recv_sems.at[k], (*chip, c))))
        return pairs

    def start(*refs):
        for send, _ in copies(*refs):
            send.start()

    def finish(*refs):
        pairs = copies(*refs)
        for _, landed in pairs:
            landed.wait_recv()
        for send, _ in pairs:
            send.wait_send()

    me = _chip_index(lax.axis_index("x"), lax.axis_index("y"))
    filled = [_own_slot(lax.dynamic_index_in_dim(p, me, 0, keepdims=True), N_CHIPS) for p in partials]
    return Comm(list(partials) + filled, [jax.ShapeDtypeStruct(p.shape, p.dtype) for p in partials],
                {n + w: w for w in range(n)}, _dma_sems(3 * n), start, finish)


def join_core_halves(grads):
    n = len(grads)

    def body(*refs):
        outs = refs[n:2 * n]
        send_sems, recv_sems = refs[2 * n:]
        x, y, c, _ = _place()
        sibling = (x, y, 1 - c)
        copies = []
        for w in range(n):
            half = outs[w].shape[0] // 2
            mine = outs[w].at[pl.ds(c * half, half), :]
            copies.append(_rcopy(mine, mine, send_sems.at[w], recv_sems.at[w], sibling))
        for cp in copies:
            cp.start()
        for w in range(n):
            half = outs[w].shape[0] // 2
            theirs = outs[w].at[pl.ds((1 - c) * half, half), :]
            _rcopy(theirs, theirs, send_sems.at[w], recv_sems.at[w], sibling).wait_recv()
        for cp in copies:
            cp.wait_send()

    return pl.pallas_call(
        body, name="join_core_halves",
        in_specs=[ANY_SPEC] * n, out_specs=[ANY_SPEC] * n,
        out_shape=[jax.ShapeDtypeStruct(g.shape, g.dtype) for g in grads],
        input_output_aliases={w: w for w in range(n)},
        scratch_shapes=[pltpu.SemaphoreType.DMA((n,)), pltpu.SemaphoreType.DMA((n,))],
    )(*grads)


def _elementwise_rows(rows, cap=512):
    for t in range(min(rows, cap), 0, -8):
        if rows % t == 0 and t % 16 == 0:
            return t
    return rows


def add_core_halves(grad, got, core, name):
    _, rows, cols = got.shape
    tr = _elementwise_rows(rows)
    nt = rows // tr

    def body(core_ref, a_ref, b_ref, o_ref):
        o_ref[...] = (a_ref[...] + b_ref[...]).astype(BF16)

    spec = pl.BlockSpec((1, tr, cols), lambda q, i, core_ref: (q, i, 0))
    own = pl.BlockSpec((1, tr, cols), lambda q, i, core_ref: (q, core_ref[0] * nt + i, 0))
    return pl.pallas_call(
        body, name=name,
        grid_spec=pltpu.PrefetchScalarGridSpec(num_scalar_prefetch=1, grid=(N_CHIPS, nt), in_specs=[own, spec],
                                               out_specs=spec),
        out_shape=jax.ShapeDtypeStruct(got.shape, BF16),
        compiler_params=_params(("arbitrary", "arbitrary")),
    )(core, grad, got)


def add_chip_partials(parts, core, name):
    _, rows, cols = parts.shape
    tr = _elementwise_rows(rows)
    nt = rows // tr

    def body(core_ref, p_ref, o_ref):
        acc = p_ref[0].astype(F32)
        for q in range(1, N_CHIPS):
            acc = acc + p_ref[q].astype(F32)
        o_ref[...] = acc

    return pl.pallas_call(
        body, name=name,
        grid_spec=pltpu.PrefetchScalarGridSpec(
            num_scalar_prefetch=1, grid=(nt,),
            in_specs=[pl.BlockSpec((N_CHIPS, tr, cols), lambda i, core_ref: (0, i, 0))],
            out_specs=pl.BlockSpec((tr, cols), lambda i, core_ref: (core_ref[0] * nt + i, 0))),
        out_shape=jax.ShapeDtypeStruct((2 * rows, cols), F32),
        compiler_params=_params(("arbitrary",)),
    )(core, parts)


def _adamw_math(w, g, m, v):
    m = ADAM_B1 * m + (1.0 - ADAM_B1) * g
    v = ADAM_B2 * v + (1.0 - ADAM_B2) * (g * g)
    m_hat = m / (1.0 - ADAM_B1 ** ADAM_STEP)
    v_hat = v / (1.0 - ADAM_B2 ** ADAM_STEP)
    delta = -ADAM_LR * (m_hat / (jnp.sqrt(v_hat) + ADAM_EPS) + ADAM_WD * w)
    return delta, m, v


def adamw(w, g, m, v, name):
    rows, cols = w.shape
    tr = _elementwise_rows(rows, 256)

    def body(w_ref, g_ref, m_ref, v_ref, d_ref, nm_ref, nv_ref):
        d, nm, nv = _adamw_math(w_ref[...], g_ref[...], m_ref[...], v_ref[...])
        d_ref[...] = d
        nm_ref[...] = nm
        nv_ref[...] = nv

    spec = pl.BlockSpec((tr, cols), lambda i: (i, 0))
    return pl.pallas_call(
        body, name=name, grid=(rows // tr,), in_specs=[spec] * 4, out_specs=[spec] * 3,
        out_shape=[jax.ShapeDtypeStruct((rows, cols), F32)] * 3,
        compiler_params=_params(("arbitrary",)),
    )(w, g, m, v)


def reduce_small(packs, w, m, v):
    n = len(packs)
    n_dev = 8
    flips = [(fx, fy, fc) for fx in (0, 1) for fy in (0, 1) for fc in (0, 1)][1:]

    def body(*refs):
        pack_refs = refs[:n]
        w_ref, m_ref, v_ref, g_out, d_out, m_out, v_out, mine, slots, send_sems, recv_sems = refs[n:]
        x, y, c, _ = _place()
        me = 4 * x + 2 * y + c
        acc = pack_refs[0][...]
        for ref in pack_refs[1:]:
            acc = acc + ref[...]
        mine[...] = acc
        sends = []
        for k, (fx, fy, fc) in enumerate(flips):
            peer = (x ^ fx, y ^ fy, c ^ fc)
            sends.append(_rcopy(mine, slots.at[me], send_sems.at[k], recv_sems.at[me], peer))
        for cp in sends:
            cp.start()
        slots[me] = acc
        for fx, fy, fc in flips:
            src = 4 * (x ^ fx) + 2 * (y ^ fy) + (c ^ fc)
            _rcopy(mine, slots.at[src], send_sems.at[0], recv_sems.at[src], (x, y, c)).wait_recv()
        for cp in sends:
            cp.wait_send()
        total = slots[0]
        for d in range(1, n_dev):
            total = total + slots[d]
        g_out[...] = total
        d, nm, nv = _adamw_math(w_ref[...], total, m_ref[...], v_ref[...])
        d_out[...] = d
        m_out[...] = nm
        v_out[...] = nv

    vm = pl.BlockSpec(memory_space=pltpu.VMEM)
    return pl.pallas_call(
        body, name="reduce_small",
        in_specs=[vm] * (n + 3), out_specs=[vm] * 4,
        out_shape=[jax.ShapeDtypeStruct((8, D_MODEL), F32)] * 4,
        scratch_shapes=[pltpu.VMEM((8, D_MODEL), F32), pltpu.VMEM((n_dev, 8, D_MODEL), F32),
                        pltpu.SemaphoreType.DMA((len(flips),)), pltpu.SemaphoreType.DMA((n_dev,))],
    )(*packs, w, m, v)


def _column_pieces(g):
    return g.reshape(g.shape[0], N_CHIPS, g.shape[1] // N_CHIPS).transpose(1, 0, 2)


def _pack_small(attn_pre, gamma, hg_norm, sb_norm, attn_post, ffn_pre, ffn_post):
    rows = [attn_pre, gamma.reshape(1, D_MODEL), jnp.concatenate([hg_norm, sb_norm], axis=1), attn_post, ffn_pre, ffn_post,
            jnp.zeros((2, D_MODEL), F32)]
    return jnp.concatenate(rows, axis=0)


def _unpack_small(pack):
    return (pack[ROW_ATTN_PRE:ROW_ATTN_PRE + 1], pack[ROW_GAMMA].reshape(2, HG_WIDTH),
            pack[ROW_MIX_NORMS:ROW_MIX_NORMS + 1, :HG_WIDTH], pack[ROW_MIX_NORMS:ROW_MIX_NORMS + 1, HG_WIDTH:],
            pack[ROW_ATTN_POST:ROW_ATTN_POST + 1], pack[ROW_FFN_PRE:ROW_FFN_PRE + 1], pack[ROW_FFN_POST:ROW_FFN_POST + 1])


def kernel(x, p, attn_pre_norm, w_in, hg_lower_gamma, hg_out_norm, sb_out_norm, w_out, attn_post_norm, ffn_pre_norm, w_gate_up, w_down, ffn_post_norm, ple_proj, ple_gate, loss_target, m_attn_pre_norm, m_w_in, m_hg_lower_gamma, m_hg_out_norm, m_sb_out_norm, m_w_out, m_attn_post_norm, m_ffn_pre_norm, m_w_gate_up, m_w_down, m_ffn_post_norm, m_ple_proj, m_ple_gate, v_attn_pre_norm, v_w_in, v_hg_lower_gamma, v_hg_out_norm, v_sb_out_norm, v_w_out, v_attn_post_norm, v_ffn_pre_norm, v_w_gate_up, v_w_down, v_ffn_post_norm, v_ple_proj, v_ple_gate):
    x2 = x[0]
    p2 = p[0, 0]
    target = loss_target[0]
    big = dict(w_in=(w_in, m_w_in, v_w_in), w_out=(w_out, m_w_out, v_w_out), w_gate_up=(w_gate_up, m_w_gate_up, v_w_gate_up),
               w_down=(w_down, m_w_down, v_w_down), ple_proj=(ple_proj, m_ple_proj, v_ple_proj),
               ple_gate=(ple_gate, m_ple_gate, v_ple_gate))
    names = list(big)
    big = {k: tuple(a[0] for a in t) for k, t in big.items()}

    shard16 = {k: big[k][0].astype(BF16) for k in names}
    w_in_full, = gather_weights([shard16["w_in"]])
    mix_norms = jnp.concatenate([hg_out_norm, sb_out_norm], axis=1)
    small_ones = ["w_out", "ple_proj", "ple_gate"]

    proj_h, sqkv, u1, *landed_small = in_proj_fwd(
        x2, attn_pre_norm, w_in_full, comm=gather_over_ici([shard16[k] for k in small_ones]))
    o_sb, sb_totals, sb_first, landed_gu = sb_fwd(sqkv, comm=gather_over_ici([shard16["w_gate_up"]]))
    o_hg, states, landed_down, *full_small = hgrn2_fwd(
        proj_h, hg_lower_gamma, comm=_both(gather_over_ici([shard16["w_down"]]), gather_over_d2d(landed_small)))
    full = dict(zip(small_ones, full_small), w_in=w_in_full)
    w_out_full = full["w_out"].reshape(D_MODEL, D_MODEL)
    w_pg_full = full["ple_gate"].reshape(D_MODEL, D_MODEL)
    cat, mix, h1, full["w_gate_up"], full["w_down"] = mix_out_fwd(
        o_hg, proj_h, o_sb, x2, mix_norms, attn_post_norm, w_out_full, comm=gather_over_d2d([landed_gu, landed_down]))
    w_down_full = full["w_down"].reshape(D_FF, D_MODEL)
    u2, gu, act, y, h2 = ffn_fwd(h1, ffn_pre_norm, ffn_post_norm, full["w_gate_up"], w_down_full)

    core = lax.axis_index("c").astype(jnp.int32).reshape(1)
    de, ds, dh2, h2b, pb, pack_loss = ple_loss(h2, p2, target, full["ple_proj"], w_pg_full)
    dy, dgu, dh1, pack_ffn = ffn_bwd(dh2, y, h1, gu, ffn_pre_norm, ffn_post_norm, full["w_gate_up"], w_down_full)
    local = dict(
        w_gate_up=weight_grad(u2, dgu, "grad_w_gate_up", tm=D_MODEL, tn=full["w_gate_up"].shape[2], tk=1024, col_pieces=True)[0],
        w_down=weight_grad(act, dy, "grad_w_down", tm=D_FF // 2, tn=D_MODEL, tk=1024)[0].reshape(full["w_down"].shape),
        ple_proj=_column_pieces(weight_grad(pb, de, "grad_ple_proj", tm=pb.shape[1], tn=D_MODEL, tk=1024)[0]),
        ple_gate=weight_grad(h2b, ds, "grad_ple_gate", tm=D_MODEL, tn=D_MODEL, tk=1024)[0].reshape(full["ple_gate"].shape),
    )
    early = list(local)
    dmix, do_hg, dhg, do_sb, pack_mix, *got = mix_out_bwd(
        dh1, mix, o_hg, proj_h, o_sb, mix_norms, attn_post_norm, w_out_full, comm=core_halves([local[k] for k in early]))
    partial = [add_core_halves(local[k], g, core, "add_core_halves_" + k) for k, g in zip(early, got)]
    local["w_out"] = weight_grad(cat, dmix, "grad_w_out", tm=D_MODEL, tn=D_MODEL, tk=1024)[0].reshape(full["w_out"].shape)
    dsq, dsk, dsv = sb_bwd(sqkv, do_sb, sb_totals, sb_first)
    dhq, dhf, dhi, pack_hg, *by_source, got_out = hgrn2_bwd(
        proj_h, hg_lower_gamma, states, do_hg, comm=_both(chip_partials(partial[:2]), core_halves([local["w_out"]])))
    early.append("w_out")
    partial.append(add_core_halves(local["w_out"], got_out, core, "add_core_halves_w_out"))
    dproj, grad_x, pack_in = in_proj_bwd([dhq, dhf, dhi, dhg, dsq, dsk, dsv], x2, dh1, attn_pre_norm, full["w_in"])

    late = ["w_in"]
    local["w_in"], *more = weight_grad(u1, dproj, "grad_w_in", tm=D_MODEL, tn=full["w_in"].shape[2], tk=1024,
                                       col_pieces=True, comm=chip_partials(partial[2:]))
    halves = {k: add_chip_partials(s, core, "add_chip_partials_" + k) for k, s in zip(early, by_source + more)}
    got = _run_comm(core_halves([local[k] for k in late]), "exchange_core_halves")
    partial = [add_core_halves(local[k], g, core, "add_core_halves_" + k) for k, g in zip(late, got)]
    by_source = _run_comm(chip_partials(partial), "exchange_chip_partials")
    halves.update({k: add_chip_partials(s, core, "add_chip_partials_" + k) for k, s in zip(late, by_source)})
    grads = dict(zip(names, join_core_halves([halves[k] for k in names])))

    upd = {k: adamw(big[k][0], grads[k], big[k][1], big[k][2], "adamw_" + k) for k in names}

    small = reduce_small(
        [pack_loss, pack_ffn, pack_mix, pack_hg, pack_in],
        _pack_small(attn_pre_norm, hg_lower_gamma, hg_out_norm, sb_out_norm, attn_post_norm, ffn_pre_norm, ffn_post_norm),
        _pack_small(m_attn_pre_norm, m_hg_lower_gamma, m_hg_out_norm, m_sb_out_norm, m_attn_post_norm, m_ffn_pre_norm, m_ffn_post_norm),
        _pack_small(v_attn_pre_norm, v_hg_lower_gamma, v_hg_out_norm, v_sb_out_norm, v_attn_post_norm, v_ffn_pre_norm, v_ffn_post_norm),
    )
    loss = small[0][ROW_LOSS, 0]
    s_grad, s_delta, s_m, s_v = (_unpack_small(t) for t in small)

    def ordered(small_vals, big_vals):
        a_pre, gam, hg_n, sb_n, a_post, f_pre, f_post = small_vals
        b = {k: big_vals[k][None] for k in names}
        return (a_pre, b["w_in"], gam, hg_n, sb_n, b["w_out"], a_post, f_pre, b["w_gate_up"], b["w_down"], f_post,
                b["ple_proj"], b["ple_gate"])

    return (loss, grad_x[None],
            *ordered(s_grad, grads),
            *ordered(s_delta, {k: upd[k][0] for k in names}),
            *ordered(s_m, {k: upd[k][1] for k in names}),
            *ordered(s_v, {k: upd[k][2] for k in names}))
```

```python
from typing import Callable, NamedTuple

import numpy as np
import jax
import jax.numpy as jnp
from jax import lax
from jax.experimental import pallas as pl
from jax.experimental.pallas import tpu as pltpu

F32 = jnp.float32
BF16 = jnp.bfloat16
MESH = pl.DeviceIdType.MESH

RMS_EPS = 1e-6
D_MODEL = 1024
HG_WIDTH = 512
HG_HEADS = 4
HG_DK = 128
HG_CHUNK = 64
HG_LEVELS = (32, 16, 8, 4, 2, 1)
HG_CHUNKS_PER_STEP = 2
SB_WIDTH = 512
SB_BLOCK = 128
SB_DH = 64
SB_SCALE = SB_DH ** -0.5
SB_UNDERFLOW_LOG = -87.5
SB_UNROLL = 2
SB_GROUP = 2
D_FF = 2816
N_CHIPS = 4
ROW_TILE = 256
WIDE_ROW_TILE = 512
V7X_VMEM_LIMIT = 56 * 1024 * 1024

ADAM_LR = 0.001
ADAM_B1 = 0.9
ADAM_B2 = 0.999
ADAM_EPS = 1e-08
ADAM_WD = 0.01
ADAM_STEP = 10

ROW_ATTN_PRE, ROW_GAMMA, ROW_MIX_NORMS, ROW_ATTN_POST, ROW_FFN_PRE, ROW_FFN_POST, ROW_LOSS = range(7)


def _params(sem=None, vmem=V7X_VMEM_LIMIT):
    return pltpu.CompilerParams(dimension_semantics=sem, vmem_limit_bytes=vmem)


def _dot(a, b):
    return jnp.dot(a.astype(BF16), b.astype(BF16), preferred_element_type=F32)


def _dot_nt(a, b):
    return lax.dot_general(a.astype(BF16), b.astype(BF16), (((1,), (1,)), ((), ())), preferred_element_type=F32)


def _dot_tn(a, b):
    return lax.dot_general(a.astype(BF16), b.astype(BF16), (((0,), (0,)), ((), ())), preferred_element_type=F32)


def _split(x):
    hi = x.astype(BF16)
    lo = (x - hi.astype(F32)).astype(BF16)
    return hi, lo


def _sum01_left(m01, x):
    hi, lo = _split(x)
    return jnp.dot(m01, hi, preferred_element_type=F32) + jnp.dot(m01, lo, preferred_element_type=F32)


def _sum01_right(x, m01_twice):
    hi, lo = _split(x)
    return jnp.dot(jnp.concatenate([hi, lo], axis=1), m01_twice, preferred_element_type=F32)


def _rms(x):
    r = lax.rsqrt(jnp.mean(x * x, axis=-1, keepdims=True) + RMS_EPS)
    return x * r, r


def _rms_bwd(dy, xhat, r, w):
    dxh = dy * w
    dx = r * (dxh - xhat * jnp.mean(dxh * xhat, axis=-1, keepdims=True))
    return dx, dy * xhat


def _sigmoid(x):
    return 1.0 / (1.0 + jnp.exp(-x))


def _neg_softplus(z):
    return -(jnp.maximum(z, 0.0) + jnp.log(1.0 + jnp.exp(-jnp.abs(z))))


def _colsum(x):
    return jnp.sum(x, axis=0, keepdims=True)


def _load_once(src_hbm, dst_vmem):
    @pl.when(pl.program_id(0) == 0)
    def _():
        pltpu.sync_copy(src_hbm, dst_vmem)


def _zero_first(ref):
    @pl.when(pl.program_id(0) == 0)
    def _():
        ref[...] = jnp.zeros(ref.shape, ref.dtype)


def _row_spec(width, col=0, rows=ROW_TILE):
    return pl.BlockSpec((rows, width), lambda i, col=col: (i, col))


def _wide_spec(width, col=0):
    return _row_spec(width, col, WIDE_ROW_TILE)


def _wide_tile(T):
    assert T % WIDE_ROW_TILE == 0
    return WIDE_ROW_TILE


def _full_spec(shape):
    return pl.BlockSpec(shape, lambda *_: (0,) * len(shape))


ANY_SPEC = pl.BlockSpec(memory_space=pl.ANY)
PACK_SPEC = _full_spec((8, D_MODEL))


class Comm(NamedTuple):
    inputs: list
    out_shape: list
    aliases: dict
    scratch: list
    start: Callable
    finish: Callable


def _pallas(body, *, comm=None, edge=None, in_specs, out_specs, out_shape, scratch_shapes=(), **kw):
    if comm is None:
        return pl.pallas_call(body, in_specs=in_specs, out_specs=out_specs, out_shape=out_shape,
                              scratch_shapes=scratch_shapes, **kw)
    n_in, n_out, n_scr = len(in_specs), len(out_specs), len(scratch_shapes)
    c_in, c_out = len(comm.inputs), len(comm.out_shape)

    def both(*refs):
        ins, c_ins = refs[:n_in], refs[n_in:n_in + c_in]
        outs = refs[n_in + c_in:n_in + c_in + n_out]
        c_outs = refs[n_in + c_in + n_out:n_in + c_in + n_out + c_out]
        rest = refs[n_in + c_in + n_out + c_out:]
        scr, c_scr = rest[:n_scr], rest[n_scr:]
        first, last = edge()

        @pl.when(first)
        def _():
            comm.start(c_ins, c_outs, c_scr)

        body(*ins, *outs, *scr)

        @pl.when(last)
        def _():
            comm.finish(c_ins, c_outs, c_scr)

    call = pl.pallas_call(
        both, in_specs=list(in_specs) + [ANY_SPEC] * c_in, out_specs=list(out_specs) + [ANY_SPEC] * c_out,
        out_shape=list(out_shape) + list(comm.out_shape), scratch_shapes=list(scratch_shapes) + list(comm.scratch),
        input_output_aliases={n_in + a: n_out + b for a, b in comm.aliases.items()}, **kw)
    return lambda *args: call(*args, *comm.inputs)


def _grid_edge(steps):
    return lambda: (pl.program_id(0) == 0, pl.program_id(0) == steps - 1)


def in_proj_fwd(x, g_pre, w_in, comm=None):
    T = x.shape[0]
    pw = w_in.shape[2]

    def body(x_ref, g_ref, w_hbm, ph_ref, sqkv_ref, u_ref, w_vmem, proj_s):
        _load_once(w_hbm, w_vmem)
        xh, _ = _rms(x_ref[...])
        u = (xh * g_ref[...]).astype(BF16)
        u_ref[...] = u
        for q in range(N_CHIPS):
            proj_s[:, pw * q:pw * (q + 1)] = jnp.dot(u, w_vmem[q], preferred_element_type=F32)
        ph_ref[...] = proj_s[:, :4 * HG_WIDTH]
        sqkv_ref[:, :SB_WIDTH] = (proj_s[:, 4 * HG_WIDTH:4 * HG_WIDTH + SB_WIDTH] * SB_SCALE).astype(BF16)
        sqkv_ref[:, SB_WIDTH:] = proj_s[:, 4 * HG_WIDTH + SB_WIDTH:].astype(BF16)

    return _pallas(
        body, comm=comm, edge=_grid_edge(T // _wide_tile(T)), name="in_proj_fwd", grid=(T // _wide_tile(T),),
        in_specs=[_wide_spec(D_MODEL), _full_spec((1, D_MODEL)), ANY_SPEC],
        out_specs=[_wide_spec(4 * HG_WIDTH), _wide_spec(3 * SB_WIDTH), _wide_spec(D_MODEL)],
        out_shape=[jax.ShapeDtypeStruct((T, 4 * HG_WIDTH), F32), jax.ShapeDtypeStruct((T, 3 * SB_WIDTH), BF16),
                   jax.ShapeDtypeStruct((T, D_MODEL), BF16)],
        scratch_shapes=[pltpu.VMEM(w_in.shape, BF16), pltpu.VMEM((_wide_tile(T), N_CHIPS * pw), F32)],
        compiler_params=_params(("arbitrary",)),
    )(x, g_pre, w_in)


def _hg_sum_matrix():
    C = HG_CHUNK
    t = np.arange(C)[:, None]
    j = np.arange(C)[None, :]
    mats = [j <= t, j > t]
    for h in HG_LEVELS:
        start = (t // (2 * h)) * (2 * h)
        upper = (t & h) != 0
        mats.append(np.where(upper, (j >= start + h) & (j <= t), (j > t) & (j <= start + h - 1)))
    return np.concatenate(mats, 0).astype(np.float32)


def _hg_level_masks():
    C = HG_CHUNK
    t = lax.broadcasted_iota(jnp.int32, (C, C), 0)
    s = lax.broadcasted_iota(jnp.int32, (C, C), 1)
    x = t ^ s
    masks = [t == s]
    for h in HG_LEVELS:
        masks.append((x >= h) & (x < 2 * h) & (t > s))
    return masks


def _hg_gates(hq, hf, gamma):
    lb = 1.0 / (1.0 + jnp.exp(gamma[1:2, :] - gamma[0:1, :]))
    sq = _sigmoid(hq)
    q = hq * sq
    sig = _sigmoid(hf)
    nsig = _sigmoid(-hf)
    f = lb + (1.0 - lb) * sig
    k = (1.0 - lb) * nsig
    g = jnp.log(f)
    return q, k, g, dict(lb=lb, sq=sq, sig=sig, nsig=nsig, f=f)


def _hg_head_decays(A, h):
    C, K = HG_CHUNK, HG_DK
    sl = slice(K * h, K * (h + 1))
    blocks = [A[C * r:C * (r + 1), sl] for r in range(2 + len(HG_LEVELS))]
    return blocks[0], blocks[1], [None] + blocks[2:]


def _hg_products(q, k, levels):
    return [_dot_nt(q, k)] + [_dot_nt(q * a, k * a) for a in levels[1:]]


def _hg_select(prods, masks):
    sc = jnp.where(masks[0], prods[0], 0.0)
    for p, m in zip(prods[1:], masks[1:]):
        sc = jnp.where(m, p, sc)
    return sc


def hgrn2_fwd(proj_h, gamma, comm=None):
    T = proj_h.shape[0]
    C, K, H, S = HG_CHUNK, HG_DK, HG_HEADS, HG_CHUNKS_PER_STEP
    n_steps = T // (S * C)
    msum = jnp.asarray(_hg_sum_matrix(), BF16)

    def body(hq_ref, hf_ref, hi_ref, gam_ref, msum_ref, o_ref, st_ref, st_s):
        _zero_first(st_s)
        q, k, g, _ = _hg_gates(hq_ref[...], hf_ref[...], gam_ref[...])
        v = hi_ref[...]
        masks = _hg_level_masks()
        parts = []
        for s in range(S):
            rows = slice(C * s, C * (s + 1))
            A = jnp.exp(_sum01_left(msum_ref[...], g[rows]))
            for h in range(H):
                sl = slice(K * h, K * (h + 1))
                ab, ar, levels = _hg_head_decays(A, h)
                parts.append(dict(s=s, h=h, rows=rows, sl=sl, ab=ab, ar=ar, levels=levels,
                                  q=q[rows, sl], k=k[rows, sl], v=v[rows, sl]))
        for pt in parts:
            pt["prods"] = _hg_products(pt["q"], pt["k"], pt["levels"])
            pt["grown"] = _dot_tn(pt["v"], pt["k"] * pt["ar"])
        for pt in parts:
            pt["sc"] = _hg_select(pt["prods"], masks)
        state = [st_s[h] for h in range(H)]
        for pt in parts:
            h, ab = pt["h"], pt["ab"]
            o_ref[pt["rows"], pt["sl"]] = _dot_nt(pt["q"] * ab, state[h]) + _dot(pt["sc"], pt["v"])
            state[h] = state[h] * ab[C - 1:C, :] + pt["grown"]
            st_ref[pt["s"], h] = state[h]
        for h in range(H):
            st_s[h] = state[h]

    blk = lambda col: pl.BlockSpec((S * C, HG_WIDTH), lambda c, col=col: (c, col))
    return _pallas(
        body, comm=comm, edge=_grid_edge(n_steps), name="hgrn2_fwd", grid=(n_steps,),
        in_specs=[blk(0), blk(1), blk(2), _full_spec((2, HG_WIDTH)), _full_spec(msum.shape)],
        out_specs=[blk(0), pl.BlockSpec((S, H, K, K), lambda c: (c, 0, 0, 0))],
        out_shape=[jax.ShapeDtypeStruct((T, HG_WIDTH), F32), jax.ShapeDtypeStruct((S * n_steps, H, K, K), F32)],
        scratch_shapes=[pltpu.VMEM((H, K, K), F32)],
        compiler_params=_params(("arbitrary",)),
    )(proj_h, proj_h, proj_h, gamma, msum)


def hgrn2_bwd(proj_h, gamma, states, do, comm=None):
    T = proj_h.shape[0]
    C, K, H, S = HG_CHUNK, HG_DK, HG_HEADS, HG_CHUNKS_PER_STEP
    n_steps = T // (S * C)
    n_sums = 2 + len(HG_LEVELS)
    msum = jnp.asarray(_hg_sum_matrix(), BF16)
    msum_t = jnp.asarray(_hg_sum_matrix().T, BF16)

    def body(hq_ref, hf_ref, hi_ref, do_ref, gam_ref, msum_ref, msum_t_ref, st_prev_ref, st_ref,
             dhq_ref, dhf_ref, dhi_ref, pack_ref, dst_s, dlb_s, dq_s, dk_s, de_s):
        step = pl.program_id(0)
        _zero_first(dst_s)
        _zero_first(dlb_s)
        _zero_first(pack_ref)
        hq = hq_ref[...]
        q, k, g, aux = _hg_gates(hq, hf_ref[...], gam_ref[...])
        v = hi_ref[...]
        do_all = do_ref[...]
        masks = _hg_level_masks()
        is_last_row = lax.broadcasted_iota(jnp.int32, (C, K), 0) == C - 1
        has_prev = (step < n_steps - 1).astype(F32)
        parts = []
        for s in reversed(range(S)):
            rows = slice(C * s, C * (s + 1))
            A = jnp.exp(_sum01_left(msum_ref[...], g[rows]))
            for h in range(H):
                sl = slice(K * h, K * (h + 1))
                ab, ar, levels = _hg_head_decays(A, h)
                st_in = st_prev_ref[0, h] * has_prev if s == 0 else st_ref[s - 1, h]
                parts.append(dict(s=s, h=h, rows=rows, sl=sl, ab=ab, ar=ar, levels=levels, st_in=st_in,
                                  q=q[rows, sl], k=k[rows, sl], v=v[rows, sl], do=do_all[rows, sl]))
        for pt in parts:
            pt["prods"] = _hg_products(pt["q"], pt["k"], pt["levels"])
            pt["da"] = _dot_nt(pt["do"], pt["v"])
            pt["t1"] = pt["ab"] * _dot(pt["do"], pt["st_in"])
            pt["dst_add"] = _dot_tn(pt["do"], pt["q"] * pt["ab"])
        dstate = [dst_s[h] for h in range(H)]
        for pt in parts:
            h = pt["h"]
            pt["dst_out"] = dstate[h]
            pt["t2"] = pt["ar"] * _dot(pt["v"], dstate[h])
            pt["dv_state"] = _dot_nt(pt["k"] * pt["ar"], dstate[h])
            dstate[h] = dstate[h] * pt["ab"][C - 1:C, :] + pt["dst_add"]
        for h in range(H):
            dst_s[h] = dstate[h]
        for pt in parts:
            pt["sc"] = _hg_select(pt["prods"], masks)
            pt["dam"] = [jnp.where(m, pt["da"], 0.0) for m in masks]
        for pt in parts:
            qh, kh = pt["q"], pt["k"]
            pt["dq_parts"] = [_dot(pt["dam"][0], kh)] + [
                a * _dot(dam, kh * a) for a, dam in zip(pt["levels"][1:], pt["dam"][1:])]
            pt["dk_parts"] = [_dot_tn(pt["dam"][0], qh)] + [
                a * _dot_tn(dam, qh * a) for a, dam in zip(pt["levels"][1:], pt["dam"][1:])]
            pt["dv_intra"] = _dot_tn(pt["sc"], pt["do"])
        for pt in parts:
            s, rows, sl, qh, kh, ab = pt["s"], pt["rows"], pt["sl"], pt["q"], pt["k"], pt["ab"]
            decayed = _colsum(pt["st_in"] * pt["dst_out"]) * ab[C - 1:C, :]
            de_s[s, 0:C, sl] = qh * pt["t1"] + jnp.where(is_last_row, decayed, 0.0)
            de_s[s, C:2 * C, sl] = kh * pt["t2"]
            dq = pt["t1"] + pt["dq_parts"][0]
            dk = pt["t2"] + pt["dk_parts"][0]
            for r, (t1, t2) in enumerate(zip(pt["dq_parts"][1:], pt["dk_parts"][1:])):
                dq = dq + t1
                dk = dk + t2
                de_s[s, C * (r + 2):C * (r + 3), sl] = qh * t1 + kh * t2
            dhi_ref[rows, sl] = pt["dv_intra"] + pt["dv_state"]
            dq_s[rows, sl] = dq
            dk_s[rows, sl] = dk
        dg = jnp.concatenate([_sum01_left(msum_t_ref[...], de_s[s]) for s in range(S)], axis=0)
        dk = dk_s[...]
        sq, lb = aux["sq"], aux["lb"]
        dhq_ref[...] = dq_s[...] * (sq * (1.0 + hq * (1.0 - sq)))
        common = dg / aux["f"] - dk
        dhf_ref[...] = (1.0 - lb) * aux["sig"] * aux["nsig"] * common
        dlb_s[...] += _colsum(aux["nsig"] * common)

        @pl.when(step == n_steps - 1)
        def _():
            dgam = lb * (1.0 - lb) * dlb_s[...]
            pack_ref[ROW_GAMMA:ROW_GAMMA + 1, :HG_WIDTH] = dgam
            pack_ref[ROW_GAMMA:ROW_GAMMA + 1, HG_WIDTH:] = -dgam

    last = n_steps - 1
    blk = lambda col: pl.BlockSpec((S * C, HG_WIDTH), lambda c, col=col: (last - c, col))
    return _pallas(
        body, comm=comm, edge=_grid_edge(n_steps), name="hgrn2_bwd", grid=(n_steps,),
        in_specs=[blk(0), blk(1), blk(2), blk(0), _full_spec((2, HG_WIDTH)), _full_spec(msum.shape),
                  _full_spec(msum_t.shape),
                  pl.BlockSpec((1, H, K, K), lambda c: (jnp.maximum(S * (last - c) - 1, 0), 0, 0, 0)),
                  pl.BlockSpec((S, H, K, K), lambda c: (last - c, 0, 0, 0))],
        out_specs=[blk(0), blk(0), blk(0), PACK_SPEC],
        out_shape=[jax.ShapeDtypeStruct((T, HG_WIDTH), F32)] * 3 + [jax.ShapeDtypeStruct((8, D_MODEL), F32)],
        scratch_shapes=[pltpu.VMEM((H, K, K), F32), pltpu.VMEM((1, HG_WIDTH), F32), pltpu.VMEM((S * C, HG_WIDTH), F32),
                        pltpu.VMEM((S * C, HG_WIDTH), F32), pltpu.VMEM((S, n_sums * C, HG_WIDTH), F32)],
        compiler_params=_params(("arbitrary",)),
    )(proj_h, proj_h, proj_h, do, gamma, msum, msum_t, states, states)


def _sb_sum_matrix(inclusive):
    B = SB_BLOCK
    j = np.arange(B)[:, None]
    s = np.arange(B)[None, :]
    tri = (j >= s) if inclusive else (j > s)
    once = np.concatenate([tri, np.ones((B, B), bool)], 1).astype(np.float32)
    return np.concatenate([once, once], 0)


def _sb_prefix_matrix(inclusive):
    B = SB_BLOCK
    j = np.arange(B)[:, None]
    s = np.arange(B)[None, :]
    tri = (j <= s) if inclusive else (j < s)
    once = np.concatenate([tri, np.ones((B, B), bool)], 1).astype(np.float32)
    return np.concatenate([once, once], 0)


def _sb_iotas():
    shape = (SB_BLOCK, SB_BLOCK)
    return lax.broadcasted_iota(jnp.int32, shape, 0), lax.broadcasted_iota(jnp.int32, shape, 1)


def _sb_heads(q, first):
    heads = []
    for g in range(SB_GROUP):
        qg = q[:, SB_BLOCK * g:SB_BLOCK * (g + 1)]
        zero = jnp.zeros_like(qg)
        heads += [(g, jnp.where(first, qg, zero)), (g, jnp.where(first, zero, qg))]
    return heads


def _lanes(x, g):
    return x[:, SB_BLOCK * g:SB_BLOCK * (g + 1)]


def sb_fwd(sqkv, comm=None):
    T = sqkv.shape[0]
    B = SB_BLOCK
    W = SB_GROUP * B
    groups = SB_WIDTH // W
    usum = jnp.asarray(_sb_sum_matrix(False), BF16)

    def body(q_ref, k_ref, v_ref, u_ref, o_ref, tl_ref, first_ref):
        p, i = pl.program_id(0), pl.program_id(1)
        row, lane = _sb_iotas()
        first = lane < SB_DH
        heads = _sb_heads(q_ref[...], first)
        u = u_ref[...]

        def more(loop):
            n, reachable, _ = loop
            return (SB_UNROLL * n <= i) & (reachable > 0)

        def step(loop):
            n, _, state = loop
            blocks = []
            for sub in range(SB_UNROLL):
                j = i - SB_UNROLL * n - sub
                off = pl.multiple_of(jnp.maximum(j, 0) * B, B)
                valid = ((lane + j * B) < (row + i * B)) & (j >= 0)
                blocks.append((k_ref[pl.ds(off, B), :], v_ref[pl.ds(off, B), :], valid))
            z = [[_dot_nt(qh, _lanes(kj, g)) for g, qh in heads] for kj, _, _ in blocks]
            lnb = [[jnp.where(valid, _neg_softplus(zz), 0.0) for zz in zs] for zs, (_, _, valid) in zip(z, blocks)]
            sums = [[_sum01_right(x, u) for x in xs] for xs in lnb]
            out = []
            for h, (carry, acc) in enumerate(state):
                for sub, (_, vj, valid) in enumerate(blocks):
                    expo = z[sub][h] + lnb[sub][h] + carry + sums[sub][h][:, :B]
                    acc = acc + _dot(jnp.where(valid, jnp.exp(expo), 0.0), _lanes(vj, heads[h][0]))
                    carry = carry + sums[sub][h][:, B:]
                out.append((carry, acc))
            state = tuple(out)
            worst = state[0][0]
            for carry, _ in state[1:]:
                worst = jnp.maximum(worst, carry)
            reachable = (jnp.max(worst) > SB_UNDERFLOW_LOG).astype(jnp.int32)
            return n + 1, reachable, state

        zero = jnp.zeros((B, B), F32)
        done, _, state = lax.while_loop(
            more, step, (jnp.int32(0), jnp.int32(1), tuple((zero, zero) for _ in heads)))
        for g in range(SB_GROUP):
            (tot0, acc0), (tot1, acc1) = state[2 * g], state[2 * g + 1]
            o_ref[:, B * g:B * (g + 1)] = jnp.where(first, acc0, acc1)
            tl_ref[:, B * g:B * (g + 1)] = jnp.where(first, tot0, tot1)
        first_ref[p, i] = jnp.maximum(i + 1 - SB_UNROLL * done, 0)

    def edge():
        p, i = pl.program_id(0), pl.program_id(1)
        return (p == 0) & (i == 0), (p == groups - 1) & (i == T // B - 1)

    return _pallas(
        body, comm=comm, edge=edge, name="sb_fwd", grid=(groups, T // B),
        in_specs=[pl.BlockSpec((B, W), lambda p, i: (i, p)),
                  pl.BlockSpec((T, W), lambda p, i: (0, groups + p)),
                  pl.BlockSpec((T, W), lambda p, i: (0, 2 * groups + p)),
                  pl.BlockSpec(usum.shape, lambda p, i: (0, 0))],
        out_specs=[pl.BlockSpec((B, W), lambda p, i: (i, p))] * 2 + [pl.BlockSpec(memory_space=pltpu.SMEM)],
        out_shape=[jax.ShapeDtypeStruct((T, SB_WIDTH), F32)] * 2 + [jax.ShapeDtypeStruct((groups, T // B), jnp.int32)],
        compiler_params=_params(("arbitrary", "arbitrary")),
    )(sqkv, sqkv, sqkv, usum)


def sb_bwd(sqkv, do, tl, first_block):
    T = sqkv.shape[0]
    B = SB_BLOCK
    W = SB_GROUP * B
    groups = SB_WIDTH // W
    upre = jnp.asarray(_sb_prefix_matrix(True), BF16)
    uexc = jnp.asarray(_sb_prefix_matrix(False), BF16)

    def body(q_ref, k_ref, v_ref, do_ref, tl_ref, up_ref, ue_ref, first_ref, dq_ref, dk_ref, dv_ref):
        p, i = pl.program_id(0), pl.program_id(1)

        @pl.when(i == 0)
        def _():
            dk_ref[...] = jnp.zeros(dk_ref.shape, F32)
            dv_ref[...] = jnp.zeros(dv_ref.shape, F32)

        row, lane = _sb_iotas()
        first = lane < SB_DH
        do = do_ref[...]
        tl_all = tl_ref[...]
        heads = []
        for (g, qh), at in zip(_sb_heads(q_ref[...], first), (0, B - 1) * SB_GROUP):
            dog = _lanes(do, g)
            keep = first if at == 0 else jnp.logical_not(first)
            heads.append((g, qh, jnp.where(keep, dog, jnp.zeros_like(dog)).astype(BF16),
                          _lanes(tl_all, g)[:, at:at + 1]))
        up = up_ref[...]
        ue = ue_ref[...]
        start = first_ref[p, i]

        def step(n, state):
            blocks = []
            for sub in range(SB_UNROLL):
                j = start + SB_UNROLL * n + sub
                off = pl.multiple_of(jnp.minimum(j, i) * B, B)
                valid = (lane + j * B) < (row + i * B)
                blocks.append((off, k_ref[pl.ds(off, B), :], v_ref[pl.ds(off, B), :], valid))
            combos = [(s, h) for s in range(SB_UNROLL) for h in range(len(heads))]
            z = {(s, h): _dot_nt(heads[h][1], _lanes(blocks[s][1], heads[h][0])) for s, h in combos}
            da = {(s, h): _dot_nt(heads[h][2], _lanes(blocks[s][2], heads[h][0])) for s, h in combos}
            lnb = {c: jnp.where(blocks[c[0]][3], _neg_softplus(z[c]), 0.0) for c in combos}
            lb = {c: z[c] + lnb[c] for c in combos}
            sums = {c: _sum01_right(lnb[c], up) for c in combos}
            a, w = {}, {}
            seen = [st[0] for st in state]
            for s, h in combos:
                expo = lb[s, h] + (heads[h][3] - seen[h] - sums[s, h][:, :B])
                a[s, h] = jnp.where(blocks[s][3], jnp.exp(expo), 0.0)
                w[s, h] = a[s, h] * da[s, h]
                seen[h] = seen[h] + sums[s, h][:, B:]
            wsums = {c: _sum01_right(w[c], ue) for c in combos}
            dz = {}
            seen_w = [st[1] for st in state]
            for s, h in combos:
                beta = jnp.exp(lb[s, h])
                before = seen_w[h] + wsums[s, h][:, :B]
                dz[s, h] = jnp.where(blocks[s][3], w[s, h] * (1.0 - beta) - before * beta, 0.0)
                seen_w[h] = seen_w[h] + wsums[s, h][:, B:]
            dq = [st[2] for st in state]
            for s, h in combos:
                dq[h] = dq[h] + _dot(dz[s, h], _lanes(blocks[s][1], heads[h][0]))
            for s in range(SB_UNROLL):
                off = blocks[s][0]
                for g in range(SB_GROUP):
                    h0, h1 = 2 * g, 2 * g + 1
                    dk_ref[pl.ds(off, B), B * g:B * (g + 1)] += (_dot_tn(dz[s, h0], heads[h0][1])
                                                                 + _dot_tn(dz[s, h1], heads[h1][1]))
                    dv_ref[pl.ds(off, B), B * g:B * (g + 1)] += (_dot_tn(a[s, h0], heads[h0][2])
                                                                 + _dot_tn(a[s, h1], heads[h1][2]))
            return tuple(zip(seen, seen_w, dq))

        zero = jnp.zeros((B, B), F32)
        trips = (i - start + SB_UNROLL) // SB_UNROLL
        state = lax.fori_loop(0, trips, step, tuple((zero, zero, zero) for _ in heads))
        for g in range(SB_GROUP):
            dq_ref[:, B * g:B * (g + 1)] = jnp.where(first, state[2 * g][2], state[2 * g + 1][2]) * SB_SCALE

    qblk = pl.BlockSpec((B, W), lambda p, i: (i, p))
    full = pl.BlockSpec((T, W), lambda p, i: (0, p))
    return pl.pallas_call(
        body, name="sb_bwd", grid=(groups, T // B),
        in_specs=[qblk, pl.BlockSpec((T, W), lambda p, i: (0, groups + p)),
                  pl.BlockSpec((T, W), lambda p, i: (0, 2 * groups + p)), qblk, qblk,
                  pl.BlockSpec(upre.shape, lambda p, i: (0, 0)), pl.BlockSpec(uexc.shape, lambda p, i: (0, 0)),
                  pl.BlockSpec(memory_space=pltpu.SMEM)],
        out_specs=[qblk, full, full],
        out_shape=[jax.ShapeDtypeStruct((T, SB_WIDTH), F32)] * 3,
        compiler_params=_params(("arbitrary", "arbitrary")),
    )(sqkv, sqkv, sqkv, do, tl, upre, uexc, first_block)


def _mixer_out(o_hg, hg, o_sb, g_hg, g_sb):
    n_hg, r_hg = _rms(o_hg)
    s_hg = _sigmoid(hg)
    n_sb, r_sb = _rms(o_sb)
    return dict(n_hg=n_hg, r_hg=r_hg, s_hg=s_hg, n_sb=n_sb, r_sb=r_sb,
                y_hg=n_hg * g_hg * (hg * s_hg), y_sb=n_sb * g_sb)


def mix_out_fwd(o_hg, proj_h, o_sb, x, norms, g_post, w_out, comm=None):
    T = x.shape[0]

    def body(ohg_ref, hg_ref, osb_ref, x_ref, nrm_ref, gp_ref, w_hbm, cat_ref, mix_ref, h1_ref, w_vmem):
        _load_once(w_hbm, w_vmem)
        nrm = nrm_ref[...]
        m = _mixer_out(ohg_ref[...], hg_ref[...], osb_ref[...], nrm[:, :HG_WIDTH], nrm[:, HG_WIDTH:])
        cat_ref[:, :HG_WIDTH] = m["y_hg"].astype(BF16)
        cat_ref[:, HG_WIDTH:] = m["y_sb"].astype(BF16)
        mix = jnp.dot(cat_ref[...], w_vmem[...], preferred_element_type=F32)
        mix_ref[...] = mix
        mh, _ = _rms(mix)
        h1_ref[...] = x_ref[...] + mh * gp_ref[...]

    return _pallas(
        body, comm=comm, edge=_grid_edge(T // _wide_tile(T)), name="mix_out_fwd", grid=(T // _wide_tile(T),),
        in_specs=[_wide_spec(HG_WIDTH), _wide_spec(HG_WIDTH, 3), _wide_spec(SB_WIDTH), _wide_spec(D_MODEL),
                  _full_spec((1, D_MODEL)), _full_spec((1, D_MODEL)), ANY_SPEC],
        out_specs=[_wide_spec(D_MODEL)] * 3,
        out_shape=[jax.ShapeDtypeStruct((T, D_MODEL), BF16), jax.ShapeDtypeStruct((T, D_MODEL), F32),
                   jax.ShapeDtypeStruct((T, D_MODEL), F32)],
        scratch_shapes=[pltpu.VMEM(w_out.shape, BF16)],
        compiler_params=_params(("arbitrary",)),
    )(o_hg, proj_h, o_sb, x, norms, g_post, w_out)


def ffn_fwd(h1, g_pre, g_post, w_gu, w_down):
    T = h1.shape[0]
    pw = w_gu.shape[2]

    def body(h1_ref, gpre_ref, gpost_ref, wgu_hbm, wd_hbm, u2_ref, gu_ref, act_ref, y_ref, h2_ref,
             wgu_vmem, wd_vmem, gu_s):
        _load_once(wgu_hbm, wgu_vmem)
        _load_once(wd_hbm, wd_vmem)
        h1v = h1_ref[...]
        hh, _ = _rms(h1v)
        u2 = (hh * gpre_ref[...]).astype(BF16)
        u2_ref[...] = u2
        for q in range(N_CHIPS):
            gu_s[:, pw * q:pw * (q + 1)] = jnp.dot(u2, wgu_vmem[q], preferred_element_type=F32)
        gu_ref[...] = gu_s[...].astype(BF16)
        gate = gu_s[:, :D_FF]
        act = (gate * _sigmoid(gate) * gu_s[:, D_FF:]).astype(BF16)
        act_ref[...] = act
        y = jnp.dot(act, wd_vmem[...], preferred_element_type=F32)
        y_ref[...] = y
        yh, _ = _rms(y)
        h2_ref[...] = h1v + yh * gpost_ref[...]

    return pl.pallas_call(
        body, name="ffn_fwd", grid=(T // ROW_TILE,),
        in_specs=[_row_spec(D_MODEL), _full_spec((1, D_MODEL)), _full_spec((1, D_MODEL)), ANY_SPEC, ANY_SPEC],
        out_specs=[_row_spec(D_MODEL), _row_spec(2 * D_FF), _row_spec(D_FF), _row_spec(D_MODEL), _row_spec(D_MODEL)],
        out_shape=[jax.ShapeDtypeStruct((T, D_MODEL), BF16), jax.ShapeDtypeStruct((T, 2 * D_FF), BF16),
                   jax.ShapeDtypeStruct((T, D_FF), BF16), jax.ShapeDtypeStruct((T, D_MODEL), F32),
                   jax.ShapeDtypeStruct((T, D_MODEL), F32)],
        scratch_shapes=[pltpu.VMEM(w_gu.shape, BF16), pltpu.VMEM(w_down.shape, BF16),
                        pltpu.VMEM((ROW_TILE, 2 * D_FF), F32)],
        compiler_params=_params(("arbitrary",)),
    )(h1, g_pre, g_post, w_gu, w_down)


def ple_loss(h2, p, target, w_ple, w_pg):
    T = h2.shape[0]
    pw = w_ple.shape[2]

    def body(h2_ref, p_ref, t_ref, wple_hbm, wpg_hbm, de_ref, ds_ref, dh2_ref, h2b_ref, pb_ref, pack_ref,
             wple_vmem, wpg_vmem, e_s):
        _load_once(wple_hbm, wple_vmem)
        _load_once(wpg_hbm, wpg_vmem)
        _zero_first(pack_ref)
        h2v = h2_ref[...]
        h2b = h2v.astype(BF16)
        h2b_ref[...] = h2b
        pb = p_ref[...].astype(BF16)
        pb_ref[...] = pb
        for q in range(N_CHIPS):
            e_s[:, pw * q:pw * (q + 1)] = jnp.dot(pb, wple_vmem[q], preferred_element_type=F32)
        e = e_s[...]
        sig = _sigmoid(jnp.dot(h2b, wpg_vmem[...], preferred_element_type=F32))
        err = h2v + e * sig - t_ref[...]
        part = 0.5 * jnp.sum(jnp.mean(err * err, axis=-1, keepdims=True), axis=0, keepdims=True)
        lane = lax.broadcasted_iota(jnp.int32, (1, D_MODEL), 1)
        pack_ref[ROW_LOSS:ROW_LOSS + 1, :] += jnp.where(lane == 0, part, 0.0)
        dh3 = err * (1.0 / D_MODEL)
        de_ref[...] = (dh3 * sig).astype(BF16)
        ds = (dh3 * e * sig * (1.0 - sig)).astype(BF16)
        ds_ref[...] = ds
        dh2_ref[...] = dh3 + _dot_nt(ds, wpg_vmem[...])

    return pl.pallas_call(
        body, name="ple_loss", grid=(T // _wide_tile(T),),
        in_specs=[_wide_spec(D_MODEL), _wide_spec(p.shape[1]), _wide_spec(D_MODEL), ANY_SPEC, ANY_SPEC],
        out_specs=[_wide_spec(D_MODEL), _wide_spec(D_MODEL), _wide_spec(D_MODEL), _wide_spec(D_MODEL),
                   _wide_spec(p.shape[1]), PACK_SPEC],
        out_shape=[jax.ShapeDtypeStruct((T, D_MODEL), BF16), jax.ShapeDtypeStruct((T, D_MODEL), BF16),
                   jax.ShapeDtypeStruct((T, D_MODEL), F32), jax.ShapeDtypeStruct((T, D_MODEL), BF16),
                   jax.ShapeDtypeStruct(p.shape, BF16), jax.ShapeDtypeStruct((8, D_MODEL), F32)],
        scratch_shapes=[pltpu.VMEM(w_ple.shape, BF16), pltpu.VMEM(w_pg.shape, BF16), pltpu.VMEM((_wide_tile(T), D_MODEL), F32)],
        compiler_params=_params(("arbitrary",)),
    )(h2, p, target, w_ple, w_pg)


def ffn_bwd(dh2, y, h1, gu, g_pre, g_post, w_gu, w_down):
    T = h1.shape[0]
    pw = w_gu.shape[2]

    def body(dh2_ref, y_ref, h1_ref, gu_ref, gpre_ref, gpost_ref, wgu_hbm, wd_hbm, dy_ref, dgu_ref, dh1_ref, pack_ref,
             wgu_vmem, wd_vmem):
        _load_once(wgu_hbm, wgu_vmem)
        _load_once(wd_hbm, wd_vmem)
        _zero_first(pack_ref)
        dh2v = dh2_ref[...]
        yh, ry = _rms(y_ref[...])
        dy, dw = _rms_bwd(dh2v, yh, ry, gpost_ref[...])
        pack_ref[ROW_FFN_POST:ROW_FFN_POST + 1, :] += _colsum(dw)
        dyb = dy.astype(BF16)
        dy_ref[...] = dyb
        dact = _dot_nt(dyb, wd_vmem[...])
        gate = gu_ref[:, :D_FF].astype(F32)
        up = gu_ref[:, D_FF:].astype(F32)
        sg = _sigmoid(gate)
        dgu_ref[:, :D_FF] = (dact * up * (sg * (1.0 + gate * (1.0 - sg)))).astype(BF16)
        dgu_ref[:, D_FF:] = (dact * gate * sg).astype(BF16)
        du2 = _dot_nt(dgu_ref[:, :pw], wgu_vmem[0])
        for q in range(1, N_CHIPS):
            du2 = du2 + _dot_nt(dgu_ref[:, pw * q:pw * (q + 1)], wgu_vmem[q])
        hh, rh = _rms(h1_ref[...])
        dh, dw = _rms_bwd(du2, hh, rh, gpre_ref[...])
        pack_ref[ROW_FFN_PRE:ROW_FFN_PRE + 1, :] += _colsum(dw)
        dh1_ref[...] = dh2v + dh

    return pl.pallas_call(
        body, name="ffn_bwd", grid=(T // ROW_TILE,),
        in_specs=[_row_spec(D_MODEL), _row_spec(D_MODEL), _row_spec(D_MODEL), _row_spec(2 * D_FF),
                  _full_spec((1, D_MODEL)), _full_spec((1, D_MODEL)), ANY_SPEC, ANY_SPEC],
        out_specs=[_row_spec(D_MODEL), _row_spec(2 * D_FF), _row_spec(D_MODEL), PACK_SPEC],
        out_shape=[jax.ShapeDtypeStruct((T, D_MODEL), BF16), jax.ShapeDtypeStruct((T, 2 * D_FF), BF16),
                   jax.ShapeDtypeStruct((T, D_MODEL), F32), jax.ShapeDtypeStruct((8, D_MODEL), F32)],
        scratch_shapes=[pltpu.VMEM(w_gu.shape, BF16), pltpu.VMEM(w_down.shape, BF16)],
        compiler_params=_params(("arbitrary",)),
    )(dh2, y, h1, gu, g_pre, g_post, w_gu, w_down)


def mix_out_bwd(dh1, mix, o_hg, proj_h, o_sb, norms, g_post, w_out, comm=None):
    T = dh1.shape[0]

    def body(dh1_ref, mix_ref, ohg_ref, hg_ref, osb_ref, nrm_ref, gp_ref, w_hbm, dmix_ref, dohg_ref, dhg_ref, dosb_ref,
             pack_ref, w_vmem):
        _load_once(w_hbm, w_vmem)
        _zero_first(pack_ref)
        mh, rm = _rms(mix_ref[...])
        dmix, dw = _rms_bwd(dh1_ref[...], mh, rm, gp_ref[...])
        pack_ref[ROW_ATTN_POST:ROW_ATTN_POST + 1, :] += _colsum(dw)
        dmb = dmix.astype(BF16)
        dmix_ref[...] = dmb
        dcat = _dot_nt(dmb, w_vmem[...])
        nrm = nrm_ref[...]
        g_hg, g_sb = nrm[:, :HG_WIDTH], nrm[:, HG_WIDTH:]
        hg = hg_ref[...]
        m = _mixer_out(ohg_ref[...], hg, osb_ref[...], g_hg, g_sb)
        d_hg = dcat[:, :HG_WIDTH]
        silu = hg * m["s_hg"]
        dhg_ref[...] = d_hg * (m["n_hg"] * g_hg) * (m["s_hg"] * (1.0 + hg * (1.0 - m["s_hg"])))
        dx, dw = _rms_bwd(d_hg * silu, m["n_hg"], m["r_hg"], g_hg)
        dohg_ref[...] = dx
        pack_ref[ROW_MIX_NORMS:ROW_MIX_NORMS + 1, :HG_WIDTH] += _colsum(dw)
        dx, dw = _rms_bwd(dcat[:, HG_WIDTH:], m["n_sb"], m["r_sb"], g_sb)
        dosb_ref[...] = dx
        pack_ref[ROW_MIX_NORMS:ROW_MIX_NORMS + 1, HG_WIDTH:] += _colsum(dw)

    return _pallas(
        body, comm=comm, edge=_grid_edge(T // _wide_tile(T)), name="mix_out_bwd", grid=(T // _wide_tile(T),),
        in_specs=[_wide_spec(D_MODEL), _wide_spec(D_MODEL), _wide_spec(HG_WIDTH), _wide_spec(HG_WIDTH, 3), _wide_spec(SB_WIDTH),
                  _full_spec((1, D_MODEL)), _full_spec((1, D_MODEL)), ANY_SPEC],
        out_specs=[_wide_spec(D_MODEL), _wide_spec(HG_WIDTH), _wide_spec(HG_WIDTH), _wide_spec(SB_WIDTH), PACK_SPEC],
        out_shape=[jax.ShapeDtypeStruct((T, D_MODEL), BF16), jax.ShapeDtypeStruct((T, HG_WIDTH), F32),
                   jax.ShapeDtypeStruct((T, HG_WIDTH), F32), jax.ShapeDtypeStruct((T, SB_WIDTH), F32),
                   jax.ShapeDtypeStruct((8, D_MODEL), F32)],
        scratch_shapes=[pltpu.VMEM(w_out.shape, BF16)],
        compiler_params=_params(("arbitrary",)),
    )(dh1, mix, o_hg, proj_h, o_sb, norms, g_post, w_out)


def in_proj_bwd(parts, x, dh1, g_pre, w_in, comm=None):
    T = x.shape[0]
    pw = w_in.shape[2]
    n_parts = len(parts)

    def body(*refs):
        part_refs = refs[:n_parts]
        x_ref, dh1_ref, g_ref, w_hbm, dproj_ref, dx_ref, pack_ref, w_vmem = refs[n_parts:]
        _load_once(w_hbm, w_vmem)
        _zero_first(pack_ref)
        for n, ref in enumerate(part_refs):
            dproj_ref[:, HG_WIDTH * n:HG_WIDTH * (n + 1)] = ref[...].astype(BF16)
        du = _dot_nt(dproj_ref[:, :pw], w_vmem[0])
        for q in range(1, N_CHIPS):
            du = du + _dot_nt(dproj_ref[:, pw * q:pw * (q + 1)], w_vmem[q])
        xh, r = _rms(x_ref[...])
        dx, dw = _rms_bwd(du, xh, r, g_ref[...])
        pack_ref[ROW_ATTN_PRE:ROW_ATTN_PRE + 1, :] += _colsum(dw)
        dx_ref[...] = dh1_ref[...] + dx

    return _pallas(
        body, comm=comm, edge=_grid_edge(T // _wide_tile(T)), name="in_proj_bwd", grid=(T // _wide_tile(T),),
        in_specs=[_wide_spec(HG_WIDTH)] * n_parts + [_wide_spec(D_MODEL), _wide_spec(D_MODEL), _full_spec((1, D_MODEL)), ANY_SPEC],
        out_specs=[_wide_spec(n_parts * HG_WIDTH), _wide_spec(D_MODEL), PACK_SPEC],
        out_shape=[jax.ShapeDtypeStruct((T, n_parts * HG_WIDTH), BF16), jax.ShapeDtypeStruct((T, D_MODEL), F32),
                   jax.ShapeDtypeStruct((8, D_MODEL), F32)],
        scratch_shapes=[pltpu.VMEM(w_in.shape, BF16)],
        compiler_params=_params(("arbitrary",)),
    )(*parts, x, dh1, g_pre, w_in)


def weight_grad(a, g, name, *, tm, tn, tk=512, col_pieces=False, comm=None):
    T, M = a.shape
    N = g.shape[1]
    tk = min(tk, T)
    steps = T // tk

    def body(a_ref, g_ref, o_ref):
        @pl.when(pl.program_id(2) == 0)
        def _():
            o_ref[...] = jnp.zeros(o_ref.shape, F32)

        o_ref[...] += _dot_tn(a_ref[...], g_ref[...]).reshape(o_ref.shape)

    if col_pieces:
        out_shape = jax.ShapeDtypeStruct((N // tn, M, tn), F32)
        out_spec = pl.BlockSpec((1, tm, tn), lambda i, j, k: (j, i, 0))
    else:
        out_shape = jax.ShapeDtypeStruct((M, N), F32)
        out_spec = pl.BlockSpec((tm, tn), lambda i, j, k: (i, j))
    grid = (M // tm, N // tn, steps)

    def edge():
        at = [pl.program_id(d) for d in range(3)]
        return ((at[0] == 0) & (at[1] == 0) & (at[2] == 0),
                (at[0] == grid[0] - 1) & (at[1] == grid[1] - 1) & (at[2] == grid[2] - 1))

    return _pallas(
        body, comm=comm, edge=edge, name=name, grid=grid,
        in_specs=[pl.BlockSpec((tk, tm), lambda i, j, k: (k, i)), pl.BlockSpec((tk, tn), lambda i, j, k: (k, j))],
        out_specs=[out_spec], out_shape=[out_shape],
        compiler_params=_params(("arbitrary", "arbitrary", "arbitrary")),
    )(a, g)


def _place():
    x, y, c = lax.axis_index("x"), lax.axis_index("y"), lax.axis_index("c")
    chips = [(1 - x, y), (x, 1 - y), (1 - x, 1 - y)]
    return x, y, c, chips


def _chip_index(cx, cy):
    return 2 * cx + cy


def _own_slot(piece, slots):
    me = _chip_index(lax.axis_index("x"), lax.axis_index("y"))
    landing = lax.empty((slots,) + piece.shape[1:], piece.dtype)
    return lax.dynamic_update_slice(landing, piece, (me,) + (0,) * (piece.ndim - 1))


def _rcopy(src, dst, send_sem, recv_sem, device):
    return pltpu.make_async_remote_copy(src_ref=src, dst_ref=dst, send_sem=send_sem, recv_sem=recv_sem,
                                        device_id=device, device_id_type=MESH)


def gather_weights(shards):
    n = len(shards)

    def body(*refs):
        ins, outs = refs[:n], refs[2 * n:3 * n]
        send_sems, recv_sems = refs[3 * n:]
        x, y, c, chips = _place()
        me = _chip_index(x, y)
        sibling = (x, y, 1 - c)

        def rows(w, core):
            half = ins[w].shape[0] // 2
            return pl.ds(core * half, half)

        sends = []
        for w in range(n):
            for j, chip in enumerate(chips):
                sends.append(_rcopy(ins[w].at[rows(w, c)], outs[w].at[me, rows(w, c)],
                                    send_sems.at[6 * w + j], recv_sems.at[6 * w + j], (*chip, c)))
        for cp in sends:
            cp.start()
        passed = []
        for w in range(n):
            for j, chip in enumerate(chips):
                block = outs[w].at[_chip_index(*chip), rows(w, c)]
                _rcopy(block, block, send_sems.at[6 * w + j], recv_sems.at[6 * w + j], (*chip, c)).wait_recv()
                cp = _rcopy(block, block, send_sems.at[6 * w + 3 + j], recv_sems.at[6 * w + 3 + j], sibling)
                cp.start()
                passed.append(cp)
        for w in range(n):
            for j, chip in enumerate(chips):
                block = outs[w].at[_chip_index(*chip), rows(w, 1 - c)]
                _rcopy(block, block, send_sems.at[6 * w + 3 + j], recv_sems.at[6 * w + 3 + j], sibling).wait_recv()
        for cp in sends + passed:
            cp.wait_send()

    filled = [_own_slot(s[None], N_CHIPS) for s in shards]
    return pl.pallas_call(
        body, name="gather_weights",
        in_specs=[ANY_SPEC] * (2 * n), out_specs=[ANY_SPEC] * n,
        out_shape=[jax.ShapeDtypeStruct(f.shape, f.dtype) for f in filled],
        input_output_aliases={n + w: w for w in range(n)},
        scratch_shapes=[pltpu.SemaphoreType.DMA((6 * n,)), pltpu.SemaphoreType.DMA((6 * n,))],
    )(*shards, *filled)


def _run_comm(comm, name):
    c_in, c_out = len(comm.inputs), len(comm.out_shape)

    def body(*refs):
        parts = refs[:c_in], refs[c_in:c_in + c_out], refs[c_in + c_out:]
        comm.start(*parts)
        comm.finish(*parts)

    return pl.pallas_call(
        body, name=name, in_specs=[ANY_SPEC] * c_in, out_specs=[ANY_SPEC] * c_out, out_shape=comm.out_shape,
        scratch_shapes=comm.scratch, input_output_aliases=comm.aliases)(*comm.inputs)


def _both(first, second):
    n_in, n_out, n_scr = len(first.inputs), len(first.out_shape), len(first.scratch)

    def split(ins, outs, scr):
        return (ins[:n_in], outs[:n_out], scr[:n_scr]), (ins[n_in:], outs[n_out:], scr[n_scr:])

    def start(*refs):
        a, b = split(*refs)
        first.start(*a)
        second.start(*b)

    def finish(*refs):
        a, b = split(*refs)
        first.finish(*a)
        second.finish(*b)

    aliases = dict(first.aliases)
    aliases.update({n_in + i: n_out + o for i, o in second.aliases.items()})
    return Comm(first.inputs + second.inputs, first.out_shape + second.out_shape, aliases,
                first.scratch + second.scratch, start, finish)


def _dma_sems(count):
    return [pltpu.SemaphoreType.DMA((count,)), pltpu.SemaphoreType.DMA((count,))]


def gather_over_ici(shards):
    n = len(shards)

    def copies(ins, outs, sems):
        send_sems, recv_sems = sems
        x, y, c, chips = _place()
        me = _chip_index(x, y)
        pairs = []
        for w in range(n):
            half = shards[w].shape[0] // 2
            rows = pl.ds(c * half, half)
            for j, chip in enumerate(chips):
                k = 3 * w + j
                landed = outs[w].at[_chip_index(*chip), rows]
                pairs.append((_rcopy(ins[w].at[rows], outs[w].at[me, rows], send_sems.at[k], recv_sems.at[k], (*chip, c)),
                              _rcopy(landed, landed, send_sems.at[k], recv_sems.at[k], (*chip, c))))
        return pairs

    def start(*refs):
        for send, _ in copies(*refs):
            send.start()

    def finish(*refs):
        pairs = copies(*refs)
        for _, landed in pairs:
            landed.wait_recv()
        for send, _ in pairs:
            send.wait_send()

    filled = [_own_slot(s[None], N_CHIPS) for s in shards]
    return Comm(list(shards) + filled, [jax.ShapeDtypeStruct(f.shape, f.dtype) for f in filled],
                {n + w: w for w in range(n)}, _dma_sems(3 * n), start, finish)


def gather_over_d2d(landed):
    n = len(landed)

    def copies(ins, outs, sems):
        send_sems, recv_sems = sems
        x, y, c, chips = _place()
        sibling = (x, y, 1 - c)
        pairs = []
        for w in range(n):
            half = landed[w].shape[1] // 2
            for j, chip in enumerate(chips):
                k = 3 * w + j
                mine = outs[w].at[_chip_index(*chip), pl.ds(c * half, half)]
                theirs = outs[w].at[_chip_index(*chip), pl.ds((1 - c) * half, half)]
                pairs.append((_rcopy(mine, mine, send_sems.at[k], recv_sems.at[k], sibling),
                              _rcopy(theirs, theirs, send_sems.at[k], recv_sems.at[k], sibling)))
        return pairs

    def start(*refs):
        for send, _ in copies(*refs):
            send.start()

    def finish(*refs):
        pairs = copies(*refs)
        for _, arrived in pairs:
            arrived.wait_recv()
        for send, _ in pairs:
            send.wait_send()

    return Comm(list(landed), [jax.ShapeDtypeStruct(a.shape, a.dtype) for a in landed], {w: w for w in range(n)},
                _dma_sems(3 * n), start, finish)


def core_halves(grads):
    n = len(grads)

    def copies(ins, outs, sems):
        send_sems, recv_sems = sems
        x, y, c, _ = _place()
        out = []
        for w in range(n):
            half = grads[w].shape[1] // 2
            out.append(_rcopy(ins[w].at[:, pl.ds((1 - c) * half, half), :], outs[w],
                              send_sems.at[w], recv_sems.at[w], (x, y, 1 - c)))
        return out

    def start(*refs):
        for cp in copies(*refs):
            cp.start()

    def finish(*refs):
        for cp in copies(*refs):
            cp.wait()

    return Comm(list(grads), [jax.ShapeDtypeStruct((g.shape[0], g.shape[1] // 2, g.shape[2]), g.dtype) for g in grads],
                {}, _dma_sems(n), start, finish)


def chip_partials(partials):
    n = len(partials)

    def copies(ins, outs, sems):
        send_sems, recv_sems = sems
        x, y, c, chips = _place()
        me = _chip_index(x, y)
        pairs = []
        for w in range(n):
            for j, chip in enumerate(chips):
                k = 3 * w + j
                landed = outs[w].at[_chip_index(*chip)]
                pairs.append((_rcopy(ins[w].at[_chip_index(*chip)], outs[w].at[me], send_sems.at[k], recv_sems.at[k],
                                     (*chip, c)),
                              _rcopy(landed, landed, send_sems.at[k], recv_sems.at[k], (*chip, c))))
        return pairs

    def start(*refs):
        for send, _ in copies(*refs):
            send.start()

    def finish(*refs):
        pairs = copies(*refs)
        for _, landed in pairs:
            landed.wait_recv()
        for send, _ in pairs:
            send.wait_send()

    me = _chip_index(lax.axis_index("x"), lax.axis_index("y"))
    filled = [_own_slot(lax.dynamic_index_in_dim(p, me, 0, keepdims=True), N_CHIPS) for p in partials]
    return Comm(list(partials) + filled, [jax.ShapeDtypeStruct(p.shape, p.dtype) for p in partials],
                {n + w: w for w in range(n)}, _dma_sems(3 * n), start, finish)


def join_core_halves(grads):
    n = len(grads)

    def body(*refs):
        outs = refs[n:2 * n]
        send_sems, recv_sems = refs[2 * n:]
        x, y, c, _ = _place()
        sibling = (x, y, 1 - c)
        copies = []
        for w in range(n):
            half = outs[w].shape[0] // 2
            mine = outs[w].at[pl.ds(c * half, half), :]
            copies.append(_rcopy(mine, mine, send_sems.at[w], recv_sems.at[w], sibling))
        for cp in copies:
            cp.start()
        for w in range(n):
            half = outs[w].shape[0] // 2
            theirs = outs[w].at[pl.ds((1 - c) * half, half), :]
            _rcopy(theirs, theirs, send_sems.at[w], recv_sems.at[w], sibling).wait_recv()
        for cp in copies:
            cp.wait_send()

    return pl.pallas_call(
        body, name="join_core_halves",
        in_specs=[ANY_SPEC] * n, out_specs=[ANY_SPEC] * n,
        out_shape=[jax.ShapeDtypeStruct(g.shape, g.dtype) for g in grads],
        input_output_aliases={w: w for w in range(n)},
        scratch_shapes=[pltpu.SemaphoreType.DMA((n,)), pltpu.SemaphoreType.DMA((n,))],
    )(*grads)


def _elementwise_rows(rows, cap=512):
    for t in range(min(rows, cap), 0, -8):
        if rows % t == 0 and t % 16 == 0:
            return t
    return rows


def add_core_halves(grad, got, core, name):
    _, rows, cols = got.shape
    tr = _elementwise_rows(rows)
    nt = rows // tr

    def body(core_ref, a_ref, b_ref, o_ref):
        o_ref[...] = (a_ref[...] + b_ref[...]).astype(BF16)

    spec = pl.BlockSpec((1, tr, cols), lambda q, i, core_ref: (q, i, 0))
    own = pl.BlockSpec((1, tr, cols), lambda q, i, core_ref: (q, core_ref[0] * nt + i, 0))
    return pl.pallas_call(
        body, name=name,
        grid_spec=pltpu.PrefetchScalarGridSpec(num_scalar_prefetch=1, grid=(N_CHIPS, nt), in_specs=[own, spec],
                                               out_specs=spec),
        out_shape=jax.ShapeDtypeStruct(got.shape, BF16),
        compiler_params=_params(("arbitrary", "arbitrary")),
    )(core, grad, got)


def add_chip_partials(parts, core, name):
    _, rows, cols = parts.shape
    tr = _elementwise_rows(rows)
    nt = rows // tr

    def body(core_ref, p_ref, o_ref):
        acc = p_ref[0].astype(F32)
        for q in range(1, N_CHIPS):
            acc = acc + p_ref[q].astype(F32)
        o_ref[...] = acc

    return pl.pallas_call(
        body, name=name,
        grid_spec=pltpu.PrefetchScalarGridSpec(
            num_scalar_prefetch=1, grid=(nt,),
            in_specs=[pl.BlockSpec((N_CHIPS, tr, cols), lambda i, core_ref: (0, i, 0))],
            out_specs=pl.BlockSpec((tr, cols), lambda i, core_ref: (core_ref[0] * nt + i, 0))),
        out_shape=jax.ShapeDtypeStruct((2 * rows, cols), F32),
        compiler_params=_params(("arbitrary",)),
    )(core, parts)


def _adamw_math(w, g, m, v):
    m = ADAM_B1 * m + (1.0 - ADAM_B1) * g
    v = ADAM_B2 * v + (1.0 - ADAM_B2) * (g * g)
    m_hat = m / (1.0 - ADAM_B1 ** ADAM_STEP)
    v_hat = v / (1.0 - ADAM_B2 ** ADAM_STEP)
    delta = -ADAM_LR * (m_hat / (jnp.sqrt(v_hat) + ADAM_EPS) + ADAM_WD * w)
    return delta, m, v


def adamw(w, g, m, v, name):
    rows, cols = w.shape
    tr = _elementwise_rows(rows, 256)

    def body(w_ref, g_ref, m_ref, v_ref, d_ref, nm_ref, nv_ref):
        d, nm, nv = _adamw_math(w_ref[...], g_ref[...], m_ref[...], v_ref[...])
        d_ref[...] = d
        nm_ref[...] = nm
        nv_ref[...] = nv

    spec = pl.BlockSpec((tr, cols), lambda i: (i, 0))
    return pl.pallas_call(
        body, name=name, grid=(rows // tr,), in_specs=[spec] * 4, out_specs=[spec] * 3,
        out_shape=[jax.ShapeDtypeStruct((rows, cols), F32)] * 3,
        compiler_params=_params(("arbitrary",)),
    )(w, g, m, v)


def reduce_small(packs, w, m, v):
    n = len(packs)
    n_dev = 8
    flips = [(fx, fy, fc) for fx in (0, 1) for fy in (0, 1) for fc in (0, 1)][1:]

    def body(*refs):
        pack_refs = refs[:n]
        w_ref, m_ref, v_ref, g_out, d_out, m_out, v_out, mine, slots, send_sems, recv_sems = refs[n:]
        x, y, c, _ = _place()
        me = 4 * x + 2 * y + c
        acc = pack_refs[0][...]
        for ref in pack_refs[1:]:
            acc = acc + ref[...]
        mine[...] = acc
        sends = []
        for k, (fx, fy, fc) in enumerate(flips):
            peer = (x ^ fx, y ^ fy, c ^ fc)
            sends.append(_rcopy(mine, slots.at[me], send_sems.at[k], recv_sems.at[me], peer))
        for cp in sends:
            cp.start()
        slots[me] = acc
        for fx, fy, fc in flips:
            src = 4 * (x ^ fx) + 2 * (y ^ fy) + (c ^ fc)
            _rcopy(mine, slots.at[src], send_sems.at[0], recv_sems.at[src], (x, y, c)).wait_recv()
        for cp in sends:
            cp.wait_send()
        total = slots[0]
        for d in range(1, n_dev):
            total = total + slots[d]
        g_out[...] = total
        d, nm, nv = _adamw_math(w_ref[...], total, m_ref[...], v_ref[...])
        d_out[...] = d
        m_out[...] = nm
        v_out[...] = nv

    vm = pl.BlockSpec(memory_space=pltpu.VMEM)
    return pl.pallas_call(
        body, name="reduce_small",
        in_specs=[vm] * (n + 3), out_specs=[vm] * 4,
        out_shape=[jax.ShapeDtypeStruct((8, D_MODEL), F32)] * 4,
        scratch_shapes=[pltpu.VMEM((8, D_MODEL), F32), pltpu.VMEM((n_dev, 8, D_MODEL), F32),
                        pltpu.SemaphoreType.DMA((len(flips),)), pltpu.SemaphoreType.DMA((n_dev,))],
    )(*packs, w, m, v)


def _column_pieces(g):
    return g.reshape(g.shape[0], N_CHIPS, g.shape[1] // N_CHIPS).transpose(1, 0, 2)


def _pack_small(attn_pre, gamma, hg_norm, sb_norm, attn_post, ffn_pre, ffn_post):
    rows = [attn_pre, gamma.reshape(1, D_MODEL), jnp.concatenate([hg_norm, sb_norm], axis=1), attn_post, ffn_pre, ffn_post,
            jnp.zeros((2, D_MODEL), F32)]
    return jnp.concatenate(rows, axis=0)


def _unpack_small(pack):
    return (pack[ROW_ATTN_PRE:ROW_ATTN_PRE + 1], pack[ROW_GAMMA].reshape(2, HG_WIDTH),
            pack[ROW_MIX_NORMS:ROW_MIX_NORMS + 1, :HG_WIDTH], pack[ROW_MIX_NORMS:ROW_MIX_NORMS + 1, HG_WIDTH:],
            pack[ROW_ATTN_POST:ROW_ATTN_POST + 1], pack[ROW_FFN_PRE:ROW_FFN_PRE + 1], pack[ROW_FFN_POST:ROW_FFN_POST + 1])


def kernel(x, p, attn_pre_norm, w_in, hg_lower_gamma, hg_out_norm, sb_out_norm, w_out, attn_post_norm, ffn_pre_norm, w_gate_up, w_down, ffn_post_norm, ple_proj, ple_gate, loss_target, m_attn_pre_norm, m_w_in, m_hg_lower_gamma, m_hg_out_norm, m_sb_out_norm, m_w_out, m_attn_post_norm, m_ffn_pre_norm, m_w_gate_up, m_w_down, m_ffn_post_norm, m_ple_proj, m_ple_gate, v_attn_pre_norm, v_w_in, v_hg_lower_gamma, v_hg_out_norm, v_sb_out_norm, v_w_out, v_attn_post_norm, v_ffn_pre_norm, v_w_gate_up, v_w_down, v_ffn_post_norm, v_ple_proj, v_ple_gate):
    x2 = x[0]
    p2 = p[0, 0]
    target = loss_target[0]
    big = dict(w_in=(w_in, m_w_in, v_w_in), w_out=(w_out, m_w_out, v_w_out), w_gate_up=(w_gate_up, m_w_gate_up, v_w_gate_up),
               w_down=(w_down, m_w_down, v_w_down), ple_proj=(ple_proj, m_ple_proj, v_ple_proj),
               ple_gate=(ple_gate, m_ple_gate, v_ple_gate))
    names = list(big)
    big = {k: tuple(a[0] for a in t) for k, t in big.items()}

    shard16 = {k: big[k][0].astype(BF16) for k in names}
    w_in_full, = gather_weights([shard16["w_in"]])
    mix_norms = jnp.concatenate([hg_out_norm, sb_out_norm], axis=1)
    small_ones = ["w_out", "ple_proj", "ple_gate"]

    proj_h, sqkv, u1, *landed_small = in_proj_fwd(
        x2, attn_pre_norm, w_in_full, comm=gather_over_ici([shard16[k] for k in small_ones]))
    o_sb, sb_totals, sb_first, landed_gu = sb_fwd(sqkv, comm=gather_over_ici([shard16["w_gate_up"]]))
    o_hg, states, landed_down, *full_small = hgrn2_fwd(
        proj_h, hg_lower_gamma, comm=_both(gather_over_ici([shard16["w_down"]]), gather_over_d2d(landed_small)))
    full = dict(zip(small_ones, full_small), w_in=w_in_full)
    w_out_full = full["w_out"].reshape(D_MODEL, D_MODEL)
    w_pg_full = full["ple_gate"].reshape(D_MODEL, D_MODEL)
    cat, mix, h1, full["w_gate_up"], full["w_down"] = mix_out_fwd(
        o_hg, proj_h, o_sb, x2, mix_norms, attn_post_norm, w_out_full, comm=gather_over_d2d([landed_gu, landed_down]))
    w_down_full = full["w_down"].reshape(D_FF, D_MODEL)
    u2, gu, act, y, h2 = ffn_fwd(h1, ffn_pre_norm, ffn_post_norm, full["w_gate_up"], w_down_full)

    core = lax.axis_index("c").astype(jnp.int32).reshape(1)
    de, ds, dh2, h2b, pb, pack_loss = ple_loss(h2, p2, target, full["ple_proj"], w_pg_full)
    dy, dgu, dh1, pack_ffn = ffn_bwd(dh2, y, h1, gu, ffn_pre_norm, ffn_post_norm, full["w_gate_up"], w_down_full)
    local = {}
    local["w_gate_up"], = weight_grad(u2, dgu, "grad_w_gate_up", tm=D_MODEL, tn=full["w_gate_up"].shape[2], tk=1024,
                                      col_pieces=True)
    grad_down, got_gu = weight_grad(act, dy, "grad_w_down", tm=D_FF // 2, tn=D_MODEL, tk=1024,
                                    comm=core_halves([local["w_gate_up"]]))
    local["w_down"] = grad_down.reshape(full["w_down"].shape)
    local["ple_proj"] = _column_pieces(weight_grad(pb, de, "grad_ple_proj", tm=pb.shape[1], tn=D_MODEL, tk=1024)[0])
    grad_pg, got_down = weight_grad(h2b, ds, "grad_ple_gate", tm=D_MODEL, tn=D_MODEL, tk=1024,
                                    comm=core_halves([local["w_down"]]))
    local["ple_gate"] = grad_pg.reshape(full["ple_gate"].shape)
    early = list(local)
    dmix, do_hg, dhg, do_sb, pack_mix, *got_ple = mix_out_bwd(
        dh1, mix, o_hg, proj_h, o_sb, mix_norms, attn_post_norm, w_out_full,
        comm=core_halves([local["ple_proj"], local["ple_gate"]]))
    got = [got_gu, got_down] + got_ple
    partial = [add_core_halves(local[k], g, core, "add_core_halves_" + k) for k, g in zip(early, got)]
    local["w_out"] = weight_grad(cat, dmix, "grad_w_out", tm=D_MODEL, tn=D_MODEL, tk=1024)[0].reshape(full["w_out"].shape)
    dsq, dsk, dsv = sb_bwd(sqkv, do_sb, sb_totals, sb_first)
    dhq, dhf, dhi, pack_hg, *by_source, got_out = hgrn2_bwd(
        proj_h, hg_lower_gamma, states, do_hg, comm=_both(chip_partials(partial[:2]), core_halves([local["w_out"]])))
    early.append("w_out")
    partial.append(add_core_halves(local["w_out"], got_out, core, "add_core_halves_w_out"))
    dproj, grad_x, pack_in = in_proj_bwd([dhq, dhf, dhi, dhg, dsq, dsk, dsv], x2, dh1, attn_pre_norm, full["w_in"])

    late = ["w_in"]
    local["w_in"], *more = weight_grad(u1, dproj, "grad_w_in", tm=D_MODEL, tn=full["w_in"].shape[2], tk=1024,
                                       col_pieces=True, comm=chip_partials(partial[2:]))
    halves = {k: add_chip_partials(s, core, "add_chip_partials_" + k) for k, s in zip(early, by_source + more)}
    got = _run_comm(core_halves([local[k] for k in late]), "exchange_core_halves")
    partial = [add_core_halves(local[k], g, core, "add_core_halves_" + k) for k, g in zip(late, got)]
    by_source = _run_comm(chip_partials(partial), "exchange_chip_partials")
    halves.update({k: add_chip_partials(s, core, "add_chip_partials_" + k) for k, s in zip(late, by_source)})
    grads = dict(zip(names, join_core_halves([halves[k] for k in names])))

    upd = {k: adamw(big[k][0], grads[k], big[k][1], big[k][2], "adamw_" + k) for k in names}

    small = reduce_small(
        [pack_loss, pack_ffn, pack_mix, pack_hg, pack_in],
        _pack_small(attn_pre_norm, hg_lower_gamma, hg_out_norm, sb_out_norm, attn_post_norm, ffn_pre_norm, ffn_post_norm),
        _pack_small(m_attn_pre_norm, m_hg_lower_gamma, m_hg_out_norm, m_sb_out_norm, m_attn_post_norm, m_ffn_pre_norm, m_ffn_post_norm),
        _pack_small(v_attn_pre_norm, v_hg_lower_gamma, v_hg_out_norm, v_sb_out_norm, v_attn_post_norm, v_ffn_pre_norm, v_ffn_post_norm),
    )
    loss = small[0][ROW_LOSS, 0]
    s_grad, s_delta, s_m, s_v = (_unpack_small(t) for t in small)

    def ordered(small_vals, big_vals):
        a_pre, gam, hg_n, sb_n, a_post, f_pre, f_post = small_vals
        b = {k: big_vals[k][None] for k in names}
        return (a_pre, b["w_in"], gam, hg_n, sb_n, b["w_out"], a_post, f_pre, b["w_gate_up"], b["w_down"], f_post,
                b["ple_proj"], b["ple_gate"])

    return (loss, grad_x[None],
            *ordered(s_grad, grads),
            *ordered(s_delta, {k: upd[k][0] for k in names}),
            *ordered(s_m, {k: upd[k][1] for k in names}),
            *ordered(s_v, {k: upd[k][2] for k in names}))
```

```python
from typing import Callable, NamedTuple

import numpy as np
import jax
import jax.numpy as jnp
from jax import lax
from jax.experimental import pallas as pl
from jax.experimental.pallas import tpu as pltpu

F32 = jnp.float32
BF16 = jnp.bfloat16
MESH = pl.DeviceIdType.MESH

RMS_EPS = 1e-6
D_MODEL = 1024
HG_WIDTH = 512
HG_HEADS = 4
HG_DK = 128
HG_CHUNK = 64
HG_LEVELS = (32, 16, 8, 4, 2, 1)
HG_CHUNKS_PER_STEP = 2
SB_WIDTH = 512
SB_BLOCK = 128
SB_DH = 64
SB_SCALE = SB_DH ** -0.5
SB_UNDERFLOW_LOG = -87.5
SB_UNROLL = 2
SB_GROUP = 2
D_FF = 2816
N_CHIPS = 4
ROW_TILE = 256
WIDE_ROW_TILE = 512
V7X_VMEM_LIMIT = 56 * 1024 * 1024

ADAM_LR = 0.001
ADAM_B1 = 0.9
ADAM_B2 = 0.999
ADAM_EPS = 1e-08
ADAM_WD = 0.01
ADAM_STEP = 10

ROW_ATTN_PRE, ROW_GAMMA, ROW_MIX_NORMS, ROW_ATTN_POST, ROW_FFN_PRE, ROW_FFN_POST, ROW_LOSS = range(7)


def _params(sem=None, vmem=V7X_VMEM_LIMIT):
    return pltpu.CompilerParams(dimension_semantics=sem, vmem_limit_bytes=vmem)


def _dot(a, b):
    return jnp.dot(a.astype(BF16), b.astype(BF16), preferred_element_type=F32)


def _dot_nt(a, b):
    return lax.dot_general(a.astype(BF16), b.astype(BF16), (((1,), (1,)), ((), ())), preferred_element_type=F32)


def _dot_tn(a, b):
    return lax.dot_general(a.astype(BF16), b.astype(BF16), (((0,), (0,)), ((), ())), preferred_element_type=F32)


def _split(x):
    hi = x.astype(BF16)
    lo = (x - hi.astype(F32)).astype(BF16)
    return hi, lo


def _sum01_left(m01, x):
    hi, lo = _split(x)
    return jnp.dot(m01, hi, preferred_element_type=F32) + jnp.dot(m01, lo, preferred_element_type=F32)


def _sum01_right(x, m01_twice):
    hi, lo = _split(x)
    return jnp.dot(jnp.concatenate([hi, lo], axis=1), m01_twice, preferred_element_type=F32)


def _rms(x):
    r = lax.rsqrt(jnp.mean(x * x, axis=-1, keepdims=True) + RMS_EPS)
    return x * r, r


def _rms_bwd(dy, xhat, r, w):
    dxh = dy * w
    dx = r * (dxh - xhat * jnp.mean(dxh * xhat, axis=-1, keepdims=True))
    return dx, dy * xhat


def _sigmoid(x):
    return 1.0 / (1.0 + jnp.exp(-x))


def _neg_softplus(z):
    return -(jnp.maximum(z, 0.0) + jnp.log(1.0 + jnp.exp(-jnp.abs(z))))


def _colsum(x):
    return jnp.sum(x, axis=0, keepdims=True)


class _Resident:
    def __init__(self, hbm, vmem, sems):
        pieces = sems.shape[0]
        if pieces == 1:
            self.copies = [pltpu.make_async_copy(hbm, vmem, sems.at[0])]
        else:
            self.copies = [pltpu.make_async_copy(hbm.at[q], vmem.at[q], sems.at[q]) for q in range(pieces)]

        @pl.when(pl.program_id(0) == 0)
        def _():
            for cp in self.copies:
                cp.start()

    def ready(self, piece=0):
        @pl.when(pl.program_id(0) == 0)
        def _():
            self.copies[piece].wait()


def _zero_first(ref):
    @pl.when(pl.program_id(0) == 0)
    def _():
        ref[...] = jnp.zeros(ref.shape, ref.dtype)


def _row_spec(width, col=0, rows=ROW_TILE):
    return pl.BlockSpec((rows, width), lambda i, col=col: (i, col))


def _wide_spec(width, col=0):
    return _row_spec(width, col, WIDE_ROW_TILE)


def _wide_tile(T):
    assert T % WIDE_ROW_TILE == 0
    return WIDE_ROW_TILE


def _full_spec(shape):
    return pl.BlockSpec(shape, lambda *_: (0,) * len(shape))


ANY_SPEC = pl.BlockSpec(memory_space=pl.ANY)
PACK_SPEC = _full_spec((8, D_MODEL))


class Comm(NamedTuple):
    inputs: list
    out_shape: list
    aliases: dict
    scratch: list
    start: Callable
    finish: Callable


def _pallas(body, *, comm=None, edge=None, in_specs, out_specs, out_shape, scratch_shapes=(), **kw):
    if comm is None:
        return pl.pallas_call(body, in_specs=in_specs, out_specs=out_specs, out_shape=out_shape,
                              scratch_shapes=scratch_shapes, **kw)
    n_in, n_out, n_scr = len(in_specs), len(out_specs), len(scratch_shapes)
    c_in, c_out = len(comm.inputs), len(comm.out_shape)

    def both(*refs):
        ins, c_ins = refs[:n_in], refs[n_in:n_in + c_in]
        outs = refs[n_in + c_in:n_in + c_in + n_out]
        c_outs = refs[n_in + c_in + n_out:n_in + c_in + n_out + c_out]
        rest = refs[n_in + c_in + n_out + c_out:]
        scr, c_scr = rest[:n_scr], rest[n_scr:]
        first, last = edge()

        @pl.when(first)
        def _():
            comm.start(c_ins, c_outs, c_scr)

        body(*ins, *outs, *scr)

        @pl.when(last)
        def _():
            comm.finish(c_ins, c_outs, c_scr)

    call = pl.pallas_call(
        both, in_specs=list(in_specs) + [ANY_SPEC] * c_in, out_specs=list(out_specs) + [ANY_SPEC] * c_out,
        out_shape=list(out_shape) + list(comm.out_shape), scratch_shapes=list(scratch_shapes) + list(comm.scratch),
        input_output_aliases={n_in + a: n_out + b for a, b in comm.aliases.items()}, **kw)
    return lambda *args: call(*args, *comm.inputs)


def _grid_edge(steps):
    return lambda: (pl.program_id(0) == 0, pl.program_id(0) == steps - 1)


def in_proj_fwd(x, g_pre, w_in, comm=None):
    T = x.shape[0]
    pw = w_in.shape[2]

    def body(x_ref, g_ref, w_hbm, ph_ref, sqkv_ref, u_ref, w_vmem, proj_s, w_sems):
        w = _Resident(w_hbm, w_vmem, w_sems)
        xh, _ = _rms(x_ref[...])
        u = (xh * g_ref[...]).astype(BF16)
        u_ref[...] = u
        for q in range(N_CHIPS):
            w.ready(q)
            proj_s[:, pw * q:pw * (q + 1)] = jnp.dot(u, w_vmem[q], preferred_element_type=F32)
        ph_ref[...] = proj_s[:, :4 * HG_WIDTH]
        sqkv_ref[:, :SB_WIDTH] = (proj_s[:, 4 * HG_WIDTH:4 * HG_WIDTH + SB_WIDTH] * SB_SCALE).astype(BF16)
        sqkv_ref[:, SB_WIDTH:] = proj_s[:, 4 * HG_WIDTH + SB_WIDTH:].astype(BF16)

    return _pallas(
        body, comm=comm, edge=_grid_edge(T // _wide_tile(T)), name="in_proj_fwd", grid=(T // _wide_tile(T),),
        in_specs=[_wide_spec(D_MODEL), _full_spec((1, D_MODEL)), ANY_SPEC],
        out_specs=[_wide_spec(4 * HG_WIDTH), _wide_spec(3 * SB_WIDTH), _wide_spec(D_MODEL)],
        out_shape=[jax.ShapeDtypeStruct((T, 4 * HG_WIDTH), F32), jax.ShapeDtypeStruct((T, 3 * SB_WIDTH), BF16),
                   jax.ShapeDtypeStruct((T, D_MODEL), BF16)],
        scratch_shapes=[pltpu.VMEM(w_in.shape, BF16), pltpu.VMEM((_wide_tile(T), N_CHIPS * pw), F32), pltpu.SemaphoreType.DMA((N_CHIPS,))],
        compiler_params=_params(("arbitrary",)),
    )(x, g_pre, w_in)


def _hg_sum_matrix():
    C = HG_CHUNK
    t = np.arange(C)[:, None]
    j = np.arange(C)[None, :]
    mats = [j <= t, j > t]
    for h in HG_LEVELS:
        start = (t // (2 * h)) * (2 * h)
        upper = (t & h) != 0
        mats.append(np.where(upper, (j >= start + h) & (j <= t), (j > t) & (j <= start + h - 1)))
    return np.concatenate(mats, 0).astype(np.float32)


def _hg_level_masks():
    C = HG_CHUNK
    t = lax.broadcasted_iota(jnp.int32, (C, C), 0)
    s = lax.broadcasted_iota(jnp.int32, (C, C), 1)
    x = t ^ s
    masks = [t == s]
    for h in HG_LEVELS:
        masks.append((x >= h) & (x < 2 * h) & (t > s))
    return masks


def _hg_gates(hq, hf, gamma):
    lb = 1.0 / (1.0 + jnp.exp(gamma[1:2, :] - gamma[0:1, :]))
    sq = _sigmoid(hq)
    q = hq * sq
    sig = _sigmoid(hf)
    nsig = _sigmoid(-hf)
    f = lb + (1.0 - lb) * sig
    k = (1.0 - lb) * nsig
    g = jnp.log(f)
    return q, k, g, dict(lb=lb, sq=sq, sig=sig, nsig=nsig, f=f)


def _hg_head_decays(A, h):
    C, K = HG_CHUNK, HG_DK
    sl = slice(K * h, K * (h + 1))
    blocks = [A[C * r:C * (r + 1), sl] for r in range(2 + len(HG_LEVELS))]
    return blocks[0], blocks[1], [None] + blocks[2:]


def _hg_products(q, k, levels):
    return [_dot_nt(q, k)] + [_dot_nt(q * a, k * a) for a in levels[1:]]


def _hg_select(prods, masks):
    sc = jnp.where(masks[0], prods[0], 0.0)
    for p, m in zip(prods[1:], masks[1:]):
        sc = jnp.where(m, p, sc)
    return sc


def hgrn2_fwd(proj_h, gamma, comm=None):
    T = proj_h.shape[0]
    C, K, H, S = HG_CHUNK, HG_DK, HG_HEADS, HG_CHUNKS_PER_STEP
    n_steps = T // (S * C)
    msum = jnp.asarray(_hg_sum_matrix(), BF16)

    def body(hq_ref, hf_ref, hi_ref, gam_ref, msum_ref, o_ref, st_ref, st_s):
        _zero_first(st_s)
        q, k, g, _ = _hg_gates(hq_ref[...], hf_ref[...], gam_ref[...])
        v = hi_ref[...]
        masks = _hg_level_masks()
        parts = []
        for s in range(S):
            rows = slice(C * s, C * (s + 1))
            A = jnp.exp(_sum01_left(msum_ref[...], g[rows]))
            for h in range(H):
                sl = slice(K * h, K * (h + 1))
                ab, ar, levels = _hg_head_decays(A, h)
                parts.append(dict(s=s, h=h, rows=rows, sl=sl, ab=ab, ar=ar, levels=levels,
                                  q=q[rows, sl], k=k[rows, sl], v=v[rows, sl]))
        for pt in parts:
            pt["prods"] = _hg_products(pt["q"], pt["k"], pt["levels"])
            pt["grown"] = _dot_tn(pt["v"], pt["k"] * pt["ar"])
        for pt in parts:
            pt["sc"] = _hg_select(pt["prods"], masks)
        state = [st_s[h] for h in range(H)]
        for pt in parts:
            h, ab = pt["h"], pt["ab"]
            o_ref[pt["rows"], pt["sl"]] = _dot_nt(pt["q"] * ab, state[h]) + _dot(pt["sc"], pt["v"])
            state[h] = state[h] * ab[C - 1:C, :] + pt["grown"]
            st_ref[pt["s"], h] = state[h]
        for h in range(H):
            st_s[h] = state[h]

    blk = lambda col: pl.BlockSpec((S * C, HG_WIDTH), lambda c, col=col: (c, col))
    return _pallas(
        body, comm=comm, edge=_grid_edge(n_steps), name="hgrn2_fwd", grid=(n_steps,),
        in_specs=[blk(0), blk(1), blk(2), _full_spec((2, HG_WIDTH)), _full_spec(msum.shape)],
        out_specs=[blk(0), pl.BlockSpec((S, H, K, K), lambda c: (c, 0, 0, 0))],
        out_shape=[jax.ShapeDtypeStruct((T, HG_WIDTH), F32), jax.ShapeDtypeStruct((S * n_steps, H, K, K), F32)],
        scratch_shapes=[pltpu.VMEM((H, K, K), F32)],
        compiler_params=_params(("arbitrary",)),
    )(proj_h, proj_h, proj_h, gamma, msum)


def hgrn2_bwd(proj_h, gamma, states, do, comm=None):
    T = proj_h.shape[0]
    C, K, H, S = HG_CHUNK, HG_DK, HG_HEADS, HG_CHUNKS_PER_STEP
    n_steps = T // (S * C)
    n_sums = 2 + len(HG_LEVELS)
    msum = jnp.asarray(_hg_sum_matrix(), BF16)
    msum_t = jnp.asarray(_hg_sum_matrix().T, BF16)

    def body(hq_ref, hf_ref, hi_ref, do_ref, gam_ref, msum_ref, msum_t_ref, st_prev_ref, st_ref,
             dhq_ref, dhf_ref, dhi_ref, pack_ref, dst_s, dlb_s, dq_s, dk_s, de_s):
        step = pl.program_id(0)
        _zero_first(dst_s)
        _zero_first(dlb_s)
        _zero_first(pack_ref)
        hq = hq_ref[...]
        q, k, g, aux = _hg_gates(hq, hf_ref[...], gam_ref[...])
        v = hi_ref[...]
        do_all = do_ref[...]
        masks = _hg_level_masks()
        is_last_row = lax.broadcasted_iota(jnp.int32, (C, K), 0) == C - 1
        has_prev = (step < n_steps - 1).astype(F32)
        parts = []
        for s in reversed(range(S)):
            rows = slice(C * s, C * (s + 1))
            A = jnp.exp(_sum01_left(msum_ref[...], g[rows]))
            for h in range(H):
                sl = slice(K * h, K * (h + 1))
                ab, ar, levels = _hg_head_decays(A, h)
                st_in = st_prev_ref[0, h] * has_prev if s == 0 else st_ref[s - 1, h]
                parts.append(dict(s=s, h=h, rows=rows, sl=sl, ab=ab, ar=ar, levels=levels, st_in=st_in,
                                  q=q[rows, sl], k=k[rows, sl], v=v[rows, sl], do=do_all[rows, sl]))
        for pt in parts:
            pt["prods"] = _hg_products(pt["q"], pt["k"], pt["levels"])
            pt["da"] = _dot_nt(pt["do"], pt["v"])
            pt["t1"] = pt["ab"] * _dot(pt["do"], pt["st_in"])
            pt["dst_add"] = _dot_tn(pt["do"], pt["q"] * pt["ab"])
        dstate = [dst_s[h] for h in range(H)]
        for pt in parts:
            h = pt["h"]
            pt["dst_out"] = dstate[h]
            pt["t2"] = pt["ar"] * _dot(pt["v"], dstate[h])
            pt["dv_state"] = _dot_nt(pt["k"] * pt["ar"], dstate[h])
            dstate[h] = dstate[h] * pt["ab"][C - 1:C, :] + pt["dst_add"]
        for h in range(H):
            dst_s[h] = dstate[h]
        for pt in parts:
            pt["sc"] = _hg_select(pt["prods"], masks)
            pt["dam"] = [jnp.where(m, pt["da"], 0.0) for m in masks]
        for pt in parts:
            qh, kh = pt["q"], pt["k"]
            pt["dq_parts"] = [_dot(pt["dam"][0], kh)] + [
                a * _dot(dam, kh * a) for a, dam in zip(pt["levels"][1:], pt["dam"][1:])]
            pt["dk_parts"] = [_dot_tn(pt["dam"][0], qh)] + [
                a * _dot_tn(dam, qh * a) for a, dam in zip(pt["levels"][1:], pt["dam"][1:])]
            pt["dv_intra"] = _dot_tn(pt["sc"], pt["do"])
        for pt in parts:
            s, rows, sl, qh, kh, ab = pt["s"], pt["rows"], pt["sl"], pt["q"], pt["k"], pt["ab"]
            decayed = _colsum(pt["st_in"] * pt["dst_out"]) * ab[C - 1:C, :]
            de_s[s, 0:C, sl] = qh * pt["t1"] + jnp.where(is_last_row, decayed, 0.0)
            de_s[s, C:2 * C, sl] = kh * pt["t2"]
            dq = pt["t1"] + pt["dq_parts"][0]
            dk = pt["t2"] + pt["dk_parts"][0]
            for r, (t1, t2) in enumerate(zip(pt["dq_parts"][1:], pt["dk_parts"][1:])):
                dq = dq + t1
                dk = dk + t2
                de_s[s, C * (r + 2):C * (r + 3), sl] = qh * t1 + kh * t2
            dhi_ref[rows, sl] = pt["dv_intra"] + pt["dv_state"]
            dq_s[rows, sl] = dq
            dk_s[rows, sl] = dk
        dg = jnp.concatenate([_sum01_left(msum_t_ref[...], de_s[s]) for s in range(S)], axis=0)
        dk = dk_s[...]
        sq, lb = aux["sq"], aux["lb"]
        dhq_ref[...] = dq_s[...] * (sq * (1.0 + hq * (1.0 - sq)))
        common = dg / aux["f"] - dk
        dhf_ref[...] = (1.0 - lb) * aux["sig"] * aux["nsig"] * common
        dlb_s[...] += _colsum(aux["nsig"] * common)

        @pl.when(step == n_steps - 1)
        def _():
            dgam = lb * (1.0 - lb) * dlb_s[...]
            pack_ref[ROW_GAMMA:ROW_GAMMA + 1, :HG_WIDTH] = dgam
            pack_ref[ROW_GAMMA:ROW_GAMMA + 1, HG_WIDTH:] = -dgam

    last = n_steps - 1
    blk = lambda col: pl.BlockSpec((S * C, HG_WIDTH), lambda c, col=col: (last - c, col))
    return _pallas(
        body, comm=comm, edge=_grid_edge(n_steps), name="hgrn2_bwd", grid=(n_steps,),
        in_specs=[blk(0), blk(1), blk(2), blk(0), _full_spec((2, HG_WIDTH)), _full_spec(msum.shape),
                  _full_spec(msum_t.shape),
                  pl.BlockSpec((1, H, K, K), lambda c: (jnp.maximum(S * (last - c) - 1, 0), 0, 0, 0)),
                  pl.BlockSpec((S, H, K, K), lambda c: (last - c, 0, 0, 0))],
        out_specs=[blk(0), blk(0), blk(0), PACK_SPEC],
        out_shape=[jax.ShapeDtypeStruct((T, HG_WIDTH), F32)] * 3 + [jax.ShapeDtypeStruct((8, D_MODEL), F32)],
        scratch_shapes=[pltpu.VMEM((H, K, K), F32), pltpu.VMEM((1, HG_WIDTH), F32), pltpu.VMEM((S * C, HG_WIDTH), F32),
                        pltpu.VMEM((S * C, HG_WIDTH), F32), pltpu.VMEM((S, n_sums * C, HG_WIDTH), F32)],
        compiler_params=_params(("arbitrary",)),
    )(proj_h, proj_h, proj_h, do, gamma, msum, msum_t, states, states)


def _sb_sum_matrix(inclusive):
    B = SB_BLOCK
    j = np.arange(B)[:, None]
    s = np.arange(B)[None, :]
    tri = (j >= s) if inclusive else (j > s)
    once = np.concatenate([tri, np.ones((B, B), bool)], 1).astype(np.float32)
    return np.concatenate([once, once], 0)


def _sb_prefix_matrix(inclusive):
    B = SB_BLOCK
    j = np.arange(B)[:, None]
    s = np.arange(B)[None, :]
    tri = (j <= s) if inclusive else (j < s)
    once = np.concatenate([tri, np.ones((B, B), bool)], 1).astype(np.float32)
    return np.concatenate([once, once], 0)


def _sb_iotas():
    shape = (SB_BLOCK, SB_BLOCK)
    return lax.broadcasted_iota(jnp.int32, shape, 0), lax.broadcasted_iota(jnp.int32, shape, 1)


def _sb_heads(q, first):
    heads = []
    for g in range(SB_GROUP):
        qg = q[:, SB_BLOCK * g:SB_BLOCK * (g + 1)]
        zero = jnp.zeros_like(qg)
        heads += [(g, jnp.where(first, qg, zero)), (g, jnp.where(first, zero, qg))]
    return heads


def _lanes(x, g):
    return x[:, SB_BLOCK * g:SB_BLOCK * (g + 1)]


def sb_fwd(sqkv, comm=None):
    T = sqkv.shape[0]
    B = SB_BLOCK
    W = SB_GROUP * B
    groups = SB_WIDTH // W
    usum = jnp.asarray(_sb_sum_matrix(False), BF16)

    def body(q_ref, k_ref, v_ref, u_ref, o_ref, tl_ref, first_ref):
        p, i = pl.program_id(0), pl.program_id(1)
        row, lane = _sb_iotas()
        first = lane < SB_DH
        heads = _sb_heads(q_ref[...], first)
        u = u_ref[...]

        def more(loop):
            n, reachable, _ = loop
            return (SB_UNROLL * n <= i) & (reachable > 0)

        def step(loop):
            n, _, state = loop
            blocks = []
            for sub in range(SB_UNROLL):
                j = i - SB_UNROLL * n - sub
                off = pl.multiple_of(jnp.maximum(j, 0) * B, B)
                valid = ((lane + j * B) < (row + i * B)) & (j >= 0)
                blocks.append((k_ref[pl.ds(off, B), :], v_ref[pl.ds(off, B), :], valid))
            z = [[_dot_nt(qh, _lanes(kj, g)) for g, qh in heads] for kj, _, _ in blocks]
            lnb = [[jnp.where(valid, _neg_softplus(zz), 0.0) for zz in zs] for zs, (_, _, valid) in zip(z, blocks)]
            sums = [[_sum01_right(x, u) for x in xs] for xs in lnb]
            out = []
            for h, (carry, acc) in enumerate(state):
                for sub, (_, vj, valid) in enumerate(blocks):
                    expo = z[sub][h] + lnb[sub][h] + carry + sums[sub][h][:, :B]
                    acc = acc + _dot(jnp.where(valid, jnp.exp(expo), 0.0), _lanes(vj, heads[h][0]))
                    carry = carry + sums[sub][h][:, B:]
                out.append((carry, acc))
            state = tuple(out)
            worst = state[0][0]
            for carry, _ in state[1:]:
                worst = jnp.maximum(worst, carry)
            reachable = (jnp.max(worst) > SB_UNDERFLOW_LOG).astype(jnp.int32)
            return n + 1, reachable, state

        zero = jnp.zeros((B, B), F32)
        done, _, state = lax.while_loop(
            more, step, (jnp.int32(0), jnp.int32(1), tuple((zero, zero) for _ in heads)))
        for g in range(SB_GROUP):
            (tot0, acc0), (tot1, acc1) = state[2 * g], state[2 * g + 1]
            o_ref[:, B * g:B * (g + 1)] = jnp.where(first, acc0, acc1)
            tl_ref[:, B * g:B * (g + 1)] = jnp.where(first, tot0, tot1)
        first_ref[p, i] = jnp.maximum(i + 1 - SB_UNROLL * done, 0)

    def edge():
        p, i = pl.program_id(0), pl.program_id(1)
        return (p == 0) & (i == 0), (p == groups - 1) & (i == T // B - 1)

    return _pallas(
        body, comm=comm, edge=edge, name="sb_fwd", grid=(groups, T // B),
        in_specs=[pl.BlockSpec((B, W), lambda p, i: (i, p)),
                  pl.BlockSpec((T, W), lambda p, i: (0, groups + p)),
                  pl.BlockSpec((T, W), lambda p, i: (0, 2 * groups + p)),
                  pl.BlockSpec(usum.shape, lambda p, i: (0, 0))],
        out_specs=[pl.BlockSpec((B, W), lambda p, i: (i, p))] * 2 + [pl.BlockSpec(memory_space=pltpu.SMEM)],
        out_shape=[jax.ShapeDtypeStruct((T, SB_WIDTH), F32)] * 2 + [jax.ShapeDtypeStruct((groups, T // B), jnp.int32)],
        compiler_params=_params(("arbitrary", "arbitrary")),
    )(sqkv, sqkv, sqkv, usum)


def sb_bwd(sqkv, do, tl, first_block):
    T = sqkv.shape[0]
    B = SB_BLOCK
    W = SB_GROUP * B
    groups = SB_WIDTH // W
    upre = jnp.asarray(_sb_prefix_matrix(True), BF16)
    uexc = jnp.asarray(_sb_prefix_matrix(False), BF16)

    def body(q_ref, k_ref, v_ref, do_ref, tl_ref, up_ref, ue_ref, first_ref, dq_ref, dk_ref, dv_ref):
        p, i = pl.program_id(0), pl.program_id(1)

        @pl.when(i == 0)
        def _():
            dk_ref[...] = jnp.zeros(dk_ref.shape, F32)
            dv_ref[...] = jnp.zeros(dv_ref.shape, F32)

        row, lane = _sb_iotas()
        first = lane < SB_DH
        do = do_ref[...]
        tl_all = tl_ref[...]
        heads = []
        for (g, qh), at in zip(_sb_heads(q_ref[...], first), (0, B - 1) * SB_GROUP):
            dog = _lanes(do, g)
            keep = first if at == 0 else jnp.logical_not(first)
            heads.append((g, qh, jnp.where(keep, dog, jnp.zeros_like(dog)).astype(BF16),
                          _lanes(tl_all, g)[:, at:at + 1]))
        up = up_ref[...]
        ue = ue_ref[...]
        start = first_ref[p, i]

        def step(n, state):
            blocks = []
            for sub in range(SB_UNROLL):
                j = start + SB_UNROLL * n + sub
                off = pl.multiple_of(jnp.minimum(j, i) * B, B)
                valid = (lane + j * B) < (row + i * B)
                blocks.append((off, k_ref[pl.ds(off, B), :], v_ref[pl.ds(off, B), :], valid))
            combos = [(s, h) for s in range(SB_UNROLL) for h in range(len(heads))]
            z = {(s, h): _dot_nt(heads[h][1], _lanes(blocks[s][1], heads[h][0])) for s, h in combos}
            da = {(s, h): _dot_nt(heads[h][2], _lanes(blocks[s][2], heads[h][0])) for s, h in combos}
            lnb = {c: jnp.where(blocks[c[0]][3], _neg_softplus(z[c]), 0.0) for c in combos}
            lb = {c: z[c] + lnb[c] for c in combos}
            sums = {c: _sum01_right(lnb[c], up) for c in combos}
            a, w = {}, {}
            seen = [st[0] for st in state]
            for s, h in combos:
                expo = lb[s, h] + (heads[h][3] - seen[h] - sums[s, h][:, :B])
                a[s, h] = jnp.where(blocks[s][3], jnp.exp(expo), 0.0)
                w[s, h] = a[s, h] * da[s, h]
                seen[h] = seen[h] + sums[s, h][:, B:]
            wsums = {c: _sum01_right(w[c], ue) for c in combos}
            dz = {}
            seen_w = [st[1] for st in state]
            for s, h in combos:
                beta = jnp.exp(lb[s, h])
                before = seen_w[h] + wsums[s, h][:, :B]
                dz[s, h] = jnp.where(blocks[s][3], w[s, h] * (1.0 - beta) - before * beta, 0.0)
                seen_w[h] = seen_w[h] + wsums[s, h][:, B:]
            dq = [st[2] for st in state]
            for s, h in combos:
                dq[h] = dq[h] + _dot(dz[s, h], _lanes(blocks[s][1], heads[h][0]))
            for s in range(SB_UNROLL):
                off = blocks[s][0]
                for g in range(SB_GROUP):
                    h0, h1 = 2 * g, 2 * g + 1
                    dk_ref[pl.ds(off, B), B * g:B * (g + 1)] += (_dot_tn(dz[s, h0], heads[h0][1])
                                                                 + _dot_tn(dz[s, h1], heads[h1][1]))
                    dv_ref[pl.ds(off, B), B * g:B * (g + 1)] += (_dot_tn(a[s, h0], heads[h0][2])
                                                                 + _dot_tn(a[s, h1], heads[h1][2]))
            return tuple(zip(seen, seen_w, dq))

        zero = jnp.zeros((B, B), F32)
        trips = (i - start + SB_UNROLL) // SB_UNROLL
        state = lax.fori_loop(0, trips, step, tuple((zero, zero, zero) for _ in heads))
        for g in range(SB_GROUP):
            dq_ref[:, B * g:B * (g + 1)] = jnp.where(first, state[2 * g][2], state[2 * g + 1][2]) * SB_SCALE

    qblk = pl.BlockSpec((B, W), lambda p, i: (i, p))
    full = pl.BlockSpec((T, W), lambda p, i: (0, p))
    return pl.pallas_call(
        body, name="sb_bwd", grid=(groups, T // B),
        in_specs=[qblk, pl.BlockSpec((T, W), lambda p, i: (0, groups + p)),
                  pl.BlockSpec((T, W), lambda p, i: (0, 2 * groups + p)), qblk, qblk,
                  pl.BlockSpec(upre.shape, lambda p, i: (0, 0)), pl.BlockSpec(uexc.shape, lambda p, i: (0, 0)),
                  pl.BlockSpec(memory_space=pltpu.SMEM)],
        out_specs=[qblk, full, full],
        out_shape=[jax.ShapeDtypeStruct((T, SB_WIDTH), F32)] * 3,
        compiler_params=_params(("arbitrary", "arbitrary")),
    )(sqkv, sqkv, sqkv, do, tl, upre, uexc, first_block)


def _mixer_out(o_hg, hg, o_sb, g_hg, g_sb):
    n_hg, r_hg = _rms(o_hg)
    s_hg = _sigmoid(hg)
    n_sb, r_sb = _rms(o_sb)
    return dict(n_hg=n_hg, r_hg=r_hg, s_hg=s_hg, n_sb=n_sb, r_sb=r_sb,
                y_hg=n_hg * g_hg * (hg * s_hg), y_sb=n_sb * g_sb)


def mix_out_fwd(o_hg, proj_h, o_sb, x, norms, g_post, w_out, comm=None):
    T = x.shape[0]

    def body(ohg_ref, hg_ref, osb_ref, x_ref, nrm_ref, gp_ref, w_hbm, cat_ref, mix_ref, h1_ref, w_vmem, w_sems):
        w = _Resident(w_hbm, w_vmem, w_sems)
        nrm = nrm_ref[...]
        m = _mixer_out(ohg_ref[...], hg_ref[...], osb_ref[...], nrm[:, :HG_WIDTH], nrm[:, HG_WIDTH:])
        cat_ref[:, :HG_WIDTH] = m["y_hg"].astype(BF16)
        cat_ref[:, HG_WIDTH:] = m["y_sb"].astype(BF16)
        w.ready()
        mix = jnp.dot(cat_ref[...], w_vmem[...], preferred_element_type=F32)
        mix_ref[...] = mix
        mh, _ = _rms(mix)
        h1_ref[...] = x_ref[...] + mh * gp_ref[...]

    return _pallas(
        body, comm=comm, edge=_grid_edge(T // _wide_tile(T)), name="mix_out_fwd", grid=(T // _wide_tile(T),),
        in_specs=[_wide_spec(HG_WIDTH), _wide_spec(HG_WIDTH, 3), _wide_spec(SB_WIDTH), _wide_spec(D_MODEL),
                  _full_spec((1, D_MODEL)), _full_spec((1, D_MODEL)), ANY_SPEC],
        out_specs=[_wide_spec(D_MODEL)] * 3,
        out_shape=[jax.ShapeDtypeStruct((T, D_MODEL), BF16), jax.ShapeDtypeStruct((T, D_MODEL), F32),
                   jax.ShapeDtypeStruct((T, D_MODEL), F32)],
        scratch_shapes=[pltpu.VMEM(w_out.shape, BF16), pltpu.SemaphoreType.DMA((1,))],
        compiler_params=_params(("arbitrary",)),
    )(o_hg, proj_h, o_sb, x, norms, g_post, w_out)


def ffn_fwd(h1, g_pre, g_post, w_gu, w_down):
    T = h1.shape[0]
    pw = w_gu.shape[2]

    def body(h1_ref, gpre_ref, gpost_ref, wgu_hbm, wd_hbm, u2_ref, gu_ref, act_ref, y_ref, h2_ref,
             wgu_vmem, wd_vmem, gu_s, wgu_sems, wd_sems):
        wgu = _Resident(wgu_hbm, wgu_vmem, wgu_sems)
        wd = _Resident(wd_hbm, wd_vmem, wd_sems)
        h1v = h1_ref[...]
        hh, _ = _rms(h1v)
        u2 = (hh * gpre_ref[...]).astype(BF16)
        u2_ref[...] = u2
        for q in range(N_CHIPS):
            wgu.ready(q)
            gu_s[:, pw * q:pw * (q + 1)] = jnp.dot(u2, wgu_vmem[q], preferred_element_type=F32)
        gu_ref[...] = gu_s[...].astype(BF16)
        gate = gu_s[:, :D_FF]
        act = (gate * _sigmoid(gate) * gu_s[:, D_FF:]).astype(BF16)
        act_ref[...] = act
        wd.ready()
        y = jnp.dot(act, wd_vmem[...], preferred_element_type=F32)
        y_ref[...] = y
        yh, _ = _rms(y)
        h2_ref[...] = h1v + yh * gpost_ref[...]

    return pl.pallas_call(
        body, name="ffn_fwd", grid=(T // ROW_TILE,),
        in_specs=[_row_spec(D_MODEL), _full_spec((1, D_MODEL)), _full_spec((1, D_MODEL)), ANY_SPEC, ANY_SPEC],
        out_specs=[_row_spec(D_MODEL), _row_spec(2 * D_FF), _row_spec(D_FF), _row_spec(D_MODEL), _row_spec(D_MODEL)],
        out_shape=[jax.ShapeDtypeStruct((T, D_MODEL), BF16), jax.ShapeDtypeStruct((T, 2 * D_FF), BF16),
                   jax.ShapeDtypeStruct((T, D_FF), BF16), jax.ShapeDtypeStruct((T, D_MODEL), F32),
                   jax.ShapeDtypeStruct((T, D_MODEL), F32)],
        scratch_shapes=[pltpu.VMEM(w_gu.shape, BF16), pltpu.VMEM(w_down.shape, BF16),
                        pltpu.VMEM((ROW_TILE, 2 * D_FF), F32), pltpu.SemaphoreType.DMA((N_CHIPS,)), pltpu.SemaphoreType.DMA((1,))],
        compiler_params=_params(("arbitrary",)),
    )(h1, g_pre, g_post, w_gu, w_down)


def ple_loss(h2, p, target, w_ple, w_pg):
    T = h2.shape[0]
    pw = w_ple.shape[2]

    def body(h2_ref, p_ref, t_ref, wple_hbm, wpg_hbm, de_ref, ds_ref, dh2_ref, h2b_ref, pb_ref, pack_ref,
             wple_vmem, wpg_vmem, e_s, wple_sems, wpg_sems):
        wple = _Resident(wple_hbm, wple_vmem, wple_sems)
        wpg = _Resident(wpg_hbm, wpg_vmem, wpg_sems)
        _zero_first(pack_ref)
        h2v = h2_ref[...]
        h2b = h2v.astype(BF16)
        h2b_ref[...] = h2b
        pb = p_ref[...].astype(BF16)
        pb_ref[...] = pb
        for q in range(N_CHIPS):
            wple.ready(q)
            e_s[:, pw * q:pw * (q + 1)] = jnp.dot(pb, wple_vmem[q], preferred_element_type=F32)
        e = e_s[...]
        wpg.ready()
        sig = _sigmoid(jnp.dot(h2b, wpg_vmem[...], preferred_element_type=F32))
        err = h2v + e * sig - t_ref[...]
        part = 0.5 * jnp.sum(jnp.mean(err * err, axis=-1, keepdims=True), axis=0, keepdims=True)
        lane = lax.broadcasted_iota(jnp.int32, (1, D_MODEL), 1)
        pack_ref[ROW_LOSS:ROW_LOSS + 1, :] += jnp.where(lane == 0, part, 0.0)
        dh3 = err * (1.0 / D_MODEL)
        de_ref[...] = (dh3 * sig).astype(BF16)
        ds = (dh3 * e * sig * (1.0 - sig)).astype(BF16)
        ds_ref[...] = ds
        dh2_ref[...] = dh3 + _dot_nt(ds, wpg_vmem[...])

    return pl.pallas_call(
        body, name="ple_loss", grid=(T // _wide_tile(T),),
        in_specs=[_wide_spec(D_MODEL), _wide_spec(p.shape[1]), _wide_spec(D_MODEL), ANY_SPEC, ANY_SPEC],
        out_specs=[_wide_spec(D_MODEL), _wide_spec(D_MODEL), _wide_spec(D_MODEL), _wide_spec(D_MODEL),
                   _wide_spec(p.shape[1]), PACK_SPEC],
        out_shape=[jax.ShapeDtypeStruct((T, D_MODEL), BF16), jax.ShapeDtypeStruct((T, D_MODEL), BF16),
                   jax.ShapeDtypeStruct((T, D_MODEL), F32), jax.ShapeDtypeStruct((T, D_MODEL), BF16),
                   jax.ShapeDtypeStruct(p.shape, BF16), jax.ShapeDtypeStruct((8, D_MODEL), F32)],
        scratch_shapes=[pltpu.VMEM(w_ple.shape, BF16), pltpu.VMEM(w_pg.shape, BF16), pltpu.VMEM((_wide_tile(T), D_MODEL), F32), pltpu.SemaphoreType.DMA((N_CHIPS,)), pltpu.SemaphoreType.DMA((1,))],
        compiler_params=_params(("arbitrary",)),
    )(h2, p, target, w_ple, w_pg)


def ffn_bwd(dh2, y, h1, gu, g_pre, g_post, w_gu, w_down):
    T = h1.shape[0]
    pw = w_gu.shape[2]

    def body(dh2_ref, y_ref, h1_ref, gu_ref, gpre_ref, gpost_ref, wgu_hbm, wd_hbm, dy_ref, dgu_ref, dh1_ref, pack_ref,
             wgu_vmem, wd_vmem, wgu_sems, wd_sems):
        wgu = _Resident(wgu_hbm, wgu_vmem, wgu_sems)
        wd = _Resident(wd_hbm, wd_vmem, wd_sems)
        _zero_first(pack_ref)
        dh2v = dh2_ref[...]
        yh, ry = _rms(y_ref[...])
        dy, dw = _rms_bwd(dh2v, yh, ry, gpost_ref[...])
        pack_ref[ROW_FFN_POST:ROW_FFN_POST + 1, :] += _colsum(dw)
        dyb = dy.astype(BF16)
        dy_ref[...] = dyb
        wd.ready()
        dact = _dot_nt(dyb, wd_vmem[...])
        gate = gu_ref[:, :D_FF].astype(F32)
        up = gu_ref[:, D_FF:].astype(F32)
        sg = _sigmoid(gate)
        dgu_ref[:, :D_FF] = (dact * up * (sg * (1.0 + gate * (1.0 - sg)))).astype(BF16)
        dgu_ref[:, D_FF:] = (dact * gate * sg).astype(BF16)
        wgu.ready(0)
        du2 = _dot_nt(dgu_ref[:, :pw], wgu_vmem[0])
        for q in range(1, N_CHIPS):
            wgu.ready(q)
            du2 = du2 + _dot_nt(dgu_ref[:, pw * q:pw * (q + 1)], wgu_vmem[q])
        hh, rh = _rms(h1_ref[...])
        dh, dw = _rms_bwd(du2, hh, rh, gpre_ref[...])
        pack_ref[ROW_FFN_PRE:ROW_FFN_PRE + 1, :] += _colsum(dw)
        dh1_ref[...] = dh2v + dh

    return pl.pallas_call(
        body, name="ffn_bwd", grid=(T // ROW_TILE,),
        in_specs=[_row_spec(D_MODEL), _row_spec(D_MODEL), _row_spec(D_MODEL), _row_spec(2 * D_FF),
                  _full_spec((1, D_MODEL)), _full_spec((1, D_MODEL)), ANY_SPEC, ANY_SPEC],
        out_specs=[_row_spec(D_MODEL), _row_spec(2 * D_FF), _row_spec(D_MODEL), PACK_SPEC],
        out_shape=[jax.ShapeDtypeStruct((T, D_MODEL), BF16), jax.ShapeDtypeStruct((T, 2 * D_FF), BF16),
                   jax.ShapeDtypeStruct((T, D_MODEL), F32), jax.ShapeDtypeStruct((8, D_MODEL), F32)],
        scratch_shapes=[pltpu.VMEM(w_gu.shape, BF16), pltpu.VMEM(w_down.shape, BF16), pltpu.SemaphoreType.DMA((N_CHIPS,)), pltpu.SemaphoreType.DMA((1,))],
        compiler_params=_params(("arbitrary",)),
    )(dh2, y, h1, gu, g_pre, g_post, w_gu, w_down)


def mix_out_bwd(dh1, mix, o_hg, proj_h, o_sb, norms, g_post, w_out, comm=None):
    T = dh1.shape[0]

    def body(dh1_ref, mix_ref, ohg_ref, hg_ref, osb_ref, nrm_ref, gp_ref, w_hbm, dmix_ref, dohg_ref, dhg_ref, dosb_ref,
             pack_ref, w_vmem, w_sems):
        w = _Resident(w_hbm, w_vmem, w_sems)
        _zero_first(pack_ref)
        mh, rm = _rms(mix_ref[...])
        dmix, dw = _rms_bwd(dh1_ref[...], mh, rm, gp_ref[...])
        pack_ref[ROW_ATTN_POST:ROW_ATTN_POST + 1, :] += _colsum(dw)
        dmb = dmix.astype(BF16)
        dmix_ref[...] = dmb
        w.ready()
        dcat = _dot_nt(dmb, w_vmem[...])
        nrm = nrm_ref[...]
        g_hg, g_sb = nrm[:, :HG_WIDTH], nrm[:, HG_WIDTH:]
        hg = hg_ref[...]
        m = _mixer_out(ohg_ref[...], hg, osb_ref[...], g_hg, g_sb)
        d_hg = dcat[:, :HG_WIDTH]
        silu = hg * m["s_hg"]
        dhg_ref[...] = d_hg * (m["n_hg"] * g_hg) * (m["s_hg"] * (1.0 + hg * (1.0 - m["s_hg"])))
        dx, dw = _rms_bwd(d_hg * silu, m["n_hg"], m["r_hg"], g_hg)
        dohg_ref[...] = dx
        pack_ref[ROW_MIX_NORMS:ROW_MIX_NORMS + 1, :HG_WIDTH] += _colsum(dw)
        dx, dw = _rms_bwd(dcat[:, HG_WIDTH:], m["n_sb"], m["r_sb"], g_sb)
        dosb_ref[...] = dx
        pack_ref[ROW_MIX_NORMS:ROW_MIX_NORMS + 1, HG_WIDTH:] += _colsum(dw)

    return _pallas(
        body, comm=comm, edge=_grid_edge(T // _wide_tile(T)), name="mix_out_bwd", grid=(T // _wide_tile(T),),
        in_specs=[_wide_spec(D_MODEL), _wide_spec(D_MODEL), _wide_spec(HG_WIDTH), _wide_spec(HG_WIDTH, 3), _wide_spec(SB_WIDTH),
                  _full_spec((1, D_MODEL)), _full_spec((1, D_MODEL)), ANY_SPEC],
        out_specs=[_wide_spec(D_MODEL), _wide_spec(HG_WIDTH), _wide_spec(HG_WIDTH), _wide_spec(SB_WIDTH), PACK_SPEC],
        out_shape=[jax.ShapeDtypeStruct((T, D_MODEL), BF16), jax.ShapeDtypeStruct((T, HG_WIDTH), F32),
                   jax.ShapeDtypeStruct((T, HG_WIDTH), F32), jax.ShapeDtypeStruct((T, SB_WIDTH), F32),
                   jax.ShapeDtypeStruct((8, D_MODEL), F32)],
        scratch_shapes=[pltpu.VMEM(w_out.shape, BF16), pltpu.SemaphoreType.DMA((1,))],
        compiler_params=_params(("arbitrary",)),
    )(dh1, mix, o_hg, proj_h, o_sb, norms, g_post, w_out)


def in_proj_bwd(parts, x, dh1, g_pre, w_in, comm=None):
    T = x.shape[0]
    pw = w_in.shape[2]
    n_parts = len(parts)

    def body(*refs):
        part_refs = refs[:n_parts]
        x_ref, dh1_ref, g_ref, w_hbm, dproj_ref, dx_ref, pack_ref, w_vmem, w_sems = refs[n_parts:]
        w = _Resident(w_hbm, w_vmem, w_sems)
        _zero_first(pack_ref)
        for n, ref in enumerate(part_refs):
            dproj_ref[:, HG_WIDTH * n:HG_WIDTH * (n + 1)] = ref[...].astype(BF16)
        w.ready(0)
        du = _dot_nt(dproj_ref[:, :pw], w_vmem[0])
        for q in range(1, N_CHIPS):
            w.ready(q)
            du = du + _dot_nt(dproj_ref[:, pw * q:pw * (q + 1)], w_vmem[q])
        xh, r = _rms(x_ref[...])
        dx, dw = _rms_bwd(du, xh, r, g_ref[...])
        pack_ref[ROW_ATTN_PRE:ROW_ATTN_PRE + 1, :] += _colsum(dw)
        dx_ref[...] = dh1_ref[...] + dx

    return _pallas(
        body, comm=comm, edge=_grid_edge(T // _wide_tile(T)), name="in_proj_bwd", grid=(T // _wide_tile(T),),
        in_specs=[_wide_spec(HG_WIDTH)] * n_parts + [_wide_spec(D_MODEL), _wide_spec(D_MODEL), _full_spec((1, D_MODEL)), ANY_SPEC],
        out_specs=[_wide_spec(n_parts * HG_WIDTH), _wide_spec(D_MODEL), PACK_SPEC],
        out_shape=[jax.ShapeDtypeStruct((T, n_parts * HG_WIDTH), BF16), jax.ShapeDtypeStruct((T, D_MODEL), F32),
                   jax.ShapeDtypeStruct((8, D_MODEL), F32)],
        scratch_shapes=[pltpu.VMEM(w_in.shape, BF16), pltpu.SemaphoreType.DMA((N_CHIPS,))],
        compiler_params=_params(("arbitrary",)),
    )(*parts, x, dh1, g_pre, w_in)


def weight_grad(a, g, name, *, tm, tn, tk=512, col_pieces=False, comm=None):
    T, M = a.shape
    N = g.shape[1]
    tk = min(tk, T)
    steps = T // tk

    def body(a_ref, g_ref, o_ref):
        @pl.when(pl.program_id(2) == 0)
        def _():
            o_ref[...] = jnp.zeros(o_ref.shape, F32)

        o_ref[...] += _dot_tn(a_ref[...], g_ref[...]).reshape(o_ref.shape)

    if col_pieces:
        out_shape = jax.ShapeDtypeStruct((N // tn, M, tn), F32)
        out_spec = pl.BlockSpec((1, tm, tn), lambda i, j, k: (j, i, 0))
    else:
        out_shape = jax.ShapeDtypeStruct((M, N), F32)
        out_spec = pl.BlockSpec((tm, tn), lambda i, j, k: (i, j))
    grid = (M // tm, N // tn, steps)

    def edge():
        at = [pl.program_id(d) for d in range(3)]
        return ((at[0] == 0) & (at[1] == 0) & (at[2] == 0),
                (at[0] == grid[0] - 1) & (at[1] == grid[1] - 1) & (at[2] == grid[2] - 1))

    return _pallas(
        body, comm=comm, edge=edge, name=name, grid=grid,
        in_specs=[pl.BlockSpec((tk, tm), lambda i, j, k: (k, i)), pl.BlockSpec((tk, tn), lambda i, j, k: (k, j))],
        out_specs=[out_spec], out_shape=[out_shape],
        compiler_params=_params(("arbitrary", "arbitrary", "arbitrary")),
    )(a, g)


def _place():
    x, y, c = lax.axis_index("x"), lax.axis_index("y"), lax.axis_index("c")
    chips = [(1 - x, y), (x, 1 - y), (1 - x, 1 - y)]
    return x, y, c, chips


def _chip_index(cx, cy):
    return 2 * cx + cy


def _own_slot(piece, slots):
    me = _chip_index(lax.axis_index("x"), lax.axis_index("y"))
    landing = lax.empty((slots,) + piece.shape[1:], piece.dtype)
    return lax.dynamic_update_slice(landing, piece, (me,) + (0,) * (piece.ndim - 1))


def _rcopy(src, dst, send_sem, recv_sem, device):
    return pltpu.make_async_remote_copy(src_ref=src, dst_ref=dst, send_sem=send_sem, recv_sem=recv_sem,
                                        device_id=device, device_id_type=MESH)


def gather_weights(shards):
    n = len(shards)

    def body(*refs):
        ins, outs = refs[:n], refs[2 * n:3 * n]
        send_sems, recv_sems = refs[3 * n:]
        x, y, c, chips = _place()
        me = _chip_index(x, y)
        sibling = (x, y, 1 - c)

        def rows(w, core):
            half = ins[w].shape[0] // 2
            return pl.ds(core * half, half)

        sends = []
        for w in range(n):
            for j, chip in enumerate(chips):
                sends.append(_rcopy(ins[w].at[rows(w, c)], outs[w].at[me, rows(w, c)],
                                    send_sems.at[6 * w + j], recv_sems.at[6 * w + j], (*chip, c)))
        for cp in sends:
            cp.start()
        passed = []
        for w in range(n):
            for j, chip in enumerate(chips):
                block = outs[w].at[_chip_index(*chip), rows(w, c)]
                _rcopy(block, block, send_sems.at[6 * w + j], recv_sems.at[6 * w + j], (*chip, c)).wait_recv()
                cp = _rcopy(block, block, send_sems.at[6 * w + 3 + j], recv_sems.at[6 * w + 3 + j], sibling)
                cp.start()
                passed.append(cp)
        for w in range(n):
            for j, chip in enumerate(chips):
                block = outs[w].at[_chip_index(*chip), rows(w, 1 - c)]
                _rcopy(block, block, send_sems.at[6 * w + 3 + j], recv_sems.at[6 * w + 3 + j], sibling).wait_recv()
        for cp in sends + passed:
            cp.wait_send()

    filled = [_own_slot(s[None], N_CHIPS) for s in shards]
    return pl.pallas_call(
        body, name="gather_weights",
        in_specs=[ANY_SPEC] * (2 * n), out_specs=[ANY_SPEC] * n,
        out_shape=[jax.ShapeDtypeStruct(f.shape, f.dtype) for f in filled],
        input_output_aliases={n + w: w for w in range(n)},
        scratch_shapes=[pltpu.SemaphoreType.DMA((6 * n,)), pltpu.SemaphoreType.DMA((6 * n,))],
    )(*shards, *filled)


def _run_comm(comm, name):
    c_in, c_out = len(comm.inputs), len(comm.out_shape)

    def body(*refs):
        parts = refs[:c_in], refs[c_in:c_in + c_out], refs[c_in + c_out:]
        comm.start(*parts)
        comm.finish(*parts)

    return pl.pallas_call(
        body, name=name, in_specs=[ANY_SPEC] * c_in, out_specs=[ANY_SPEC] * c_out, out_shape=comm.out_shape,
        scratch_shapes=comm.scratch, input_output_aliases=comm.aliases)(*comm.inputs)


def _both(first, second):
    n_in, n_out, n_scr = len(first.inputs), len(first.out_shape), len(first.scratch)

    def split(ins, outs, scr):
        return (ins[:n_in], outs[:n_out], scr[:n_scr]), (ins[n_in:], outs[n_out:], scr[n_scr:])

    def start(*refs):
        a, b = split(*refs)
        first.start(*a)
        second.start(*b)

    def finish(*refs):
        a, b = split(*refs)
        first.finish(*a)
        second.finish(*b)

    aliases = dict(first.aliases)
    aliases.update({n_in + i: n_out + o for i, o in second.aliases.items()})
    return Comm(first.inputs + second.inputs, first.out_shape + second.out_shape, aliases,
                first.scratch + second.scratch, start, finish)


def _dma_sems(count):
    return [pltpu.SemaphoreType.DMA((count,)), pltpu.SemaphoreType.DMA((count,))]


def gather_over_ici(shards):
    n = len(shards)

    def copies(ins, outs, sems):
        send_sems, recv_sems = sems
        x, y, c, chips = _place()
        me = _chip_index(x, y)
        pairs = []
        for w in range(n):
            half = shards[w].shape[0] // 2
            rows = pl.ds(c * half, half)
            for j, chip in enumerate(chips):
                k = 3 * w + j
                landed = outs[w].at[_chip_index(*chip), rows]
                pairs.append((_rcopy(ins[w].at[rows], outs[w].at[me, rows], send_sems.at[k], recv_sems.at[k], (*chip, c)),
                              _rcopy(landed, landed, send_sems.at[k], recv_sems.at[k], (*chip, c))))
        return pairs

    def start(*refs):
        for send, _ in copies(*refs):
            send.start()

    def finish(*refs):
        pairs = copies(*refs)
        for _, landed in pairs:
            landed.wait_recv()
        for send, _ in pairs:
            send.wait_send()

    filled = [_own_slot(s[None], N_CHIPS) for s in shards]
    return Comm(list(shards) + filled, [jax.ShapeDtypeStruct(f.shape, f.dtype) for f in filled],
                {n + w: w for w in range(n)}, _dma_sems(3 * n), start, finish)


def gather_over_d2d(landed):
    n = len(landed)

    def copies(ins, outs, sems):
        send_sems, recv_sems = sems
        x, y, c, chips = _place()
        sibling = (x, y, 1 - c)
        pairs = []
        for w in range(n):
            half = landed[w].shape[1] // 2
            for j, chip in enumerate(chips):
                k = 3 * w + j
                mine = outs[w].at[_chip_index(*chip), pl.ds(c * half, half)]
                theirs = outs[w].at[_chip_index(*chip), pl.ds((1 - c) * half, half)]
                pairs.append((_rcopy(mine, mine, send_sems.at[k], recv_sems.at[k], sibling),
                              _rcopy(theirs, theirs, send_sems.at[k], recv_sems.at[k], sibling)))
        return pairs

    def start(*refs):
        for send, _ in copies(*refs):
            send.start()

    def finish(*refs):
        pairs = copies(*refs)
        for _, arrived in pairs:
            arrived.wait_recv()
        for send, _ in pairs:
            send.wait_send()

    return Comm(list(landed), [jax.ShapeDtypeStruct(a.shape, a.dtype) for a in landed], {w: w for w in range(n)},
                _dma_sems(3 * n), start, finish)


def core_halves(grads):
    n = len(grads)

    def copies(ins, outs, sems):
        send_sems, recv_sems = sems
        x, y, c, _ = _place()
        out = []
        for w in range(n):
            half = grads[w].shape[1] // 2
            out.append(_rcopy(ins[w].at[:, pl.ds((1 - c) * half, half), :], outs[w],
                              send_sems.at[w], recv_sems.at[w], (x, y, 1 - c)))
        return out

    def start(*refs):
        for cp in copies(*refs):
            cp.start()

    def finish(*refs):
        for cp in copies(*refs):
            cp.wait()

    return Comm(list(grads), [jax.ShapeDtypeStruct((g.shape[0], g.shape[1] // 2, g.shape[2]), g.dtype) for g in grads],
                {}, _dma_sems(n), start, finish)


def chip_partials(partials):
    n = len(partials)

    def copies(ins, outs, sems):
        send_sems, recv_sems = sems
        x, y, c, chips = _place()
        me = _chip_index(x, y)
        pairs = []
        for w in range(n):
            for j, chip in enumerate(chips):
                k = 3 * w + j
                landed = outs[w].at[_chip_index(*chip)]
                pairs.append((_rcopy(ins[w].at[_chip_index(*chip)], outs[w].at[me], send_sems.at[k], recv_sems.at[k],
                                     (*chip, c)),
                              _rcopy(landed, landed, send_sems.at[k], recv_sems.at[k], (*chip, c))))
        return pairs

    def start(*refs):
        for send, _ in copies(*refs):
            send.start()

    def finish(*refs):
        pairs = copies(*refs)
        for _, landed in pairs:
            landed.wait_recv()
        for send, _ in pairs:
            send.wait_send()

    me = _chip_index(lax.axis_index("x"), lax.axis_index("y"))
    filled = [_own_slot(lax.dynamic_index_in_dim(p, me, 0, keepdims=True), N_CHIPS) for p in partials]
    return Comm(list(partials) + filled, [jax.ShapeDtypeStruct(p.shape, p.dtype) for p in partials],
                {n + w: w for w in range(n)}, _dma_sems(3 * n), start, finish)


def join_core_halves(grads):
    n = len(grads)

    def body(*refs):
        outs = refs[n:2 * n]
        send_sems, recv_sems = refs[2 * n:]
        x, y, c, _ = _place()
        sibling = (x, y, 1 - c)
        copies = []
        for w in range(n):
            half = outs[w].shape[0] // 2
            mine = outs[w].at[pl.ds(c * half, half), :]
            copies.append(_rcopy(mine, mine, send_sems.at[w], recv_sems.at[w], sibling))
        for cp in copies:
            cp.start()
        for w in range(n):
            half = outs[w].shape[0] // 2
            theirs = outs[w].at[pl.ds((1 - c) * half, half), :]
            _rcopy(theirs, theirs, send_sems.at[w], recv_sems.at[w], sibling).wait_recv()
        for cp in copies:
            cp.wait_send()

    return pl.pallas_call(
        body, name="join_core_halves",
        in_specs=[ANY_SPEC] * n, out_specs=[ANY_SPEC] * n,
        out_shape=[jax.ShapeDtypeStruct(g.shape, g.dtype) for g in grads],
        input_output_aliases={w: w for w in range(n)},
        scratch_shapes=[pltpu.SemaphoreType.DMA((n,)), pltpu.SemaphoreType.DMA((n,))],
    )(*grads)


def _elementwise_rows(rows, cap=512):
    for t in range(min(rows, cap), 0, -8):
        if rows % t == 0 and t % 16 == 0:
            return t
    return rows


def add_core_halves(grad, got, core, name):
    _, rows, cols = got.shape
    tr = _elementwise_rows(rows)
    nt = rows // tr

    def body(core_ref, a_ref, b_ref, o_ref):
        o_ref[...] = (a_ref[...] + b_ref[...]).astype(BF16)

    spec = pl.BlockSpec((1, tr, cols), lambda q, i, core_ref: (q, i, 0))
    own = pl.BlockSpec((1, tr, cols), lambda q, i, core_ref: (q, core_ref[0] * nt + i, 0))
    return pl.pallas_call(
        body, name=name,
        grid_spec=pltpu.PrefetchScalarGridSpec(num_scalar_prefetch=1, grid=(N_CHIPS, nt), in_specs=[own, spec],
                                               out_specs=spec),
        out_shape=jax.ShapeDtypeStruct(got.shape, BF16),
        compiler_params=_params(("arbitrary", "arbitrary")),
    )(core, grad, got)


def add_chip_partials(parts, core, name):
    _, rows, cols = parts.shape
    tr = _elementwise_rows(rows)
    nt = rows // tr

    def body(core_ref, p_ref, o_ref):
        acc = p_ref[0].astype(F32)
        for q in range(1, N_CHIPS):
            acc = acc + p_ref[q].astype(F32)
        o_ref[...] = acc

    return pl.pallas_call(
        body, name=name,
        grid_spec=pltpu.PrefetchScalarGridSpec(
            num_scalar_prefetch=1, grid=(nt,),
            in_specs=[pl.BlockSpec((N_CHIPS, tr, cols), lambda i, core_ref: (0, i, 0))],
            out_specs=pl.BlockSpec((tr, cols), lambda i, core_ref: (core_ref[0] * nt + i, 0))),
        out_shape=jax.ShapeDtypeStruct((2 * rows, cols), F32),
        compiler_params=_params(("arbitrary",)),
    )(core, parts)


def _adamw_math(w, g, m, v):
    m = ADAM_B1 * m + (1.0 - ADAM_B1) * g
    v = ADAM_B2 * v + (1.0 - ADAM_B2) * (g * g)
    m_hat = m / (1.0 - ADAM_B1 ** ADAM_STEP)
    v_hat = v / (1.0 - ADAM_B2 ** ADAM_STEP)
    delta = -ADAM_LR * (m_hat / (jnp.sqrt(v_hat) + ADAM_EPS) + ADAM_WD * w)
    return delta, m, v


def adamw(w, g, m, v, name):
    rows, cols = w.shape
    tr = _elementwise_rows(rows, 256)

    def body(w_ref, g_ref, m_ref, v_ref, d_ref, nm_ref, nv_ref):
        d, nm, nv = _adamw_math(w_ref[...], g_ref[...], m_ref[...], v_ref[...])
        d_ref[...] = d
        nm_ref[...] = nm
        nv_ref[...] = nv

    spec = pl.BlockSpec((tr, cols), lambda i: (i, 0))
    return pl.pallas_call(
        body, name=name, grid=(rows // tr,), in_specs=[spec] * 4, out_specs=[spec] * 3,
        out_shape=[jax.ShapeDtypeStruct((rows, cols), F32)] * 3,
        compiler_params=_params(("arbitrary",)),
    )(w, g, m, v)


def reduce_small(packs, w, m, v):
    n = len(packs)
    n_dev = 8
    flips = [(fx, fy, fc) for fx in (0, 1) for fy in (0, 1) for fc in (0, 1)][1:]

    def body(*refs):
        pack_refs = refs[:n]
        w_ref, m_ref, v_ref, g_out, d_out, m_out, v_out, mine, slots, send_sems, recv_sems = refs[n:]
        x, y, c, _ = _place()
        me = 4 * x + 2 * y + c
        acc = pack_refs[0][...]
        for ref in pack_refs[1:]:
            acc = acc + ref[...]
        mine[...] = acc
        sends = []
        for k, (fx, fy, fc) in enumerate(flips):
            peer = (x ^ fx, y ^ fy, c ^ fc)
            sends.append(_rcopy(mine, slots.at[me], send_sems.at[k], recv_sems.at[me], peer))
        for cp in sends:
            cp.start()
        slots[me] = acc
        for fx, fy, fc in flips:
            src = 4 * (x ^ fx) + 2 * (y ^ fy) + (c ^ fc)
            _rcopy(mine, slots.at[src], send_sems.at[0], recv_sems.at[src], (x, y, c)).wait_recv()
        for cp in sends:
            cp.wait_send()
        total = slots[0]
        for d in range(1, n_dev):
            total = total + slots[d]
        g_out[...] = total
        d, nm, nv = _adamw_math(w_ref[...], total, m_ref[...], v_ref[...])
        d_out[...] = d
        m_out[...] = nm
        v_out[...] = nv

    vm = pl.BlockSpec(memory_space=pltpu.VMEM)
    return pl.pallas_call(
        body, name="reduce_small",
        in_specs=[vm] * (n + 3), out_specs=[vm] * 4,
        out_shape=[jax.ShapeDtypeStruct((8, D_MODEL), F32)] * 4,
        scratch_shapes=[pltpu.VMEM((8, D_MODEL), F32), pltpu.VMEM((n_dev, 8, D_MODEL), F32),
                        pltpu.SemaphoreType.DMA((len(flips),)), pltpu.SemaphoreType.DMA((n_dev,))],
    )(*packs, w, m, v)


def _column_pieces(g):
    return g.reshape(g.shape[0], N_CHIPS, g.shape[1] // N_CHIPS).transpose(1, 0, 2)


def _pack_small(attn_pre, gamma, hg_norm, sb_norm, attn_post, ffn_pre, ffn_post):
    rows = [attn_pre, gamma.reshape(1, D_MODEL), jnp.concatenate([hg_norm, sb_norm], axis=1), attn_post, ffn_pre, ffn_post,
            jnp.zeros((2, D_MODEL), F32)]
    return jnp.concatenate(rows, axis=0)


def _unpack_small(pack):
    return (pack[ROW_ATTN_PRE:ROW_ATTN_PRE + 1], pack[ROW_GAMMA].reshape(2, HG_WIDTH),
            pack[ROW_MIX_NORMS:ROW_MIX_NORMS + 1, :HG_WIDTH], pack[ROW_MIX_NORMS:ROW_MIX_NORMS + 1, HG_WIDTH:],
            pack[ROW_ATTN_POST:ROW_ATTN_POST + 1], pack[ROW_FFN_PRE:ROW_FFN_PRE + 1], pack[ROW_FFN_POST:ROW_FFN_POST + 1])


def kernel(x, p, attn_pre_norm, w_in, hg_lower_gamma, hg_out_norm, sb_out_norm, w_out, attn_post_norm, ffn_pre_norm, w_gate_up, w_down, ffn_post_norm, ple_proj, ple_gate, loss_target, m_attn_pre_norm, m_w_in, m_hg_lower_gamma, m_hg_out_norm, m_sb_out_norm, m_w_out, m_attn_post_norm, m_ffn_pre_norm, m_w_gate_up, m_w_down, m_ffn_post_norm, m_ple_proj, m_ple_gate, v_attn_pre_norm, v_w_in, v_hg_lower_gamma, v_hg_out_norm, v_sb_out_norm, v_w_out, v_attn_post_norm, v_ffn_pre_norm, v_w_gate_up, v_w_down, v_ffn_post_norm, v_ple_proj, v_ple_gate):
    x2 = x[0]
    p2 = p[0, 0]
    target = loss_target[0]
    big = dict(w_in=(w_in, m_w_in, v_w_in), w_out=(w_out, m_w_out, v_w_out), w_gate_up=(w_gate_up, m_w_gate_up, v_w_gate_up),
               w_down=(w_down, m_w_down, v_w_down), ple_proj=(ple_proj, m_ple_proj, v_ple_proj),
               ple_gate=(ple_gate, m_ple_gate, v_ple_gate))
    names = list(big)
    big = {k: tuple(a[0] for a in t) for k, t in big.items()}

    shard16 = {k: big[k][0].astype(BF16) for k in names}
    w_in_full, = gather_weights([shard16["w_in"]])
    mix_norms = jnp.concatenate([hg_out_norm, sb_out_norm], axis=1)
    small_ones = ["w_out", "ple_proj", "ple_gate"]

    proj_h, sqkv, u1, *landed_small = in_proj_fwd(
        x2, attn_pre_norm, w_in_full, comm=gather_over_ici([shard16[k] for k in small_ones]))
    o_sb, sb_totals, sb_first, landed_gu = sb_fwd(sqkv, comm=gather_over_ici([shard16["w_gate_up"]]))
    o_hg, states, landed_down, *full_small = hgrn2_fwd(
        proj_h, hg_lower_gamma, comm=_both(gather_over_ici([shard16["w_down"]]), gather_over_d2d(landed_small)))
    full = dict(zip(small_ones, full_small), w_in=w_in_full)
    w_out_full = full["w_out"].reshape(D_MODEL, D_MODEL)
    w_pg_full = full["ple_gate"].reshape(D_MODEL, D_MODEL)
    cat, mix, h1, full["w_gate_up"], full["w_down"] = mix_out_fwd(
        o_hg, proj_h, o_sb, x2, mix_norms, attn_post_norm, w_out_full, comm=gather_over_d2d([landed_gu, landed_down]))
    w_down_full = full["w_down"].reshape(D_FF, D_MODEL)
    u2, gu, act, y, h2 = ffn_fwd(h1, ffn_pre_norm, ffn_post_norm, full["w_gate_up"], w_down_full)

    core = lax.axis_index("c").astype(jnp.int32).reshape(1)
    de, ds, dh2, h2b, pb, pack_loss = ple_loss(h2, p2, target, full["ple_proj"], w_pg_full)
    dy, dgu, dh1, pack_ffn = ffn_bwd(dh2, y, h1, gu, ffn_pre_norm, ffn_post_norm, full["w_gate_up"], w_down_full)
    local = {}
    local["w_gate_up"], = weight_grad(u2, dgu, "grad_w_gate_up", tm=D_MODEL, tn=full["w_gate_up"].shape[2], tk=1024,
                                      col_pieces=True)
    grad_down, got_gu = weight_grad(act, dy, "grad_w_down", tm=D_FF // 2, tn=D_MODEL, tk=1024,
                                    comm=core_halves([local["w_gate_up"]]))
    local["w_down"] = grad_down.reshape(full["w_down"].shape)
    local["ple_proj"] = _column_pieces(weight_grad(pb, de, "grad_ple_proj", tm=pb.shape[1], tn=D_MODEL, tk=1024)[0])
    grad_pg, got_down = weight_grad(h2b, ds, "grad_ple_gate", tm=D_MODEL, tn=D_MODEL, tk=1024,
                                    comm=core_halves([local["w_down"]]))
    local["ple_gate"] = grad_pg.reshape(full["ple_gate"].shape)
    early = list(local)
    dmix, do_hg, dhg, do_sb, pack_mix, *got_ple = mix_out_bwd(
        dh1, mix, o_hg, proj_h, o_sb, mix_norms, attn_post_norm, w_out_full,
        comm=core_halves([local["ple_proj"], local["ple_gate"]]))
    got = [got_gu, got_down] + got_ple
    partial = [add_core_halves(local[k], g, core, "add_core_halves_" + k) for k, g in zip(early, got)]
    local["w_out"] = weight_grad(cat, dmix, "grad_w_out", tm=D_MODEL, tn=D_MODEL, tk=1024)[0].reshape(full["w_out"].shape)
    dsq, dsk, dsv = sb_bwd(sqkv, do_sb, sb_totals, sb_first)
    dhq, dhf, dhi, pack_hg, *by_source, got_out = hgrn2_bwd(
        proj_h, hg_lower_gamma, states, do_hg, comm=_both(chip_partials(partial[:2]), core_halves([local["w_out"]])))
    early.append("w_out")
    partial.append(add_core_halves(local["w_out"], got_out, core, "add_core_halves_w_out"))
    dproj, grad_x, pack_in = in_proj_bwd([dhq, dhf, dhi, dhg, dsq, dsk, dsv], x2, dh1, attn_pre_norm, full["w_in"])

    late = ["w_in"]
    local["w_in"], *more = weight_grad(u1, dproj, "grad_w_in", tm=D_MODEL, tn=full["w_in"].shape[2], tk=1024,
                                       col_pieces=True, comm=chip_partials(partial[2:]))
    halves = {k: add_chip_partials(s, core, "add_chip_partials_" + k) for k, s in zip(early, by_source + more)}
    got = _run_comm(core_halves([local[k] for k in late]), "exchange_core_halves")
    partial = [add_core_halves(local[k], g, core, "add_core_halves_" + k) for k, g in zip(late, got)]
    by_source = _run_comm(chip_partials(partial), "exchange_chip_partials")
    halves.update({k: add_chip_partials(s, core, "add_chip_partials_" + k) for k, s in zip(late, by_source)})
    grads = dict(zip(names, join_core_halves([halves[k] for k in names])))

    upd = {k: adamw(big[k][0], grads[k], big[k][1], big[k][2], "adamw_" + k) for k in names}

    small = reduce_small(
        [pack_loss, pack_ffn, pack_mix, pack_hg, pack_in],
        _pack_small(attn_pre_norm, hg_lower_gamma, hg_out_norm, sb_out_norm, attn_post_norm, ffn_pre_norm, ffn_post_norm),
        _pack_small(m_attn_pre_norm, m_hg_lower_gamma, m_hg_out_norm, m_sb_out_norm, m_attn_post_norm, m_ffn_pre_norm, m_ffn_post_norm),
        _pack_small(v_attn_pre_norm, v_hg_lower_gamma, v_hg_out_norm, v_sb_out_norm, v_attn_post_norm, v_ffn_pre_norm, v_ffn_post_norm),
    )
    loss = small[0][ROW_LOSS, 0]
    s_grad, s_delta, s_m, s_v = (_unpack_small(t) for t in small)

    def ordered(small_vals, big_vals):
        a_pre, gam, hg_n, sb_n, a_post, f_pre, f_post = small_vals
        b = {k: big_vals[k][None] for k in names}
        return (a_pre, b["w_in"], gam, hg_n, sb_n, b["w_out"], a_post, f_pre, b["w_gate_up"], b["w_down"], f_post,
                b["ple_proj"], b["ple_gate"])

    return (loss, grad_x[None],
            *ordered(s_grad, grads),
            *ordered(s_delta, {k: upd[k][0] for k in names}),
            *ordered(s_m, {k: upd[k][1] for k in names}),
            *ordered(s_v, {k: upd[k][2] for k in names}))
```

```python
from typing import Callable, NamedTuple

import numpy as np
import jax
import jax.numpy as jnp
from jax import lax
from jax.experimental import pallas as pl
from jax.experimental.pallas import tpu as pltpu

F32 = jnp.float32
BF16 = jnp.bfloat16
MESH = pl.DeviceIdType.MESH

RMS_EPS = 1e-6
D_MODEL = 1024
HG_WIDTH = 512
HG_HEADS = 4
HG_DK = 128
HG_CHUNK = 64
HG_LEVELS = (32, 16, 8, 4, 2, 1)
HG_CHUNKS_PER_STEP = 2
SB_WIDTH = 512
SB_BLOCK = 128
SB_DH = 64
SB_SCALE = SB_DH ** -0.5
SB_UNDERFLOW_LOG = -87.5
SB_UNROLL = 2
SB_GROUP = 4
D_FF = 2816
N_CHIPS = 4
ROW_TILE = 256
WIDE_ROW_TILE = 512
V7X_VMEM_LIMIT = 56 * 1024 * 1024

ADAM_LR = 0.001
ADAM_B1 = 0.9
ADAM_B2 = 0.999
ADAM_EPS = 1e-08
ADAM_WD = 0.01
ADAM_STEP = 10

ROW_ATTN_PRE, ROW_GAMMA, ROW_MIX_NORMS, ROW_ATTN_POST, ROW_FFN_PRE, ROW_FFN_POST, ROW_LOSS = range(7)


def _params(sem=None, vmem=V7X_VMEM_LIMIT):
    return pltpu.CompilerParams(dimension_semantics=sem, vmem_limit_bytes=vmem)


def _dot(a, b):
    return jnp.dot(a.astype(BF16), b.astype(BF16), preferred_element_type=F32)


def _dot_nt(a, b):
    return lax.dot_general(a.astype(BF16), b.astype(BF16), (((1,), (1,)), ((), ())), preferred_element_type=F32)


def _dot_tn(a, b):
    return lax.dot_general(a.astype(BF16), b.astype(BF16), (((0,), (0,)), ((), ())), preferred_element_type=F32)


def _split(x):
    hi = x.astype(BF16)
    lo = (x - hi.astype(F32)).astype(BF16)
    return hi, lo


def _sum01_left(m01, x):
    hi, lo = _split(x)
    return jnp.dot(m01, hi, preferred_element_type=F32) + jnp.dot(m01, lo, preferred_element_type=F32)


def _sum01_right(x, m01_twice):
    hi, lo = _split(x)
    return jnp.dot(jnp.concatenate([hi, lo], axis=1), m01_twice, preferred_element_type=F32)


def _rms(x):
    r = lax.rsqrt(jnp.mean(x * x, axis=-1, keepdims=True) + RMS_EPS)
    return x * r, r


def _rms_bwd(dy, xhat, r, w):
    dxh = dy * w
    dx = r * (dxh - xhat * jnp.mean(dxh * xhat, axis=-1, keepdims=True))
    return dx, dy * xhat


def _sigmoid(x):
    return 1.0 / (1.0 + jnp.exp(-x))


def _neg_softplus(z):
    return -(jnp.maximum(z, 0.0) + jnp.log(1.0 + jnp.exp(-jnp.abs(z))))


def _colsum(x):
    return jnp.sum(x, axis=0, keepdims=True)


def _load_once(src_hbm, dst_vmem):
    @pl.when(pl.program_id(0) == 0)
    def _():
        pltpu.sync_copy(src_hbm, dst_vmem)


def _zero_first(ref):
    @pl.when(pl.program_id(0) == 0)
    def _():
        ref[...] = jnp.zeros(ref.shape, ref.dtype)


def _row_spec(width, col=0, rows=ROW_TILE):
    return pl.BlockSpec((rows, width), lambda i, col=col: (i, col))


def _wide_spec(width, col=0):
    return _row_spec(width, col, WIDE_ROW_TILE)


def _wide_tile(T):
    assert T % WIDE_ROW_TILE == 0
    return WIDE_ROW_TILE


def _full_spec(shape):
    return pl.BlockSpec(shape, lambda *_: (0,) * len(shape))


ANY_SPEC = pl.BlockSpec(memory_space=pl.ANY)
PACK_SPEC = _full_spec((8, D_MODEL))


class Comm(NamedTuple):
    inputs: list
    out_shape: list
    aliases: dict
    scratch: list
    start: Callable
    finish: Callable


def _pallas(body, *, comm=None, edge=None, in_specs, out_specs, out_shape, scratch_shapes=(), **kw):
    if comm is None:
        return pl.pallas_call(body, in_specs=in_specs, out_specs=out_specs, out_shape=out_shape,
                              scratch_shapes=scratch_shapes, **kw)
    n_in, n_out, n_scr = len(in_specs), len(out_specs), len(scratch_shapes)
    c_in, c_out = len(comm.inputs), len(comm.out_shape)

    def both(*refs):
        ins, c_ins = refs[:n_in], refs[n_in:n_in + c_in]
        outs = refs[n_in + c_in:n_in + c_in + n_out]
        c_outs = refs[n_in + c_in + n_out:n_in + c_in + n_out + c_out]
        rest = refs[n_in + c_in + n_out + c_out:]
        scr, c_scr = rest[:n_scr], rest[n_scr:]
        first, last = edge()

        @pl.when(first)
        def _():
            comm.start(c_ins, c_outs, c_scr)

        body(*ins, *outs, *scr)

        @pl.when(last)
        def _():
            comm.finish(c_ins, c_outs, c_scr)

    call = pl.pallas_call(
        both, in_specs=list(in_specs) + [ANY_SPEC] * c_in, out_specs=list(out_specs) + [ANY_SPEC] * c_out,
        out_shape=list(out_shape) + list(comm.out_shape), scratch_shapes=list(scratch_shapes) + list(comm.scratch),
        input_output_aliases={n_in + a: n_out + b for a, b in comm.aliases.items()}, **kw)
    return lambda *args: call(*args, *comm.inputs)


def _grid_edge(steps):
    return lambda: (pl.program_id(0) == 0, pl.program_id(0) == steps - 1)


def in_proj_fwd(x, g_pre, w_in, comm=None):
    T = x.shape[0]
    pw = w_in.shape[2]

    def body(x_ref, g_ref, w_hbm, ph_ref, sqkv_ref, u_ref, w_vmem, proj_s):
        _load_once(w_hbm, w_vmem)
        xh, _ = _rms(x_ref[...])
        u = (xh * g_ref[...]).astype(BF16)
        u_ref[...] = u
        for q in range(N_CHIPS):
            proj_s[:, pw * q:pw * (q + 1)] = jnp.dot(u, w_vmem[q], preferred_element_type=F32)
        ph_ref[...] = proj_s[:, :4 * HG_WIDTH]
        sqkv_ref[:, :SB_WIDTH] = (proj_s[:, 4 * HG_WIDTH:4 * HG_WIDTH + SB_WIDTH] * SB_SCALE).astype(BF16)
        sqkv_ref[:, SB_WIDTH:] = proj_s[:, 4 * HG_WIDTH + SB_WIDTH:].astype(BF16)

    return _pallas(
        body, comm=comm, edge=_grid_edge(T // _wide_tile(T)), name="in_proj_fwd", grid=(T // _wide_tile(T),),
        in_specs=[_wide_spec(D_MODEL), _full_spec((1, D_MODEL)), ANY_SPEC],
        out_specs=[_wide_spec(4 * HG_WIDTH), _wide_spec(3 * SB_WIDTH), _wide_spec(D_MODEL)],
        out_shape=[jax.ShapeDtypeStruct((T, 4 * HG_WIDTH), F32), jax.ShapeDtypeStruct((T, 3 * SB_WIDTH), BF16),
                   jax.ShapeDtypeStruct((T, D_MODEL), BF16)],
        scratch_shapes=[pltpu.VMEM(w_in.shape, BF16), pltpu.VMEM((_wide_tile(T), N_CHIPS * pw), F32)],
        compiler_params=_params(("arbitrary",)),
    )(x, g_pre, w_in)


def _hg_sum_matrix():
    C = HG_CHUNK
    t = np.arange(C)[:, None]
    j = np.arange(C)[None, :]
    mats = [j <= t, j > t]
    for h in HG_LEVELS:
        start = (t // (2 * h)) * (2 * h)
        upper = (t & h) != 0
        mats.append(np.where(upper, (j >= start + h) & (j <= t), (j > t) & (j <= start + h - 1)))
    return np.concatenate(mats, 0).astype(np.float32)


def _hg_level_masks():
    C = HG_CHUNK
    t = lax.broadcasted_iota(jnp.int32, (C, C), 0)
    s = lax.broadcasted_iota(jnp.int32, (C, C), 1)
    x = t ^ s
    masks = [t == s]
    for h in HG_LEVELS:
        masks.append((x >= h) & (x < 2 * h) & (t > s))
    return masks


def _hg_gates(hq, hf, gamma):
    lb = 1.0 / (1.0 + jnp.exp(gamma[1:2, :] - gamma[0:1, :]))
    sq = _sigmoid(hq)
    q = hq * sq
    sig = _sigmoid(hf)
    nsig = _sigmoid(-hf)
    f = lb + (1.0 - lb) * sig
    k = (1.0 - lb) * nsig
    g = jnp.log(f)
    return q, k, g, dict(lb=lb, sq=sq, sig=sig, nsig=nsig, f=f)


def _hg_head_decays(A, h):
    C, K = HG_CHUNK, HG_DK
    sl = slice(K * h, K * (h + 1))
    blocks = [A[C * r:C * (r + 1), sl] for r in range(2 + len(HG_LEVELS))]
    return blocks[0], blocks[1], [None] + blocks[2:]


def _hg_products(q, k, levels):
    return [_dot_nt(q, k)] + [_dot_nt(q * a, k * a) for a in levels[1:]]


def _hg_select(prods, masks):
    sc = jnp.where(masks[0], prods[0], 0.0)
    for p, m in zip(prods[1:], masks[1:]):
        sc = jnp.where(m, p, sc)
    return sc


def hgrn2_fwd(proj_h, gamma, comm=None):
    T = proj_h.shape[0]
    C, K, H, S = HG_CHUNK, HG_DK, HG_HEADS, HG_CHUNKS_PER_STEP
    n_steps = T // (S * C)
    msum = jnp.asarray(_hg_sum_matrix(), BF16)

    def body(hq_ref, hf_ref, hi_ref, gam_ref, msum_ref, o_ref, st_ref, st_s):
        _zero_first(st_s)
        q, k, g, _ = _hg_gates(hq_ref[...], hf_ref[...], gam_ref[...])
        v = hi_ref[...]
        masks = _hg_level_masks()
        parts = []
        for s in range(S):
            rows = slice(C * s, C * (s + 1))
            A = jnp.exp(_sum01_left(msum_ref[...], g[rows]))
            for h in range(H):
                sl = slice(K * h, K * (h + 1))
                ab, ar, levels = _hg_head_decays(A, h)
                parts.append(dict(s=s, h=h, rows=rows, sl=sl, ab=ab, ar=ar, levels=levels,
                                  q=q[rows, sl], k=k[rows, sl], v=v[rows, sl]))
        for pt in parts:
            pt["prods"] = _hg_products(pt["q"], pt["k"], pt["levels"])
            pt["grown"] = _dot_tn(pt["v"], pt["k"] * pt["ar"])
        for pt in parts:
            pt["sc"] = _hg_select(pt["prods"], masks)
        state = [st_s[h] for h in range(H)]
        for pt in parts:
            h, ab = pt["h"], pt["ab"]
            o_ref[pt["rows"], pt["sl"]] = _dot_nt(pt["q"] * ab, state[h]) + _dot(pt["sc"], pt["v"])
            state[h] = state[h] * ab[C - 1:C, :] + pt["grown"]
            st_ref[pt["s"], h] = state[h]
        for h in range(H):
            st_s[h] = state[h]

    blk = lambda col: pl.BlockSpec((S * C, HG_WIDTH), lambda c, col=col: (c, col))
    return _pallas(
        body, comm=comm, edge=_grid_edge(n_steps), name="hgrn2_fwd", grid=(n_steps,),
        in_specs=[blk(0), blk(1), blk(2), _full_spec((2, HG_WIDTH)), _full_spec(msum.shape)],
        out_specs=[blk(0), pl.BlockSpec((S, H, K, K), lambda c: (c, 0, 0, 0))],
        out_shape=[jax.ShapeDtypeStruct((T, HG_WIDTH), F32), jax.ShapeDtypeStruct((S * n_steps, H, K, K), F32)],
        scratch_shapes=[pltpu.VMEM((H, K, K), F32)],
        compiler_params=_params(("arbitrary",)),
    )(proj_h, proj_h, proj_h, gamma, msum)


def hgrn2_bwd(proj_h, gamma, states, do, comm=None):
    T = proj_h.shape[0]
    C, K, H, S = HG_CHUNK, HG_DK, HG_HEADS, HG_CHUNKS_PER_STEP
    n_steps = T // (S * C)
    n_sums = 2 + len(HG_LEVELS)
    msum = jnp.asarray(_hg_sum_matrix(), BF16)
    msum_t = jnp.asarray(_hg_sum_matrix().T, BF16)

    def body(hq_ref, hf_ref, hi_ref, do_ref, gam_ref, msum_ref, msum_t_ref, st_prev_ref, st_ref,
             dhq_ref, dhf_ref, dhi_ref, pack_ref, dst_s, dlb_s, dq_s, dk_s, de_s):
        step = pl.program_id(0)
        _zero_first(dst_s)
        _zero_first(dlb_s)
        _zero_first(pack_ref)
        hq = hq_ref[...]
        q, k, g, aux = _hg_gates(hq, hf_ref[...], gam_ref[...])
        v = hi_ref[...]
        do_all = do_ref[...]
        masks = _hg_level_masks()
        is_last_row = lax.broadcasted_iota(jnp.int32, (C, K), 0) == C - 1
        has_prev = (step < n_steps - 1).astype(F32)
        parts = []
        for s in reversed(range(S)):
            rows = slice(C * s, C * (s + 1))
            A = jnp.exp(_sum01_left(msum_ref[...], g[rows]))
            for h in range(H):
                sl = slice(K * h, K * (h + 1))
                ab, ar, levels = _hg_head_decays(A, h)
                st_in = st_prev_ref[0, h] * has_prev if s == 0 else st_ref[s - 1, h]
                parts.append(dict(s=s, h=h, rows=rows, sl=sl, ab=ab, ar=ar, levels=levels, st_in=st_in,
                                  q=q[rows, sl], k=k[rows, sl], v=v[rows, sl], do=do_all[rows, sl]))
        for pt in parts:
            pt["prods"] = _hg_products(pt["q"], pt["k"], pt["levels"])
            pt["da"] = _dot_nt(pt["do"], pt["v"])
            pt["t1"] = pt["ab"] * _dot(pt["do"], pt["st_in"])
            pt["dst_add"] = _dot_tn(pt["do"], pt["q"] * pt["ab"])
        dstate = [dst_s[h] for h in range(H)]
        for pt in parts:
            h = pt["h"]
            pt["dst_out"] = dstate[h]
            pt["t2"] = pt["ar"] * _dot(pt["v"], dstate[h])
            pt["dv_state"] = _dot_nt(pt["k"] * pt["ar"], dstate[h])
            dstate[h] = dstate[h] * pt["ab"][C - 1:C, :] + pt["dst_add"]
        for h in range(H):
            dst_s[h] = dstate[h]
        for pt in parts:
            pt["sc"] = _hg_select(pt["prods"], masks)
            pt["dam"] = [jnp.where(m, pt["da"], 0.0) for m in masks]
        for pt in parts:
            qh, kh = pt["q"], pt["k"]
            pt["dq_parts"] = [_dot(pt["dam"][0], kh)] + [
                a * _dot(dam, kh * a) for a, dam in zip(pt["levels"][1:], pt["dam"][1:])]
            pt["dk_parts"] = [_dot_tn(pt["dam"][0], qh)] + [
                a * _dot_tn(dam, qh * a) for a, dam in zip(pt["levels"][1:], pt["dam"][1:])]
            pt["dv_intra"] = _dot_tn(pt["sc"], pt["do"])
        for pt in parts:
            s, rows, sl, qh, kh, ab = pt["s"], pt["rows"], pt["sl"], pt["q"], pt["k"], pt["ab"]
            decayed = _colsum(pt["st_in"] * pt["dst_out"]) * ab[C - 1:C, :]
            de_s[s, 0:C, sl] = qh * pt["t1"] + jnp.where(is_last_row, decayed, 0.0)
            de_s[s, C:2 * C, sl] = kh * pt["t2"]
            dq = pt["t1"] + pt["dq_parts"][0]
            dk = pt["t2"] + pt["dk_parts"][0]
            for r, (t1, t2) in enumerate(zip(pt["dq_parts"][1:], pt["dk_parts"][1:])):
                dq = dq + t1
                dk = dk + t2
                de_s[s, C * (r + 2):C * (r + 3), sl] = qh * t1 + kh * t2
            dhi_ref[rows, sl] = pt["dv_intra"] + pt["dv_state"]
            dq_s[rows, sl] = dq
            dk_s[rows, sl] = dk
        dg = jnp.concatenate([_sum01_left(msum_t_ref[...], de_s[s]) for s in range(S)], axis=0)
        dk = dk_s[...]
        sq, lb = aux["sq"], aux["lb"]
        dhq_ref[...] = dq_s[...] * (sq * (1.0 + hq * (1.0 - sq)))
        common = dg / aux["f"] - dk
        dhf_ref[...] = (1.0 - lb) * aux["sig"] * aux["nsig"] * common
        dlb_s[...] += _colsum(aux["nsig"] * common)

        @pl.when(step == n_steps - 1)
        def _():
            dgam = lb * (1.0 - lb) * dlb_s[...]
            pack_ref[ROW_GAMMA:ROW_GAMMA + 1, :HG_WIDTH] = dgam
            pack_ref[ROW_GAMMA:ROW_GAMMA + 1, HG_WIDTH:] = -dgam

    last = n_steps - 1
    blk = lambda col: pl.BlockSpec((S * C, HG_WIDTH), lambda c, col=col: (last - c, col))
    return _pallas(
        body, comm=comm, edge=_grid_edge(n_steps), name="hgrn2_bwd", grid=(n_steps,),
        in_specs=[blk(0), blk(1), blk(2), blk(0), _full_spec((2, HG_WIDTH)), _full_spec(msum.shape),
                  _full_spec(msum_t.shape),
                  pl.BlockSpec((1, H, K, K), lambda c: (jnp.maximum(S * (last - c) - 1, 0), 0, 0, 0)),
                  pl.BlockSpec((S, H, K, K), lambda c: (last - c, 0, 0, 0))],
        out_specs=[blk(0), blk(0), blk(0), PACK_SPEC],
        out_shape=[jax.ShapeDtypeStruct((T, HG_WIDTH), F32)] * 3 + [jax.ShapeDtypeStruct((8, D_MODEL), F32)],
        scratch_shapes=[pltpu.VMEM((H, K, K), F32), pltpu.VMEM((1, HG_WIDTH), F32), pltpu.VMEM((S * C, HG_WIDTH), F32),
                        pltpu.VMEM((S * C, HG_WIDTH), F32), pltpu.VMEM((S, n_sums * C, HG_WIDTH), F32)],
        compiler_params=_params(("arbitrary",)),
    )(proj_h, proj_h, proj_h, do, gamma, msum, msum_t, states, states)


def _sb_sum_matrix(inclusive):
    B = SB_BLOCK
    j = np.arange(B)[:, None]
    s = np.arange(B)[None, :]
    tri = (j >= s) if inclusive else (j > s)
    once = np.concatenate([tri, np.ones((B, B), bool)], 1).astype(np.float32)
    return np.concatenate([once, once], 0)


def _sb_prefix_matrix(inclusive):
    B = SB_BLOCK
    j = np.arange(B)[:, None]
    s = np.arange(B)[None, :]
    tri = (j <= s) if inclusive else (j < s)
    once = np.concatenate([tri, np.ones((B, B), bool)], 1).astype(np.float32)
    return np.concatenate([once, once], 0)


def _sb_iotas():
    shape = (SB_BLOCK, SB_BLOCK)
    return lax.broadcasted_iota(jnp.int32, shape, 0), lax.broadcasted_iota(jnp.int32, shape, 1)


def _sb_heads(q, first):
    heads = []
    for g in range(SB_GROUP):
        qg = q[:, SB_BLOCK * g:SB_BLOCK * (g + 1)]
        zero = jnp.zeros_like(qg)
        heads += [(g, jnp.where(first, qg, zero)), (g, jnp.where(first, zero, qg))]
    return heads


def _lanes(x, g):
    return x[:, SB_BLOCK * g:SB_BLOCK * (g + 1)]


def sb_fwd(sqkv, comm=None):
    T = sqkv.shape[0]
    B = SB_BLOCK
    W = SB_GROUP * B
    groups = SB_WIDTH // W
    usum = jnp.asarray(_sb_sum_matrix(False), BF16)

    def body(q_ref, k_ref, v_ref, u_ref, o_ref, tl_ref, first_ref):
        p, i = pl.program_id(0), pl.program_id(1)
        row, lane = _sb_iotas()
        first = lane < SB_DH
        heads = _sb_heads(q_ref[...], first)
        u = u_ref[...]

        def more(loop):
            n, reachable, _ = loop
            return (SB_UNROLL * n <= i) & (reachable > 0)

        def step(loop):
            n, _, state = loop
            blocks = []
            for sub in range(SB_UNROLL):
                j = i - SB_UNROLL * n - sub
                off = pl.multiple_of(jnp.maximum(j, 0) * B, B)
                valid = ((lane + j * B) < (row + i * B)) & (j >= 0)
                blocks.append((k_ref[pl.ds(off, B), :], v_ref[pl.ds(off, B), :], valid))
            z = [[_dot_nt(qh, _lanes(kj, g)) for g, qh in heads] for kj, _, _ in blocks]
            lnb = [[jnp.where(valid, _neg_softplus(zz), 0.0) for zz in zs] for zs, (_, _, valid) in zip(z, blocks)]
            sums = [[_sum01_right(x, u) for x in xs] for xs in lnb]
            out = []
            for h, (carry, acc) in enumerate(state):
                for sub, (_, vj, valid) in enumerate(blocks):
                    expo = z[sub][h] + lnb[sub][h] + carry + sums[sub][h][:, :B]
                    acc = acc + _dot(jnp.where(valid, jnp.exp(expo), 0.0), _lanes(vj, heads[h][0]))
                    carry = carry + sums[sub][h][:, B:]
                out.append((carry, acc))
            state = tuple(out)
            worst = state[0][0]
            for carry, _ in state[1:]:
                worst = jnp.maximum(worst, carry)
            reachable = (jnp.max(worst) > SB_UNDERFLOW_LOG).astype(jnp.int32)
            return n + 1, reachable, state

        zero = jnp.zeros((B, B), F32)
        done, _, state = lax.while_loop(
            more, step, (jnp.int32(0), jnp.int32(1), tuple((zero, zero) for _ in heads)))
        for g in range(SB_GROUP):
            (tot0, acc0), (tot1, acc1) = state[2 * g], state[2 * g + 1]
            o_ref[:, B * g:B * (g + 1)] = jnp.where(first, acc0, acc1)
            tl_ref[:, B * g:B * (g + 1)] = jnp.where(first, tot0, tot1)
        first_ref[p, i] = jnp.maximum(i + 1 - SB_UNROLL * done, 0)

    def edge():
        p, i = pl.program_id(0), pl.program_id(1)
        return (p == 0) & (i == 0), (p == groups - 1) & (i == T // B - 1)

    return _pallas(
        body, comm=comm, edge=edge, name="sb_fwd", grid=(groups, T // B),
        in_specs=[pl.BlockSpec((B, W), lambda p, i: (i, p)),
                  pl.BlockSpec((T, W), lambda p, i: (0, groups + p)),
                  pl.BlockSpec((T, W), lambda p, i: (0, 2 * groups + p)),
                  pl.BlockSpec(usum.shape, lambda p, i: (0, 0))],
        out_specs=[pl.BlockSpec((B, W), lambda p, i: (i, p))] * 2 + [pl.BlockSpec(memory_space=pltpu.SMEM)],
        out_shape=[jax.ShapeDtypeStruct((T, SB_WIDTH), F32)] * 2 + [jax.ShapeDtypeStruct((groups, T // B), jnp.int32)],
        compiler_params=_params(("arbitrary", "arbitrary")),
    )(sqkv, sqkv, sqkv, usum)


def sb_bwd(sqkv, do, tl, first_block):
    T = sqkv.shape[0]
    B = SB_BLOCK
    W = SB_GROUP * B
    groups = SB_WIDTH // W
    upre = jnp.asarray(_sb_prefix_matrix(True), BF16)
    uexc = jnp.asarray(_sb_prefix_matrix(False), BF16)

    def body(q_ref, k_ref, v_ref, do_ref, tl_ref, up_ref, ue_ref, first_ref, dq_ref, dk_ref, dv_ref):
        p, i = pl.program_id(0), pl.program_id(1)

        @pl.when(i == 0)
        def _():
            dk_ref[...] = jnp.zeros(dk_ref.shape, F32)
            dv_ref[...] = jnp.zeros(dv_ref.shape, F32)

        row, lane = _sb_iotas()
        first = lane < SB_DH
        do = do_ref[...]
        tl_all = tl_ref[...]
        heads = []
        for (g, qh), at in zip(_sb_heads(q_ref[...], first), (0, B - 1) * SB_GROUP):
            dog = _lanes(do, g)
            keep = first if at == 0 else jnp.logical_not(first)
            heads.append((g, qh, jnp.where(keep, dog, jnp.zeros_like(dog)).astype(BF16),
                          _lanes(tl_all, g)[:, at:at + 1]))
        up = up_ref[...]
        ue = ue_ref[...]
        start = first_ref[p, i]

        def step(n, state):
            blocks = []
            for sub in range(SB_UNROLL):
                j = start + SB_UNROLL * n + sub
                off = pl.multiple_of(jnp.minimum(j, i) * B, B)
                valid = (lane + j * B) < (row + i * B)
                blocks.append((off, k_ref[pl.ds(off, B), :], v_ref[pl.ds(off, B), :], valid))
            combos = [(s, h) for s in range(SB_UNROLL) for h in range(len(heads))]
            z = {(s, h): _dot_nt(heads[h][1], _lanes(blocks[s][1], heads[h][0])) for s, h in combos}
            da = {(s, h): _dot_nt(heads[h][2], _lanes(blocks[s][2], heads[h][0])) for s, h in combos}
            lnb = {c: jnp.where(blocks[c[0]][3], _neg_softplus(z[c]), 0.0) for c in combos}
            lb = {c: z[c] + lnb[c] for c in combos}
            sums = {c: _sum01_right(lnb[c], up) for c in combos}
            a, w = {}, {}
            seen = [st[0] for st in state]
            for s, h in combos:
                expo = lb[s, h] + (heads[h][3] - seen[h] - sums[s, h][:, :B])
                a[s, h] = jnp.where(blocks[s][3], jnp.exp(expo), 0.0)
                w[s, h] = a[s, h] * da[s, h]
                seen[h] = seen[h] + sums[s, h][:, B:]
            wsums = {c: _sum01_right(w[c], ue) for c in combos}
            dz = {}
            seen_w = [st[1] for st in state]
            for s, h in combos:
                beta = jnp.exp(lb[s, h])
                before = seen_w[h] + wsums[s, h][:, :B]
                dz[s, h] = jnp.where(blocks[s][3], w[s, h] * (1.0 - beta) - before * beta, 0.0)
                seen_w[h] = seen_w[h] + wsums[s, h][:, B:]
            dq = [st[2] for st in state]
            for s, h in combos:
                dq[h] = dq[h] + _dot(dz[s, h], _lanes(blocks[s][1], heads[h][0]))
            for s in range(SB_UNROLL):
                off = blocks[s][0]
                for g in range(SB_GROUP):
                    h0, h1 = 2 * g, 2 * g + 1
                    dk_ref[pl.ds(off, B), B * g:B * (g + 1)] += (_dot_tn(dz[s, h0], heads[h0][1])
                                                                 + _dot_tn(dz[s, h1], heads[h1][1]))
                    dv_ref[pl.ds(off, B), B * g:B * (g + 1)] += (_dot_tn(a[s, h0], heads[h0][2])
                                                                 + _dot_tn(a[s, h1], heads[h1][2]))
            return tuple(zip(seen, seen_w, dq))

        zero = jnp.zeros((B, B), F32)
        trips = (i - start + SB_UNROLL) // SB_UNROLL
        state = lax.fori_loop(0, trips, step, tuple((zero, zero, zero) for _ in heads))
        for g in range(SB_GROUP):
            dq_ref[:, B * g:B * (g + 1)] = jnp.where(first, state[2 * g][2], state[2 * g + 1][2]) * SB_SCALE

    qblk = pl.BlockSpec((B, W), lambda p, i: (i, p))
    full = pl.BlockSpec((T, W), lambda p, i: (0, p))
    return pl.pallas_call(
        body, name="sb_bwd", grid=(groups, T // B),
        in_specs=[qblk, pl.BlockSpec((T, W), lambda p, i: (0, groups + p)),
                  pl.BlockSpec((T, W), lambda p, i: (0, 2 * groups + p)), qblk, qblk,
                  pl.BlockSpec(upre.shape, lambda p, i: (0, 0)), pl.BlockSpec(uexc.shape, lambda p, i: (0, 0)),
                  pl.BlockSpec(memory_space=pltpu.SMEM)],
        out_specs=[qblk, full, full],
        out_shape=[jax.ShapeDtypeStruct((T, SB_WIDTH), F32)] * 3,
        compiler_params=_params(("arbitrary", "arbitrary")),
    )(sqkv, sqkv, sqkv, do, tl, upre, uexc, first_block)


def _mixer_out(o_hg, hg, o_sb, g_hg, g_sb):
    n_hg, r_hg = _rms(o_hg)
    s_hg = _sigmoid(hg)
    n_sb, r_sb = _rms(o_sb)
    return dict(n_hg=n_hg, r_hg=r_hg, s_hg=s_hg, n_sb=n_sb, r_sb=r_sb,
                y_hg=n_hg * g_hg * (hg * s_hg), y_sb=n_sb * g_sb)


def mix_out_fwd(o_hg, proj_h, o_sb, x, norms, g_post, w_out, comm=None):
    T = x.shape[0]

    def body(ohg_ref, hg_ref, osb_ref, x_ref, nrm_ref, gp_ref, w_hbm, cat_ref, mix_ref, h1_ref, w_vmem):
        _load_once(w_hbm, w_vmem)
        nrm = nrm_ref[...]
        m = _mixer_out(ohg_ref[...], hg_ref[...], osb_ref[...], nrm[:, :HG_WIDTH], nrm[:, HG_WIDTH:])
        cat_ref[:, :HG_WIDTH] = m["y_hg"].astype(BF16)
        cat_ref[:, HG_WIDTH:] = m["y_sb"].astype(BF16)
        mix = jnp.dot(cat_ref[...], w_vmem[...], preferred_element_type=F32)
        mix_ref[...] = mix
        mh, _ = _rms(mix)
        h1_ref[...] = x_ref[...] + mh * gp_ref[...]

    return _pallas(
        body, comm=comm, edge=_grid_edge(T // _wide_tile(T)), name="mix_out_fwd", grid=(T // _wide_tile(T),),
        in_specs=[_wide_spec(HG_WIDTH), _wide_spec(HG_WIDTH, 3), _wide_spec(SB_WIDTH), _wide_spec(D_MODEL),
                  _full_spec((1, D_MODEL)), _full_spec((1, D_MODEL)), ANY_SPEC],
        out_specs=[_wide_spec(D_MODEL)] * 3,
        out_shape=[jax.ShapeDtypeStruct((T, D_MODEL), BF16), jax.ShapeDtypeStruct((T, D_MODEL), F32),
                   jax.ShapeDtypeStruct((T, D_MODEL), F32)],
        scratch_shapes=[pltpu.VMEM(w_out.shape, BF16)],
        compiler_params=_params(("arbitrary",)),
    )(o_hg, proj_h, o_sb, x, norms, g_post, w_out)


def ffn_fwd(h1, g_pre, g_post, w_gu, w_down):
    T = h1.shape[0]
    pw = w_gu.shape[2]

    def body(h1_ref, gpre_ref, gpost_ref, wgu_hbm, wd_hbm, u2_ref, gu_ref, act_ref, y_ref, h2_ref,
             wgu_vmem, wd_vmem, gu_s):
        _load_once(wgu_hbm, wgu_vmem)
        _load_once(wd_hbm, wd_vmem)
        h1v = h1_ref[...]
        hh, _ = _rms(h1v)
        u2 = (hh * gpre_ref[...]).astype(BF16)
        u2_ref[...] = u2
        for q in range(N_CHIPS):
            gu_s[:, pw * q:pw * (q + 1)] = jnp.dot(u2, wgu_vmem[q], preferred_element_type=F32)
        gu_ref[...] = gu_s[...].astype(BF16)
        gate = gu_s[:, :D_FF]
        act = (gate * _sigmoid(gate) * gu_s[:, D_FF:]).astype(BF16)
        act_ref[...] = act
        y = jnp.dot(act, wd_vmem[...], preferred_element_type=F32)
        y_ref[...] = y
        yh, _ = _rms(y)
        h2_ref[...] = h1v + yh * gpost_ref[...]

    return pl.pallas_call(
        body, name="ffn_fwd", grid=(T // ROW_TILE,),
        in_specs=[_row_spec(D_MODEL), _full_spec((1, D_MODEL)), _full_spec((1, D_MODEL)), ANY_SPEC, ANY_SPEC],
        out_specs=[_row_spec(D_MODEL), _row_spec(2 * D_FF), _row_spec(D_FF), _row_spec(D_MODEL), _row_spec(D_MODEL)],
        out_shape=[jax.ShapeDtypeStruct((T, D_MODEL), BF16), jax.ShapeDtypeStruct((T, 2 * D_FF), BF16),
                   jax.ShapeDtypeStruct((T, D_FF), BF16), jax.ShapeDtypeStruct((T, D_MODEL), F32),
                   jax.ShapeDtypeStruct((T, D_MODEL), F32)],
        scratch_shapes=[pltpu.VMEM(w_gu.shape, BF16), pltpu.VMEM(w_down.shape, BF16),
                        pltpu.VMEM((ROW_TILE, 2 * D_FF), F32)],
        compiler_params=_params(("arbitrary",)),
    )(h1, g_pre, g_post, w_gu, w_down)


def ple_loss(h2, p, target, w_ple, w_pg):
    T = h2.shape[0]
    pw = w_ple.shape[2]

    def body(h2_ref, p_ref, t_ref, wple_hbm, wpg_hbm, de_ref, ds_ref, dh2_ref, h2b_ref, pb_ref, pack_ref,
             wple_vmem, wpg_vmem, e_s):
        _load_once(wple_hbm, wple_vmem)
        _load_once(wpg_hbm, wpg_vmem)
        _zero_first(pack_ref)
        h2v = h2_ref[...]
        h2b = h2v.astype(BF16)
        h2b_ref[...] = h2b
        pb = p_ref[...].astype(BF16)
        pb_ref[...] = pb
        for q in range(N_CHIPS):
            e_s[:, pw * q:pw * (q + 1)] = jnp.dot(pb, wple_vmem[q], preferred_element_type=F32)
        e = e_s[...]
        sig = _sigmoid(jnp.dot(h2b, wpg_vmem[...], preferred_element_type=F32))
        err = h2v + e * sig - t_ref[...]
        part = 0.5 * jnp.sum(jnp.mean(err * err, axis=-1, keepdims=True), axis=0, keepdims=True)
        lane = lax.broadcasted_iota(jnp.int32, (1, D_MODEL), 1)
        pack_ref[ROW_LOSS:ROW_LOSS + 1, :] += jnp.where(lane == 0, part, 0.0)
        dh3 = err * (1.0 / D_MODEL)
        de_ref[...] = (dh3 * sig).astype(BF16)
        ds = (dh3 * e * sig * (1.0 - sig)).astype(BF16)
        ds_ref[...] = ds
        dh2_ref[...] = dh3 + _dot_nt(ds, wpg_vmem[...])

    return pl.pallas_call(
        body, name="ple_loss", grid=(T // _wide_tile(T),),
        in_specs=[_wide_spec(D_MODEL), _wide_spec(p.shape[1]), _wide_spec(D_MODEL), ANY_SPEC, ANY_SPEC],
        out_specs=[_wide_spec(D_MODEL), _wide_spec(D_MODEL), _wide_spec(D_MODEL), _wide_spec(D_MODEL),
                   _wide_spec(p.shape[1]), PACK_SPEC],
        out_shape=[jax.ShapeDtypeStruct((T, D_MODEL), BF16), jax.ShapeDtypeStruct((T, D_MODEL), BF16),
                   jax.ShapeDtypeStruct((T, D_MODEL), F32), jax.ShapeDtypeStruct((T, D_MODEL), BF16),
                   jax.ShapeDtypeStruct(p.shape, BF16), jax.ShapeDtypeStruct((8, D_MODEL), F32)],
        scratch_shapes=[pltpu.VMEM(w_ple.shape, BF16), pltpu.VMEM(w_pg.shape, BF16), pltpu.VMEM((_wide_tile(T), D_MODEL), F32)],
        compiler_params=_params(("arbitrary",)),
    )(h2, p, target, w_ple, w_pg)


def ffn_bwd(dh2, y, h1, gu, g_pre, g_post, w_gu, w_down):
    T = h1.shape[0]
    pw = w_gu.shape[2]

    def body(dh2_ref, y_ref, h1_ref, gu_ref, gpre_ref, gpost_ref, wgu_hbm, wd_hbm, dy_ref, dgu_ref, dh1_ref, pack_ref,
             wgu_vmem, wd_vmem):
        _load_once(wgu_hbm, wgu_vmem)
        _load_once(wd_hbm, wd_vmem)
        _zero_first(pack_ref)
        dh2v = dh2_ref[...]
        yh, ry = _rms(y_ref[...])
        dy, dw = _rms_bwd(dh2v, yh, ry, gpost_ref[...])
        pack_ref[ROW_FFN_POST:ROW_FFN_POST + 1, :] += _colsum(dw)
        dyb = dy.astype(BF16)
        dy_ref[...] = dyb
        dact = _dot_nt(dyb, wd_vmem[...])
        gate = gu_ref[:, :D_FF].astype(F32)
        up = gu_ref[:, D_FF:].astype(F32)
        sg = _sigmoid(gate)
        dgu_ref[:, :D_FF] = (dact * up * (sg * (1.0 + gate * (1.0 - sg)))).astype(BF16)
        dgu_ref[:, D_FF:] = (dact * gate * sg).astype(BF16)
        du2 = _dot_nt(dgu_ref[:, :pw], wgu_vmem[0])
        for q in range(1, N_CHIPS):
            du2 = du2 + _dot_nt(dgu_ref[:, pw * q:pw * (q + 1)], wgu_vmem[q])
        hh, rh = _rms(h1_ref[...])
        dh, dw = _rms_bwd(du2, hh, rh, gpre_ref[...])
        pack_ref[ROW_FFN_PRE:ROW_FFN_PRE + 1, :] += _colsum(dw)
        dh1_ref[...] = dh2v + dh

    return pl.pallas_call(
        body, name="ffn_bwd", grid=(T // ROW_TILE,),
        in_specs=[_row_spec(D_MODEL), _row_spec(D_MODEL), _row_spec(D_MODEL), _row_spec(2 * D_FF),
                  _full_spec((1, D_MODEL)), _full_spec((1, D_MODEL)), ANY_SPEC, ANY_SPEC],
        out_specs=[_row_spec(D_MODEL), _row_spec(2 * D_FF), _row_spec(D_MODEL), PACK_SPEC],
        out_shape=[jax.ShapeDtypeStruct((T, D_MODEL), BF16), jax.ShapeDtypeStruct((T, 2 * D_FF), BF16),
                   jax.ShapeDtypeStruct((T, D_MODEL), F32), jax.ShapeDtypeStruct((8, D_MODEL), F32)],
        scratch_shapes=[pltpu.VMEM(w_gu.shape, BF16), pltpu.VMEM(w_down.shape, BF16)],
        compiler_params=_params(("arbitrary",)),
    )(dh2, y, h1, gu, g_pre, g_post, w_gu, w_down)


def mix_out_bwd(dh1, mix, o_hg, proj_h, o_sb, norms, g_post, w_out, comm=None):
    T = dh1.shape[0]

    def body(dh1_ref, mix_ref, ohg_ref, hg_ref, osb_ref, nrm_ref, gp_ref, w_hbm, dmix_ref, dohg_ref, dhg_ref, dosb_ref,
             pack_ref, w_vmem):
        _load_once(w_hbm, w_vmem)
        _zero_first(pack_ref)
        mh, rm = _rms(mix_ref[...])
        dmix, dw = _rms_bwd(dh1_ref[...], mh, rm, gp_ref[...])
        pack_ref[ROW_ATTN_POST:ROW_ATTN_POST + 1, :] += _colsum(dw)
        dmb = dmix.astype(BF16)
        dmix_ref[...] = dmb
        dcat = _dot_nt(dmb, w_vmem[...])
        nrm = nrm_ref[...]
        g_hg, g_sb = nrm[:, :HG_WIDTH], nrm[:, HG_WIDTH:]
        hg = hg_ref[...]
        m = _mixer_out(ohg_ref[...], hg, osb_ref[...], g_hg, g_sb)
        d_hg = dcat[:, :HG_WIDTH]
        silu = hg * m["s_hg"]
        dhg_ref[...] = d_hg * (m["n_hg"] * g_hg) * (m["s_hg"] * (1.0 + hg * (1.0 - m["s_hg"])))
        dx, dw = _rms_bwd(d_hg * silu, m["n_hg"], m["r_hg"], g_hg)
        dohg_ref[...] = dx
        pack_ref[ROW_MIX_NORMS:ROW_MIX_NORMS + 1, :HG_WIDTH] += _colsum(dw)
        dx, dw = _rms_bwd(dcat[:, HG_WIDTH:], m["n_sb"], m["r_sb"], g_sb)
        dosb_ref[...] = dx
        pack_ref[ROW_MIX_NORMS:ROW_MIX_NORMS + 1, HG_WIDTH:] += _colsum(dw)

    return _pallas(
        body, comm=comm, edge=_grid_edge(T // _wide_tile(T)), name="mix_out_bwd", grid=(T // _wide_tile(T),),
        in_specs=[_wide_spec(D_MODEL), _wide_spec(D_MODEL), _wide_spec(HG_WIDTH), _wide_spec(HG_WIDTH, 3), _wide_spec(SB_WIDTH),
                  _full_spec((1, D_MODEL)), _full_spec((1, D_MODEL)), ANY_SPEC],
        out_specs=[_wide_spec(D_MODEL), _wide_spec(HG_WIDTH), _wide_spec(HG_WIDTH), _wide_spec(SB_WIDTH), PACK_SPEC],
        out_shape=[jax.ShapeDtypeStruct((T, D_MODEL), BF16), jax.ShapeDtypeStruct((T, HG_WIDTH), F32),
                   jax.ShapeDtypeStruct((T, HG_WIDTH), F32), jax.ShapeDtypeStruct((T, SB_WIDTH), F32),
                   jax.ShapeDtypeStruct((8, D_MODEL), F32)],
        scratch_shapes=[pltpu.VMEM(w_out.shape, BF16)],
        compiler_params=_params(("arbitrary",)),
    )(dh1, mix, o_hg, proj_h, o_sb, norms, g_post, w_out)


def in_proj_bwd(parts, x, dh1, g_pre, w_in, comm=None):
    T = x.shape[0]
    pw = w_in.shape[2]
    n_parts = len(parts)

    def body(*refs):
        part_refs = refs[:n_parts]
        x_ref, dh1_ref, g_ref, w_hbm, dproj_ref, dx_ref, pack_ref, w_vmem = refs[n_parts:]
        _load_once(w_hbm, w_vmem)
        _zero_first(pack_ref)
        for n, ref in enumerate(part_refs):
            dproj_ref[:, HG_WIDTH * n:HG_WIDTH * (n + 1)] = ref[...].astype(BF16)
        du = _dot_nt(dproj_ref[:, :pw], w_vmem[0])
        for q in range(1, N_CHIPS):
            du = du + _dot_nt(dproj_ref[:, pw * q:pw * (q + 1)], w_vmem[q])
        xh, r = _rms(x_ref[...])
        dx, dw = _rms_bwd(du, xh, r, g_ref[...])
        pack_ref[ROW_ATTN_PRE:ROW_ATTN_PRE + 1, :] += _colsum(dw)
        dx_ref[...] = dh1_ref[...] + dx

    return _pallas(
        body, comm=comm, edge=_grid_edge(T // _wide_tile(T)), name="in_proj_bwd", grid=(T // _wide_tile(T),),
        in_specs=[_wide_spec(HG_WIDTH)] * n_parts + [_wide_spec(D_MODEL), _wide_spec(D_MODEL), _full_spec((1, D_MODEL)), ANY_SPEC],
        out_specs=[_wide_spec(n_parts * HG_WIDTH), _wide_spec(D_MODEL), PACK_SPEC],
        out_shape=[jax.ShapeDtypeStruct((T, n_parts * HG_WIDTH), BF16), jax.ShapeDtypeStruct((T, D_MODEL), F32),
                   jax.ShapeDtypeStruct((8, D_MODEL), F32)],
        scratch_shapes=[pltpu.VMEM(w_in.shape, BF16)],
        compiler_params=_params(("arbitrary",)),
    )(*parts, x, dh1, g_pre, w_in)


def weight_grad(a, g, name, *, tm, tn, tk=512, col_pieces=False, comm=None):
    T, M = a.shape
    N = g.shape[1]
    tk = min(tk, T)
    steps = T // tk

    def body(a_ref, g_ref, o_ref):
        @pl.when(pl.program_id(2) == 0)
        def _():
            o_ref[...] = jnp.zeros(o_ref.shape, F32)

        o_ref[...] += _dot_tn(a_ref[...], g_ref[...]).reshape(o_ref.shape)

    if col_pieces:
        out_shape = jax.ShapeDtypeStruct((N // tn, M, tn), F32)
        out_spec = pl.BlockSpec((1, tm, tn), lambda i, j, k: (j, i, 0))
    else:
        out_shape = jax.ShapeDtypeStruct((M, N), F32)
        out_spec = pl.BlockSpec((tm, tn), lambda i, j, k: (i, j))
    grid = (M // tm, N // tn, steps)

    def edge():
        at = [pl.program_id(d) for d in range(3)]
        return ((at[0] == 0) & (at[1] == 0) & (at[2] == 0),
                (at[0] == grid[0] - 1) & (at[1] == grid[1] - 1) & (at[2] == grid[2] - 1))

    return _pallas(
        body, comm=comm, edge=edge, name=name, grid=grid,
        in_specs=[pl.BlockSpec((tk, tm), lambda i, j, k: (k, i)), pl.BlockSpec((tk, tn), lambda i, j, k: (k, j))],
        out_specs=[out_spec], out_shape=[out_shape],
        compiler_params=_params(("arbitrary", "arbitrary", "arbitrary")),
    )(a, g)


def _place():
    x, y, c = lax.axis_index("x"), lax.axis_index("y"), lax.axis_index("c")
    chips = [(1 - x, y), (x, 1 - y), (1 - x, 1 - y)]
    return x, y, c, chips


def _chip_index(cx, cy):
    return 2 * cx + cy


def _own_slot(piece, slots):
    me = _chip_index(lax.axis_index("x"), lax.axis_index("y"))
    landing = lax.empty((slots,) + piece.shape[1:], piece.dtype)
    return lax.dynamic_update_slice(landing, piece, (me,) + (0,) * (piece.ndim - 1))


def _rcopy(src, dst, send_sem, recv_sem, device):
    return pltpu.make_async_remote_copy(src_ref=src, dst_ref=dst, send_sem=send_sem, recv_sem=recv_sem,
                                        device_id=device, device_id_type=MESH)


def gather_weights(shards):
    n = len(shards)

    def body(*refs):
        ins, outs = refs[:n], refs[2 * n:3 * n]
        send_sems, recv_sems = refs[3 * n:]
        x, y, c, chips = _place()
        me = _chip_index(x, y)
        sibling = (x, y, 1 - c)

        def rows(w, core):
            half = ins[w].shape[0] // 2
            return pl.ds(core * half, half)

        sends = []
        for w in range(n):
            for j, chip in enumerate(chips):
                sends.append(_rcopy(ins[w].at[rows(w, c)], outs[w].at[me, rows(w, c)],
                                    send_sems.at[6 * w + j], recv_sems.at[6 * w + j], (*chip, c)))
        for cp in sends:
            cp.start()
        passed = []
        for w in range(n):
            for j, chip in enumerate(chips):
                block = outs[w].at[_chip_index(*chip), rows(w, c)]
                _rcopy(block, block, send_sems.at[6 * w + j], recv_sems.at[6 * w + j], (*chip, c)).wait_recv()
                cp = _rcopy(block, block, send_sems.at[6 * w + 3 + j], recv_sems.at[6 * w + 3 + j], sibling)
                cp.start()
                passed.append(cp)
        for w in range(n):
            for j, chip in enumerate(chips):
                block = outs[w].at[_chip_index(*chip), rows(w, 1 - c)]
                _rcopy(block, block, send_sems.at[6 * w + 3 + j], recv_sems.at[6 * w + 3 + j], sibling).wait_recv()
        for cp in sends + passed:
            cp.wait_send()

    filled = [_own_slot(s[None], N_CHIPS) for s in shards]
    return pl.pallas_call(
        body, name="gather_weights",
        in_specs=[ANY_SPEC] * (2 * n), out_specs=[ANY_SPEC] * n,
        out_shape=[jax.ShapeDtypeStruct(f.shape, f.dtype) for f in filled],
        input_output_aliases={n + w: w for w in range(n)},
        scratch_shapes=[pltpu.SemaphoreType.DMA((6 * n,)), pltpu.SemaphoreType.DMA((6 * n,))],
    )(*shards, *filled)


def _run_comm(comm, name):
    c_in, c_out = len(comm.inputs), len(comm.out_shape)

    def body(*refs):
        parts = refs[:c_in], refs[c_in:c_in + c_out], refs[c_in + c_out:]
        comm.start(*parts)
        comm.finish(*parts)

    return pl.pallas_call(
        body, name=name, in_specs=[ANY_SPEC] * c_in, out_specs=[ANY_SPEC] * c_out, out_shape=comm.out_shape,
        scratch_shapes=comm.scratch, input_output_aliases=comm.aliases)(*comm.inputs)


def _both(first, second):
    n_in, n_out, n_scr = len(first.inputs), len(first.out_shape), len(first.scratch)

    def split(ins, outs, scr):
        return (ins[:n_in], outs[:n_out], scr[:n_scr]), (ins[n_in:], outs[n_out:], scr[n_scr:])

    def start(*refs):
        a, b = split(*refs)
        first.start(*a)
        second.start(*b)

    def finish(*refs):
        a, b = split(*refs)
        first.finish(*a)
        second.finish(*b)

    aliases = dict(first.aliases)
    aliases.update({n_in + i: n_out + o for i, o in second.aliases.items()})
    return Comm(first.inputs + second.inputs, first.out_shape + second.out_shape, aliases,
                first.scratch + second.scratch, start, finish)


def _dma_sems(count):
    return [pltpu.SemaphoreType.DMA((count,)), pltpu.SemaphoreType.DMA((count,))]


def gather_over_ici(shards):
    n = len(shards)

    def copies(ins, outs, sems):
        send_sems, recv_sems = sems
        x, y, c, chips = _place()
        me = _chip_index(x, y)
        pairs = []
        for w in range(n):
            half = shards[w].shape[0] // 2
            rows = pl.ds(c * half, half)
            for j, chip in enumerate(chips):
                k = 3 * w + j
                landed = outs[w].at[_chip_index(*chip), rows]
                pairs.append((_rcopy(ins[w].at[rows], outs[w].at[me, rows], send_sems.at[k], recv_sems.at[k], (*chip, c)),
                              _rcopy(landed, landed, send_sems.at[k], recv_sems.at[k], (*chip, c))))
        return pairs

    def start(*refs):
        for send, _ in copies(*refs):
            send.start()

    def finish(*refs):
        pairs = copies(*refs)
        for _, landed in pairs:
            landed.wait_recv()
        for send, _ in pairs:
            send.wait_send()

    filled = [_own_slot(s[None], N_CHIPS) for s in shards]
    return Comm(list(shards) + filled, [jax.ShapeDtypeStruct(f.shape, f.dtype) for f in filled],
                {n + w: w for w in range(n)}, _dma_sems(3 * n), start, finish)


def gather_over_d2d(landed):
    n = len(landed)

    def copies(ins, outs, sems):
        send_sems, recv_sems = sems
        x, y, c, chips = _place()
        sibling = (x, y, 1 - c)
        pairs = []
        for w in range(n):
            half = landed[w].shape[1] // 2
            for j, chip in enumerate(chips):
                k = 3 * w + j
                mine = outs[w].at[_chip_index(*chip), pl.ds(c * half, half)]
                theirs = outs[w].at[_chip_index(*chip), pl.ds((1 - c) * half, half)]
                pairs.append((_rcopy(mine, mine, send_sems.at[k], recv_sems.at[k], sibling),
                              _rcopy(theirs, theirs, send_sems.at[k], recv_sems.at[k], sibling)))
        return pairs

    def start(*refs):
        for send, _ in copies(*refs):
            send.start()

    def finish(*refs):
        pairs = copies(*refs)
        for _, arrived in pairs:
            arrived.wait_recv()
        for send, _ in pairs:
            send.wait_send()

    return Comm(list(landed), [jax.ShapeDtypeStruct(a.shape, a.dtype) for a in landed], {w: w for w in range(n)},
                _dma_sems(3 * n), start, finish)


def core_halves(grads):
    n = len(grads)

    def copies(ins, outs, sems):
        send_sems, recv_sems = sems
        x, y, c, _ = _place()
        out = []
        for w in range(n):
            half = grads[w].shape[1] // 2
            out.append(_rcopy(ins[w].at[:, pl.ds((1 - c) * half, half), :], outs[w],
                              send_sems.at[w], recv_sems.at[w], (x, y, 1 - c)))
        return out

    def start(*refs):
        for cp in copies(*refs):
            cp.start()

    def finish(*refs):
        for cp in copies(*refs):
            cp.wait()

    return Comm(list(grads), [jax.ShapeDtypeStruct((g.shape[0], g.shape[1] // 2, g.shape[2]), g.dtype) for g in grads],
                {}, _dma_sems(n), start, finish)


def chip_partials(partials):
    n = len(partials)

    def copies(ins, outs, sems):
        send_sems, recv_sems = sems
        x, y, c, chips = _place()
        me = _chip_index(x, y)
        pairs = []
        for w in range(n):
            for j, chip in enumerate(chips):
                k = 3 * w + j
                landed = outs[w].at[_chip_index(*chip)]
                pairs.append((_rcopy(ins[w].at[_chip_index(*chip)], outs[w].at[me], send_sems.at[k], recv_sems.at[k],
                                     (*chip, c)),
                              _rcopy(landed, landed, send_sems.at[k], recv_sems.at[k], (*chip, c))))
        return pairs

    def start(*refs):
        for send, _ in copies(*refs):
            send.start()

    def finish(*refs):
        pairs = copies(*refs)
        for _, landed in pairs:
            landed.wait_recv()
        for send, _ in pairs:
            send.wait_send()

    me = _chip_index(lax.axis_index("x"), lax.axis_index("y"))
    filled = [_own_slot(lax.dynamic_index_in_dim(p, me, 0, keepdims=True), N_CHIPS) for p in partials]
    return Comm(list(partials) + filled, [jax.ShapeDtypeStruct(p.shape, p.dtype) for p in partials],
                {n + w: w for w in range(n)}, _dma_sems(3 * n), start, finish)


def join_core_halves(grads):
    n = len(grads)

    def body(*refs):
        outs = refs[n:2 * n]
        send_sems, recv_sems = refs[2 * n:]
        x, y, c, _ = _place()
        sibling = (x, y, 1 - c)
        copies = []
        for w in range(n):
            half = outs[w].shape[0] // 2
            mine = outs[w].at[pl.ds(c * half, half), :]
            copies.append(_rcopy(mine, mine, send_sems.at[w], recv_sems.at[w], sibling))
        for cp in copies:
            cp.start()
        for w in range(n):
            half = outs[w].shape[0] // 2
            theirs = outs[w].at[pl.ds((1 - c) * half, half), :]
            _rcopy(theirs, theirs, send_sems.at[w], recv_sems.at[w], sibling).wait_recv()
        for cp in copies:
            cp.wait_send()

    return pl.pallas_call(
        body, name="join_core_halves",
        in_specs=[ANY_SPEC] * n, out_specs=[ANY_SPEC] * n,
        out_shape=[jax.ShapeDtypeStruct(g.shape, g.dtype) for g in grads],
        input_output_aliases={w: w for w in range(n)},
        scratch_shapes=[pltpu.SemaphoreType.DMA((n,)), pltpu.SemaphoreType.DMA((n,))],
    )(*grads)


def _elementwise_rows(rows, cap=512):
    for t in range(min(rows, cap), 0, -8):
        if rows % t == 0 and t % 16 == 0:
            return t
    return rows


def add_core_halves(grad, got, core, name):
    _, rows, cols = got.shape
    tr = _elementwise_rows(rows)
    nt = rows // tr

    def body(core_ref, a_ref, b_ref, o_ref):
        o_ref[...] = (a_ref[...] + b_ref[...]).astype(BF16)

    spec = pl.BlockSpec((1, tr, cols), lambda q, i, core_ref: (q, i, 0))
    own = pl.BlockSpec((1, tr, cols), lambda q, i, core_ref: (q, core_ref[0] * nt + i, 0))
    return pl.pallas_call(
        body, name=name,
        grid_spec=pltpu.PrefetchScalarGridSpec(num_scalar_prefetch=1, grid=(N_CHIPS, nt), in_specs=[own, spec],
                                               out_specs=spec),
        out_shape=jax.ShapeDtypeStruct(got.shape, BF16),
        compiler_params=_params(("arbitrary", "arbitrary")),
    )(core, grad, got)


def add_chip_partials(parts, core, name):
    _, rows, cols = parts.shape
    tr = _elementwise_rows(rows)
    nt = rows // tr

    def body(core_ref, p_ref, o_ref):
        acc = p_ref[0].astype(F32)
        for q in range(1, N_CHIPS):
            acc = acc + p_ref[q].astype(F32)
        o_ref[...] = acc

    return pl.pallas_call(
        body, name=name,
        grid_spec=pltpu.PrefetchScalarGridSpec(
            num_scalar_prefetch=1, grid=(nt,),
            in_specs=[pl.BlockSpec((N_CHIPS, tr, cols), lambda i, core_ref: (0, i, 0))],
            out_specs=pl.BlockSpec((tr, cols), lambda i, core_ref: (core_ref[0] * nt + i, 0))),
        out_shape=jax.ShapeDtypeStruct((2 * rows, cols), F32),
        compiler_params=_params(("arbitrary",)),
    )(core, parts)


def _adamw_math(w, g, m, v):
    m = ADAM_B1 * m + (1.0 - ADAM_B1) * g
    v = ADAM_B2 * v + (1.0 - ADAM_B2) * (g * g)
    m_hat = m / (1.0 - ADAM_B1 ** ADAM_STEP)
    v_hat = v / (1.0 - ADAM_B2 ** ADAM_STEP)
    delta = -ADAM_LR * (m_hat / (jnp.sqrt(v_hat) + ADAM_EPS) + ADAM_WD * w)
    return delta, m, v


def adamw(w, g, m, v, name):
    rows, cols = w.shape
    tr = _elementwise_rows(rows, 256)

    def body(w_ref, g_ref, m_ref, v_ref, d_ref, nm_ref, nv_ref):
        d, nm, nv = _adamw_math(w_ref[...], g_ref[...], m_ref[...], v_ref[...])
        d_ref[...] = d
        nm_ref[...] = nm
        nv_ref[...] = nv

    spec = pl.BlockSpec((tr, cols), lambda i: (i, 0))
    return pl.pallas_call(
        body, name=name, grid=(rows // tr,), in_specs=[spec] * 4, out_specs=[spec] * 3,
        out_shape=[jax.ShapeDtypeStruct((rows, cols), F32)] * 3,
        compiler_params=_params(("arbitrary",)),
    )(w, g, m, v)


def reduce_small(packs, w, m, v):
    n = len(packs)
    n_dev = 8
    flips = [(fx, fy, fc) for fx in (0, 1) for fy in (0, 1) for fc in (0, 1)][1:]

    def body(*refs):
        pack_refs = refs[:n]
        w_ref, m_ref, v_ref, g_out, d_out, m_out, v_out, mine, slots, send_sems, recv_sems = refs[n:]
        x, y, c, _ = _place()
        me = 4 * x + 2 * y + c
        acc = pack_refs[0][...]
        for ref in pack_refs[1:]:
            acc = acc + ref[...]
        mine[...] = acc
        sends = []
        for k, (fx, fy, fc) in enumerate(flips):
            peer = (x ^ fx, y ^ fy, c ^ fc)
            sends.append(_rcopy(mine, slots.at[me], send_sems.at[k], recv_sems.at[me], peer))
        for cp in sends:
            cp.start()
        slots[me] = acc
        for fx, fy, fc in flips:
            src = 4 * (x ^ fx) + 2 * (y ^ fy) + (c ^ fc)
            _rcopy(mine, slots.at[src], send_sems.at[0], recv_sems.at[src], (x, y, c)).wait_recv()
        for cp in sends:
            cp.wait_send()
        total = slots[0]
        for d in range(1, n_dev):
            total = total + slots[d]
        g_out[...] = total
        d, nm, nv = _adamw_math(w_ref[...], total, m_ref[...], v_ref[...])
        d_out[...] = d
        m_out[...] = nm
        v_out[...] = nv

    vm = pl.BlockSpec(memory_space=pltpu.VMEM)
    return pl.pallas_call(
        body, name="reduce_small",
        in_specs=[vm] * (n + 3), out_specs=[vm] * 4,
        out_shape=[jax.ShapeDtypeStruct((8, D_MODEL), F32)] * 4,
        scratch_shapes=[pltpu.VMEM((8, D_MODEL), F32), pltpu.VMEM((n_dev, 8, D_MODEL), F32),
                        pltpu.SemaphoreType.DMA((len(flips),)), pltpu.SemaphoreType.DMA((n_dev,))],
    )(*packs, w, m, v)


def _column_pieces(g):
    return g.reshape(g.shape[0], N_CHIPS, g.shape[1] // N_CHIPS).transpose(1, 0, 2)


def _pack_small(attn_pre, gamma, hg_norm, sb_norm, attn_post, ffn_pre, ffn_post):
    rows = [attn_pre, gamma.reshape(1, D_MODEL), jnp.concatenate([hg_norm, sb_norm], axis=1), attn_post, ffn_pre, ffn_post,
            jnp.zeros((2, D_MODEL), F32)]
    return jnp.concatenate(rows, axis=0)


def _unpack_small(pack):
    return (pack[ROW_ATTN_PRE:ROW_ATTN_PRE + 1], pack[ROW_GAMMA].reshape(2, HG_WIDTH),
            pack[ROW_MIX_NORMS:ROW_MIX_NORMS + 1, :HG_WIDTH], pack[ROW_MIX_NORMS:ROW_MIX_NORMS + 1, HG_WIDTH:],
            pack[ROW_ATTN_POST:ROW_ATTN_POST + 1], pack[ROW_FFN_PRE:ROW_FFN_PRE + 1], pack[ROW_FFN_POST:ROW_FFN_POST + 1])


def kernel(x, p, attn_pre_norm, w_in, hg_lower_gamma, hg_out_norm, sb_out_norm, w_out, attn_post_norm, ffn_pre_norm, w_gate_up, w_down, ffn_post_norm, ple_proj, ple_gate, loss_target, m_attn_pre_norm, m_w_in, m_hg_lower_gamma, m_hg_out_norm, m_sb_out_norm, m_w_out, m_attn_post_norm, m_ffn_pre_norm, m_w_gate_up, m_w_down, m_ffn_post_norm, m_ple_proj, m_ple_gate, v_attn_pre_norm, v_w_in, v_hg_lower_gamma, v_hg_out_norm, v_sb_out_norm, v_w_out, v_attn_post_norm, v_ffn_pre_norm, v_w_gate_up, v_w_down, v_ffn_post_norm, v_ple_proj, v_ple_gate):
    x2 = x[0]
    p2 = p[0, 0]
    target = loss_target[0]
    big = dict(w_in=(w_in, m_w_in, v_w_in), w_out=(w_out, m_w_out, v_w_out), w_gate_up=(w_gate_up, m_w_gate_up, v_w_gate_up),
               w_down=(w_down, m_w_down, v_w_down), ple_proj=(ple_proj, m_ple_proj, v_ple_proj),
               ple_gate=(ple_gate, m_ple_gate, v_ple_gate))
    names = list(big)
    big = {k: tuple(a[0] for a in t) for k, t in big.items()}

    shard16 = {k: big[k][0].astype(BF16) for k in names}
    w_in_full, = gather_weights([shard16["w_in"]])
    mix_norms = jnp.concatenate([hg_out_norm, sb_out_norm], axis=1)
    small_ones = ["w_out", "ple_proj", "ple_gate"]

    proj_h, sqkv, u1, *landed_small = in_proj_fwd(
        x2, attn_pre_norm, w_in_full, comm=gather_over_ici([shard16[k] for k in small_ones]))
    o_sb, sb_totals, sb_first, landed_gu = sb_fwd(sqkv, comm=gather_over_ici([shard16["w_gate_up"]]))
    o_hg, states, landed_down, *full_small = hgrn2_fwd(
        proj_h, hg_lower_gamma, comm=_both(gather_over_ici([shard16["w_down"]]), gather_over_d2d(landed_small)))
    full = dict(zip(small_ones, full_small), w_in=w_in_full)
    w_out_full = full["w_out"].reshape(D_MODEL, D_MODEL)
    w_pg_full = full["ple_gate"].reshape(D_MODEL, D_MODEL)
    cat, mix, h1, full["w_gate_up"], full["w_down"] = mix_out_fwd(
        o_hg, proj_h, o_sb, x2, mix_norms, attn_post_norm, w_out_full, comm=gather_over_d2d([landed_gu, landed_down]))
    w_down_full = full["w_down"].reshape(D_FF, D_MODEL)
    u2, gu, act, y, h2 = ffn_fwd(h1, ffn_pre_norm, ffn_post_norm, full["w_gate_up"], w_down_full)

    core = lax.axis_index("c").astype(jnp.int32).reshape(1)
    de, ds, dh2, h2b, pb, pack_loss = ple_loss(h2, p2, target, full["ple_proj"], w_pg_full)
    dy, dgu, dh1, pack_ffn = ffn_bwd(dh2, y, h1, gu, ffn_pre_norm, ffn_post_norm, full["w_gate_up"], w_down_full)
    local = {}
    local["w_gate_up"], = weight_grad(u2, dgu, "grad_w_gate_up", tm=D_MODEL, tn=full["w_gate_up"].shape[2], tk=1024,
                                      col_pieces=True)
    grad_down, got_gu = weight_grad(act, dy, "grad_w_down", tm=D_FF // 2, tn=D_MODEL, tk=1024,
                                    comm=core_halves([local["w_gate_up"]]))
    local["w_down"] = grad_down.reshape(full["w_down"].shape)
    local["ple_proj"] = _column_pieces(weight_grad(pb, de, "grad_ple_proj", tm=pb.shape[1], tn=D_MODEL, tk=1024)[0])
    grad_pg, got_down = weight_grad(h2b, ds, "grad_ple_gate", tm=D_MODEL, tn=D_MODEL, tk=1024,
                                    comm=core_halves([local["w_down"]]))
    local["ple_gate"] = grad_pg.reshape(full["ple_gate"].shape)
    early = list(local)
    dmix, do_hg, dhg, do_sb, pack_mix, *got_ple = mix_out_bwd(
        dh1, mix, o_hg, proj_h, o_sb, mix_norms, attn_post_norm, w_out_full,
        comm=core_halves([local["ple_proj"], local["ple_gate"]]))
    got = [got_gu, got_down] + got_ple
    partial = [add_core_halves(local[k], g, core, "add_core_halves_" + k) for k, g in zip(early, got)]
    local["w_out"] = weight_grad(cat, dmix, "grad_w_out", tm=D_MODEL, tn=D_MODEL, tk=1024)[0].reshape(full["w_out"].shape)
    dsq, dsk, dsv = sb_bwd(sqkv, do_sb, sb_totals, sb_first)
    dhq, dhf, dhi, pack_hg, *by_source, got_out = hgrn2_bwd(
        proj_h, hg_lower_gamma, states, do_hg, comm=_both(chip_partials(partial[:2]), core_halves([local["w_out"]])))
    early.append("w_out")
    partial.append(add_core_halves(local["w_out"], got_out, core, "add_core_halves_w_out"))
    dproj, grad_x, pack_in = in_proj_bwd([dhq, dhf, dhi, dhg, dsq, dsk, dsv], x2, dh1, attn_pre_norm, full["w_in"])

    late = ["w_in"]
    local["w_in"], *more = weight_grad(u1, dproj, "grad_w_in", tm=D_MODEL, tn=full["w_in"].shape[2], tk=1024,
                                       col_pieces=True, comm=chip_partials(partial[2:]))
    halves = {k: add_chip_partials(s, core, "add_chip_partials_" + k) for k, s in zip(early, by_source + more)}
    got = _run_comm(core_halves([local[k] for k in late]), "exchange_core_halves")
    partial = [add_core_halves(local[k], g, core, "add_core_halves_" + k) for k, g in zip(late, got)]
    by_source = _run_comm(chip_partials(partial), "exchange_chip_partials")
    halves.update({k: add_chip_partials(s, core, "add_chip_partials_" + k) for k, s in zip(late, by_source)})
    grads = dict(zip(names, join_core_halves([halves[k] for k in names])))

    upd = {k: adamw(big[k][0], grads[k], big[k][1], big[k][2], "adamw_" + k) for k in names}

    small = reduce_small(
        [pack_loss, pack_ffn, pack_mix, pack_hg, pack_in],
        _pack_small(attn_pre_norm, hg_lower_gamma, hg_out_norm, sb_out_norm, attn_post_norm, ffn_pre_norm, ffn_post_norm),
        _pack_small(m_attn_pre_norm, m_hg_lower_gamma, m_hg_out_norm, m_sb_out_norm, m_attn_post_norm, m_ffn_pre_norm, m_ffn_post_norm),
        _pack_small(v_attn_pre_norm, v_hg_lower_gamma, v_hg_out_norm, v_sb_out_norm, v_attn_post_norm, v_ffn_pre_norm, v_ffn_post_norm),
    )
    loss = small[0][ROW_LOSS, 0]
    s_grad, s_delta, s_m, s_v = (_unpack_small(t) for t in small)

    def ordered(small_vals, big_vals):
        a_pre, gam, hg_n, sb_n, a_post, f_pre, f_post = small_vals
        b = {k: big_vals[k][None] for k in names}
        return (a_pre, b["w_in"], gam, hg_n, sb_n, b["w_out"], a_post, f_pre, b["w_gate_up"], b["w_down"], f_post,
                b["ple_proj"], b["ple_gate"])

    return (loss, grad_x[None],
            *ordered(s_grad, grads),
            *ordered(s_delta, {k: upd[k][0] for k in names}),
            *ordered(s_m, {k: upd[k][1] for k in names}),
            *ordered(s_v, {k: upd[k][2] for k in names}))
```

```python
from typing import Callable, NamedTuple

import numpy as np
import jax
import jax.numpy as jnp
from jax import lax
from jax.experimental import pallas as pl
from jax.experimental.pallas import tpu as pltpu

F32 = jnp.float32
BF16 = jnp.bfloat16
MESH = pl.DeviceIdType.MESH

RMS_EPS = 1e-6
D_MODEL = 1024
HG_WIDTH = 512
HG_HEADS = 4
HG_DK = 128
HG_CHUNK = 64
HG_LEVELS = (32, 16, 8, 4, 2, 1)
HG_CHUNKS_PER_STEP = 2
SB_WIDTH = 512
SB_BLOCK = 128
SB_DH = 64
SB_SCALE = SB_DH ** -0.5
SB_UNDERFLOW_LOG = -87.5
SB_UNROLL = 2
SB_GROUP = 4
D_FF = 2816
N_CHIPS = 4
ROW_TILE = 256
WIDE_ROW_TILE = 512
V7X_VMEM_LIMIT = 56 * 1024 * 1024

ADAM_LR = 0.001
ADAM_B1 = 0.9
ADAM_B2 = 0.999
ADAM_EPS = 1e-08
ADAM_WD = 0.01
ADAM_STEP = 10

ROW_ATTN_PRE, ROW_GAMMA, ROW_MIX_NORMS, ROW_ATTN_POST, ROW_FFN_PRE, ROW_FFN_POST, ROW_LOSS = range(7)


def _params(sem=None, vmem=V7X_VMEM_LIMIT):
    return pltpu.CompilerParams(dimension_semantics=sem, vmem_limit_bytes=vmem)


def _dot(a, b):
    return jnp.dot(a.astype(BF16), b.astype(BF16), preferred_element_type=F32)


def _dot_nt(a, b):
    return lax.dot_general(a.astype(BF16), b.astype(BF16), (((1,), (1,)), ((), ())), preferred_element_type=F32)


def _dot_tn(a, b):
    return lax.dot_general(a.astype(BF16), b.astype(BF16), (((0,), (0,)), ((), ())), preferred_element_type=F32)


def _split(x):
    hi = x.astype(BF16)
    lo = (x - hi.astype(F32)).astype(BF16)
    return hi, lo


def _sum01_left(m01, x):
    hi, lo = _split(x)
    return jnp.dot(m01, hi, preferred_element_type=F32) + jnp.dot(m01, lo, preferred_element_type=F32)


def _sum01_right(x, m01_twice):
    hi, lo = _split(x)
    return jnp.dot(jnp.concatenate([hi, lo], axis=1), m01_twice, preferred_element_type=F32)


def _rms(x):
    r = lax.rsqrt(jnp.mean(x * x, axis=-1, keepdims=True) + RMS_EPS)
    return x * r, r


def _rms_bwd(dy, xhat, r, w):
    dxh = dy * w
    dx = r * (dxh - xhat * jnp.mean(dxh * xhat, axis=-1, keepdims=True))
    return dx, dy * xhat


def _sigmoid(x):
    return 1.0 / (1.0 + jnp.exp(-x))


def _neg_softplus(z):
    return -(jnp.maximum(z, 0.0) + jnp.log(1.0 + jnp.exp(-jnp.abs(z))))


def _colsum(x):
    return jnp.sum(x, axis=0, keepdims=True)


def _load_once(src_hbm, dst_vmem):
    @pl.when(pl.program_id(0) == 0)
    def _():
        pltpu.sync_copy(src_hbm, dst_vmem)


def _zero_first(ref):
    @pl.when(pl.program_id(0) == 0)
    def _():
        ref[...] = jnp.zeros(ref.shape, ref.dtype)


def _row_spec(width, col=0, rows=ROW_TILE):
    return pl.BlockSpec((rows, width), lambda i, col=col: (i, col))


def _wide_spec(width, col=0):
    return _row_spec(width, col, WIDE_ROW_TILE)


def _wide_tile(T):
    assert T % WIDE_ROW_TILE == 0
    return WIDE_ROW_TILE


def _full_spec(shape):
    return pl.BlockSpec(shape, lambda *_: (0,) * len(shape))


ANY_SPEC = pl.BlockSpec(memory_space=pl.ANY)
PACK_SPEC = _full_spec((8, D_MODEL))


class Comm(NamedTuple):
    inputs: list
    out_shape: list
    aliases: dict
    scratch: list
    start: Callable
    finish: Callable


def _pallas(body, *, comm=None, edge=None, in_specs, out_specs, out_shape, scratch_shapes=(), **kw):
    if comm is None:
        return pl.pallas_call(body, in_specs=in_specs, out_specs=out_specs, out_shape=out_shape,
                              scratch_shapes=scratch_shapes, **kw)
    n_in, n_out, n_scr = len(in_specs), len(out_specs), len(scratch_shapes)
    c_in, c_out = len(comm.inputs), len(comm.out_shape)

    def both(*refs):
        ins, c_ins = refs[:n_in], refs[n_in:n_in + c_in]
        outs = refs[n_in + c_in:n_in + c_in + n_out]
        c_outs = refs[n_in + c_in + n_out:n_in + c_in + n_out + c_out]
        rest = refs[n_in + c_in + n_out + c_out:]
        scr, c_scr = rest[:n_scr], rest[n_scr:]
        first, last = edge()

        @pl.when(first)
        def _():
            comm.start(c_ins, c_outs, c_scr)

        body(*ins, *outs, *scr)

        @pl.when(last)
        def _():
            comm.finish(c_ins, c_outs, c_scr)

    call = pl.pallas_call(
        both, in_specs=list(in_specs) + [ANY_SPEC] * c_in, out_specs=list(out_specs) + [ANY_SPEC] * c_out,
        out_shape=list(out_shape) + list(comm.out_shape), scratch_shapes=list(scratch_shapes) + list(comm.scratch),
        input_output_aliases={n_in + a: n_out + b for a, b in comm.aliases.items()}, **kw)
    return lambda *args: call(*args, *comm.inputs)


def _grid_edge(steps):
    return lambda: (pl.program_id(0) == 0, pl.program_id(0) == steps - 1)


def in_proj_fwd(x, g_pre, w_in, comm=None):
    T = x.shape[0]
    pw = w_in.shape[2]

    def body(x_ref, g_ref, w_hbm, ph_ref, sqkv_ref, u_ref, w_vmem, proj_s):
        _load_once(w_hbm, w_vmem)
        xh, _ = _rms(x_ref[...])
        u = (xh * g_ref[...]).astype(BF16)
        u_ref[...] = u
        for q in range(N_CHIPS):
            proj_s[:, pw * q:pw * (q + 1)] = jnp.dot(u, w_vmem[q], preferred_element_type=F32)
        ph_ref[...] = proj_s[:, :4 * HG_WIDTH]
        sqkv_ref[:, :SB_WIDTH] = (proj_s[:, 4 * HG_WIDTH:4 * HG_WIDTH + SB_WIDTH] * SB_SCALE).astype(BF16)
        sqkv_ref[:, SB_WIDTH:] = proj_s[:, 4 * HG_WIDTH + SB_WIDTH:].astype(BF16)

    return _pallas(
        body, comm=comm, edge=_grid_edge(T // _wide_tile(T)), name="in_proj_fwd", grid=(T // _wide_tile(T),),
        in_specs=[_wide_spec(D_MODEL), _full_spec((1, D_MODEL)), ANY_SPEC],
        out_specs=[_wide_spec(4 * HG_WIDTH), _wide_spec(3 * SB_WIDTH), _wide_spec(D_MODEL)],
        out_shape=[jax.ShapeDtypeStruct((T, 4 * HG_WIDTH), F32), jax.ShapeDtypeStruct((T, 3 * SB_WIDTH), BF16),
                   jax.ShapeDtypeStruct((T, D_MODEL), BF16)],
        scratch_shapes=[pltpu.VMEM(w_in.shape, BF16), pltpu.VMEM((_wide_tile(T), N_CHIPS * pw), F32)],
        compiler_params=_params(("arbitrary",)),
    )(x, g_pre, w_in)


def _hg_sum_matrix():
    C = HG_CHUNK
    t = np.arange(C)[:, None]
    j = np.arange(C)[None, :]
    mats = [j <= t, j > t]
    for h in HG_LEVELS:
        start = (t // (2 * h)) * (2 * h)
        upper = (t & h) != 0
        mats.append(np.where(upper, (j >= start + h) & (j <= t), (j > t) & (j <= start + h - 1)))
    return np.concatenate(mats, 0).astype(np.float32)


def _hg_level_masks():
    C = HG_CHUNK
    t = lax.broadcasted_iota(jnp.int32, (C, C), 0)
    s = lax.broadcasted_iota(jnp.int32, (C, C), 1)
    x = t ^ s
    masks = [t == s]
    for h in HG_LEVELS:
        masks.append((x >= h) & (x < 2 * h) & (t > s))
    return masks


def _hg_gates(hq, hf, gamma):
    lb = 1.0 / (1.0 + jnp.exp(gamma[1:2, :] - gamma[0:1, :]))
    sq = _sigmoid(hq)
    q = hq * sq
    sig = _sigmoid(hf)
    nsig = _sigmoid(-hf)
    f = lb + (1.0 - lb) * sig
    k = (1.0 - lb) * nsig
    g = jnp.log(f)
    return q, k, g, dict(lb=lb, sq=sq, sig=sig, nsig=nsig, f=f)


def _hg_head_decays(A, h):
    C, K = HG_CHUNK, HG_DK
    sl = slice(K * h, K * (h + 1))
    blocks = [A[C * r:C * (r + 1), sl] for r in range(2 + len(HG_LEVELS))]
    return blocks[0], blocks[1], [None] + blocks[2:]


def _hg_products(q, k, levels):
    return [_dot_nt(q, k)] + [_dot_nt(q * a, k * a) for a in levels[1:]]


def _hg_select(prods, masks):
    sc = jnp.where(masks[0], prods[0], 0.0)
    for p, m in zip(prods[1:], masks[1:]):
        sc = jnp.where(m, p, sc)
    return sc


def hgrn2_fwd(proj_h, gamma, comm=None):
    T = proj_h.shape[0]
    C, K, H, S = HG_CHUNK, HG_DK, HG_HEADS, HG_CHUNKS_PER_STEP
    n_steps = T // (S * C)
    msum = jnp.asarray(_hg_sum_matrix(), BF16)

    def body(hq_ref, hf_ref, hi_ref, gam_ref, msum_ref, o_ref, st_ref, st_s):
        _zero_first(st_s)
        q, k, g, _ = _hg_gates(hq_ref[...], hf_ref[...], gam_ref[...])
        v = hi_ref[...]
        masks = _hg_level_masks()
        parts = []
        for s in range(S):
            rows = slice(C * s, C * (s + 1))
            A = jnp.exp(_sum01_left(msum_ref[...], g[rows]))
            for h in range(H):
                sl = slice(K * h, K * (h + 1))
                ab, ar, levels = _hg_head_decays(A, h)
                parts.append(dict(s=s, h=h, rows=rows, sl=sl, ab=ab, ar=ar, levels=levels,
                                  q=q[rows, sl], k=k[rows, sl], v=v[rows, sl]))
        for pt in parts:
            pt["prods"] = _hg_products(pt["q"], pt["k"], pt["levels"])
            pt["grown"] = _dot_tn(pt["v"], pt["k"] * pt["ar"])
        for pt in parts:
            pt["sc"] = _hg_select(pt["prods"], masks)
        state = [st_s[h] for h in range(H)]
        for pt in parts:
            h, ab = pt["h"], pt["ab"]
            o_ref[pt["rows"], pt["sl"]] = _dot_nt(pt["q"] * ab, state[h]) + _dot(pt["sc"], pt["v"])
            state[h] = state[h] * ab[C - 1:C, :] + pt["grown"]
            st_ref[pt["s"], h] = state[h]
        for h in range(H):
            st_s[h] = state[h]

    blk = lambda col: pl.BlockSpec((S * C, HG_WIDTH), lambda c, col=col: (c, col))
    return _pallas(
        body, comm=comm, edge=_grid_edge(n_steps), name="hgrn2_fwd", grid=(n_steps,),
        in_specs=[blk(0), blk(1), blk(2), _full_spec((2, HG_WIDTH)), _full_spec(msum.shape)],
        out_specs=[blk(0), pl.BlockSpec((S, H, K, K), lambda c: (c, 0, 0, 0))],
        out_shape=[jax.ShapeDtypeStruct((T, HG_WIDTH), F32), jax.ShapeDtypeStruct((S * n_steps, H, K, K), F32)],
        scratch_shapes=[pltpu.VMEM((H, K, K), F32)],
        compiler_params=_params(("arbitrary",)),
    )(proj_h, proj_h, proj_h, gamma, msum)


def hgrn2_bwd(proj_h, gamma, states, do, comm=None):
    T = proj_h.shape[0]
    C, K, H, S = HG_CHUNK, HG_DK, HG_HEADS, HG_CHUNKS_PER_STEP
    n_steps = T // (S * C)
    n_sums = 2 + len(HG_LEVELS)
    msum = jnp.asarray(_hg_sum_matrix(), BF16)
    msum_t = jnp.asarray(_hg_sum_matrix().T, BF16)

    def body(hq_ref, hf_ref, hi_ref, do_ref, gam_ref, msum_ref, msum_t_ref, st_prev_ref, st_ref,
             dhq_ref, dhf_ref, dhi_ref, pack_ref, dst_s, dlb_s, dq_s, dk_s, de_s):
        step = pl.program_id(0)
        _zero_first(dst_s)
        _zero_first(dlb_s)
        _zero_first(pack_ref)
        hq = hq_ref[...]
        q, k, g, aux = _hg_gates(hq, hf_ref[...], gam_ref[...])
        v = hi_ref[...]
        do_all = do_ref[...]
        masks = _hg_level_masks()
        is_last_row = lax.broadcasted_iota(jnp.int32, (C, K), 0) == C - 1
        has_prev = (step < n_steps - 1).astype(F32)
        parts = []
        for s in reversed(range(S)):
            rows = slice(C * s, C * (s + 1))
            A = jnp.exp(_sum01_left(msum_ref[...], g[rows]))
            for h in range(H):
                sl = slice(K * h, K * (h + 1))
                ab, ar, levels = _hg_head_decays(A, h)
                st_in = st_prev_ref[0, h] * has_prev if s == 0 else st_ref[s - 1, h]
                parts.append(dict(s=s, h=h, rows=rows, sl=sl, ab=ab, ar=ar, levels=levels, st_in=st_in,
                                  q=q[rows, sl], k=k[rows, sl], v=v[rows, sl], do=do_all[rows, sl]))
        for pt in parts:
            pt["prods"] = _hg_products(pt["q"], pt["k"], pt["levels"])
            pt["da"] = _dot_nt(pt["do"], pt["v"])
            pt["t1"] = pt["ab"] * _dot(pt["do"], pt["st_in"])
            pt["dst_add"] = _dot_tn(pt["do"], pt["q"] * pt["ab"])
        dstate = [dst_s[h] for h in range(H)]
        for pt in parts:
            h = pt["h"]
            pt["dst_out"] = dstate[h]
            pt["t2"] = pt["ar"] * _dot(pt["v"], dstate[h])
            pt["dv_state"] = _dot_nt(pt["k"] * pt["ar"], dstate[h])
            dstate[h] = dstate[h] * pt["ab"][C - 1:C, :] + pt["dst_add"]
        for h in range(H):
            dst_s[h] = dstate[h]
        for pt in parts:
            pt["sc"] = _hg_select(pt["prods"], masks)
            pt["dam"] = [jnp.where(m, pt["da"], 0.0) for m in masks]
        for pt in parts:
            qh, kh = pt["q"], pt["k"]
            pt["dq_parts"] = [_dot(pt["dam"][0], kh)] + [
                a * _dot(dam, kh * a) for a, dam in zip(pt["levels"][1:], pt["dam"][1:])]
            pt["dk_parts"] = [_dot_tn(pt["dam"][0], qh)] + [
                a * _dot_tn(dam, qh * a) for a, dam in zip(pt["levels"][1:], pt["dam"][1:])]
            pt["dv_intra"] = _dot_tn(pt["sc"], pt["do"])
        for pt in parts:
            s, rows, sl, qh, kh, ab = pt["s"], pt["rows"], pt["sl"], pt["q"], pt["k"], pt["ab"]
            decayed = _colsum(pt["st_in"] * pt["dst_out"]) * ab[C - 1:C, :]
            de_s[s, 0:C, sl] = qh * pt["t1"] + jnp.where(is_last_row, decayed, 0.0)
            de_s[s, C:2 * C, sl] = kh * pt["t2"]
            dq = pt["t1"] + pt["dq_parts"][0]
            dk = pt["t2"] + pt["dk_parts"][0]
            for r, (t1, t2) in enumerate(zip(pt["dq_parts"][1:], pt["dk_parts"][1:])):
                dq = dq + t1
                dk = dk + t2
                de_s[s, C * (r + 2):C * (r + 3), sl] = qh * t1 + kh * t2
            dhi_ref[rows, sl] = pt["dv_intra"] + pt["dv_state"]
            dq_s[rows, sl] = dq
            dk_s[rows, sl] = dk
        dg = jnp.concatenate([_sum01_left(msum_t_ref[...], de_s[s]) for s in range(S)], axis=0)
        dk = dk_s[...]
        sq, lb = aux["sq"], aux["lb"]
        dhq_ref[...] = dq_s[...] * (sq * (1.0 + hq * (1.0 - sq)))
        common = dg / aux["f"] - dk
        dhf_ref[...] = (1.0 - lb) * aux["sig"] * aux["nsig"] * common
        dlb_s[...] += _colsum(aux["nsig"] * common)

        @pl.when(step == n_steps - 1)
        def _():
            dgam = lb * (1.0 - lb) * dlb_s[...]
            pack_ref[ROW_GAMMA:ROW_GAMMA + 1, :HG_WIDTH] = dgam
            pack_ref[ROW_GAMMA:ROW_GAMMA + 1, HG_WIDTH:] = -dgam

    last = n_steps - 1
    blk = lambda col: pl.BlockSpec((S * C, HG_WIDTH), lambda c, col=col: (last - c, col))
    return _pallas(
        body, comm=comm, edge=_grid_edge(n_steps), name="hgrn2_bwd", grid=(n_steps,),
        in_specs=[blk(0), blk(1), blk(2), blk(0), _full_spec((2, HG_WIDTH)), _full_spec(msum.shape),
                  _full_spec(msum_t.shape),
                  pl.BlockSpec((1, H, K, K), lambda c: (jnp.maximum(S * (last - c) - 1, 0), 0, 0, 0)),
                  pl.BlockSpec((S, H, K, K), lambda c: (last - c, 0, 0, 0))],
        out_specs=[blk(0), blk(0), blk(0), PACK_SPEC],
        out_shape=[jax.ShapeDtypeStruct((T, HG_WIDTH), F32)] * 3 + [jax.ShapeDtypeStruct((8, D_MODEL), F32)],
        scratch_shapes=[pltpu.VMEM((H, K, K), F32), pltpu.VMEM((1, HG_WIDTH), F32), pltpu.VMEM((S * C, HG_WIDTH), F32),
                        pltpu.VMEM((S * C, HG_WIDTH), F32), pltpu.VMEM((S, n_sums * C, HG_WIDTH), F32)],
        compiler_params=_params(("arbitrary",)),
    )(proj_h, proj_h, proj_h, do, gamma, msum, msum_t, states, states)


def _sb_sum_matrix(inclusive):
    B = SB_BLOCK
    j = np.arange(B)[:, None]
    s = np.arange(B)[None, :]
    tri = (j >= s) if inclusive else (j > s)
    once = np.concatenate([tri, np.ones((B, B), bool)], 1).astype(np.float32)
    return np.concatenate([once, once], 0)


def _sb_prefix_matrix(inclusive):
    B = SB_BLOCK
    j = np.arange(B)[:, None]
    s = np.arange(B)[None, :]
    tri = (j <= s) if inclusive else (j < s)
    once = np.concatenate([tri, np.ones((B, B), bool)], 1).astype(np.float32)
    return np.concatenate([once, once], 0)


def _sb_iotas():
    shape = (SB_BLOCK, SB_BLOCK)
    return lax.broadcasted_iota(jnp.int32, shape, 0), lax.broadcasted_iota(jnp.int32, shape, 1)


def _sb_heads(q, first):
    heads = []
    for g in range(SB_GROUP):
        qg = q[:, SB_BLOCK * g:SB_BLOCK * (g + 1)]
        zero = jnp.zeros_like(qg)
        heads += [(g, jnp.where(first, qg, zero)), (g, jnp.where(first, zero, qg))]
    return heads


def _lanes(x, g):
    return x[:, SB_BLOCK * g:SB_BLOCK * (g + 1)]


def sb_fwd(sqkv, comm=None):
    T = sqkv.shape[0]
    B = SB_BLOCK
    W = SB_GROUP * B
    groups = SB_WIDTH // W
    usum = jnp.asarray(_sb_sum_matrix(False), BF16)

    def body(q_ref, k_ref, v_ref, u_ref, o_ref, tl_ref, first_ref):
        p, i = pl.program_id(0), pl.program_id(1)
        row, lane = _sb_iotas()
        first = lane < SB_DH
        heads = _sb_heads(q_ref[...], first)
        u = u_ref[...]

        def more(loop):
            n, reachable, _ = loop
            return (SB_UNROLL * n <= i) & (reachable > 0)

        def step(loop):
            n, _, state = loop
            blocks = []
            for sub in range(SB_UNROLL):
                j = i - SB_UNROLL * n - sub
                off = pl.multiple_of(jnp.maximum(j, 0) * B, B)
                valid = ((lane + j * B) < (row + i * B)) & (j >= 0)
                blocks.append((k_ref[pl.ds(off, B), :], v_ref[pl.ds(off, B), :], valid))
            z = [[_dot_nt(qh, _lanes(kj, g)) for g, qh in heads] for kj, _, _ in blocks]
            lnb = [[jnp.where(valid, _neg_softplus(zz), 0.0) for zz in zs] for zs, (_, _, valid) in zip(z, blocks)]
            sums = [[_sum01_right(x, u) for x in xs] for xs in lnb]
            out = []
            for h, (carry, acc) in enumerate(state):
                for sub, (_, vj, valid) in enumerate(blocks):
                    expo = z[sub][h] + lnb[sub][h] + carry + sums[sub][h][:, :B]
                    acc = acc + _dot(jnp.where(valid, jnp.exp(expo), 0.0), _lanes(vj, heads[h][0]))
                    carry = carry + sums[sub][h][:, B:]
                out.append((carry, acc))
            state = tuple(out)
            worst = state[0][0]
            for carry, _ in state[1:]:
                worst = jnp.maximum(worst, carry)
            reachable = (jnp.max(worst) > SB_UNDERFLOW_LOG).astype(jnp.int32)
            return n + 1, reachable, state

        zero = jnp.zeros((B, B), F32)
        done, _, state = lax.while_loop(
            more, step, (jnp.int32(0), jnp.int32(1), tuple((zero, zero) for _ in heads)))
        for g in range(SB_GROUP):
            (tot0, acc0), (tot1, acc1) = state[2 * g], state[2 * g + 1]
            o_ref[:, B * g:B * (g + 1)] = jnp.where(first, acc0, acc1)
            tl_ref[:, B * g:B * (g + 1)] = jnp.where(first, tot0, tot1)
        first_ref[p, i] = jnp.maximum(i + 1 - SB_UNROLL * done, 0)

    def edge():
        p, i = pl.program_id(0), pl.program_id(1)
        return (p == 0) & (i == 0), (p == groups - 1) & (i == T // B - 1)

    return _pallas(
        body, comm=comm, edge=edge, name="sb_fwd", grid=(groups, T // B),
        in_specs=[pl.BlockSpec((B, W), lambda p, i: (i, p)),
                  pl.BlockSpec((T, W), lambda p, i: (0, groups + p)),
                  pl.BlockSpec((T, W), lambda p, i: (0, 2 * groups + p)),
                  pl.BlockSpec(usum.shape, lambda p, i: (0, 0))],
        out_specs=[pl.BlockSpec((B, W), lambda p, i: (i, p))] * 2 + [pl.BlockSpec(memory_space=pltpu.SMEM)],
        out_shape=[jax.ShapeDtypeStruct((T, SB_WIDTH), F32)] * 2 + [jax.ShapeDtypeStruct((groups, T // B), jnp.int32)],
        compiler_params=_params(("arbitrary", "arbitrary")),
    )(sqkv, sqkv, sqkv, usum)


def sb_bwd(sqkv, do, tl, first_block):
    T = sqkv.shape[0]
    B = SB_BLOCK
    W = SB_GROUP * B
    groups = SB_WIDTH // W
    upre = jnp.asarray(_sb_prefix_matrix(True), BF16)
    uexc = jnp.asarray(_sb_prefix_matrix(False), BF16)

    def body(q_ref, k_ref, v_ref, do_ref, tl_ref, up_ref, ue_ref, first_ref, dq_ref, dk_ref, dv_ref):
        p, i = pl.program_id(0), pl.program_id(1)

        @pl.when(i == 0)
        def _():
            dk_ref[...] = jnp.zeros(dk_ref.shape, F32)
            dv_ref[...] = jnp.zeros(dv_ref.shape, F32)

        row, lane = _sb_iotas()
        first = lane < SB_DH
        do = do_ref[...]
        tl_all = tl_ref[...]
        heads = []
        for (g, qh), at in zip(_sb_heads(q_ref[...], first), (0, B - 1) * SB_GROUP):
            dog = _lanes(do, g)
            keep = first if at == 0 else jnp.logical_not(first)
            heads.append((g, qh, jnp.where(keep, dog, jnp.zeros_like(dog)).astype(BF16),
                          _lanes(tl_all, g)[:, at:at + 1]))
        up = up_ref[...]
        ue = ue_ref[...]
        start = first_ref[p, i]

        def step(n, state):
            blocks = []
            for sub in range(SB_UNROLL):
                j = start + SB_UNROLL * n + sub
                off = pl.multiple_of(jnp.minimum(j, i) * B, B)
                valid = (lane + j * B) < (row + i * B)
                blocks.append((off, k_ref[pl.ds(off, B), :], v_ref[pl.ds(off, B), :], valid))
            combos = [(s, h) for s in range(SB_UNROLL) for h in range(len(heads))]
            z = {(s, h): _dot_nt(heads[h][1], _lanes(blocks[s][1], heads[h][0])) for s, h in combos}
            da = {(s, h): _dot_nt(heads[h][2], _lanes(blocks[s][2], heads[h][0])) for s, h in combos}
            lnb = {c: jnp.where(blocks[c[0]][3], _neg_softplus(z[c]), 0.0) for c in combos}
            lb = {c: z[c] + lnb[c] for c in combos}
            sums = {c: _sum01_right(lnb[c], up) for c in combos}
            a, w = {}, {}
            seen = [st[0] for st in state]
            for s, h in combos:
                expo = lb[s, h] + (heads[h][3] - seen[h] - sums[s, h][:, :B])
                a[s, h] = jnp.where(blocks[s][3], jnp.exp(expo), 0.0)
                w[s, h] = a[s, h] * da[s, h]
                seen[h] = seen[h] + sums[s, h][:, B:]
            wsums = {c: _sum01_right(w[c], ue) for c in combos}
            dz = {}
            seen_w = [st[1] for st in state]
            for s, h in combos:
                beta = jnp.exp(lb[s, h])
                before = seen_w[h] + wsums[s, h][:, :B]
                dz[s, h] = jnp.where(blocks[s][3], w[s, h] * (1.0 - beta) - before * beta, 0.0)
                seen_w[h] = seen_w[h] + wsums[s, h][:, B:]
            dq = [st[2] for st in state]
            for s, h in combos:
                dq[h] = dq[h] + _dot(dz[s, h], _lanes(blocks[s][1], heads[h][0]))
            for s in range(SB_UNROLL):
                off = blocks[s][0]
                for g in range(SB_GROUP):
                    h0, h1 = 2 * g, 2 * g + 1
                    dk_ref[pl.ds(off, B), B * g:B * (g + 1)] += (_dot_tn(dz[s, h0], heads[h0][1])
                                                                 + _dot_tn(dz[s, h1], heads[h1][1]))
                    dv_ref[pl.ds(off, B), B * g:B * (g + 1)] += (_dot_tn(a[s, h0], heads[h0][2])
                                                                 + _dot_tn(a[s, h1], heads[h1][2]))
            return tuple(zip(seen, seen_w, dq))

        zero = jnp.zeros((B, B), F32)
        trips = (i - start + SB_UNROLL) // SB_UNROLL
        state = lax.fori_loop(0, trips, step, tuple((zero, zero, zero) for _ in heads))
        for g in range(SB_GROUP):
            dq_ref[:, B * g:B * (g + 1)] = jnp.where(first, state[2 * g][2], state[2 * g + 1][2]) * SB_SCALE

    qblk = pl.BlockSpec((B, W), lambda p, i: (i, p))
    full = pl.BlockSpec((T, W), lambda p, i: (0, p))
    return pl.pallas_call(
        body, name="sb_bwd", grid=(groups, T // B),
        in_specs=[qblk, pl.BlockSpec((T, W), lambda p, i: (0, groups + p)),
                  pl.BlockSpec((T, W), lambda p, i: (0, 2 * groups + p)), qblk, qblk,
                  pl.BlockSpec(upre.shape, lambda p, i: (0, 0)), pl.BlockSpec(uexc.shape, lambda p, i: (0, 0)),
                  pl.BlockSpec(memory_space=pltpu.SMEM)],
        out_specs=[qblk, full, full],
        out_shape=[jax.ShapeDtypeStruct((T, SB_WIDTH), F32)] * 3,
        compiler_params=_params(("arbitrary", "arbitrary")),
    )(sqkv, sqkv, sqkv, do, tl, upre, uexc, first_block)


def _mixer_out(o_hg, hg, o_sb, g_hg, g_sb):
    n_hg, r_hg = _rms(o_hg)
    s_hg = _sigmoid(hg)
    n_sb, r_sb = _rms(o_sb)
    return dict(n_hg=n_hg, r_hg=r_hg, s_hg=s_hg, n_sb=n_sb, r_sb=r_sb,
                y_hg=n_hg * g_hg * (hg * s_hg), y_sb=n_sb * g_sb)


def mix_out_fwd(o_hg, proj_h, o_sb, x, norms, g_post, w_out, comm=None):
    T = x.shape[0]

    def body(ohg_ref, hg_ref, osb_ref, x_ref, nrm_ref, gp_ref, w_hbm, cat_ref, mix_ref, h1_ref, w_vmem):
        _load_once(w_hbm, w_vmem)
        nrm = nrm_ref[...]
        m = _mixer_out(ohg_ref[...], hg_ref[...], osb_ref[...], nrm[:, :HG_WIDTH], nrm[:, HG_WIDTH:])
        cat_ref[:, :HG_WIDTH] = m["y_hg"].astype(BF16)
        cat_ref[:, HG_WIDTH:] = m["y_sb"].astype(BF16)
        mix = jnp.dot(cat_ref[...], w_vmem[...], preferred_element_type=F32)
        mix_ref[...] = mix
        mh, _ = _rms(mix)
        h1_ref[...] = x_ref[...] + mh * gp_ref[...]

    return _pallas(
        body, comm=comm, edge=_grid_edge(T // _wide_tile(T)), name="mix_out_fwd", grid=(T // _wide_tile(T),),
        in_specs=[_wide_spec(HG_WIDTH), _wide_spec(HG_WIDTH, 3), _wide_spec(SB_WIDTH), _wide_spec(D_MODEL),
                  _full_spec((1, D_MODEL)), _full_spec((1, D_MODEL)), ANY_SPEC],
        out_specs=[_wide_spec(D_MODEL)] * 3,
        out_shape=[jax.ShapeDtypeStruct((T, D_MODEL), BF16), jax.ShapeDtypeStruct((T, D_MODEL), F32),
                   jax.ShapeDtypeStruct((T, D_MODEL), F32)],
        scratch_shapes=[pltpu.VMEM(w_out.shape, BF16)],
        compiler_params=_params(("arbitrary",)),
    )(o_hg, proj_h, o_sb, x, norms, g_post, w_out)


def ffn_fwd(h1, g_pre, g_post, w_gu, w_down):
    T = h1.shape[0]
    pw = w_gu.shape[2]

    def body(h1_ref, gpre_ref, gpost_ref, wgu_hbm, wd_hbm, u2_ref, gu_ref, act_ref, y_ref, h2_ref,
             wgu_vmem, wd_vmem, gu_s):
        _load_once(wgu_hbm, wgu_vmem)
        _load_once(wd_hbm, wd_vmem)
        h1v = h1_ref[...]
        hh, _ = _rms(h1v)
        u2 = (hh * gpre_ref[...]).astype(BF16)
        u2_ref[...] = u2
        for q in range(N_CHIPS):
            gu_s[:, pw * q:pw * (q + 1)] = jnp.dot(u2, wgu_vmem[q], preferred_element_type=F32)
        gu_ref[...] = gu_s[...].astype(BF16)
        gate = gu_s[:, :D_FF]
        act = (gate * _sigmoid(gate) * gu_s[:, D_FF:]).astype(BF16)
        act_ref[...] = act
        y = jnp.dot(act, wd_vmem[...], preferred_element_type=F32)
        y_ref[...] = y
        yh, _ = _rms(y)
        h2_ref[...] = h1v + yh * gpost_ref[...]

    return pl.pallas_call(
        body, name="ffn_fwd", grid=(T // ROW_TILE,),
        in_specs=[_row_spec(D_MODEL), _full_spec((1, D_MODEL)), _full_spec((1, D_MODEL)), ANY_SPEC, ANY_SPEC],
        out_specs=[_row_spec(D_MODEL), _row_spec(2 * D_FF), _row_spec(D_FF), _row_spec(D_MODEL), _row_spec(D_MODEL)],
        out_shape=[jax.ShapeDtypeStruct((T, D_MODEL), BF16), jax.ShapeDtypeStruct((T, 2 * D_FF), BF16),
                   jax.ShapeDtypeStruct((T, D_FF), BF16), jax.ShapeDtypeStruct((T, D_MODEL), F32),
                   jax.ShapeDtypeStruct((T, D_MODEL), F32)],
        scratch_shapes=[pltpu.VMEM(w_gu.shape, BF16), pltpu.VMEM(w_down.shape, BF16),
                        pltpu.VMEM((ROW_TILE, 2 * D_FF), F32)],
        compiler_params=_params(("arbitrary",)),
    )(h1, g_pre, g_post, w_gu, w_down)


def ple_loss(h2, p, target, w_ple, w_pg):
    T = h2.shape[0]
    pw = w_ple.shape[2]

    def body(h2_ref, p_ref, t_ref, wple_hbm, wpg_hbm, de_ref, ds_ref, dh2_ref, h2b_ref, pb_ref, pack_ref,
             wple_vmem, wpg_vmem, e_s):
        _load_once(wple_hbm, wple_vmem)
        _load_once(wpg_hbm, wpg_vmem)
        _zero_first(pack_ref)
        h2v = h2_ref[...]
        h2b = h2v.astype(BF16)
        h2b_ref[...] = h2b
        pb = p_ref[...].astype(BF16)
        pb_ref[...] = pb
        for q in range(N_CHIPS):
            e_s[:, pw * q:pw * (q + 1)] = jnp.dot(pb, wple_vmem[q], preferred_element_type=F32)
        e = e_s[...]
        sig = _sigmoid(jnp.dot(h2b, wpg_vmem[...], preferred_element_type=F32))
        err = h2v + e * sig - t_ref[...]
        part = 0.5 * jnp.sum(jnp.mean(err * err, axis=-1, keepdims=True), axis=0, keepdims=True)
        lane = lax.broadcasted_iota(jnp.int32, (1, D_MODEL), 1)
        pack_ref[ROW_LOSS:ROW_LOSS + 1, :] += jnp.where(lane == 0, part, 0.0)
        dh3 = err * (1.0 / D_MODEL)
        de_ref[...] = (dh3 * sig).astype(BF16)
        ds = (dh3 * e * sig * (1.0 - sig)).astype(BF16)
        ds_ref[...] = ds
        dh2_ref[...] = dh3 + _dot_nt(ds, wpg_vmem[...])

    return pl.pallas_call(
        body, name="ple_loss", grid=(T // _wide_tile(T),),
        in_specs=[_wide_spec(D_MODEL), _wide_spec(p.shape[1]), _wide_spec(D_MODEL), ANY_SPEC, ANY_SPEC],
        out_specs=[_wide_spec(D_MODEL), _wide_spec(D_MODEL), _wide_spec(D_MODEL), _wide_spec(D_MODEL),
                   _wide_spec(p.shape[1]), PACK_SPEC],
        out_shape=[jax.ShapeDtypeStruct((T, D_MODEL), BF16), jax.ShapeDtypeStruct((T, D_MODEL), BF16),
                   jax.ShapeDtypeStruct((T, D_MODEL), F32), jax.ShapeDtypeStruct((T, D_MODEL), BF16),
                   jax.ShapeDtypeStruct(p.shape, BF16), jax.ShapeDtypeStruct((8, D_MODEL), F32)],
        scratch_shapes=[pltpu.VMEM(w_ple.shape, BF16), pltpu.VMEM(w_pg.shape, BF16), pltpu.VMEM((_wide_tile(T), D_MODEL), F32)],
        compiler_params=_params(("arbitrary",)),
    )(h2, p, target, w_ple, w_pg)


def ffn_bwd(dh2, y, h1, gu, g_pre, g_post, w_gu, w_down):
    T = h1.shape[0]
    pw = w_gu.shape[2]

    def body(dh2_ref, y_ref, h1_ref, gu_ref, gpre_ref, gpost_ref, wgu_hbm, wd_hbm, dy_ref, dgu_ref, dh1_ref, pack_ref,
             wgu_vmem, wd_vmem):
        _load_once(wgu_hbm, wgu_vmem)
        _load_once(wd_hbm, wd_vmem)
        _zero_first(pack_ref)
        dh2v = dh2_ref[...]
        yh, ry = _rms(y_ref[...])
        dy, dw = _rms_bwd(dh2v, yh, ry, gpost_ref[...])
        pack_ref[ROW_FFN_POST:ROW_FFN_POST + 1, :] += _colsum(dw)
        dyb = dy.astype(BF16)
        dy_ref[...] = dyb
        dact = _dot_nt(dyb, wd_vmem[...])
        gate = gu_ref[:, :D_FF].astype(F32)
        up = gu_ref[:, D_FF:].astype(F32)
        sg = _sigmoid(gate)
        dgu_ref[:, :D_FF] = (dact * up * (sg * (1.0 + gate * (1.0 - sg)))).astype(BF16)
        dgu_ref[:, D_FF:] = (dact * gate * sg).astype(BF16)
        du2 = _dot_nt(dgu_ref[:, :pw], wgu_vmem[0])
        for q in range(1, N_CHIPS):
            du2 = du2 + _dot_nt(dgu_ref[:, pw * q:pw * (q + 1)], wgu_vmem[q])
        hh, rh = _rms(h1_ref[...])
        dh, dw = _rms_bwd(du2, hh, rh, gpre_ref[...])
        pack_ref[ROW_FFN_PRE:ROW_FFN_PRE + 1, :] += _colsum(dw)
        dh1_ref[...] = dh2v + dh

    return pl.pallas_call(
        body, name="ffn_bwd", grid=(T // ROW_TILE,),
        in_specs=[_row_spec(D_MODEL), _row_spec(D_MODEL), _row_spec(D_MODEL), _row_spec(2 * D_FF),
                  _full_spec((1, D_MODEL)), _full_spec((1, D_MODEL)), ANY_SPEC, ANY_SPEC],
        out_specs=[_row_spec(D_MODEL), _row_spec(2 * D_FF), _row_spec(D_MODEL), PACK_SPEC],
        out_shape=[jax.ShapeDtypeStruct((T, D_MODEL), BF16), jax.ShapeDtypeStruct((T, 2 * D_FF), BF16),
                   jax.ShapeDtypeStruct((T, D_MODEL), F32), jax.ShapeDtypeStruct((8, D_MODEL), F32)],
        scratch_shapes=[pltpu.VMEM(w_gu.shape, BF16), pltpu.VMEM(w_down.shape, BF16)],
        compiler_params=_params(("arbitrary",)),
    )(dh2, y, h1, gu, g_pre, g_post, w_gu, w_down)


def mix_out_bwd(dh1, mix, o_hg, proj_h, o_sb, norms, g_post, w_out, comm=None):
    T = dh1.shape[0]

    def body(dh1_ref, mix_ref, ohg_ref, hg_ref, osb_ref, nrm_ref, gp_ref, w_hbm, dmix_ref, dohg_ref, dhg_ref, dosb_ref,
             pack_ref, w_vmem):
        _load_once(w_hbm, w_vmem)
        _zero_first(pack_ref)
        mh, rm = _rms(mix_ref[...])
        dmix, dw = _rms_bwd(dh1_ref[...], mh, rm, gp_ref[...])
        pack_ref[ROW_ATTN_POST:ROW_ATTN_POST + 1, :] += _colsum(dw)
        dmb = dmix.astype(BF16)
        dmix_ref[...] = dmb
        dcat = _dot_nt(dmb, w_vmem[...])
        nrm = nrm_ref[...]
        g_hg, g_sb = nrm[:, :HG_WIDTH], nrm[:, HG_WIDTH:]
        hg = hg_ref[...]
        m = _mixer_out(ohg_ref[...], hg, osb_ref[...], g_hg, g_sb)
        d_hg = dcat[:, :HG_WIDTH]
        silu = hg * m["s_hg"]
        dhg_ref[...] = d_hg * (m["n_hg"] * g_hg) * (m["s_hg"] * (1.0 + hg * (1.0 - m["s_hg"])))
        dx, dw = _rms_bwd(d_hg * silu, m["n_hg"], m["r_hg"], g_hg)
        dohg_ref[...] = dx
        pack_ref[ROW_MIX_NORMS:ROW_MIX_NORMS + 1, :HG_WIDTH] += _colsum(dw)
        dx, dw = _rms_bwd(dcat[:, HG_WIDTH:], m["n_sb"], m["r_sb"], g_sb)
        dosb_ref[...] = dx
        pack_ref[ROW_MIX_NORMS:ROW_MIX_NORMS + 1, HG_WIDTH:] += _colsum(dw)

    return _pallas(
        body, comm=comm, edge=_grid_edge(T // _wide_tile(T)), name="mix_out_bwd", grid=(T // _wide_tile(T),),
        in_specs=[_wide_spec(D_MODEL), _wide_spec(D_MODEL), _wide_spec(HG_WIDTH), _wide_spec(HG_WIDTH, 3), _wide_spec(SB_WIDTH),
                  _full_spec((1, D_MODEL)), _full_spec((1, D_MODEL)), ANY_SPEC],
        out_specs=[_wide_spec(D_MODEL), _wide_spec(HG_WIDTH), _wide_spec(HG_WIDTH), _wide_spec(SB_WIDTH), PACK_SPEC],
        out_shape=[jax.ShapeDtypeStruct((T, D_MODEL), BF16), jax.ShapeDtypeStruct((T, HG_WIDTH), F32),
                   jax.ShapeDtypeStruct((T, HG_WIDTH), F32), jax.ShapeDtypeStruct((T, SB_WIDTH), F32),
                   jax.ShapeDtypeStruct((8, D_MODEL), F32)],
        scratch_shapes=[pltpu.VMEM(w_out.shape, BF16)],
        compiler_params=_params(("arbitrary",)),
    )(dh1, mix, o_hg, proj_h, o_sb, norms, g_post, w_out)


def in_proj_bwd(parts, x, dh1, g_pre, w_in, comm=None):
    T = x.shape[0]
    pw = w_in.shape[2]
    n_parts = len(parts)

    def body(*refs):
        part_refs = refs[:n_parts]
        x_ref, dh1_ref, g_ref, w_hbm, dproj_ref, dx_ref, pack_ref, w_vmem = refs[n_parts:]
        _load_once(w_hbm, w_vmem)
        _zero_first(pack_ref)
        for n, ref in enumerate(part_refs):
            dproj_ref[:, HG_WIDTH * n:HG_WIDTH * (n + 1)] = ref[...].astype(BF16)
        du = _dot_nt(dproj_ref[:, :pw], w_vmem[0])
        for q in range(1, N_CHIPS):
            du = du + _dot_nt(dproj_ref[:, pw * q:pw * (q + 1)], w_vmem[q])
        xh, r = _rms(x_ref[...])
        dx, dw = _rms_bwd(du, xh, r, g_ref[...])
        pack_ref[ROW_ATTN_PRE:ROW_ATTN_PRE + 1, :] += _colsum(dw)
        dx_ref[...] = dh1_ref[...] + dx

    return _pallas(
        body, comm=comm, edge=_grid_edge(T // _wide_tile(T)), name="in_proj_bwd", grid=(T // _wide_tile(T),),
        in_specs=[_wide_spec(HG_WIDTH)] * n_parts + [_wide_spec(D_MODEL), _wide_spec(D_MODEL), _full_spec((1, D_MODEL)), ANY_SPEC],
        out_specs=[_wide_spec(n_parts * HG_WIDTH), _wide_spec(D_MODEL), PACK_SPEC],
        out_shape=[jax.ShapeDtypeStruct((T, n_parts * HG_WIDTH), BF16), jax.ShapeDtypeStruct((T, D_MODEL), F32),
                   jax.ShapeDtypeStruct((8, D_MODEL), F32)],
        scratch_shapes=[pltpu.VMEM(w_in.shape, BF16)],
        compiler_params=_params(("arbitrary",)),
    )(*parts, x, dh1, g_pre, w_in)


def weight_grad(a, g, name, *, tm, tn, tk=512, col_pieces=False, comm=None):
    T, M = a.shape
    N = g.shape[1]
    tk = min(tk, T)
    steps = T // tk

    def body(a_ref, g_ref, o_ref):
        @pl.when(pl.program_id(2) == 0)
        def _():
            o_ref[...] = jnp.zeros(o_ref.shape, F32)

        o_ref[...] += _dot_tn(a_ref[...], g_ref[...]).reshape(o_ref.shape)

    if col_pieces:
        out_shape = jax.ShapeDtypeStruct((N // tn, M, tn), F32)
        out_spec = pl.BlockSpec((1, tm, tn), lambda i, j, k: (j, i, 0))
    else:
        out_shape = jax.ShapeDtypeStruct((M, N), F32)
        out_spec = pl.BlockSpec((tm, tn), lambda i, j, k: (i, j))
    grid = (M // tm, N // tn, steps)

    def edge():
        at = [pl.program_id(d) for d in range(3)]
        return ((at[0] == 0) & (at[1] == 0) & (at[2] == 0),
                (at[0] == grid[0] - 1) & (at[1] == grid[1] - 1) & (at[2] == grid[2] - 1))

    return _pallas(
        body, comm=comm, edge=edge, name=name, grid=grid,
        in_specs=[pl.BlockSpec((tk, tm), lambda i, j, k: (k, i)), pl.BlockSpec((tk, tn), lambda i, j, k: (k, j))],
        out_specs=[out_spec], out_shape=[out_shape],
        compiler_params=_params(("arbitrary", "arbitrary", "arbitrary")),
    )(a, g)


def _place():
    x, y, c = lax.axis_index("x"), lax.axis_index("y"), lax.axis_index("c")
    chips = [(1 - x, y), (x, 1 - y), (1 - x, 1 - y)]
    return x, y, c, chips


def _chip_index(cx, cy):
    return 2 * cx + cy


def _own_slot(piece, slots):
    me = _chip_index(lax.axis_index("x"), lax.axis_index("y"))
    landing = lax.empty((slots,) + piece.shape[1:], piece.dtype)
    return lax.dynamic_update_slice(landing, piece, (me,) + (0,) * (piece.ndim - 1))


def _rcopy(src, dst, send_sem, recv_sem, device):
    return pltpu.make_async_remote_copy(src_ref=src, dst_ref=dst, send_sem=send_sem, recv_sem=recv_sem,
                                        device_id=device, device_id_type=MESH)


def gather_weights(shards):
    n = len(shards)

    def body(*refs):
        ins, outs = refs[:n], refs[2 * n:3 * n]
        send_sems, recv_sems = refs[3 * n:]
        x, y, c, chips = _place()
        me = _chip_index(x, y)
        sibling = (x, y, 1 - c)

        def rows(w, core):
            half = ins[w].shape[0] // 2
            return pl.ds(core * half, half)

        sends = []
        for w in range(n):
            for j, chip in enumerate(chips):
                sends.append(_rcopy(ins[w].at[rows(w, c)], outs[w].at[me, rows(w, c)],
                                    send_sems.at[6 * w + j], recv_sems.at[6 * w + j], (*chip, c)))
        for cp in sends:
            cp.start()
        passed = []
        for w in range(n):
            for j, chip in enumerate(chips):
                block = outs[w].at[_chip_index(*chip), rows(w, c)]
                _rcopy(block, block, send_sems.at[6 * w + j], recv_sems.at[6 * w + j], (*chip, c)).wait_recv()
                cp = _rcopy(block, block, send_sems.at[6 * w + 3 + j], recv_sems.at[6 * w + 3 + j], sibling)
                cp.start()
                passed.append(cp)
        for w in range(n):
            for j, chip in enumerate(chips):
                block = outs[w].at[_chip_index(*chip), rows(w, 1 - c)]
                _rcopy(block, block, send_sems.at[6 * w + 3 + j], recv_sems.at[6 * w + 3 + j], sibling).wait_recv()
        for cp in sends + passed:
            cp.wait_send()

    filled = [_own_slot(s[None], N_CHIPS) for s in shards]
    return pl.pallas_call(
        body, name="gather_weights",
        in_specs=[ANY_SPEC] * (2 * n), out_specs=[ANY_SPEC] * n,
        out_shape=[jax.ShapeDtypeStruct(f.shape, f.dtype) for f in filled],
        input_output_aliases={n + w: w for w in range(n)},
        scratch_shapes=[pltpu.SemaphoreType.DMA((6 * n,)), pltpu.SemaphoreType.DMA((6 * n,))],
    )(*shards, *filled)


def _run_comm(comm, name):
    c_in, c_out = len(comm.inputs), len(comm.out_shape)

    def body(*refs):
        parts = refs[:c_in], refs[c_in:c_in + c_out], refs[c_in + c_out:]
        comm.start(*parts)
        comm.finish(*parts)

    return pl.pallas_call(
        body, name=name, in_specs=[ANY_SPEC] * c_in, out_specs=[ANY_SPEC] * c_out, out_shape=comm.out_shape,
        scratch_shapes=comm.scratch, input_output_aliases=comm.aliases)(*comm.inputs)


def _both(first, second):
    n_in, n_out, n_scr = len(first.inputs), len(first.out_shape), len(first.scratch)

    def split(ins, outs, scr):
        return (ins[:n_in], outs[:n_out], scr[:n_scr]), (ins[n_in:], outs[n_out:], scr[n_scr:])

    def start(*refs):
        a, b = split(*refs)
        first.start(*a)
        second.start(*b)

    def finish(*refs):
        a, b = split(*refs)
        first.finish(*a)
        second.finish(*b)

    aliases = dict(first.aliases)
    aliases.update({n_in + i: n_out + o for i, o in second.aliases.items()})
    return Comm(first.inputs + second.inputs, first.out_shape + second.out_shape, aliases,
                first.scratch + second.scratch, start, finish)


def _dma_sems(count):
    return [pltpu.SemaphoreType.DMA((count,)), pltpu.SemaphoreType.DMA((count,))]


def gather_over_ici(shards):
    n = len(shards)

    def copies(ins, outs, sems):
        send_sems, recv_sems = sems
        x, y, c, chips = _place()
        me = _chip_index(x, y)
        pairs = []
        for w in range(n):
            half = shards[w].shape[0] // 2
            rows = pl.ds(c * half, half)
            for j, chip in enumerate(chips):
                k = 3 * w + j
                landed = outs[w].at[_chip_index(*chip), rows]
                pairs.append((_rcopy(ins[w].at[rows], outs[w].at[me, rows], send_sems.at[k], recv_sems.at[k], (*chip, c)),
                              _rcopy(landed, landed, send_sems.at[k], recv_sems.at[k], (*chip, c))))
        return pairs

    def start(*refs):
        for send, _ in copies(*refs):
            send.start()

    def finish(*refs):
        pairs = copies(*refs)
        for _, landed in pairs:
            landed.wait_recv()
        for send, _ in pairs:
            send.wait_send()

    filled = [_own_slot(s[None], N_CHIPS) for s in shards]
    return Comm(list(shards) + filled, [jax.ShapeDtypeStruct(f.shape, f.dtype) for f in filled],
                {n + w: w for w in range(n)}, _dma_sems(3 * n), start, finish)


def gather_over_d2d(landed):
    n = len(landed)

    def copies(ins, outs, sems):
        send_sems, recv_sems = sems
        x, y, c, chips = _place()
        sibling = (x, y, 1 - c)
        pairs = []
        for w in range(n):
            half = landed[w].shape[1] // 2
            for j, chip in enumerate(chips):
                k = 3 * w + j
                mine = outs[w].at[_chip_index(*chip), pl.ds(c * half, half)]
                theirs = outs[w].at[_chip_index(*chip), pl.ds((1 - c) * half, half)]
                pairs.append((_rcopy(mine, mine, send_sems.at[k], recv_sems.at[k], sibling),
                              _rcopy(theirs, theirs, send_sems.at[k], recv_sems.at[k], sibling)))
        return pairs

    def start(*refs):
        for send, _ in copies(*refs):
            send.start()

    def finish(*refs):
        pairs = copies(*refs)
        for _, arrived in pairs:
            arrived.wait_recv()
        for send, _ in pairs:
            send.wait_send()

    return Comm(list(landed), [jax.ShapeDtypeStruct(a.shape, a.dtype) for a in landed], {w: w for w in range(n)},
                _dma_sems(3 * n), start, finish)


def core_halves(grads):
    n = len(grads)

    def copies(ins, outs, sems):
        send_sems, recv_sems = sems
        x, y, c, _ = _place()
        out = []
        for w in range(n):
            half = grads[w].shape[1] // 2
            out.append(_rcopy(ins[w].at[:, pl.ds((1 - c) * half, half), :], outs[w],
                              send_sems.at[w], recv_sems.at[w], (x, y, 1 - c)))
        return out

    def start(*refs):
        for cp in copies(*refs):
            cp.start()

    def finish(*refs):
        for cp in copies(*refs):
            cp.wait()

    return Comm(list(grads), [jax.ShapeDtypeStruct((g.shape[0], g.shape[1] // 2, g.shape[2]), g.dtype) for g in grads],
                {}, _dma_sems(n), start, finish)


def chip_partials(partials):
    n = len(partials)

    def copies(ins, outs, sems):
        send_sems, recv_sems = sems
        x, y, c, chips = _place()
        me = _chip_index(x, y)
        pairs = []
        for w in range(n):
            for j, chip in enumerate(chips):
                k = 3 * w + j
                landed = outs[w].at[_chip_index(*chip)]
                pairs.append((_rcopy(ins[w].at[_chip_index(*chip)], outs[w].at[me], send_sems.at[k], recv_sems.at[k],
                                     (*chip, c)),
                              _rcopy(landed, landed, send_sems.at[k], recv_sems.at[k], (*chip, c))))
        return pairs

    def start(*refs):
        for send, _ in copies(*refs):
            send.start()

    def finish(*refs):
        pairs = copies(*refs)
        for _, landed in pairs:
            landed.wait_recv()
        for send, _ in pairs:
            send.wait_send()

    me = _chip_index(lax.axis_index("x"), lax.axis_index("y"))
    filled = [_own_slot(lax.dynamic_index_in_dim(p, me, 0, keepdims=True), N_CHIPS) for p in partials]
    return Comm(list(partials) + filled, [jax.ShapeDtypeStruct(p.shape, p.dtype) for p in partials],
                {n + w: w for w in range(n)}, _dma_sems(3 * n), start, finish)


def join_core_halves(grads):
    n = len(grads)

    def body(*refs):
        outs = refs[n:2 * n]
        send_sems, recv_sems = refs[2 * n:]
        x, y, c, _ = _place()
        sibling = (x, y, 1 - c)
        copies = []
        for w in range(n):
            half = outs[w].shape[0] // 2
            mine = outs[w].at[pl.ds(c * half, half), :]
            copies.append(_rcopy(mine, mine, send_sems.at[w], recv_sems.at[w], sibling))
        for cp in copies:
            cp.start()
        for w in range(n):
            half = outs[w].shape[0] // 2
            theirs = outs[w].at[pl.ds((1 - c) * half, half), :]
            _rcopy(theirs, theirs, send_sems.at[w], recv_sems.at[w], sibling).wait_recv()
        for cp in copies:
            cp.wait_send()

    return pl.pallas_call(
        body, name="join_core_halves",
        in_specs=[ANY_SPEC] * n, out_specs=[ANY_SPEC] * n,
        out_shape=[jax.ShapeDtypeStruct(g.shape, g.dtype) for g in grads],
        input_output_aliases={w: w for w in range(n)},
        scratch_shapes=[pltpu.SemaphoreType.DMA((n,)), pltpu.SemaphoreType.DMA((n,))],
    )(*grads)


def _elementwise_rows(rows, cap=512):
    for t in range(min(rows, cap), 0, -8):
        if rows % t == 0 and t % 16 == 0:
            return t
    return rows


def add_core_halves(grad, got, core, name):
    _, rows, cols = got.shape
    tr = _elementwise_rows(rows)
    nt = rows // tr

    def body(core_ref, a_ref, b_ref, o_ref):
        o_ref[...] = (a_ref[...] + b_ref[...]).astype(BF16)

    spec = pl.BlockSpec((1, tr, cols), lambda q, i, core_ref: (q, i, 0))
    own = pl.BlockSpec((1, tr, cols), lambda q, i, core_ref: (q, core_ref[0] * nt + i, 0))
    return pl.pallas_call(
        body, name=name,
        grid_spec=pltpu.PrefetchScalarGridSpec(num_scalar_prefetch=1, grid=(N_CHIPS, nt), in_specs=[own, spec],
                                               out_specs=spec),
        out_shape=jax.ShapeDtypeStruct(got.shape, BF16),
        compiler_params=_params(("arbitrary", "arbitrary")),
    )(core, grad, got)


def add_chip_partials(parts, core, name):
    _, rows, cols = parts.shape
    tr = _elementwise_rows(rows)
    nt = rows // tr

    def body(core_ref, p_ref, o_ref):
        acc = p_ref[0].astype(F32)
        for q in range(1, N_CHIPS):
            acc = acc + p_ref[q].astype(F32)
        o_ref[...] = acc

    return pl.pallas_call(
        body, name=name,
        grid_spec=pltpu.PrefetchScalarGridSpec(
            num_scalar_prefetch=1, grid=(nt,),
            in_specs=[pl.BlockSpec((N_CHIPS, tr, cols), lambda i, core_ref: (0, i, 0))],
            out_specs=pl.BlockSpec((tr, cols), lambda i, core_ref: (core_ref[0] * nt + i, 0))),
        out_shape=jax.ShapeDtypeStruct((2 * rows, cols), F32),
        compiler_params=_params(("arbitrary",)),
    )(core, parts)


def _adamw_math(w, g, m, v):
    m = ADAM_B1 * m + (1.0 - ADAM_B1) * g
    v = ADAM_B2 * v + (1.0 - ADAM_B2) * (g * g)
    m_hat = m / (1.0 - ADAM_B1 ** ADAM_STEP)
    v_hat = v / (1.0 - ADAM_B2 ** ADAM_STEP)
    delta = -ADAM_LR * (m_hat / (jnp.sqrt(v_hat) + ADAM_EPS) + ADAM_WD * w)
    return delta, m, v


def adamw(w, g, m, v, name):
    rows, cols = w.shape
    tr = _elementwise_rows(rows, 256)

    def body(w_ref, g_ref, m_ref, v_ref, d_ref, nm_ref, nv_ref):
        d, nm, nv = _adamw_math(w_ref[...], g_ref[...], m_ref[...], v_ref[...])
        d_ref[...] = d
        nm_ref[...] = nm
        nv_ref[...] = nv

    spec = pl.BlockSpec((tr, cols), lambda i: (i, 0))
    return pl.pallas_call(
        body, name=name, grid=(rows // tr,), in_specs=[spec] * 4, out_specs=[spec] * 3,
        out_shape=[jax.ShapeDtypeStruct((rows, cols), F32)] * 3,
        compiler_params=_params(("arbitrary",)),
    )(w, g, m, v)


def reduce_small(packs, w, m, v):
    n = len(packs)
    n_dev = 8
    flips = [(fx, fy, fc) for fx in (0, 1) for fy in (0, 1) for fc in (0, 1)][1:]

    def body(*refs):
        pack_refs = refs[:n]
        w_ref, m_ref, v_ref, g_out, d_out, m_out, v_out, mine, slots, send_sems, recv_sems = refs[n:]
        x, y, c, _ = _place()
        me = 4 * x + 2 * y + c
        acc = pack_refs[0][...]
        for ref in pack_refs[1:]:
            acc = acc + ref[...]
        mine[...] = acc
        sends = []
        for k, (fx, fy, fc) in enumerate(flips):
            peer = (x ^ fx, y ^ fy, c ^ fc)
            sends.append(_rcopy(mine, slots.at[me], send_sems.at[k], recv_sems.at[me], peer))
        for cp in sends:
            cp.start()
        slots[me] = acc
        for fx, fy, fc in flips:
            src = 4 * (x ^ fx) + 2 * (y ^ fy) + (c ^ fc)
            _rcopy(mine, slots.at[src], send_sems.at[0], recv_sems.at[src], (x, y, c)).wait_recv()
        for cp in sends:
            cp.wait_send()
        total = slots[0]
        for d in range(1, n_dev):
            total = total + slots[d]
        g_out[...] = total
        d, nm, nv = _adamw_math(w_ref[...], total, m_ref[...], v_ref[...])
        d_out[...] = d
        m_out[...] = nm
        v_out[...] = nv

    vm = pl.BlockSpec(memory_space=pltpu.VMEM)
    return pl.pallas_call(
        body, name="reduce_small",
        in_specs=[vm] * (n + 3), out_specs=[vm] * 4,
        out_shape=[jax.ShapeDtypeStruct((8, D_MODEL), F32)] * 4,
        scratch_shapes=[pltpu.VMEM((8, D_MODEL), F32), pltpu.VMEM((n_dev, 8, D_MODEL), F32),
                        pltpu.SemaphoreType.DMA((len(flips),)), pltpu.SemaphoreType.DMA((n_dev,))],
    )(*packs, w, m, v)


def _column_pieces(g):
    return g.reshape(g.shape[0], N_CHIPS, g.shape[1] // N_CHIPS).transpose(1, 0, 2)


def _pack_small(attn_pre, gamma, hg_norm, sb_norm, attn_post, ffn_pre, ffn_post):
    rows = [attn_pre, gamma.reshape(1, D_MODEL), jnp.concatenate([hg_norm, sb_norm], axis=1), attn_post, ffn_pre, ffn_post,
            jnp.zeros((2, D_MODEL), F32)]
    return jnp.concatenate(rows, axis=0)


def _unpack_small(pack):
    return (pack[ROW_ATTN_PRE:ROW_ATTN_PRE + 1], pack[ROW_GAMMA].reshape(2, HG_WIDTH),
            pack[ROW_MIX_NORMS:ROW_MIX_NORMS + 1, :HG_WIDTH], pack[ROW_MIX_NORMS:ROW_MIX_NORMS + 1, HG_WIDTH:],
            pack[ROW_ATTN_POST:ROW_ATTN_POST + 1], pack[ROW_FFN_PRE:ROW_FFN_PRE + 1], pack[ROW_FFN_POST:ROW_FFN_POST + 1])


def kernel(x, p, attn_pre_norm, w_in, hg_lower_gamma, hg_out_norm, sb_out_norm, w_out, attn_post_norm, ffn_pre_norm, w_gate_up, w_down, ffn_post_norm, ple_proj, ple_gate, loss_target, m_attn_pre_norm, m_w_in, m_hg_lower_gamma, m_hg_out_norm, m_sb_out_norm, m_w_out, m_attn_post_norm, m_ffn_pre_norm, m_w_gate_up, m_w_down, m_ffn_post_norm, m_ple_proj, m_ple_gate, v_attn_pre_norm, v_w_in, v_hg_lower_gamma, v_hg_out_norm, v_sb_out_norm, v_w_out, v_attn_post_norm, v_ffn_pre_norm, v_w_gate_up, v_w_down, v_ffn_post_norm, v_ple_proj, v_ple_gate):
    x2 = x[0]
    p2 = p[0, 0]
    target = loss_target[0]
    big = dict(w_in=(w_in, m_w_in, v_w_in), w_out=(w_out, m_w_out, v_w_out), w_gate_up=(w_gate_up, m_w_gate_up, v_w_gate_up),
               w_down=(w_down, m_w_down, v_w_down), ple_proj=(ple_proj, m_ple_proj, v_ple_proj),
               ple_gate=(ple_gate, m_ple_gate, v_ple_gate))
    names = list(big)
    big = {k: tuple(a[0] for a in t) for k, t in big.items()}

    shard16 = {k: big[k][0].astype(BF16) for k in names}
    w_in_full, = gather_weights([shard16["w_in"]])
    mix_norms = jnp.concatenate([hg_out_norm, sb_out_norm], axis=1)
    small_ones = ["w_out", "ple_proj", "ple_gate"]

    proj_h, sqkv, u1, landed_down = in_proj_fwd(x2, attn_pre_norm, w_in_full, comm=gather_over_ici([shard16["w_down"]]))
    o_sb, sb_totals, sb_first, *landed_small = sb_fwd(sqkv, comm=gather_over_ici([shard16[k] for k in small_ones]))
    o_hg, states, landed_gu, *full_small = hgrn2_fwd(
        proj_h, hg_lower_gamma, comm=_both(gather_over_ici([shard16["w_gate_up"]]), gather_over_d2d(landed_small)))
    full = dict(zip(small_ones, full_small), w_in=w_in_full)
    w_out_full = full["w_out"].reshape(D_MODEL, D_MODEL)
    w_pg_full = full["ple_gate"].reshape(D_MODEL, D_MODEL)
    cat, mix, h1, full["w_gate_up"], full["w_down"] = mix_out_fwd(
        o_hg, proj_h, o_sb, x2, mix_norms, attn_post_norm, w_out_full, comm=gather_over_d2d([landed_gu, landed_down]))
    w_down_full = full["w_down"].reshape(D_FF, D_MODEL)
    u2, gu, act, y, h2 = ffn_fwd(h1, ffn_pre_norm, ffn_post_norm, full["w_gate_up"], w_down_full)

    core = lax.axis_index("c").astype(jnp.int32).reshape(1)
    de, ds, dh2, h2b, pb, pack_loss = ple_loss(h2, p2, target, full["ple_proj"], w_pg_full)
    dy, dgu, dh1, pack_ffn = ffn_bwd(dh2, y, h1, gu, ffn_pre_norm, ffn_post_norm, full["w_gate_up"], w_down_full)
    local = {}
    local["w_gate_up"], = weight_grad(u2, dgu, "grad_w_gate_up", tm=D_MODEL, tn=full["w_gate_up"].shape[2], tk=1024,
                                      col_pieces=True)
    grad_down, got_gu = weight_grad(act, dy, "grad_w_down", tm=D_FF // 2, tn=D_MODEL, tk=1024,
                                    comm=core_halves([local["w_gate_up"]]))
    local["w_down"] = grad_down.reshape(full["w_down"].shape)
    local["ple_proj"] = _column_pieces(weight_grad(pb, de, "grad_ple_proj", tm=pb.shape[1], tn=D_MODEL, tk=1024)[0])
    grad_pg, got_down = weight_grad(h2b, ds, "grad_ple_gate", tm=D_MODEL, tn=D_MODEL, tk=1024,
                                    comm=core_halves([local["w_down"]]))
    local["ple_gate"] = grad_pg.reshape(full["ple_gate"].shape)
    early = list(local)
    dmix, do_hg, dhg, do_sb, pack_mix, *got_ple = mix_out_bwd(
        dh1, mix, o_hg, proj_h, o_sb, mix_norms, attn_post_norm, w_out_full,
        comm=core_halves([local["ple_proj"], local["ple_gate"]]))
    got = [got_gu, got_down] + got_ple
    partial = [add_core_halves(local[k], g, core, "add_core_halves_" + k) for k, g in zip(early, got)]
    local["w_out"] = weight_grad(cat, dmix, "grad_w_out", tm=D_MODEL, tn=D_MODEL, tk=1024)[0].reshape(full["w_out"].shape)
    dsq, dsk, dsv = sb_bwd(sqkv, do_sb, sb_totals, sb_first)
    dhq, dhf, dhi, pack_hg, *by_source, got_out = hgrn2_bwd(
        proj_h, hg_lower_gamma, states, do_hg, comm=_both(chip_partials(partial[:2]), core_halves([local["w_out"]])))
    early.append("w_out")
    partial.append(add_core_halves(local["w_out"], got_out, core, "add_core_halves_w_out"))
    dproj, grad_x, pack_in = in_proj_bwd([dhq, dhf, dhi, dhg, dsq, dsk, dsv], x2, dh1, attn_pre_norm, full["w_in"])

    late = ["w_in"]
    local["w_in"], *more = weight_grad(u1, dproj, "grad_w_in", tm=D_MODEL, tn=full["w_in"].shape[2], tk=1024,
                                       col_pieces=True, comm=chip_partials(partial[2:]))
    halves = {k: add_chip_partials(s, core, "add_chip_partials_" + k) for k, s in zip(early, by_source + more)}
    got = _run_comm(core_halves([local[k] for k in late]), "exchange_core_halves")
    partial = [add_core_halves(local[k], g, core, "add_core_halves_" + k) for k, g in zip(late, got)]
    by_source = _run_comm(chip_partials(partial), "exchange_chip_partials")
    halves.update({k: add_chip_partials(s, core, "add_chip_partials_" + k) for k, s in zip(late, by_source)})
    grads = dict(zip(names, join_core_halves([halves[k] for k in names])))

    upd = {k: adamw(big[k][0], grads[k], big[k][1], big[k][2], "adamw_" + k) for k in names}

    small = reduce_small(
        [pack_loss, pack_ffn, pack_mix, pack_hg, pack_in],
        _pack_small(attn_pre_norm, hg_lower_gamma, hg_out_norm, sb_out_norm, attn_post_norm, ffn_pre_norm, ffn_post_norm),
        _pack_small(m_attn_pre_norm, m_hg_lower_gamma, m_hg_out_norm, m_sb_out_norm, m_attn_post_norm, m_ffn_pre_norm, m_ffn_post_norm),
        _pack_small(v_attn_pre_norm, v_hg_lower_gamma, v_hg_out_norm, v_sb_out_norm, v_attn_post_norm, v_ffn_pre_norm, v_ffn_post_norm),
    )
    loss = small[0][ROW_LOSS, 0]
    s_grad, s_delta, s_m, s_v = (_unpack_small(t) for t in small)

    def ordered(small_vals, big_vals):
        a_pre, gam, hg_n, sb_n, a_post, f_pre, f_post = small_vals
        b = {k: big_vals[k][None] for k in names}
        return (a_pre, b["w_in"], gam, hg_n, sb_n, b["w_out"], a_post, f_pre, b["w_gate_up"], b["w_down"], f_post,
                b["ple_proj"], b["ple_gate"])

    return (loss, grad_x[None],
            *ordered(s_grad, grads),
            *ordered(s_delta, {k: upd[k][0] for k in names}),
            *ordered(s_m, {k: upd[k][1] for k in names}),
            *ordered(s_v, {k: upd[k][2] for k in names}))
```

```python
from typing import Callable, NamedTuple

import numpy as np
import jax
import jax.numpy as jnp
from jax import lax
from jax.experimental import pallas as pl
from jax.experimental.pallas import tpu as pltpu

F32 = jnp.float32
BF16 = jnp.bfloat16
MESH = pl.DeviceIdType.MESH

RMS_EPS = 1e-6
D_MODEL = 1024
HG_WIDTH = 512
HG_HEADS = 4
HG_DK = 128
HG_CHUNK = 64
HG_LEVELS = (32, 16, 8, 4, 2, 1)
HG_CHUNKS_PER_STEP = 2
SB_WIDTH = 512
SB_BLOCK = 128
SB_DH = 64
SB_SCALE = SB_DH ** -0.5
SB_UNDERFLOW_LOG = -87.5
SB_UNROLL = 2
SB_GROUP = 4
D_FF = 2816
N_CHIPS = 4
ROW_TILE = 256
WIDE_ROW_TILE = 512
V7X_VMEM_LIMIT = 56 * 1024 * 1024

ADAM_LR = 0.001
ADAM_B1 = 0.9
ADAM_B2 = 0.999
ADAM_EPS = 1e-08
ADAM_WD = 0.01
ADAM_STEP = 10

ROW_ATTN_PRE, ROW_GAMMA, ROW_MIX_NORMS, ROW_ATTN_POST, ROW_FFN_PRE, ROW_FFN_POST, ROW_LOSS = range(7)


def _params(sem=None, vmem=V7X_VMEM_LIMIT):
    return pltpu.CompilerParams(dimension_semantics=sem, vmem_limit_bytes=vmem)


def _dot(a, b):
    return jnp.dot(a.astype(BF16), b.astype(BF16), preferred_element_type=F32)


def _dot_nt(a, b):
    return lax.dot_general(a.astype(BF16), b.astype(BF16), (((1,), (1,)), ((), ())), preferred_element_type=F32)


def _dot_tn(a, b):
    return lax.dot_general(a.astype(BF16), b.astype(BF16), (((0,), (0,)), ((), ())), preferred_element_type=F32)


def _split(x):
    hi = x.astype(BF16)
    lo = (x - hi.astype(F32)).astype(BF16)
    return hi, lo


def _sum01_left(m01, x):
    hi, lo = _split(x)
    return jnp.dot(m01, hi, preferred_element_type=F32) + jnp.dot(m01, lo, preferred_element_type=F32)


def _sum01_right(x, m01_twice):
    hi, lo = _split(x)
    return jnp.dot(jnp.concatenate([hi, lo], axis=1), m01_twice, preferred_element_type=F32)


def _rms(x):
    r = lax.rsqrt(jnp.mean(x * x, axis=-1, keepdims=True) + RMS_EPS)
    return x * r, r


def _rms_bwd(dy, xhat, r, w):
    dxh = dy * w
    dx = r * (dxh - xhat * jnp.mean(dxh * xhat, axis=-1, keepdims=True))
    return dx, dy * xhat


def _sigmoid(x):
    return 1.0 / (1.0 + jnp.exp(-x))


def _neg_softplus(z):
    return -(jnp.maximum(z, 0.0) + jnp.log(1.0 + jnp.exp(-jnp.abs(z))))


def _colsum(x):
    return jnp.sum(x, axis=0, keepdims=True)


def _load_once(src_hbm, dst_vmem):
    @pl.when(pl.program_id(0) == 0)
    def _():
        pltpu.sync_copy(src_hbm, dst_vmem)


def _zero_first(ref):
    @pl.when(pl.program_id(0) == 0)
    def _():
        ref[...] = jnp.zeros(ref.shape, ref.dtype)


def _row_spec(width, col=0, rows=ROW_TILE):
    return pl.BlockSpec((rows, width), lambda i, col=col: (i, col))


def _wide_spec(width, col=0):
    return _row_spec(width, col, WIDE_ROW_TILE)


def _wide_tile(T):
    assert T % WIDE_ROW_TILE == 0
    return WIDE_ROW_TILE


def _full_spec(shape):
    return pl.BlockSpec(shape, lambda *_: (0,) * len(shape))


ANY_SPEC = pl.BlockSpec(memory_space=pl.ANY)
PACK_SPEC = _full_spec((8, D_MODEL))


class Comm(NamedTuple):
    inputs: list
    out_shape: list
    aliases: dict
    scratch: list
    start: Callable
    finish: Callable


def _pallas(body, *, comm=None, edge=None, in_specs, out_specs, out_shape, scratch_shapes=(), **kw):
    if comm is None:
        return pl.pallas_call(body, in_specs=in_specs, out_specs=out_specs, out_shape=out_shape,
                              scratch_shapes=scratch_shapes, **kw)
    n_in, n_out, n_scr = len(in_specs), len(out_specs), len(scratch_shapes)
    c_in, c_out = len(comm.inputs), len(comm.out_shape)

    def both(*refs):
        ins, c_ins = refs[:n_in], refs[n_in:n_in + c_in]
        outs = refs[n_in + c_in:n_in + c_in + n_out]
        c_outs = refs[n_in + c_in + n_out:n_in + c_in + n_out + c_out]
        rest = refs[n_in + c_in + n_out + c_out:]
        scr, c_scr = rest[:n_scr], rest[n_scr:]
        first, last = edge()

        @pl.when(first)
        def _():
            comm.start(c_ins, c_outs, c_scr)

        body(*ins, *outs, *scr)

        @pl.when(last)
        def _():
            comm.finish(c_ins, c_outs, c_scr)

    call = pl.pallas_call(
        both, in_specs=list(in_specs) + [ANY_SPEC] * c_in, out_specs=list(out_specs) + [ANY_SPEC] * c_out,
        out_shape=list(out_shape) + list(comm.out_shape), scratch_shapes=list(scratch_shapes) + list(comm.scratch),
        input_output_aliases={n_in + a: n_out + b for a, b in comm.aliases.items()}, **kw)
    return lambda *args: call(*args, *comm.inputs)


def _grid_edge(steps):
    return lambda: (pl.program_id(0) == 0, pl.program_id(0) == steps - 1)


def in_proj_fwd(x, g_pre, w_in, comm=None):
    T = x.shape[0]
    pw = w_in.shape[2]

    def body(x_ref, g_ref, w_hbm, ph_ref, sqkv_ref, u_ref, w_vmem, proj_s):
        _load_once(w_hbm, w_vmem)
        xh, _ = _rms(x_ref[...])
        u = (xh * g_ref[...]).astype(BF16)
        u_ref[...] = u
        for q in range(N_CHIPS):
            proj_s[:, pw * q:pw * (q + 1)] = jnp.dot(u, w_vmem[q], preferred_element_type=F32)
        ph_ref[...] = proj_s[:, :4 * HG_WIDTH]
        sqkv_ref[:, :SB_WIDTH] = (proj_s[:, 4 * HG_WIDTH:4 * HG_WIDTH + SB_WIDTH] * SB_SCALE).astype(BF16)
        sqkv_ref[:, SB_WIDTH:] = proj_s[:, 4 * HG_WIDTH + SB_WIDTH:].astype(BF16)

    return _pallas(
        body, comm=comm, edge=_grid_edge(T // _wide_tile(T)), name="in_proj_fwd", grid=(T // _wide_tile(T),),
        in_specs=[_wide_spec(D_MODEL), _full_spec((1, D_MODEL)), ANY_SPEC],
        out_specs=[_wide_spec(4 * HG_WIDTH), _wide_spec(3 * SB_WIDTH), _wide_spec(D_MODEL)],
        out_shape=[jax.ShapeDtypeStruct((T, 4 * HG_WIDTH), F32), jax.ShapeDtypeStruct((T, 3 * SB_WIDTH), BF16),
                   jax.ShapeDtypeStruct((T, D_MODEL), BF16)],
        scratch_shapes=[pltpu.VMEM(w_in.shape, BF16), pltpu.VMEM((_wide_tile(T), N_CHIPS * pw), F32)],
        compiler_params=_params(("arbitrary",)),
    )(x, g_pre, w_in)


def _hg_sum_matrix():
    C = HG_CHUNK
    t = np.arange(C)[:, None]
    j = np.arange(C)[None, :]
    mats = [j <= t, j > t]
    for h in HG_LEVELS:
        start = (t // (2 * h)) * (2 * h)
        upper = (t & h) != 0
        mats.append(np.where(upper, (j >= start + h) & (j <= t), (j > t) & (j <= start + h - 1)))
    return np.concatenate(mats, 0).astype(np.float32)


def _hg_level_masks():
    C = HG_CHUNK
    t = lax.broadcasted_iota(jnp.int32, (C, C), 0)
    s = lax.broadcasted_iota(jnp.int32, (C, C), 1)
    x = t ^ s
    masks = [t == s]
    for h in HG_LEVELS:
        masks.append((x >= h) & (x < 2 * h) & (t > s))
    return masks


def _hg_gates(hq, hf, gamma):
    lb = 1.0 / (1.0 + jnp.exp(gamma[1:2, :] - gamma[0:1, :]))
    sq = _sigmoid(hq)
    q = hq * sq
    sig = _sigmoid(hf)
    nsig = _sigmoid(-hf)
    f = lb + (1.0 - lb) * sig
    k = (1.0 - lb) * nsig
    g = jnp.log(f)
    return q, k, g, dict(lb=lb, sq=sq, sig=sig, nsig=nsig, f=f)


def _hg_head_decays(A, h):
    C, K = HG_CHUNK, HG_DK
    sl = slice(K * h, K * (h + 1))
    blocks = [A[C * r:C * (r + 1), sl] for r in range(2 + len(HG_LEVELS))]
    return blocks[0], blocks[1], [None] + blocks[2:]


def _hg_products(q, k, levels):
    return [_dot_nt(q, k)] + [_dot_nt(q * a, k * a) for a in levels[1:]]


def _hg_select(prods, masks):
    sc = jnp.where(masks[0], prods[0], 0.0)
    for p, m in zip(prods[1:], masks[1:]):
        sc = jnp.where(m, p, sc)
    return sc


def hgrn2_fwd(proj_h, gamma, comm=None):
    T = proj_h.shape[0]
    C, K, H, S = HG_CHUNK, HG_DK, HG_HEADS, HG_CHUNKS_PER_STEP
    n_steps = T // (S * C)
    msum = jnp.asarray(_hg_sum_matrix(), BF16)

    def body(hq_ref, hf_ref, hi_ref, gam_ref, msum_ref, o_ref, st_ref, st_s):
        _zero_first(st_s)
        q, k, g, _ = _hg_gates(hq_ref[...], hf_ref[...], gam_ref[...])
        v = hi_ref[...]
        masks = _hg_level_masks()
        parts = []
        for s in range(S):
            rows = slice(C * s, C * (s + 1))
            A = jnp.exp(_sum01_left(msum_ref[...], g[rows]))
            for h in range(H):
                sl = slice(K * h, K * (h + 1))
                ab, ar, levels = _hg_head_decays(A, h)
                parts.append(dict(s=s, h=h, rows=rows, sl=sl, ab=ab, ar=ar, levels=levels,
                                  q=q[rows, sl], k=k[rows, sl], v=v[rows, sl]))
        for pt in parts:
            pt["prods"] = _hg_products(pt["q"], pt["k"], pt["levels"])
            pt["grown"] = _dot_tn(pt["v"], pt["k"] * pt["ar"])
        for pt in parts:
            pt["sc"] = _hg_select(pt["prods"], masks)
        state = [st_s[h] for h in range(H)]
        for pt in parts:
            h, ab = pt["h"], pt["ab"]
            o_ref[pt["rows"], pt["sl"]] = _dot_nt(pt["q"] * ab, state[h]) + _dot(pt["sc"], pt["v"])
            state[h] = state[h] * ab[C - 1:C, :] + pt["grown"]
            st_ref[pt["s"], h] = state[h]
        for h in range(H):
            st_s[h] = state[h]

    blk = lambda col: pl.BlockSpec((S * C, HG_WIDTH), lambda c, col=col: (c, col))
    return _pallas(
        body, comm=comm, edge=_grid_edge(n_steps), name="hgrn2_fwd", grid=(n_steps,),
        in_specs=[blk(0), blk(1), blk(2), _full_spec((2, HG_WIDTH)), _full_spec(msum.shape)],
        out_specs=[blk(0), pl.BlockSpec((S, H, K, K), lambda c: (c, 0, 0, 0))],
        out_shape=[jax.ShapeDtypeStruct((T, HG_WIDTH), F32), jax.ShapeDtypeStruct((S * n_steps, H, K, K), F32)],
        scratch_shapes=[pltpu.VMEM((H, K, K), F32)],
        compiler_params=_params(("arbitrary",)),
    )(proj_h, proj_h, proj_h, gamma, msum)


def hgrn2_bwd(proj_h, gamma, states, do, comm=None):
    T = proj_h.shape[0]
    C, K, H, S = HG_CHUNK, HG_DK, HG_HEADS, HG_CHUNKS_PER_STEP
    n_steps = T // (S * C)
    n_sums = 2 + len(HG_LEVELS)
    msum = jnp.asarray(_hg_sum_matrix(), BF16)
    msum_t = jnp.asarray(_hg_sum_matrix().T, BF16)

    def body(hq_ref, hf_ref, hi_ref, do_ref, gam_ref, msum_ref, msum_t_ref, st_prev_ref, st_ref,
             dhq_ref, dhf_ref, dhi_ref, pack_ref, dst_s, dlb_s, dq_s, dk_s, de_s):
        step = pl.program_id(0)
        _zero_first(dst_s)
        _zero_first(dlb_s)
        _zero_first(pack_ref)
        hq = hq_ref[...]
        q, k, g, aux = _hg_gates(hq, hf_ref[...], gam_ref[...])
        v = hi_ref[...]
        do_all = do_ref[...]
        masks = _hg_level_masks()
        is_last_row = lax.broadcasted_iota(jnp.int32, (C, K), 0) == C - 1
        has_prev = (step < n_steps - 1).astype(F32)
        parts = []
        for s in reversed(range(S)):
            rows = slice(C * s, C * (s + 1))
            A = jnp.exp(_sum01_left(msum_ref[...], g[rows]))
            for h in range(H):
                sl = slice(K * h, K * (h + 1))
                ab, ar, levels = _hg_head_decays(A, h)
                st_in = st_prev_ref[0, h] * has_prev if s == 0 else st_ref[s - 1, h]
                parts.append(dict(s=s, h=h, rows=rows, sl=sl, ab=ab, ar=ar, levels=levels, st_in=st_in,
                                  q=q[rows, sl], k=k[rows, sl], v=v[rows, sl], do=do_all[rows, sl]))
        for pt in parts:
            pt["prods"] = _hg_products(pt["q"], pt["k"], pt["levels"])
            pt["da"] = _dot_nt(pt["do"], pt["v"])
            pt["t1"] = pt["ab"] * _dot(pt["do"], pt["st_in"])
            pt["dst_add"] = _dot_tn(pt["do"], pt["q"] * pt["ab"])
        dstate = [dst_s[h] for h in range(H)]
        for pt in parts:
            h = pt["h"]
            pt["dst_out"] = dstate[h]
            pt["t2"] = pt["ar"] * _dot(pt["v"], dstate[h])
            pt["dv_state"] = _dot_nt(pt["k"] * pt["ar"], dstate[h])
            dstate[h] = dstate[h] * pt["ab"][C - 1:C, :] + pt["dst_add"]
        for h in range(H):
            dst_s[h] = dstate[h]
        for pt in parts:
            pt["sc"] = _hg_select(pt["prods"], masks)
            pt["dam"] = [jnp.where(m, pt["da"], 0.0) for m in masks]
        for pt in parts:
            qh, kh = pt["q"], pt["k"]
            pt["dq_parts"] = [_dot(pt["dam"][0], kh)] + [
                a * _dot(dam, kh * a) for a, dam in zip(pt["levels"][1:], pt["dam"][1:])]
            pt["dk_parts"] = [_dot_tn(pt["dam"][0], qh)] + [
                a * _dot_tn(dam, qh * a) for a, dam in zip(pt["levels"][1:], pt["dam"][1:])]
            pt["dv_intra"] = _dot_tn(pt["sc"], pt["do"])
        for pt in parts:
            s, rows, sl, qh, kh, ab = pt["s"], pt["rows"], pt["sl"], pt["q"], pt["k"], pt["ab"]
            decayed = _colsum(pt["st_in"] * pt["dst_out"]) * ab[C - 1:C, :]
            de_s[s, 0:C, sl] = qh * pt["t1"] + jnp.where(is_last_row, decayed, 0.0)
            de_s[s, C:2 * C, sl] = kh * pt["t2"]
            dq = pt["t1"] + pt["dq_parts"][0]
            dk = pt["t2"] + pt["dk_parts"][0]
            for r, (t1, t2) in enumerate(zip(pt["dq_parts"][1:], pt["dk_parts"][1:])):
                dq = dq + t1
                dk = dk + t2
                de_s[s, C * (r + 2):C * (r + 3), sl] = qh * t1 + kh * t2
            dhi_ref[rows, sl] = pt["dv_intra"] + pt["dv_state"]
            dq_s[rows, sl] = dq
            dk_s[rows, sl] = dk
        dg = jnp.concatenate([_sum01_left(msum_t_ref[...], de_s[s]) for s in range(S)], axis=0)
        dk = dk_s[...]
        sq, lb = aux["sq"], aux["lb"]
        dhq_ref[...] = dq_s[...] * (sq * (1.0 + hq * (1.0 - sq)))
        common = dg / aux["f"] - dk
        dhf_ref[...] = (1.0 - lb) * aux["sig"] * aux["nsig"] * common
        dlb_s[...] += _colsum(aux["nsig"] * common)

        @pl.when(step == n_steps - 1)
        def _():
            dgam = lb * (1.0 - lb) * dlb_s[...]
            pack_ref[ROW_GAMMA:ROW_GAMMA + 1, :HG_WIDTH] = dgam
            pack_ref[ROW_GAMMA:ROW_GAMMA + 1, HG_WIDTH:] = -dgam

    last = n_steps - 1
    blk = lambda col: pl.BlockSpec((S * C, HG_WIDTH), lambda c, col=col: (last - c, col))
    return _pallas(
        body, comm=comm, edge=_grid_edge(n_steps), name="hgrn2_bwd", grid=(n_steps,),
        in_specs=[blk(0), blk(1), blk(2), blk(0), _full_spec((2, HG_WIDTH)), _full_spec(msum.shape),
                  _full_spec(msum_t.shape),
                  pl.BlockSpec((1, H, K, K), lambda c: (jnp.maximum(S * (last - c) - 1, 0), 0, 0, 0)),
                  pl.BlockSpec((S, H, K, K), lambda c: (last - c, 0, 0, 0))],
        out_specs=[blk(0), blk(0), blk(0), PACK_SPEC],
        out_shape=[jax.ShapeDtypeStruct((T, HG_WIDTH), F32)] * 3 + [jax.ShapeDtypeStruct((8, D_MODEL), F32)],
        scratch_shapes=[pltpu.VMEM((H, K, K), F32), pltpu.VMEM((1, HG_WIDTH), F32), pltpu.VMEM((S * C, HG_WIDTH), F32),
                        pltpu.VMEM((S * C, HG_WIDTH), F32), pltpu.VMEM((S, n_sums * C, HG_WIDTH), F32)],
        compiler_params=_params(("arbitrary",)),
    )(proj_h, proj_h, proj_h, do, gamma, msum, msum_t, states, states)


def _sb_sum_matrix(inclusive):
    B = SB_BLOCK
    j = np.arange(B)[:, None]
    s = np.arange(B)[None, :]
    tri = (j >= s) if inclusive else (j > s)
    once = np.concatenate([tri, np.ones((B, B), bool)], 1).astype(np.float32)
    return np.concatenate([once, once], 0)


def _sb_prefix_matrix(inclusive):
    B = SB_BLOCK
    j = np.arange(B)[:, None]
    s = np.arange(B)[None, :]
    tri = (j <= s) if inclusive else (j < s)
    once = np.concatenate([tri, np.ones((B, B), bool)], 1).astype(np.float32)
    return np.concatenate([once, once], 0)


def _sb_iotas():
    shape = (SB_BLOCK, SB_BLOCK)
    return lax.broadcasted_iota(jnp.int32, shape, 0), lax.broadcasted_iota(jnp.int32, shape, 1)


def _sb_heads(q, first):
    heads = []
    for g in range(SB_GROUP):
        qg = q[:, SB_BLOCK * g:SB_BLOCK * (g + 1)]
        zero = jnp.zeros_like(qg)
        heads += [(g, jnp.where(first, qg, zero)), (g, jnp.where(first, zero, qg))]
    return heads


def _lanes(x, g):
    return x[:, SB_BLOCK * g:SB_BLOCK * (g + 1)]


def sb_fwd(sqkv, comm=None):
    T = sqkv.shape[0]
    B = SB_BLOCK
    W = SB_GROUP * B
    groups = SB_WIDTH // W
    usum = jnp.asarray(_sb_sum_matrix(False), BF16)

    def body(q_ref, k_ref, v_ref, u_ref, o_ref, tl_ref, first_ref):
        p, i = pl.program_id(0), pl.program_id(1)
        row, lane = _sb_iotas()
        first = lane < SB_DH
        heads = _sb_heads(q_ref[...], first)
        u = u_ref[...]

        def more(loop):
            n, reachable, _ = loop
            return (SB_UNROLL * n <= i) & (reachable > 0)

        def step(loop):
            n, _, state = loop
            blocks = []
            for sub in range(SB_UNROLL):
                j = i - SB_UNROLL * n - sub
                off = pl.multiple_of(jnp.maximum(j, 0) * B, B)
                valid = ((lane + j * B) < (row + i * B)) & (j >= 0)
                blocks.append((k_ref[pl.ds(off, B), :], v_ref[pl.ds(off, B), :], valid))
            z = [[_dot_nt(qh, _lanes(kj, g)) for g, qh in heads] for kj, _, _ in blocks]
            lnb = [[jnp.where(valid, _neg_softplus(zz), 0.0) for zz in zs] for zs, (_, _, valid) in zip(z, blocks)]
            sums = [[_sum01_right(x, u) for x in xs] for xs in lnb]
            out = []
            for h, (carry, acc) in enumerate(state):
                for sub, (_, vj, valid) in enumerate(blocks):
                    expo = z[sub][h] + lnb[sub][h] + carry + sums[sub][h][:, :B]
                    acc = acc + _dot(jnp.where(valid, jnp.exp(expo), 0.0), _lanes(vj, heads[h][0]))
                    carry = carry + sums[sub][h][:, B:]
                out.append((carry, acc))
            state = tuple(out)
            worst = state[0][0]
            for carry, _ in state[1:]:
                worst = jnp.maximum(worst, carry)
            reachable = (jnp.max(worst) > SB_UNDERFLOW_LOG).astype(jnp.int32)
            return n + 1, reachable, state

        zero = jnp.zeros((B, B), F32)
        done, _, state = lax.while_loop(
            more, step, (jnp.int32(0), jnp.int32(1), tuple((zero, zero) for _ in heads)))
        for g in range(SB_GROUP):
            (tot0, acc0), (tot1, acc1) = state[2 * g], state[2 * g + 1]
            o_ref[:, B * g:B * (g + 1)] = jnp.where(first, acc0, acc1)
            tl_ref[:, B * g:B * (g + 1)] = jnp.where(first, tot0, tot1)
        first_ref[p, i] = jnp.maximum(i + 1 - SB_UNROLL * done, 0)

    def edge():
        p, i = pl.program_id(0), pl.program_id(1)
        return (p == 0) & (i == 0), (p == groups - 1) & (i == T // B - 1)

    return _pallas(
        body, comm=comm, edge=edge, name="sb_fwd", grid=(groups, T // B),
        in_specs=[pl.BlockSpec((B, W), lambda p, i: (i, p)),
                  pl.BlockSpec((T, W), lambda p, i: (0, groups + p)),
                  pl.BlockSpec((T, W), lambda p, i: (0, 2 * groups + p)),
                  pl.BlockSpec(usum.shape, lambda p, i: (0, 0))],
        out_specs=[pl.BlockSpec((B, W), lambda p, i: (i, p))] * 2 + [pl.BlockSpec(memory_space=pltpu.SMEM)],
        out_shape=[jax.ShapeDtypeStruct((T, SB_WIDTH), F32)] * 2 + [jax.ShapeDtypeStruct((groups, T // B), jnp.int32)],
        compiler_params=_params(("arbitrary", "arbitrary")),
    )(sqkv, sqkv, sqkv, usum)


def sb_bwd(sqkv, do, tl, first_block):
    T = sqkv.shape[0]
    B = SB_BLOCK
    W = SB_GROUP * B
    groups = SB_WIDTH // W
    upre = jnp.asarray(_sb_prefix_matrix(True), BF16)
    uexc = jnp.asarray(_sb_prefix_matrix(False), BF16)

    def body(q_ref, k_ref, v_ref, do_ref, tl_ref, up_ref, ue_ref, first_ref, dq_ref, dk_ref, dv_ref):
        p, i = pl.program_id(0), pl.program_id(1)

        @pl.when(i == 0)
        def _():
            dk_ref[...] = jnp.zeros(dk_ref.shape, F32)
            dv_ref[...] = jnp.zeros(dv_ref.shape, F32)

        row, lane = _sb_iotas()
        first = lane < SB_DH
        do = do_ref[...]
        tl_all = tl_ref[...]
        heads = []
        for (g, qh), at in zip(_sb_heads(q_ref[...], first), (0, B - 1) * SB_GROUP):
            dog = _lanes(do, g)
            keep = first if at == 0 else jnp.logical_not(first)
            heads.append((g, qh, jnp.where(keep, dog, jnp.zeros_like(dog)).astype(BF16),
                          _lanes(tl_all, g)[:, at:at + 1]))
        up = up_ref[...]
        ue = ue_ref[...]
        start = first_ref[p, i]

        def step(n, state):
            blocks = []
            for sub in range(SB_UNROLL):
                j = start + SB_UNROLL * n + sub
                off = pl.multiple_of(jnp.minimum(j, i) * B, B)
                valid = (lane + j * B) < (row + i * B)
                blocks.append((off, k_ref[pl.ds(off, B), :], v_ref[pl.ds(off, B), :], valid))
            combos = [(s, h) for s in range(SB_UNROLL) for h in range(len(heads))]
            z = {(s, h): _dot_nt(heads[h][1], _lanes(blocks[s][1], heads[h][0])) for s, h in combos}
            da = {(s, h): _dot_nt(heads[h][2], _lanes(blocks[s][2], heads[h][0])) for s, h in combos}
            lnb = {c: jnp.where(blocks[c[0]][3], _neg_softplus(z[c]), 0.0) for c in combos}
            lb = {c: z[c] + lnb[c] for c in combos}
            sums = {c: _sum01_right(lnb[c], up) for c in combos}
            a, w = {}, {}
            seen = [st[0] for st in state]
            for s, h in combos:
                expo = lb[s, h] + (heads[h][3] - seen[h] - sums[s, h][:, :B])
                a[s, h] = jnp.where(blocks[s][3], jnp.exp(expo), 0.0)
                w[s, h] = a[s, h] * da[s, h]
                seen[h] = seen[h] + sums[s, h][:, B:]
            wsums = {c: _sum01_right(w[c], ue) for c in combos}
            dz = {}
            seen_w = [st[1] for st in state]
            for s, h in combos:
                beta = jnp.exp(lb[s, h])
                before = seen_w[h] + wsums[s, h][:, :B]
                dz[s, h] = jnp.where(blocks[s][3], w[s, h] * (1.0 - beta) - before * beta, 0.0)
                seen_w[h] = seen_w[h] + wsums[s, h][:, B:]
            dq = [st[2] for st in state]
            for s, h in combos:
                dq[h] = dq[h] + _dot(dz[s, h], _lanes(blocks[s][1], heads[h][0]))
            for s in range(SB_UNROLL):
                off = blocks[s][0]
                for g in range(SB_GROUP):
                    h0, h1 = 2 * g, 2 * g + 1
                    dk_ref[pl.ds(off, B), B * g:B * (g + 1)] += (_dot_tn(dz[s, h0], heads[h0][1])
                                                                 + _dot_tn(dz[s, h1], heads[h1][1]))
                    dv_ref[pl.ds(off, B), B * g:B * (g + 1)] += (_dot_tn(a[s, h0], heads[h0][2])
                                                                 + _dot_tn(a[s, h1], heads[h1][2]))
            return tuple(zip(seen, seen_w, dq))

        zero = jnp.zeros((B, B), F32)
        trips = (i - start + SB_UNROLL) // SB_UNROLL
        state = lax.fori_loop(0, trips, step, tuple((zero, zero, zero) for _ in heads))
        for g in range(SB_GROUP):
            dq_ref[:, B * g:B * (g + 1)] = jnp.where(first, state[2 * g][2], state[2 * g + 1][2]) * SB_SCALE

    qblk = pl.BlockSpec((B, W), lambda p, i: (i, p))
    full = pl.BlockSpec((T, W), lambda p, i: (0, p))
    return pl.pallas_call(
        body, name="sb_bwd", grid=(groups, T // B),
        in_specs=[qblk, pl.BlockSpec((T, W), lambda p, i: (0, groups + p)),
                  pl.BlockSpec((T, W), lambda p, i: (0, 2 * groups + p)), qblk, qblk,
                  pl.BlockSpec(upre.shape, lambda p, i: (0, 0)), pl.BlockSpec(uexc.shape, lambda p, i: (0, 0)),
                  pl.BlockSpec(memory_space=pltpu.SMEM)],
        out_specs=[qblk, full, full],
        out_shape=[jax.ShapeDtypeStruct((T, SB_WIDTH), F32)] * 3,
        compiler_params=_params(("arbitrary", "arbitrary")),
    )(sqkv, sqkv, sqkv, do, tl, upre, uexc, first_block)


def _mixer_out(o_hg, hg, o_sb, g_hg, g_sb):
    n_hg, r_hg = _rms(o_hg)
    s_hg = _sigmoid(hg)
    n_sb, r_sb = _rms(o_sb)
    return dict(n_hg=n_hg, r_hg=r_hg, s_hg=s_hg, n_sb=n_sb, r_sb=r_sb,
                y_hg=n_hg * g_hg * (hg * s_hg), y_sb=n_sb * g_sb)


def mix_out_fwd(o_hg, proj_h, o_sb, x, norms, g_post, w_out, comm=None):
    T = x.shape[0]

    def body(ohg_ref, hg_ref, osb_ref, x_ref, nrm_ref, gp_ref, w_hbm, cat_ref, mix_ref, h1_ref, w_vmem):
        _load_once(w_hbm, w_vmem)
        nrm = nrm_ref[...]
        m = _mixer_out(ohg_ref[...], hg_ref[...], osb_ref[...], nrm[:, :HG_WIDTH], nrm[:, HG_WIDTH:])
        cat_ref[:, :HG_WIDTH] = m["y_hg"].astype(BF16)
        cat_ref[:, HG_WIDTH:] = m["y_sb"].astype(BF16)
        mix = jnp.dot(cat_ref[...], w_vmem[...], preferred_element_type=F32)
        mix_ref[...] = mix
        mh, _ = _rms(mix)
        h1_ref[...] = x_ref[...] + mh * gp_ref[...]

    return _pallas(
        body, comm=comm, edge=_grid_edge(T // _wide_tile(T)), name="mix_out_fwd", grid=(T // _wide_tile(T),),
        in_specs=[_wide_spec(HG_WIDTH), _wide_spec(HG_WIDTH, 3), _wide_spec(SB_WIDTH), _wide_spec(D_MODEL),
                  _full_spec((1, D_MODEL)), _full_spec((1, D_MODEL)), ANY_SPEC],
        out_specs=[_wide_spec(D_MODEL)] * 3,
        out_shape=[jax.ShapeDtypeStruct((T, D_MODEL), BF16), jax.ShapeDtypeStruct((T, D_MODEL), F32),
                   jax.ShapeDtypeStruct((T, D_MODEL), F32)],
        scratch_shapes=[pltpu.VMEM(w_out.shape, BF16)],
        compiler_params=_params(("arbitrary",)),
    )(o_hg, proj_h, o_sb, x, norms, g_post, w_out)


def ffn_fwd(h1, g_pre, g_post, w_gu, w_down):
    T = h1.shape[0]
    pw = w_gu.shape[2]

    def body(h1_ref, gpre_ref, gpost_ref, wgu_hbm, wd_hbm, u2_ref, gu_ref, act_ref, y_ref, h2_ref,
             wgu_vmem, wd_vmem, gu_s):
        _load_once(wgu_hbm, wgu_vmem)
        _load_once(wd_hbm, wd_vmem)
        h1v = h1_ref[...]
        hh, _ = _rms(h1v)
        u2 = (hh * gpre_ref[...]).astype(BF16)
        u2_ref[...] = u2
        for q in range(N_CHIPS):
            gu_s[:, pw * q:pw * (q + 1)] = jnp.dot(u2, wgu_vmem[q], preferred_element_type=F32)
        gu_ref[...] = gu_s[...].astype(BF16)
        gate = gu_s[:, :D_FF]
        act = (gate * _sigmoid(gate) * gu_s[:, D_FF:]).astype(BF16)
        act_ref[...] = act
        y = jnp.dot(act, wd_vmem[...], preferred_element_type=F32)
        y_ref[...] = y
        yh, _ = _rms(y)
        h2_ref[...] = h1v + yh * gpost_ref[...]

    return pl.pallas_call(
        body, name="ffn_fwd", grid=(T // ROW_TILE,),
        in_specs=[_row_spec(D_MODEL), _full_spec((1, D_MODEL)), _full_spec((1, D_MODEL)), ANY_SPEC, ANY_SPEC],
        out_specs=[_row_spec(D_MODEL), _row_spec(2 * D_FF), _row_spec(D_FF), _row_spec(D_MODEL), _row_spec(D_MODEL)],
        out_shape=[jax.ShapeDtypeStruct((T, D_MODEL), BF16), jax.ShapeDtypeStruct((T, 2 * D_FF), BF16),
                   jax.ShapeDtypeStruct((T, D_FF), BF16), jax.ShapeDtypeStruct((T, D_MODEL), F32),
                   jax.ShapeDtypeStruct((T, D_MODEL), F32)],
        scratch_shapes=[pltpu.VMEM(w_gu.shape, BF16), pltpu.VMEM(w_down.shape, BF16),
                        pltpu.VMEM((ROW_TILE, 2 * D_FF), F32)],
        compiler_params=_params(("arbitrary",)),
    )(h1, g_pre, g_post, w_gu, w_down)


def ple_loss(h2, p, target, w_ple, w_pg):
    T = h2.shape[0]
    pw = w_ple.shape[2]

    def body(h2_ref, p_ref, t_ref, wple_hbm, wpg_hbm, de_ref, ds_ref, dh2_ref, h2b_ref, pb_ref, pack_ref,
             wple_vmem, wpg_vmem, e_s):
        _load_once(wple_hbm, wple_vmem)
        _load_once(wpg_hbm, wpg_vmem)
        _zero_first(pack_ref)
        h2v = h2_ref[...]
        h2b = h2v.astype(BF16)
        h2b_ref[...] = h2b
        pb = p_ref[...].astype(BF16)
        pb_ref[...] = pb
        for q in range(N_CHIPS):
            e_s[:, pw * q:pw * (q + 1)] = jnp.dot(pb, wple_vmem[q], preferred_element_type=F32)
        e = e_s[...]
        sig = _sigmoid(jnp.dot(h2b, wpg_vmem[...], preferred_element_type=F32))
        err = h2v + e * sig - t_ref[...]
        part = 0.5 * jnp.sum(jnp.mean(err * err, axis=-1, keepdims=True), axis=0, keepdims=True)
        lane = lax.broadcasted_iota(jnp.int32, (1, D_MODEL), 1)
        pack_ref[ROW_LOSS:ROW_LOSS + 1, :] += jnp.where(lane == 0, part, 0.0)
        dh3 = err * (1.0 / D_MODEL)
        de_ref[...] = (dh3 * sig).astype(BF16)
        ds = (dh3 * e * sig * (1.0 - sig)).astype(BF16)
        ds_ref[...] = ds
        dh2_ref[...] = dh3 + _dot_nt(ds, wpg_vmem[...])

    return pl.pallas_call(
        body, name="ple_loss", grid=(T // _wide_tile(T),),
        in_specs=[_wide_spec(D_MODEL), _wide_spec(p.shape[1]), _wide_spec(D_MODEL), ANY_SPEC, ANY_SPEC],
        out_specs=[_wide_spec(D_MODEL), _wide_spec(D_MODEL), _wide_spec(D_MODEL), _wide_spec(D_MODEL),
                   _wide_spec(p.shape[1]), PACK_SPEC],
        out_shape=[jax.ShapeDtypeStruct((T, D_MODEL), BF16), jax.ShapeDtypeStruct((T, D_MODEL), BF16),
                   jax.ShapeDtypeStruct((T, D_MODEL), F32), jax.ShapeDtypeStruct((T, D_MODEL), BF16),
                   jax.ShapeDtypeStruct(p.shape, BF16), jax.ShapeDtypeStruct((8, D_MODEL), F32)],
        scratch_shapes=[pltpu.VMEM(w_ple.shape, BF16), pltpu.VMEM(w_pg.shape, BF16), pltpu.VMEM((_wide_tile(T), D_MODEL), F32)],
        compiler_params=_params(("arbitrary",)),
    )(h2, p, target, w_ple, w_pg)


def ffn_bwd(dh2, y, h1, gu, g_pre, g_post, w_gu, w_down):
    T = h1.shape[0]
    pw = w_gu.shape[2]

    def body(dh2_ref, y_ref, h1_ref, gu_ref, gpre_ref, gpost_ref, wgu_hbm, wd_hbm, dy_ref, dgu_ref, dh1_ref, pack_ref,
             wgu_vmem, wd_vmem):
        _load_once(wgu_hbm, wgu_vmem)
        _load_once(wd_hbm, wd_vmem)
        _zero_first(pack_ref)
        dh2v = dh2_ref[...]
        yh, ry = _rms(y_ref[...])
        dy, dw = _rms_bwd(dh2v, yh, ry, gpost_ref[...])
        pack_ref[ROW_FFN_POST:ROW_FFN_POST + 1, :] += _colsum(dw)
        dyb = dy.astype(BF16)
        dy_ref[...] = dyb
        dact = _dot_nt(dyb, wd_vmem[...])
        gate = gu_ref[:, :D_FF].astype(F32)
        up = gu_ref[:, D_FF:].astype(F32)
        sg = _sigmoid(gate)
        dgu_ref[:, :D_FF] = (dact * up * (sg * (1.0 + gate * (1.0 - sg)))).astype(BF16)
        dgu_ref[:, D_FF:] = (dact * gate * sg).astype(BF16)
        du2 = _dot_nt(dgu_ref[:, :pw], wgu_vmem[0])
        for q in range(1, N_CHIPS):
            du2 = du2 + _dot_nt(dgu_ref[:, pw * q:pw * (q + 1)], wgu_vmem[q])
        hh, rh = _rms(h1_ref[...])
        dh, dw = _rms_bwd(du2, hh, rh, gpre_ref[...])
        pack_ref[ROW_FFN_PRE:ROW_FFN_PRE + 1, :] += _colsum(dw)
        dh1_ref[...] = dh2v + dh

    return pl.pallas_call(
        body, name="ffn_bwd", grid=(T // ROW_TILE,),
        in_specs=[_row_spec(D_MODEL), _row_spec(D_MODEL), _row_spec(D_MODEL), _row_spec(2 * D_FF),
                  _full_spec((1, D_MODEL)), _full_spec((1, D_MODEL)), ANY_SPEC, ANY_SPEC],
        out_specs=[_row_spec(D_MODEL), _row_spec(2 * D_FF), _row_spec(D_MODEL), PACK_SPEC],
        out_shape=[jax.ShapeDtypeStruct((T, D_MODEL), BF16), jax.ShapeDtypeStruct((T, 2 * D_FF), BF16),
                   jax.ShapeDtypeStruct((T, D_MODEL), F32), jax.ShapeDtypeStruct((8, D_MODEL), F32)],
        scratch_shapes=[pltpu.VMEM(w_gu.shape, BF16), pltpu.VMEM(w_down.shape, BF16)],
        compiler_params=_params(("arbitrary",)),
    )(dh2, y, h1, gu, g_pre, g_post, w_gu, w_down)


def mix_out_bwd(dh1, mix, o_hg, proj_h, o_sb, norms, g_post, w_out, comm=None):
    T = dh1.shape[0]

    def body(dh1_ref, mix_ref, ohg_ref, hg_ref, osb_ref, nrm_ref, gp_ref, w_hbm, dmix_ref, dohg_ref, dhg_ref, dosb_ref,
             pack_ref, w_vmem):
        _load_once(w_hbm, w_vmem)
        _zero_first(pack_ref)
        mh, rm = _rms(mix_ref[...])
        dmix, dw = _rms_bwd(dh1_ref[...], mh, rm, gp_ref[...])
        pack_ref[ROW_ATTN_POST:ROW_ATTN_POST + 1, :] += _colsum(dw)
        dmb = dmix.astype(BF16)
        dmix_ref[...] = dmb
        dcat = _dot_nt(dmb, w_vmem[...])
        nrm = nrm_ref[...]
        g_hg, g_sb = nrm[:, :HG_WIDTH], nrm[:, HG_WIDTH:]
        hg = hg_ref[...]
        m = _mixer_out(ohg_ref[...], hg, osb_ref[...], g_hg, g_sb)
        d_hg = dcat[:, :HG_WIDTH]
        silu = hg * m["s_hg"]
        dhg_ref[...] = d_hg * (m["n_hg"] * g_hg) * (m["s_hg"] * (1.0 + hg * (1.0 - m["s_hg"])))
        dx, dw = _rms_bwd(d_hg * silu, m["n_hg"], m["r_hg"], g_hg)
        dohg_ref[...] = dx
        pack_ref[ROW_MIX_NORMS:ROW_MIX_NORMS + 1, :HG_WIDTH] += _colsum(dw)
        dx, dw = _rms_bwd(dcat[:, HG_WIDTH:], m["n_sb"], m["r_sb"], g_sb)
        dosb_ref[...] = dx
        pack_ref[ROW_MIX_NORMS:ROW_MIX_NORMS + 1, HG_WIDTH:] += _colsum(dw)

    return _pallas(
        body, comm=comm, edge=_grid_edge(T // _wide_tile(T)), name="mix_out_bwd", grid=(T // _wide_tile(T),),
        in_specs=[_wide_spec(D_MODEL), _wide_spec(D_MODEL), _wide_spec(HG_WIDTH), _wide_spec(HG_WIDTH, 3), _wide_spec(SB_WIDTH),
                  _full_spec((1, D_MODEL)), _full_spec((1, D_MODEL)), ANY_SPEC],
        out_specs=[_wide_spec(D_MODEL), _wide_spec(HG_WIDTH), _wide_spec(HG_WIDTH), _wide_spec(SB_WIDTH), PACK_SPEC],
        out_shape=[jax.ShapeDtypeStruct((T, D_MODEL), BF16), jax.ShapeDtypeStruct((T, HG_WIDTH), F32),
                   jax.ShapeDtypeStruct((T, HG_WIDTH), F32), jax.ShapeDtypeStruct((T, SB_WIDTH), F32),
                   jax.ShapeDtypeStruct((8, D_MODEL), F32)],
        scratch_shapes=[pltpu.VMEM(w_out.shape, BF16)],
        compiler_params=_params(("arbitrary",)),
    )(dh1, mix, o_hg, proj_h, o_sb, norms, g_post, w_out)


def in_proj_bwd(parts, x, dh1, g_pre, w_in, comm=None):
    T = x.shape[0]
    pw = w_in.shape[2]
    n_parts = len(parts)

    def body(*refs):
        part_refs = refs[:n_parts]
        x_ref, dh1_ref, g_ref, w_hbm, dproj_ref, dx_ref, pack_ref, w_vmem = refs[n_parts:]
        _load_once(w_hbm, w_vmem)
        _zero_first(pack_ref)
        for n, ref in enumerate(part_refs):
            dproj_ref[:, HG_WIDTH * n:HG_WIDTH * (n + 1)] = ref[...].astype(BF16)
        du = _dot_nt(dproj_ref[:, :pw], w_vmem[0])
        for q in range(1, N_CHIPS):
            du = du + _dot_nt(dproj_ref[:, pw * q:pw * (q + 1)], w_vmem[q])
        xh, r = _rms(x_ref[...])
        dx, dw = _rms_bwd(du, xh, r, g_ref[...])
        pack_ref[ROW_ATTN_PRE:ROW_ATTN_PRE + 1, :] += _colsum(dw)
        dx_ref[...] = dh1_ref[...] + dx

    return _pallas(
        body, comm=comm, edge=_grid_edge(T // _wide_tile(T)), name="in_proj_bwd", grid=(T // _wide_tile(T),),
        in_specs=[_wide_spec(HG_WIDTH)] * n_parts + [_wide_spec(D_MODEL), _wide_spec(D_MODEL), _full_spec((1, D_MODEL)), ANY_SPEC],
        out_specs=[_wide_spec(n_parts * HG_WIDTH), _wide_spec(D_MODEL), PACK_SPEC],
        out_shape=[jax.ShapeDtypeStruct((T, n_parts * HG_WIDTH), BF16), jax.ShapeDtypeStruct((T, D_MODEL), F32),
                   jax.ShapeDtypeStruct((8, D_MODEL), F32)],
        scratch_shapes=[pltpu.VMEM(w_in.shape, BF16)],
        compiler_params=_params(("arbitrary",)),
    )(*parts, x, dh1, g_pre, w_in)


def weight_grad(a, g, name, *, tm, tn, tk=512, col_pieces=False, comm=None):
    T, M = a.shape
    N = g.shape[1]
    tk = min(tk, T)
    steps = T // tk

    def body(a_ref, g_ref, o_ref):
        @pl.when(pl.program_id(2) == 0)
        def _():
            o_ref[...] = jnp.zeros(o_ref.shape, F32)

        o_ref[...] += _dot_tn(a_ref[...], g_ref[...]).reshape(o_ref.shape)

    if col_pieces:
        out_shape = jax.ShapeDtypeStruct((N // tn, M, tn), F32)
        out_spec = pl.BlockSpec((1, tm, tn), lambda i, j, k: (j, i, 0))
    else:
        out_shape = jax.ShapeDtypeStruct((M, N), F32)
        out_spec = pl.BlockSpec((tm, tn), lambda i, j, k: (i, j))
    grid = (M // tm, N // tn, steps)

    def edge():
        at = [pl.program_id(d) for d in range(3)]
        return ((at[0] == 0) & (at[1] == 0) & (at[2] == 0),
                (at[0] == grid[0] - 1) & (at[1] == grid[1] - 1) & (at[2] == grid[2] - 1))

    return _pallas(
        body, comm=comm, edge=edge, name=name, grid=grid,
        in_specs=[pl.BlockSpec((tk, tm), lambda i, j, k: (k, i)), pl.BlockSpec((tk, tn), lambda i, j, k: (k, j))],
        out_specs=[out_spec], out_shape=[out_shape],
        compiler_params=_params(("arbitrary", "arbitrary", "arbitrary")),
    )(a, g)


def _place():
    x, y, c = lax.axis_index("x"), lax.axis_index("y"), lax.axis_index("c")
    chips = [(1 - x, y), (x, 1 - y), (1 - x, 1 - y)]
    return x, y, c, chips


def _chip_index(cx, cy):
    return 2 * cx + cy


def _own_slot(piece, slots):
    me = _chip_index(lax.axis_index("x"), lax.axis_index("y"))
    landing = lax.empty((slots,) + piece.shape[1:], piece.dtype)
    return lax.dynamic_update_slice(landing, piece, (me,) + (0,) * (piece.ndim - 1))


def _rcopy(src, dst, send_sem, recv_sem, device):
    return pltpu.make_async_remote_copy(src_ref=src, dst_ref=dst, send_sem=send_sem, recv_sem=recv_sem,
                                        device_id=device, device_id_type=MESH)


def gather_weights(shards):
    n = len(shards)

    def body(*refs):
        ins, outs = refs[:n], refs[2 * n:3 * n]
        send_sems, recv_sems = refs[3 * n:]
        x, y, c, chips = _place()
        me = _chip_index(x, y)
        sibling = (x, y, 1 - c)

        def rows(w, core):
            half = ins[w].shape[0] // 2
            return pl.ds(core * half, half)

        sends = []
        for w in range(n):
            for j, chip in enumerate(chips):
                sends.append(_rcopy(ins[w].at[rows(w, c)], outs[w].at[me, rows(w, c)],
                                    send_sems.at[6 * w + j], recv_sems.at[6 * w + j], (*chip, c)))
        for cp in sends:
            cp.start()
        passed = []
        for w in range(n):
            for j, chip in enumerate(chips):
                block = outs[w].at[_chip_index(*chip), rows(w, c)]
                _rcopy(block, block, send_sems.at[6 * w + j], recv_sems.at[6 * w + j], (*chip, c)).wait_recv()
                cp = _rcopy(block, block, send_sems.at[6 * w + 3 + j], recv_sems.at[6 * w + 3 + j], sibling)
                cp.start()
                passed.append(cp)
        for w in range(n):
            for j, chip in enumerate(chips):
                block = outs[w].at[_chip_index(*chip), rows(w, 1 - c)]
                _rcopy(block, block, send_sems.at[6 * w + 3 + j], recv_sems.at[6 * w + 3 + j], sibling).wait_recv()
        for cp in sends + passed:
            cp.wait_send()

    filled = [_own_slot(s[None], N_CHIPS) for s in shards]
    return pl.pallas_call(
        body, name="gather_weights",
        in_specs=[ANY_SPEC] * (2 * n), out_specs=[ANY_SPEC] * n,
        out_shape=[jax.ShapeDtypeStruct(f.shape, f.dtype) for f in filled],
        input_output_aliases={n + w: w for w in range(n)},
        scratch_shapes=[pltpu.SemaphoreType.DMA((6 * n,)), pltpu.SemaphoreType.DMA((6 * n,))],
    )(*shards, *filled)


def _run_comm(comm, name):
    c_in, c_out = len(comm.inputs), len(comm.out_shape)

    def body(*refs):
        parts = refs[:c_in], refs[c_in:c_in + c_out], refs[c_in + c_out:]
        comm.start(*parts)
        comm.finish(*parts)

    return pl.pallas_call(
        body, name=name, in_specs=[ANY_SPEC] * c_in, out_specs=[ANY_SPEC] * c_out, out_shape=comm.out_shape,
        scratch_shapes=comm.scratch, input_output_aliases=comm.aliases)(*comm.inputs)


def _both(first, second):
    n_in, n_out, n_scr = len(first.inputs), len(first.out_shape), len(first.scratch)

    def split(ins, outs, scr):
        return (ins[:n_in], outs[:n_out], scr[:n_scr]), (ins[n_in:], outs[n_out:], scr[n_scr:])

    def start(*refs):
        a, b = split(*refs)
        first.start(*a)
        second.start(*b)

    def finish(*refs):
        a, b = split(*refs)
        first.finish(*a)
        second.finish(*b)

    aliases = dict(first.aliases)
    aliases.update({n_in + i: n_out + o for i, o in second.aliases.items()})
    return Comm(first.inputs + second.inputs, first.out_shape + second.out_shape, aliases,
                first.scratch + second.scratch, start, finish)


def _dma_sems(count):
    return [pltpu.SemaphoreType.DMA((count,)), pltpu.SemaphoreType.DMA((count,))]


def gather_over_ici(shards):
    n = len(shards)

    def copies(ins, outs, sems):
        send_sems, recv_sems = sems
        x, y, c, chips = _place()
        me = _chip_index(x, y)
        pairs = []
        for w in range(n):
            half = shards[w].shape[0] // 2
            rows = pl.ds(c * half, half)
            for j, chip in enumerate(chips):
                k = 3 * w + j
                landed = outs[w].at[_chip_index(*chip), rows]
                pairs.append((_rcopy(ins[w].at[rows], outs[w].at[me, rows], send_sems.at[k], recv_sems.at[k], (*chip, c)),
                              _rcopy(landed, landed, send_sems.at[k], recv_sems.at[k], (*chip, c))))
        return pairs

    def start(*refs):
        for send, _ in copies(*refs):
            send.start()

    def finish(*refs):
        pairs = copies(*refs)
        for _, landed in pairs:
            landed.wait_recv()
        for send, _ in pairs:
            send.wait_send()

    filled = [_own_slot(s[None], N_CHIPS) for s in shards]
    return Comm(list(shards) + filled, [jax.ShapeDtypeStruct(f.shape, f.dtype) for f in filled],
                {n + w: w for w in range(n)}, _dma_sems(3 * n), start, finish)


def gather_over_d2d(landed):
    n = len(landed)

    def copies(ins, outs, sems):
        send_sems, recv_sems = sems
        x, y, c, chips = _place()
        sibling = (x, y, 1 - c)
        pairs = []
        for w in range(n):
            half = landed[w].shape[1] // 2
            for j, chip in enumerate(chips):
                k = 3 * w + j
                mine = outs[w].at[_chip_index(*chip), pl.ds(c * half, half)]
                theirs = outs[w].at[_chip_index(*chip), pl.ds((1 - c) * half, half)]
                pairs.append((_rcopy(mine, mine, send_sems.at[k], recv_sems.at[k], sibling),
                              _rcopy(theirs, theirs, send_sems.at[k], recv_sems.at[k], sibling)))
        return pairs

    def start(*refs):
        for send, _ in copies(*refs):
            send.start()

    def finish(*refs):
        pairs = copies(*refs)
        for _, arrived in pairs:
            arrived.wait_recv()
        for send, _ in pairs:
            send.wait_send()

    return Comm(list(landed), [jax.ShapeDtypeStruct(a.shape, a.dtype) for a in landed], {w: w for w in range(n)},
                _dma_sems(3 * n), start, finish)


def core_halves(grads):
    n = len(grads)

    def copies(ins, outs, sems):
        send_sems, recv_sems = sems
        x, y, c, _ = _place()
        out = []
        for w in range(n):
            half = grads[w].shape[1] // 2
            out.append(_rcopy(ins[w].at[:, pl.ds((1 - c) * half, half), :], outs[w],
                              send_sems.at[w], recv_sems.at[w], (x, y, 1 - c)))
        return out

    def start(*refs):
        for cp in copies(*refs):
            cp.start()

    def finish(*refs):
        for cp in copies(*refs):
            cp.wait()

    return Comm(list(grads), [jax.ShapeDtypeStruct((g.shape[0], g.shape[1] // 2, g.shape[2]), g.dtype) for g in grads],
                {}, _dma_sems(n), start, finish)


def chip_partials(pairs):
    partials = [p for p, _ in pairs]
    filled = [landing for _, landing in pairs]
    n = len(partials)

    def copies(ins, outs, sems):
        send_sems, recv_sems = sems
        x, y, c, chips = _place()
        me = _chip_index(x, y)
        pairs = []
        for w in range(n):
            for j, chip in enumerate(chips):
                k = 3 * w + j
                landed = outs[w].at[_chip_index(*chip)]
                pairs.append((_rcopy(ins[w].at[_chip_index(*chip)], outs[w].at[me], send_sems.at[k], recv_sems.at[k],
                                     (*chip, c)),
                              _rcopy(landed, landed, send_sems.at[k], recv_sems.at[k], (*chip, c))))
        return pairs

    def start(*refs):
        for send, _ in copies(*refs):
            send.start()

    def finish(*refs):
        pairs = copies(*refs)
        for _, landed in pairs:
            landed.wait_recv()
        for send, _ in pairs:
            send.wait_send()

    return Comm(list(partials) + filled, [jax.ShapeDtypeStruct(p.shape, p.dtype) for p in partials],
                {n + w: w for w in range(n)}, _dma_sems(3 * n), start, finish)


def join_core_halves(grads):
    n = len(grads)

    def body(*refs):
        outs = refs[n:2 * n]
        send_sems, recv_sems = refs[2 * n:]
        x, y, c, _ = _place()
        sibling = (x, y, 1 - c)
        copies = []
        for w in range(n):
            half = outs[w].shape[0] // 2
            mine = outs[w].at[pl.ds(c * half, half), :]
            copies.append(_rcopy(mine, mine, send_sems.at[w], recv_sems.at[w], sibling))
        for cp in copies:
            cp.start()
        for w in range(n):
            half = outs[w].shape[0] // 2
            theirs = outs[w].at[pl.ds((1 - c) * half, half), :]
            _rcopy(theirs, theirs, send_sems.at[w], recv_sems.at[w], sibling).wait_recv()
        for cp in copies:
            cp.wait_send()

    return pl.pallas_call(
        body, name="join_core_halves",
        in_specs=[ANY_SPEC] * n, out_specs=[ANY_SPEC] * n,
        out_shape=[jax.ShapeDtypeStruct(g.shape, g.dtype) for g in grads],
        input_output_aliases={w: w for w in range(n)},
        scratch_shapes=[pltpu.SemaphoreType.DMA((n,)), pltpu.SemaphoreType.DMA((n,))],
    )(*grads)


def _elementwise_rows(rows, cap=512):
    for t in range(min(rows, cap), 0, -8):
        if rows % t == 0 and t % 16 == 0:
            return t
    return rows


def add_core_halves(grad, got, core, name):
    _, rows, cols = got.shape
    tr = _elementwise_rows(rows)
    nt = rows // tr

    def body(core_ref, a_ref, b_ref, o_ref, landing_ref):
        o_ref[...] = (a_ref[...] + b_ref[...]).astype(BF16)
        landing_ref[...] = o_ref[...]

    spec = pl.BlockSpec((1, tr, cols), lambda q, i, core_ref: (q, i, 0))
    own = pl.BlockSpec((1, tr, cols), lambda q, i, core_ref: (q, core_ref[0] * nt + i, 0))
    return pl.pallas_call(
        body, name=name,
        grid_spec=pltpu.PrefetchScalarGridSpec(num_scalar_prefetch=1, grid=(N_CHIPS, nt), in_specs=[own, spec],
                                               out_specs=[spec, spec]),
        out_shape=[jax.ShapeDtypeStruct(got.shape, BF16)] * 2,
        compiler_params=_params(("arbitrary", "arbitrary")),
    )(core, grad, got)


def add_chip_partials(parts, core, name):
    _, rows, cols = parts.shape
    tr = _elementwise_rows(rows)
    nt = rows // tr

    def body(core_ref, p_ref, o_ref):
        acc = p_ref[0].astype(F32)
        for q in range(1, N_CHIPS):
            acc = acc + p_ref[q].astype(F32)
        o_ref[...] = acc

    return pl.pallas_call(
        body, name=name,
        grid_spec=pltpu.PrefetchScalarGridSpec(
            num_scalar_prefetch=1, grid=(nt,),
            in_specs=[pl.BlockSpec((N_CHIPS, tr, cols), lambda i, core_ref: (0, i, 0))],
            out_specs=pl.BlockSpec((tr, cols), lambda i, core_ref: (core_ref[0] * nt + i, 0))),
        out_shape=jax.ShapeDtypeStruct((2 * rows, cols), F32),
        compiler_params=_params(("arbitrary",)),
    )(core, parts)


def _adamw_math(w, g, m, v):
    m = ADAM_B1 * m + (1.0 - ADAM_B1) * g
    v = ADAM_B2 * v + (1.0 - ADAM_B2) * (g * g)
    m_hat = m / (1.0 - ADAM_B1 ** ADAM_STEP)
    v_hat = v / (1.0 - ADAM_B2 ** ADAM_STEP)
    delta = -ADAM_LR * (m_hat / (jnp.sqrt(v_hat) + ADAM_EPS) + ADAM_WD * w)
    return delta, m, v


def adamw(w, g, m, v, name):
    rows, cols = w.shape
    tr = _elementwise_rows(rows, 256)

    def body(w_ref, g_ref, m_ref, v_ref, g_out_ref, d_ref, nm_ref, nv_ref):
        d, nm, nv = _adamw_math(w_ref[...], g_ref[...], m_ref[...], v_ref[...])
        g_out_ref[...] = g_ref[...]
        d_ref[...] = d
        nm_ref[...] = nm
        nv_ref[...] = nv

    spec = pl.BlockSpec((tr, cols), lambda i: (i, 0))
    return pl.pallas_call(
        body, name=name, grid=(rows // tr,), in_specs=[spec] * 4, out_specs=[spec] * 4,
        out_shape=[jax.ShapeDtypeStruct((rows, cols), F32)] * 4,
        compiler_params=_params(("arbitrary",)),
    )(w, g, m, v)


def reduce_small(packs, w, m, v):
    n = len(packs)
    n_dev = 8
    flips = [(fx, fy, fc) for fx in (0, 1) for fy in (0, 1) for fc in (0, 1)][1:]

    def body(*refs):
        pack_refs = refs[:n]
        w_ref, m_ref, v_ref, g_out, d_out, m_out, v_out, mine, slots, send_sems, recv_sems = refs[n:]
        x, y, c, _ = _place()
        me = 4 * x + 2 * y + c
        acc = pack_refs[0][...]
        for ref in pack_refs[1:]:
            acc = acc + ref[...]
        mine[...] = acc
        sends = []
        for k, (fx, fy, fc) in enumerate(flips):
            peer = (x ^ fx, y ^ fy, c ^ fc)
            sends.append(_rcopy(mine, slots.at[me], send_sems.at[k], recv_sems.at[me], peer))
        for cp in sends:
            cp.start()
        slots[me] = acc
        for fx, fy, fc in flips:
            src = 4 * (x ^ fx) + 2 * (y ^ fy) + (c ^ fc)
            _rcopy(mine, slots.at[src], send_sems.at[0], recv_sems.at[src], (x, y, c)).wait_recv()
        for cp in sends:
            cp.wait_send()
        total = slots[0]
        for d in range(1, n_dev):
            total = total + slots[d]
        g_out[...] = total
        d, nm, nv = _adamw_math(w_ref[...], total, m_ref[...], v_ref[...])
        d_out[...] = d
        m_out[...] = nm
        v_out[...] = nv

    vm = pl.BlockSpec(memory_space=pltpu.VMEM)
    return pl.pallas_call(
        body, name="reduce_small",
        in_specs=[vm] * (n + 3), out_specs=[vm] * 4,
        out_shape=[jax.ShapeDtypeStruct((8, D_MODEL), F32)] * 4,
        scratch_shapes=[pltpu.VMEM((8, D_MODEL), F32), pltpu.VMEM((n_dev, 8, D_MODEL), F32),
                        pltpu.SemaphoreType.DMA((len(flips),)), pltpu.SemaphoreType.DMA((n_dev,))],
    )(*packs, w, m, v)


def _column_pieces(g):
    return g.reshape(g.shape[0], N_CHIPS, g.shape[1] // N_CHIPS).transpose(1, 0, 2)


def _pack_small(attn_pre, gamma, hg_norm, sb_norm, attn_post, ffn_pre, ffn_post):
    rows = [attn_pre, gamma.reshape(1, D_MODEL), jnp.concatenate([hg_norm, sb_norm], axis=1), attn_post, ffn_pre, ffn_post,
            jnp.zeros((2, D_MODEL), F32)]
    return jnp.concatenate(rows, axis=0)


def _unpack_small(pack):
    return (pack[ROW_ATTN_PRE:ROW_ATTN_PRE + 1], pack[ROW_GAMMA].reshape(2, HG_WIDTH),
            pack[ROW_MIX_NORMS:ROW_MIX_NORMS + 1, :HG_WIDTH], pack[ROW_MIX_NORMS:ROW_MIX_NORMS + 1, HG_WIDTH:],
            pack[ROW_ATTN_POST:ROW_ATTN_POST + 1], pack[ROW_FFN_PRE:ROW_FFN_PRE + 1], pack[ROW_FFN_POST:ROW_FFN_POST + 1])


def kernel(x, p, attn_pre_norm, w_in, hg_lower_gamma, hg_out_norm, sb_out_norm, w_out, attn_post_norm, ffn_pre_norm, w_gate_up, w_down, ffn_post_norm, ple_proj, ple_gate, loss_target, m_attn_pre_norm, m_w_in, m_hg_lower_gamma, m_hg_out_norm, m_sb_out_norm, m_w_out, m_attn_post_norm, m_ffn_pre_norm, m_w_gate_up, m_w_down, m_ffn_post_norm, m_ple_proj, m_ple_gate, v_attn_pre_norm, v_w_in, v_hg_lower_gamma, v_hg_out_norm, v_sb_out_norm, v_w_out, v_attn_post_norm, v_ffn_pre_norm, v_w_gate_up, v_w_down, v_ffn_post_norm, v_ple_proj, v_ple_gate):
    x2 = x[0]
    p2 = p[0, 0]
    target = loss_target[0]
    big = dict(w_in=(w_in, m_w_in, v_w_in), w_out=(w_out, m_w_out, v_w_out), w_gate_up=(w_gate_up, m_w_gate_up, v_w_gate_up),
               w_down=(w_down, m_w_down, v_w_down), ple_proj=(ple_proj, m_ple_proj, v_ple_proj),
               ple_gate=(ple_gate, m_ple_gate, v_ple_gate))
    names = list(big)
    big = {k: tuple(a[0] for a in t) for k, t in big.items()}

    shard16 = {k: big[k][0].astype(BF16) for k in names}
    w_in_full, = gather_weights([shard16["w_in"]])
    mix_norms = jnp.concatenate([hg_out_norm, sb_out_norm], axis=1)
    small_ones = ["w_out", "ple_proj", "ple_gate"]

    proj_h, sqkv, u1, *landed_small = in_proj_fwd(
        x2, attn_pre_norm, w_in_full, comm=gather_over_ici([shard16[k] for k in small_ones]))
    o_sb, sb_totals, sb_first, landed_gu = sb_fwd(sqkv, comm=gather_over_ici([shard16["w_gate_up"]]))
    o_hg, states, landed_down, *full_small = hgrn2_fwd(
        proj_h, hg_lower_gamma, comm=_both(gather_over_ici([shard16["w_down"]]), gather_over_d2d(landed_small)))
    full = dict(zip(small_ones, full_small), w_in=w_in_full)
    w_out_full = full["w_out"].reshape(D_MODEL, D_MODEL)
    w_pg_full = full["ple_gate"].reshape(D_MODEL, D_MODEL)
    cat, mix, h1, full["w_gate_up"], full["w_down"] = mix_out_fwd(
        o_hg, proj_h, o_sb, x2, mix_norms, attn_post_norm, w_out_full, comm=gather_over_d2d([landed_gu, landed_down]))
    w_down_full = full["w_down"].reshape(D_FF, D_MODEL)
    u2, gu, act, y, h2 = ffn_fwd(h1, ffn_pre_norm, ffn_post_norm, full["w_gate_up"], w_down_full)

    core = lax.axis_index("c").astype(jnp.int32).reshape(1)
    de, ds, dh2, h2b, pb, pack_loss = ple_loss(h2, p2, target, full["ple_proj"], w_pg_full)
    dy, dgu, dh1, pack_ffn = ffn_bwd(dh2, y, h1, gu, ffn_pre_norm, ffn_post_norm, full["w_gate_up"], w_down_full)
    local = {}
    local["w_gate_up"], = weight_grad(u2, dgu, "grad_w_gate_up", tm=D_MODEL, tn=full["w_gate_up"].shape[2], tk=1024,
                                      col_pieces=True)
    grad_down, got_gu = weight_grad(act, dy, "grad_w_down", tm=D_FF // 2, tn=D_MODEL, tk=1024,
                                    comm=core_halves([local["w_gate_up"]]))
    local["w_down"] = grad_down.reshape(full["w_down"].shape)
    local["ple_proj"] = _column_pieces(weight_grad(pb, de, "grad_ple_proj", tm=pb.shape[1], tn=D_MODEL, tk=1024)[0])
    grad_pg, got_down = weight_grad(h2b, ds, "grad_ple_gate", tm=D_MODEL, tn=D_MODEL, tk=1024,
                                    comm=core_halves([local["w_down"]]))
    local["ple_gate"] = grad_pg.reshape(full["ple_gate"].shape)
    early = list(local)
    dmix, do_hg, dhg, do_sb, pack_mix, *got_ple = mix_out_bwd(
        dh1, mix, o_hg, proj_h, o_sb, mix_norms, attn_post_norm, w_out_full,
        comm=core_halves([local["ple_proj"], local["ple_gate"]]))
    got = [got_gu, got_down] + got_ple
    partial = [add_core_halves(local[k], g, core, "add_core_halves_" + k) for k, g in zip(early, got)]
    local["w_out"] = weight_grad(cat, dmix, "grad_w_out", tm=D_MODEL, tn=D_MODEL, tk=1024)[0].reshape(full["w_out"].shape)
    dsq, dsk, dsv = sb_bwd(sqkv, do_sb, sb_totals, sb_first)
    dhq, dhf, dhi, pack_hg, *by_source, got_out = hgrn2_bwd(
        proj_h, hg_lower_gamma, states, do_hg, comm=_both(chip_partials(partial[:2]), core_halves([local["w_out"]])))
    early.append("w_out")
    partial.append(add_core_halves(local["w_out"], got_out, core, "add_core_halves_w_out"))
    dproj, grad_x, pack_in = in_proj_bwd([dhq, dhf, dhi, dhg, dsq, dsk, dsv], x2, dh1, attn_pre_norm, full["w_in"])

    late = ["w_in"]
    local["w_in"], *more = weight_grad(u1, dproj, "grad_w_in", tm=D_MODEL, tn=full["w_in"].shape[2], tk=1024,
                                       col_pieces=True, comm=chip_partials(partial[2:]))
    halves = {k: add_chip_partials(s, core, "add_chip_partials_" + k) for k, s in zip(early, by_source + more)}
    got = _run_comm(core_halves([local[k] for k in late]), "exchange_core_halves")
    partial = [add_core_halves(local[k], g, core, "add_core_halves_" + k) for k, g in zip(late, got)]
    by_source = _run_comm(chip_partials(partial), "exchange_chip_partials")
    halves.update({k: add_chip_partials(s, core, "add_chip_partials_" + k) for k, s in zip(late, by_source)})
    grads = dict(zip(names, join_core_halves([halves[k] for k in names])))

    upd = {k: adamw(big[k][0], grads[k], big[k][1], big[k][2], "adamw_" + k) for k in names}

    small = reduce_small(
        [pack_loss, pack_ffn, pack_mix, pack_hg, pack_in],
        _pack_small(attn_pre_norm, hg_lower_gamma, hg_out_norm, sb_out_norm, attn_post_norm, ffn_pre_norm, ffn_post_norm),
        _pack_small(m_attn_pre_norm, m_hg_lower_gamma, m_hg_out_norm, m_sb_out_norm, m_attn_post_norm, m_ffn_pre_norm, m_ffn_post_norm),
        _pack_small(v_attn_pre_norm, v_hg_lower_gamma, v_hg_out_norm, v_sb_out_norm, v_attn_post_norm, v_ffn_pre_norm, v_ffn_post_norm),
    )
    loss = small[0][ROW_LOSS, 0]
    s_grad, s_delta, s_m, s_v = (_unpack_small(t) for t in small)

    def ordered(small_vals, big_vals):
        a_pre, gam, hg_n, sb_n, a_post, f_pre, f_post = small_vals
        b = {k: big_vals[k][None] for k in names}
        return (a_pre, b["w_in"], gam, hg_n, sb_n, b["w_out"], a_post, f_pre, b["w_gate_up"], b["w_down"], f_post,
                b["ple_proj"], b["ple_gate"])

    return (loss, grad_x[None],
            *ordered(s_grad, {k: upd[k][0] for k in names}),
            *ordered(s_delta, {k: upd[k][1] for k in names}),
            *ordered(s_m, {k: upd[k][2] for k in names}),
            *ordered(s_v, {k: upd[k][3] for k in names}))
```

```python
from typing import Callable, NamedTuple

import numpy as np
import jax
import jax.numpy as jnp
from jax import lax
from jax.experimental import pallas as pl
from jax.experimental.pallas import tpu as pltpu

F32 = jnp.float32
BF16 = jnp.bfloat16
MESH = pl.DeviceIdType.MESH

RMS_EPS = 1e-6
D_MODEL = 1024
HG_WIDTH = 512
HG_HEADS = 4
HG_DK = 128
HG_CHUNK = 64
HG_LEVELS = (32, 16, 8, 4, 2, 1)
HG_CHUNKS_PER_STEP = 4
SB_WIDTH = 512
SB_BLOCK = 128
SB_DH = 64
SB_SCALE = SB_DH ** -0.5
SB_UNDERFLOW_LOG = -87.5
SB_UNROLL = 2
SB_GROUP = 4
D_FF = 2816
N_CHIPS = 4
ROW_TILE = 256
WIDE_ROW_TILE = 512
V7X_VMEM_LIMIT = 56 * 1024 * 1024

ADAM_LR = 0.001
ADAM_B1 = 0.9
ADAM_B2 = 0.999
ADAM_EPS = 1e-08
ADAM_WD = 0.01
ADAM_STEP = 10

ROW_ATTN_PRE, ROW_GAMMA, ROW_MIX_NORMS, ROW_ATTN_POST, ROW_FFN_PRE, ROW_FFN_POST, ROW_LOSS = range(7)


def _params(sem=None, vmem=V7X_VMEM_LIMIT):
    return pltpu.CompilerParams(dimension_semantics=sem, vmem_limit_bytes=vmem)


def _dot(a, b):
    return jnp.dot(a.astype(BF16), b.astype(BF16), preferred_element_type=F32)


def _dot_nt(a, b):
    return lax.dot_general(a.astype(BF16), b.astype(BF16), (((1,), (1,)), ((), ())), preferred_element_type=F32)


def _dot_tn(a, b):
    return lax.dot_general(a.astype(BF16), b.astype(BF16), (((0,), (0,)), ((), ())), preferred_element_type=F32)


def _split(x):
    hi = x.astype(BF16)
    lo = (x - hi.astype(F32)).astype(BF16)
    return hi, lo


def _sum01_left(m01, x):
    hi, lo = _split(x)
    return jnp.dot(m01, hi, preferred_element_type=F32) + jnp.dot(m01, lo, preferred_element_type=F32)


def _sum01_right(x, m01_twice):
    hi, lo = _split(x)
    return jnp.dot(jnp.concatenate([hi, lo], axis=1), m01_twice, preferred_element_type=F32)


def _rms(x):
    r = lax.rsqrt(jnp.mean(x * x, axis=-1, keepdims=True) + RMS_EPS)
    return x * r, r


def _rms_bwd(dy, xhat, r, w):
    dxh = dy * w
    dx = r * (dxh - xhat * jnp.mean(dxh * xhat, axis=-1, keepdims=True))
    return dx, dy * xhat


def _sigmoid(x):
    return 1.0 / (1.0 + jnp.exp(-x))


def _neg_softplus(z):
    return -(jnp.maximum(z, 0.0) + jnp.log(1.0 + jnp.exp(-jnp.abs(z))))


def _colsum(x):
    return jnp.sum(x, axis=0, keepdims=True)


def _load_once(src_hbm, dst_vmem):
    @pl.when(pl.program_id(0) == 0)
    def _():
        pltpu.sync_copy(src_hbm, dst_vmem)


def _zero_first(ref):
    @pl.when(pl.program_id(0) == 0)
    def _():
        ref[...] = jnp.zeros(ref.shape, ref.dtype)


def _row_spec(width, col=0, rows=ROW_TILE):
    return pl.BlockSpec((rows, width), lambda i, col=col: (i, col))


def _wide_spec(width, col=0):
    return _row_spec(width, col, WIDE_ROW_TILE)


def _wide_tile(T):
    assert T % WIDE_ROW_TILE == 0
    return WIDE_ROW_TILE


def _full_spec(shape):
    return pl.BlockSpec(shape, lambda *_: (0,) * len(shape))


ANY_SPEC = pl.BlockSpec(memory_space=pl.ANY)
PACK_SPEC = _full_spec((8, D_MODEL))


class Comm(NamedTuple):
    inputs: list
    out_shape: list
    aliases: dict
    scratch: list
    start: Callable
    finish: Callable


def _pallas(body, *, comm=None, edge=None, in_specs, out_specs, out_shape, scratch_shapes=(), **kw):
    if comm is None:
        return pl.pallas_call(body, in_specs=in_specs, out_specs=out_specs, out_shape=out_shape,
                              scratch_shapes=scratch_shapes, **kw)
    n_in, n_out, n_scr = len(in_specs), len(out_specs), len(scratch_shapes)
    c_in, c_out = len(comm.inputs), len(comm.out_shape)

    def both(*refs):
        ins, c_ins = refs[:n_in], refs[n_in:n_in + c_in]
        outs = refs[n_in + c_in:n_in + c_in + n_out]
        c_outs = refs[n_in + c_in + n_out:n_in + c_in + n_out + c_out]
        rest = refs[n_in + c_in + n_out + c_out:]
        scr, c_scr = rest[:n_scr], rest[n_scr:]
        first, last = edge()

        @pl.when(first)
        def _():
            comm.start(c_ins, c_outs, c_scr)

        body(*ins, *outs, *scr)

        @pl.when(last)
        def _():
            comm.finish(c_ins, c_outs, c_scr)

    call = pl.pallas_call(
        both, in_specs=list(in_specs) + [ANY_SPEC] * c_in, out_specs=list(out_specs) + [ANY_SPEC] * c_out,
        out_shape=list(out_shape) + list(comm.out_shape), scratch_shapes=list(scratch_shapes) + list(comm.scratch),
        input_output_aliases={n_in + a: n_out + b for a, b in comm.aliases.items()}, **kw)
    return lambda *args: call(*args, *comm.inputs)


def _grid_edge(steps):
    return lambda: (pl.program_id(0) == 0, pl.program_id(0) == steps - 1)


def in_proj_fwd(x, g_pre, w_in, comm=None):
    T = x.shape[0]
    pw = w_in.shape[2]

    def body(x_ref, g_ref, w_hbm, ph_ref, sqkv_ref, u_ref, w_vmem, proj_s):
        _load_once(w_hbm, w_vmem)
        xh, _ = _rms(x_ref[...])
        u = (xh * g_ref[...]).astype(BF16)
        u_ref[...] = u
        for q in range(N_CHIPS):
            proj_s[:, pw * q:pw * (q + 1)] = jnp.dot(u, w_vmem[q], preferred_element_type=F32)
        ph_ref[...] = proj_s[:, :4 * HG_WIDTH]
        sqkv_ref[:, :SB_WIDTH] = (proj_s[:, 4 * HG_WIDTH:4 * HG_WIDTH + SB_WIDTH] * SB_SCALE).astype(BF16)
        sqkv_ref[:, SB_WIDTH:] = proj_s[:, 4 * HG_WIDTH + SB_WIDTH:].astype(BF16)

    return _pallas(
        body, comm=comm, edge=_grid_edge(T // _wide_tile(T)), name="in_proj_fwd", grid=(T // _wide_tile(T),),
        in_specs=[_wide_spec(D_MODEL), _full_spec((1, D_MODEL)), ANY_SPEC],
        out_specs=[_wide_spec(4 * HG_WIDTH), _wide_spec(3 * SB_WIDTH), _wide_spec(D_MODEL)],
        out_shape=[jax.ShapeDtypeStruct((T, 4 * HG_WIDTH), F32), jax.ShapeDtypeStruct((T, 3 * SB_WIDTH), BF16),
                   jax.ShapeDtypeStruct((T, D_MODEL), BF16)],
        scratch_shapes=[pltpu.VMEM(w_in.shape, BF16), pltpu.VMEM((_wide_tile(T), N_CHIPS * pw), F32)],
        compiler_params=_params(("arbitrary",)),
    )(x, g_pre, w_in)


def _hg_sum_matrix():
    C = HG_CHUNK
    t = np.arange(C)[:, None]
    j = np.arange(C)[None, :]
    mats = [j <= t, j > t]
    for h in HG_LEVELS:
        start = (t // (2 * h)) * (2 * h)
        upper = (t & h) != 0
        mats.append(np.where(upper, (j >= start + h) & (j <= t), (j > t) & (j <= start + h - 1)))
    return np.concatenate(mats, 0).astype(np.float32)


def _hg_level_masks():
    C = HG_CHUNK
    t = lax.broadcasted_iota(jnp.int32, (C, C), 0)
    s = lax.broadcasted_iota(jnp.int32, (C, C), 1)
    x = t ^ s
    masks = [t == s]
    for h in HG_LEVELS:
        masks.append((x >= h) & (x < 2 * h) & (t > s))
    return masks


def _hg_gates(hq, hf, gamma):
    lb = 1.0 / (1.0 + jnp.exp(gamma[1:2, :] - gamma[0:1, :]))
    sq = _sigmoid(hq)
    q = hq * sq
    sig = _sigmoid(hf)
    nsig = _sigmoid(-hf)
    f = lb + (1.0 - lb) * sig
    k = (1.0 - lb) * nsig
    g = jnp.log(f)
    return q, k, g, dict(lb=lb, sq=sq, sig=sig, nsig=nsig, f=f)


def _hg_head_decays(A, h):
    C, K = HG_CHUNK, HG_DK
    sl = slice(K * h, K * (h + 1))
    blocks = [A[C * r:C * (r + 1), sl] for r in range(2 + len(HG_LEVELS))]
    return blocks[0], blocks[1], [None] + blocks[2:]


def _hg_products(q, k, levels):
    return [_dot_nt(q, k)] + [_dot_nt(q * a, k * a) for a in levels[1:]]


def _hg_select(prods, masks):
    sc = jnp.where(masks[0], prods[0], 0.0)
    for p, m in zip(prods[1:], masks[1:]):
        sc = jnp.where(m, p, sc)
    return sc


def hgrn2_fwd(proj_h, gamma, comm=None):
    T = proj_h.shape[0]
    C, K, H, S = HG_CHUNK, HG_DK, HG_HEADS, HG_CHUNKS_PER_STEP
    n_steps = T // (S * C)
    msum = jnp.asarray(_hg_sum_matrix(), BF16)

    def body(hq_ref, hf_ref, hi_ref, gam_ref, msum_ref, o_ref, st_ref, st_s):
        _zero_first(st_s)
        q, k, g, _ = _hg_gates(hq_ref[...], hf_ref[...], gam_ref[...])
        v = hi_ref[...]
        masks = _hg_level_masks()
        parts = []
        for s in range(S):
            rows = slice(C * s, C * (s + 1))
            A = jnp.exp(_sum01_left(msum_ref[...], g[rows]))
            for h in range(H):
                sl = slice(K * h, K * (h + 1))
                ab, ar, levels = _hg_head_decays(A, h)
                parts.append(dict(s=s, h=h, rows=rows, sl=sl, ab=ab, ar=ar, levels=levels,
                                  q=q[rows, sl], k=k[rows, sl], v=v[rows, sl]))
        for pt in parts:
            pt["prods"] = _hg_products(pt["q"], pt["k"], pt["levels"])
            pt["grown"] = _dot_tn(pt["v"], pt["k"] * pt["ar"])
        for pt in parts:
            pt["sc"] = _hg_select(pt["prods"], masks)
        state = [st_s[h] for h in range(H)]
        for pt in parts:
            h, ab = pt["h"], pt["ab"]
            o_ref[pt["rows"], pt["sl"]] = _dot_nt(pt["q"] * ab, state[h]) + _dot(pt["sc"], pt["v"])
            state[h] = state[h] * ab[C - 1:C, :] + pt["grown"]
            st_ref[pt["s"], h] = state[h]
        for h in range(H):
            st_s[h] = state[h]

    blk = lambda col: pl.BlockSpec((S * C, HG_WIDTH), lambda c, col=col: (c, col))
    return _pallas(
        body, comm=comm, edge=_grid_edge(n_steps), name="hgrn2_fwd", grid=(n_steps,),
        in_specs=[blk(0), blk(1), blk(2), _full_spec((2, HG_WIDTH)), _full_spec(msum.shape)],
        out_specs=[blk(0), pl.BlockSpec((S, H, K, K), lambda c: (c, 0, 0, 0))],
        out_shape=[jax.ShapeDtypeStruct((T, HG_WIDTH), F32), jax.ShapeDtypeStruct((S * n_steps, H, K, K), F32)],
        scratch_shapes=[pltpu.VMEM((H, K, K), F32)],
        compiler_params=_params(("arbitrary",)),
    )(proj_h, proj_h, proj_h, gamma, msum)


def hgrn2_bwd(proj_h, gamma, states, do, comm=None):
    T = proj_h.shape[0]
    C, K, H, S = HG_CHUNK, HG_DK, HG_HEADS, HG_CHUNKS_PER_STEP
    n_steps = T // (S * C)
    n_sums = 2 + len(HG_LEVELS)
    msum = jnp.asarray(_hg_sum_matrix(), BF16)
    msum_t = jnp.asarray(_hg_sum_matrix().T, BF16)

    def body(hq_ref, hf_ref, hi_ref, do_ref, gam_ref, msum_ref, msum_t_ref, st_prev_ref, st_ref,
             dhq_ref, dhf_ref, dhi_ref, pack_ref, dst_s, dlb_s, dq_s, dk_s, de_s):
        step = pl.program_id(0)
        _zero_first(dst_s)
        _zero_first(dlb_s)
        _zero_first(pack_ref)
        hq = hq_ref[...]
        q, k, g, aux = _hg_gates(hq, hf_ref[...], gam_ref[...])
        v = hi_ref[...]
        do_all = do_ref[...]
        masks = _hg_level_masks()
        is_last_row = lax.broadcasted_iota(jnp.int32, (C, K), 0) == C - 1
        has_prev = (step < n_steps - 1).astype(F32)
        parts = []
        for s in reversed(range(S)):
            rows = slice(C * s, C * (s + 1))
            A = jnp.exp(_sum01_left(msum_ref[...], g[rows]))
            for h in range(H):
                sl = slice(K * h, K * (h + 1))
                ab, ar, levels = _hg_head_decays(A, h)
                st_in = st_prev_ref[0, h] * has_prev if s == 0 else st_ref[s - 1, h]
                parts.append(dict(s=s, h=h, rows=rows, sl=sl, ab=ab, ar=ar, levels=levels, st_in=st_in,
                                  q=q[rows, sl], k=k[rows, sl], v=v[rows, sl], do=do_all[rows, sl]))
        for pt in parts:
            pt["prods"] = _hg_products(pt["q"], pt["k"], pt["levels"])
            pt["da"] = _dot_nt(pt["do"], pt["v"])
            pt["t1"] = pt["ab"] * _dot(pt["do"], pt["st_in"])
            pt["dst_add"] = _dot_tn(pt["do"], pt["q"] * pt["ab"])
        dstate = [dst_s[h] for h in range(H)]
        for pt in parts:
            h = pt["h"]
            pt["dst_out"] = dstate[h]
            pt["t2"] = pt["ar"] * _dot(pt["v"], dstate[h])
            pt["dv_state"] = _dot_nt(pt["k"] * pt["ar"], dstate[h])
            dstate[h] = dstate[h] * pt["ab"][C - 1:C, :] + pt["dst_add"]
        for h in range(H):
            dst_s[h] = dstate[h]
        for pt in parts:
            pt["sc"] = _hg_select(pt["prods"], masks)
            pt["dam"] = [jnp.where(m, pt["da"], 0.0) for m in masks]
        for pt in parts:
            qh, kh = pt["q"], pt["k"]
            pt["dq_parts"] = [_dot(pt["dam"][0], kh)] + [
                a * _dot(dam, kh * a) for a, dam in zip(pt["levels"][1:], pt["dam"][1:])]
            pt["dk_parts"] = [_dot_tn(pt["dam"][0], qh)] + [
                a * _dot_tn(dam, qh * a) for a, dam in zip(pt["levels"][1:], pt["dam"][1:])]
            pt["dv_intra"] = _dot_tn(pt["sc"], pt["do"])
        for pt in parts:
            s, rows, sl, qh, kh, ab = pt["s"], pt["rows"], pt["sl"], pt["q"], pt["k"], pt["ab"]
            decayed = _colsum(pt["st_in"] * pt["dst_out"]) * ab[C - 1:C, :]
            de_s[s, 0:C, sl] = qh * pt["t1"] + jnp.where(is_last_row, decayed, 0.0)
            de_s[s, C:2 * C, sl] = kh * pt["t2"]
            dq = pt["t1"] + pt["dq_parts"][0]
            dk = pt["t2"] + pt["dk_parts"][0]
            for r, (t1, t2) in enumerate(zip(pt["dq_parts"][1:], pt["dk_parts"][1:])):
                dq = dq + t1
                dk = dk + t2
                de_s[s, C * (r + 2):C * (r + 3), sl] = qh * t1 + kh * t2
            dhi_ref[rows, sl] = pt["dv_intra"] + pt["dv_state"]
            dq_s[rows, sl] = dq
            dk_s[rows, sl] = dk
        dg = jnp.concatenate([_sum01_left(msum_t_ref[...], de_s[s]) for s in range(S)], axis=0)
        dk = dk_s[...]
        sq, lb = aux["sq"], aux["lb"]
        dhq_ref[...] = dq_s[...] * (sq * (1.0 + hq * (1.0 - sq)))
        common = dg / aux["f"] - dk
        dhf_ref[...] = (1.0 - lb) * aux["sig"] * aux["nsig"] * common
        dlb_s[...] += _colsum(aux["nsig"] * common)

        @pl.when(step == n_steps - 1)
        def _():
            dgam = lb * (1.0 - lb) * dlb_s[...]
            pack_ref[ROW_GAMMA:ROW_GAMMA + 1, :HG_WIDTH] = dgam
            pack_ref[ROW_GAMMA:ROW_GAMMA + 1, HG_WIDTH:] = -dgam

    last = n_steps - 1
    blk = lambda col: pl.BlockSpec((S * C, HG_WIDTH), lambda c, col=col: (last - c, col))
    return _pallas(
        body, comm=comm, edge=_grid_edge(n_steps), name="hgrn2_bwd", grid=(n_steps,),
        in_specs=[blk(0), blk(1), blk(2), blk(0), _full_spec((2, HG_WIDTH)), _full_spec(msum.shape),
                  _full_spec(msum_t.shape),
                  pl.BlockSpec((1, H, K, K), lambda c: (jnp.maximum(S * (last - c) - 1, 0), 0, 0, 0)),
                  pl.BlockSpec((S, H, K, K), lambda c: (last - c, 0, 0, 0))],
        out_specs=[blk(0), blk(0), blk(0), PACK_SPEC],
        out_shape=[jax.ShapeDtypeStruct((T, HG_WIDTH), F32)] * 3 + [jax.ShapeDtypeStruct((8, D_MODEL), F32)],
        scratch_shapes=[pltpu.VMEM((H, K, K), F32), pltpu.VMEM((1, HG_WIDTH), F32), pltpu.VMEM((S * C, HG_WIDTH), F32),
                        pltpu.VMEM((S * C, HG_WIDTH), F32), pltpu.VMEM((S, n_sums * C, HG_WIDTH), F32)],
        compiler_params=_params(("arbitrary",)),
    )(proj_h, proj_h, proj_h, do, gamma, msum, msum_t, states, states)


def _sb_sum_matrix(inclusive):
    B = SB_BLOCK
    j = np.arange(B)[:, None]
    s = np.arange(B)[None, :]
    tri = (j >= s) if inclusive else (j > s)
    once = np.concatenate([tri, np.ones((B, B), bool)], 1).astype(np.float32)
    return np.concatenate([once, once], 0)


def _sb_prefix_matrix(inclusive):
    B = SB_BLOCK
    j = np.arange(B)[:, None]
    s = np.arange(B)[None, :]
    tri = (j <= s) if inclusive else (j < s)
    once = np.concatenate([tri, np.ones((B, B), bool)], 1).astype(np.float32)
    return np.concatenate([once, once], 0)


def _sb_iotas():
    shape = (SB_BLOCK, SB_BLOCK)
    return lax.broadcasted_iota(jnp.int32, shape, 0), lax.broadcasted_iota(jnp.int32, shape, 1)


def _sb_heads(q, first):
    heads = []
    for g in range(SB_GROUP):
        qg = q[:, SB_BLOCK * g:SB_BLOCK * (g + 1)]
        zero = jnp.zeros_like(qg)
        heads += [(g, jnp.where(first, qg, zero)), (g, jnp.where(first, zero, qg))]
    return heads


def _lanes(x, g):
    return x[:, SB_BLOCK * g:SB_BLOCK * (g + 1)]


def sb_fwd(sqkv, comm=None):
    T = sqkv.shape[0]
    B = SB_BLOCK
    W = SB_GROUP * B
    groups = SB_WIDTH // W
    usum = jnp.asarray(_sb_sum_matrix(False), BF16)

    def body(q_ref, k_ref, v_ref, u_ref, o_ref, tl_ref, first_ref):
        p, i = pl.program_id(0), pl.program_id(1)
        row, lane = _sb_iotas()
        first = lane < SB_DH
        heads = _sb_heads(q_ref[...], first)
        u = u_ref[...]

        def more(loop):
            n, reachable, _ = loop
            return (SB_UNROLL * n <= i) & (reachable > 0)

        def step(loop):
            n, _, state = loop
            blocks = []
            for sub in range(SB_UNROLL):
                j = i - SB_UNROLL * n - sub
                off = pl.multiple_of(jnp.maximum(j, 0) * B, B)
                valid = ((lane + j * B) < (row + i * B)) & (j >= 0)
                blocks.append((k_ref[pl.ds(off, B), :], v_ref[pl.ds(off, B), :], valid))
            z = [[_dot_nt(qh, _lanes(kj, g)) for g, qh in heads] for kj, _, _ in blocks]
            lnb = [[jnp.where(valid, _neg_softplus(zz), 0.0) for zz in zs] for zs, (_, _, valid) in zip(z, blocks)]
            sums = [[_sum01_right(x, u) for x in xs] for xs in lnb]
            out = []
            for h, (carry, acc) in enumerate(state):
                for sub, (_, vj, valid) in enumerate(blocks):
                    expo = z[sub][h] + lnb[sub][h] + carry + sums[sub][h][:, :B]
                    acc = acc + _dot(jnp.where(valid, jnp.exp(expo), 0.0), _lanes(vj, heads[h][0]))
                    carry = carry + sums[sub][h][:, B:]
                out.append((carry, acc))
            state = tuple(out)
            worst = state[0][0]
            for carry, _ in state[1:]:
                worst = jnp.maximum(worst, carry)
            reachable = (jnp.max(worst) > SB_UNDERFLOW_LOG).astype(jnp.int32)
            return n + 1, reachable, state

        zero = jnp.zeros((B, B), F32)
        done, _, state = lax.while_loop(
            more, step, (jnp.int32(0), jnp.int32(1), tuple((zero, zero) for _ in heads)))
        for g in range(SB_GROUP):
            (tot0, acc0), (tot1, acc1) = state[2 * g], state[2 * g + 1]
            o_ref[:, B * g:B * (g + 1)] = jnp.where(first, acc0, acc1)
            tl_ref[:, B * g:B * (g + 1)] = jnp.where(first, tot0, tot1)
        first_ref[p, i] = jnp.maximum(i + 1 - SB_UNROLL * done, 0)

    def edge():
        p, i = pl.program_id(0), pl.program_id(1)
        return (p == 0) & (i == 0), (p == groups - 1) & (i == T // B - 1)

    return _pallas(
        body, comm=comm, edge=edge, name="sb_fwd", grid=(groups, T // B),
        in_specs=[pl.BlockSpec((B, W), lambda p, i: (i, p)),
                  pl.BlockSpec((T, W), lambda p, i: (0, groups + p)),
                  pl.BlockSpec((T, W), lambda p, i: (0, 2 * groups + p)),
                  pl.BlockSpec(usum.shape, lambda p, i: (0, 0))],
        out_specs=[pl.BlockSpec((B, W), lambda p, i: (i, p))] * 2 + [pl.BlockSpec(memory_space=pltpu.SMEM)],
        out_shape=[jax.ShapeDtypeStruct((T, SB_WIDTH), F32)] * 2 + [jax.ShapeDtypeStruct((groups, T // B), jnp.int32)],
        compiler_params=_params(("arbitrary", "arbitrary")),
    )(sqkv, sqkv, sqkv, usum)


def sb_bwd(sqkv, do, tl, first_block):
    T = sqkv.shape[0]
    B = SB_BLOCK
    W = SB_GROUP * B
    groups = SB_WIDTH // W
    upre = jnp.asarray(_sb_prefix_matrix(True), BF16)
    uexc = jnp.asarray(_sb_prefix_matrix(False), BF16)

    def body(q_ref, k_ref, v_ref, do_ref, tl_ref, up_ref, ue_ref, first_ref, dq_ref, dk_ref, dv_ref):
        p, i = pl.program_id(0), pl.program_id(1)

        @pl.when(i == 0)
        def _():
            dk_ref[...] = jnp.zeros(dk_ref.shape, F32)
            dv_ref[...] = jnp.zeros(dv_ref.shape, F32)

        row, lane = _sb_iotas()
        first = lane < SB_DH
        do = do_ref[...]
        tl_all = tl_ref[...]
        heads = []
        for (g, qh), at in zip(_sb_heads(q_ref[...], first), (0, B - 1) * SB_GROUP):
            dog = _lanes(do, g)
            keep = first if at == 0 else jnp.logical_not(first)
            heads.append((g, qh, jnp.where(keep, dog, jnp.zeros_like(dog)).astype(BF16),
                          _lanes(tl_all, g)[:, at:at + 1]))
        up = up_ref[...]
        ue = ue_ref[...]
        start = first_ref[p, i]

        def step(n, state):
            blocks = []
            for sub in range(SB_UNROLL):
                j = start + SB_UNROLL * n + sub
                off = pl.multiple_of(jnp.minimum(j, i) * B, B)
                valid = (lane + j * B) < (row + i * B)
                blocks.append((off, k_ref[pl.ds(off, B), :], v_ref[pl.ds(off, B), :], valid))
            combos = [(s, h) for s in range(SB_UNROLL) for h in range(len(heads))]
            z = {(s, h): _dot_nt(heads[h][1], _lanes(blocks[s][1], heads[h][0])) for s, h in combos}
            da = {(s, h): _dot_nt(heads[h][2], _lanes(blocks[s][2], heads[h][0])) for s, h in combos}
            lnb = {c: jnp.where(blocks[c[0]][3], _neg_softplus(z[c]), 0.0) for c in combos}
            lb = {c: z[c] + lnb[c] for c in combos}
            sums = {c: _sum01_right(lnb[c], up) for c in combos}
            a, w = {}, {}
            seen = [st[0] for st in state]
            for s, h in combos:
                expo = lb[s, h] + (heads[h][3] - seen[h] - sums[s, h][:, :B])
                a[s, h] = jnp.where(blocks[s][3], jnp.exp(expo), 0.0)
                w[s, h] = a[s, h] * da[s, h]
                seen[h] = seen[h] + sums[s, h][:, B:]
            wsums = {c: _sum01_right(w[c], ue) for c in combos}
            dz = {}
            seen_w = [st[1] for st in state]
            for s, h in combos:
                beta = jnp.exp(lb[s, h])
                before = seen_w[h] + wsums[s, h][:, :B]
                dz[s, h] = jnp.where(blocks[s][3], w[s, h] * (1.0 - beta) - before * beta, 0.0)
                seen_w[h] = seen_w[h] + wsums[s, h][:, B:]
            dq = [st[2] for st in state]
            for s, h in combos:
                dq[h] = dq[h] + _dot(dz[s, h], _lanes(blocks[s][1], heads[h][0]))
            for s in range(SB_UNROLL):
                off = blocks[s][0]
                for g in range(SB_GROUP):
                    h0, h1 = 2 * g, 2 * g + 1
                    dk_ref[pl.ds(off, B), B * g:B * (g + 1)] += (_dot_tn(dz[s, h0], heads[h0][1])
                                                                 + _dot_tn(dz[s, h1], heads[h1][1]))
                    dv_ref[pl.ds(off, B), B * g:B * (g + 1)] += (_dot_tn(a[s, h0], heads[h0][2])
                                                                 + _dot_tn(a[s, h1], heads[h1][2]))
            return tuple(zip(seen, seen_w, dq))

        zero = jnp.zeros((B, B), F32)
        trips = (i - start + SB_UNROLL) // SB_UNROLL
        state = lax.fori_loop(0, trips, step, tuple((zero, zero, zero) for _ in heads))
        for g in range(SB_GROUP):
            dq_ref[:, B * g:B * (g + 1)] = jnp.where(first, state[2 * g][2], state[2 * g + 1][2]) * SB_SCALE

    qblk = pl.BlockSpec((B, W), lambda p, i: (i, p))
    full = pl.BlockSpec((T, W), lambda p, i: (0, p))
    return pl.pallas_call(
        body, name="sb_bwd", grid=(groups, T // B),
        in_specs=[qblk, pl.BlockSpec((T, W), lambda p, i: (0, groups + p)),
                  pl.BlockSpec((T, W), lambda p, i: (0, 2 * groups + p)), qblk, qblk,
                  pl.BlockSpec(upre.shape, lambda p, i: (0, 0)), pl.BlockSpec(uexc.shape, lambda p, i: (0, 0)),
                  pl.BlockSpec(memory_space=pltpu.SMEM)],
        out_specs=[qblk, full, full],
        out_shape=[jax.ShapeDtypeStruct((T, SB_WIDTH), F32)] * 3,
        compiler_params=_params(("arbitrary", "arbitrary")),
    )(sqkv, sqkv, sqkv, do, tl, upre, uexc, first_block)


def _mixer_out(o_hg, hg, o_sb, g_hg, g_sb):
    n_hg, r_hg = _rms(o_hg)
    s_hg = _sigmoid(hg)
    n_sb, r_sb = _rms(o_sb)
    return dict(n_hg=n_hg, r_hg=r_hg, s_hg=s_hg, n_sb=n_sb, r_sb=r_sb,
                y_hg=n_hg * g_hg * (hg * s_hg), y_sb=n_sb * g_sb)


def mix_out_fwd(o_hg, proj_h, o_sb, x, norms, g_post, w_out, comm=None):
    T = x.shape[0]

    def body(ohg_ref, hg_ref, osb_ref, x_ref, nrm_ref, gp_ref, w_hbm, cat_ref, mix_ref, h1_ref, w_vmem):
        _load_once(w_hbm, w_vmem)
        nrm = nrm_ref[...]
        m = _mixer_out(ohg_ref[...], hg_ref[...], osb_ref[...], nrm[:, :HG_WIDTH], nrm[:, HG_WIDTH:])
        cat_ref[:, :HG_WIDTH] = m["y_hg"].astype(BF16)
        cat_ref[:, HG_WIDTH:] = m["y_sb"].astype(BF16)
        mix = jnp.dot(cat_ref[...], w_vmem[...], preferred_element_type=F32)
        mix_ref[...] = mix
        mh, _ = _rms(mix)
        h1_ref[...] = x_ref[...] + mh * gp_ref[...]

    return _pallas(
        body, comm=comm, edge=_grid_edge(T // _wide_tile(T)), name="mix_out_fwd", grid=(T // _wide_tile(T),),
        in_specs=[_wide_spec(HG_WIDTH), _wide_spec(HG_WIDTH, 3), _wide_spec(SB_WIDTH), _wide_spec(D_MODEL),
                  _full_spec((1, D_MODEL)), _full_spec((1, D_MODEL)), ANY_SPEC],
        out_specs=[_wide_spec(D_MODEL)] * 3,
        out_shape=[jax.ShapeDtypeStruct((T, D_MODEL), BF16), jax.ShapeDtypeStruct((T, D_MODEL), F32),
                   jax.ShapeDtypeStruct((T, D_MODEL), F32)],
        scratch_shapes=[pltpu.VMEM(w_out.shape, BF16)],
        compiler_params=_params(("arbitrary",)),
    )(o_hg, proj_h, o_sb, x, norms, g_post, w_out)


def ffn_fwd(h1, g_pre, g_post, w_gu, w_down):
    T = h1.shape[0]
    pw = w_gu.shape[2]

    def body(h1_ref, gpre_ref, gpost_ref, wgu_hbm, wd_hbm, u2_ref, gu_ref, act_ref, y_ref, h2_ref,
             wgu_vmem, wd_vmem, gu_s):
        _load_once(wgu_hbm, wgu_vmem)
        _load_once(wd_hbm, wd_vmem)
        h1v = h1_ref[...]
        hh, _ = _rms(h1v)
        u2 = (hh * gpre_ref[...]).astype(BF16)
        u2_ref[...] = u2
        for q in range(N_CHIPS):
            gu_s[:, pw * q:pw * (q + 1)] = jnp.dot(u2, wgu_vmem[q], preferred_element_type=F32)
        gu_ref[...] = gu_s[...].astype(BF16)
        gate = gu_s[:, :D_FF]
        act = (gate * _sigmoid(gate) * gu_s[:, D_FF:]).astype(BF16)
        act_ref[...] = act
        y = jnp.dot(act, wd_vmem[...], preferred_element_type=F32)
        y_ref[...] = y
        yh, _ = _rms(y)
        h2_ref[...] = h1v + yh * gpost_ref[...]

    return pl.pallas_call(
        body, name="ffn_fwd", grid=(T // ROW_TILE,),
        in_specs=[_row_spec(D_MODEL), _full_spec((1, D_MODEL)), _full_spec((1, D_MODEL)), ANY_SPEC, ANY_SPEC],
        out_specs=[_row_spec(D_MODEL), _row_spec(2 * D_FF), _row_spec(D_FF), _row_spec(D_MODEL), _row_spec(D_MODEL)],
        out_shape=[jax.ShapeDtypeStruct((T, D_MODEL), BF16), jax.ShapeDtypeStruct((T, 2 * D_FF), BF16),
                   jax.ShapeDtypeStruct((T, D_FF), BF16), jax.ShapeDtypeStruct((T, D_MODEL), F32),
                   jax.ShapeDtypeStruct((T, D_MODEL), F32)],
        scratch_shapes=[pltpu.VMEM(w_gu.shape, BF16), pltpu.VMEM(w_down.shape, BF16),
                        pltpu.VMEM((ROW_TILE, 2 * D_FF), F32)],
        compiler_params=_params(("arbitrary",)),
    )(h1, g_pre, g_post, w_gu, w_down)


def ple_loss(h2, p, target, w_ple, w_pg):
    T = h2.shape[0]
    pw = w_ple.shape[2]

    def body(h2_ref, p_ref, t_ref, wple_hbm, wpg_hbm, de_ref, ds_ref, dh2_ref, h2b_ref, pb_ref, pack_ref,
             wple_vmem, wpg_vmem, e_s):
        _load_once(wple_hbm, wple_vmem)
        _load_once(wpg_hbm, wpg_vmem)
        _zero_first(pack_ref)
        h2v = h2_ref[...]
        h2b = h2v.astype(BF16)
        h2b_ref[...] = h2b
        pb = p_ref[...].astype(BF16)
        pb_ref[...] = pb
        for q in range(N_CHIPS):
            e_s[:, pw * q:pw * (q + 1)] = jnp.dot(pb, wple_vmem[q], preferred_element_type=F32)
        e = e_s[...]
        sig = _sigmoid(jnp.dot(h2b, wpg_vmem[...], preferred_element_type=F32))
        err = h2v + e * sig - t_ref[...]
        part = 0.5 * jnp.sum(jnp.mean(err * err, axis=-1, keepdims=True), axis=0, keepdims=True)
        lane = lax.broadcasted_iota(jnp.int32, (1, D_MODEL), 1)
        pack_ref[ROW_LOSS:ROW_LOSS + 1, :] += jnp.where(lane == 0, part, 0.0)
        dh3 = err * (1.0 / D_MODEL)
        de_ref[...] = (dh3 * sig).astype(BF16)
        ds = (dh3 * e * sig * (1.0 - sig)).astype(BF16)
        ds_ref[...] = ds
        dh2_ref[...] = dh3 + _dot_nt(ds, wpg_vmem[...])

    return pl.pallas_call(
        body, name="ple_loss", grid=(T // _wide_tile(T),),
        in_specs=[_wide_spec(D_MODEL), _wide_spec(p.shape[1]), _wide_spec(D_MODEL), ANY_SPEC, ANY_SPEC],
        out_specs=[_wide_spec(D_MODEL), _wide_spec(D_MODEL), _wide_spec(D_MODEL), _wide_spec(D_MODEL),
                   _wide_spec(p.shape[1]), PACK_SPEC],
        out_shape=[jax.ShapeDtypeStruct((T, D_MODEL), BF16), jax.ShapeDtypeStruct((T, D_MODEL), BF16),
                   jax.ShapeDtypeStruct((T, D_MODEL), F32), jax.ShapeDtypeStruct((T, D_MODEL), BF16),
                   jax.ShapeDtypeStruct(p.shape, BF16), jax.ShapeDtypeStruct((8, D_MODEL), F32)],
        scratch_shapes=[pltpu.VMEM(w_ple.shape, BF16), pltpu.VMEM(w_pg.shape, BF16), pltpu.VMEM((_wide_tile(T), D_MODEL), F32)],
        compiler_params=_params(("arbitrary",)),
    )(h2, p, target, w_ple, w_pg)


def ffn_bwd(dh2, y, h1, gu, g_pre, g_post, w_gu, w_down):
    T = h1.shape[0]
    pw = w_gu.shape[2]

    def body(dh2_ref, y_ref, h1_ref, gu_ref, gpre_ref, gpost_ref, wgu_hbm, wd_hbm, dy_ref, dgu_ref, dh1_ref, pack_ref,
             wgu_vmem, wd_vmem):
        _load_once(wgu_hbm, wgu_vmem)
        _load_once(wd_hbm, wd_vmem)
        _zero_first(pack_ref)
        dh2v = dh2_ref[...]
        yh, ry = _rms(y_ref[...])
        dy, dw = _rms_bwd(dh2v, yh, ry, gpost_ref[...])
        pack_ref[ROW_FFN_POST:ROW_FFN_POST + 1, :] += _colsum(dw)
        dyb = dy.astype(BF16)
        dy_ref[...] = dyb
        dact = _dot_nt(dyb, wd_vmem[...])
        gate = gu_ref[:, :D_FF].astype(F32)
        up = gu_ref[:, D_FF:].astype(F32)
        sg = _sigmoid(gate)
        dgu_ref[:, :D_FF] = (dact * up * (sg * (1.0 + gate * (1.0 - sg)))).astype(BF16)
        dgu_ref[:, D_FF:] = (dact * gate * sg).astype(BF16)
        du2 = _dot_nt(dgu_ref[:, :pw], wgu_vmem[0])
        for q in range(1, N_CHIPS):
            du2 = du2 + _dot_nt(dgu_ref[:, pw * q:pw * (q + 1)], wgu_vmem[q])
        hh, rh = _rms(h1_ref[...])
        dh, dw = _rms_bwd(du2, hh, rh, gpre_ref[...])
        pack_ref[ROW_FFN_PRE:ROW_FFN_PRE + 1, :] += _colsum(dw)
        dh1_ref[...] = dh2v + dh

    return pl.pallas_call(
        body, name="ffn_bwd", grid=(T // ROW_TILE,),
        in_specs=[_row_spec(D_MODEL), _row_spec(D_MODEL), _row_spec(D_MODEL), _row_spec(2 * D_FF),
                  _full_spec((1, D_MODEL)), _full_spec((1, D_MODEL)), ANY_SPEC, ANY_SPEC],
        out_specs=[_row_spec(D_MODEL), _row_spec(2 * D_FF), _row_spec(D_MODEL), PACK_SPEC],
        out_shape=[jax.ShapeDtypeStruct((T, D_MODEL), BF16), jax.ShapeDtypeStruct((T, 2 * D_FF), BF16),
                   jax.ShapeDtypeStruct((T, D_MODEL), F32), jax.ShapeDtypeStruct((8, D_MODEL), F32)],
        scratch_shapes=[pltpu.VMEM(w_gu.shape, BF16), pltpu.VMEM(w_down.shape, BF16)],
        compiler_params=_params(("arbitrary",)),
    )(dh2, y, h1, gu, g_pre, g_post, w_gu, w_down)


def mix_out_bwd(dh1, mix, o_hg, proj_h, o_sb, norms, g_post, w_out, comm=None):
    T = dh1.shape[0]

    def body(dh1_ref, mix_ref, ohg_ref, hg_ref, osb_ref, nrm_ref, gp_ref, w_hbm, dmix_ref, dohg_ref, dhg_ref, dosb_ref,
             pack_ref, w_vmem):
        _load_once(w_hbm, w_vmem)
        _zero_first(pack_ref)
        mh, rm = _rms(mix_ref[...])
        dmix, dw = _rms_bwd(dh1_ref[...], mh, rm, gp_ref[...])
        pack_ref[ROW_ATTN_POST:ROW_ATTN_POST + 1, :] += _colsum(dw)
        dmb = dmix.astype(BF16)
        dmix_ref[...] = dmb
        dcat = _dot_nt(dmb, w_vmem[...])
        nrm = nrm_ref[...]
        g_hg, g_sb = nrm[:, :HG_WIDTH], nrm[:, HG_WIDTH:]
        hg = hg_ref[...]
        m = _mixer_out(ohg_ref[...], hg, osb_ref[...], g_hg, g_sb)
        d_hg = dcat[:, :HG_WIDTH]
        silu = hg * m["s_hg"]
        dhg_ref[...] = d_hg * (m["n_hg"] * g_hg) * (m["s_hg"] * (1.0 + hg * (1.0 - m["s_hg"])))
        dx, dw = _rms_bwd(d_hg * silu, m["n_hg"], m["r_hg"], g_hg)
        dohg_ref[...] = dx
        pack_ref[ROW_MIX_NORMS:ROW_MIX_NORMS + 1, :HG_WIDTH] += _colsum(dw)
        dx, dw = _rms_bwd(dcat[:, HG_WIDTH:], m["n_sb"], m["r_sb"], g_sb)
        dosb_ref[...] = dx
        pack_ref[ROW_MIX_NORMS:ROW_MIX_NORMS + 1, HG_WIDTH:] += _colsum(dw)

    return _pallas(
        body, comm=comm, edge=_grid_edge(T // _wide_tile(T)), name="mix_out_bwd", grid=(T // _wide_tile(T),),
        in_specs=[_wide_spec(D_MODEL), _wide_spec(D_MODEL), _wide_spec(HG_WIDTH), _wide_spec(HG_WIDTH, 3), _wide_spec(SB_WIDTH),
                  _full_spec((1, D_MODEL)), _full_spec((1, D_MODEL)), ANY_SPEC],
        out_specs=[_wide_spec(D_MODEL), _wide_spec(HG_WIDTH), _wide_spec(HG_WIDTH), _wide_spec(SB_WIDTH), PACK_SPEC],
        out_shape=[jax.ShapeDtypeStruct((T, D_MODEL), BF16), jax.ShapeDtypeStruct((T, HG_WIDTH), F32),
                   jax.ShapeDtypeStruct((T, HG_WIDTH), F32), jax.ShapeDtypeStruct((T, SB_WIDTH), F32),
                   jax.ShapeDtypeStruct((8, D_MODEL), F32)],
        scratch_shapes=[pltpu.VMEM(w_out.shape, BF16)],
        compiler_params=_params(("arbitrary",)),
    )(dh1, mix, o_hg, proj_h, o_sb, norms, g_post, w_out)


def in_proj_bwd(parts, x, dh1, g_pre, w_in, comm=None):
    T = x.shape[0]
    pw = w_in.shape[2]
    n_parts = len(parts)

    def body(*refs):
        part_refs = refs[:n_parts]
        x_ref, dh1_ref, g_ref, w_hbm, dproj_ref, dx_ref, pack_ref, w_vmem = refs[n_parts:]
        _load_once(w_hbm, w_vmem)
        _zero_first(pack_ref)
        for n, ref in enumerate(part_refs):
            dproj_ref[:, HG_WIDTH * n:HG_WIDTH * (n + 1)] = ref[...].astype(BF16)
        du = _dot_nt(dproj_ref[:, :pw], w_vmem[0])
        for q in range(1, N_CHIPS):
            du = du + _dot_nt(dproj_ref[:, pw * q:pw * (q + 1)], w_vmem[q])
        xh, r = _rms(x_ref[...])
        dx, dw = _rms_bwd(du, xh, r, g_ref[...])
        pack_ref[ROW_ATTN_PRE:ROW_ATTN_PRE + 1, :] += _colsum(dw)
        dx_ref[...] = dh1_ref[...] + dx

    return _pallas(
        body, comm=comm, edge=_grid_edge(T // _wide_tile(T)), name="in_proj_bwd", grid=(T // _wide_tile(T),),
        in_specs=[_wide_spec(HG_WIDTH)] * n_parts + [_wide_spec(D_MODEL), _wide_spec(D_MODEL), _full_spec((1, D_MODEL)), ANY_SPEC],
        out_specs=[_wide_spec(n_parts * HG_WIDTH), _wide_spec(D_MODEL), PACK_SPEC],
        out_shape=[jax.ShapeDtypeStruct((T, n_parts * HG_WIDTH), BF16), jax.ShapeDtypeStruct((T, D_MODEL), F32),
                   jax.ShapeDtypeStruct((8, D_MODEL), F32)],
        scratch_shapes=[pltpu.VMEM(w_in.shape, BF16)],
        compiler_params=_params(("arbitrary",)),
    )(*parts, x, dh1, g_pre, w_in)


def weight_grad(a, g, name, *, tm, tn, tk=512, col_pieces=False, comm=None):
    T, M = a.shape
    N = g.shape[1]
    tk = min(tk, T)
    steps = T // tk

    def body(a_ref, g_ref, o_ref):
        @pl.when(pl.program_id(2) == 0)
        def _():
            o_ref[...] = jnp.zeros(o_ref.shape, F32)

        o_ref[...] += _dot_tn(a_ref[...], g_ref[...]).reshape(o_ref.shape)

    if col_pieces:
        out_shape = jax.ShapeDtypeStruct((N // tn, M, tn), F32)
        out_spec = pl.BlockSpec((1, tm, tn), lambda i, j, k: (j, i, 0))
    else:
        out_shape = jax.ShapeDtypeStruct((M, N), F32)
        out_spec = pl.BlockSpec((tm, tn), lambda i, j, k: (i, j))
    grid = (M // tm, N // tn, steps)

    def edge():
        at = [pl.program_id(d) for d in range(3)]
        return ((at[0] == 0) & (at[1] == 0) & (at[2] == 0),
                (at[0] == grid[0] - 1) & (at[1] == grid[1] - 1) & (at[2] == grid[2] - 1))

    return _pallas(
        body, comm=comm, edge=edge, name=name, grid=grid,
        in_specs=[pl.BlockSpec((tk, tm), lambda i, j, k: (k, i)), pl.BlockSpec((tk, tn), lambda i, j, k: (k, j))],
        out_specs=[out_spec], out_shape=[out_shape],
        compiler_params=_params(("arbitrary", "arbitrary", "arbitrary")),
    )(a, g)


def _place():
    x, y, c = lax.axis_index("x"), lax.axis_index("y"), lax.axis_index("c")
    chips = [(1 - x, y), (x, 1 - y), (1 - x, 1 - y)]
    return x, y, c, chips


def _chip_index(cx, cy):
    return 2 * cx + cy


def _own_slot(piece, slots):
    me = _chip_index(lax.axis_index("x"), lax.axis_index("y"))
    landing = lax.empty((slots,) + piece.shape[1:], piece.dtype)
    return lax.dynamic_update_slice(landing, piece, (me,) + (0,) * (piece.ndim - 1))


def _rcopy(src, dst, send_sem, recv_sem, device):
    return pltpu.make_async_remote_copy(src_ref=src, dst_ref=dst, send_sem=send_sem, recv_sem=recv_sem,
                                        device_id=device, device_id_type=MESH)


def gather_weights(shards):
    n = len(shards)

    def body(*refs):
        ins, outs = refs[:n], refs[2 * n:3 * n]
        send_sems, recv_sems = refs[3 * n:]
        x, y, c, chips = _place()
        me = _chip_index(x, y)
        sibling = (x, y, 1 - c)

        def rows(w, core):
            half = ins[w].shape[0] // 2
            return pl.ds(core * half, half)

        sends = []
        for w in range(n):
            for j, chip in enumerate(chips):
                sends.append(_rcopy(ins[w].at[rows(w, c)], outs[w].at[me, rows(w, c)],
                                    send_sems.at[6 * w + j], recv_sems.at[6 * w + j], (*chip, c)))
        for cp in sends:
            cp.start()
        passed = []
        for w in range(n):
            for j, chip in enumerate(chips):
                block = outs[w].at[_chip_index(*chip), rows(w, c)]
                _rcopy(block, block, send_sems.at[6 * w + j], recv_sems.at[6 * w + j], (*chip, c)).wait_recv()
                cp = _rcopy(block, block, send_sems.at[6 * w + 3 + j], recv_sems.at[6 * w + 3 + j], sibling)
                cp.start()
                passed.append(cp)
        for w in range(n):
            for j, chip in enumerate(chips):
                block = outs[w].at[_chip_index(*chip), rows(w, 1 - c)]
                _rcopy(block, block, send_sems.at[6 * w + 3 + j], recv_sems.at[6 * w + 3 + j], sibling).wait_recv()
        for cp in sends + passed:
            cp.wait_send()

    filled = [_own_slot(s[None], N_CHIPS) for s in shards]
    return pl.pallas_call(
        body, name="gather_weights",
        in_specs=[ANY_SPEC] * (2 * n), out_specs=[ANY_SPEC] * n,
        out_shape=[jax.ShapeDtypeStruct(f.shape, f.dtype) for f in filled],
        input_output_aliases={n + w: w for w in range(n)},
        scratch_shapes=[pltpu.SemaphoreType.DMA((6 * n,)), pltpu.SemaphoreType.DMA((6 * n,))],
    )(*shards, *filled)


def _run_comm(comm, name):
    c_in, c_out = len(comm.inputs), len(comm.out_shape)

    def body(*refs):
        parts = refs[:c_in], refs[c_in:c_in + c_out], refs[c_in + c_out:]
        comm.start(*parts)
        comm.finish(*parts)

    return pl.pallas_call(
        body, name=name, in_specs=[ANY_SPEC] * c_in, out_specs=[ANY_SPEC] * c_out, out_shape=comm.out_shape,
        scratch_shapes=comm.scratch, input_output_aliases=comm.aliases)(*comm.inputs)


def _both(first, second):
    n_in, n_out, n_scr = len(first.inputs), len(first.out_shape), len(first.scratch)

    def split(ins, outs, scr):
        return (ins[:n_in], outs[:n_out], scr[:n_scr]), (ins[n_in:], outs[n_out:], scr[n_scr:])

    def start(*refs):
        a, b = split(*refs)
        first.start(*a)
        second.start(*b)

    def finish(*refs):
        a, b = split(*refs)
        first.finish(*a)
        second.finish(*b)

    aliases = dict(first.aliases)
    aliases.update({n_in + i: n_out + o for i, o in second.aliases.items()})
    return Comm(first.inputs + second.inputs, first.out_shape + second.out_shape, aliases,
                first.scratch + second.scratch, start, finish)


def _dma_sems(count):
    return [pltpu.SemaphoreType.DMA((count,)), pltpu.SemaphoreType.DMA((count,))]


def gather_over_ici(shards):
    n = len(shards)

    def copies(ins, outs, sems):
        send_sems, recv_sems = sems
        x, y, c, chips = _place()
        me = _chip_index(x, y)
        pairs = []
        for w in range(n):
            half = shards[w].shape[0] // 2
            rows = pl.ds(c * half, half)
            for j, chip in enumerate(chips):
                k = 3 * w + j
                landed = outs[w].at[_chip_index(*chip), rows]
                pairs.append((_rcopy(ins[w].at[rows], outs[w].at[me, rows], send_sems.at[k], recv_sems.at[k], (*chip, c)),
                              _rcopy(landed, landed, send_sems.at[k], recv_sems.at[k], (*chip, c))))
        return pairs

    def start(*refs):
        for send, _ in copies(*refs):
            send.start()

    def finish(*refs):
        pairs = copies(*refs)
        for _, landed in pairs:
            landed.wait_recv()
        for send, _ in pairs:
            send.wait_send()

    filled = [_own_slot(s[None], N_CHIPS) for s in shards]
    return Comm(list(shards) + filled, [jax.ShapeDtypeStruct(f.shape, f.dtype) for f in filled],
                {n + w: w for w in range(n)}, _dma_sems(3 * n), start, finish)


def gather_over_d2d(landed):
    n = len(landed)

    def copies(ins, outs, sems):
        send_sems, recv_sems = sems
        x, y, c, chips = _place()
        sibling = (x, y, 1 - c)
        pairs = []
        for w in range(n):
            half = landed[w].shape[1] // 2
            for j, chip in enumerate(chips):
                k = 3 * w + j
                mine = outs[w].at[_chip_index(*chip), pl.ds(c * half, half)]
                theirs = outs[w].at[_chip_index(*chip), pl.ds((1 - c) * half, half)]
                pairs.append((_rcopy(mine, mine, send_sems.at[k], recv_sems.at[k], sibling),
                              _rcopy(theirs, theirs, send_sems.at[k], recv_sems.at[k], sibling)))
        return pairs

    def start(*refs):
        for send, _ in copies(*refs):
            send.start()

    def finish(*refs):
        pairs = copies(*refs)
        for _, arrived in pairs:
            arrived.wait_recv()
        for send, _ in pairs:
            send.wait_send()

    return Comm(list(landed), [jax.ShapeDtypeStruct(a.shape, a.dtype) for a in landed], {w: w for w in range(n)},
                _dma_sems(3 * n), start, finish)


def core_halves(grads):
    n = len(grads)

    def copies(ins, outs, sems):
        send_sems, recv_sems = sems
        x, y, c, _ = _place()
        out = []
        for w in range(n):
            half = grads[w].shape[1] // 2
            out.append(_rcopy(ins[w].at[:, pl.ds((1 - c) * half, half), :], outs[w],
                              send_sems.at[w], recv_sems.at[w], (x, y, 1 - c)))
        return out

    def start(*refs):
        for cp in copies(*refs):
            cp.start()

    def finish(*refs):
        for cp in copies(*refs):
            cp.wait()

    return Comm(list(grads), [jax.ShapeDtypeStruct((g.shape[0], g.shape[1] // 2, g.shape[2]), g.dtype) for g in grads],
                {}, _dma_sems(n), start, finish)


def chip_partials(pairs):
    partials = [p for p, _ in pairs]
    filled = [landing for _, landing in pairs]
    n = len(partials)

    def copies(ins, outs, sems):
        send_sems, recv_sems = sems
        x, y, c, chips = _place()
        me = _chip_index(x, y)
        pairs = []
        for w in range(n):
            for j, chip in enumerate(chips):
                k = 3 * w + j
                landed = outs[w].at[_chip_index(*chip)]
                pairs.append((_rcopy(ins[w].at[_chip_index(*chip)], outs[w].at[me], send_sems.at[k], recv_sems.at[k],
                                     (*chip, c)),
                              _rcopy(landed, landed, send_sems.at[k], recv_sems.at[k], (*chip, c))))
        return pairs

    def start(*refs):
        for send, _ in copies(*refs):
            send.start()

    def finish(*refs):
        pairs = copies(*refs)
        for _, landed in pairs:
            landed.wait_recv()
        for send, _ in pairs:
            send.wait_send()

    return Comm(list(partials) + filled, [jax.ShapeDtypeStruct(p.shape, p.dtype) for p in partials],
                {n + w: w for w in range(n)}, _dma_sems(3 * n), start, finish)


def join_core_halves(grads):
    n = len(grads)

    def body(*refs):
        outs = refs[n:2 * n]
        send_sems, recv_sems = refs[2 * n:]
        x, y, c, _ = _place()
        sibling = (x, y, 1 - c)
        copies = []
        for w in range(n):
            half = outs[w].shape[0] // 2
            mine = outs[w].at[pl.ds(c * half, half), :]
            copies.append(_rcopy(mine, mine, send_sems.at[w], recv_sems.at[w], sibling))
        for cp in copies:
            cp.start()
        for w in range(n):
            half = outs[w].shape[0] // 2
            theirs = outs[w].at[pl.ds((1 - c) * half, half), :]
            _rcopy(theirs, theirs, send_sems.at[w], recv_sems.at[w], sibling).wait_recv()
        for cp in copies:
            cp.wait_send()

    return pl.pallas_call(
        body, name="join_core_halves",
        in_specs=[ANY_SPEC] * n, out_specs=[ANY_SPEC] * n,
        out_shape=[jax.ShapeDtypeStruct(g.shape, g.dtype) for g in grads],
        input_output_aliases={w: w for w in range(n)},
        scratch_shapes=[pltpu.SemaphoreType.DMA((n,)), pltpu.SemaphoreType.DMA((n,))],
    )(*grads)


def _elementwise_rows(rows, cap=512):
    for t in range(min(rows, cap), 0, -8):
        if rows % t == 0 and t % 16 == 0:
            return t
    return rows


def add_core_halves(grad, got, core, name):
    _, rows, cols = got.shape
    tr = _elementwise_rows(rows)
    nt = rows // tr

    def body(core_ref, a_ref, b_ref, o_ref, landing_ref):
        o_ref[...] = (a_ref[...] + b_ref[...]).astype(BF16)
        landing_ref[...] = o_ref[...]

    spec = pl.BlockSpec((1, tr, cols), lambda q, i, core_ref: (q, i, 0))
    own = pl.BlockSpec((1, tr, cols), lambda q, i, core_ref: (q, core_ref[0] * nt + i, 0))
    return pl.pallas_call(
        body, name=name,
        grid_spec=pltpu.PrefetchScalarGridSpec(num_scalar_prefetch=1, grid=(N_CHIPS, nt), in_specs=[own, spec],
                                               out_specs=[spec, spec]),
        out_shape=[jax.ShapeDtypeStruct(got.shape, BF16)] * 2,
        compiler_params=_params(("arbitrary", "arbitrary")),
    )(core, grad, got)


def add_chip_partials(parts, core, name):
    _, rows, cols = parts.shape
    tr = _elementwise_rows(rows)
    nt = rows // tr

    def body(core_ref, p_ref, o_ref):
        acc = p_ref[0].astype(F32)
        for q in range(1, N_CHIPS):
            acc = acc + p_ref[q].astype(F32)
        o_ref[...] = acc

    return pl.pallas_call(
        body, name=name,
        grid_spec=pltpu.PrefetchScalarGridSpec(
            num_scalar_prefetch=1, grid=(nt,),
            in_specs=[pl.BlockSpec((N_CHIPS, tr, cols), lambda i, core_ref: (0, i, 0))],
            out_specs=pl.BlockSpec((tr, cols), lambda i, core_ref: (core_ref[0] * nt + i, 0))),
        out_shape=jax.ShapeDtypeStruct((2 * rows, cols), F32),
        compiler_params=_params(("arbitrary",)),
    )(core, parts)


def _adamw_math(w, g, m, v):
    m = ADAM_B1 * m + (1.0 - ADAM_B1) * g
    v = ADAM_B2 * v + (1.0 - ADAM_B2) * (g * g)
    m_hat = m / (1.0 - ADAM_B1 ** ADAM_STEP)
    v_hat = v / (1.0 - ADAM_B2 ** ADAM_STEP)
    delta = -ADAM_LR * (m_hat / (jnp.sqrt(v_hat) + ADAM_EPS) + ADAM_WD * w)
    return delta, m, v


def adamw(w, g, m, v, name):
    rows, cols = w.shape
    tr = _elementwise_rows(rows, 256)

    def body(w_ref, g_ref, m_ref, v_ref, g_out_ref, d_ref, nm_ref, nv_ref):
        d, nm, nv = _adamw_math(w_ref[...], g_ref[...], m_ref[...], v_ref[...])
        g_out_ref[...] = g_ref[...]
        d_ref[...] = d
        nm_ref[...] = nm
        nv_ref[...] = nv

    spec = pl.BlockSpec((tr, cols), lambda i: (i, 0))
    return pl.pallas_call(
        body, name=name, grid=(rows // tr,), in_specs=[spec] * 4, out_specs=[spec] * 4,
        out_shape=[jax.ShapeDtypeStruct((rows, cols), F32)] * 4,
        compiler_params=_params(("arbitrary",)),
    )(w, g, m, v)


def reduce_small(packs, w, m, v):
    n = len(packs)
    n_dev = 8
    flips = [(fx, fy, fc) for fx in (0, 1) for fy in (0, 1) for fc in (0, 1)][1:]

    def body(*refs):
        pack_refs = refs[:n]
        w_ref, m_ref, v_ref, g_out, d_out, m_out, v_out, mine, slots, send_sems, recv_sems = refs[n:]
        x, y, c, _ = _place()
        me = 4 * x + 2 * y + c
        acc = pack_refs[0][...]
        for ref in pack_refs[1:]:
            acc = acc + ref[...]
        mine[...] = acc
        sends = []
        for k, (fx, fy, fc) in enumerate(flips):
            peer = (x ^ fx, y ^ fy, c ^ fc)
            sends.append(_rcopy(mine, slots.at[me], send_sems.at[k], recv_sems.at[me], peer))
        for cp in sends:
            cp.start()
        slots[me] = acc
        for fx, fy, fc in flips:
            src = 4 * (x ^ fx) + 2 * (y ^ fy) + (c ^ fc)
            _rcopy(mine, slots.at[src], send_sems.at[0], recv_sems.at[src], (x, y, c)).wait_recv()
        for cp in sends:
            cp.wait_send()
        total = slots[0]
        for d in range(1, n_dev):
            total = total + slots[d]
        g_out[...] = total
        d, nm, nv = _adamw_math(w_ref[...], total, m_ref[...], v_ref[...])
        d_out[...] = d
        m_out[...] = nm
        v_out[...] = nv

    vm = pl.BlockSpec(memory_space=pltpu.VMEM)
    return pl.pallas_call(
        body, name="reduce_small",
        in_specs=[vm] * (n + 3), out_specs=[vm] * 4,
        out_shape=[jax.ShapeDtypeStruct((8, D_MODEL), F32)] * 4,
        scratch_shapes=[pltpu.VMEM((8, D_MODEL), F32), pltpu.VMEM((n_dev, 8, D_MODEL), F32),
                        pltpu.SemaphoreType.DMA((len(flips),)), pltpu.SemaphoreType.DMA((n_dev,))],
    )(*packs, w, m, v)


def _column_pieces(g):
    return g.reshape(g.shape[0], N_CHIPS, g.shape[1] // N_CHIPS).transpose(1, 0, 2)


def _pack_small(attn_pre, gamma, hg_norm, sb_norm, attn_post, ffn_pre, ffn_post):
    rows = [attn_pre, gamma.reshape(1, D_MODEL), jnp.concatenate([hg_norm, sb_norm], axis=1), attn_post, ffn_pre, ffn_post,
            jnp.zeros((2, D_MODEL), F32)]
    return jnp.concatenate(rows, axis=0)


def _unpack_small(pack):
    return (pack[ROW_ATTN_PRE:ROW_ATTN_PRE + 1], pack[ROW_GAMMA].reshape(2, HG_WIDTH),
            pack[ROW_MIX_NORMS:ROW_MIX_NORMS + 1, :HG_WIDTH], pack[ROW_MIX_NORMS:ROW_MIX_NORMS + 1, HG_WIDTH:],
            pack[ROW_ATTN_POST:ROW_ATTN_POST + 1], pack[ROW_FFN_PRE:ROW_FFN_PRE + 1], pack[ROW_FFN_POST:ROW_FFN_POST + 1])


def kernel(x, p, attn_pre_norm, w_in, hg_lower_gamma, hg_out_norm, sb_out_norm, w_out, attn_post_norm, ffn_pre_norm, w_gate_up, w_down, ffn_post_norm, ple_proj, ple_gate, loss_target, m_attn_pre_norm, m_w_in, m_hg_lower_gamma, m_hg_out_norm, m_sb_out_norm, m_w_out, m_attn_post_norm, m_ffn_pre_norm, m_w_gate_up, m_w_down, m_ffn_post_norm, m_ple_proj, m_ple_gate, v_attn_pre_norm, v_w_in, v_hg_lower_gamma, v_hg_out_norm, v_sb_out_norm, v_w_out, v_attn_post_norm, v_ffn_pre_norm, v_w_gate_up, v_w_down, v_ffn_post_norm, v_ple_proj, v_ple_gate):
    x2 = x[0]
    p2 = p[0, 0]
    target = loss_target[0]
    big = dict(w_in=(w_in, m_w_in, v_w_in), w_out=(w_out, m_w_out, v_w_out), w_gate_up=(w_gate_up, m_w_gate_up, v_w_gate_up),
               w_down=(w_down, m_w_down, v_w_down), ple_proj=(ple_proj, m_ple_proj, v_ple_proj),
               ple_gate=(ple_gate, m_ple_gate, v_ple_gate))
    names = list(big)
    big = {k: tuple(a[0] for a in t) for k, t in big.items()}

    shard16 = {k: big[k][0].astype(BF16) for k in names}
    w_in_full, = gather_weights([shard16["w_in"]])
    mix_norms = jnp.concatenate([hg_out_norm, sb_out_norm], axis=1)
    small_ones = ["w_out", "ple_proj", "ple_gate"]

    proj_h, sqkv, u1, *landed_small = in_proj_fwd(
        x2, attn_pre_norm, w_in_full, comm=gather_over_ici([shard16[k] for k in small_ones]))
    o_sb, sb_totals, sb_first, landed_gu = sb_fwd(sqkv, comm=gather_over_ici([shard16["w_gate_up"]]))
    o_hg, states, landed_down, *full_small = hgrn2_fwd(
        proj_h, hg_lower_gamma, comm=_both(gather_over_ici([shard16["w_down"]]), gather_over_d2d(landed_small)))
    full = dict(zip(small_ones, full_small), w_in=w_in_full)
    w_out_full = full["w_out"].reshape(D_MODEL, D_MODEL)
    w_pg_full = full["ple_gate"].reshape(D_MODEL, D_MODEL)
    cat, mix, h1, full["w_gate_up"], full["w_down"] = mix_out_fwd(
        o_hg, proj_h, o_sb, x2, mix_norms, attn_post_norm, w_out_full, comm=gather_over_d2d([landed_gu, landed_down]))
    w_down_full = full["w_down"].reshape(D_FF, D_MODEL)
    u2, gu, act, y, h2 = ffn_fwd(h1, ffn_pre_norm, ffn_post_norm, full["w_gate_up"], w_down_full)

    core = lax.axis_index("c").astype(jnp.int32).reshape(1)
    de, ds, dh2, h2b, pb, pack_loss = ple_loss(h2, p2, target, full["ple_proj"], w_pg_full)
    dy, dgu, dh1, pack_ffn = ffn_bwd(dh2, y, h1, gu, ffn_pre_norm, ffn_post_norm, full["w_gate_up"], w_down_full)
    local = {}
    local["w_gate_up"], = weight_grad(u2, dgu, "grad_w_gate_up", tm=D_MODEL, tn=full["w_gate_up"].shape[2], tk=1024,
                                      col_pieces=True)
    grad_down, got_gu = weight_grad(act, dy, "grad_w_down", tm=D_FF // 2, tn=D_MODEL, tk=1024,
                                    comm=core_halves([local["w_gate_up"]]))
    local["w_down"] = grad_down.reshape(full["w_down"].shape)
    local["ple_proj"] = _column_pieces(weight_grad(pb, de, "grad_ple_proj", tm=pb.shape[1], tn=D_MODEL, tk=1024)[0])
    grad_pg, got_down = weight_grad(h2b, ds, "grad_ple_gate", tm=D_MODEL, tn=D_MODEL, tk=1024,
                                    comm=core_halves([local["w_down"]]))
    local["ple_gate"] = grad_pg.reshape(full["ple_gate"].shape)
    early = list(local)
    dmix, do_hg, dhg, do_sb, pack_mix, *got_ple = mix_out_bwd(
        dh1, mix, o_hg, proj_h, o_sb, mix_norms, attn_post_norm, w_out_full,
        comm=core_halves([local["ple_proj"], local["ple_gate"]]))
    got = [got_gu, got_down] + got_ple
    partial = [add_core_halves(local[k], g, core, "add_core_halves_" + k) for k, g in zip(early, got)]
    local["w_out"] = weight_grad(cat, dmix, "grad_w_out", tm=D_MODEL, tn=D_MODEL, tk=1024)[0].reshape(full["w_out"].shape)
    dsq, dsk, dsv = sb_bwd(sqkv, do_sb, sb_totals, sb_first)
    dhq, dhf, dhi, pack_hg, *by_source, got_out = hgrn2_bwd(
        proj_h, hg_lower_gamma, states, do_hg, comm=_both(chip_partials(partial[:2]), core_halves([local["w_out"]])))
    early.append("w_out")
    partial.append(add_core_halves(local["w_out"], got_out, core, "add_core_halves_w_out"))
    dproj, grad_x, pack_in = in_proj_bwd([dhq, dhf, dhi, dhg, dsq, dsk, dsv], x2, dh1, attn_pre_norm, full["w_in"])

    late = ["w_in"]
    local["w_in"], *more = weight_grad(u1, dproj, "grad_w_in", tm=D_MODEL, tn=full["w_in"].shape[2], tk=1024,
                                       col_pieces=True, comm=chip_partials(partial[2:]))
    halves = {k: add_chip_partials(s, core, "add_chip_partials_" + k) for k, s in zip(early, by_source + more)}
    got = _run_comm(core_halves([local[k] for k in late]), "exchange_core_halves")
    partial = [add_core_halves(local[k], g, core, "add_core_halves_" + k) for k, g in zip(late, got)]
    by_source = _run_comm(chip_partials(partial), "exchange_chip_partials")
    halves.update({k: add_chip_partials(s, core, "add_chip_partials_" + k) for k, s in zip(late, by_source)})
    grads = dict(zip(names, join_core_halves([halves[k] for k in names])))

    upd = {k: adamw(big[k][0], grads[k], big[k][1], big[k][2], "adamw_" + k) for k in names}

    small = reduce_small(
        [pack_loss, pack_ffn, pack_mix, pack_hg, pack_in],
        _pack_small(attn_pre_norm, hg_lower_gamma, hg_out_norm, sb_out_norm, attn_post_norm, ffn_pre_norm, ffn_post_norm),
        _pack_small(m_attn_pre_norm, m_hg_lower_gamma, m_hg_out_norm, m_sb_out_norm, m_attn_post_norm, m_ffn_pre_norm, m_ffn_post_norm),
        _pack_small(v_attn_pre_norm, v_hg_lower_gamma, v_hg_out_norm, v_sb_out_norm, v_attn_post_norm, v_ffn_pre_norm, v_ffn_post_norm),
    )
    loss = small[0][ROW_LOSS, 0]
    s_grad, s_delta, s_m, s_v = (_unpack_small(t) for t in small)

    def ordered(small_vals, big_vals):
        a_pre, gam, hg_n, sb_n, a_post, f_pre, f_post = small_vals
        b = {k: big_vals[k][None] for k in names}
        return (a_pre, b["w_in"], gam, hg_n, sb_n, b["w_out"], a_post, f_pre, b["w_gate_up"], b["w_down"], f_post,
                b["ple_proj"], b["ple_gate"])

    return (loss, grad_x[None],
            *ordered(s_grad, {k: upd[k][0] for k in names}),
            *ordered(s_delta, {k: upd[k][1] for k in names}),
            *ordered(s_m, {k: upd[k][2] for k in names}),
            *ordered(s_v, {k: upd[k][3] for k in names}))
```

```python
from typing import Callable, NamedTuple

import numpy as np
import jax
import jax.numpy as jnp
from jax import lax
from jax.experimental import pallas as pl
from jax.experimental.pallas import tpu as pltpu

F32 = jnp.float32
BF16 = jnp.bfloat16
MESH = pl.DeviceIdType.MESH

RMS_EPS = 1e-6
D_MODEL = 1024
HG_WIDTH = 512
HG_HEADS = 4
HG_DK = 128
HG_CHUNK = 64
HG_LEVELS = (32, 16, 8, 4, 2, 1)
HG_CHUNKS_PER_STEP = 8
SB_WIDTH = 512
SB_BLOCK = 128
SB_DH = 64
SB_SCALE = SB_DH ** -0.5
SB_UNDERFLOW_LOG = -87.5
SB_UNROLL = 2
SB_GROUP = 4
D_FF = 2816
N_CHIPS = 4
ROW_TILE = 256
WIDE_ROW_TILE = 512
V7X_VMEM_LIMIT = 56 * 1024 * 1024

ADAM_LR = 0.001
ADAM_B1 = 0.9
ADAM_B2 = 0.999
ADAM_EPS = 1e-08
ADAM_WD = 0.01
ADAM_STEP = 10

ROW_ATTN_PRE, ROW_GAMMA, ROW_MIX_NORMS, ROW_ATTN_POST, ROW_FFN_PRE, ROW_FFN_POST, ROW_LOSS = range(7)


def _params(sem=None, vmem=V7X_VMEM_LIMIT):
    return pltpu.CompilerParams(dimension_semantics=sem, vmem_limit_bytes=vmem)


def _dot(a, b):
    return jnp.dot(a.astype(BF16), b.astype(BF16), preferred_element_type=F32)


def _dot_nt(a, b):
    return lax.dot_general(a.astype(BF16), b.astype(BF16), (((1,), (1,)), ((), ())), preferred_element_type=F32)


def _dot_tn(a, b):
    return lax.dot_general(a.astype(BF16), b.astype(BF16), (((0,), (0,)), ((), ())), preferred_element_type=F32)


def _split(x):
    hi = x.astype(BF16)
    lo = (x - hi.astype(F32)).astype(BF16)
    return hi, lo


def _sum01_left(m01, x):
    hi, lo = _split(x)
    return jnp.dot(m01, hi, preferred_element_type=F32) + jnp.dot(m01, lo, preferred_element_type=F32)


def _sum01_right(x, m01_twice):
    hi, lo = _split(x)
    return jnp.dot(jnp.concatenate([hi, lo], axis=1), m01_twice, preferred_element_type=F32)


def _rms(x):
    r = lax.rsqrt(jnp.mean(x * x, axis=-1, keepdims=True) + RMS_EPS)
    return x * r, r


def _rms_bwd(dy, xhat, r, w):
    dxh = dy * w
    dx = r * (dxh - xhat * jnp.mean(dxh * xhat, axis=-1, keepdims=True))
    return dx, dy * xhat


def _sigmoid(x):
    return 1.0 / (1.0 + jnp.exp(-x))


def _neg_softplus(z):
    return -(jnp.maximum(z, 0.0) + jnp.log(1.0 + jnp.exp(-jnp.abs(z))))


def _colsum(x):
    return jnp.sum(x, axis=0, keepdims=True)


def _load_once(src_hbm, dst_vmem):
    @pl.when(pl.program_id(0) == 0)
    def _():
        pltpu.sync_copy(src_hbm, dst_vmem)


def _zero_first(ref):
    @pl.when(pl.program_id(0) == 0)
    def _():
        ref[...] = jnp.zeros(ref.shape, ref.dtype)


def _row_spec(width, col=0, rows=ROW_TILE):
    return pl.BlockSpec((rows, width), lambda i, col=col: (i, col))


def _wide_spec(width, col=0):
    return _row_spec(width, col, WIDE_ROW_TILE)


def _wide_tile(T):
    assert T % WIDE_ROW_TILE == 0
    return WIDE_ROW_TILE


def _full_spec(shape):
    return pl.BlockSpec(shape, lambda *_: (0,) * len(shape))


ANY_SPEC = pl.BlockSpec(memory_space=pl.ANY)
PACK_SPEC = _full_spec((8, D_MODEL))


class Comm(NamedTuple):
    inputs: list
    out_shape: list
    aliases: dict
    scratch: list
    start: Callable
    finish: Callable


def _pallas(body, *, comm=None, edge=None, in_specs, out_specs, out_shape, scratch_shapes=(), **kw):
    if comm is None:
        return pl.pallas_call(body, in_specs=in_specs, out_specs=out_specs, out_shape=out_shape,
                              scratch_shapes=scratch_shapes, **kw)
    n_in, n_out, n_scr = len(in_specs), len(out_specs), len(scratch_shapes)
    c_in, c_out = len(comm.inputs), len(comm.out_shape)

    def both(*refs):
        ins, c_ins = refs[:n_in], refs[n_in:n_in + c_in]
        outs = refs[n_in + c_in:n_in + c_in + n_out]
        c_outs = refs[n_in + c_in + n_out:n_in + c_in + n_out + c_out]
        rest = refs[n_in + c_in + n_out + c_out:]
        scr, c_scr = rest[:n_scr], rest[n_scr:]
        first, last = edge()

        @pl.when(first)
        def _():
            comm.start(c_ins, c_outs, c_scr)

        body(*ins, *outs, *scr)

        @pl.when(last)
        def _():
            comm.finish(c_ins, c_outs, c_scr)

    call = pl.pallas_call(
        both, in_specs=list(in_specs) + [ANY_SPEC] * c_in, out_specs=list(out_specs) + [ANY_SPEC] * c_out,
        out_shape=list(out_shape) + list(comm.out_shape), scratch_shapes=list(scratch_shapes) + list(comm.scratch),
        input_output_aliases={n_in + a: n_out + b for a, b in comm.aliases.items()}, **kw)
    return lambda *args: call(*args, *comm.inputs)


def _grid_edge(steps):
    return lambda: (pl.program_id(0) == 0, pl.program_id(0) == steps - 1)


def in_proj_fwd(x, g_pre, w_in, comm=None):
    T = x.shape[0]
    pw = w_in.shape[2]

    def body(x_ref, g_ref, w_hbm, ph_ref, sqkv_ref, u_ref, w_vmem, proj_s):
        _load_once(w_hbm, w_vmem)
        xh, _ = _rms(x_ref[...])
        u = (xh * g_ref[...]).astype(BF16)
        u_ref[...] = u
        for q in range(N_CHIPS):
            proj_s[:, pw * q:pw * (q + 1)] = jnp.dot(u, w_vmem[q], preferred_element_type=F32)
        ph_ref[...] = proj_s[:, :4 * HG_WIDTH]
        sqkv_ref[:, :SB_WIDTH] = (proj_s[:, 4 * HG_WIDTH:4 * HG_WIDTH + SB_WIDTH] * SB_SCALE).astype(BF16)
        sqkv_ref[:, SB_WIDTH:] = proj_s[:, 4 * HG_WIDTH + SB_WIDTH:].astype(BF16)

    return _pallas(
        body, comm=comm, edge=_grid_edge(T // _wide_tile(T)), name="in_proj_fwd", grid=(T // _wide_tile(T),),
        in_specs=[_wide_spec(D_MODEL), _full_spec((1, D_MODEL)), ANY_SPEC],
        out_specs=[_wide_spec(4 * HG_WIDTH), _wide_spec(3 * SB_WIDTH), _wide_spec(D_MODEL)],
        out_shape=[jax.ShapeDtypeStruct((T, 4 * HG_WIDTH), F32), jax.ShapeDtypeStruct((T, 3 * SB_WIDTH), BF16),
                   jax.ShapeDtypeStruct((T, D_MODEL), BF16)],
        scratch_shapes=[pltpu.VMEM(w_in.shape, BF16), pltpu.VMEM((_wide_tile(T), N_CHIPS * pw), F32)],
        compiler_params=_params(("arbitrary",)),
    )(x, g_pre, w_in)


def _hg_sum_matrix():
    C = HG_CHUNK
    t = np.arange(C)[:, None]
    j = np.arange(C)[None, :]
    mats = [j <= t, j > t]
    for h in HG_LEVELS:
        start = (t // (2 * h)) * (2 * h)
        upper = (t & h) != 0
        mats.append(np.where(upper, (j >= start + h) & (j <= t), (j > t) & (j <= start + h - 1)))
    return np.concatenate(mats, 0).astype(np.float32)


def _hg_level_masks():
    C = HG_CHUNK
    t = lax.broadcasted_iota(jnp.int32, (C, C), 0)
    s = lax.broadcasted_iota(jnp.int32, (C, C), 1)
    x = t ^ s
    masks = [t == s]
    for h in HG_LEVELS:
        masks.append((x >= h) & (x < 2 * h) & (t > s))
    return masks


def _hg_gates(hq, hf, gamma):
    lb = 1.0 / (1.0 + jnp.exp(gamma[1:2, :] - gamma[0:1, :]))
    sq = _sigmoid(hq)
    q = hq * sq
    sig = _sigmoid(hf)
    nsig = _sigmoid(-hf)
    f = lb + (1.0 - lb) * sig
    k = (1.0 - lb) * nsig
    g = jnp.log(f)
    return q, k, g, dict(lb=lb, sq=sq, sig=sig, nsig=nsig, f=f)


def _hg_head_decays(A, h):
    C, K = HG_CHUNK, HG_DK
    sl = slice(K * h, K * (h + 1))
    blocks = [A[C * r:C * (r + 1), sl] for r in range(2 + len(HG_LEVELS))]
    return blocks[0], blocks[1], [None] + blocks[2:]


def _hg_products(q, k, levels):
    return [_dot_nt(q, k)] + [_dot_nt(q * a, k * a) for a in levels[1:]]


def _hg_select(prods, masks):
    sc = jnp.where(masks[0], prods[0], 0.0)
    for p, m in zip(prods[1:], masks[1:]):
        sc = jnp.where(m, p, sc)
    return sc


def hgrn2_fwd(proj_h, gamma, comm=None):
    T = proj_h.shape[0]
    C, K, H, S = HG_CHUNK, HG_DK, HG_HEADS, HG_CHUNKS_PER_STEP
    n_steps = T // (S * C)
    msum = jnp.asarray(_hg_sum_matrix(), BF16)

    def body(hq_ref, hf_ref, hi_ref, gam_ref, msum_ref, o_ref, st_ref, st_s):
        _zero_first(st_s)
        q, k, g, _ = _hg_gates(hq_ref[...], hf_ref[...], gam_ref[...])
        v = hi_ref[...]
        masks = _hg_level_masks()
        parts = []
        for s in range(S):
            rows = slice(C * s, C * (s + 1))
            A = jnp.exp(_sum01_left(msum_ref[...], g[rows]))
            for h in range(H):
                sl = slice(K * h, K * (h + 1))
                ab, ar, levels = _hg_head_decays(A, h)
                parts.append(dict(s=s, h=h, rows=rows, sl=sl, ab=ab, ar=ar, levels=levels,
                                  q=q[rows, sl], k=k[rows, sl], v=v[rows, sl]))
        for pt in parts:
            pt["prods"] = _hg_products(pt["q"], pt["k"], pt["levels"])
            pt["grown"] = _dot_tn(pt["v"], pt["k"] * pt["ar"])
        for pt in parts:
            pt["sc"] = _hg_select(pt["prods"], masks)
        state = [st_s[h] for h in range(H)]
        for pt in parts:
            h, ab = pt["h"], pt["ab"]
            o_ref[pt["rows"], pt["sl"]] = _dot_nt(pt["q"] * ab, state[h]) + _dot(pt["sc"], pt["v"])
            state[h] = state[h] * ab[C - 1:C, :] + pt["grown"]
            st_ref[pt["s"], h] = state[h]
        for h in range(H):
            st_s[h] = state[h]

    blk = lambda col: pl.BlockSpec((S * C, HG_WIDTH), lambda c, col=col: (c, col))
    return _pallas(
        body, comm=comm, edge=_grid_edge(n_steps), name="hgrn2_fwd", grid=(n_steps,),
        in_specs=[blk(0), blk(1), blk(2), _full_spec((2, HG_WIDTH)), _full_spec(msum.shape)],
        out_specs=[blk(0), pl.BlockSpec((S, H, K, K), lambda c: (c, 0, 0, 0))],
        out_shape=[jax.ShapeDtypeStruct((T, HG_WIDTH), F32), jax.ShapeDtypeStruct((S * n_steps, H, K, K), F32)],
        scratch_shapes=[pltpu.VMEM((H, K, K), F32)],
        compiler_params=_params(("arbitrary",)),
    )(proj_h, proj_h, proj_h, gamma, msum)


def hgrn2_bwd(proj_h, gamma, states, do, comm=None):
    T = proj_h.shape[0]
    C, K, H, S = HG_CHUNK, HG_DK, HG_HEADS, HG_CHUNKS_PER_STEP
    n_steps = T // (S * C)
    n_sums = 2 + len(HG_LEVELS)
    msum = jnp.asarray(_hg_sum_matrix(), BF16)
    msum_t = jnp.asarray(_hg_sum_matrix().T, BF16)

    def body(hq_ref, hf_ref, hi_ref, do_ref, gam_ref, msum_ref, msum_t_ref, st_prev_ref, st_ref,
             dhq_ref, dhf_ref, dhi_ref, pack_ref, dst_s, dlb_s, dq_s, dk_s, de_s):
        step = pl.program_id(0)
        _zero_first(dst_s)
        _zero_first(dlb_s)
        _zero_first(pack_ref)
        hq = hq_ref[...]
        q, k, g, aux = _hg_gates(hq, hf_ref[...], gam_ref[...])
        v = hi_ref[...]
        do_all = do_ref[...]
        masks = _hg_level_masks()
        is_last_row = lax.broadcasted_iota(jnp.int32, (C, K), 0) == C - 1
        has_prev = (step < n_steps - 1).astype(F32)
        parts = []
        for s in reversed(range(S)):
            rows = slice(C * s, C * (s + 1))
            A = jnp.exp(_sum01_left(msum_ref[...], g[rows]))
            for h in range(H):
                sl = slice(K * h, K * (h + 1))
                ab, ar, levels = _hg_head_decays(A, h)
                st_in = st_prev_ref[0, h] * has_prev if s == 0 else st_ref[s - 1, h]
                parts.append(dict(s=s, h=h, rows=rows, sl=sl, ab=ab, ar=ar, levels=levels, st_in=st_in,
                                  q=q[rows, sl], k=k[rows, sl], v=v[rows, sl], do=do_all[rows, sl]))
        for pt in parts:
            pt["prods"] = _hg_products(pt["q"], pt["k"], pt["levels"])
            pt["da"] = _dot_nt(pt["do"], pt["v"])
            pt["t1"] = pt["ab"] * _dot(pt["do"], pt["st_in"])
            pt["dst_add"] = _dot_tn(pt["do"], pt["q"] * pt["ab"])
        dstate = [dst_s[h] for h in range(H)]
        for pt in parts:
            h = pt["h"]
            pt["dst_out"] = dstate[h]
            pt["t2"] = pt["ar"] * _dot(pt["v"], dstate[h])
            pt["dv_state"] = _dot_nt(pt["k"] * pt["ar"], dstate[h])
            dstate[h] = dstate[h] * pt["ab"][C - 1:C, :] + pt["dst_add"]
        for h in range(H):
            dst_s[h] = dstate[h]
        for pt in parts:
            pt["sc"] = _hg_select(pt["prods"], masks)
            pt["dam"] = [jnp.where(m, pt["da"], 0.0) for m in masks]
        for pt in parts:
            qh, kh = pt["q"], pt["k"]
            pt["dq_parts"] = [_dot(pt["dam"][0], kh)] + [
                a * _dot(dam, kh * a) for a, dam in zip(pt["levels"][1:], pt["dam"][1:])]
            pt["dk_parts"] = [_dot_tn(pt["dam"][0], qh)] + [
                a * _dot_tn(dam, qh * a) for a, dam in zip(pt["levels"][1:], pt["dam"][1:])]
            pt["dv_intra"] = _dot_tn(pt["sc"], pt["do"])
        for pt in parts:
            s, rows, sl, qh, kh, ab = pt["s"], pt["rows"], pt["sl"], pt["q"], pt["k"], pt["ab"]
            decayed = _colsum(pt["st_in"] * pt["dst_out"]) * ab[C - 1:C, :]
            de_s[s, 0:C, sl] = qh * pt["t1"] + jnp.where(is_last_row, decayed, 0.0)
            de_s[s, C:2 * C, sl] = kh * pt["t2"]
            dq = pt["t1"] + pt["dq_parts"][0]
            dk = pt["t2"] + pt["dk_parts"][0]
            for r, (t1, t2) in enumerate(zip(pt["dq_parts"][1:], pt["dk_parts"][1:])):
                dq = dq + t1
                dk = dk + t2
                de_s[s, C * (r + 2):C * (r + 3), sl] = qh * t1 + kh * t2
            dhi_ref[rows, sl] = pt["dv_intra"] + pt["dv_state"]
            dq_s[rows, sl] = dq
            dk_s[rows, sl] = dk
        dg = jnp.concatenate([_sum01_left(msum_t_ref[...], de_s[s]) for s in range(S)], axis=0)
        dk = dk_s[...]
        sq, lb = aux["sq"], aux["lb"]
        dhq_ref[...] = dq_s[...] * (sq * (1.0 + hq * (1.0 - sq)))
        common = dg / aux["f"] - dk
        dhf_ref[...] = (1.0 - lb) * aux["sig"] * aux["nsig"] * common
        dlb_s[...] += _colsum(aux["nsig"] * common)

        @pl.when(step == n_steps - 1)
        def _():
            dgam = lb * (1.0 - lb) * dlb_s[...]
            pack_ref[ROW_GAMMA:ROW_GAMMA + 1, :HG_WIDTH] = dgam
            pack_ref[ROW_GAMMA:ROW_GAMMA + 1, HG_WIDTH:] = -dgam

    last = n_steps - 1
    blk = lambda col: pl.BlockSpec((S * C, HG_WIDTH), lambda c, col=col: (last - c, col))
    return _pallas(
        body, comm=comm, edge=_grid_edge(n_steps), name="hgrn2_bwd", grid=(n_steps,),
        in_specs=[blk(0), blk(1), blk(2), blk(0), _full_spec((2, HG_WIDTH)), _full_spec(msum.shape),
                  _full_spec(msum_t.shape),
                  pl.BlockSpec((1, H, K, K), lambda c: (jnp.maximum(S * (last - c) - 1, 0), 0, 0, 0)),
                  pl.BlockSpec((S, H, K, K), lambda c: (last - c, 0, 0, 0))],
        out_specs=[blk(0), blk(0), blk(0), PACK_SPEC],
        out_shape=[jax.ShapeDtypeStruct((T, HG_WIDTH), F32)] * 3 + [jax.ShapeDtypeStruct((8, D_MODEL), F32)],
        scratch_shapes=[pltpu.VMEM((H, K, K), F32), pltpu.VMEM((1, HG_WIDTH), F32), pltpu.VMEM((S * C, HG_WIDTH), F32),
                        pltpu.VMEM((S * C, HG_WIDTH), F32), pltpu.VMEM((S, n_sums * C, HG_WIDTH), F32)],
        compiler_params=_params(("arbitrary",)),
    )(proj_h, proj_h, proj_h, do, gamma, msum, msum_t, states, states)


def _sb_sum_matrix(inclusive):
    B = SB_BLOCK
    j = np.arange(B)[:, None]
    s = np.arange(B)[None, :]
    tri = (j >= s) if inclusive else (j > s)
    once = np.concatenate([tri, np.ones((B, B), bool)], 1).astype(np.float32)
    return np.concatenate([once, once], 0)


def _sb_prefix_matrix(inclusive):
    B = SB_BLOCK
    j = np.arange(B)[:, None]
    s = np.arange(B)[None, :]
    tri = (j <= s) if inclusive else (j < s)
    once = np.concatenate([tri, np.ones((B, B), bool)], 1).astype(np.float32)
    return np.concatenate([once, once], 0)


def _sb_iotas():
    shape = (SB_BLOCK, SB_BLOCK)
    return lax.broadcasted_iota(jnp.int32, shape, 0), lax.broadcasted_iota(jnp.int32, shape, 1)


def _sb_heads(q, first):
    heads = []
    for g in range(SB_GROUP):
        qg = q[:, SB_BLOCK * g:SB_BLOCK * (g + 1)]
        zero = jnp.zeros_like(qg)
        heads += [(g, jnp.where(first, qg, zero)), (g, jnp.where(first, zero, qg))]
    return heads


def _lanes(x, g):
    return x[:, SB_BLOCK * g:SB_BLOCK * (g + 1)]


def sb_fwd(sqkv, comm=None):
    T = sqkv.shape[0]
    B = SB_BLOCK
    W = SB_GROUP * B
    groups = SB_WIDTH // W
    usum = jnp.asarray(_sb_sum_matrix(False), BF16)

    def body(q_ref, k_ref, v_ref, u_ref, o_ref, tl_ref, first_ref):
        p, i = pl.program_id(0), pl.program_id(1)
        row, lane = _sb_iotas()
        first = lane < SB_DH
        heads = _sb_heads(q_ref[...], first)
        u = u_ref[...]

        def more(loop):
            n, reachable, _ = loop
            return (SB_UNROLL * n <= i) & (reachable > 0)

        def step(loop):
            n, _, state = loop
            blocks = []
            for sub in range(SB_UNROLL):
                j = i - SB_UNROLL * n - sub
                off = pl.multiple_of(jnp.maximum(j, 0) * B, B)
                valid = ((lane + j * B) < (row + i * B)) & (j >= 0)
                blocks.append((k_ref[pl.ds(off, B), :], v_ref[pl.ds(off, B), :], valid))
            z = [[_dot_nt(qh, _lanes(kj, g)) for g, qh in heads] for kj, _, _ in blocks]
            lnb = [[jnp.where(valid, _neg_softplus(zz), 0.0) for zz in zs] for zs, (_, _, valid) in zip(z, blocks)]
            sums = [[_sum01_right(x, u) for x in xs] for xs in lnb]
            out = []
            for h, (carry, acc) in enumerate(state):
                for sub, (_, vj, valid) in enumerate(blocks):
                    expo = z[sub][h] + lnb[sub][h] + carry + sums[sub][h][:, :B]
                    acc = acc + _dot(jnp.where(valid, jnp.exp(expo), 0.0), _lanes(vj, heads[h][0]))
                    carry = carry + sums[sub][h][:, B:]
                out.append((carry, acc))
            state = tuple(out)
            worst = state[0][0]
            for carry, _ in state[1:]:
                worst = jnp.maximum(worst, carry)
            reachable = (jnp.max(worst) > SB_UNDERFLOW_LOG).astype(jnp.int32)
            return n + 1, reachable, state

        zero = jnp.zeros((B, B), F32)
        done, _, state = lax.while_loop(
            more, step, (jnp.int32(0), jnp.int32(1), tuple((zero, zero) for _ in heads)))
        for g in range(SB_GROUP):
            (tot0, acc0), (tot1, acc1) = state[2 * g], state[2 * g + 1]
            o_ref[:, B * g:B * (g + 1)] = jnp.where(first, acc0, acc1)
            tl_ref[:, B * g:B * (g + 1)] = jnp.where(first, tot0, tot1)
        first_ref[p, i] = jnp.maximum(i + 1 - SB_UNROLL * done, 0)

    def edge():
        p, i = pl.program_id(0), pl.program_id(1)
        return (p == 0) & (i == 0), (p == groups - 1) & (i == T // B - 1)

    return _pallas(
        body, comm=comm, edge=edge, name="sb_fwd", grid=(groups, T // B),
        in_specs=[pl.BlockSpec((B, W), lambda p, i: (i, p)),
                  pl.BlockSpec((T, W), lambda p, i: (0, groups + p)),
                  pl.BlockSpec((T, W), lambda p, i: (0, 2 * groups + p)),
                  pl.BlockSpec(usum.shape, lambda p, i: (0, 0))],
        out_specs=[pl.BlockSpec((B, W), lambda p, i: (i, p))] * 2 + [pl.BlockSpec(memory_space=pltpu.SMEM)],
        out_shape=[jax.ShapeDtypeStruct((T, SB_WIDTH), F32)] * 2 + [jax.ShapeDtypeStruct((groups, T // B), jnp.int32)],
        compiler_params=_params(("arbitrary", "arbitrary")),
    )(sqkv, sqkv, sqkv, usum)


def sb_bwd(sqkv, do, tl, first_block):
    T = sqkv.shape[0]
    B = SB_BLOCK
    W = SB_GROUP * B
    groups = SB_WIDTH // W
    upre = jnp.asarray(_sb_prefix_matrix(True), BF16)
    uexc = jnp.asarray(_sb_prefix_matrix(False), BF16)

    def body(q_ref, k_ref, v_ref, do_ref, tl_ref, up_ref, ue_ref, first_ref, dq_ref, dk_ref, dv_ref):
        p, i = pl.program_id(0), pl.program_id(1)

        @pl.when(i == 0)
        def _():
            dk_ref[...] = jnp.zeros(dk_ref.shape, F32)
            dv_ref[...] = jnp.zeros(dv_ref.shape, F32)

        row, lane = _sb_iotas()
        first = lane < SB_DH
        do = do_ref[...]
        tl_all = tl_ref[...]
        heads = []
        for (g, qh), at in zip(_sb_heads(q_ref[...], first), (0, B - 1) * SB_GROUP):
            dog = _lanes(do, g)
            keep = first if at == 0 else jnp.logical_not(first)
            heads.append((g, qh, jnp.where(keep, dog, jnp.zeros_like(dog)).astype(BF16),
                          _lanes(tl_all, g)[:, at:at + 1]))
        up = up_ref[...]
        ue = ue_ref[...]
        start = first_ref[p, i]

        def step(n, state):
            blocks = []
            for sub in range(SB_UNROLL):
                j = start + SB_UNROLL * n + sub
                off = pl.multiple_of(jnp.minimum(j, i) * B, B)
                valid = (lane + j * B) < (row + i * B)
                blocks.append((off, k_ref[pl.ds(off, B), :], v_ref[pl.ds(off, B), :], valid))
            combos = [(s, h) for s in range(SB_UNROLL) for h in range(len(heads))]
            z = {(s, h): _dot_nt(heads[h][1], _lanes(blocks[s][1], heads[h][0])) for s, h in combos}
            da = {(s, h): _dot_nt(heads[h][2], _lanes(blocks[s][2], heads[h][0])) for s, h in combos}
            lnb = {c: jnp.where(blocks[c[0]][3], _neg_softplus(z[c]), 0.0) for c in combos}
            lb = {c: z[c] + lnb[c] for c in combos}
            sums = {c: _sum01_right(lnb[c], up) for c in combos}
            a, w = {}, {}
            seen = [st[0] for st in state]
            for s, h in combos:
                expo = lb[s, h] + (heads[h][3] - seen[h] - sums[s, h][:, :B])
                a[s, h] = jnp.where(blocks[s][3], jnp.exp(expo), 0.0)
                w[s, h] = a[s, h] * da[s, h]
                seen[h] = seen[h] + sums[s, h][:, B:]
            wsums = {c: _sum01_right(w[c], ue) for c in combos}
            dz = {}
            seen_w = [st[1] for st in state]
            for s, h in combos:
                beta = jnp.exp(lb[s, h])
                before = seen_w[h] + wsums[s, h][:, :B]
                dz[s, h] = jnp.where(blocks[s][3], w[s, h] * (1.0 - beta) - before * beta, 0.0)
                seen_w[h] = seen_w[h] + wsums[s, h][:, B:]
            dq = [st[2] for st in state]
            for s, h in combos:
                dq[h] = dq[h] + _dot(dz[s, h], _lanes(blocks[s][1], heads[h][0]))
            for s in range(SB_UNROLL):
                off = blocks[s][0]
                for g in range(SB_GROUP):
                    h0, h1 = 2 * g, 2 * g + 1
                    dk_ref[pl.ds(off, B), B * g:B * (g + 1)] += (_dot_tn(dz[s, h0], heads[h0][1])
                                                                 + _dot_tn(dz[s, h1], heads[h1][1]))
                    dv_ref[pl.ds(off, B), B * g:B * (g + 1)] += (_dot_tn(a[s, h0], heads[h0][2])
                                                                 + _dot_tn(a[s, h1], heads[h1][2]))
            return tuple(zip(seen, seen_w, dq))

        zero = jnp.zeros((B, B), F32)
        trips = (i - start + SB_UNROLL) // SB_UNROLL
        state = lax.fori_loop(0, trips, step, tuple((zero, zero, zero) for _ in heads))
        for g in range(SB_GROUP):
            dq_ref[:, B * g:B * (g + 1)] = jnp.where(first, state[2 * g][2], state[2 * g + 1][2]) * SB_SCALE

    qblk = pl.BlockSpec((B, W), lambda p, i: (i, p))
    full = pl.BlockSpec((T, W), lambda p, i: (0, p))
    return pl.pallas_call(
        body, name="sb_bwd", grid=(groups, T // B),
        in_specs=[qblk, pl.BlockSpec((T, W), lambda p, i: (0, groups + p)),
                  pl.BlockSpec((T, W), lambda p, i: (0, 2 * groups + p)), qblk, qblk,
                  pl.BlockSpec(upre.shape, lambda p, i: (0, 0)), pl.BlockSpec(uexc.shape, lambda p, i: (0, 0)),
                  pl.BlockSpec(memory_space=pltpu.SMEM)],
        out_specs=[qblk, full, full],
        out_shape=[jax.ShapeDtypeStruct((T, SB_WIDTH), F32)] * 3,
        compiler_params=_params(("arbitrary", "arbitrary")),
    )(sqkv, sqkv, sqkv, do, tl, upre, uexc, first_block)


def _mixer_out(o_hg, hg, o_sb, g_hg, g_sb):
    n_hg, r_hg = _rms(o_hg)
    s_hg = _sigmoid(hg)
    n_sb, r_sb = _rms(o_sb)
    return dict(n_hg=n_hg, r_hg=r_hg, s_hg=s_hg, n_sb=n_sb, r_sb=r_sb,
                y_hg=n_hg * g_hg * (hg * s_hg), y_sb=n_sb * g_sb)


def mix_out_fwd(o_hg, proj_h, o_sb, x, norms, g_post, w_out, comm=None):
    T = x.shape[0]

    def body(ohg_ref, hg_ref, osb_ref, x_ref, nrm_ref, gp_ref, w_hbm, cat_ref, mix_ref, h1_ref, w_vmem):
        _load_once(w_hbm, w_vmem)
        nrm = nrm_ref[...]
        m = _mixer_out(ohg_ref[...], hg_ref[...], osb_ref[...], nrm[:, :HG_WIDTH], nrm[:, HG_WIDTH:])
        cat_ref[:, :HG_WIDTH] = m["y_hg"].astype(BF16)
        cat_ref[:, HG_WIDTH:] = m["y_sb"].astype(BF16)
        mix = jnp.dot(cat_ref[...], w_vmem[...], preferred_element_type=F32)
        mix_ref[...] = mix
        mh, _ = _rms(mix)
        h1_ref[...] = x_ref[...] + mh * gp_ref[...]

    return _pallas(
        body, comm=comm, edge=_grid_edge(T // _wide_tile(T)), name="mix_out_fwd", grid=(T // _wide_tile(T),),
        in_specs=[_wide_spec(HG_WIDTH), _wide_spec(HG_WIDTH, 3), _wide_spec(SB_WIDTH), _wide_spec(D_MODEL),
                  _full_spec((1, D_MODEL)), _full_spec((1, D_MODEL)), ANY_SPEC],
        out_specs=[_wide_spec(D_MODEL)] * 3,
        out_shape=[jax.ShapeDtypeStruct((T, D_MODEL), BF16), jax.ShapeDtypeStruct((T, D_MODEL), F32),
                   jax.ShapeDtypeStruct((T, D_MODEL), F32)],
        scratch_shapes=[pltpu.VMEM(w_out.shape, BF16)],
        compiler_params=_params(("arbitrary",)),
    )(o_hg, proj_h, o_sb, x, norms, g_post, w_out)


def ffn_fwd(h1, g_pre, g_post, w_gu, w_down):
    T = h1.shape[0]
    pw = w_gu.shape[2]

    def body(h1_ref, gpre_ref, gpost_ref, wgu_hbm, wd_hbm, u2_ref, gu_ref, act_ref, y_ref, h2_ref,
             wgu_vmem, wd_vmem, gu_s):
        _load_once(wgu_hbm, wgu_vmem)
        _load_once(wd_hbm, wd_vmem)
        h1v = h1_ref[...]
        hh, _ = _rms(h1v)
        u2 = (hh * gpre_ref[...]).astype(BF16)
        u2_ref[...] = u2
        for q in range(N_CHIPS):
            gu_s[:, pw * q:pw * (q + 1)] = jnp.dot(u2, wgu_vmem[q], preferred_element_type=F32)
        gu_ref[...] = gu_s[...].astype(BF16)
        gate = gu_s[:, :D_FF]
        act = (gate * _sigmoid(gate) * gu_s[:, D_FF:]).astype(BF16)
        act_ref[...] = act
        y = jnp.dot(act, wd_vmem[...], preferred_element_type=F32)
        y_ref[...] = y
        yh, _ = _rms(y)
        h2_ref[...] = h1v + yh * gpost_ref[...]

    return pl.pallas_call(
        body, name="ffn_fwd", grid=(T // ROW_TILE,),
        in_specs=[_row_spec(D_MODEL), _full_spec((1, D_MODEL)), _full_spec((1, D_MODEL)), ANY_SPEC, ANY_SPEC],
        out_specs=[_row_spec(D_MODEL), _row_spec(2 * D_FF), _row_spec(D_FF), _row_spec(D_MODEL), _row_spec(D_MODEL)],
        out_shape=[jax.ShapeDtypeStruct((T, D_MODEL), BF16), jax.ShapeDtypeStruct((T, 2 * D_FF), BF16),
                   jax.ShapeDtypeStruct((T, D_FF), BF16), jax.ShapeDtypeStruct((T, D_MODEL), F32),
                   jax.ShapeDtypeStruct((T, D_MODEL), F32)],
        scratch_shapes=[pltpu.VMEM(w_gu.shape, BF16), pltpu.VMEM(w_down.shape, BF16),
                        pltpu.VMEM((ROW_TILE, 2 * D_FF), F32)],
        compiler_params=_params(("arbitrary",)),
    )(h1, g_pre, g_post, w_gu, w_down)


def ple_loss(h2, p, target, w_ple, w_pg):
    T = h2.shape[0]
    pw = w_ple.shape[2]

    def body(h2_ref, p_ref, t_ref, wple_hbm, wpg_hbm, de_ref, ds_ref, dh2_ref, h2b_ref, pb_ref, pack_ref,
             wple_vmem, wpg_vmem, e_s):
        _load_once(wple_hbm, wple_vmem)
        _load_once(wpg_hbm, wpg_vmem)
        _zero_first(pack_ref)
        h2v = h2_ref[...]
        h2b = h2v.astype(BF16)
        h2b_ref[...] = h2b
        pb = p_ref[...].astype(BF16)
        pb_ref[...] = pb
        for q in range(N_CHIPS):
            e_s[:, pw * q:pw * (q + 1)] = jnp.dot(pb, wple_vmem[q], preferred_element_type=F32)
        e = e_s[...]
        sig = _sigmoid(jnp.dot(h2b, wpg_vmem[...], preferred_element_type=F32))
        err = h2v + e * sig - t_ref[...]
        part = 0.5 * jnp.sum(jnp.mean(err * err, axis=-1, keepdims=True), axis=0, keepdims=True)
        lane = lax.broadcasted_iota(jnp.int32, (1, D_MODEL), 1)
        pack_ref[ROW_LOSS:ROW_LOSS + 1, :] += jnp.where(lane == 0, part, 0.0)
        dh3 = err * (1.0 / D_MODEL)
        de_ref[...] = (dh3 * sig).astype(BF16)
        ds = (dh3 * e * sig * (1.0 - sig)).astype(BF16)
        ds_ref[...] = ds
        dh2_ref[...] = dh3 + _dot_nt(ds, wpg_vmem[...])

    return pl.pallas_call(
        body, name="ple_loss", grid=(T // _wide_tile(T),),
        in_specs=[_wide_spec(D_MODEL), _wide_spec(p.shape[1]), _wide_spec(D_MODEL), ANY_SPEC, ANY_SPEC],
        out_specs=[_wide_spec(D_MODEL), _wide_spec(D_MODEL), _wide_spec(D_MODEL), _wide_spec(D_MODEL),
                   _wide_spec(p.shape[1]), PACK_SPEC],
        out_shape=[jax.ShapeDtypeStruct((T, D_MODEL), BF16), jax.ShapeDtypeStruct((T, D_MODEL), BF16),
                   jax.ShapeDtypeStruct((T, D_MODEL), F32), jax.ShapeDtypeStruct((T, D_MODEL), BF16),
                   jax.ShapeDtypeStruct(p.shape, BF16), jax.ShapeDtypeStruct((8, D_MODEL), F32)],
        scratch_shapes=[pltpu.VMEM(w_ple.shape, BF16), pltpu.VMEM(w_pg.shape, BF16), pltpu.VMEM((_wide_tile(T), D_MODEL), F32)],
        compiler_params=_params(("arbitrary",)),
    )(h2, p, target, w_ple, w_pg)


def ffn_bwd(dh2, y, h1, gu, g_pre, g_post, w_gu, w_down):
    T = h1.shape[0]
    pw = w_gu.shape[2]

    def body(dh2_ref, y_ref, h1_ref, gu_ref, gpre_ref, gpost_ref, wgu_hbm, wd_hbm, dy_ref, dgu_ref, dh1_ref, pack_ref,
             wgu_vmem, wd_vmem):
        _load_once(wgu_hbm, wgu_vmem)
        _load_once(wd_hbm, wd_vmem)
        _zero_first(pack_ref)
        dh2v = dh2_ref[...]
        yh, ry = _rms(y_ref[...])
        dy, dw = _rms_bwd(dh2v, yh, ry, gpost_ref[...])
        pack_ref[ROW_FFN_POST:ROW_FFN_POST + 1, :] += _colsum(dw)
        dyb = dy.astype(BF16)
        dy_ref[...] = dyb
        dact = _dot_nt(dyb, wd_vmem[...])
        gate = gu_ref[:, :D_FF].astype(F32)
        up = gu_ref[:, D_FF:].astype(F32)
        sg = _sigmoid(gate)
        dgu_ref[:, :D_FF] = (dact * up * (sg * (1.0 + gate * (1.0 - sg)))).astype(BF16)
        dgu_ref[:, D_FF:] = (dact * gate * sg).astype(BF16)
        du2 = _dot_nt(dgu_ref[:, :pw], wgu_vmem[0])
        for q in range(1, N_CHIPS):
            du2 = du2 + _dot_nt(dgu_ref[:, pw * q:pw * (q + 1)], wgu_vmem[q])
        hh, rh = _rms(h1_ref[...])
        dh, dw = _rms_bwd(du2, hh, rh, gpre_ref[...])
        pack_ref[ROW_FFN_PRE:ROW_FFN_PRE + 1, :] += _colsum(dw)
        dh1_ref[...] = dh2v + dh

    return pl.pallas_call(
        body, name="ffn_bwd", grid=(T // ROW_TILE,),
        in_specs=[_row_spec(D_MODEL), _row_spec(D_MODEL), _row_spec(D_MODEL), _row_spec(2 * D_FF),
                  _full_spec((1, D_MODEL)), _full_spec((1, D_MODEL)), ANY_SPEC, ANY_SPEC],
        out_specs=[_row_spec(D_MODEL), _row_spec(2 * D_FF), _row_spec(D_MODEL), PACK_SPEC],
        out_shape=[jax.ShapeDtypeStruct((T, D_MODEL), BF16), jax.ShapeDtypeStruct((T, 2 * D_FF), BF16),
                   jax.ShapeDtypeStruct((T, D_MODEL), F32), jax.ShapeDtypeStruct((8, D_MODEL), F32)],
        scratch_shapes=[pltpu.VMEM(w_gu.shape, BF16), pltpu.VMEM(w_down.shape, BF16)],
        compiler_params=_params(("arbitrary",)),
    )(dh2, y, h1, gu, g_pre, g_post, w_gu, w_down)


def mix_out_bwd(dh1, mix, o_hg, proj_h, o_sb, norms, g_post, w_out, comm=None):
    T = dh1.shape[0]

    def body(dh1_ref, mix_ref, ohg_ref, hg_ref, osb_ref, nrm_ref, gp_ref, w_hbm, dmix_ref, dohg_ref, dhg_ref, dosb_ref,
             pack_ref, w_vmem):
        _load_once(w_hbm, w_vmem)
        _zero_first(pack_ref)
        mh, rm = _rms(mix_ref[...])
        dmix, dw = _rms_bwd(dh1_ref[...], mh, rm, gp_ref[...])
        pack_ref[ROW_ATTN_POST:ROW_ATTN_POST + 1, :] += _colsum(dw)
        dmb = dmix.astype(BF16)
        dmix_ref[...] = dmb
        dcat = _dot_nt(dmb, w_vmem[...])
        nrm = nrm_ref[...]
        g_hg, g_sb = nrm[:, :HG_WIDTH], nrm[:, HG_WIDTH:]
        hg = hg_ref[...]
        m = _mixer_out(ohg_ref[...], hg, osb_ref[...], g_hg, g_sb)
        d_hg = dcat[:, :HG_WIDTH]
        silu = hg * m["s_hg"]
        dhg_ref[...] = d_hg * (m["n_hg"] * g_hg) * (m["s_hg"] * (1.0 + hg * (1.0 - m["s_hg"])))
        dx, dw = _rms_bwd(d_hg * silu, m["n_hg"], m["r_hg"], g_hg)
        dohg_ref[...] = dx
        pack_ref[ROW_MIX_NORMS:ROW_MIX_NORMS + 1, :HG_WIDTH] += _colsum(dw)
        dx, dw = _rms_bwd(dcat[:, HG_WIDTH:], m["n_sb"], m["r_sb"], g_sb)
        dosb_ref[...] = dx
        pack_ref[ROW_MIX_NORMS:ROW_MIX_NORMS + 1, HG_WIDTH:] += _colsum(dw)

    return _pallas(
        body, comm=comm, edge=_grid_edge(T // _wide_tile(T)), name="mix_out_bwd", grid=(T // _wide_tile(T),),
        in_specs=[_wide_spec(D_MODEL), _wide_spec(D_MODEL), _wide_spec(HG_WIDTH), _wide_spec(HG_WIDTH, 3), _wide_spec(SB_WIDTH),
                  _full_spec((1, D_MODEL)), _full_spec((1, D_MODEL)), ANY_SPEC],
        out_specs=[_wide_spec(D_MODEL), _wide_spec(HG_WIDTH), _wide_spec(HG_WIDTH), _wide_spec(SB_WIDTH), PACK_SPEC],
        out_shape=[jax.ShapeDtypeStruct((T, D_MODEL), BF16), jax.ShapeDtypeStruct((T, HG_WIDTH), F32),
                   jax.ShapeDtypeStruct((T, HG_WIDTH), F32), jax.ShapeDtypeStruct((T, SB_WIDTH), F32),
                   jax.ShapeDtypeStruct((8, D_MODEL), F32)],
        scratch_shapes=[pltpu.VMEM(w_out.shape, BF16)],
        compiler_params=_params(("arbitrary",)),
    )(dh1, mix, o_hg, proj_h, o_sb, norms, g_post, w_out)


def in_proj_bwd(parts, x, dh1, g_pre, w_in, comm=None):
    T = x.shape[0]
    pw = w_in.shape[2]
    n_parts = len(parts)

    def body(*refs):
        part_refs = refs[:n_parts]
        x_ref, dh1_ref, g_ref, w_hbm, dproj_ref, dx_ref, pack_ref, w_vmem = refs[n_parts:]
        _load_once(w_hbm, w_vmem)
        _zero_first(pack_ref)
        for n, ref in enumerate(part_refs):
            dproj_ref[:, HG_WIDTH * n:HG_WIDTH * (n + 1)] = ref[...].astype(BF16)
        du = _dot_nt(dproj_ref[:, :pw], w_vmem[0])
        for q in range(1, N_CHIPS):
            du = du + _dot_nt(dproj_ref[:, pw * q:pw * (q + 1)], w_vmem[q])
        xh, r = _rms(x_ref[...])
        dx, dw = _rms_bwd(du, xh, r, g_ref[...])
        pack_ref[ROW_ATTN_PRE:ROW_ATTN_PRE + 1, :] += _colsum(dw)
        dx_ref[...] = dh1_ref[...] + dx

    return _pallas(
        body, comm=comm, edge=_grid_edge(T // _wide_tile(T)), name="in_proj_bwd", grid=(T // _wide_tile(T),),
        in_specs=[_wide_spec(HG_WIDTH)] * n_parts + [_wide_spec(D_MODEL), _wide_spec(D_MODEL), _full_spec((1, D_MODEL)), ANY_SPEC],
        out_specs=[_wide_spec(n_parts * HG_WIDTH), _wide_spec(D_MODEL), PACK_SPEC],
        out_shape=[jax.ShapeDtypeStruct((T, n_parts * HG_WIDTH), BF16), jax.ShapeDtypeStruct((T, D_MODEL), F32),
                   jax.ShapeDtypeStruct((8, D_MODEL), F32)],
        scratch_shapes=[pltpu.VMEM(w_in.shape, BF16)],
        compiler_params=_params(("arbitrary",)),
    )(*parts, x, dh1, g_pre, w_in)


def weight_grad(a, g, name, *, tm, tn, tk=512, col_pieces=False, comm=None):
    T, M = a.shape
    N = g.shape[1]
    tk = min(tk, T)
    steps = T // tk

    def body(a_ref, g_ref, o_ref):
        @pl.when(pl.program_id(2) == 0)
        def _():
            o_ref[...] = jnp.zeros(o_ref.shape, F32)

        o_ref[...] += _dot_tn(a_ref[...], g_ref[...]).reshape(o_ref.shape)

    if col_pieces:
        out_shape = jax.ShapeDtypeStruct((N // tn, M, tn), F32)
        out_spec = pl.BlockSpec((1, tm, tn), lambda i, j, k: (j, i, 0))
    else:
        out_shape = jax.ShapeDtypeStruct((M, N), F32)
        out_spec = pl.BlockSpec((tm, tn), lambda i, j, k: (i, j))
    grid = (M // tm, N // tn, steps)

    def edge():
        at = [pl.program_id(d) for d in range(3)]
        return ((at[0] == 0) & (at[1] == 0) & (at[2] == 0),
                (at[0] == grid[0] - 1) & (at[1] == grid[1] - 1) & (at[2] == grid[2] - 1))

    return _pallas(
        body, comm=comm, edge=edge, name=name, grid=grid,
        in_specs=[pl.BlockSpec((tk, tm), lambda i, j, k: (k, i)), pl.BlockSpec((tk, tn), lambda i, j, k: (k, j))],
        out_specs=[out_spec], out_shape=[out_shape],
        compiler_params=_params(("arbitrary", "arbitrary", "arbitrary")),
    )(a, g)


def _place():
    x, y, c = lax.axis_index("x"), lax.axis_index("y"), lax.axis_index("c")
    chips = [(1 - x, y), (x, 1 - y), (1 - x, 1 - y)]
    return x, y, c, chips


def _chip_index(cx, cy):
    return 2 * cx + cy


def _own_slot(piece, slots):
    me = _chip_index(lax.axis_index("x"), lax.axis_index("y"))
    landing = lax.empty((slots,) + piece.shape[1:], piece.dtype)
    return lax.dynamic_update_slice(landing, piece, (me,) + (0,) * (piece.ndim - 1))


def _rcopy(src, dst, send_sem, recv_sem, device):
    return pltpu.make_async_remote_copy(src_ref=src, dst_ref=dst, send_sem=send_sem, recv_sem=recv_sem,
                                        device_id=device, device_id_type=MESH)


def gather_weights(shards):
    n = len(shards)

    def body(*refs):
        ins, outs = refs[:n], refs[2 * n:3 * n]
        send_sems, recv_sems = refs[3 * n:]
        x, y, c, chips = _place()
        me = _chip_index(x, y)
        sibling = (x, y, 1 - c)

        def rows(w, core):
            half = ins[w].shape[0] // 2
            return pl.ds(core * half, half)

        sends = []
        for w in range(n):
            for j, chip in enumerate(chips):
                sends.append(_rcopy(ins[w].at[rows(w, c)], outs[w].at[me, rows(w, c)],
                                    send_sems.at[6 * w + j], recv_sems.at[6 * w + j], (*chip, c)))
        for cp in sends:
            cp.start()
        passed = []
        for w in range(n):
            for j, chip in enumerate(chips):
                block = outs[w].at[_chip_index(*chip), rows(w, c)]
                _rcopy(block, block, send_sems.at[6 * w + j], recv_sems.at[6 * w + j], (*chip, c)).wait_recv()
                cp = _rcopy(block, block, send_sems.at[6 * w + 3 + j], recv_sems.at[6 * w + 3 + j], sibling)
                cp.start()
                passed.append(cp)
        for w in range(n):
            for j, chip in enumerate(chips):
                block = outs[w].at[_chip_index(*chip), rows(w, 1 - c)]
                _rcopy(block, block, send_sems.at[6 * w + 3 + j], recv_sems.at[6 * w + 3 + j], sibling).wait_recv()
        for cp in sends + passed:
            cp.wait_send()

    filled = [_own_slot(s[None], N_CHIPS) for s in shards]
    return pl.pallas_call(
        body, name="gather_weights",
        in_specs=[ANY_SPEC] * (2 * n), out_specs=[ANY_SPEC] * n,
        out_shape=[jax.ShapeDtypeStruct(f.shape, f.dtype) for f in filled],
        input_output_aliases={n + w: w for w in range(n)},
        scratch_shapes=[pltpu.SemaphoreType.DMA((6 * n,)), pltpu.SemaphoreType.DMA((6 * n,))],
    )(*shards, *filled)


def _run_comm(comm, name):
    c_in, c_out = len(comm.inputs), len(comm.out_shape)

    def body(*refs):
        parts = refs[:c_in], refs[c_in:c_in + c_out], refs[c_in + c_out:]
        comm.start(*parts)
        comm.finish(*parts)

    return pl.pallas_call(
        body, name=name, in_specs=[ANY_SPEC] * c_in, out_specs=[ANY_SPEC] * c_out, out_shape=comm.out_shape,
        scratch_shapes=comm.scratch, input_output_aliases=comm.aliases)(*comm.inputs)


def _both(first, second):
    n_in, n_out, n_scr = len(first.inputs), len(first.out_shape), len(first.scratch)

    def split(ins, outs, scr):
        return (ins[:n_in], outs[:n_out], scr[:n_scr]), (ins[n_in:], outs[n_out:], scr[n_scr:])

    def start(*refs):
        a, b = split(*refs)
        first.start(*a)
        second.start(*b)

    def finish(*refs):
        a, b = split(*refs)
        first.finish(*a)
        second.finish(*b)

    aliases = dict(first.aliases)
    aliases.update({n_in + i: n_out + o for i, o in second.aliases.items()})
    return Comm(first.inputs + second.inputs, first.out_shape + second.out_shape, aliases,
                first.scratch + second.scratch, start, finish)


def _dma_sems(count):
    return [pltpu.SemaphoreType.DMA((count,)), pltpu.SemaphoreType.DMA((count,))]


def gather_over_ici(shards):
    n = len(shards)

    def copies(ins, outs, sems):
        send_sems, recv_sems = sems
        x, y, c, chips = _place()
        me = _chip_index(x, y)
        pairs = []
        for w in range(n):
            half = shards[w].shape[0] // 2
            rows = pl.ds(c * half, half)
            for j, chip in enumerate(chips):
                k = 3 * w + j
                landed = outs[w].at[_chip_index(*chip), rows]
                pairs.append((_rcopy(ins[w].at[rows], outs[w].at[me, rows], send_sems.at[k], recv_sems.at[k], (*chip, c)),
                              _rcopy(landed, landed, send_sems.at[k], recv_sems.at[k], (*chip, c))))
        return pairs

    def start(*refs):
        for send, _ in copies(*refs):
            send.start()

    def finish(*refs):
        pairs = copies(*refs)
        for _, landed in pairs:
            landed.wait_recv()
        for send, _ in pairs:
            send.wait_send()

    filled = [_own_slot(s[None], N_CHIPS) for s in shards]
    return Comm(list(shards) + filled, [jax.ShapeDtypeStruct(f.shape, f.dtype) for f in filled],
                {n + w: w for w in range(n)}, _dma_sems(3 * n), start, finish)


def gather_over_d2d(landed):
    n = len(landed)

    def copies(ins, outs, sems):
        send_sems, recv_sems = sems
        x, y, c, chips = _place()
        sibling = (x, y, 1 - c)
        pairs = []
        for w in range(n):
            half = landed[w].shape[1] // 2
            for j, chip in enumerate(chips):
                k = 3 * w + j
                mine = outs[w].at[_chip_index(*chip), pl.ds(c * half, half)]
                theirs = outs[w].at[_chip_index(*chip), pl.ds((1 - c) * half, half)]
                pairs.append((_rcopy(mine, mine, send_sems.at[k], recv_sems.at[k], sibling),
                              _rcopy(theirs, theirs, send_sems.at[k], recv_sems.at[k], sibling)))
        return pairs

    def start(*refs):
        for send, _ in copies(*refs):
            send.start()

    def finish(*refs):
        pairs = copies(*refs)
        for _, arrived in pairs:
            arrived.wait_recv()
        for send, _ in pairs:
            send.wait_send()

    return Comm(list(landed), [jax.ShapeDtypeStruct(a.shape, a.dtype) for a in landed], {w: w for w in range(n)},
                _dma_sems(3 * n), start, finish)


def core_halves(grads):
    n = len(grads)

    def copies(ins, outs, sems):
        send_sems, recv_sems = sems
        x, y, c, _ = _place()
        out = []
        for w in range(n):
            half = grads[w].shape[1] // 2
            out.append(_rcopy(ins[w].at[:, pl.ds((1 - c) * half, half), :], outs[w],
                              send_sems.at[w], recv_sems.at[w], (x, y, 1 - c)))
        return out

    def start(*refs):
        for cp in copies(*refs):
            cp.start()

    def finish(*refs):
        for cp in copies(*refs):
            cp.wait()

    return Comm(list(grads), [jax.ShapeDtypeStruct((g.shape[0], g.shape[1] // 2, g.shape[2]), g.dtype) for g in grads],
                {}, _dma_sems(n), start, finish)


def chip_partials(pairs):
    partials = [p for p, _ in pairs]
    filled = [landing for _, landing in pairs]
    n = len(partials)

    def copies(ins, outs, sems):
        send_sems, recv_sems = sems
        x, y, c, chips = _place()
        me = _chip_index(x, y)
        pairs = []
        for w in range(n):
            for j, chip in enumerate(chips):
                k = 3 * w + j
                landed = outs[w].at[_chip_index(*chip)]
                pairs.append((_rcopy(ins[w].at[_chip_index(*chip)], outs[w].at[me], send_sems.at[k], recv_sems.at[k],
                                     (*chip, c)),
                              _rcopy(landed, landed, send_sems.at[k], recv_sems.at[k], (*chip, c))))
        return pairs

    def start(*refs):
        for send, _ in copies(*refs):
            send.start()

    def finish(*refs):
        pairs = copies(*refs)
        for _, landed in pairs:
            landed.wait_recv()
        for send, _ in pairs:
            send.wait_send()

    return Comm(list(partials) + filled, [jax.ShapeDtypeStruct(p.shape, p.dtype) for p in partials],
                {n + w: w for w in range(n)}, _dma_sems(3 * n), start, finish)


def join_core_halves(grads):
    n = len(grads)

    def body(*refs):
        outs = refs[n:2 * n]
        send_sems, recv_sems = refs[2 * n:]
        x, y, c, _ = _place()
        sibling = (x, y, 1 - c)
        copies = []
        for w in range(n):
            half = outs[w].shape[0] // 2
            mine = outs[w].at[pl.ds(c * half, half), :]
            copies.append(_rcopy(mine, mine, send_sems.at[w], recv_sems.at[w], sibling))
        for cp in copies:
            cp.start()
        for w in range(n):
            half = outs[w].shape[0] // 2
            theirs = outs[w].at[pl.ds((1 - c) * half, half), :]
            _rcopy(theirs, theirs, send_sems.at[w], recv_sems.at[w], sibling).wait_recv()
        for cp in copies:
            cp.wait_send()

    return pl.pallas_call(
        body, name="join_core_halves",
        in_specs=[ANY_SPEC] * n, out_specs=[ANY_SPEC] * n,
        out_shape=[jax.ShapeDtypeStruct(g.shape, g.dtype) for g in grads],
        input_output_aliases={w: w for w in range(n)},
        scratch_shapes=[pltpu.SemaphoreType.DMA((n,)), pltpu.SemaphoreType.DMA((n,))],
    )(*grads)


def _elementwise_rows(rows, cap=512):
    for t in range(min(rows, cap), 0, -8):
        if rows % t == 0 and t % 16 == 0:
            return t
    return rows


def add_core_halves(grad, got, core, name):
    _, rows, cols = got.shape
    tr = _elementwise_rows(rows)
    nt = rows // tr

    def body(core_ref, a_ref, b_ref, o_ref, landing_ref):
        o_ref[...] = (a_ref[...] + b_ref[...]).astype(BF16)
        landing_ref[...] = o_ref[...]

    spec = pl.BlockSpec((1, tr, cols), lambda q, i, core_ref: (q, i, 0))
    own = pl.BlockSpec((1, tr, cols), lambda q, i, core_ref: (q, core_ref[0] * nt + i, 0))
    return pl.pallas_call(
        body, name=name,
        grid_spec=pltpu.PrefetchScalarGridSpec(num_scalar_prefetch=1, grid=(N_CHIPS, nt), in_specs=[own, spec],
                                               out_specs=[spec, spec]),
        out_shape=[jax.ShapeDtypeStruct(got.shape, BF16)] * 2,
        compiler_params=_params(("arbitrary", "arbitrary")),
    )(core, grad, got)


def add_chip_partials(parts, core, name):
    _, rows, cols = parts.shape
    tr = _elementwise_rows(rows)
    nt = rows // tr

    def body(core_ref, p_ref, o_ref):
        acc = p_ref[0].astype(F32)
        for q in range(1, N_CHIPS):
            acc = acc + p_ref[q].astype(F32)
        o_ref[...] = acc

    return pl.pallas_call(
        body, name=name,
        grid_spec=pltpu.PrefetchScalarGridSpec(
            num_scalar_prefetch=1, grid=(nt,),
            in_specs=[pl.BlockSpec((N_CHIPS, tr, cols), lambda i, core_ref: (0, i, 0))],
            out_specs=pl.BlockSpec((tr, cols), lambda i, core_ref: (core_ref[0] * nt + i, 0))),
        out_shape=jax.ShapeDtypeStruct((2 * rows, cols), F32),
        compiler_params=_params(("arbitrary",)),
    )(core, parts)


def _adamw_math(w, g, m, v):
    m = ADAM_B1 * m + (1.0 - ADAM_B1) * g
    v = ADAM_B2 * v + (1.0 - ADAM_B2) * (g * g)
    m_hat = m / (1.0 - ADAM_B1 ** ADAM_STEP)
    v_hat = v / (1.0 - ADAM_B2 ** ADAM_STEP)
    delta = -ADAM_LR * (m_hat / (jnp.sqrt(v_hat) + ADAM_EPS) + ADAM_WD * w)
    return delta, m, v


def adamw(w, g, m, v, name):
    rows, cols = w.shape
    tr = _elementwise_rows(rows, 256)

    def body(w_ref, g_ref, m_ref, v_ref, g_out_ref, d_ref, nm_ref, nv_ref):
        d, nm, nv = _adamw_math(w_ref[...], g_ref[...], m_ref[...], v_ref[...])
        g_out_ref[...] = g_ref[...]
        d_ref[...] = d
        nm_ref[...] = nm
        nv_ref[...] = nv

    spec = pl.BlockSpec((tr, cols), lambda i: (i, 0))
    return pl.pallas_call(
        body, name=name, grid=(rows // tr,), in_specs=[spec] * 4, out_specs=[spec] * 4,
        out_shape=[jax.ShapeDtypeStruct((rows, cols), F32)] * 4,
        compiler_params=_params(("arbitrary",)),
    )(w, g, m, v)


def reduce_small(packs, w, m, v):
    n = len(packs)
    n_dev = 8
    flips = [(fx, fy, fc) for fx in (0, 1) for fy in (0, 1) for fc in (0, 1)][1:]

    def body(*refs):
        pack_refs = refs[:n]
        w_ref, m_ref, v_ref, g_out, d_out, m_out, v_out, mine, slots, send_sems, recv_sems = refs[n:]
        x, y, c, _ = _place()
        me = 4 * x + 2 * y + c
        acc = pack_refs[0][...]
        for ref in pack_refs[1:]:
            acc = acc + ref[...]
        mine[...] = acc
        sends = []
        for k, (fx, fy, fc) in enumerate(flips):
            peer = (x ^ fx, y ^ fy, c ^ fc)
            sends.append(_rcopy(mine, slots.at[me], send_sems.at[k], recv_sems.at[me], peer))
        for cp in sends:
            cp.start()
        slots[me] = acc
        for fx, fy, fc in flips:
            src = 4 * (x ^ fx) + 2 * (y ^ fy) + (c ^ fc)
            _rcopy(mine, slots.at[src], send_sems.at[0], recv_sems.at[src], (x, y, c)).wait_recv()
        for cp in sends:
            cp.wait_send()
        total = slots[0]
        for d in range(1, n_dev):
            total = total + slots[d]
        g_out[...] = total
        d, nm, nv = _adamw_math(w_ref[...], total, m_ref[...], v_ref[...])
        d_out[...] = d
        m_out[...] = nm
        v_out[...] = nv

    vm = pl.BlockSpec(memory_space=pltpu.VMEM)
    return pl.pallas_call(
        body, name="reduce_small",
        in_specs=[vm] * (n + 3), out_specs=[vm] * 4,
        out_shape=[jax.ShapeDtypeStruct((8, D_MODEL), F32)] * 4,
        scratch_shapes=[pltpu.VMEM((8, D_MODEL), F32), pltpu.VMEM((n_dev, 8, D_MODEL), F32),
                        pltpu.SemaphoreType.DMA((len(flips),)), pltpu.SemaphoreType.DMA((n_dev,))],
    )(*packs, w, m, v)


def _column_pieces(g):
    return g.reshape(g.shape[0], N_CHIPS, g.shape[1] // N_CHIPS).transpose(1, 0, 2)


def _pack_small(attn_pre, gamma, hg_norm, sb_norm, attn_post, ffn_pre, ffn_post):
    rows = [attn_pre, gamma.reshape(1, D_MODEL), jnp.concatenate([hg_norm, sb_norm], axis=1), attn_post, ffn_pre, ffn_post,
            jnp.zeros((2, D_MODEL), F32)]
    return jnp.concatenate(rows, axis=0)


def _unpack_small(pack):
    return (pack[ROW_ATTN_PRE:ROW_ATTN_PRE + 1], pack[ROW_GAMMA].reshape(2, HG_WIDTH),
            pack[ROW_MIX_NORMS:ROW_MIX_NORMS + 1, :HG_WIDTH], pack[ROW_MIX_NORMS:ROW_MIX_NORMS + 1, HG_WIDTH:],
            pack[ROW_ATTN_POST:ROW_ATTN_POST + 1], pack[ROW_FFN_PRE:ROW_FFN_PRE + 1], pack[ROW_FFN_POST:ROW_FFN_POST + 1])


def kernel(x, p, attn_pre_norm, w_in, hg_lower_gamma, hg_out_norm, sb_out_norm, w_out, attn_post_norm, ffn_pre_norm, w_gate_up, w_down, ffn_post_norm, ple_proj, ple_gate, loss_target, m_attn_pre_norm, m_w_in, m_hg_lower_gamma, m_hg_out_norm, m_sb_out_norm, m_w_out, m_attn_post_norm, m_ffn_pre_norm, m_w_gate_up, m_w_down, m_ffn_post_norm, m_ple_proj, m_ple_gate, v_attn_pre_norm, v_w_in, v_hg_lower_gamma, v_hg_out_norm, v_sb_out_norm, v_w_out, v_attn_post_norm, v_ffn_pre_norm, v_w_gate_up, v_w_down, v_ffn_post_norm, v_ple_proj, v_ple_gate):
    x2 = x[0]
    p2 = p[0, 0]
    target = loss_target[0]
    big = dict(w_in=(w_in, m_w_in, v_w_in), w_out=(w_out, m_w_out, v_w_out), w_gate_up=(w_gate_up, m_w_gate_up, v_w_gate_up),
               w_down=(w_down, m_w_down, v_w_down), ple_proj=(ple_proj, m_ple_proj, v_ple_proj),
               ple_gate=(ple_gate, m_ple_gate, v_ple_gate))
    names = list(big)
    big = {k: tuple(a[0] for a in t) for k, t in big.items()}

    shard16 = {k: big[k][0].astype(BF16) for k in names}
    w_in_full, = gather_weights([shard16["w_in"]])
    mix_norms = jnp.concatenate([hg_out_norm, sb_out_norm], axis=1)
    small_ones = ["w_out", "ple_proj", "ple_gate"]

    proj_h, sqkv, u1, *landed_small = in_proj_fwd(
        x2, attn_pre_norm, w_in_full, comm=gather_over_ici([shard16[k] for k in small_ones]))
    o_sb, sb_totals, sb_first, landed_gu = sb_fwd(sqkv, comm=gather_over_ici([shard16["w_gate_up"]]))
    o_hg, states, landed_down, *full_small = hgrn2_fwd(
        proj_h, hg_lower_gamma, comm=_both(gather_over_ici([shard16["w_down"]]), gather_over_d2d(landed_small)))
    full = dict(zip(small_ones, full_small), w_in=w_in_full)
    w_out_full = full["w_out"].reshape(D_MODEL, D_MODEL)
    w_pg_full = full["ple_gate"].reshape(D_MODEL, D_MODEL)
    cat, mix, h1, full["w_gate_up"], full["w_down"] = mix_out_fwd(
        o_hg, proj_h, o_sb, x2, mix_norms, attn_post_norm, w_out_full, comm=gather_over_d2d([landed_gu, landed_down]))
    w_down_full = full["w_down"].reshape(D_FF, D_MODEL)
    u2, gu, act, y, h2 = ffn_fwd(h1, ffn_pre_norm, ffn_post_norm, full["w_gate_up"], w_down_full)

    core = lax.axis_index("c").astype(jnp.int32).reshape(1)
    de, ds, dh2, h2b, pb, pack_loss = ple_loss(h2, p2, target, full["ple_proj"], w_pg_full)
    dy, dgu, dh1, pack_ffn = ffn_bwd(dh2, y, h1, gu, ffn_pre_norm, ffn_post_norm, full["w_gate_up"], w_down_full)
    local = {}
    local["w_gate_up"], = weight_grad(u2, dgu, "grad_w_gate_up", tm=D_MODEL, tn=full["w_gate_up"].shape[2], tk=2048,
                                      col_pieces=True)
    grad_down, got_gu = weight_grad(act, dy, "grad_w_down", tm=D_FF // 2, tn=D_MODEL, tk=1024,
                                    comm=core_halves([local["w_gate_up"]]))
    local["w_down"] = grad_down.reshape(full["w_down"].shape)
    local["ple_proj"] = _column_pieces(weight_grad(pb, de, "grad_ple_proj", tm=pb.shape[1], tn=D_MODEL, tk=1024)[0])
    grad_pg, got_down = weight_grad(h2b, ds, "grad_ple_gate", tm=D_MODEL, tn=D_MODEL, tk=1024,
                                    comm=core_halves([local["w_down"]]))
    local["ple_gate"] = grad_pg.reshape(full["ple_gate"].shape)
    early = list(local)
    dmix, do_hg, dhg, do_sb, pack_mix, *got_ple = mix_out_bwd(
        dh1, mix, o_hg, proj_h, o_sb, mix_norms, attn_post_norm, w_out_full,
        comm=core_halves([local["ple_proj"], local["ple_gate"]]))
    got = [got_gu, got_down] + got_ple
    partial = [add_core_halves(local[k], g, core, "add_core_halves_" + k) for k, g in zip(early, got)]
    local["w_out"] = weight_grad(cat, dmix, "grad_w_out", tm=D_MODEL, tn=D_MODEL, tk=1024)[0].reshape(full["w_out"].shape)
    dsq, dsk, dsv = sb_bwd(sqkv, do_sb, sb_totals, sb_first)
    dhq, dhf, dhi, pack_hg, *by_source, got_out = hgrn2_bwd(
        proj_h, hg_lower_gamma, states, do_hg, comm=_both(chip_partials(partial[:2]), core_halves([local["w_out"]])))
    early.append("w_out")
    partial.append(add_core_halves(local["w_out"], got_out, core, "add_core_halves_w_out"))
    dproj, grad_x, pack_in = in_proj_bwd([dhq, dhf, dhi, dhg, dsq, dsk, dsv], x2, dh1, attn_pre_norm, full["w_in"])

    late = ["w_in"]
    local["w_in"], *more = weight_grad(u1, dproj, "grad_w_in", tm=D_MODEL, tn=full["w_in"].shape[2], tk=2048,
                                       col_pieces=True, comm=chip_partials(partial[2:]))
    halves = {k: add_chip_partials(s, core, "add_chip_partials_" + k) for k, s in zip(early, by_source + more)}
    got = _run_comm(core_halves([local[k] for k in late]), "exchange_core_halves")
    partial = [add_core_halves(local[k], g, core, "add_core_halves_" + k) for k, g in zip(late, got)]
    by_source = _run_comm(chip_partials(partial), "exchange_chip_partials")
    halves.update({k: add_chip_partials(s, core, "add_chip_partials_" + k) for k, s in zip(late, by_source)})
    grads = dict(zip(names, join_core_halves([halves[k] for k in names])))

    upd = {k: adamw(big[k][0], grads[k], big[k][1], big[k][2], "adamw_" + k) for k in names}

    small = reduce_small(
        [pack_loss, pack_ffn, pack_mix, pack_hg, pack_in],
        _pack_small(attn_pre_norm, hg_lower_gamma, hg_out_norm, sb_out_norm, attn_post_norm, ffn_pre_norm, ffn_post_norm),
        _pack_small(m_attn_pre_norm, m_hg_lower_gamma, m_hg_out_norm, m_sb_out_norm, m_attn_post_norm, m_ffn_pre_norm, m_ffn_post_norm),
        _pack_small(v_attn_pre_norm, v_hg_lower_gamma, v_hg_out_norm, v_sb_out_norm, v_attn_post_norm, v_ffn_pre_norm, v_ffn_post_norm),
    )
    loss = small[0][ROW_LOSS, 0]
    s_grad, s_delta, s_m, s_v = (_unpack_small(t) for t in small)

    def ordered(small_vals, big_vals):
        a_pre, gam, hg_n, sb_n, a_post, f_pre, f_post = small_vals
        b = {k: big_vals[k][None] for k in names}
        return (a_pre, b["w_in"], gam, hg_n, sb_n, b["w_out"], a_post, f_pre, b["w_gate_up"], b["w_down"], f_post,
                b["ple_proj"], b["ple_gate"])

    return (loss, grad_x[None],
            *ordered(s_grad, {k: upd[k][0] for k in names}),
            *ordered(s_delta, {k: upd[k][1] for k in names}),
            *ordered(s_m, {k: upd[k][2] for k in names}),
            *ordered(s_v, {k: upd[k][3] for k in names}))
```

```python
from typing import Callable, NamedTuple

import numpy as np
import jax
import jax.numpy as jnp
from jax import lax
from jax.experimental import pallas as pl
from jax.experimental.pallas import tpu as pltpu

F32 = jnp.float32
BF16 = jnp.bfloat16
MESH = pl.DeviceIdType.MESH

RMS_EPS = 1e-6
D_MODEL = 1024
HG_WIDTH = 512
HG_HEADS = 4
HG_DK = 128
HG_CHUNK = 64
HG_LEVELS = (32, 16, 8, 4, 2, 1)
HG_CHUNKS_PER_STEP = 8
HG_CHUNKS_PER_STEP_BWD = 4
SB_WIDTH = 512
SB_BLOCK = 128
SB_DH = 64
SB_SCALE = SB_DH ** -0.5
SB_UNDERFLOW_LOG = -87.5
SB_UNROLL = 2
SB_GROUP = 4
D_FF = 2816
N_CHIPS = 4
ROW_TILE = 256
WIDE_ROW_TILE = 512
V7X_VMEM_LIMIT = 56 * 1024 * 1024

ADAM_LR = 0.001
ADAM_B1 = 0.9
ADAM_B2 = 0.999
ADAM_EPS = 1e-08
ADAM_WD = 0.01
ADAM_STEP = 10

ROW_ATTN_PRE, ROW_GAMMA, ROW_MIX_NORMS, ROW_ATTN_POST, ROW_FFN_PRE, ROW_FFN_POST, ROW_LOSS = range(7)


def _params(sem=None, vmem=V7X_VMEM_LIMIT):
    return pltpu.CompilerParams(dimension_semantics=sem, vmem_limit_bytes=vmem)


def _dot(a, b):
    return jnp.dot(a.astype(BF16), b.astype(BF16), preferred_element_type=F32)


def _dot_nt(a, b):
    return lax.dot_general(a.astype(BF16), b.astype(BF16), (((1,), (1,)), ((), ())), preferred_element_type=F32)


def _dot_tn(a, b):
    return lax.dot_general(a.astype(BF16), b.astype(BF16), (((0,), (0,)), ((), ())), preferred_element_type=F32)


def _split(x):
    hi = x.astype(BF16)
    lo = (x - hi.astype(F32)).astype(BF16)
    return hi, lo


def _sum01_left(m01, x):
    hi, lo = _split(x)
    return jnp.dot(m01, hi, preferred_element_type=F32) + jnp.dot(m01, lo, preferred_element_type=F32)


def _sum01_right(x, m01_twice):
    hi, lo = _split(x)
    return jnp.dot(jnp.concatenate([hi, lo], axis=1), m01_twice, preferred_element_type=F32)


def _rms(x):
    r = lax.rsqrt(jnp.mean(x * x, axis=-1, keepdims=True) + RMS_EPS)
    return x * r, r


def _rms_bwd(dy, xhat, r, w):
    dxh = dy * w
    dx = r * (dxh - xhat * jnp.mean(dxh * xhat, axis=-1, keepdims=True))
    return dx, dy * xhat


def _sigmoid(x):
    return 1.0 / (1.0 + jnp.exp(-x))


def _neg_softplus(z):
    return -(jnp.maximum(z, 0.0) + jnp.log(1.0 + jnp.exp(-jnp.abs(z))))


def _colsum(x):
    return jnp.sum(x, axis=0, keepdims=True)


def _load_once(src_hbm, dst_vmem):
    @pl.when(pl.program_id(0) == 0)
    def _():
        pltpu.sync_copy(src_hbm, dst_vmem)


def _zero_first(ref):
    @pl.when(pl.program_id(0) == 0)
    def _():
        ref[...] = jnp.zeros(ref.shape, ref.dtype)


def _row_spec(width, col=0, rows=ROW_TILE):
    return pl.BlockSpec((rows, width), lambda i, col=col: (i, col))


def _wide_spec(width, col=0):
    return _row_spec(width, col, WIDE_ROW_TILE)


def _wide_tile(T):
    assert T % WIDE_ROW_TILE == 0
    return WIDE_ROW_TILE


def _full_spec(shape):
    return pl.BlockSpec(shape, lambda *_: (0,) * len(shape))


ANY_SPEC = pl.BlockSpec(memory_space=pl.ANY)
PACK_SPEC = _full_spec((8, D_MODEL))


class Comm(NamedTuple):
    inputs: list
    out_shape: list
    aliases: dict
    scratch: list
    start: Callable
    finish: Callable


def _pallas(body, *, comm=None, edge=None, in_specs, out_specs, out_shape, scratch_shapes=(), **kw):
    if comm is None:
        return pl.pallas_call(body, in_specs=in_specs, out_specs=out_specs, out_shape=out_shape,
                              scratch_shapes=scratch_shapes, **kw)
    n_in, n_out, n_scr = len(in_specs), len(out_specs), len(scratch_shapes)
    c_in, c_out = len(comm.inputs), len(comm.out_shape)

    def both(*refs):
        ins, c_ins = refs[:n_in], refs[n_in:n_in + c_in]
        outs = refs[n_in + c_in:n_in + c_in + n_out]
        c_outs = refs[n_in + c_in + n_out:n_in + c_in + n_out + c_out]
        rest = refs[n_in + c_in + n_out + c_out:]
        scr, c_scr = rest[:n_scr], rest[n_scr:]
        first, last = edge()

        @pl.when(first)
        def _():
            comm.start(c_ins, c_outs, c_scr)

        body(*ins, *outs, *scr)

        @pl.when(last)
        def _():
            comm.finish(c_ins, c_outs, c_scr)

    call = pl.pallas_call(
        both, in_specs=list(in_specs) + [ANY_SPEC] * c_in, out_specs=list(out_specs) + [ANY_SPEC] * c_out,
        out_shape=list(out_shape) + list(comm.out_shape), scratch_shapes=list(scratch_shapes) + list(comm.scratch),
        input_output_aliases={n_in + a: n_out + b for a, b in comm.aliases.items()}, **kw)
    return lambda *args: call(*args, *comm.inputs)


def _grid_edge(steps):
    return lambda: (pl.program_id(0) == 0, pl.program_id(0) == steps - 1)


def in_proj_fwd(x, g_pre, w_in, comm=None):
    T = x.shape[0]
    pw = w_in.shape[2]

    def body(x_ref, g_ref, w_hbm, ph_ref, sqkv_ref, u_ref, w_vmem, proj_s):
        _load_once(w_hbm, w_vmem)
        xh, _ = _rms(x_ref[...])
        u = (xh * g_ref[...]).astype(BF16)
        u_ref[...] = u
        for q in range(N_CHIPS):
            proj_s[:, pw * q:pw * (q + 1)] = jnp.dot(u, w_vmem[q], preferred_element_type=F32)
        ph_ref[...] = proj_s[:, :4 * HG_WIDTH]
        sqkv_ref[:, :SB_WIDTH] = (proj_s[:, 4 * HG_WIDTH:4 * HG_WIDTH + SB_WIDTH] * SB_SCALE).astype(BF16)
        sqkv_ref[:, SB_WIDTH:] = proj_s[:, 4 * HG_WIDTH + SB_WIDTH:].astype(BF16)

    return _pallas(
        body, comm=comm, edge=_grid_edge(T // _wide_tile(T)), name="in_proj_fwd", grid=(T // _wide_tile(T),),
        in_specs=[_wide_spec(D_MODEL), _full_spec((1, D_MODEL)), ANY_SPEC],
        out_specs=[_wide_spec(4 * HG_WIDTH), _wide_spec(3 * SB_WIDTH), _wide_spec(D_MODEL)],
        out_shape=[jax.ShapeDtypeStruct((T, 4 * HG_WIDTH), F32), jax.ShapeDtypeStruct((T, 3 * SB_WIDTH), BF16),
                   jax.ShapeDtypeStruct((T, D_MODEL), BF16)],
        scratch_shapes=[pltpu.VMEM(w_in.shape, BF16), pltpu.VMEM((_wide_tile(T), N_CHIPS * pw), F32)],
        compiler_params=_params(("arbitrary",)),
    )(x, g_pre, w_in)


def _hg_sum_matrix():
    C = HG_CHUNK
    t = np.arange(C)[:, None]
    j = np.arange(C)[None, :]
    mats = [j <= t, j > t]
    for h in HG_LEVELS:
        start = (t // (2 * h)) * (2 * h)
        upper = (t & h) != 0
        mats.append(np.where(upper, (j >= start + h) & (j <= t), (j > t) & (j <= start + h - 1)))
    return np.concatenate(mats, 0).astype(np.float32)


def _hg_level_masks():
    C = HG_CHUNK
    t = lax.broadcasted_iota(jnp.int32, (C, C), 0)
    s = lax.broadcasted_iota(jnp.int32, (C, C), 1)
    x = t ^ s
    masks = [t == s]
    for h in HG_LEVELS:
        masks.append((x >= h) & (x < 2 * h) & (t > s))
    return masks


def _hg_gates(hq, hf, gamma):
    lb = 1.0 / (1.0 + jnp.exp(gamma[1:2, :] - gamma[0:1, :]))
    sq = _sigmoid(hq)
    q = hq * sq
    sig = _sigmoid(hf)
    nsig = _sigmoid(-hf)
    f = lb + (1.0 - lb) * sig
    k = (1.0 - lb) * nsig
    g = jnp.log(f)
    return q, k, g, dict(lb=lb, sq=sq, sig=sig, nsig=nsig, f=f)


def _hg_head_decays(A, h):
    C, K = HG_CHUNK, HG_DK
    sl = slice(K * h, K * (h + 1))
    blocks = [A[C * r:C * (r + 1), sl] for r in range(2 + len(HG_LEVELS))]
    return blocks[0], blocks[1], [None] + blocks[2:]


def _hg_products(q, k, levels):
    return [_dot_nt(q, k)] + [_dot_nt(q * a, k * a) for a in levels[1:]]


def _hg_select(prods, masks):
    sc = jnp.where(masks[0], prods[0], 0.0)
    for p, m in zip(prods[1:], masks[1:]):
        sc = jnp.where(m, p, sc)
    return sc


def hgrn2_fwd(proj_h, gamma, comm=None):
    T = proj_h.shape[0]
    C, K, H, S = HG_CHUNK, HG_DK, HG_HEADS, HG_CHUNKS_PER_STEP
    n_steps = T // (S * C)
    msum = jnp.asarray(_hg_sum_matrix(), BF16)

    def body(hq_ref, hf_ref, hi_ref, gam_ref, msum_ref, o_ref, st_ref, st_s):
        _zero_first(st_s)
        q, k, g, _ = _hg_gates(hq_ref[...], hf_ref[...], gam_ref[...])
        v = hi_ref[...]
        masks = _hg_level_masks()
        parts = []
        for s in range(S):
            rows = slice(C * s, C * (s + 1))
            A = jnp.exp(_sum01_left(msum_ref[...], g[rows]))
            for h in range(H):
                sl = slice(K * h, K * (h + 1))
                ab, ar, levels = _hg_head_decays(A, h)
                parts.append(dict(s=s, h=h, rows=rows, sl=sl, ab=ab, ar=ar, levels=levels,
                                  q=q[rows, sl], k=k[rows, sl], v=v[rows, sl]))
        for pt in parts:
            pt["prods"] = _hg_products(pt["q"], pt["k"], pt["levels"])
            pt["grown"] = _dot_tn(pt["v"], pt["k"] * pt["ar"])
        for pt in parts:
            pt["sc"] = _hg_select(pt["prods"], masks)
        state = [st_s[h] for h in range(H)]
        for pt in parts:
            h, ab = pt["h"], pt["ab"]
            o_ref[pt["rows"], pt["sl"]] = _dot_nt(pt["q"] * ab, state[h]) + _dot(pt["sc"], pt["v"])
            state[h] = state[h] * ab[C - 1:C, :] + pt["grown"]
            st_ref[pt["s"], h] = state[h]
        for h in range(H):
            st_s[h] = state[h]

    blk = lambda col: pl.BlockSpec((S * C, HG_WIDTH), lambda c, col=col: (c, col))
    return _pallas(
        body, comm=comm, edge=_grid_edge(n_steps), name="hgrn2_fwd", grid=(n_steps,),
        in_specs=[blk(0), blk(1), blk(2), _full_spec((2, HG_WIDTH)), _full_spec(msum.shape)],
        out_specs=[blk(0), pl.BlockSpec((S, H, K, K), lambda c: (c, 0, 0, 0))],
        out_shape=[jax.ShapeDtypeStruct((T, HG_WIDTH), F32), jax.ShapeDtypeStruct((S * n_steps, H, K, K), F32)],
        scratch_shapes=[pltpu.VMEM((H, K, K), F32)],
        compiler_params=_params(("arbitrary",)),
    )(proj_h, proj_h, proj_h, gamma, msum)


def hgrn2_bwd(proj_h, gamma, states, do, comm=None):
    T = proj_h.shape[0]
    C, K, H, S = HG_CHUNK, HG_DK, HG_HEADS, HG_CHUNKS_PER_STEP_BWD
    n_steps = T // (S * C)
    n_sums = 2 + len(HG_LEVELS)
    msum = jnp.asarray(_hg_sum_matrix(), BF16)
    msum_t = jnp.asarray(_hg_sum_matrix().T, BF16)

    def body(hq_ref, hf_ref, hi_ref, do_ref, gam_ref, msum_ref, msum_t_ref, st_prev_ref, st_ref,
             dhq_ref, dhf_ref, dhi_ref, pack_ref, dst_s, dlb_s, dq_s, dk_s, de_s):
        step = pl.program_id(0)
        _zero_first(dst_s)
        _zero_first(dlb_s)
        _zero_first(pack_ref)
        hq = hq_ref[...]
        q, k, g, aux = _hg_gates(hq, hf_ref[...], gam_ref[...])
        v = hi_ref[...]
        do_all = do_ref[...]
        masks = _hg_level_masks()
        is_last_row = lax.broadcasted_iota(jnp.int32, (C, K), 0) == C - 1
        has_prev = (step < n_steps - 1).astype(F32)
        parts = []
        for s in reversed(range(S)):
            rows = slice(C * s, C * (s + 1))
            A = jnp.exp(_sum01_left(msum_ref[...], g[rows]))
            for h in range(H):
                sl = slice(K * h, K * (h + 1))
                ab, ar, levels = _hg_head_decays(A, h)
                st_in = st_prev_ref[0, h] * has_prev if s == 0 else st_ref[s - 1, h]
                parts.append(dict(s=s, h=h, rows=rows, sl=sl, ab=ab, ar=ar, levels=levels, st_in=st_in,
                                  q=q[rows, sl], k=k[rows, sl], v=v[rows, sl], do=do_all[rows, sl]))
        for pt in parts:
            pt["prods"] = _hg_products(pt["q"], pt["k"], pt["levels"])
            pt["da"] = _dot_nt(pt["do"], pt["v"])
            pt["t1"] = pt["ab"] * _dot(pt["do"], pt["st_in"])
            pt["dst_add"] = _dot_tn(pt["do"], pt["q"] * pt["ab"])
        dstate = [dst_s[h] for h in range(H)]
        for pt in parts:
            h = pt["h"]
            pt["dst_out"] = dstate[h]
            pt["t2"] = pt["ar"] * _dot(pt["v"], dstate[h])
            pt["dv_state"] = _dot_nt(pt["k"] * pt["ar"], dstate[h])
            dstate[h] = dstate[h] * pt["ab"][C - 1:C, :] + pt["dst_add"]
        for h in range(H):
            dst_s[h] = dstate[h]
        for pt in parts:
            pt["sc"] = _hg_select(pt["prods"], masks)
            pt["dam"] = [jnp.where(m, pt["da"], 0.0) for m in masks]
        for pt in parts:
            qh, kh = pt["q"], pt["k"]
            pt["dq_parts"] = [_dot(pt["dam"][0], kh)] + [
                a * _dot(dam, kh * a) for a, dam in zip(pt["levels"][1:], pt["dam"][1:])]
            pt["dk_parts"] = [_dot_tn(pt["dam"][0], qh)] + [
                a * _dot_tn(dam, qh * a) for a, dam in zip(pt["levels"][1:], pt["dam"][1:])]
            pt["dv_intra"] = _dot_tn(pt["sc"], pt["do"])
        for pt in parts:
            s, rows, sl, qh, kh, ab = pt["s"], pt["rows"], pt["sl"], pt["q"], pt["k"], pt["ab"]
            decayed = _colsum(pt["st_in"] * pt["dst_out"]) * ab[C - 1:C, :]
            de_s[s, 0:C, sl] = qh * pt["t1"] + jnp.where(is_last_row, decayed, 0.0)
            de_s[s, C:2 * C, sl] = kh * pt["t2"]
            dq = pt["t1"] + pt["dq_parts"][0]
            dk = pt["t2"] + pt["dk_parts"][0]
            for r, (t1, t2) in enumerate(zip(pt["dq_parts"][1:], pt["dk_parts"][1:])):
                dq = dq + t1
                dk = dk + t2
                de_s[s, C * (r + 2):C * (r + 3), sl] = qh * t1 + kh * t2
            dhi_ref[rows, sl] = pt["dv_intra"] + pt["dv_state"]
            dq_s[rows, sl] = dq
            dk_s[rows, sl] = dk
        dg = jnp.concatenate([_sum01_left(msum_t_ref[...], de_s[s]) for s in range(S)], axis=0)
        dk = dk_s[...]
        sq, lb = aux["sq"], aux["lb"]
        dhq_ref[...] = dq_s[...] * (sq * (1.0 + hq * (1.0 - sq)))
        common = dg / aux["f"] - dk
        dhf_ref[...] = (1.0 - lb) * aux["sig"] * aux["nsig"] * common
        dlb_s[...] += _colsum(aux["nsig"] * common)

        @pl.when(step == n_steps - 1)
        def _():
            dgam = lb * (1.0 - lb) * dlb_s[...]
            pack_ref[ROW_GAMMA:ROW_GAMMA + 1, :HG_WIDTH] = dgam
            pack_ref[ROW_GAMMA:ROW_GAMMA + 1, HG_WIDTH:] = -dgam

    last = n_steps - 1
    blk = lambda col: pl.BlockSpec((S * C, HG_WIDTH), lambda c, col=col: (last - c, col))
    return _pallas(
        body, comm=comm, edge=_grid_edge(n_steps), name="hgrn2_bwd", grid=(n_steps,),
        in_specs=[blk(0), blk(1), blk(2), blk(0), _full_spec((2, HG_WIDTH)), _full_spec(msum.shape),
                  _full_spec(msum_t.shape),
                  pl.BlockSpec((1, H, K, K), lambda c: (jnp.maximum(S * (last - c) - 1, 0), 0, 0, 0)),
                  pl.BlockSpec((S, H, K, K), lambda c: (last - c, 0, 0, 0))],
        out_specs=[blk(0), blk(0), blk(0), PACK_SPEC],
        out_shape=[jax.ShapeDtypeStruct((T, HG_WIDTH), F32)] * 3 + [jax.ShapeDtypeStruct((8, D_MODEL), F32)],
        scratch_shapes=[pltpu.VMEM((H, K, K), F32), pltpu.VMEM((1, HG_WIDTH), F32), pltpu.VMEM((S * C, HG_WIDTH), F32),
                        pltpu.VMEM((S * C, HG_WIDTH), F32), pltpu.VMEM((S, n_sums * C, HG_WIDTH), F32)],
        compiler_params=_params(("arbitrary",)),
    )(proj_h, proj_h, proj_h, do, gamma, msum, msum_t, states, states)


def _sb_sum_matrix(inclusive):
    B = SB_BLOCK
    j = np.arange(B)[:, None]
    s = np.arange(B)[None, :]
    tri = (j >= s) if inclusive else (j > s)
    once = np.concatenate([tri, np.ones((B, B), bool)], 1).astype(np.float32)
    return np.concatenate([once, once], 0)


def _sb_prefix_matrix(inclusive):
    B = SB_BLOCK
    j = np.arange(B)[:, None]
    s = np.arange(B)[None, :]
    tri = (j <= s) if inclusive else (j < s)
    once = np.concatenate([tri, np.ones((B, B), bool)], 1).astype(np.float32)
    return np.concatenate([once, once], 0)


def _sb_iotas():
    shape = (SB_BLOCK, SB_BLOCK)
    return lax.broadcasted_iota(jnp.int32, shape, 0), lax.broadcasted_iota(jnp.int32, shape, 1)


def _sb_heads(q, first):
    heads = []
    for g in range(SB_GROUP):
        qg = q[:, SB_BLOCK * g:SB_BLOCK * (g + 1)]
        zero = jnp.zeros_like(qg)
        heads += [(g, jnp.where(first, qg, zero)), (g, jnp.where(first, zero, qg))]
    return heads


def _lanes(x, g):
    return x[:, SB_BLOCK * g:SB_BLOCK * (g + 1)]


def sb_fwd(sqkv, comm=None):
    T = sqkv.shape[0]
    B = SB_BLOCK
    W = SB_GROUP * B
    groups = SB_WIDTH // W
    usum = jnp.asarray(_sb_sum_matrix(False), BF16)

    def body(q_ref, k_ref, v_ref, u_ref, o_ref, tl_ref, first_ref):
        p, i = pl.program_id(0), pl.program_id(1)
        row, lane = _sb_iotas()
        first = lane < SB_DH
        heads = _sb_heads(q_ref[...], first)
        u = u_ref[...]

        def more(loop):
            n, reachable, _ = loop
            return (SB_UNROLL * n <= i) & (reachable > 0)

        def step(loop):
            n, _, state = loop
            blocks = []
            for sub in range(SB_UNROLL):
                j = i - SB_UNROLL * n - sub
                off = pl.multiple_of(jnp.maximum(j, 0) * B, B)
                valid = ((lane + j * B) < (row + i * B)) & (j >= 0)
                blocks.append((k_ref[pl.ds(off, B), :], v_ref[pl.ds(off, B), :], valid))
            z = [[_dot_nt(qh, _lanes(kj, g)) for g, qh in heads] for kj, _, _ in blocks]
            lnb = [[jnp.where(valid, _neg_softplus(zz), 0.0) for zz in zs] for zs, (_, _, valid) in zip(z, blocks)]
            sums = [[_sum01_right(x, u) for x in xs] for xs in lnb]
            out = []
            for h, (carry, acc) in enumerate(state):
                for sub, (_, vj, valid) in enumerate(blocks):
                    expo = z[sub][h] + lnb[sub][h] + carry + sums[sub][h][:, :B]
                    acc = acc + _dot(jnp.where(valid, jnp.exp(expo), 0.0), _lanes(vj, heads[h][0]))
                    carry = carry + sums[sub][h][:, B:]
                out.append((carry, acc))
            state = tuple(out)
            worst = state[0][0]
            for carry, _ in state[1:]:
                worst = jnp.maximum(worst, carry)
            reachable = (jnp.max(worst) > SB_UNDERFLOW_LOG).astype(jnp.int32)
            return n + 1, reachable, state

        zero = jnp.zeros((B, B), F32)
        done, _, state = lax.while_loop(
            more, step, (jnp.int32(0), jnp.int32(1), tuple((zero, zero) for _ in heads)))
        for g in range(SB_GROUP):
            (tot0, acc0), (tot1, acc1) = state[2 * g], state[2 * g + 1]
            o_ref[:, B * g:B * (g + 1)] = jnp.where(first, acc0, acc1)
            tl_ref[:, B * g:B * (g + 1)] = jnp.where(first, tot0, tot1)
        first_ref[p, i] = jnp.maximum(i + 1 - SB_UNROLL * done, 0)

    def edge():
        p, i = pl.program_id(0), pl.program_id(1)
        return (p == 0) & (i == 0), (p == groups - 1) & (i == T // B - 1)

    return _pallas(
        body, comm=comm, edge=edge, name="sb_fwd", grid=(groups, T // B),
        in_specs=[pl.BlockSpec((B, W), lambda p, i: (i, p)),
                  pl.BlockSpec((T, W), lambda p, i: (0, groups + p)),
                  pl.BlockSpec((T, W), lambda p, i: (0, 2 * groups + p)),
                  pl.BlockSpec(usum.shape, lambda p, i: (0, 0))],
        out_specs=[pl.BlockSpec((B, W), lambda p, i: (i, p))] * 2 + [pl.BlockSpec(memory_space=pltpu.SMEM)],
        out_shape=[jax.ShapeDtypeStruct((T, SB_WIDTH), F32)] * 2 + [jax.ShapeDtypeStruct((groups, T // B), jnp.int32)],
        compiler_params=_params(("arbitrary", "arbitrary")),
    )(sqkv, sqkv, sqkv, usum)


def sb_bwd(sqkv, do, tl, first_block):
    T = sqkv.shape[0]
    B = SB_BLOCK
    W = SB_GROUP * B
    groups = SB_WIDTH // W
    upre = jnp.asarray(_sb_prefix_matrix(True), BF16)
    uexc = jnp.asarray(_sb_prefix_matrix(False), BF16)

    def body(q_ref, k_ref, v_ref, do_ref, tl_ref, up_ref, ue_ref, first_ref, dq_ref, dk_ref, dv_ref):
        p, i = pl.program_id(0), pl.program_id(1)

        @pl.when(i == 0)
        def _():
            dk_ref[...] = jnp.zeros(dk_ref.shape, F32)
            dv_ref[...] = jnp.zeros(dv_ref.shape, F32)

        row, lane = _sb_iotas()
        first = lane < SB_DH
        do = do_ref[...]
        tl_all = tl_ref[...]
        heads = []
        for (g, qh), at in zip(_sb_heads(q_ref[...], first), (0, B - 1) * SB_GROUP):
            dog = _lanes(do, g)
            keep = first if at == 0 else jnp.logical_not(first)
            heads.append((g, qh, jnp.where(keep, dog, jnp.zeros_like(dog)).astype(BF16),
                          _lanes(tl_all, g)[:, at:at + 1]))
        up = up_ref[...]
        ue = ue_ref[...]
        start = first_ref[p, i]

        def step(n, state):
            blocks = []
            for sub in range(SB_UNROLL):
                j = start + SB_UNROLL * n + sub
                off = pl.multiple_of(jnp.minimum(j, i) * B, B)
                valid = (lane + j * B) < (row + i * B)
                blocks.append((off, k_ref[pl.ds(off, B), :], v_ref[pl.ds(off, B), :], valid))
            combos = [(s, h) for s in range(SB_UNROLL) for h in range(len(heads))]
            z = {(s, h): _dot_nt(heads[h][1], _lanes(blocks[s][1], heads[h][0])) for s, h in combos}
            da = {(s, h): _dot_nt(heads[h][2], _lanes(blocks[s][2], heads[h][0])) for s, h in combos}
            lnb = {c: jnp.where(blocks[c[0]][3], _neg_softplus(z[c]), 0.0) for c in combos}
            lb = {c: z[c] + lnb[c] for c in combos}
            sums = {c: _sum01_right(lnb[c], up) for c in combos}
            a, w = {}, {}
            seen = [st[0] for st in state]
            for s, h in combos:
                expo = lb[s, h] + (heads[h][3] - seen[h] - sums[s, h][:, :B])
                a[s, h] = jnp.where(blocks[s][3], jnp.exp(expo), 0.0)
                w[s, h] = a[s, h] * da[s, h]
                seen[h] = seen[h] + sums[s, h][:, B:]
            wsums = {c: _sum01_right(w[c], ue) for c in combos}
            dz = {}
            seen_w = [st[1] for st in state]
            for s, h in combos:
                beta = jnp.exp(lb[s, h])
                before = seen_w[h] + wsums[s, h][:, :B]
                dz[s, h] = jnp.where(blocks[s][3], w[s, h] * (1.0 - beta) - before * beta, 0.0)
                seen_w[h] = seen_w[h] + wsums[s, h][:, B:]
            dq = [st[2] for st in state]
            for s, h in combos:
                dq[h] = dq[h] + _dot(dz[s, h], _lanes(blocks[s][1], heads[h][0]))
            for s in range(SB_UNROLL):
                off = blocks[s][0]
                for g in range(SB_GROUP):
                    h0, h1 = 2 * g, 2 * g + 1
                    dk_ref[pl.ds(off, B), B * g:B * (g + 1)] += (_dot_tn(dz[s, h0], heads[h0][1])
                                                                 + _dot_tn(dz[s, h1], heads[h1][1]))
                    dv_ref[pl.ds(off, B), B * g:B * (g + 1)] += (_dot_tn(a[s, h0], heads[h0][2])
                                                                 + _dot_tn(a[s, h1], heads[h1][2]))
            return tuple(zip(seen, seen_w, dq))

        zero = jnp.zeros((B, B), F32)
        trips = (i - start + SB_UNROLL) // SB_UNROLL
        state = lax.fori_loop(0, trips, step, tuple((zero, zero, zero) for _ in heads))
        for g in range(SB_GROUP):
            dq_ref[:, B * g:B * (g + 1)] = jnp.where(first, state[2 * g][2], state[2 * g + 1][2]) * SB_SCALE

    qblk = pl.BlockSpec((B, W), lambda p, i: (i, p))
    full = pl.BlockSpec((T, W), lambda p, i: (0, p))
    return pl.pallas_call(
        body, name="sb_bwd", grid=(groups, T // B),
        in_specs=[qblk, pl.BlockSpec((T, W), lambda p, i: (0, groups + p)),
                  pl.BlockSpec((T, W), lambda p, i: (0, 2 * groups + p)), qblk, qblk,
                  pl.BlockSpec(upre.shape, lambda p, i: (0, 0)), pl.BlockSpec(uexc.shape, lambda p, i: (0, 0)),
                  pl.BlockSpec(memory_space=pltpu.SMEM)],
        out_specs=[qblk, full, full],
        out_shape=[jax.ShapeDtypeStruct((T, SB_WIDTH), F32)] * 3,
        compiler_params=_params(("arbitrary", "arbitrary")),
    )(sqkv, sqkv, sqkv, do, tl, upre, uexc, first_block)


def _mixer_out(o_hg, hg, o_sb, g_hg, g_sb):
    n_hg, r_hg = _rms(o_hg)
    s_hg = _sigmoid(hg)
    n_sb, r_sb = _rms(o_sb)
    return dict(n_hg=n_hg, r_hg=r_hg, s_hg=s_hg, n_sb=n_sb, r_sb=r_sb,
                y_hg=n_hg * g_hg * (hg * s_hg), y_sb=n_sb * g_sb)


def mix_out_fwd(o_hg, proj_h, o_sb, x, norms, g_post, w_out, comm=None):
    T = x.shape[0]

    def body(ohg_ref, hg_ref, osb_ref, x_ref, nrm_ref, gp_ref, w_hbm, cat_ref, mix_ref, h1_ref, w_vmem):
        _load_once(w_hbm, w_vmem)
        nrm = nrm_ref[...]
        m = _mixer_out(ohg_ref[...], hg_ref[...], osb_ref[...], nrm[:, :HG_WIDTH], nrm[:, HG_WIDTH:])
        cat_ref[:, :HG_WIDTH] = m["y_hg"].astype(BF16)
        cat_ref[:, HG_WIDTH:] = m["y_sb"].astype(BF16)
        mix = jnp.dot(cat_ref[...], w_vmem[...], preferred_element_type=F32)
        mix_ref[...] = mix
        mh, _ = _rms(mix)
        h1_ref[...] = x_ref[...] + mh * gp_ref[...]

    return _pallas(
        body, comm=comm, edge=_grid_edge(T // _wide_tile(T)), name="mix_out_fwd", grid=(T // _wide_tile(T),),
        in_specs=[_wide_spec(HG_WIDTH), _wide_spec(HG_WIDTH, 3), _wide_spec(SB_WIDTH), _wide_spec(D_MODEL),
                  _full_spec((1, D_MODEL)), _full_spec((1, D_MODEL)), ANY_SPEC],
        out_specs=[_wide_spec(D_MODEL)] * 3,
        out_shape=[jax.ShapeDtypeStruct((T, D_MODEL), BF16), jax.ShapeDtypeStruct((T, D_MODEL), F32),
                   jax.ShapeDtypeStruct((T, D_MODEL), F32)],
        scratch_shapes=[pltpu.VMEM(w_out.shape, BF16)],
        compiler_params=_params(("arbitrary",)),
    )(o_hg, proj_h, o_sb, x, norms, g_post, w_out)


def ffn_fwd(h1, g_pre, g_post, w_gu, w_down):
    T = h1.shape[0]
    pw = w_gu.shape[2]

    def body(h1_ref, gpre_ref, gpost_ref, wgu_hbm, wd_hbm, u2_ref, gu_ref, act_ref, y_ref, h2_ref,
             wgu_vmem, wd_vmem, gu_s):
        _load_once(wgu_hbm, wgu_vmem)
        _load_once(wd_hbm, wd_vmem)
        h1v = h1_ref[...]
        hh, _ = _rms(h1v)
        u2 = (hh * gpre_ref[...]).astype(BF16)
        u2_ref[...] = u2
        for q in range(N_CHIPS):
            gu_s[:, pw * q:pw * (q + 1)] = jnp.dot(u2, wgu_vmem[q], preferred_element_type=F32)
        gu_ref[...] = gu_s[...].astype(BF16)
        gate = gu_s[:, :D_FF]
        act = (gate * _sigmoid(gate) * gu_s[:, D_FF:]).astype(BF16)
        act_ref[...] = act
        y = jnp.dot(act, wd_vmem[...], preferred_element_type=F32)
        y_ref[...] = y
        yh, _ = _rms(y)
        h2_ref[...] = h1v + yh * gpost_ref[...]

    return pl.pallas_call(
        body, name="ffn_fwd", grid=(T // ROW_TILE,),
        in_specs=[_row_spec(D_MODEL), _full_spec((1, D_MODEL)), _full_spec((1, D_MODEL)), ANY_SPEC, ANY_SPEC],
        out_specs=[_row_spec(D_MODEL), _row_spec(2 * D_FF), _row_spec(D_FF), _row_spec(D_MODEL), _row_spec(D_MODEL)],
        out_shape=[jax.ShapeDtypeStruct((T, D_MODEL), BF16), jax.ShapeDtypeStruct((T, 2 * D_FF), BF16),
                   jax.ShapeDtypeStruct((T, D_FF), BF16), jax.ShapeDtypeStruct((T, D_MODEL), F32),
                   jax.ShapeDtypeStruct((T, D_MODEL), F32)],
        scratch_shapes=[pltpu.VMEM(w_gu.shape, BF16), pltpu.VMEM(w_down.shape, BF16),
                        pltpu.VMEM((ROW_TILE, 2 * D_FF), F32)],
        compiler_params=_params(("arbitrary",)),
    )(h1, g_pre, g_post, w_gu, w_down)


def ple_loss(h2, p, target, w_ple, w_pg):
    T = h2.shape[0]
    pw = w_ple.shape[2]

    def body(h2_ref, p_ref, t_ref, wple_hbm, wpg_hbm, de_ref, ds_ref, dh2_ref, h2b_ref, pb_ref, pack_ref,
             wple_vmem, wpg_vmem, e_s):
        _load_once(wple_hbm, wple_vmem)
        _load_once(wpg_hbm, wpg_vmem)
        _zero_first(pack_ref)
        h2v = h2_ref[...]
        h2b = h2v.astype(BF16)
        h2b_ref[...] = h2b
        pb = p_ref[...].astype(BF16)
        pb_ref[...] = pb
        for q in range(N_CHIPS):
            e_s[:, pw * q:pw * (q + 1)] = jnp.dot(pb, wple_vmem[q], preferred_element_type=F32)
        e = e_s[...]
        sig = _sigmoid(jnp.dot(h2b, wpg_vmem[...], preferred_element_type=F32))
        err = h2v + e * sig - t_ref[...]
        part = 0.5 * jnp.sum(jnp.mean(err * err, axis=-1, keepdims=True), axis=0, keepdims=True)
        lane = lax.broadcasted_iota(jnp.int32, (1, D_MODEL), 1)
        pack_ref[ROW_LOSS:ROW_LOSS + 1, :] += jnp.where(lane == 0, part, 0.0)
        dh3 = err * (1.0 / D_MODEL)
        de_ref[...] = (dh3 * sig).astype(BF16)
        ds = (dh3 * e * sig * (1.0 - sig)).astype(BF16)
        ds_ref[...] = ds
        dh2_ref[...] = dh3 + _dot_nt(ds, wpg_vmem[...])

    return pl.pallas_call(
        body, name="ple_loss", grid=(T // _wide_tile(T),),
        in_specs=[_wide_spec(D_MODEL), _wide_spec(p.shape[1]), _wide_spec(D_MODEL), ANY_SPEC, ANY_SPEC],
        out_specs=[_wide_spec(D_MODEL), _wide_spec(D_MODEL), _wide_spec(D_MODEL), _wide_spec(D_MODEL),
                   _wide_spec(p.shape[1]), PACK_SPEC],
        out_shape=[jax.ShapeDtypeStruct((T, D_MODEL), BF16), jax.ShapeDtypeStruct((T, D_MODEL), BF16),
                   jax.ShapeDtypeStruct((T, D_MODEL), F32), jax.ShapeDtypeStruct((T, D_MODEL), BF16),
                   jax.ShapeDtypeStruct(p.shape, BF16), jax.ShapeDtypeStruct((8, D_MODEL), F32)],
        scratch_shapes=[pltpu.VMEM(w_ple.shape, BF16), pltpu.VMEM(w_pg.shape, BF16), pltpu.VMEM((_wide_tile(T), D_MODEL), F32)],
        compiler_params=_params(("arbitrary",)),
    )(h2, p, target, w_ple, w_pg)


def ffn_bwd(dh2, y, h1, gu, g_pre, g_post, w_gu, w_down):
    T = h1.shape[0]
    pw = w_gu.shape[2]

    def body(dh2_ref, y_ref, h1_ref, gu_ref, gpre_ref, gpost_ref, wgu_hbm, wd_hbm, dy_ref, dgu_ref, dh1_ref, pack_ref,
             wgu_vmem, wd_vmem):
        _load_once(wgu_hbm, wgu_vmem)
        _load_once(wd_hbm, wd_vmem)
        _zero_first(pack_ref)
        dh2v = dh2_ref[...]
        yh, ry = _rms(y_ref[...])
        dy, dw = _rms_bwd(dh2v, yh, ry, gpost_ref[...])
        pack_ref[ROW_FFN_POST:ROW_FFN_POST + 1, :] += _colsum(dw)
        dyb = dy.astype(BF16)
        dy_ref[...] = dyb
        dact = _dot_nt(dyb, wd_vmem[...])
        gate = gu_ref[:, :D_FF].astype(F32)
        up = gu_ref[:, D_FF:].astype(F32)
        sg = _sigmoid(gate)
        dgu_ref[:, :D_FF] = (dact * up * (sg * (1.0 + gate * (1.0 - sg)))).astype(BF16)
        dgu_ref[:, D_FF:] = (dact * gate * sg).astype(BF16)
        du2 = _dot_nt(dgu_ref[:, :pw], wgu_vmem[0])
        for q in range(1, N_CHIPS):
            du2 = du2 + _dot_nt(dgu_ref[:, pw * q:pw * (q + 1)], wgu_vmem[q])
        hh, rh = _rms(h1_ref[...])
        dh, dw = _rms_bwd(du2, hh, rh, gpre_ref[...])
        pack_ref[ROW_FFN_PRE:ROW_FFN_PRE + 1, :] += _colsum(dw)
        dh1_ref[...] = dh2v + dh

    return pl.pallas_call(
        body, name="ffn_bwd", grid=(T // ROW_TILE,),
        in_specs=[_row_spec(D_MODEL), _row_spec(D_MODEL), _row_spec(D_MODEL), _row_spec(2 * D_FF),
                  _full_spec((1, D_MODEL)), _full_spec((1, D_MODEL)), ANY_SPEC, ANY_SPEC],
        out_specs=[_row_spec(D_MODEL), _row_spec(2 * D_FF), _row_spec(D_MODEL), PACK_SPEC],
        out_shape=[jax.ShapeDtypeStruct((T, D_MODEL), BF16), jax.ShapeDtypeStruct((T, 2 * D_FF), BF16),
                   jax.ShapeDtypeStruct((T, D_MODEL), F32), jax.ShapeDtypeStruct((8, D_MODEL), F32)],
        scratch_shapes=[pltpu.VMEM(w_gu.shape, BF16), pltpu.VMEM(w_down.shape, BF16)],
        compiler_params=_params(("arbitrary",)),
    )(dh2, y, h1, gu, g_pre, g_post, w_gu, w_down)


def mix_out_bwd(dh1, mix, o_hg, proj_h, o_sb, norms, g_post, w_out, comm=None):
    T = dh1.shape[0]

    def body(dh1_ref, mix_ref, ohg_ref, hg_ref, osb_ref, nrm_ref, gp_ref, w_hbm, dmix_ref, dohg_ref, dhg_ref, dosb_ref,
             pack_ref, w_vmem):
        _load_once(w_hbm, w_vmem)
        _zero_first(pack_ref)
        mh, rm = _rms(mix_ref[...])
        dmix, dw = _rms_bwd(dh1_ref[...], mh, rm, gp_ref[...])
        pack_ref[ROW_ATTN_POST:ROW_ATTN_POST + 1, :] += _colsum(dw)
        dmb = dmix.astype(BF16)
        dmix_ref[...] = dmb
        dcat = _dot_nt(dmb, w_vmem[...])
        nrm = nrm_ref[...]
        g_hg, g_sb = nrm[:, :HG_WIDTH], nrm[:, HG_WIDTH:]
        hg = hg_ref[...]
        m = _mixer_out(ohg_ref[...], hg, osb_ref[...], g_hg, g_sb)
        d_hg = dcat[:, :HG_WIDTH]
        silu = hg * m["s_hg"]
        dhg_ref[...] = d_hg * (m["n_hg"] * g_hg) * (m["s_hg"] * (1.0 + hg * (1.0 - m["s_hg"])))
        dx, dw = _rms_bwd(d_hg * silu, m["n_hg"], m["r_hg"], g_hg)
        dohg_ref[...] = dx
        pack_ref[ROW_MIX_NORMS:ROW_MIX_NORMS + 1, :HG_WIDTH] += _colsum(dw)
        dx, dw = _rms_bwd(dcat[:, HG_WIDTH:], m["n_sb"], m["r_sb"], g_sb)
        dosb_ref[...] = dx
        pack_ref[ROW_MIX_NORMS:ROW_MIX_NORMS + 1, HG_WIDTH:] += _colsum(dw)

    return _pallas(
        body, comm=comm, edge=_grid_edge(T // _wide_tile(T)), name="mix_out_bwd", grid=(T // _wide_tile(T),),
        in_specs=[_wide_spec(D_MODEL), _wide_spec(D_MODEL), _wide_spec(HG_WIDTH), _wide_spec(HG_WIDTH, 3), _wide_spec(SB_WIDTH),
                  _full_spec((1, D_MODEL)), _full_spec((1, D_MODEL)), ANY_SPEC],
        out_specs=[_wide_spec(D_MODEL), _wide_spec(HG_WIDTH), _wide_spec(HG_WIDTH), _wide_spec(SB_WIDTH), PACK_SPEC],
        out_shape=[jax.ShapeDtypeStruct((T, D_MODEL), BF16), jax.ShapeDtypeStruct((T, HG_WIDTH), F32),
                   jax.ShapeDtypeStruct((T, HG_WIDTH), F32), jax.ShapeDtypeStruct((T, SB_WIDTH), F32),
                   jax.ShapeDtypeStruct((8, D_MODEL), F32)],
        scratch_shapes=[pltpu.VMEM(w_out.shape, BF16)],
        compiler_params=_params(("arbitrary",)),
    )(dh1, mix, o_hg, proj_h, o_sb, norms, g_post, w_out)


def in_proj_bwd(parts, x, dh1, g_pre, w_in, comm=None):
    T = x.shape[0]
    pw = w_in.shape[2]
    n_parts = len(parts)

    def body(*refs):
        part_refs = refs[:n_parts]
        x_ref, dh1_ref, g_ref, w_hbm, dproj_ref, dx_ref, pack_ref, w_vmem = refs[n_parts:]
        _load_once(w_hbm, w_vmem)
        _zero_first(pack_ref)
        for n, ref in enumerate(part_refs):
            dproj_ref[:, HG_WIDTH * n:HG_WIDTH * (n + 1)] = ref[...].astype(BF16)
        du = _dot_nt(dproj_ref[:, :pw], w_vmem[0])
        for q in range(1, N_CHIPS):
            du = du + _dot_nt(dproj_ref[:, pw * q:pw * (q + 1)], w_vmem[q])
        xh, r = _rms(x_ref[...])
        dx, dw = _rms_bwd(du, xh, r, g_ref[...])
        pack_ref[ROW_ATTN_PRE:ROW_ATTN_PRE + 1, :] += _colsum(dw)
        dx_ref[...] = dh1_ref[...] + dx

    return _pallas(
        body, comm=comm, edge=_grid_edge(T // _wide_tile(T)), name="in_proj_bwd", grid=(T // _wide_tile(T),),
        in_specs=[_wide_spec(HG_WIDTH)] * n_parts + [_wide_spec(D_MODEL), _wide_spec(D_MODEL), _full_spec((1, D_MODEL)), ANY_SPEC],
        out_specs=[_wide_spec(n_parts * HG_WIDTH), _wide_spec(D_MODEL), PACK_SPEC],
        out_shape=[jax.ShapeDtypeStruct((T, n_parts * HG_WIDTH), BF16), jax.ShapeDtypeStruct((T, D_MODEL), F32),
                   jax.ShapeDtypeStruct((8, D_MODEL), F32)],
        scratch_shapes=[pltpu.VMEM(w_in.shape, BF16)],
        compiler_params=_params(("arbitrary",)),
    )(*parts, x, dh1, g_pre, w_in)


def weight_grad(a, g, name, *, tm, tn, tk=512, col_pieces=False, comm=None):
    T, M = a.shape
    N = g.shape[1]
    tk = min(tk, T)
    steps = T // tk

    def body(a_ref, g_ref, o_ref):
        @pl.when(pl.program_id(2) == 0)
        def _():
            o_ref[...] = jnp.zeros(o_ref.shape, F32)

        o_ref[...] += _dot_tn(a_ref[...], g_ref[...]).reshape(o_ref.shape)

    if col_pieces:
        out_shape = jax.ShapeDtypeStruct((N // tn, M, tn), F32)
        out_spec = pl.BlockSpec((1, tm, tn), lambda i, j, k: (j, i, 0))
    else:
        out_shape = jax.ShapeDtypeStruct((M, N), F32)
        out_spec = pl.BlockSpec((tm, tn), lambda i, j, k: (i, j))
    grid = (M // tm, N // tn, steps)

    def edge():
        at = [pl.program_id(d) for d in range(3)]
        return ((at[0] == 0) & (at[1] == 0) & (at[2] == 0),
                (at[0] == grid[0] - 1) & (at[1] == grid[1] - 1) & (at[2] == grid[2] - 1))

    return _pallas(
        body, comm=comm, edge=edge, name=name, grid=grid,
        in_specs=[pl.BlockSpec((tk, tm), lambda i, j, k: (k, i)), pl.BlockSpec((tk, tn), lambda i, j, k: (k, j))],
        out_specs=[out_spec], out_shape=[out_shape],
        compiler_params=_params(("arbitrary", "arbitrary", "arbitrary")),
    )(a, g)


def _place():
    x, y, c = lax.axis_index("x"), lax.axis_index("y"), lax.axis_index("c")
    chips = [(1 - x, y), (x, 1 - y), (1 - x, 1 - y)]
    return x, y, c, chips


def _chip_index(cx, cy):
    return 2 * cx + cy


def _own_slot(piece, slots):
    me = _chip_index(lax.axis_index("x"), lax.axis_index("y"))
    landing = lax.empty((slots,) + piece.shape[1:], piece.dtype)
    return lax.dynamic_update_slice(landing, piece, (me,) + (0,) * (piece.ndim - 1))


def _rcopy(src, dst, send_sem, recv_sem, device):
    return pltpu.make_async_remote_copy(src_ref=src, dst_ref=dst, send_sem=send_sem, recv_sem=recv_sem,
                                        device_id=device, device_id_type=MESH)


def gather_weights(shards):
    n = len(shards)

    def body(*refs):
        ins, outs = refs[:n], refs[2 * n:3 * n]
        send_sems, recv_sems = refs[3 * n:]
        x, y, c, chips = _place()
        me = _chip_index(x, y)
        sibling = (x, y, 1 - c)

        def rows(w, core):
            half = ins[w].shape[0] // 2
            return pl.ds(core * half, half)

        sends = []
        for w in range(n):
            for j, chip in enumerate(chips):
                sends.append(_rcopy(ins[w].at[rows(w, c)], outs[w].at[me, rows(w, c)],
                                    send_sems.at[6 * w + j], recv_sems.at[6 * w + j], (*chip, c)))
        for cp in sends:
            cp.start()
        passed = []
        for w in range(n):
            for j, chip in enumerate(chips):
                block = outs[w].at[_chip_index(*chip), rows(w, c)]
                _rcopy(block, block, send_sems.at[6 * w + j], recv_sems.at[6 * w + j], (*chip, c)).wait_recv()
                cp = _rcopy(block, block, send_sems.at[6 * w + 3 + j], recv_sems.at[6 * w + 3 + j], sibling)
                cp.start()
                passed.append(cp)
        for w in range(n):
            for j, chip in enumerate(chips):
                block = outs[w].at[_chip_index(*chip), rows(w, 1 - c)]
                _rcopy(block, block, send_sems.at[6 * w + 3 + j], recv_sems.at[6 * w + 3 + j], sibling).wait_recv()
        for cp in sends + passed:
            cp.wait_send()

    filled = [_own_slot(s[None], N_CHIPS) for s in shards]
    return pl.pallas_call(
        body, name="gather_weights",
        in_specs=[ANY_SPEC] * (2 * n), out_specs=[ANY_SPEC] * n,
        out_shape=[jax.ShapeDtypeStruct(f.shape, f.dtype) for f in filled],
        input_output_aliases={n + w: w for w in range(n)},
        scratch_shapes=[pltpu.SemaphoreType.DMA((6 * n,)), pltpu.SemaphoreType.DMA((6 * n,))],
    )(*shards, *filled)


def _run_comm(comm, name):
    c_in, c_out = len(comm.inputs), len(comm.out_shape)

    def body(*refs):
        parts = refs[:c_in], refs[c_in:c_in + c_out], refs[c_in + c_out:]
        comm.start(*parts)
        comm.finish(*parts)

    return pl.pallas_call(
        body, name=name, in_specs=[ANY_SPEC] * c_in, out_specs=[ANY_SPEC] * c_out, out_shape=comm.out_shape,
        scratch_shapes=comm.scratch, input_output_aliases=comm.aliases)(*comm.inputs)


def _both(first, second):
    n_in, n_out, n_scr = len(first.inputs), len(first.out_shape), len(first.scratch)

    def split(ins, outs, scr):
        return (ins[:n_in], outs[:n_out], scr[:n_scr]), (ins[n_in:], outs[n_out:], scr[n_scr:])

    def start(*refs):
        a, b = split(*refs)
        first.start(*a)
        second.start(*b)

    def finish(*refs):
        a, b = split(*refs)
        first.finish(*a)
        second.finish(*b)

    aliases = dict(first.aliases)
    aliases.update({n_in + i: n_out + o for i, o in second.aliases.items()})
    return Comm(first.inputs + second.inputs, first.out_shape + second.out_shape, aliases,
                first.scratch + second.scratch, start, finish)


def _dma_sems(count):
    return [pltpu.SemaphoreType.DMA((count,)), pltpu.SemaphoreType.DMA((count,))]


def gather_over_ici(shards):
    n = len(shards)

    def copies(ins, outs, sems):
        send_sems, recv_sems = sems
        x, y, c, chips = _place()
        me = _chip_index(x, y)
        pairs = []
        for w in range(n):
            half = shards[w].shape[0] // 2
            rows = pl.ds(c * half, half)
            for j, chip in enumerate(chips):
                k = 3 * w + j
                landed = outs[w].at[_chip_index(*chip), rows]
                pairs.append((_rcopy(ins[w].at[rows], outs[w].at[me, rows], send_sems.at[k], recv_sems.at[k], (*chip, c)),
                              _rcopy(landed, landed, send_sems.at[k], recv_sems.at[k], (*chip, c))))
        return pairs

    def start(*refs):
        for send, _ in copies(*refs):
            send.start()

    def finish(*refs):
        pairs = copies(*refs)
        for _, landed in pairs:
            landed.wait_recv()
        for send, _ in pairs:
            send.wait_send()

    filled = [_own_slot(s[None], N_CHIPS) for s in shards]
    return Comm(list(shards) + filled, [jax.ShapeDtypeStruct(f.shape, f.dtype) for f in filled],
                {n + w: w for w in range(n)}, _dma_sems(3 * n), start, finish)


def gather_over_d2d(landed):
    n = len(landed)

    def copies(ins, outs, sems):
        send_sems, recv_sems = sems
        x, y, c, chips = _place()
        sibling = (x, y, 1 - c)
        pairs = []
        for w in range(n):
            half = landed[w].shape[1] // 2
            for j, chip in enumerate(chips):
                k = 3 * w + j
                mine = outs[w].at[_chip_index(*chip), pl.ds(c * half, half)]
                theirs = outs[w].at[_chip_index(*chip), pl.ds((1 - c) * half, half)]
                pairs.append((_rcopy(mine, mine, send_sems.at[k], recv_sems.at[k], sibling),
                              _rcopy(theirs, theirs, send_sems.at[k], recv_sems.at[k], sibling)))
        return pairs

    def start(*refs):
        for send, _ in copies(*refs):
            send.start()

    def finish(*refs):
        pairs = copies(*refs)
        for _, arrived in pairs:
            arrived.wait_recv()
        for send, _ in pairs:
            send.wait_send()

    return Comm(list(landed), [jax.ShapeDtypeStruct(a.shape, a.dtype) for a in landed], {w: w for w in range(n)},
                _dma_sems(3 * n), start, finish)


def core_halves(grads):
    n = len(grads)

    def copies(ins, outs, sems):
        send_sems, recv_sems = sems
        x, y, c, _ = _place()
        out = []
        for w in range(n):
            half = grads[w].shape[1] // 2
            out.append(_rcopy(ins[w].at[:, pl.ds((1 - c) * half, half), :], outs[w],
                              send_sems.at[w], recv_sems.at[w], (x, y, 1 - c)))
        return out

    def start(*refs):
        for cp in copies(*refs):
            cp.start()

    def finish(*refs):
        for cp in copies(*refs):
            cp.wait()

    return Comm(list(grads), [jax.ShapeDtypeStruct((g.shape[0], g.shape[1] // 2, g.shape[2]), g.dtype) for g in grads],
                {}, _dma_sems(n), start, finish)


def chip_partials(pairs):
    partials = [p for p, _ in pairs]
    filled = [landing for _, landing in pairs]
    n = len(partials)

    def copies(ins, outs, sems):
        send_sems, recv_sems = sems
        x, y, c, chips = _place()
        me = _chip_index(x, y)
        pairs = []
        for w in range(n):
            for j, chip in enumerate(chips):
                k = 3 * w + j
                landed = outs[w].at[_chip_index(*chip)]
                pairs.append((_rcopy(ins[w].at[_chip_index(*chip)], outs[w].at[me], send_sems.at[k], recv_sems.at[k],
                                     (*chip, c)),
                              _rcopy(landed, landed, send_sems.at[k], recv_sems.at[k], (*chip, c))))
        return pairs

    def start(*refs):
        for send, _ in copies(*refs):
            send.start()

    def finish(*refs):
        pairs = copies(*refs)
        for _, landed in pairs:
            landed.wait_recv()
        for send, _ in pairs:
            send.wait_send()

    return Comm(list(partials) + filled, [jax.ShapeDtypeStruct(p.shape, p.dtype) for p in partials],
                {n + w: w for w in range(n)}, _dma_sems(3 * n), start, finish)


def join_core_halves(grads):
    n = len(grads)

    def body(*refs):
        outs = refs[n:2 * n]
        send_sems, recv_sems = refs[2 * n:]
        x, y, c, _ = _place()
        sibling = (x, y, 1 - c)
        copies = []
        for w in range(n):
            half = outs[w].shape[0] // 2
            mine = outs[w].at[pl.ds(c * half, half), :]
            copies.append(_rcopy(mine, mine, send_sems.at[w], recv_sems.at[w], sibling))
        for cp in copies:
            cp.start()
        for w in range(n):
            half = outs[w].shape[0] // 2
            theirs = outs[w].at[pl.ds((1 - c) * half, half), :]
            _rcopy(theirs, theirs, send_sems.at[w], recv_sems.at[w], sibling).wait_recv()
        for cp in copies:
            cp.wait_send()

    return pl.pallas_call(
        body, name="join_core_halves",
        in_specs=[ANY_SPEC] * n, out_specs=[ANY_SPEC] * n,
        out_shape=[jax.ShapeDtypeStruct(g.shape, g.dtype) for g in grads],
        input_output_aliases={w: w for w in range(n)},
        scratch_shapes=[pltpu.SemaphoreType.DMA((n,)), pltpu.SemaphoreType.DMA((n,))],
    )(*grads)


def _elementwise_rows(rows, cap=512):
    for t in range(min(rows, cap), 0, -8):
        if rows % t == 0 and t % 16 == 0:
            return t
    return rows


def add_core_halves(grad, got, core, name):
    _, rows, cols = got.shape
    tr = _elementwise_rows(rows)
    nt = rows // tr

    def body(core_ref, a_ref, b_ref, o_ref, landing_ref):
        o_ref[...] = (a_ref[...] + b_ref[...]).astype(BF16)
        landing_ref[...] = o_ref[...]

    spec = pl.BlockSpec((1, tr, cols), lambda q, i, core_ref: (q, i, 0))
    own = pl.BlockSpec((1, tr, cols), lambda q, i, core_ref: (q, core_ref[0] * nt + i, 0))
    return pl.pallas_call(
        body, name=name,
        grid_spec=pltpu.PrefetchScalarGridSpec(num_scalar_prefetch=1, grid=(N_CHIPS, nt), in_specs=[own, spec],
                                               out_specs=[spec, spec]),
        out_shape=[jax.ShapeDtypeStruct(got.shape, BF16)] * 2,
        compiler_params=_params(("arbitrary", "arbitrary")),
    )(core, grad, got)


def add_chip_partials(parts, core, name):
    _, rows, cols = parts.shape
    tr = _elementwise_rows(rows)
    nt = rows // tr

    def body(core_ref, p_ref, o_ref):
        acc = p_ref[0].astype(F32)
        for q in range(1, N_CHIPS):
            acc = acc + p_ref[q].astype(F32)
        o_ref[...] = acc

    return pl.pallas_call(
        body, name=name,
        grid_spec=pltpu.PrefetchScalarGridSpec(
            num_scalar_prefetch=1, grid=(nt,),
            in_specs=[pl.BlockSpec((N_CHIPS, tr, cols), lambda i, core_ref: (0, i, 0))],
            out_specs=pl.BlockSpec((tr, cols), lambda i, core_ref: (core_ref[0] * nt + i, 0))),
        out_shape=jax.ShapeDtypeStruct((2 * rows, cols), F32),
        compiler_params=_params(("arbitrary",)),
    )(core, parts)


def _adamw_math(w, g, m, v):
    m = ADAM_B1 * m + (1.0 - ADAM_B1) * g
    v = ADAM_B2 * v + (1.0 - ADAM_B2) * (g * g)
    m_hat = m / (1.0 - ADAM_B1 ** ADAM_STEP)
    v_hat = v / (1.0 - ADAM_B2 ** ADAM_STEP)
    delta = -ADAM_LR * (m_hat / (jnp.sqrt(v_hat) + ADAM_EPS) + ADAM_WD * w)
    return delta, m, v


def adamw(w, g, m, v, name):
    rows, cols = w.shape
    tr = _elementwise_rows(rows, 256)

    def body(w_ref, g_ref, m_ref, v_ref, g_out_ref, d_ref, nm_ref, nv_ref):
        d, nm, nv = _adamw_math(w_ref[...], g_ref[...], m_ref[...], v_ref[...])
        g_out_ref[...] = g_ref[...]
        d_ref[...] = d
        nm_ref[...] = nm
        nv_ref[...] = nv

    spec = pl.BlockSpec((tr, cols), lambda i: (i, 0))
    return pl.pallas_call(
        body, name=name, grid=(rows // tr,), in_specs=[spec] * 4, out_specs=[spec] * 4,
        out_shape=[jax.ShapeDtypeStruct((rows, cols), F32)] * 4,
        compiler_params=_params(("arbitrary",)),
    )(w, g, m, v)


def reduce_small(packs, w, m, v):
    n = len(packs)
    n_dev = 8
    flips = [(fx, fy, fc) for fx in (0, 1) for fy in (0, 1) for fc in (0, 1)][1:]

    def body(*refs):
        pack_refs = refs[:n]
        w_ref, m_ref, v_ref, g_out, d_out, m_out, v_out, mine, slots, send_sems, recv_sems = refs[n:]
        x, y, c, _ = _place()
        me = 4 * x + 2 * y + c
        acc = pack_refs[0][...]
        for ref in pack_refs[1:]:
            acc = acc + ref[...]
        mine[...] = acc
        sends = []
        for k, (fx, fy, fc) in enumerate(flips):
            peer = (x ^ fx, y ^ fy, c ^ fc)
            sends.append(_rcopy(mine, slots.at[me], send_sems.at[k], recv_sems.at[me], peer))
        for cp in sends:
            cp.start()
        slots[me] = acc
        for fx, fy, fc in flips:
            src = 4 * (x ^ fx) + 2 * (y ^ fy) + (c ^ fc)
            _rcopy(mine, slots.at[src], send_sems.at[0], recv_sems.at[src], (x, y, c)).wait_recv()
        for cp in sends:
            cp.wait_send()
        total = slots[0]
        for d in range(1, n_dev):
            total = total + slots[d]
        g_out[...] = total
        d, nm, nv = _adamw_math(w_ref[...], total, m_ref[...], v_ref[...])
        d_out[...] = d
        m_out[...] = nm
        v_out[...] = nv

    vm = pl.BlockSpec(memory_space=pltpu.VMEM)
    return pl.pallas_call(
        body, name="reduce_small",
        in_specs=[vm] * (n + 3), out_specs=[vm] * 4,
        out_shape=[jax.ShapeDtypeStruct((8, D_MODEL), F32)] * 4,
        scratch_shapes=[pltpu.VMEM((8, D_MODEL), F32), pltpu.VMEM((n_dev, 8, D_MODEL), F32),
                        pltpu.SemaphoreType.DMA((len(flips),)), pltpu.SemaphoreType.DMA((n_dev,))],
    )(*packs, w, m, v)


def _column_pieces(g):
    return g.reshape(g.shape[0], N_CHIPS, g.shape[1] // N_CHIPS).transpose(1, 0, 2)


def _pack_small(attn_pre, gamma, hg_norm, sb_norm, attn_post, ffn_pre, ffn_post):
    rows = [attn_pre, gamma.reshape(1, D_MODEL), jnp.concatenate([hg_norm, sb_norm], axis=1), attn_post, ffn_pre, ffn_post,
            jnp.zeros((2, D_MODEL), F32)]
    return jnp.concatenate(rows, axis=0)


def _unpack_small(pack):
    return (pack[ROW_ATTN_PRE:ROW_ATTN_PRE + 1], pack[ROW_GAMMA].reshape(2, HG_WIDTH),
            pack[ROW_MIX_NORMS:ROW_MIX_NORMS + 1, :HG_WIDTH], pack[ROW_MIX_NORMS:ROW_MIX_NORMS + 1, HG_WIDTH:],
            pack[ROW_ATTN_POST:ROW_ATTN_POST + 1], pack[ROW_FFN_PRE:ROW_FFN_PRE + 1], pack[ROW_FFN_POST:ROW_FFN_POST + 1])


def kernel(x, p, attn_pre_norm, w_in, hg_lower_gamma, hg_out_norm, sb_out_norm, w_out, attn_post_norm, ffn_pre_norm, w_gate_up, w_down, ffn_post_norm, ple_proj, ple_gate, loss_target, m_attn_pre_norm, m_w_in, m_hg_lower_gamma, m_hg_out_norm, m_sb_out_norm, m_w_out, m_attn_post_norm, m_ffn_pre_norm, m_w_gate_up, m_w_down, m_ffn_post_norm, m_ple_proj, m_ple_gate, v_attn_pre_norm, v_w_in, v_hg_lower_gamma, v_hg_out_norm, v_sb_out_norm, v_w_out, v_attn_post_norm, v_ffn_pre_norm, v_w_gate_up, v_w_down, v_ffn_post_norm, v_ple_proj, v_ple_gate):
    x2 = x[0]
    p2 = p[0, 0]
    target = loss_target[0]
    big = dict(w_in=(w_in, m_w_in, v_w_in), w_out=(w_out, m_w_out, v_w_out), w_gate_up=(w_gate_up, m_w_gate_up, v_w_gate_up),
               w_down=(w_down, m_w_down, v_w_down), ple_proj=(ple_proj, m_ple_proj, v_ple_proj),
               ple_gate=(ple_gate, m_ple_gate, v_ple_gate))
    names = list(big)
    big = {k: tuple(a[0] for a in t) for k, t in big.items()}

    shard16 = {k: big[k][0].astype(BF16) for k in names}
    w_in_full, = gather_weights([shard16["w_in"]])
    mix_norms = jnp.concatenate([hg_out_norm, sb_out_norm], axis=1)
    small_ones = ["w_out", "ple_proj", "ple_gate"]

    proj_h, sqkv, u1, *landed_small = in_proj_fwd(
        x2, attn_pre_norm, w_in_full, comm=gather_over_ici([shard16[k] for k in small_ones]))
    o_sb, sb_totals, sb_first, landed_gu = sb_fwd(sqkv, comm=gather_over_ici([shard16["w_gate_up"]]))
    o_hg, states, landed_down, *full_small = hgrn2_fwd(
        proj_h, hg_lower_gamma, comm=_both(gather_over_ici([shard16["w_down"]]), gather_over_d2d(landed_small)))
    full = dict(zip(small_ones, full_small), w_in=w_in_full)
    w_out_full = full["w_out"].reshape(D_MODEL, D_MODEL)
    w_pg_full = full["ple_gate"].reshape(D_MODEL, D_MODEL)
    cat, mix, h1, full["w_gate_up"], full["w_down"] = mix_out_fwd(
        o_hg, proj_h, o_sb, x2, mix_norms, attn_post_norm, w_out_full, comm=gather_over_d2d([landed_gu, landed_down]))
    w_down_full = full["w_down"].reshape(D_FF, D_MODEL)
    u2, gu, act, y, h2 = ffn_fwd(h1, ffn_pre_norm, ffn_post_norm, full["w_gate_up"], w_down_full)

    core = lax.axis_index("c").astype(jnp.int32).reshape(1)
    de, ds, dh2, h2b, pb, pack_loss = ple_loss(h2, p2, target, full["ple_proj"], w_pg_full)
    dy, dgu, dh1, pack_ffn = ffn_bwd(dh2, y, h1, gu, ffn_pre_norm, ffn_post_norm, full["w_gate_up"], w_down_full)
    local = {}
    local["w_gate_up"], = weight_grad(u2, dgu, "grad_w_gate_up", tm=D_MODEL, tn=full["w_gate_up"].shape[2], tk=2048,
                                      col_pieces=True)
    grad_down, got_gu = weight_grad(act, dy, "grad_w_down", tm=D_FF // 2, tn=D_MODEL, tk=2048,
                                    comm=core_halves([local["w_gate_up"]]))
    local["w_down"] = grad_down.reshape(full["w_down"].shape)
    local["ple_proj"] = _column_pieces(weight_grad(pb, de, "grad_ple_proj", tm=pb.shape[1], tn=D_MODEL, tk=1024)[0])
    grad_pg, got_down = weight_grad(h2b, ds, "grad_ple_gate", tm=D_MODEL, tn=D_MODEL, tk=2048,
                                    comm=core_halves([local["w_down"]]))
    local["ple_gate"] = grad_pg.reshape(full["ple_gate"].shape)
    early = list(local)
    dmix, do_hg, dhg, do_sb, pack_mix, *got_ple = mix_out_bwd(
        dh1, mix, o_hg, proj_h, o_sb, mix_norms, attn_post_norm, w_out_full,
        comm=core_halves([local["ple_proj"], local["ple_gate"]]))
    got = [got_gu, got_down] + got_ple
    partial = [add_core_halves(local[k], g, core, "add_core_halves_" + k) for k, g in zip(early, got)]
    local["w_out"] = weight_grad(cat, dmix, "grad_w_out", tm=D_MODEL, tn=D_MODEL, tk=2048)[0].reshape(full["w_out"].shape)
    dsq, dsk, dsv = sb_bwd(sqkv, do_sb, sb_totals, sb_first)
    dhq, dhf, dhi, pack_hg, *by_source, got_out = hgrn2_bwd(
        proj_h, hg_lower_gamma, states, do_hg, comm=_both(chip_partials(partial[:2]), core_halves([local["w_out"]])))
    early.append("w_out")
    partial.append(add_core_halves(local["w_out"], got_out, core, "add_core_halves_w_out"))
    dproj, grad_x, pack_in = in_proj_bwd([dhq, dhf, dhi, dhg, dsq, dsk, dsv], x2, dh1, attn_pre_norm, full["w_in"])

    late = ["w_in"]
    local["w_in"], *more = weight_grad(u1, dproj, "grad_w_in", tm=D_MODEL, tn=full["w_in"].shape[2], tk=2048,
                                       col_pieces=True, comm=chip_partials(partial[2:]))
    halves = {k: add_chip_partials(s, core, "add_chip_partials_" + k) for k, s in zip(early, by_source + more)}
    got = _run_comm(core_halves([local[k] for k in late]), "exchange_core_halves")
    partial = [add_core_halves(local[k], g, core, "add_core_halves_" + k) for k, g in zip(late, got)]
    by_source = _run_comm(chip_partials(partial), "exchange_chip_partials")
    halves.update({k: add_chip_partials(s, core, "add_chip_partials_" + k) for k, s in zip(late, by_source)})
    grads = dict(zip(names, join_core_halves([halves[k] for k in names])))

    upd = {k: adamw(big[k][0], grads[k], big[k][1], big[k][2], "adamw_" + k) for k in names}

    small = reduce_small(
        [pack_loss, pack_ffn, pack_mix, pack_hg, pack_in],
        _pack_small(attn_pre_norm, hg_lower_gamma, hg_out_norm, sb_out_norm, attn_post_norm, ffn_pre_norm, ffn_post_norm),
        _pack_small(m_attn_pre_norm, m_hg_lower_gamma, m_hg_out_norm, m_sb_out_norm, m_attn_post_norm, m_ffn_pre_norm, m_ffn_post_norm),
        _pack_small(v_attn_pre_norm, v_hg_lower_gamma, v_hg_out_norm, v_sb_out_norm, v_attn_post_norm, v_ffn_pre_norm, v_ffn_post_norm),
    )
    loss = small[0][ROW_LOSS, 0]
    s_grad, s_delta, s_m, s_v = (_unpack_small(t) for t in small)

    def ordered(small_vals, big_vals):
        a_pre, gam, hg_n, sb_n, a_post, f_pre, f_post = small_vals
        b = {k: big_vals[k][None] for k in names}
        return (a_pre, b["w_in"], gam, hg_n, sb_n, b["w_out"], a_post, f_pre, b["w_gate_up"], b["w_down"], f_post,
                b["ple_proj"], b["ple_gate"])

    return (loss, grad_x[None],
            *ordered(s_grad, {k: upd[k][0] for k in names}),
            *ordered(s_delta, {k: upd[k][1] for k in names}),
            *ordered(s_m, {k: upd[k][2] for k in names}),
            *ordered(s_v, {k: upd[k][3] for k in names}))
```

```python
import functools
from typing import Callable, NamedTuple

import numpy as np
import jax
import jax.numpy as jnp
from jax import lax
from jax.experimental import pallas as pl
from jax.experimental.pallas import tpu as pltpu

F32 = jnp.float32
BF16 = jnp.bfloat16
MESH = pl.DeviceIdType.MESH

RMS_EPS = 1e-6
D_MODEL = 1024
HG_WIDTH = 512
HG_HEADS = 4
HG_DK = 128
HG_CHUNK = 64
HG_LEVELS = (32, 16, 8, 4, 2, 1)
HG_CHUNKS_PER_STEP = 8
HG_CHUNKS_PER_STEP_BWD = 4
SB_WIDTH = 512
SB_BLOCK = 128
SB_DH = 64
SB_SCALE = SB_DH ** -0.5
SB_UNDERFLOW_LOG = -87.5
SB_UNROLL = 2
SB_GROUP = 4
D_FF = 2816
N_CHIPS = 4
ROW_TILE = 256
WIDE_ROW_TILE = 512
V7X_VMEM_LIMIT = 56 * 1024 * 1024

ADAM_LR = 0.001
ADAM_B1 = 0.9
ADAM_B2 = 0.999
ADAM_EPS = 1e-08
ADAM_WD = 0.01
ADAM_STEP = 10

ROW_ATTN_PRE, ROW_GAMMA, ROW_MIX_NORMS, ROW_ATTN_POST, ROW_FFN_PRE, ROW_FFN_POST, ROW_LOSS = range(7)


def _params(sem=None, vmem=V7X_VMEM_LIMIT):
    return pltpu.CompilerParams(dimension_semantics=sem, vmem_limit_bytes=vmem)


def _dot(a, b):
    return jnp.dot(a.astype(BF16), b.astype(BF16), preferred_element_type=F32)


def _dot_nt(a, b):
    return lax.dot_general(a.astype(BF16), b.astype(BF16), (((1,), (1,)), ((), ())), preferred_element_type=F32)


def _dot_tn(a, b):
    return lax.dot_general(a.astype(BF16), b.astype(BF16), (((0,), (0,)), ((), ())), preferred_element_type=F32)


def _split(x):
    hi = x.astype(BF16)
    lo = (x - hi.astype(F32)).astype(BF16)
    return hi, lo


def _sum01_left(m01, x):
    hi, lo = _split(x)
    return jnp.dot(m01, hi, preferred_element_type=F32) + jnp.dot(m01, lo, preferred_element_type=F32)


def _sum01_right(x, m01_twice):
    hi, lo = _split(x)
    return jnp.dot(jnp.concatenate([hi, lo], axis=1), m01_twice, preferred_element_type=F32)


def _rms(x):
    r = lax.rsqrt(jnp.mean(x * x, axis=-1, keepdims=True) + RMS_EPS)
    return x * r, r


def _rms_bwd(dy, xhat, r, w):
    dxh = dy * w
    dx = r * (dxh - xhat * jnp.mean(dxh * xhat, axis=-1, keepdims=True))
    return dx, dy * xhat


def _sigmoid(x):
    return 1.0 / (1.0 + jnp.exp(-x))


def _neg_softplus(z):
    return -(jnp.maximum(z, 0.0) + jnp.log(1.0 + jnp.exp(-jnp.abs(z))))


def _colsum(x):
    return jnp.sum(x, axis=0, keepdims=True)


def _load_once(src_hbm, dst_vmem):
    @pl.when(pl.program_id(0) == 0)
    def _():
        pltpu.sync_copy(src_hbm, dst_vmem)


def _zero_first(ref):
    @pl.when(pl.program_id(0) == 0)
    def _():
        ref[...] = jnp.zeros(ref.shape, ref.dtype)


def _row_spec(width, col=0, rows=ROW_TILE):
    return pl.BlockSpec((rows, width), lambda i, col=col: (i, col))


def _wide_spec(width, col=0):
    return _row_spec(width, col, WIDE_ROW_TILE)


def _wide_tile(T):
    assert T % WIDE_ROW_TILE == 0
    return WIDE_ROW_TILE


def _full_spec(shape):
    return pl.BlockSpec(shape, lambda *_: (0,) * len(shape))


ANY_SPEC = pl.BlockSpec(memory_space=pl.ANY)
PACK_SPEC = _full_spec((8, D_MODEL))


class Comm(NamedTuple):
    inputs: list
    out_shape: list
    aliases: dict
    scratch: list
    start: Callable
    finish: Callable


def _pallas(body, *, comm=None, edge=None, in_specs, out_specs, out_shape, scratch_shapes=(), **kw):
    if comm is None:
        return pl.pallas_call(body, in_specs=in_specs, out_specs=out_specs, out_shape=out_shape,
                              scratch_shapes=scratch_shapes, **kw)
    n_in, n_out, n_scr = len(in_specs), len(out_specs), len(scratch_shapes)
    c_in, c_out = len(comm.inputs), len(comm.out_shape)

    def both(*refs):
        ins, c_ins = refs[:n_in], refs[n_in:n_in + c_in]
        outs = refs[n_in + c_in:n_in + c_in + n_out]
        c_outs = refs[n_in + c_in + n_out:n_in + c_in + n_out + c_out]
        rest = refs[n_in + c_in + n_out + c_out:]
        scr, c_scr = rest[:n_scr], rest[n_scr:]
        first, last = edge()

        @pl.when(first)
        def _():
            comm.start(c_ins, c_outs, c_scr)

        body(*ins, *outs, *scr)

        @pl.when(last)
        def _():
            comm.finish(c_ins, c_outs, c_scr)

    call = pl.pallas_call(
        both, in_specs=list(in_specs) + [ANY_SPEC] * c_in, out_specs=list(out_specs) + [ANY_SPEC] * c_out,
        out_shape=list(out_shape) + list(comm.out_shape), scratch_shapes=list(scratch_shapes) + list(comm.scratch),
        input_output_aliases={n_in + a: n_out + b for a, b in comm.aliases.items()}, **kw)
    return lambda *args: call(*args, *comm.inputs)


def _grid_edge(steps):
    return lambda: (pl.program_id(0) == 0, pl.program_id(0) == steps - 1)


def in_proj_fwd(x, g_pre, w_in, comm=None):
    T = x.shape[0]
    pw = w_in.shape[2]

    def body(x_ref, g_ref, w_hbm, ph_ref, sqkv_ref, u_ref, w_vmem, proj_s):
        _load_once(w_hbm, w_vmem)
        xh, _ = _rms(x_ref[...])
        u = (xh * g_ref[...]).astype(BF16)
        u_ref[...] = u
        for q in range(N_CHIPS):
            proj_s[:, pw * q:pw * (q + 1)] = jnp.dot(u, w_vmem[q], preferred_element_type=F32)
        ph_ref[...] = proj_s[:, :4 * HG_WIDTH]
        sqkv_ref[:, :SB_WIDTH] = (proj_s[:, 4 * HG_WIDTH:4 * HG_WIDTH + SB_WIDTH] * SB_SCALE).astype(BF16)
        sqkv_ref[:, SB_WIDTH:] = proj_s[:, 4 * HG_WIDTH + SB_WIDTH:].astype(BF16)

    return _pallas(
        body, comm=comm, edge=_grid_edge(T // _wide_tile(T)), name="in_proj_fwd", grid=(T // _wide_tile(T),),
        in_specs=[_wide_spec(D_MODEL), _full_spec((1, D_MODEL)), ANY_SPEC],
        out_specs=[_wide_spec(4 * HG_WIDTH), _wide_spec(3 * SB_WIDTH), _wide_spec(D_MODEL)],
        out_shape=[jax.ShapeDtypeStruct((T, 4 * HG_WIDTH), F32), jax.ShapeDtypeStruct((T, 3 * SB_WIDTH), BF16),
                   jax.ShapeDtypeStruct((T, D_MODEL), BF16)],
        scratch_shapes=[pltpu.VMEM(w_in.shape, BF16), pltpu.VMEM((_wide_tile(T), N_CHIPS * pw), F32)],
        compiler_params=_params(("arbitrary",)),
    )(x, g_pre, w_in)


def _hg_sum_matrix():
    C = HG_CHUNK
    t = np.arange(C)[:, None]
    j = np.arange(C)[None, :]
    mats = [j <= t, j > t]
    for h in HG_LEVELS:
        start = (t // (2 * h)) * (2 * h)
        upper = (t & h) != 0
        mats.append(np.where(upper, (j >= start + h) & (j <= t), (j > t) & (j <= start + h - 1)))
    return np.concatenate(mats, 0).astype(np.float32)


def _hg_level_masks():
    C = HG_CHUNK
    t = lax.broadcasted_iota(jnp.int32, (C, C), 0)
    s = lax.broadcasted_iota(jnp.int32, (C, C), 1)
    x = t ^ s
    masks = [t == s]
    for h in HG_LEVELS:
        masks.append((x >= h) & (x < 2 * h) & (t > s))
    return masks


def _hg_gates(hq, hf, gamma):
    lb = 1.0 / (1.0 + jnp.exp(gamma[1:2, :] - gamma[0:1, :]))
    sq = _sigmoid(hq)
    q = hq * sq
    sig = _sigmoid(hf)
    nsig = _sigmoid(-hf)
    f = lb + (1.0 - lb) * sig
    k = (1.0 - lb) * nsig
    g = jnp.log(f)
    return q, k, g, dict(lb=lb, sq=sq, sig=sig, nsig=nsig, f=f)


def _hg_head_decays(A, h):
    C, K = HG_CHUNK, HG_DK
    sl = slice(K * h, K * (h + 1))
    blocks = [A[C * r:C * (r + 1), sl] for r in range(2 + len(HG_LEVELS))]
    return blocks[0], blocks[1], [None] + blocks[2:]


def _hg_products(q, k, levels):
    return [_dot_nt(q, k)] + [_dot_nt(q * a, k * a) for a in levels[1:]]


def _hg_select(prods, masks):
    sc = jnp.where(masks[0], prods[0], 0.0)
    for p, m in zip(prods[1:], masks[1:]):
        sc = jnp.where(m, p, sc)
    return sc


def hgrn2_fwd(proj_h, gamma, comm=None):
    T = proj_h.shape[0]
    C, K, H, S = HG_CHUNK, HG_DK, HG_HEADS, HG_CHUNKS_PER_STEP
    n_steps = T // (S * C)
    msum = jnp.asarray(_hg_sum_matrix(), BF16)

    def body(hq_ref, hf_ref, hi_ref, gam_ref, msum_ref, o_ref, st_ref, st_s):
        _zero_first(st_s)
        q, k, g, _ = _hg_gates(hq_ref[...], hf_ref[...], gam_ref[...])
        v = hi_ref[...]
        masks = _hg_level_masks()
        parts = []
        for s in range(S):
            rows = slice(C * s, C * (s + 1))
            A = jnp.exp(_sum01_left(msum_ref[...], g[rows]))
            for h in range(H):
                sl = slice(K * h, K * (h + 1))
                ab, ar, levels = _hg_head_decays(A, h)
                parts.append(dict(s=s, h=h, rows=rows, sl=sl, ab=ab, ar=ar, levels=levels,
                                  q=q[rows, sl], k=k[rows, sl], v=v[rows, sl]))
        for pt in parts:
            pt["prods"] = _hg_products(pt["q"], pt["k"], pt["levels"])
            pt["grown"] = _dot_tn(pt["v"], pt["k"] * pt["ar"])
        for pt in parts:
            pt["sc"] = _hg_select(pt["prods"], masks)
        state = [st_s[h] for h in range(H)]
        for pt in parts:
            h, ab = pt["h"], pt["ab"]
            o_ref[pt["rows"], pt["sl"]] = _dot_nt(pt["q"] * ab, state[h]) + _dot(pt["sc"], pt["v"])
            state[h] = state[h] * ab[C - 1:C, :] + pt["grown"]
            st_ref[pt["s"], h] = state[h]
        for h in range(H):
            st_s[h] = state[h]

    blk = lambda col: pl.BlockSpec((S * C, HG_WIDTH), lambda c, col=col: (c, col))
    return _pallas(
        body, comm=comm, edge=_grid_edge(n_steps), name="hgrn2_fwd", grid=(n_steps,),
        in_specs=[blk(0), blk(1), blk(2), _full_spec((2, HG_WIDTH)), _full_spec(msum.shape)],
        out_specs=[blk(0), pl.BlockSpec((S, H, K, K), lambda c: (c, 0, 0, 0))],
        out_shape=[jax.ShapeDtypeStruct((T, HG_WIDTH), F32), jax.ShapeDtypeStruct((S * n_steps, H, K, K), F32)],
        scratch_shapes=[pltpu.VMEM((H, K, K), F32)],
        compiler_params=_params(("arbitrary",)),
    )(proj_h, proj_h, proj_h, gamma, msum)


def hgrn2_bwd(proj_h, gamma, states, do, comm=None):
    T = proj_h.shape[0]
    C, K, H, S = HG_CHUNK, HG_DK, HG_HEADS, HG_CHUNKS_PER_STEP_BWD
    n_steps = T // (S * C)
    n_sums = 2 + len(HG_LEVELS)
    msum = jnp.asarray(_hg_sum_matrix(), BF16)
    msum_t = jnp.asarray(_hg_sum_matrix().T, BF16)

    def body(hq_ref, hf_ref, hi_ref, do_ref, gam_ref, msum_ref, msum_t_ref, st_prev_ref, st_ref,
             dhq_ref, dhf_ref, dhi_ref, pack_ref, dst_s, dlb_s, dq_s, dk_s, de_s):
        step = pl.program_id(0)
        _zero_first(dst_s)
        _zero_first(dlb_s)
        _zero_first(pack_ref)
        hq = hq_ref[...]
        q, k, g, aux = _hg_gates(hq, hf_ref[...], gam_ref[...])
        v = hi_ref[...]
        do_all = do_ref[...]
        masks = _hg_level_masks()
        is_last_row = lax.broadcasted_iota(jnp.int32, (C, K), 0) == C - 1
        has_prev = (step < n_steps - 1).astype(F32)
        parts = []
        for s in reversed(range(S)):
            rows = slice(C * s, C * (s + 1))
            A = jnp.exp(_sum01_left(msum_ref[...], g[rows]))
            for h in range(H):
                sl = slice(K * h, K * (h + 1))
                ab, ar, levels = _hg_head_decays(A, h)
                st_in = st_prev_ref[0, h] * has_prev if s == 0 else st_ref[s - 1, h]
                parts.append(dict(s=s, h=h, rows=rows, sl=sl, ab=ab, ar=ar, levels=levels, st_in=st_in,
                                  q=q[rows, sl], k=k[rows, sl], v=v[rows, sl], do=do_all[rows, sl]))
        for pt in parts:
            pt["prods"] = _hg_products(pt["q"], pt["k"], pt["levels"])
            pt["da"] = _dot_nt(pt["do"], pt["v"])
            pt["t1"] = pt["ab"] * _dot(pt["do"], pt["st_in"])
            pt["dst_add"] = _dot_tn(pt["do"], pt["q"] * pt["ab"])
        dstate = [dst_s[h] for h in range(H)]
        for pt in parts:
            h = pt["h"]
            pt["dst_out"] = dstate[h]
            pt["t2"] = pt["ar"] * _dot(pt["v"], dstate[h])
            pt["dv_state"] = _dot_nt(pt["k"] * pt["ar"], dstate[h])
            dstate[h] = dstate[h] * pt["ab"][C - 1:C, :] + pt["dst_add"]
        for h in range(H):
            dst_s[h] = dstate[h]
        for pt in parts:
            pt["sc"] = _hg_select(pt["prods"], masks)
            pt["dam"] = [jnp.where(m, pt["da"], 0.0) for m in masks]
        for pt in parts:
            qh, kh = pt["q"], pt["k"]
            pt["dq_parts"] = [_dot(pt["dam"][0], kh)] + [
                a * _dot(dam, kh * a) for a, dam in zip(pt["levels"][1:], pt["dam"][1:])]
            pt["dk_parts"] = [_dot_tn(pt["dam"][0], qh)] + [
                a * _dot_tn(dam, qh * a) for a, dam in zip(pt["levels"][1:], pt["dam"][1:])]
            pt["dv_intra"] = _dot_tn(pt["sc"], pt["do"])
        for pt in parts:
            s, rows, sl, qh, kh, ab = pt["s"], pt["rows"], pt["sl"], pt["q"], pt["k"], pt["ab"]
            decayed = _colsum(pt["st_in"] * pt["dst_out"]) * ab[C - 1:C, :]
            de_s[s, 0:C, sl] = qh * pt["t1"] + jnp.where(is_last_row, decayed, 0.0)
            de_s[s, C:2 * C, sl] = kh * pt["t2"]
            dq = pt["t1"] + pt["dq_parts"][0]
            dk = pt["t2"] + pt["dk_parts"][0]
            for r, (t1, t2) in enumerate(zip(pt["dq_parts"][1:], pt["dk_parts"][1:])):
                dq = dq + t1
                dk = dk + t2
                de_s[s, C * (r + 2):C * (r + 3), sl] = qh * t1 + kh * t2
            dhi_ref[rows, sl] = pt["dv_intra"] + pt["dv_state"]
            dq_s[rows, sl] = dq
            dk_s[rows, sl] = dk
        dg = jnp.concatenate([_sum01_left(msum_t_ref[...], de_s[s]) for s in range(S)], axis=0)
        dk = dk_s[...]
        sq, lb = aux["sq"], aux["lb"]
        dhq_ref[...] = dq_s[...] * (sq * (1.0 + hq * (1.0 - sq)))
        common = dg / aux["f"] - dk
        dhf_ref[...] = (1.0 - lb) * aux["sig"] * aux["nsig"] * common
        dlb_s[...] += _colsum(aux["nsig"] * common)

        @pl.when(step == n_steps - 1)
        def _():
            dgam = lb * (1.0 - lb) * dlb_s[...]
            pack_ref[ROW_GAMMA:ROW_GAMMA + 1, :HG_WIDTH] = dgam
            pack_ref[ROW_GAMMA:ROW_GAMMA + 1, HG_WIDTH:] = -dgam

    last = n_steps - 1
    blk = lambda col: pl.BlockSpec((S * C, HG_WIDTH), lambda c, col=col: (last - c, col))
    return _pallas(
        body, comm=comm, edge=_grid_edge(n_steps), name="hgrn2_bwd", grid=(n_steps,),
        in_specs=[blk(0), blk(1), blk(2), blk(0), _full_spec((2, HG_WIDTH)), _full_spec(msum.shape),
                  _full_spec(msum_t.shape),
                  pl.BlockSpec((1, H, K, K), lambda c: (jnp.maximum(S * (last - c) - 1, 0), 0, 0, 0)),
                  pl.BlockSpec((S, H, K, K), lambda c: (last - c, 0, 0, 0))],
        out_specs=[blk(0), blk(0), blk(0), PACK_SPEC],
        out_shape=[jax.ShapeDtypeStruct((T, HG_WIDTH), F32)] * 3 + [jax.ShapeDtypeStruct((8, D_MODEL), F32)],
        scratch_shapes=[pltpu.VMEM((H, K, K), F32), pltpu.VMEM((1, HG_WIDTH), F32), pltpu.VMEM((S * C, HG_WIDTH), F32),
                        pltpu.VMEM((S * C, HG_WIDTH), F32), pltpu.VMEM((S, n_sums * C, HG_WIDTH), F32)],
        compiler_params=_params(("arbitrary",)),
    )(proj_h, proj_h, proj_h, do, gamma, msum, msum_t, states, states)


def _sb_sum_matrix(inclusive):
    B = SB_BLOCK
    j = np.arange(B)[:, None]
    s = np.arange(B)[None, :]
    tri = (j >= s) if inclusive else (j > s)
    once = np.concatenate([tri, np.ones((B, B), bool)], 1).astype(np.float32)
    return np.concatenate([once, once], 0)


def _sb_prefix_matrix(inclusive):
    B = SB_BLOCK
    j = np.arange(B)[:, None]
    s = np.arange(B)[None, :]
    tri = (j <= s) if inclusive else (j < s)
    once = np.concatenate([tri, np.ones((B, B), bool)], 1).astype(np.float32)
    return np.concatenate([once, once], 0)


def _sb_iotas():
    shape = (SB_BLOCK, SB_BLOCK)
    return lax.broadcasted_iota(jnp.int32, shape, 0), lax.broadcasted_iota(jnp.int32, shape, 1)


def _sb_heads(q, first):
    heads = []
    for g in range(SB_GROUP):
        qg = q[:, SB_BLOCK * g:SB_BLOCK * (g + 1)]
        zero = jnp.zeros_like(qg)
        heads += [(g, jnp.where(first, qg, zero)), (g, jnp.where(first, zero, qg))]
    return heads


def _lanes(x, g):
    return x[:, SB_BLOCK * g:SB_BLOCK * (g + 1)]


def sb_fwd(sqkv, comm=None):
    T = sqkv.shape[0]
    B = SB_BLOCK
    W = SB_GROUP * B
    groups = SB_WIDTH // W
    usum = jnp.asarray(_sb_sum_matrix(False), BF16)

    def body(q_ref, k_ref, v_ref, u_ref, o_ref, tl_ref, first_ref):
        p, i = pl.program_id(0), pl.program_id(1)
        row, lane = _sb_iotas()
        first = lane < SB_DH
        heads = _sb_heads(q_ref[...], first)
        u = u_ref[...]

        def more(loop):
            done, reachable, _ = loop
            return (done <= i) & (reachable > 0)

        def walk(count, loop):
            done, _, state = loop
            blocks = []
            for sub in range(count):
                j = i - done - sub
                off = pl.multiple_of(jnp.maximum(j, 0) * B, B)
                valid = ((lane + j * B) < (row + i * B)) & (j >= 0)
                blocks.append((k_ref[pl.ds(off, B), :], v_ref[pl.ds(off, B), :], valid))
            z = [[_dot_nt(qh, _lanes(kj, g)) for g, qh in heads] for kj, _, _ in blocks]
            lnb = [[jnp.where(valid, _neg_softplus(zz), 0.0) for zz in zs] for zs, (_, _, valid) in zip(z, blocks)]
            sums = [[_sum01_right(x, u) for x in xs] for xs in lnb]
            out = []
            for h, (carry, acc) in enumerate(state):
                for sub, (_, vj, valid) in enumerate(blocks):
                    expo = z[sub][h] + lnb[sub][h] + carry + sums[sub][h][:, :B]
                    acc = acc + _dot(jnp.where(valid, jnp.exp(expo), 0.0), _lanes(vj, heads[h][0]))
                    carry = carry + sums[sub][h][:, B:]
                out.append((carry, acc))
            state = tuple(out)
            worst = state[0][0]
            for carry, _ in state[1:]:
                worst = jnp.maximum(worst, carry)
            reachable = (jnp.max(worst) > SB_UNDERFLOW_LOG).astype(jnp.int32)
            return done + count, reachable, state

        zero = jnp.zeros((B, B), F32)
        loop = walk(SB_UNROLL, (jnp.int32(0), jnp.int32(1), tuple((zero, zero) for _ in heads)))
        done, _, state = lax.while_loop(more, functools.partial(walk, 1), loop)
        for g in range(SB_GROUP):
            (tot0, acc0), (tot1, acc1) = state[2 * g], state[2 * g + 1]
            o_ref[:, B * g:B * (g + 1)] = jnp.where(first, acc0, acc1)
            tl_ref[:, B * g:B * (g + 1)] = jnp.where(first, tot0, tot1)
        first_ref[p, i] = jnp.maximum(i + 1 - done, 0)

    def edge():
        p, i = pl.program_id(0), pl.program_id(1)
        return (p == 0) & (i == 0), (p == groups - 1) & (i == T // B - 1)

    return _pallas(
        body, comm=comm, edge=edge, name="sb_fwd", grid=(groups, T // B),
        in_specs=[pl.BlockSpec((B, W), lambda p, i: (i, p)),
                  pl.BlockSpec((T, W), lambda p, i: (0, groups + p)),
                  pl.BlockSpec((T, W), lambda p, i: (0, 2 * groups + p)),
                  pl.BlockSpec(usum.shape, lambda p, i: (0, 0))],
        out_specs=[pl.BlockSpec((B, W), lambda p, i: (i, p))] * 2 + [pl.BlockSpec(memory_space=pltpu.SMEM)],
        out_shape=[jax.ShapeDtypeStruct((T, SB_WIDTH), F32)] * 2 + [jax.ShapeDtypeStruct((groups, T // B), jnp.int32)],
        compiler_params=_params(("arbitrary", "arbitrary")),
    )(sqkv, sqkv, sqkv, usum)


def sb_bwd(sqkv, do, tl, first_block):
    T = sqkv.shape[0]
    B = SB_BLOCK
    W = SB_GROUP * B
    groups = SB_WIDTH // W
    upre = jnp.asarray(_sb_prefix_matrix(True), BF16)
    uexc = jnp.asarray(_sb_prefix_matrix(False), BF16)

    def body(q_ref, k_ref, v_ref, do_ref, tl_ref, up_ref, ue_ref, first_ref, dq_ref, dk_ref, dv_ref):
        p, i = pl.program_id(0), pl.program_id(1)

        @pl.when(i == 0)
        def _():
            dk_ref[...] = jnp.zeros(dk_ref.shape, F32)
            dv_ref[...] = jnp.zeros(dv_ref.shape, F32)

        row, lane = _sb_iotas()
        first = lane < SB_DH
        do = do_ref[...]
        tl_all = tl_ref[...]
        heads = []
        for (g, qh), at in zip(_sb_heads(q_ref[...], first), (0, B - 1) * SB_GROUP):
            dog = _lanes(do, g)
            keep = first if at == 0 else jnp.logical_not(first)
            heads.append((g, qh, jnp.where(keep, dog, jnp.zeros_like(dog)).astype(BF16),
                          _lanes(tl_all, g)[:, at:at + 1]))
        up = up_ref[...]
        ue = ue_ref[...]
        start = first_ref[p, i]

        def walk(count, j0, state):
            blocks = []
            for sub in range(count):
                j = j0 + sub
                off = pl.multiple_of(jnp.minimum(j, i) * B, B)
                valid = (lane + j * B) < (row + i * B)
                blocks.append((off, k_ref[pl.ds(off, B), :], v_ref[pl.ds(off, B), :], valid))
            combos = [(s, h) for s in range(count) for h in range(len(heads))]
            z = {(s, h): _dot_nt(heads[h][1], _lanes(blocks[s][1], heads[h][0])) for s, h in combos}
            da = {(s, h): _dot_nt(heads[h][2], _lanes(blocks[s][2], heads[h][0])) for s, h in combos}
            lnb = {c: jnp.where(blocks[c[0]][3], _neg_softplus(z[c]), 0.0) for c in combos}
            lb = {c: z[c] + lnb[c] for c in combos}
            sums = {c: _sum01_right(lnb[c], up) for c in combos}
            a, w = {}, {}
            seen = [st[0] for st in state]
            for s, h in combos:
                expo = lb[s, h] + (heads[h][3] - seen[h] - sums[s, h][:, :B])
                a[s, h] = jnp.where(blocks[s][3], jnp.exp(expo), 0.0)
                w[s, h] = a[s, h] * da[s, h]
                seen[h] = seen[h] + sums[s, h][:, B:]
            wsums = {c: _sum01_right(w[c], ue) for c in combos}
            dz = {}
            seen_w = [st[1] for st in state]
            for s, h in combos:
                beta = jnp.exp(lb[s, h])
                before = seen_w[h] + wsums[s, h][:, :B]
                dz[s, h] = jnp.where(blocks[s][3], w[s, h] * (1.0 - beta) - before * beta, 0.0)
                seen_w[h] = seen_w[h] + wsums[s, h][:, B:]
            dq = [st[2] for st in state]
            for s, h in combos:
                dq[h] = dq[h] + _dot(dz[s, h], _lanes(blocks[s][1], heads[h][0]))
            for s in range(count):
                off = blocks[s][0]
                for g in range(SB_GROUP):
                    h0, h1 = 2 * g, 2 * g + 1
                    dk_ref[pl.ds(off, B), B * g:B * (g + 1)] += (_dot_tn(dz[s, h0], heads[h0][1])
                                                                 + _dot_tn(dz[s, h1], heads[h1][1]))
                    dv_ref[pl.ds(off, B), B * g:B * (g + 1)] += (_dot_tn(a[s, h0], heads[h0][2])
                                                                 + _dot_tn(a[s, h1], heads[h1][2]))
            return tuple(zip(seen, seen_w, dq))

        zero = jnp.zeros((B, B), F32)
        state = walk(SB_UNROLL, start, tuple((zero, zero, zero) for _ in heads))
        state = lax.fori_loop(start + SB_UNROLL, i + 1, functools.partial(walk, 1), state)
        for g in range(SB_GROUP):
            dq_ref[:, B * g:B * (g + 1)] = jnp.where(first, state[2 * g][2], state[2 * g + 1][2]) * SB_SCALE

    qblk = pl.BlockSpec((B, W), lambda p, i: (i, p))
    full = pl.BlockSpec((T, W), lambda p, i: (0, p))
    return pl.pallas_call(
        body, name="sb_bwd", grid=(groups, T // B),
        in_specs=[qblk, pl.BlockSpec((T, W), lambda p, i: (0, groups + p)),
                  pl.BlockSpec((T, W), lambda p, i: (0, 2 * groups + p)), qblk, qblk,
                  pl.BlockSpec(upre.shape, lambda p, i: (0, 0)), pl.BlockSpec(uexc.shape, lambda p, i: (0, 0)),
                  pl.BlockSpec(memory_space=pltpu.SMEM)],
        out_specs=[qblk, full, full],
        out_shape=[jax.ShapeDtypeStruct((T, SB_WIDTH), F32)] * 3,
        compiler_params=_params(("arbitrary", "arbitrary")),
    )(sqkv, sqkv, sqkv, do, tl, upre, uexc, first_block)


def _mixer_out(o_hg, hg, o_sb, g_hg, g_sb):
    n_hg, r_hg = _rms(o_hg)
    s_hg = _sigmoid(hg)
    n_sb, r_sb = _rms(o_sb)
    return dict(n_hg=n_hg, r_hg=r_hg, s_hg=s_hg, n_sb=n_sb, r_sb=r_sb,
                y_hg=n_hg * g_hg * (hg * s_hg), y_sb=n_sb * g_sb)


def mix_out_fwd(o_hg, proj_h, o_sb, x, norms, g_post, w_out, comm=None):
    T = x.shape[0]

    def body(ohg_ref, hg_ref, osb_ref, x_ref, nrm_ref, gp_ref, w_hbm, cat_ref, mix_ref, h1_ref, w_vmem):
        _load_once(w_hbm, w_vmem)
        nrm = nrm_ref[...]
        m = _mixer_out(ohg_ref[...], hg_ref[...], osb_ref[...], nrm[:, :HG_WIDTH], nrm[:, HG_WIDTH:])
        cat_ref[:, :HG_WIDTH] = m["y_hg"].astype(BF16)
        cat_ref[:, HG_WIDTH:] = m["y_sb"].astype(BF16)
        mix = jnp.dot(cat_ref[...], w_vmem[...], preferred_element_type=F32)
        mix_ref[...] = mix
        mh, _ = _rms(mix)
        h1_ref[...] = x_ref[...] + mh * gp_ref[...]

    return _pallas(
        body, comm=comm, edge=_grid_edge(T // _wide_tile(T)), name="mix_out_fwd", grid=(T // _wide_tile(T),),
        in_specs=[_wide_spec(HG_WIDTH), _wide_spec(HG_WIDTH, 3), _wide_spec(SB_WIDTH), _wide_spec(D_MODEL),
                  _full_spec((1, D_MODEL)), _full_spec((1, D_MODEL)), ANY_SPEC],
        out_specs=[_wide_spec(D_MODEL)] * 3,
        out_shape=[jax.ShapeDtypeStruct((T, D_MODEL), BF16), jax.ShapeDtypeStruct((T, D_MODEL), F32),
                   jax.ShapeDtypeStruct((T, D_MODEL), F32)],
        scratch_shapes=[pltpu.VMEM(w_out.shape, BF16)],
        compiler_params=_params(("arbitrary",)),
    )(o_hg, proj_h, o_sb, x, norms, g_post, w_out)


def ffn_fwd(h1, g_pre, g_post, w_gu, w_down):
    T = h1.shape[0]
    pw = w_gu.shape[2]

    def body(h1_ref, gpre_ref, gpost_ref, wgu_hbm, wd_hbm, u2_ref, gu_ref, act_ref, y_ref, h2_ref,
             wgu_vmem, wd_vmem, gu_s):
        _load_once(wgu_hbm, wgu_vmem)
        _load_once(wd_hbm, wd_vmem)
        h1v = h1_ref[...]
        hh, _ = _rms(h1v)
        u2 = (hh * gpre_ref[...]).astype(BF16)
        u2_ref[...] = u2
        for q in range(N_CHIPS):
            gu_s[:, pw * q:pw * (q + 1)] = jnp.dot(u2, wgu_vmem[q], preferred_element_type=F32)
        gu_ref[...] = gu_s[...].astype(BF16)
        gate = gu_s[:, :D_FF]
        act = (gate * _sigmoid(gate) * gu_s[:, D_FF:]).astype(BF16)
        act_ref[...] = act
        y = jnp.dot(act, wd_vmem[...], preferred_element_type=F32)
        y_ref[...] = y
        yh, _ = _rms(y)
        h2_ref[...] = h1v + yh * gpost_ref[...]

    return pl.pallas_call(
        body, name="ffn_fwd", grid=(T // ROW_TILE,),
        in_specs=[_row_spec(D_MODEL), _full_spec((1, D_MODEL)), _full_spec((1, D_MODEL)), ANY_SPEC, ANY_SPEC],
        out_specs=[_row_spec(D_MODEL), _row_spec(2 * D_FF), _row_spec(D_FF), _row_spec(D_MODEL), _row_spec(D_MODEL)],
        out_shape=[jax.ShapeDtypeStruct((T, D_MODEL), BF16), jax.ShapeDtypeStruct((T, 2 * D_FF), BF16),
                   jax.ShapeDtypeStruct((T, D_FF), BF16), jax.ShapeDtypeStruct((T, D_MODEL), F32),
                   jax.ShapeDtypeStruct((T, D_MODEL), F32)],
        scratch_shapes=[pltpu.VMEM(w_gu.shape, BF16), pltpu.VMEM(w_down.shape, BF16),
                        pltpu.VMEM((ROW_TILE, 2 * D_FF), F32)],
        compiler_params=_params(("arbitrary",)),
    )(h1, g_pre, g_post, w_gu, w_down)


def ple_loss(h2, p, target, w_ple, w_pg):
    T = h2.shape[0]
    pw = w_ple.shape[2]

    def body(h2_ref, p_ref, t_ref, wple_hbm, wpg_hbm, de_ref, ds_ref, dh2_ref, h2b_ref, pb_ref, pack_ref,
             wple_vmem, wpg_vmem, e_s):
        _load_once(wple_hbm, wple_vmem)
        _load_once(wpg_hbm, wpg_vmem)
        _zero_first(pack_ref)
        h2v = h2_ref[...]
        h2b = h2v.astype(BF16)
        h2b_ref[...] = h2b
        pb = p_ref[...].astype(BF16)
        pb_ref[...] = pb
        for q in range(N_CHIPS):
            e_s[:, pw * q:pw * (q + 1)] = jnp.dot(pb, wple_vmem[q], preferred_element_type=F32)
        e = e_s[...]
        sig = _sigmoid(jnp.dot(h2b, wpg_vmem[...], preferred_element_type=F32))
        err = h2v + e * sig - t_ref[...]
        part = 0.5 * jnp.sum(jnp.mean(err * err, axis=-1, keepdims=True), axis=0, keepdims=True)
        lane = lax.broadcasted_iota(jnp.int32, (1, D_MODEL), 1)
        pack_ref[ROW_LOSS:ROW_LOSS + 1, :] += jnp.where(lane == 0, part, 0.0)
        dh3 = err * (1.0 / D_MODEL)
        de_ref[...] = (dh3 * sig).astype(BF16)
        ds = (dh3 * e * sig * (1.0 - sig)).astype(BF16)
        ds_ref[...] = ds
        dh2_ref[...] = dh3 + _dot_nt(ds, wpg_vmem[...])

    return pl.pallas_call(
        body, name="ple_loss", grid=(T // _wide_tile(T),),
        in_specs=[_wide_spec(D_MODEL), _wide_spec(p.shape[1]), _wide_spec(D_MODEL), ANY_SPEC, ANY_SPEC],
        out_specs=[_wide_spec(D_MODEL), _wide_spec(D_MODEL), _wide_spec(D_MODEL), _wide_spec(D_MODEL),
                   _wide_spec(p.shape[1]), PACK_SPEC],
        out_shape=[jax.ShapeDtypeStruct((T, D_MODEL), BF16), jax.ShapeDtypeStruct((T, D_MODEL), BF16),
                   jax.ShapeDtypeStruct((T, D_MODEL), F32), jax.ShapeDtypeStruct((T, D_MODEL), BF16),
                   jax.ShapeDtypeStruct(p.shape, BF16), jax.ShapeDtypeStruct((8, D_MODEL), F32)],
        scratch_shapes=[pltpu.VMEM(w_ple.shape, BF16), pltpu.VMEM(w_pg.shape, BF16), pltpu.VMEM((_wide_tile(T), D_MODEL), F32)],
        compiler_params=_params(("arbitrary",)),
    )(h2, p, target, w_ple, w_pg)


def ffn_bwd(dh2, y, h1, gu, g_pre, g_post, w_gu, w_down):
    T = h1.shape[0]
    pw = w_gu.shape[2]

    def body(dh2_ref, y_ref, h1_ref, gu_ref, gpre_ref, gpost_ref, wgu_hbm, wd_hbm, dy_ref, dgu_ref, dh1_ref, pack_ref,
             wgu_vmem, wd_vmem):
        _load_once(wgu_hbm, wgu_vmem)
        _load_once(wd_hbm, wd_vmem)
        _zero_first(pack_ref)
        dh2v = dh2_ref[...]
        yh, ry = _rms(y_ref[...])
        dy, dw = _rms_bwd(dh2v, yh, ry, gpost_ref[...])
        pack_ref[ROW_FFN_POST:ROW_FFN_POST + 1, :] += _colsum(dw)
        dyb = dy.astype(BF16)
        dy_ref[...] = dyb
        dact = _dot_nt(dyb, wd_vmem[...])
        gate = gu_ref[:, :D_FF].astype(F32)
        up = gu_ref[:, D_FF:].astype(F32)
        sg = _sigmoid(gate)
        dgu_ref[:, :D_FF] = (dact * up * (sg * (1.0 + gate * (1.0 - sg)))).astype(BF16)
        dgu_ref[:, D_FF:] = (dact * gate * sg).astype(BF16)
        du2 = _dot_nt(dgu_ref[:, :pw], wgu_vmem[0])
        for q in range(1, N_CHIPS):
            du2 = du2 + _dot_nt(dgu_ref[:, pw * q:pw * (q + 1)], wgu_vmem[q])
        hh, rh = _rms(h1_ref[...])
        dh, dw = _rms_bwd(du2, hh, rh, gpre_ref[...])
        pack_ref[ROW_FFN_PRE:ROW_FFN_PRE + 1, :] += _colsum(dw)
        dh1_ref[...] = dh2v + dh

    return pl.pallas_call(
        body, name="ffn_bwd", grid=(T // ROW_TILE,),
        in_specs=[_row_spec(D_MODEL), _row_spec(D_MODEL), _row_spec(D_MODEL), _row_spec(2 * D_FF),
                  _full_spec((1, D_MODEL)), _full_spec((1, D_MODEL)), ANY_SPEC, ANY_SPEC],
        out_specs=[_row_spec(D_MODEL), _row_spec(2 * D_FF), _row_spec(D_MODEL), PACK_SPEC],
        out_shape=[jax.ShapeDtypeStruct((T, D_MODEL), BF16), jax.ShapeDtypeStruct((T, 2 * D_FF), BF16),
                   jax.ShapeDtypeStruct((T, D_MODEL), F32), jax.ShapeDtypeStruct((8, D_MODEL), F32)],
        scratch_shapes=[pltpu.VMEM(w_gu.shape, BF16), pltpu.VMEM(w_down.shape, BF16)],
        compiler_params=_params(("arbitrary",)),
    )(dh2, y, h1, gu, g_pre, g_post, w_gu, w_down)


def mix_out_bwd(dh1, mix, o_hg, proj_h, o_sb, norms, g_post, w_out, comm=None):
    T = dh1.shape[0]

    def body(dh1_ref, mix_ref, ohg_ref, hg_ref, osb_ref, nrm_ref, gp_ref, w_hbm, dmix_ref, dohg_ref, dhg_ref, dosb_ref,
             pack_ref, w_vmem):
        _load_once(w_hbm, w_vmem)
        _zero_first(pack_ref)
        mh, rm = _rms(mix_ref[...])
        dmix, dw = _rms_bwd(dh1_ref[...], mh, rm, gp_ref[...])
        pack_ref[ROW_ATTN_POST:ROW_ATTN_POST + 1, :] += _colsum(dw)
        dmb = dmix.astype(BF16)
        dmix_ref[...] = dmb
        dcat = _dot_nt(dmb, w_vmem[...])
        nrm = nrm_ref[...]
        g_hg, g_sb = nrm[:, :HG_WIDTH], nrm[:, HG_WIDTH:]
        hg = hg_ref[...]
        m = _mixer_out(ohg_ref[...], hg, osb_ref[...], g_hg, g_sb)
        d_hg = dcat[:, :HG_WIDTH]
        silu = hg * m["s_hg"]
        dhg_ref[...] = d_hg * (m["n_hg"] * g_hg) * (m["s_hg"] * (1.0 + hg * (1.0 - m["s_hg"])))
        dx, dw = _rms_bwd(d_hg * silu, m["n_hg"], m["r_hg"], g_hg)
        dohg_ref[...] = dx
        pack_ref[ROW_MIX_NORMS:ROW_MIX_NORMS + 1, :HG_WIDTH] += _colsum(dw)
        dx, dw = _rms_bwd(dcat[:, HG_WIDTH:], m["n_sb"], m["r_sb"], g_sb)
        dosb_ref[...] = dx
        pack_ref[ROW_MIX_NORMS:ROW_MIX_NORMS + 1, HG_WIDTH:] += _colsum(dw)

    return _pallas(
        body, comm=comm, edge=_grid_edge(T // _wide_tile(T)), name="mix_out_bwd", grid=(T // _wide_tile(T),),
        in_specs=[_wide_spec(D_MODEL), _wide_spec(D_MODEL), _wide_spec(HG_WIDTH), _wide_spec(HG_WIDTH, 3), _wide_spec(SB_WIDTH),
                  _full_spec((1, D_MODEL)), _full_spec((1, D_MODEL)), ANY_SPEC],
        out_specs=[_wide_spec(D_MODEL), _wide_spec(HG_WIDTH), _wide_spec(HG_WIDTH), _wide_spec(SB_WIDTH), PACK_SPEC],
        out_shape=[jax.ShapeDtypeStruct((T, D_MODEL), BF16), jax.ShapeDtypeStruct((T, HG_WIDTH), F32),
                   jax.ShapeDtypeStruct((T, HG_WIDTH), F32), jax.ShapeDtypeStruct((T, SB_WIDTH), F32),
                   jax.ShapeDtypeStruct((8, D_MODEL), F32)],
        scratch_shapes=[pltpu.VMEM(w_out.shape, BF16)],
        compiler_params=_params(("arbitrary",)),
    )(dh1, mix, o_hg, proj_h, o_sb, norms, g_post, w_out)


def in_proj_bwd(parts, x, dh1, g_pre, w_in, comm=None):
    T = x.shape[0]
    pw = w_in.shape[2]
    n_parts = len(parts)

    def body(*refs):
        part_refs = refs[:n_parts]
        x_ref, dh1_ref, g_ref, w_hbm, dproj_ref, dx_ref, pack_ref, w_vmem = refs[n_parts:]
        _load_once(w_hbm, w_vmem)
        _zero_first(pack_ref)
        for n, ref in enumerate(part_refs):
            dproj_ref[:, HG_WIDTH * n:HG_WIDTH * (n + 1)] = ref[...].astype(BF16)
        du = _dot_nt(dproj_ref[:, :pw], w_vmem[0])
        for q in range(1, N_CHIPS):
            du = du + _dot_nt(dproj_ref[:, pw * q:pw * (q + 1)], w_vmem[q])
        xh, r = _rms(x_ref[...])
        dx, dw = _rms_bwd(du, xh, r, g_ref[...])
        pack_ref[ROW_ATTN_PRE:ROW_ATTN_PRE + 1, :] += _colsum(dw)
        dx_ref[...] = dh1_ref[...] + dx

    return _pallas(
        body, comm=comm, edge=_grid_edge(T // _wide_tile(T)), name="in_proj_bwd", grid=(T // _wide_tile(T),),
        in_specs=[_wide_spec(HG_WIDTH)] * n_parts + [_wide_spec(D_MODEL), _wide_spec(D_MODEL), _full_spec((1, D_MODEL)), ANY_SPEC],
        out_specs=[_wide_spec(n_parts * HG_WIDTH), _wide_spec(D_MODEL), PACK_SPEC],
        out_shape=[jax.ShapeDtypeStruct((T, n_parts * HG_WIDTH), BF16), jax.ShapeDtypeStruct((T, D_MODEL), F32),
                   jax.ShapeDtypeStruct((8, D_MODEL), F32)],
        scratch_shapes=[pltpu.VMEM(w_in.shape, BF16)],
        compiler_params=_params(("arbitrary",)),
    )(*parts, x, dh1, g_pre, w_in)


def weight_grad(a, g, name, *, tm, tn, tk=512, col_pieces=False, comm=None):
    T, M = a.shape
    N = g.shape[1]
    tk = min(tk, T)
    steps = T // tk

    def body(a_ref, g_ref, o_ref):
        @pl.when(pl.program_id(2) == 0)
        def _():
            o_ref[...] = jnp.zeros(o_ref.shape, F32)

        o_ref[...] += _dot_tn(a_ref[...], g_ref[...]).reshape(o_ref.shape)

    if col_pieces:
        out_shape = jax.ShapeDtypeStruct((N // tn, M, tn), F32)
        out_spec = pl.BlockSpec((1, tm, tn), lambda i, j, k: (j, i, 0))
    else:
        out_shape = jax.ShapeDtypeStruct((M, N), F32)
        out_spec = pl.BlockSpec((tm, tn), lambda i, j, k: (i, j))
    grid = (M // tm, N // tn, steps)

    def edge():
        at = [pl.program_id(d) for d in range(3)]
        return ((at[0] == 0) & (at[1] == 0) & (at[2] == 0),
                (at[0] == grid[0] - 1) & (at[1] == grid[1] - 1) & (at[2] == grid[2] - 1))

    return _pallas(
        body, comm=comm, edge=edge, name=name, grid=grid,
        in_specs=[pl.BlockSpec((tk, tm), lambda i, j, k: (k, i)), pl.BlockSpec((tk, tn), lambda i, j, k: (k, j))],
        out_specs=[out_spec], out_shape=[out_shape],
        compiler_params=_params(("arbitrary", "arbitrary", "arbitrary")),
    )(a, g)


def _place():
    x, y, c = lax.axis_index("x"), lax.axis_index("y"), lax.axis_index("c")
    chips = [(1 - x, y), (x, 1 - y), (1 - x, 1 - y)]
    return x, y, c, chips


def _chip_index(cx, cy):
    return 2 * cx + cy


def _own_slot(piece, slots):
    me = _chip_index(lax.axis_index("x"), lax.axis_index("y"))
    landing = lax.empty((slots,) + piece.shape[1:], piece.dtype)
    return lax.dynamic_update_slice(landing, piece, (me,) + (0,) * (piece.ndim - 1))


def _rcopy(src, dst, send_sem, recv_sem, device):
    return pltpu.make_async_remote_copy(src_ref=src, dst_ref=dst, send_sem=send_sem, recv_sem=recv_sem,
                                        device_id=device, device_id_type=MESH)


def gather_weights(shards):
    n = len(shards)

    def body(*refs):
        ins, outs = refs[:n], refs[2 * n:3 * n]
        send_sems, recv_sems = refs[3 * n:]
        x, y, c, chips = _place()
        me = _chip_index(x, y)
        sibling = (x, y, 1 - c)

        def rows(w, core):
            half = ins[w].shape[0] // 2
            return pl.ds(core * half, half)

        sends = []
        for w in range(n):
            for j, chip in enumerate(chips):
                sends.append(_rcopy(ins[w].at[rows(w, c)], outs[w].at[me, rows(w, c)],
                                    send_sems.at[6 * w + j], recv_sems.at[6 * w + j], (*chip, c)))
        for cp in sends:
            cp.start()
        passed = []
        for w in range(n):
            for j, chip in enumerate(chips):
                block = outs[w].at[_chip_index(*chip), rows(w, c)]
                _rcopy(block, block, send_sems.at[6 * w + j], recv_sems.at[6 * w + j], (*chip, c)).wait_recv()
                cp = _rcopy(block, block, send_sems.at[6 * w + 3 + j], recv_sems.at[6 * w + 3 + j], sibling)
                cp.start()
                passed.append(cp)
        for w in range(n):
            for j, chip in enumerate(chips):
                block = outs[w].at[_chip_index(*chip), rows(w, 1 - c)]
                _rcopy(block, block, send_sems.at[6 * w + 3 + j], recv_sems.at[6 * w + 3 + j], sibling).wait_recv()
        for cp in sends + passed:
            cp.wait_send()

    filled = [_own_slot(s[None], N_CHIPS) for s in shards]
    return pl.pallas_call(
        body, name="gather_weights",
        in_specs=[ANY_SPEC] * (2 * n), out_specs=[ANY_SPEC] * n,
        out_shape=[jax.ShapeDtypeStruct(f.shape, f.dtype) for f in filled],
        input_output_aliases={n + w: w for w in range(n)},
        scratch_shapes=[pltpu.SemaphoreType.DMA((6 * n,)), pltpu.SemaphoreType.DMA((6 * n,))],
    )(*shards, *filled)


def _run_comm(comm, name):
    c_in, c_out = len(comm.inputs), len(comm.out_shape)

    def body(*refs):
        parts = refs[:c_in], refs[c_in:c_in + c_out], refs[c_in + c_out:]
        comm.start(*parts)
        comm.finish(*parts)

    return pl.pallas_call(
        body, name=name, in_specs=[ANY_SPEC] * c_in, out_specs=[ANY_SPEC] * c_out, out_shape=comm.out_shape,
        scratch_shapes=comm.scratch, input_output_aliases=comm.aliases)(*comm.inputs)


def _both(first, second):
    n_in, n_out, n_scr = len(first.inputs), len(first.out_shape), len(first.scratch)

    def split(ins, outs, scr):
        return (ins[:n_in], outs[:n_out], scr[:n_scr]), (ins[n_in:], outs[n_out:], scr[n_scr:])

    def start(*refs):
        a, b = split(*refs)
        first.start(*a)
        second.start(*b)

    def finish(*refs):
        a, b = split(*refs)
        first.finish(*a)
        second.finish(*b)

    aliases = dict(first.aliases)
    aliases.update({n_in + i: n_out + o for i, o in second.aliases.items()})
    return Comm(first.inputs + second.inputs, first.out_shape + second.out_shape, aliases,
                first.scratch + second.scratch, start, finish)


def _dma_sems(count):
    return [pltpu.SemaphoreType.DMA((count,)), pltpu.SemaphoreType.DMA((count,))]


def gather_over_ici(shards):
    n = len(shards)

    def copies(ins, outs, sems):
        send_sems, recv_sems = sems
        x, y, c, chips = _place()
        me = _chip_index(x, y)
        pairs = []
        for w in range(n):
            half = shards[w].shape[0] // 2
            rows = pl.ds(c * half, half)
            for j, chip in enumerate(chips):
                k = 3 * w + j
                landed = outs[w].at[_chip_index(*chip), rows]
                pairs.append((_rcopy(ins[w].at[rows], outs[w].at[me, rows], send_sems.at[k], recv_sems.at[k], (*chip, c)),
                              _rcopy(landed, landed, send_sems.at[k], recv_sems.at[k], (*chip, c))))
        return pairs

    def start(*refs):
        for send, _ in copies(*refs):
            send.start()

    def finish(*refs):
        pairs = copies(*refs)
        for _, landed in pairs:
            landed.wait_recv()
        for send, _ in pairs:
            send.wait_send()

    filled = [_own_slot(s[None], N_CHIPS) for s in shards]
    return Comm(list(shards) + filled, [jax.ShapeDtypeStruct(f.shape, f.dtype) for f in filled],
                {n + w: w for w in range(n)}, _dma_sems(3 * n), start, finish)


def gather_over_d2d(landed):
    n = len(landed)

    def copies(ins, outs, sems):
        send_sems, recv_sems = sems
        x, y, c, chips = _place()
        sibling = (x, y, 1 - c)
        pairs = []
        for w in range(n):
            half = landed[w].shape[1] // 2
            for j, chip in enumerate(chips):
                k = 3 * w + j
                mine = outs[w].at[_chip_index(*chip), pl.ds(c * half, half)]
                theirs = outs[w].at[_chip_index(*chip), pl.ds((1 - c) * half, half)]
                pairs.append((_rcopy(mine, mine, send_sems.at[k], recv_sems.at[k], sibling),
                              _rcopy(theirs, theirs, send_sems.at[k], recv_sems.at[k], sibling)))
        return pairs

    def start(*refs):
        for send, _ in copies(*refs):
            send.start()

    def finish(*refs):
        pairs = copies(*refs)
        for _, arrived in pairs:
            arrived.wait_recv()
        for send, _ in pairs:
            send.wait_send()

    return Comm(list(landed), [jax.ShapeDtypeStruct(a.shape, a.dtype) for a in landed], {w: w for w in range(n)},
                _dma_sems(3 * n), start, finish)


def core_halves(grads):
    n = len(grads)

    def copies(ins, outs, sems):
        send_sems, recv_sems = sems
        x, y, c, _ = _place()
        out = []
        for w in range(n):
            half = grads[w].shape[1] // 2
            out.append(_rcopy(ins[w].at[:, pl.ds((1 - c) * half, half), :], outs[w],
                              send_sems.at[w], recv_sems.at[w], (x, y, 1 - c)))
        return out

    def start(*refs):
        for cp in copies(*refs):
            cp.start()

    def finish(*refs):
        for cp in copies(*refs):
            cp.wait()

    return Comm(list(grads), [jax.ShapeDtypeStruct((g.shape[0], g.shape[1] // 2, g.shape[2]), g.dtype) for g in grads],
                {}, _dma_sems(n), start, finish)


def chip_partials(pairs):
    partials = [p for p, _ in pairs]
    filled = [landing for _, landing in pairs]
    n = len(partials)

    def copies(ins, outs, sems):
        send_sems, recv_sems = sems
        x, y, c, chips = _place()
        me = _chip_index(x, y)
        pairs = []
        for w in range(n):
            for j, chip in enumerate(chips):
                k = 3 * w + j
                landed = outs[w].at[_chip_index(*chip)]
                pairs.append((_rcopy(ins[w].at[_chip_index(*chip)], outs[w].at[me], send_sems.at[k], recv_sems.at[k],
                                     (*chip, c)),
                              _rcopy(landed, landed, send_sems.at[k], recv_sems.at[k], (*chip, c))))
        return pairs

    def start(*refs):
        for send, _ in copies(*refs):
            send.start()

    def finish(*refs):
        pairs = copies(*refs)
        for _, landed in pairs:
            landed.wait_recv()
        for send, _ in pairs:
            send.wait_send()

    return Comm(list(partials) + filled, [jax.ShapeDtypeStruct(p.shape, p.dtype) for p in partials],
                {n + w: w for w in range(n)}, _dma_sems(3 * n), start, finish)


def join_core_halves(grads):
    n = len(grads)

    def body(*refs):
        outs = refs[n:2 * n]
        send_sems, recv_sems = refs[2 * n:]
        x, y, c, _ = _place()
        sibling = (x, y, 1 - c)
        copies = []
        for w in range(n):
            half = outs[w].shape[0] // 2
            mine = outs[w].at[pl.ds(c * half, half), :]
            copies.append(_rcopy(mine, mine, send_sems.at[w], recv_sems.at[w], sibling))
        for cp in copies:
            cp.start()
        for w in range(n):
            half = outs[w].shape[0] // 2
            theirs = outs[w].at[pl.ds((1 - c) * half, half), :]
            _rcopy(theirs, theirs, send_sems.at[w], recv_sems.at[w], sibling).wait_recv()
        for cp in copies:
            cp.wait_send()

    return pl.pallas_call(
        body, name="join_core_halves",
        in_specs=[ANY_SPEC] * n, out_specs=[ANY_SPEC] * n,
        out_shape=[jax.ShapeDtypeStruct(g.shape, g.dtype) for g in grads],
        input_output_aliases={w: w for w in range(n)},
        scratch_shapes=[pltpu.SemaphoreType.DMA((n,)), pltpu.SemaphoreType.DMA((n,))],
    )(*grads)


def _elementwise_rows(rows, cap=512):
    for t in range(min(rows, cap), 0, -8):
        if rows % t == 0 and t % 16 == 0:
            return t
    return rows


def add_core_halves(grad, got, core, name):
    _, rows, cols = got.shape
    tr = _elementwise_rows(rows)
    nt = rows // tr

    def body(core_ref, a_ref, b_ref, o_ref, landing_ref):
        o_ref[...] = (a_ref[...] + b_ref[...]).astype(BF16)
        landing_ref[...] = o_ref[...]

    spec = pl.BlockSpec((1, tr, cols), lambda q, i, core_ref: (q, i, 0))
    own = pl.BlockSpec((1, tr, cols), lambda q, i, core_ref: (q, core_ref[0] * nt + i, 0))
    return pl.pallas_call(
        body, name=name,
        grid_spec=pltpu.PrefetchScalarGridSpec(num_scalar_prefetch=1, grid=(N_CHIPS, nt), in_specs=[own, spec],
                                               out_specs=[spec, spec]),
        out_shape=[jax.ShapeDtypeStruct(got.shape, BF16)] * 2,
        compiler_params=_params(("arbitrary", "arbitrary")),
    )(core, grad, got)


def add_chip_partials(parts, core, name):
    _, rows, cols = parts.shape
    tr = _elementwise_rows(rows)
    nt = rows // tr

    def body(core_ref, p_ref, o_ref):
        acc = p_ref[0].astype(F32)
        for q in range(1, N_CHIPS):
            acc = acc + p_ref[q].astype(F32)
        o_ref[...] = acc

    return pl.pallas_call(
        body, name=name,
        grid_spec=pltpu.PrefetchScalarGridSpec(
            num_scalar_prefetch=1, grid=(nt,),
            in_specs=[pl.BlockSpec((N_CHIPS, tr, cols), lambda i, core_ref: (0, i, 0))],
            out_specs=pl.BlockSpec((tr, cols), lambda i, core_ref: (core_ref[0] * nt + i, 0))),
        out_shape=jax.ShapeDtypeStruct((2 * rows, cols), F32),
        compiler_params=_params(("arbitrary",)),
    )(core, parts)


def _adamw_math(w, g, m, v):
    m = ADAM_B1 * m + (1.0 - ADAM_B1) * g
    v = ADAM_B2 * v + (1.0 - ADAM_B2) * (g * g)
    m_hat = m / (1.0 - ADAM_B1 ** ADAM_STEP)
    v_hat = v / (1.0 - ADAM_B2 ** ADAM_STEP)
    delta = -ADAM_LR * (m_hat / (jnp.sqrt(v_hat) + ADAM_EPS) + ADAM_WD * w)
    return delta, m, v


def adamw(w, g, m, v, name):
    rows, cols = w.shape
    tr = _elementwise_rows(rows, 256)

    def body(w_ref, g_ref, m_ref, v_ref, g_out_ref, d_ref, nm_ref, nv_ref):
        d, nm, nv = _adamw_math(w_ref[...], g_ref[...], m_ref[...], v_ref[...])
        g_out_ref[...] = g_ref[...]
        d_ref[...] = d
        nm_ref[...] = nm
        nv_ref[...] = nv

    spec = pl.BlockSpec((tr, cols), lambda i: (i, 0))
    return pl.pallas_call(
        body, name=name, grid=(rows // tr,), in_specs=[spec] * 4, out_specs=[spec] * 4,
        out_shape=[jax.ShapeDtypeStruct((rows, cols), F32)] * 4,
        compiler_params=_params(("arbitrary",)),
    )(w, g, m, v)


def reduce_small(packs, w, m, v):
    n = len(packs)
    n_dev = 8
    flips = [(fx, fy, fc) for fx in (0, 1) for fy in (0, 1) for fc in (0, 1)][1:]

    def body(*refs):
        pack_refs = refs[:n]
        w_ref, m_ref, v_ref, g_out, d_out, m_out, v_out, mine, slots, send_sems, recv_sems = refs[n:]
        x, y, c, _ = _place()
        me = 4 * x + 2 * y + c
        acc = pack_refs[0][...]
        for ref in pack_refs[1:]:
            acc = acc + ref[...]
        mine[...] = acc
        sends = []
        for k, (fx, fy, fc) in enumerate(flips):
            peer = (x ^ fx, y ^ fy, c ^ fc)
            sends.append(_rcopy(mine, slots.at[me], send_sems.at[k], recv_sems.at[me], peer))
        for cp in sends:
            cp.start()
        slots[me] = acc
        for fx, fy, fc in flips:
            src = 4 * (x ^ fx) + 2 * (y ^ fy) + (c ^ fc)
            _rcopy(mine, slots.at[src], send_sems.at[0], recv_sems.at[src], (x, y, c)).wait_recv()
        for cp in sends:
            cp.wait_send()
        total = slots[0]
        for d in range(1, n_dev):
            total = total + slots[d]
        g_out[...] = total
        d, nm, nv = _adamw_math(w_ref[...], total, m_ref[...], v_ref[...])
        d_out[...] = d
        m_out[...] = nm
        v_out[...] = nv

    vm = pl.BlockSpec(memory_space=pltpu.VMEM)
    return pl.pallas_call(
        body, name="reduce_small",
        in_specs=[vm] * (n + 3), out_specs=[vm] * 4,
        out_shape=[jax.ShapeDtypeStruct((8, D_MODEL), F32)] * 4,
        scratch_shapes=[pltpu.VMEM((8, D_MODEL), F32), pltpu.VMEM((n_dev, 8, D_MODEL), F32),
                        pltpu.SemaphoreType.DMA((len(flips),)), pltpu.SemaphoreType.DMA((n_dev,))],
    )(*packs, w, m, v)


def _column_pieces(g):
    return g.reshape(g.shape[0], N_CHIPS, g.shape[1] // N_CHIPS).transpose(1, 0, 2)


def _pack_small(attn_pre, gamma, hg_norm, sb_norm, attn_post, ffn_pre, ffn_post):
    rows = [attn_pre, gamma.reshape(1, D_MODEL), jnp.concatenate([hg_norm, sb_norm], axis=1), attn_post, ffn_pre, ffn_post,
            jnp.zeros((2, D_MODEL), F32)]
    return jnp.concatenate(rows, axis=0)


def _unpack_small(pack):
    return (pack[ROW_ATTN_PRE:ROW_ATTN_PRE + 1], pack[ROW_GAMMA].reshape(2, HG_WIDTH),
            pack[ROW_MIX_NORMS:ROW_MIX_NORMS + 1, :HG_WIDTH], pack[ROW_MIX_NORMS:ROW_MIX_NORMS + 1, HG_WIDTH:],
            pack[ROW_ATTN_POST:ROW_ATTN_POST + 1], pack[ROW_FFN_PRE:ROW_FFN_PRE + 1], pack[ROW_FFN_POST:ROW_FFN_POST + 1])


def kernel(x, p, attn_pre_norm, w_in, hg_lower_gamma, hg_out_norm, sb_out_norm, w_out, attn_post_norm, ffn_pre_norm, w_gate_up, w_down, ffn_post_norm, ple_proj, ple_gate, loss_target, m_attn_pre_norm, m_w_in, m_hg_lower_gamma, m_hg_out_norm, m_sb_out_norm, m_w_out, m_attn_post_norm, m_ffn_pre_norm, m_w_gate_up, m_w_down, m_ffn_post_norm, m_ple_proj, m_ple_gate, v_attn_pre_norm, v_w_in, v_hg_lower_gamma, v_hg_out_norm, v_sb_out_norm, v_w_out, v_attn_post_norm, v_ffn_pre_norm, v_w_gate_up, v_w_down, v_ffn_post_norm, v_ple_proj, v_ple_gate):
    x2 = x[0]
    p2 = p[0, 0]
    target = loss_target[0]
    big = dict(w_in=(w_in, m_w_in, v_w_in), w_out=(w_out, m_w_out, v_w_out), w_gate_up=(w_gate_up, m_w_gate_up, v_w_gate_up),
               w_down=(w_down, m_w_down, v_w_down), ple_proj=(ple_proj, m_ple_proj, v_ple_proj),
               ple_gate=(ple_gate, m_ple_gate, v_ple_gate))
    names = list(big)
    big = {k: tuple(a[0] for a in t) for k, t in big.items()}

    shard16 = {k: big[k][0].astype(BF16) for k in names}
    w_in_full, = gather_weights([shard16["w_in"]])
    mix_norms = jnp.concatenate([hg_out_norm, sb_out_norm], axis=1)
    small_ones = ["w_out", "ple_proj", "ple_gate"]

    proj_h, sqkv, u1, *landed_small = in_proj_fwd(
        x2, attn_pre_norm, w_in_full, comm=gather_over_ici([shard16[k] for k in small_ones]))
    o_sb, sb_totals, sb_first, landed_gu = sb_fwd(sqkv, comm=gather_over_ici([shard16["w_gate_up"]]))
    o_hg, states, landed_down, *full_small = hgrn2_fwd(
        proj_h, hg_lower_gamma, comm=_both(gather_over_ici([shard16["w_down"]]), gather_over_d2d(landed_small)))
    full = dict(zip(small_ones, full_small), w_in=w_in_full)
    w_out_full = full["w_out"].reshape(D_MODEL, D_MODEL)
    w_pg_full = full["ple_gate"].reshape(D_MODEL, D_MODEL)
    cat, mix, h1, full["w_gate_up"], full["w_down"] = mix_out_fwd(
        o_hg, proj_h, o_sb, x2, mix_norms, attn_post_norm, w_out_full, comm=gather_over_d2d([landed_gu, landed_down]))
    w_down_full = full["w_down"].reshape(D_FF, D_MODEL)
    u2, gu, act, y, h2 = ffn_fwd(h1, ffn_pre_norm, ffn_post_norm, full["w_gate_up"], w_down_full)

    core = lax.axis_index("c").astype(jnp.int32).reshape(1)
    de, ds, dh2, h2b, pb, pack_loss = ple_loss(h2, p2, target, full["ple_proj"], w_pg_full)
    dy, dgu, dh1, pack_ffn = ffn_bwd(dh2, y, h1, gu, ffn_pre_norm, ffn_post_norm, full["w_gate_up"], w_down_full)
    local = {}
    local["w_gate_up"], = weight_grad(u2, dgu, "grad_w_gate_up", tm=D_MODEL, tn=full["w_gate_up"].shape[2], tk=2048,
                                      col_pieces=True)
    grad_down, got_gu = weight_grad(act, dy, "grad_w_down", tm=D_FF // 2, tn=D_MODEL, tk=2048,
                                    comm=core_halves([local["w_gate_up"]]))
    local["w_down"] = grad_down.reshape(full["w_down"].shape)
    local["ple_proj"] = _column_pieces(weight_grad(pb, de, "grad_ple_proj", tm=pb.shape[1], tn=D_MODEL, tk=1024)[0])
    grad_pg, got_down = weight_grad(h2b, ds, "grad_ple_gate", tm=D_MODEL, tn=D_MODEL, tk=2048,
                                    comm=core_halves([local["w_down"]]))
    local["ple_gate"] = grad_pg.reshape(full["ple_gate"].shape)
    early = list(local)
    dmix, do_hg, dhg, do_sb, pack_mix, *got_ple = mix_out_bwd(
        dh1, mix, o_hg, proj_h, o_sb, mix_norms, attn_post_norm, w_out_full,
        comm=core_halves([local["ple_proj"], local["ple_gate"]]))
    got = [got_gu, got_down] + got_ple
    partial = [add_core_halves(local[k], g, core, "add_core_halves_" + k) for k, g in zip(early, got)]
    local["w_out"] = weight_grad(cat, dmix, "grad_w_out", tm=D_MODEL, tn=D_MODEL, tk=2048)[0].reshape(full["w_out"].shape)
    dsq, dsk, dsv = sb_bwd(sqkv, do_sb, sb_totals, sb_first)
    dhq, dhf, dhi, pack_hg, *by_source, got_out = hgrn2_bwd(
        proj_h, hg_lower_gamma, states, do_hg, comm=_both(chip_partials(partial[:2]), core_halves([local["w_out"]])))
    early.append("w_out")
    partial.append(add_core_halves(local["w_out"], got_out, core, "add_core_halves_w_out"))
    dproj, grad_x, pack_in = in_proj_bwd([dhq, dhf, dhi, dhg, dsq, dsk, dsv], x2, dh1, attn_pre_norm, full["w_in"])

    late = ["w_in"]
    local["w_in"], *more = weight_grad(u1, dproj, "grad_w_in", tm=D_MODEL, tn=full["w_in"].shape[2], tk=2048,
                                       col_pieces=True, comm=chip_partials(partial[2:]))
    halves = {k: add_chip_partials(s, core, "add_chip_partials_" + k) for k, s in zip(early, by_source + more)}
    got = _run_comm(core_halves([local[k] for k in late]), "exchange_core_halves")
    partial = [add_core_halves(local[k], g, core, "add_core_halves_" + k) for k, g in zip(late, got)]
    by_source = _run_comm(chip_partials(partial), "exchange_chip_partials")
    halves.update({k: add_chip_partials(s, core, "add_chip_partials_" + k) for k, s in zip(late, by_source)})
    grads = dict(zip(names, join_core_halves([halves[k] for k in names])))

    upd = {k: adamw(big[k][0], grads[k], big[k][1], big[k][2], "adamw_" + k) for k in names}

    small = reduce_small(
        [pack_loss, pack_ffn, pack_mix, pack_hg, pack_in],
        _pack_small(attn_pre_norm, hg_lower_gamma, hg_out_norm, sb_out_norm, attn_post_norm, ffn_pre_norm, ffn_post_norm),
        _pack_small(m_attn_pre_norm, m_hg_lower_gamma, m_hg_out_norm, m_sb_out_norm, m_attn_post_norm, m_ffn_pre_norm, m_ffn_post_norm),
        _pack_small(v_attn_pre_norm, v_hg_lower_gamma, v_hg_out_norm, v_sb_out_norm, v_attn_post_norm, v_ffn_pre_norm, v_ffn_post_norm),
    )
    loss = small[0][ROW_LOSS, 0]
    s_grad, s_delta, s_m, s_v = (_unpack_small(t) for t in small)

    def ordered(small_vals, big_vals):
        a_pre, gam, hg_n, sb_n, a_post, f_pre, f_post = small_vals
        b = {k: big_vals[k][None] for k in names}
        return (a_pre, b["w_in"], gam, hg_n, sb_n, b["w_out"], a_post, f_pre, b["w_gate_up"], b["w_down"], f_post,
                b["ple_proj"], b["ple_gate"])

    return (loss, grad_x[None],
            *ordered(s_grad, {k: upd[k][0] for k in names}),
            *ordered(s_delta, {k: upd[k][1] for k in names}),
            *ordered(s_m, {k: upd[k][2] for k in names}),
            *ordered(s_v, {k: upd[k][3] for k in names}))
```

```python
import functools
from typing import Callable, NamedTuple

import numpy as np
import jax
import jax.numpy as jnp
from jax import lax
from jax.experimental import pallas as pl
from jax.experimental.pallas import tpu as pltpu

F32 = jnp.float32
BF16 = jnp.bfloat16
MESH = pl.DeviceIdType.MESH

RMS_EPS = 1e-6
D_MODEL = 1024
HG_WIDTH = 512
HG_HEADS = 4
HG_DK = 128
HG_CHUNK = 64
HG_LEVELS = (32, 16, 8, 4, 2, 1)
HG_CHUNKS_PER_STEP = 8
HG_CHUNKS_PER_STEP_BWD = 4
SB_WIDTH = 512
SB_BLOCK = 128
SB_DH = 64
SB_SCALE = SB_DH ** -0.5
SB_UNDERFLOW_LOG = -87.5
SB_UNROLL = 2
SB_GROUP = 4
D_FF = 2816
N_CHIPS = 4
ROW_TILE = 256
WIDE_ROW_TILE = 512
V7X_VMEM_LIMIT = 56 * 1024 * 1024

ADAM_LR = 0.001
ADAM_B1 = 0.9
ADAM_B2 = 0.999
ADAM_EPS = 1e-08
ADAM_WD = 0.01
ADAM_STEP = 10

ROW_ATTN_PRE, ROW_GAMMA, ROW_MIX_NORMS, ROW_ATTN_POST, ROW_FFN_PRE, ROW_FFN_POST, ROW_LOSS = range(7)


def _params(sem=None, vmem=V7X_VMEM_LIMIT):
    return pltpu.CompilerParams(dimension_semantics=sem, vmem_limit_bytes=vmem)


def _dot(a, b):
    return jnp.dot(a.astype(BF16), b.astype(BF16), preferred_element_type=F32)


def _dot_nt(a, b):
    return lax.dot_general(a.astype(BF16), b.astype(BF16), (((1,), (1,)), ((), ())), preferred_element_type=F32)


def _dot_tn(a, b):
    return lax.dot_general(a.astype(BF16), b.astype(BF16), (((0,), (0,)), ((), ())), preferred_element_type=F32)


def _split(x):
    hi = x.astype(BF16)
    lo = (x - hi.astype(F32)).astype(BF16)
    return hi, lo


def _sum01_left(m01, x):
    hi, lo = _split(x)
    return jnp.dot(m01, hi, preferred_element_type=F32) + jnp.dot(m01, lo, preferred_element_type=F32)


def _sum01_right(x, m01_twice):
    hi, lo = _split(x)
    return jnp.dot(jnp.concatenate([hi, lo], axis=1), m01_twice, preferred_element_type=F32)


def _rms(x):
    r = lax.rsqrt(jnp.mean(x * x, axis=-1, keepdims=True) + RMS_EPS)
    return x * r, r


def _rms_bwd(dy, xhat, r, w):
    dxh = dy * w
    dx = r * (dxh - xhat * jnp.mean(dxh * xhat, axis=-1, keepdims=True))
    return dx, dy * xhat


def _sigmoid(x):
    return 1.0 / (1.0 + jnp.exp(-x))


def _neg_softplus(z):
    return -(jnp.maximum(z, 0.0) + jnp.log(1.0 + jnp.exp(-jnp.abs(z))))


def _colsum(x):
    return jnp.sum(x, axis=0, keepdims=True)


def _load_once(src_hbm, dst_vmem):
    @pl.when(pl.program_id(0) == 0)
    def _():
        pltpu.sync_copy(src_hbm, dst_vmem)


def _zero_first(ref):
    @pl.when(pl.program_id(0) == 0)
    def _():
        ref[...] = jnp.zeros(ref.shape, ref.dtype)


def _row_spec(width, col=0, rows=ROW_TILE):
    return pl.BlockSpec((rows, width), lambda i, col=col: (i, col))


def _wide_spec(width, col=0):
    return _row_spec(width, col, WIDE_ROW_TILE)


def _wide_tile(T):
    assert T % WIDE_ROW_TILE == 0
    return WIDE_ROW_TILE


def _full_spec(shape):
    return pl.BlockSpec(shape, lambda *_: (0,) * len(shape))


ANY_SPEC = pl.BlockSpec(memory_space=pl.ANY)
PACK_SPEC = _full_spec((8, D_MODEL))


class Comm(NamedTuple):
    inputs: list
    out_shape: list
    aliases: dict
    scratch: list
    start: Callable
    finish: Callable


def _pallas(body, *, comm=None, edge=None, in_specs, out_specs, out_shape, scratch_shapes=(), **kw):
    if comm is None:
        return pl.pallas_call(body, in_specs=in_specs, out_specs=out_specs, out_shape=out_shape,
                              scratch_shapes=scratch_shapes, **kw)
    n_in, n_out, n_scr = len(in_specs), len(out_specs), len(scratch_shapes)
    c_in, c_out = len(comm.inputs), len(comm.out_shape)

    def both(*refs):
        ins, c_ins = refs[:n_in], refs[n_in:n_in + c_in]
        outs = refs[n_in + c_in:n_in + c_in + n_out]
        c_outs = refs[n_in + c_in + n_out:n_in + c_in + n_out + c_out]
        rest = refs[n_in + c_in + n_out + c_out:]
        scr, c_scr = rest[:n_scr], rest[n_scr:]
        first, last = edge()

        @pl.when(first)
        def _():
            comm.start(c_ins, c_outs, c_scr)

        body(*ins, *outs, *scr)

        @pl.when(last)
        def _():
            comm.finish(c_ins, c_outs, c_scr)

    call = pl.pallas_call(
        both, in_specs=list(in_specs) + [ANY_SPEC] * c_in, out_specs=list(out_specs) + [ANY_SPEC] * c_out,
        out_shape=list(out_shape) + list(comm.out_shape), scratch_shapes=list(scratch_shapes) + list(comm.scratch),
        input_output_aliases={n_in + a: n_out + b for a, b in comm.aliases.items()}, **kw)
    return lambda *args: call(*args, *comm.inputs)


def _grid_edge(steps):
    return lambda: (pl.program_id(0) == 0, pl.program_id(0) == steps - 1)


def in_proj_fwd(x, g_pre, w_in, comm=None):
    T = x.shape[0]
    pw = w_in.shape[2]

    def body(x_ref, g_ref, w_hbm, ph_ref, sqkv_ref, u_ref, w_vmem, proj_s):
        _load_once(w_hbm, w_vmem)
        xh, _ = _rms(x_ref[...])
        u = (xh * g_ref[...]).astype(BF16)
        u_ref[...] = u
        for q in range(N_CHIPS):
            proj_s[:, pw * q:pw * (q + 1)] = jnp.dot(u, w_vmem[q], preferred_element_type=F32)
        ph_ref[...] = proj_s[:, :4 * HG_WIDTH]
        sqkv_ref[:, :SB_WIDTH] = (proj_s[:, 4 * HG_WIDTH:4 * HG_WIDTH + SB_WIDTH] * SB_SCALE).astype(BF16)
        sqkv_ref[:, SB_WIDTH:] = proj_s[:, 4 * HG_WIDTH + SB_WIDTH:].astype(BF16)

    return _pallas(
        body, comm=comm, edge=_grid_edge(T // _wide_tile(T)), name="in_proj_fwd", grid=(T // _wide_tile(T),),
        in_specs=[_wide_spec(D_MODEL), _full_spec((1, D_MODEL)), ANY_SPEC],
        out_specs=[_wide_spec(4 * HG_WIDTH), _wide_spec(3 * SB_WIDTH), _wide_spec(D_MODEL)],
        out_shape=[jax.ShapeDtypeStruct((T, 4 * HG_WIDTH), F32), jax.ShapeDtypeStruct((T, 3 * SB_WIDTH), BF16),
                   jax.ShapeDtypeStruct((T, D_MODEL), BF16)],
        scratch_shapes=[pltpu.VMEM(w_in.shape, BF16), pltpu.VMEM((_wide_tile(T), N_CHIPS * pw), F32)],
        compiler_params=_params(("arbitrary",)),
    )(x, g_pre, w_in)


def _hg_sum_matrix():
    C = HG_CHUNK
    t = np.arange(C)[:, None]
    j = np.arange(C)[None, :]
    mats = [j <= t, j > t]
    for h in HG_LEVELS:
        start = (t // (2 * h)) * (2 * h)
        upper = (t & h) != 0
        mats.append(np.where(upper, (j >= start + h) & (j <= t), (j > t) & (j <= start + h - 1)))
    return np.concatenate(mats, 0).astype(np.float32)


def _hg_level_masks():
    C = HG_CHUNK
    t = lax.broadcasted_iota(jnp.int32, (C, C), 0)
    s = lax.broadcasted_iota(jnp.int32, (C, C), 1)
    x = t ^ s
    masks = [t == s]
    for h in HG_LEVELS:
        masks.append((x >= h) & (x < 2 * h) & (t > s))
    return masks


def _hg_gates(hq, hf, gamma):
    lb = 1.0 / (1.0 + jnp.exp(gamma[1:2, :] - gamma[0:1, :]))
    sq = _sigmoid(hq)
    q = hq * sq
    sig = _sigmoid(hf)
    nsig = _sigmoid(-hf)
    f = lb + (1.0 - lb) * sig
    k = (1.0 - lb) * nsig
    g = jnp.log(f)
    return q, k, g, dict(lb=lb, sq=sq, sig=sig, nsig=nsig, f=f)


def _hg_head_decays(A, h):
    C, K = HG_CHUNK, HG_DK
    sl = slice(K * h, K * (h + 1))
    blocks = [A[C * r:C * (r + 1), sl] for r in range(2 + len(HG_LEVELS))]
    return blocks[0], blocks[1], [None] + blocks[2:]


def _hg_products(q, k, levels):
    return [_dot_nt(q, k)] + [_dot_nt(q * a, k * a) for a in levels[1:]]


def _hg_select(prods, masks):
    sc = jnp.where(masks[0], prods[0], 0.0)
    for p, m in zip(prods[1:], masks[1:]):
        sc = jnp.where(m, p, sc)
    return sc


def hgrn2_fwd(proj_h, gamma, comm=None):
    T = proj_h.shape[0]
    C, K, H, S = HG_CHUNK, HG_DK, HG_HEADS, HG_CHUNKS_PER_STEP
    n_steps = T // (S * C)
    msum = jnp.asarray(_hg_sum_matrix(), BF16)

    def body(hq_ref, hf_ref, hi_ref, gam_ref, msum_ref, o_ref, st_ref, st_s):
        _zero_first(st_s)
        q, k, g, _ = _hg_gates(hq_ref[...], hf_ref[...], gam_ref[...])
        v = hi_ref[...]
        masks = _hg_level_masks()
        parts = []
        for s in range(S):
            rows = slice(C * s, C * (s + 1))
            A = jnp.exp(_sum01_left(msum_ref[...], g[rows]))
            for h in range(H):
                sl = slice(K * h, K * (h + 1))
                ab, ar, levels = _hg_head_decays(A, h)
                parts.append(dict(s=s, h=h, rows=rows, sl=sl, ab=ab, ar=ar, levels=levels,
                                  q=q[rows, sl], k=k[rows, sl], v=v[rows, sl]))
        for pt in parts:
            pt["prods"] = _hg_products(pt["q"], pt["k"], pt["levels"])
            pt["grown"] = _dot_tn(pt["v"], pt["k"] * pt["ar"])
        for pt in parts:
            pt["sc"] = _hg_select(pt["prods"], masks)
        state = [st_s[h] for h in range(H)]
        for pt in parts:
            h, ab = pt["h"], pt["ab"]
            o_ref[pt["rows"], pt["sl"]] = _dot_nt(pt["q"] * ab, state[h]) + _dot(pt["sc"], pt["v"])
            state[h] = state[h] * ab[C - 1:C, :] + pt["grown"]
            st_ref[pt["s"], h] = state[h]
        for h in range(H):
            st_s[h] = state[h]

    blk = lambda col: pl.BlockSpec((S * C, HG_WIDTH), lambda c, col=col: (c, col))
    return _pallas(
        body, comm=comm, edge=_grid_edge(n_steps), name="hgrn2_fwd", grid=(n_steps,),
        in_specs=[blk(0), blk(1), blk(2), _full_spec((2, HG_WIDTH)), _full_spec(msum.shape)],
        out_specs=[blk(0), pl.BlockSpec((S, H, K, K), lambda c: (c, 0, 0, 0))],
        out_shape=[jax.ShapeDtypeStruct((T, HG_WIDTH), F32), jax.ShapeDtypeStruct((S * n_steps, H, K, K), F32)],
        scratch_shapes=[pltpu.VMEM((H, K, K), F32)],
        compiler_params=_params(("arbitrary",)),
    )(proj_h, proj_h, proj_h, gamma, msum)


def hgrn2_bwd(proj_h, gamma, states, do, comm=None):
    T = proj_h.shape[0]
    C, K, H, S = HG_CHUNK, HG_DK, HG_HEADS, HG_CHUNKS_PER_STEP_BWD
    n_steps = T // (S * C)
    n_sums = 2 + len(HG_LEVELS)
    msum = jnp.asarray(_hg_sum_matrix(), BF16)
    msum_t = jnp.asarray(_hg_sum_matrix().T, BF16)

    def body(hq_ref, hf_ref, hi_ref, do_ref, gam_ref, msum_ref, msum_t_ref, st_prev_ref, st_ref,
             dhq_ref, dhf_ref, dhi_ref, pack_ref, dst_s, dlb_s, dq_s, dk_s, de_s):
        step = pl.program_id(0)
        _zero_first(dst_s)
        _zero_first(dlb_s)
        _zero_first(pack_ref)
        hq = hq_ref[...]
        q, k, g, aux = _hg_gates(hq, hf_ref[...], gam_ref[...])
        v = hi_ref[...]
        do_all = do_ref[...]
        masks = _hg_level_masks()
        is_last_row = lax.broadcasted_iota(jnp.int32, (C, K), 0) == C - 1
        has_prev = (step < n_steps - 1).astype(F32)
        parts = []
        for s in reversed(range(S)):
            rows = slice(C * s, C * (s + 1))
            A = jnp.exp(_sum01_left(msum_ref[...], g[rows]))
            for h in range(H):
                sl = slice(K * h, K * (h + 1))
                ab, ar, levels = _hg_head_decays(A, h)
                st_in = st_prev_ref[0, h] * has_prev if s == 0 else st_ref[s - 1, h]
                parts.append(dict(s=s, h=h, rows=rows, sl=sl, ab=ab, ar=ar, levels=levels, st_in=st_in,
                                  q=q[rows, sl], k=k[rows, sl], v=v[rows, sl], do=do_all[rows, sl]))
        for pt in parts:
            pt["prods"] = _hg_products(pt["q"], pt["k"], pt["levels"])
            pt["da"] = _dot_nt(pt["do"], pt["v"])
            pt["t1"] = pt["ab"] * _dot(pt["do"], pt["st_in"])
            pt["dst_add"] = _dot_tn(pt["do"], pt["q"] * pt["ab"])
        dstate = [dst_s[h] for h in range(H)]
        for pt in parts:
            h = pt["h"]
            pt["dst_out"] = dstate[h]
            pt["t2"] = pt["ar"] * _dot(pt["v"], dstate[h])
            pt["dv_state"] = _dot_nt(pt["k"] * pt["ar"], dstate[h])
            dstate[h] = dstate[h] * pt["ab"][C - 1:C, :] + pt["dst_add"]
        for h in range(H):
            dst_s[h] = dstate[h]
        for pt in parts:
            pt["sc"] = _hg_select(pt["prods"], masks)
            pt["dam"] = [jnp.where(m, pt["da"], 0.0) for m in masks]
        for pt in parts:
            qh, kh = pt["q"], pt["k"]
            pt["dq_parts"] = [_dot(pt["dam"][0], kh)] + [
                a * _dot(dam, kh * a) for a, dam in zip(pt["levels"][1:], pt["dam"][1:])]
            pt["dk_parts"] = [_dot_tn(pt["dam"][0], qh)] + [
                a * _dot_tn(dam, qh * a) for a, dam in zip(pt["levels"][1:], pt["dam"][1:])]
            pt["dv_intra"] = _dot_tn(pt["sc"], pt["do"])
        for pt in parts:
            s, rows, sl, qh, kh, ab = pt["s"], pt["rows"], pt["sl"], pt["q"], pt["k"], pt["ab"]
            decayed = _colsum(pt["st_in"] * pt["dst_out"]) * ab[C - 1:C, :]
            de_s[s, 0:C, sl] = qh * pt["t1"] + jnp.where(is_last_row, decayed, 0.0)
            de_s[s, C:2 * C, sl] = kh * pt["t2"]
            dq = pt["t1"] + pt["dq_parts"][0]
            dk = pt["t2"] + pt["dk_parts"][0]
            for r, (t1, t2) in enumerate(zip(pt["dq_parts"][1:], pt["dk_parts"][1:])):
                dq = dq + t1
                dk = dk + t2
                de_s[s, C * (r + 2):C * (r + 3), sl] = qh * t1 + kh * t2
            dhi_ref[rows, sl] = pt["dv_intra"] + pt["dv_state"]
            dq_s[rows, sl] = dq
            dk_s[rows, sl] = dk
        dg = jnp.concatenate([_sum01_left(msum_t_ref[...], de_s[s]) for s in range(S)], axis=0)
        dk = dk_s[...]
        sq, lb = aux["sq"], aux["lb"]
        dhq_ref[...] = dq_s[...] * (sq * (1.0 + hq * (1.0 - sq)))
        common = dg / aux["f"] - dk
        dhf_ref[...] = (1.0 - lb) * aux["sig"] * aux["nsig"] * common
        dlb_s[...] += _colsum(aux["nsig"] * common)

        @pl.when(step == n_steps - 1)
        def _():
            dgam = lb * (1.0 - lb) * dlb_s[...]
            pack_ref[ROW_GAMMA:ROW_GAMMA + 1, :HG_WIDTH] = dgam
            pack_ref[ROW_GAMMA:ROW_GAMMA + 1, HG_WIDTH:] = -dgam

    last = n_steps - 1
    blk = lambda col: pl.BlockSpec((S * C, HG_WIDTH), lambda c, col=col: (last - c, col))
    return _pallas(
        body, comm=comm, edge=_grid_edge(n_steps), name="hgrn2_bwd", grid=(n_steps,),
        in_specs=[blk(0), blk(1), blk(2), blk(0), _full_spec((2, HG_WIDTH)), _full_spec(msum.shape),
                  _full_spec(msum_t.shape),
                  pl.BlockSpec((1, H, K, K), lambda c: (jnp.maximum(S * (last - c) - 1, 0), 0, 0, 0)),
                  pl.BlockSpec((S, H, K, K), lambda c: (last - c, 0, 0, 0))],
        out_specs=[blk(0), blk(0), blk(0), PACK_SPEC],
        out_shape=[jax.ShapeDtypeStruct((T, HG_WIDTH), F32)] * 3 + [jax.ShapeDtypeStruct((8, D_MODEL), F32)],
        scratch_shapes=[pltpu.VMEM((H, K, K), F32), pltpu.VMEM((1, HG_WIDTH), F32), pltpu.VMEM((S * C, HG_WIDTH), F32),
                        pltpu.VMEM((S * C, HG_WIDTH), F32), pltpu.VMEM((S, n_sums * C, HG_WIDTH), F32)],
        compiler_params=_params(("arbitrary",)),
    )(proj_h, proj_h, proj_h, do, gamma, msum, msum_t, states, states)


def _sb_sum_matrix(inclusive):
    B = SB_BLOCK
    j = np.arange(B)[:, None]
    s = np.arange(B)[None, :]
    tri = (j >= s) if inclusive else (j > s)
    once = np.concatenate([tri, np.ones((B, B), bool)], 1).astype(np.float32)
    return np.concatenate([once, once], 0)


def _sb_prefix_matrix(inclusive):
    B = SB_BLOCK
    j = np.arange(B)[:, None]
    s = np.arange(B)[None, :]
    tri = (j <= s) if inclusive else (j < s)
    once = np.concatenate([tri, np.ones((B, B), bool)], 1).astype(np.float32)
    return np.concatenate([once, once], 0)


def _sb_iotas():
    shape = (SB_BLOCK, SB_BLOCK)
    return lax.broadcasted_iota(jnp.int32, shape, 0), lax.broadcasted_iota(jnp.int32, shape, 1)


def _sb_heads(q, first):
    heads = []
    for g in range(SB_GROUP):
        qg = q[:, SB_BLOCK * g:SB_BLOCK * (g + 1)]
        zero = jnp.zeros_like(qg)
        heads += [(g, jnp.where(first, qg, zero)), (g, jnp.where(first, zero, qg))]
    return heads


def _lanes(x, g):
    return x[:, SB_BLOCK * g:SB_BLOCK * (g + 1)]


def sb_fwd(sqkv, comm=None):
    T = sqkv.shape[0]
    B = SB_BLOCK
    W = SB_GROUP * B
    groups = SB_WIDTH // W
    usum = jnp.asarray(_sb_sum_matrix(False), BF16)

    def body(q_ref, k_ref, v_ref, u_ref, o_ref, tl_ref, first_ref):
        p, i = pl.program_id(0), pl.program_id(1)
        row, lane = _sb_iotas()
        first = lane < SB_DH
        heads = _sb_heads(q_ref[...], first)
        u = u_ref[...]

        def more(loop):
            done, reachable, _ = loop
            return (done <= i) & (reachable > 0)

        def walk(count, loop):
            done, _, state = loop
            blocks = []
            for sub in range(count):
                j = i - done - sub
                off = pl.multiple_of(jnp.maximum(j, 0) * B, B)
                valid = ((lane + j * B) < (row + i * B)) & (j >= 0)
                blocks.append((k_ref[pl.ds(off, B), :], v_ref[pl.ds(off, B), :], valid))
            z = [[_dot_nt(qh, _lanes(kj, g)) for g, qh in heads] for kj, _, _ in blocks]
            lnb = [[jnp.where(valid, _neg_softplus(zz), 0.0) for zz in zs] for zs, (_, _, valid) in zip(z, blocks)]
            sums = [[_sum01_right(x, u) for x in xs] for xs in lnb]
            out = []
            for h, (carry, acc) in enumerate(state):
                for sub, (_, vj, valid) in enumerate(blocks):
                    expo = z[sub][h] + lnb[sub][h] + carry + sums[sub][h][:, :B]
                    acc = acc + _dot(jnp.where(valid, jnp.exp(expo), 0.0), _lanes(vj, heads[h][0]))
                    carry = carry + sums[sub][h][:, B:]
                out.append((carry, acc))
            state = tuple(out)
            worst = state[0][0]
            for carry, _ in state[1:]:
                worst = jnp.maximum(worst, carry)
            reachable = (jnp.max(worst) > SB_UNDERFLOW_LOG).astype(jnp.int32)
            return done + count, reachable, state

        zero = jnp.zeros((B, B), F32)
        loop = walk(SB_UNROLL, (jnp.int32(0), jnp.int32(1), tuple((zero, zero) for _ in heads)))
        done, _, state = lax.while_loop(more, functools.partial(walk, 1), loop)
        for g in range(SB_GROUP):
            (tot0, acc0), (tot1, acc1) = state[2 * g], state[2 * g + 1]
            o_ref[:, B * g:B * (g + 1)] = jnp.where(first, acc0, acc1)
            tl_ref[:, B * g:B * (g + 1)] = jnp.where(first, tot0, tot1)
        first_ref[p, i] = jnp.maximum(i + 1 - done, 0)

    def edge():
        p, i = pl.program_id(0), pl.program_id(1)
        return (p == 0) & (i == 0), (p == groups - 1) & (i == T // B - 1)

    return _pallas(
        body, comm=comm, edge=edge, name="sb_fwd", grid=(groups, T // B),
        in_specs=[pl.BlockSpec((B, W), lambda p, i: (i, p)),
                  pl.BlockSpec((T, W), lambda p, i: (0, groups + p)),
                  pl.BlockSpec((T, W), lambda p, i: (0, 2 * groups + p)),
                  pl.BlockSpec(usum.shape, lambda p, i: (0, 0))],
        out_specs=[pl.BlockSpec((B, W), lambda p, i: (i, p))] * 2 + [pl.BlockSpec(memory_space=pltpu.SMEM)],
        out_shape=[jax.ShapeDtypeStruct((T, SB_WIDTH), F32)] * 2 + [jax.ShapeDtypeStruct((groups, T // B), jnp.int32)],
        compiler_params=_params(("arbitrary", "arbitrary")),
    )(sqkv, sqkv, sqkv, usum)


def sb_bwd(sqkv, do, tl, first_block):
    T = sqkv.shape[0]
    B = SB_BLOCK
    W = SB_GROUP * B
    groups = SB_WIDTH // W
    upre = jnp.asarray(_sb_prefix_matrix(True), BF16)
    uexc = jnp.asarray(_sb_prefix_matrix(False), BF16)

    def body(q_ref, k_ref, v_ref, do_ref, tl_ref, up_ref, ue_ref, first_ref, dq_ref, dk_ref, dv_ref):
        p, i = pl.program_id(0), pl.program_id(1)

        @pl.when(i == 0)
        def _():
            dk_ref[...] = jnp.zeros(dk_ref.shape, F32)
            dv_ref[...] = jnp.zeros(dv_ref.shape, F32)

        row, lane = _sb_iotas()
        first = lane < SB_DH
        do = do_ref[...]
        tl_all = tl_ref[...]
        heads = []
        for (g, qh), at in zip(_sb_heads(q_ref[...], first), (0, B - 1) * SB_GROUP):
            dog = _lanes(do, g)
            keep = first if at == 0 else jnp.logical_not(first)
            heads.append((g, qh, jnp.where(keep, dog, jnp.zeros_like(dog)).astype(BF16),
                          _lanes(tl_all, g)[:, at:at + 1]))
        up = up_ref[...]
        ue = ue_ref[...]
        start = first_ref[p, i]

        def walk(count, j0, state):
            blocks = []
            for sub in range(count):
                j = j0 + sub
                off = pl.multiple_of(jnp.minimum(j, i) * B, B)
                valid = (lane + j * B) < (row + i * B)
                blocks.append((off, k_ref[pl.ds(off, B), :], v_ref[pl.ds(off, B), :], valid))
            combos = [(s, h) for s in range(count) for h in range(len(heads))]
            z = {(s, h): _dot_nt(heads[h][1], _lanes(blocks[s][1], heads[h][0])) for s, h in combos}
            da = {(s, h): _dot_nt(heads[h][2], _lanes(blocks[s][2], heads[h][0])) for s, h in combos}
            lnb = {c: jnp.where(blocks[c[0]][3], _neg_softplus(z[c]), 0.0) for c in combos}
            lb = {c: z[c] + lnb[c] for c in combos}
            sums = {c: _sum01_right(lnb[c], up) for c in combos}
            a, w = {}, {}
            seen = [st[0] for st in state]
            for s, h in combos:
                expo = lb[s, h] + (heads[h][3] - seen[h] - sums[s, h][:, :B])
                a[s, h] = jnp.where(blocks[s][3], jnp.exp(expo), 0.0)
                w[s, h] = a[s, h] * da[s, h]
                seen[h] = seen[h] + sums[s, h][:, B:]
            wsums = {c: _sum01_right(w[c], ue) for c in combos}
            dz = {}
            seen_w = [st[1] for st in state]
            for s, h in combos:
                beta = jnp.exp(lb[s, h])
                before = seen_w[h] + wsums[s, h][:, :B]
                dz[s, h] = jnp.where(blocks[s][3], w[s, h] * (1.0 - beta) - before * beta, 0.0)
                seen_w[h] = seen_w[h] + wsums[s, h][:, B:]
            dq = [st[2] for st in state]
            for s, h in combos:
                dq[h] = dq[h] + _dot(dz[s, h], _lanes(blocks[s][1], heads[h][0]))
            for s in range(count):
                off = blocks[s][0]
                for g in range(SB_GROUP):
                    h0, h1 = 2 * g, 2 * g + 1
                    dk_ref[pl.ds(off, B), B * g:B * (g + 1)] += (_dot_tn(dz[s, h0], heads[h0][1])
                                                                 + _dot_tn(dz[s, h1], heads[h1][1]))
                    dv_ref[pl.ds(off, B), B * g:B * (g + 1)] += (_dot_tn(a[s, h0], heads[h0][2])
                                                                 + _dot_tn(a[s, h1], heads[h1][2]))
            return tuple(zip(seen, seen_w, dq))

        zero = jnp.zeros((B, B), F32)
        state = walk(SB_UNROLL, start, tuple((zero, zero, zero) for _ in heads))
        state = lax.fori_loop(start + SB_UNROLL, i + 1, functools.partial(walk, 1), state)
        for g in range(SB_GROUP):
            dq_ref[:, B * g:B * (g + 1)] = jnp.where(first, state[2 * g][2], state[2 * g + 1][2]) * SB_SCALE

    qblk = pl.BlockSpec((B, W), lambda p, i: (i, p))
    full = pl.BlockSpec((T, W), lambda p, i: (0, p))
    return pl.pallas_call(
        body, name="sb_bwd", grid=(groups, T // B),
        in_specs=[qblk, pl.BlockSpec((T, W), lambda p, i: (0, groups + p)),
                  pl.BlockSpec((T, W), lambda p, i: (0, 2 * groups + p)), qblk, qblk,
                  pl.BlockSpec(upre.shape, lambda p, i: (0, 0)), pl.BlockSpec(uexc.shape, lambda p, i: (0, 0)),
                  pl.BlockSpec(memory_space=pltpu.SMEM)],
        out_specs=[qblk, full, full],
        out_shape=[jax.ShapeDtypeStruct((T, SB_WIDTH), F32)] * 3,
        compiler_params=_params(("arbitrary", "arbitrary")),
    )(sqkv, sqkv, sqkv, do, tl, upre, uexc, first_block)


def _mixer_out(o_hg, hg, o_sb, g_hg, g_sb):
    n_hg, r_hg = _rms(o_hg)
    s_hg = _sigmoid(hg)
    n_sb, r_sb = _rms(o_sb)
    return dict(n_hg=n_hg, r_hg=r_hg, s_hg=s_hg, n_sb=n_sb, r_sb=r_sb,
                y_hg=n_hg * g_hg * (hg * s_hg), y_sb=n_sb * g_sb)


def mix_out_fwd(o_hg, proj_h, o_sb, x, norms, g_post, w_out, comm=None):
    T = x.shape[0]

    def body(ohg_ref, hg_ref, osb_ref, x_ref, nrm_ref, gp_ref, w_hbm, cat_ref, mix_ref, h1_ref, w_vmem):
        _load_once(w_hbm, w_vmem)
        nrm = nrm_ref[...]
        m = _mixer_out(ohg_ref[...], hg_ref[...], osb_ref[...], nrm[:, :HG_WIDTH], nrm[:, HG_WIDTH:])
        cat_ref[:, :HG_WIDTH] = m["y_hg"].astype(BF16)
        cat_ref[:, HG_WIDTH:] = m["y_sb"].astype(BF16)
        mix = jnp.dot(cat_ref[...], w_vmem[...], preferred_element_type=F32)
        mix_ref[...] = mix
        mh, _ = _rms(mix)
        h1_ref[...] = x_ref[...] + mh * gp_ref[...]

    return _pallas(
        body, comm=comm, edge=_grid_edge(T // _wide_tile(T)), name="mix_out_fwd", grid=(T // _wide_tile(T),),
        in_specs=[_wide_spec(HG_WIDTH), _wide_spec(HG_WIDTH, 3), _wide_spec(SB_WIDTH), _wide_spec(D_MODEL),
                  _full_spec((1, D_MODEL)), _full_spec((1, D_MODEL)), ANY_SPEC],
        out_specs=[_wide_spec(D_MODEL)] * 3,
        out_shape=[jax.ShapeDtypeStruct((T, D_MODEL), BF16), jax.ShapeDtypeStruct((T, D_MODEL), F32),
                   jax.ShapeDtypeStruct((T, D_MODEL), F32)],
        scratch_shapes=[pltpu.VMEM(w_out.shape, BF16)],
        compiler_params=_params(("arbitrary",)),
    )(o_hg, proj_h, o_sb, x, norms, g_post, w_out)


def ffn_fwd(h1, g_pre, g_post, w_gu, w_down):
    T = h1.shape[0]
    pw = w_gu.shape[2]

    def body(h1_ref, gpre_ref, gpost_ref, wgu_hbm, wd_hbm, u2_ref, gu_ref, act_ref, y_ref, h2_ref,
             wgu_vmem, wd_vmem, gu_s):
        _load_once(wgu_hbm, wgu_vmem)
        _load_once(wd_hbm, wd_vmem)
        h1v = h1_ref[...]
        hh, _ = _rms(h1v)
        u2 = (hh * gpre_ref[...]).astype(BF16)
        u2_ref[...] = u2
        for q in range(N_CHIPS):
            gu_s[:, pw * q:pw * (q + 1)] = jnp.dot(u2, wgu_vmem[q], preferred_element_type=F32)
        gu_ref[...] = gu_s[...].astype(BF16)
        gate = gu_s[:, :D_FF]
        act = (gate * _sigmoid(gate) * gu_s[:, D_FF:]).astype(BF16)
        act_ref[...] = act
        y = jnp.dot(act, wd_vmem[...], preferred_element_type=F32)
        y_ref[...] = y
        yh, _ = _rms(y)
        h2_ref[...] = h1v + yh * gpost_ref[...]

    return pl.pallas_call(
        body, name="ffn_fwd", grid=(T // ROW_TILE,),
        in_specs=[_row_spec(D_MODEL), _full_spec((1, D_MODEL)), _full_spec((1, D_MODEL)), ANY_SPEC, ANY_SPEC],
        out_specs=[_row_spec(D_MODEL), _row_spec(2 * D_FF), _row_spec(D_FF), _row_spec(D_MODEL), _row_spec(D_MODEL)],
        out_shape=[jax.ShapeDtypeStruct((T, D_MODEL), BF16), jax.ShapeDtypeStruct((T, 2 * D_FF), BF16),
                   jax.ShapeDtypeStruct((T, D_FF), BF16), jax.ShapeDtypeStruct((T, D_MODEL), F32),
                   jax.ShapeDtypeStruct((T, D_MODEL), F32)],
        scratch_shapes=[pltpu.VMEM(w_gu.shape, BF16), pltpu.VMEM(w_down.shape, BF16),
                        pltpu.VMEM((ROW_TILE, 2 * D_FF), F32)],
        compiler_params=_params(("arbitrary",)),
    )(h1, g_pre, g_post, w_gu, w_down)


def ple_loss(h2, p, target, w_ple, w_pg):
    T = h2.shape[0]
    pw = w_ple.shape[2]

    def body(h2_ref, p_ref, t_ref, wple_hbm, wpg_hbm, de_ref, ds_ref, dh2_ref, h2b_ref, pb_ref, pack_ref,
             wple_vmem, wpg_vmem, e_s):
        _load_once(wple_hbm, wple_vmem)
        _load_once(wpg_hbm, wpg_vmem)
        _zero_first(pack_ref)
        h2v = h2_ref[...]
        h2b = h2v.astype(BF16)
        h2b_ref[...] = h2b
        pb = p_ref[...].astype(BF16)
        pb_ref[...] = pb
        for q in range(N_CHIPS):
            e_s[:, pw * q:pw * (q + 1)] = jnp.dot(pb, wple_vmem[q], preferred_element_type=F32)
        e = e_s[...]
        sig = _sigmoid(jnp.dot(h2b, wpg_vmem[...], preferred_element_type=F32))
        err = h2v + e * sig - t_ref[...]
        part = 0.5 * jnp.sum(jnp.mean(err * err, axis=-1, keepdims=True), axis=0, keepdims=True)
        lane = lax.broadcasted_iota(jnp.int32, (1, D_MODEL), 1)
        pack_ref[ROW_LOSS:ROW_LOSS + 1, :] += jnp.where(lane == 0, part, 0.0)
        dh3 = err * (1.0 / D_MODEL)
        de_ref[...] = (dh3 * sig).astype(BF16)
        ds = (dh3 * e * sig * (1.0 - sig)).astype(BF16)
        ds_ref[...] = ds
        dh2_ref[...] = dh3 + _dot_nt(ds, wpg_vmem[...])

    return pl.pallas_call(
        body, name="ple_loss", grid=(T // _wide_tile(T),),
        in_specs=[_wide_spec(D_MODEL), _wide_spec(p.shape[1]), _wide_spec(D_MODEL), ANY_SPEC, ANY_SPEC],
        out_specs=[_wide_spec(D_MODEL), _wide_spec(D_MODEL), _wide_spec(D_MODEL), _wide_spec(D_MODEL),
                   _wide_spec(p.shape[1]), PACK_SPEC],
        out_shape=[jax.ShapeDtypeStruct((T, D_MODEL), BF16), jax.ShapeDtypeStruct((T, D_MODEL), BF16),
                   jax.ShapeDtypeStruct((T, D_MODEL), F32), jax.ShapeDtypeStruct((T, D_MODEL), BF16),
                   jax.ShapeDtypeStruct(p.shape, BF16), jax.ShapeDtypeStruct((8, D_MODEL), F32)],
        scratch_shapes=[pltpu.VMEM(w_ple.shape, BF16), pltpu.VMEM(w_pg.shape, BF16), pltpu.VMEM((_wide_tile(T), D_MODEL), F32)],
        compiler_params=_params(("arbitrary",)),
    )(h2, p, target, w_ple, w_pg)


def ffn_bwd(dh2, y, h1, gu, g_pre, g_post, w_gu, w_down):
    T = h1.shape[0]
    pw = w_gu.shape[2]

    def body(dh2_ref, y_ref, h1_ref, gu_ref, gpre_ref, gpost_ref, wgu_hbm, wd_hbm, dy_ref, dgu_ref, dh1_ref, pack_ref,
             wgu_vmem, wd_vmem):
        _load_once(wgu_hbm, wgu_vmem)
        _load_once(wd_hbm, wd_vmem)
        _zero_first(pack_ref)
        dh2v = dh2_ref[...]
        yh, ry = _rms(y_ref[...])
        dy, dw = _rms_bwd(dh2v, yh, ry, gpost_ref[...])
        pack_ref[ROW_FFN_POST:ROW_FFN_POST + 1, :] += _colsum(dw)
        dyb = dy.astype(BF16)
        dy_ref[...] = dyb
        dact = _dot_nt(dyb, wd_vmem[...])
        gate = gu_ref[:, :D_FF].astype(F32)
        up = gu_ref[:, D_FF:].astype(F32)
        sg = _sigmoid(gate)
        dgu_ref[:, :D_FF] = (dact * up * (sg * (1.0 + gate * (1.0 - sg)))).astype(BF16)
        dgu_ref[:, D_FF:] = (dact * gate * sg).astype(BF16)
        du2 = _dot_nt(dgu_ref[:, :pw], wgu_vmem[0])
        for q in range(1, N_CHIPS):
            du2 = du2 + _dot_nt(dgu_ref[:, pw * q:pw * (q + 1)], wgu_vmem[q])
        hh, rh = _rms(h1_ref[...])
        dh, dw = _rms_bwd(du2, hh, rh, gpre_ref[...])
        pack_ref[ROW_FFN_PRE:ROW_FFN_PRE + 1, :] += _colsum(dw)
        dh1_ref[...] = dh2v + dh

    return pl.pallas_call(
        body, name="ffn_bwd", grid=(T // ROW_TILE,),
        in_specs=[_row_spec(D_MODEL), _row_spec(D_MODEL), _row_spec(D_MODEL), _row_spec(2 * D_FF),
                  _full_spec((1, D_MODEL)), _full_spec((1, D_MODEL)), ANY_SPEC, ANY_SPEC],
        out_specs=[_row_spec(D_MODEL), _row_spec(2 * D_FF), _row_spec(D_MODEL), PACK_SPEC],
        out_shape=[jax.ShapeDtypeStruct((T, D_MODEL), BF16), jax.ShapeDtypeStruct((T, 2 * D_FF), BF16),
                   jax.ShapeDtypeStruct((T, D_MODEL), F32), jax.ShapeDtypeStruct((8, D_MODEL), F32)],
        scratch_shapes=[pltpu.VMEM(w_gu.shape, BF16), pltpu.VMEM(w_down.shape, BF16)],
        compiler_params=_params(("arbitrary",)),
    )(dh2, y, h1, gu, g_pre, g_post, w_gu, w_down)


def mix_out_bwd(dh1, mix, o_hg, proj_h, o_sb, norms, g_post, w_out, comm=None):
    T = dh1.shape[0]

    def body(dh1_ref, mix_ref, ohg_ref, hg_ref, osb_ref, nrm_ref, gp_ref, w_hbm, dmix_ref, dohg_ref, dhg_ref, dosb_ref,
             pack_ref, w_vmem):
        _load_once(w_hbm, w_vmem)
        _zero_first(pack_ref)
        mh, rm = _rms(mix_ref[...])
        dmix, dw = _rms_bwd(dh1_ref[...], mh, rm, gp_ref[...])
        pack_ref[ROW_ATTN_POST:ROW_ATTN_POST + 1, :] += _colsum(dw)
        dmb = dmix.astype(BF16)
        dmix_ref[...] = dmb
        dcat = _dot_nt(dmb, w_vmem[...])
        nrm = nrm_ref[...]
        g_hg, g_sb = nrm[:, :HG_WIDTH], nrm[:, HG_WIDTH:]
        hg = hg_ref[...]
        m = _mixer_out(ohg_ref[...], hg, osb_ref[...], g_hg, g_sb)
        d_hg = dcat[:, :HG_WIDTH]
        silu = hg * m["s_hg"]
        dhg_ref[...] = d_hg * (m["n_hg"] * g_hg) * (m["s_hg"] * (1.0 + hg * (1.0 - m["s_hg"])))
        dx, dw = _rms_bwd(d_hg * silu, m["n_hg"], m["r_hg"], g_hg)
        dohg_ref[...] = dx
        pack_ref[ROW_MIX_NORMS:ROW_MIX_NORMS + 1, :HG_WIDTH] += _colsum(dw)
        dx, dw = _rms_bwd(dcat[:, HG_WIDTH:], m["n_sb"], m["r_sb"], g_sb)
        dosb_ref[...] = dx
        pack_ref[ROW_MIX_NORMS:ROW_MIX_NORMS + 1, HG_WIDTH:] += _colsum(dw)

    return _pallas(
        body, comm=comm, edge=_grid_edge(T // _wide_tile(T)), name="mix_out_bwd", grid=(T // _wide_tile(T),),
        in_specs=[_wide_spec(D_MODEL), _wide_spec(D_MODEL), _wide_spec(HG_WIDTH), _wide_spec(HG_WIDTH, 3), _wide_spec(SB_WIDTH),
                  _full_spec((1, D_MODEL)), _full_spec((1, D_MODEL)), ANY_SPEC],
        out_specs=[_wide_spec(D_MODEL), _wide_spec(HG_WIDTH), _wide_spec(HG_WIDTH), _wide_spec(SB_WIDTH), PACK_SPEC],
        out_shape=[jax.ShapeDtypeStruct((T, D_MODEL), BF16), jax.ShapeDtypeStruct((T, HG_WIDTH), F32),
                   jax.ShapeDtypeStruct((T, HG_WIDTH), F32), jax.ShapeDtypeStruct((T, SB_WIDTH), F32),
                   jax.ShapeDtypeStruct((8, D_MODEL), F32)],
        scratch_shapes=[pltpu.VMEM(w_out.shape, BF16)],
        compiler_params=_params(("arbitrary",)),
    )(dh1, mix, o_hg, proj_h, o_sb, norms, g_post, w_out)


def in_proj_bwd(parts, x, dh1, g_pre, w_in, comm=None):
    T = x.shape[0]
    pw = w_in.shape[2]
    n_parts = len(parts)

    def body(*refs):
        part_refs = refs[:n_parts]
        x_ref, dh1_ref, g_ref, w_hbm, dproj_ref, dx_ref, pack_ref, w_vmem = refs[n_parts:]
        _load_once(w_hbm, w_vmem)
        _zero_first(pack_ref)
        for n, ref in enumerate(part_refs):
            dproj_ref[:, HG_WIDTH * n:HG_WIDTH * (n + 1)] = ref[...].astype(BF16)
        du = _dot_nt(dproj_ref[:, :pw], w_vmem[0])
        for q in range(1, N_CHIPS):
            du = du + _dot_nt(dproj_ref[:, pw * q:pw * (q + 1)], w_vmem[q])
        xh, r = _rms(x_ref[...])
        dx, dw = _rms_bwd(du, xh, r, g_ref[...])
        pack_ref[ROW_ATTN_PRE:ROW_ATTN_PRE + 1, :] += _colsum(dw)
        dx_ref[...] = dh1_ref[...] + dx

    return _pallas(
        body, comm=comm, edge=_grid_edge(T // _wide_tile(T)), name="in_proj_bwd", grid=(T // _wide_tile(T),),
        in_specs=[_wide_spec(HG_WIDTH)] * n_parts + [_wide_spec(D_MODEL), _wide_spec(D_MODEL), _full_spec((1, D_MODEL)), ANY_SPEC],
        out_specs=[_wide_spec(n_parts * HG_WIDTH), _wide_spec(D_MODEL), PACK_SPEC],
        out_shape=[jax.ShapeDtypeStruct((T, n_parts * HG_WIDTH), BF16), jax.ShapeDtypeStruct((T, D_MODEL), F32),
                   jax.ShapeDtypeStruct((8, D_MODEL), F32)],
        scratch_shapes=[pltpu.VMEM(w_in.shape, BF16)],
        compiler_params=_params(("arbitrary",)),
    )(*parts, x, dh1, g_pre, w_in)


def weight_grad(a, g, name, *, tm, tn, tk=512, col_pieces=False, comm=None):
    T, M = a.shape
    N = g.shape[1]
    tk = min(tk, T)
    steps = T // tk

    def body(a_ref, g_ref, o_ref):
        @pl.when(pl.program_id(2) == 0)
        def _():
            o_ref[...] = jnp.zeros(o_ref.shape, F32)

        o_ref[...] += _dot_tn(a_ref[...], g_ref[...]).reshape(o_ref.shape)

    if col_pieces:
        out_shape = jax.ShapeDtypeStruct((N // tn, M, tn), F32)
        out_spec = pl.BlockSpec((1, tm, tn), lambda i, j, k: (j, i, 0))
    else:
        out_shape = jax.ShapeDtypeStruct((M, N), F32)
        out_spec = pl.BlockSpec((tm, tn), lambda i, j, k: (i, j))
    grid = (M // tm, N // tn, steps)

    def edge():
        at = [pl.program_id(d) for d in range(3)]
        return ((at[0] == 0) & (at[1] == 0) & (at[2] == 0),
                (at[0] == grid[0] - 1) & (at[1] == grid[1] - 1) & (at[2] == grid[2] - 1))

    return _pallas(
        body, comm=comm, edge=edge, name=name, grid=grid,
        in_specs=[pl.BlockSpec((tk, tm), lambda i, j, k: (k, i)), pl.BlockSpec((tk, tn), lambda i, j, k: (k, j))],
        out_specs=[out_spec], out_shape=[out_shape],
        compiler_params=_params(("arbitrary", "arbitrary", "arbitrary")),
    )(a, g)


def _place():
    x, y, c = lax.axis_index("x"), lax.axis_index("y"), lax.axis_index("c")
    chips = [(1 - x, y), (x, 1 - y), (1 - x, 1 - y)]
    return x, y, c, chips


def _chip_index(cx, cy):
    return 2 * cx + cy


def _own_slot(piece, slots):
    me = _chip_index(lax.axis_index("x"), lax.axis_index("y"))
    landing = lax.empty((slots,) + piece.shape[1:], piece.dtype)
    return lax.dynamic_update_slice(landing, piece, (me,) + (0,) * (piece.ndim - 1))


def _rcopy(src, dst, send_sem, recv_sem, device):
    return pltpu.make_async_remote_copy(src_ref=src, dst_ref=dst, send_sem=send_sem, recv_sem=recv_sem,
                                        device_id=device, device_id_type=MESH)


def gather_weights(shards):
    n = len(shards)

    def body(*refs):
        ins, outs = refs[:n], refs[2 * n:3 * n]
        send_sems, recv_sems = refs[3 * n:]
        x, y, c, chips = _place()
        me = _chip_index(x, y)
        sibling = (x, y, 1 - c)

        def rows(w, core):
            half = ins[w].shape[0] // 2
            return pl.ds(core * half, half)

        sends = []
        for w in range(n):
            for j, chip in enumerate(chips):
                sends.append(_rcopy(ins[w].at[rows(w, c)], outs[w].at[me, rows(w, c)],
                                    send_sems.at[6 * w + j], recv_sems.at[6 * w + j], (*chip, c)))
        for cp in sends:
            cp.start()
        passed = []
        for w in range(n):
            for j, chip in enumerate(chips):
                block = outs[w].at[_chip_index(*chip), rows(w, c)]
                _rcopy(block, block, send_sems.at[6 * w + j], recv_sems.at[6 * w + j], (*chip, c)).wait_recv()
                cp = _rcopy(block, block, send_sems.at[6 * w + 3 + j], recv_sems.at[6 * w + 3 + j], sibling)
                cp.start()
                passed.append(cp)
        for w in range(n):
            for j, chip in enumerate(chips):
                block = outs[w].at[_chip_index(*chip), rows(w, 1 - c)]
                _rcopy(block, block, send_sems.at[6 * w + 3 + j], recv_sems.at[6 * w + 3 + j], sibling).wait_recv()
        for cp in sends + passed:
            cp.wait_send()

    filled = [_own_slot(s[None], N_CHIPS) for s in shards]
    return pl.pallas_call(
        body, name="gather_weights",
        in_specs=[ANY_SPEC] * (2 * n), out_specs=[ANY_SPEC] * n,
        out_shape=[jax.ShapeDtypeStruct(f.shape, f.dtype) for f in filled],
        input_output_aliases={n + w: w for w in range(n)},
        scratch_shapes=[pltpu.SemaphoreType.DMA((6 * n,)), pltpu.SemaphoreType.DMA((6 * n,))],
    )(*shards, *filled)


def _run_comm(comm, name):
    c_in, c_out = len(comm.inputs), len(comm.out_shape)

    def body(*refs):
        parts = refs[:c_in], refs[c_in:c_in + c_out], refs[c_in + c_out:]
        comm.start(*parts)
        comm.finish(*parts)

    return pl.pallas_call(
        body, name=name, in_specs=[ANY_SPEC] * c_in, out_specs=[ANY_SPEC] * c_out, out_shape=comm.out_shape,
        scratch_shapes=comm.scratch, input_output_aliases=comm.aliases)(*comm.inputs)


def _both(first, second):
    n_in, n_out, n_scr = len(first.inputs), len(first.out_shape), len(first.scratch)

    def split(ins, outs, scr):
        return (ins[:n_in], outs[:n_out], scr[:n_scr]), (ins[n_in:], outs[n_out:], scr[n_scr:])

    def start(*refs):
        a, b = split(*refs)
        first.start(*a)
        second.start(*b)

    def finish(*refs):
        a, b = split(*refs)
        first.finish(*a)
        second.finish(*b)

    aliases = dict(first.aliases)
    aliases.update({n_in + i: n_out + o for i, o in second.aliases.items()})
    return Comm(first.inputs + second.inputs, first.out_shape + second.out_shape, aliases,
                first.scratch + second.scratch, start, finish)


def _dma_sems(count):
    return [pltpu.SemaphoreType.DMA((count,)), pltpu.SemaphoreType.DMA((count,))]


def gather_over_ici(shards, which=(0, 1, 2), landing=None):
    n = len(shards)

    def copies(ins, outs, sems):
        send_sems, recv_sems = sems
        x, y, c, chips = _place()
        me = _chip_index(x, y)
        pairs = []
        for w in range(n):
            half = shards[w].shape[0] // 2
            rows = pl.ds(c * half, half)
            for j, chip in enumerate(chips):
                if j not in which:
                    continue
                k = 3 * w + j
                landed = outs[w].at[_chip_index(*chip), rows]
                pairs.append((_rcopy(ins[w].at[rows], outs[w].at[me, rows], send_sems.at[k], recv_sems.at[k], (*chip, c)),
                              _rcopy(landed, landed, send_sems.at[k], recv_sems.at[k], (*chip, c))))
        return pairs

    def start(*refs):
        for send, _ in copies(*refs):
            send.start()

    def finish(*refs):
        pairs = copies(*refs)
        for _, landed in pairs:
            landed.wait_recv()
        for send, _ in pairs:
            send.wait_send()

    filled = list(landing) if landing is not None else [_own_slot(s[None], N_CHIPS) for s in shards]
    return Comm(list(shards) + filled, [jax.ShapeDtypeStruct(f.shape, f.dtype) for f in filled],
                {n + w: w for w in range(n)}, _dma_sems(3 * n), start, finish)


def gather_over_d2d(landed):
    n = len(landed)

    def copies(ins, outs, sems):
        send_sems, recv_sems = sems
        x, y, c, chips = _place()
        sibling = (x, y, 1 - c)
        pairs = []
        for w in range(n):
            half = landed[w].shape[1] // 2
            for j, chip in enumerate(chips):
                k = 3 * w + j
                mine = outs[w].at[_chip_index(*chip), pl.ds(c * half, half)]
                theirs = outs[w].at[_chip_index(*chip), pl.ds((1 - c) * half, half)]
                pairs.append((_rcopy(mine, mine, send_sems.at[k], recv_sems.at[k], sibling),
                              _rcopy(theirs, theirs, send_sems.at[k], recv_sems.at[k], sibling)))
        return pairs

    def start(*refs):
        for send, _ in copies(*refs):
            send.start()

    def finish(*refs):
        pairs = copies(*refs)
        for _, arrived in pairs:
            arrived.wait_recv()
        for send, _ in pairs:
            send.wait_send()

    return Comm(list(landed), [jax.ShapeDtypeStruct(a.shape, a.dtype) for a in landed], {w: w for w in range(n)},
                _dma_sems(3 * n), start, finish)


def core_halves(grads):
    n = len(grads)

    def copies(ins, outs, sems):
        send_sems, recv_sems = sems
        x, y, c, _ = _place()
        out = []
        for w in range(n):
            half = grads[w].shape[1] // 2
            out.append(_rcopy(ins[w].at[:, pl.ds((1 - c) * half, half), :], outs[w],
                              send_sems.at[w], recv_sems.at[w], (x, y, 1 - c)))
        return out

    def start(*refs):
        for cp in copies(*refs):
            cp.start()

    def finish(*refs):
        for cp in copies(*refs):
            cp.wait()

    return Comm(list(grads), [jax.ShapeDtypeStruct((g.shape[0], g.shape[1] // 2, g.shape[2]), g.dtype) for g in grads],
                {}, _dma_sems(n), start, finish)


def chip_partials(pairs):
    partials = [p for p, _ in pairs]
    filled = [landing for _, landing in pairs]
    n = len(partials)

    def copies(ins, outs, sems):
        send_sems, recv_sems = sems
        x, y, c, chips = _place()
        me = _chip_index(x, y)
        pairs = []
        for w in range(n):
            for j, chip in enumerate(chips):
                k = 3 * w + j
                landed = outs[w].at[_chip_index(*chip)]
                pairs.append((_rcopy(ins[w].at[_chip_index(*chip)], outs[w].at[me], send_sems.at[k], recv_sems.at[k],
                                     (*chip, c)),
                              _rcopy(landed, landed, send_sems.at[k], recv_sems.at[k], (*chip, c))))
        return pairs

    def start(*refs):
        for send, _ in copies(*refs):
            send.start()

    def finish(*refs):
        pairs = copies(*refs)
        for _, landed in pairs:
            landed.wait_recv()
        for send, _ in pairs:
            send.wait_send()

    return Comm(list(partials) + filled, [jax.ShapeDtypeStruct(p.shape, p.dtype) for p in partials],
                {n + w: w for w in range(n)}, _dma_sems(3 * n), start, finish)


def join_core_halves(grads):
    n = len(grads)

    def body(*refs):
        outs = refs[n:2 * n]
        send_sems, recv_sems = refs[2 * n:]
        x, y, c, _ = _place()
        sibling = (x, y, 1 - c)
        copies = []
        for w in range(n):
            half = outs[w].shape[0] // 2
            mine = outs[w].at[pl.ds(c * half, half), :]
            copies.append(_rcopy(mine, mine, send_sems.at[w], recv_sems.at[w], sibling))
        for cp in copies:
            cp.start()
        for w in range(n):
            half = outs[w].shape[0] // 2
            theirs = outs[w].at[pl.ds((1 - c) * half, half), :]
            _rcopy(theirs, theirs, send_sems.at[w], recv_sems.at[w], sibling).wait_recv()
        for cp in copies:
            cp.wait_send()

    return pl.pallas_call(
        body, name="join_core_halves",
        in_specs=[ANY_SPEC] * n, out_specs=[ANY_SPEC] * n,
        out_shape=[jax.ShapeDtypeStruct(g.shape, g.dtype) for g in grads],
        input_output_aliases={w: w for w in range(n)},
        scratch_shapes=[pltpu.SemaphoreType.DMA((n,)), pltpu.SemaphoreType.DMA((n,))],
    )(*grads)


def _elementwise_rows(rows, cap=512):
    for t in range(min(rows, cap), 0, -8):
        if rows % t == 0 and t % 16 == 0:
            return t
    return rows


def add_core_halves(grad, got, core, name):
    _, rows, cols = got.shape
    tr = _elementwise_rows(rows)
    nt = rows // tr

    def body(core_ref, a_ref, b_ref, o_ref, landing_ref):
        o_ref[...] = (a_ref[...] + b_ref[...]).astype(BF16)
        landing_ref[...] = o_ref[...]

    spec = pl.BlockSpec((1, tr, cols), lambda q, i, core_ref: (q, i, 0))
    own = pl.BlockSpec((1, tr, cols), lambda q, i, core_ref: (q, core_ref[0] * nt + i, 0))
    return pl.pallas_call(
        body, name=name,
        grid_spec=pltpu.PrefetchScalarGridSpec(num_scalar_prefetch=1, grid=(N_CHIPS, nt), in_specs=[own, spec],
                                               out_specs=[spec, spec]),
        out_shape=[jax.ShapeDtypeStruct(got.shape, BF16)] * 2,
        compiler_params=_params(("arbitrary", "arbitrary")),
    )(core, grad, got)


def add_chip_partials(parts, core, name):
    _, rows, cols = parts.shape
    tr = _elementwise_rows(rows)
    nt = rows // tr

    def body(core_ref, p_ref, o_ref):
        acc = p_ref[0].astype(F32)
        for q in range(1, N_CHIPS):
            acc = acc + p_ref[q].astype(F32)
        o_ref[...] = acc

    return pl.pallas_call(
        body, name=name,
        grid_spec=pltpu.PrefetchScalarGridSpec(
            num_scalar_prefetch=1, grid=(nt,),
            in_specs=[pl.BlockSpec((N_CHIPS, tr, cols), lambda i, core_ref: (0, i, 0))],
            out_specs=pl.BlockSpec((tr, cols), lambda i, core_ref: (core_ref[0] * nt + i, 0))),
        out_shape=jax.ShapeDtypeStruct((2 * rows, cols), F32),
        compiler_params=_params(("arbitrary",)),
    )(core, parts)


def _adamw_math(w, g, m, v):
    m = ADAM_B1 * m + (1.0 - ADAM_B1) * g
    v = ADAM_B2 * v + (1.0 - ADAM_B2) * (g * g)
    m_hat = m / (1.0 - ADAM_B1 ** ADAM_STEP)
    v_hat = v / (1.0 - ADAM_B2 ** ADAM_STEP)
    delta = -ADAM_LR * (m_hat / (jnp.sqrt(v_hat) + ADAM_EPS) + ADAM_WD * w)
    return delta, m, v


def adamw(w, g, m, v, name):
    rows, cols = w.shape
    tr = _elementwise_rows(rows, 256)

    def body(w_ref, g_ref, m_ref, v_ref, g_out_ref, d_ref, nm_ref, nv_ref):
        d, nm, nv = _adamw_math(w_ref[...], g_ref[...], m_ref[...], v_ref[...])
        g_out_ref[...] = g_ref[...]
        d_ref[...] = d
        nm_ref[...] = nm
        nv_ref[...] = nv

    spec = pl.BlockSpec((tr, cols), lambda i: (i, 0))
    return pl.pallas_call(
        body, name=name, grid=(rows // tr,), in_specs=[spec] * 4, out_specs=[spec] * 4,
        out_shape=[jax.ShapeDtypeStruct((rows, cols), F32)] * 4,
        compiler_params=_params(("arbitrary",)),
    )(w, g, m, v)


def reduce_small(packs, w, m, v):
    n = len(packs)
    n_dev = 8
    flips = [(fx, fy, fc) for fx in (0, 1) for fy in (0, 1) for fc in (0, 1)][1:]

    def body(*refs):
        pack_refs = refs[:n]
        w_ref, m_ref, v_ref, g_out, d_out, m_out, v_out, mine, slots, send_sems, recv_sems = refs[n:]
        x, y, c, _ = _place()
        me = 4 * x + 2 * y + c
        acc = pack_refs[0][...]
        for ref in pack_refs[1:]:
            acc = acc + ref[...]
        mine[...] = acc
        sends = []
        for k, (fx, fy, fc) in enumerate(flips):
            peer = (x ^ fx, y ^ fy, c ^ fc)
            sends.append(_rcopy(mine, slots.at[me], send_sems.at[k], recv_sems.at[me], peer))
        for cp in sends:
            cp.start()
        slots[me] = acc
        for fx, fy, fc in flips:
            src = 4 * (x ^ fx) + 2 * (y ^ fy) + (c ^ fc)
            _rcopy(mine, slots.at[src], send_sems.at[0], recv_sems.at[src], (x, y, c)).wait_recv()
        for cp in sends:
            cp.wait_send()
        total = slots[0]
        for d in range(1, n_dev):
            total = total + slots[d]
        g_out[...] = total
        d, nm, nv = _adamw_math(w_ref[...], total, m_ref[...], v_ref[...])
        d_out[...] = d
        m_out[...] = nm
        v_out[...] = nv

    vm = pl.BlockSpec(memory_space=pltpu.VMEM)
    return pl.pallas_call(
        body, name="reduce_small",
        in_specs=[vm] * (n + 3), out_specs=[vm] * 4,
        out_shape=[jax.ShapeDtypeStruct((8, D_MODEL), F32)] * 4,
        scratch_shapes=[pltpu.VMEM((8, D_MODEL), F32), pltpu.VMEM((n_dev, 8, D_MODEL), F32),
                        pltpu.SemaphoreType.DMA((len(flips),)), pltpu.SemaphoreType.DMA((n_dev,))],
    )(*packs, w, m, v)


def _column_pieces(g):
    return g.reshape(g.shape[0], N_CHIPS, g.shape[1] // N_CHIPS).transpose(1, 0, 2)


def _pack_small(attn_pre, gamma, hg_norm, sb_norm, attn_post, ffn_pre, ffn_post):
    rows = [attn_pre, gamma.reshape(1, D_MODEL), jnp.concatenate([hg_norm, sb_norm], axis=1), attn_post, ffn_pre, ffn_post,
            jnp.zeros((2, D_MODEL), F32)]
    return jnp.concatenate(rows, axis=0)


def _unpack_small(pack):
    return (pack[ROW_ATTN_PRE:ROW_ATTN_PRE + 1], pack[ROW_GAMMA].reshape(2, HG_WIDTH),
            pack[ROW_MIX_NORMS:ROW_MIX_NORMS + 1, :HG_WIDTH], pack[ROW_MIX_NORMS:ROW_MIX_NORMS + 1, HG_WIDTH:],
            pack[ROW_ATTN_POST:ROW_ATTN_POST + 1], pack[ROW_FFN_PRE:ROW_FFN_PRE + 1], pack[ROW_FFN_POST:ROW_FFN_POST + 1])


def kernel(x, p, attn_pre_norm, w_in, hg_lower_gamma, hg_out_norm, sb_out_norm, w_out, attn_post_norm, ffn_pre_norm, w_gate_up, w_down, ffn_post_norm, ple_proj, ple_gate, loss_target, m_attn_pre_norm, m_w_in, m_hg_lower_gamma, m_hg_out_norm, m_sb_out_norm, m_w_out, m_attn_post_norm, m_ffn_pre_norm, m_w_gate_up, m_w_down, m_ffn_post_norm, m_ple_proj, m_ple_gate, v_attn_pre_norm, v_w_in, v_hg_lower_gamma, v_hg_out_norm, v_sb_out_norm, v_w_out, v_attn_post_norm, v_ffn_pre_norm, v_w_gate_up, v_w_down, v_ffn_post_norm, v_ple_proj, v_ple_gate):
    x2 = x[0]
    p2 = p[0, 0]
    target = loss_target[0]
    big = dict(w_in=(w_in, m_w_in, v_w_in), w_out=(w_out, m_w_out, v_w_out), w_gate_up=(w_gate_up, m_w_gate_up, v_w_gate_up),
               w_down=(w_down, m_w_down, v_w_down), ple_proj=(ple_proj, m_ple_proj, v_ple_proj),
               ple_gate=(ple_gate, m_ple_gate, v_ple_gate))
    names = list(big)
    big = {k: tuple(a[0] for a in t) for k, t in big.items()}

    shard16 = {k: big[k][0].astype(BF16) for k in names}
    w_in_full, = gather_weights([shard16["w_in"]])
    mix_norms = jnp.concatenate([hg_out_norm, sb_out_norm], axis=1)
    small_ones = ["w_out", "ple_proj", "ple_gate"]

    proj_h, sqkv, u1, *landed_small, landed_gu = in_proj_fwd(
        x2, attn_pre_norm, w_in_full,
        comm=_both(gather_over_ici([shard16[k] for k in small_ones]), gather_over_ici([shard16["w_gate_up"]], which=(2,))))
    o_sb, sb_totals, sb_first, landed_gu = sb_fwd(
        sqkv, comm=gather_over_ici([shard16["w_gate_up"]], which=(0, 1), landing=[landed_gu]))
    o_hg, states, landed_down, *full_small = hgrn2_fwd(
        proj_h, hg_lower_gamma, comm=_both(gather_over_ici([shard16["w_down"]]), gather_over_d2d(landed_small)))
    full = dict(zip(small_ones, full_small), w_in=w_in_full)
    w_out_full = full["w_out"].reshape(D_MODEL, D_MODEL)
    w_pg_full = full["ple_gate"].reshape(D_MODEL, D_MODEL)
    cat, mix, h1, full["w_gate_up"], full["w_down"] = mix_out_fwd(
        o_hg, proj_h, o_sb, x2, mix_norms, attn_post_norm, w_out_full, comm=gather_over_d2d([landed_gu, landed_down]))
    w_down_full = full["w_down"].reshape(D_FF, D_MODEL)
    u2, gu, act, y, h2 = ffn_fwd(h1, ffn_pre_norm, ffn_post_norm, full["w_gate_up"], w_down_full)

    core = lax.axis_index("c").astype(jnp.int32).reshape(1)
    de, ds, dh2, h2b, pb, pack_loss = ple_loss(h2, p2, target, full["ple_proj"], w_pg_full)
    dy, dgu, dh1, pack_ffn = ffn_bwd(dh2, y, h1, gu, ffn_pre_norm, ffn_post_norm, full["w_gate_up"], w_down_full)
    local = {}
    local["w_gate_up"], = weight_grad(u2, dgu, "grad_w_gate_up", tm=D_MODEL, tn=full["w_gate_up"].shape[2], tk=2048,
                                      col_pieces=True)
    grad_down, got_gu = weight_grad(act, dy, "grad_w_down", tm=D_FF // 2, tn=D_MODEL, tk=2048,
                                    comm=core_halves([local["w_gate_up"]]))
    local["w_down"] = grad_down.reshape(full["w_down"].shape)
    local["ple_proj"] = _column_pieces(weight_grad(pb, de, "grad_ple_proj", tm=pb.shape[1], tn=D_MODEL, tk=1024)[0])
    grad_pg, got_down = weight_grad(h2b, ds, "grad_ple_gate", tm=D_MODEL, tn=D_MODEL, tk=2048,
                                    comm=core_halves([local["w_down"]]))
    local["ple_gate"] = grad_pg.reshape(full["ple_gate"].shape)
    early = list(local)
    dmix, do_hg, dhg, do_sb, pack_mix, *got_ple = mix_out_bwd(
        dh1, mix, o_hg, proj_h, o_sb, mix_norms, attn_post_norm, w_out_full,
        comm=core_halves([local["ple_proj"], local["ple_gate"]]))
    got = [got_gu, got_down] + got_ple
    partial = [add_core_halves(local[k], g, core, "add_core_halves_" + k) for k, g in zip(early, got)]
    local["w_out"] = weight_grad(cat, dmix, "grad_w_out", tm=D_MODEL, tn=D_MODEL, tk=2048)[0].reshape(full["w_out"].shape)
    dsq, dsk, dsv = sb_bwd(sqkv, do_sb, sb_totals, sb_first)
    dhq, dhf, dhi, pack_hg, *by_source, got_out = hgrn2_bwd(
        proj_h, hg_lower_gamma, states, do_hg, comm=_both(chip_partials(partial[:2]), core_halves([local["w_out"]])))
    early.append("w_out")
    partial.append(add_core_halves(local["w_out"], got_out, core, "add_core_halves_w_out"))
    dproj, grad_x, pack_in = in_proj_bwd([dhq, dhf, dhi, dhg, dsq, dsk, dsv], x2, dh1, attn_pre_norm, full["w_in"])

    late = ["w_in"]
    local["w_in"], *more = weight_grad(u1, dproj, "grad_w_in", tm=D_MODEL, tn=full["w_in"].shape[2], tk=2048,
                                       col_pieces=True, comm=chip_partials(partial[2:]))
    halves = {k: add_chip_partials(s, core, "add_chip_partials_" + k) for k, s in zip(early, by_source + more)}
    got = _run_comm(core_halves([local[k] for k in late]), "exchange_core_halves")
    partial = [add_core_halves(local[k], g, core, "add_core_halves_" + k) for k, g in zip(late, got)]
    by_source = _run_comm(chip_partials(partial), "exchange_chip_partials")
    halves.update({k: add_chip_partials(s, core, "add_chip_partials_" + k) for k, s in zip(late, by_source)})
    grads = dict(zip(names, join_core_halves([halves[k] for k in names])))

    upd = {k: adamw(big[k][0], grads[k], big[k][1], big[k][2], "adamw_" + k) for k in names}

    small = reduce_small(
        [pack_loss, pack_ffn, pack_mix, pack_hg, pack_in],
        _pack_small(attn_pre_norm, hg_lower_gamma, hg_out_norm, sb_out_norm, attn_post_norm, ffn_pre_norm, ffn_post_norm),
        _pack_small(m_attn_pre_norm, m_hg_lower_gamma, m_hg_out_norm, m_sb_out_norm, m_attn_post_norm, m_ffn_pre_norm, m_ffn_post_norm),
        _pack_small(v_attn_pre_norm, v_hg_lower_gamma, v_hg_out_norm, v_sb_out_norm, v_attn_post_norm, v_ffn_pre_norm, v_ffn_post_norm),
    )
    loss = small[0][ROW_LOSS, 0]
    s_grad, s_delta, s_m, s_v = (_unpack_small(t) for t in small)

    def ordered(small_vals, big_vals):
        a_pre, gam, hg_n, sb_n, a_post, f_pre, f_post = small_vals
        b = {k: big_vals[k][None] for k in names}
        return (a_pre, b["w_in"], gam, hg_n, sb_n, b["w_out"], a_post, f_pre, b["w_gate_up"], b["w_down"], f_post,
                b["ple_proj"], b["ple_gate"])

    return (loss, grad_x[None],
            *ordered(s_grad, {k: upd[k][0] for k in names}),
            *ordered(s_delta, {k: upd[k][1] for k in names}),
            *ordered(s_m, {k: upd[k][2] for k in names}),
            *ordered(s_v, {k: upd[k][3] for k in names}))
```
